```python
import math
import jax, jax.numpy as jnp
from jax import lax
import numpy as np

D_MODEL = 1024
BATCH = 8
SEQ = 4096
DEPTH = 4

MIX_WIDTH = D_MODEL
LRU_WIDTH = D_MODEL // 2
LRU_BLOCKS = 8
LRU_BLOCK = LRU_WIDTH // LRU_BLOCKS
LRU_C = 8.0
GDN_HEAD_DIM = 128
GDN_HEADS = (MIX_WIDTH - LRU_WIDTH) // GDN_HEAD_DIM
GDN_WIDTH = GDN_HEADS * GDN_HEAD_DIM
GDN_CHUNK = 64
CONV_WIDTH = 4
D_FF = 4 * D_MODEL
N_MOD = 6
IN_COLS = 2 * LRU_WIDTH + 4 * GDN_WIDTH + 2 * GDN_HEADS
NORM_EPS = 1e-6

kernel_name = "hymba_rglru_gdn_hybrid"


def rms_norm(x, w):
    xf = x.astype(jnp.float32)
    y = xf * lax.rsqrt(jnp.mean(xf * xf, axis=-1, keepdims=True) + NORM_EPS)
    return (y * w.astype(jnp.float32)).astype(x.dtype)


def causal_depthwise_conv(x, w):
    K = w.shape[0]
    S = x.shape[1]
    xp = jnp.pad(x, ((0, 0), (K - 1, 0), (0, 0)))
    y = xp[:, 0:S] * w[0]
    for k in range(1, K):
        y = y + xp[:, k:k + S] * w[k]
    return y


def rg_lru(x, r_pre, i_pre, lam):
    dt = x.dtype
    xf = x.astype(jnp.float32)
    r = jax.nn.sigmoid(r_pre.astype(jnp.float32))
    i = jax.nn.sigmoid(i_pre.astype(jnp.float32))
    log_a = LRU_C * r * jax.nn.log_sigmoid(lam.astype(jnp.float32))
    a = jnp.exp(log_a)
    mult = jnp.sqrt(jnp.maximum(-jnp.expm1(2.0 * log_a), 1e-12))
    b = mult * (i * xf)

    def combine(left, right):
        a1, b1 = left
        a2, b2 = right
        return a1 * a2, a2 * b1 + b2

    _, h = lax.associative_scan(combine, (a, b), axis=1)
    return h.astype(dt)


def l2_normalize(t):
    return t * lax.rsqrt(jnp.sum(t * t, axis=-1, keepdims=True) + 1e-6)


def gated_delta_rule_chunked(q, k, v, g, beta):
    dt = v.dtype
    B, S, H, Dk = q.shape
    Dv = v.shape[-1]
    C = GDN_CHUNK
    N = S // C
    q = l2_normalize(q.astype(jnp.float32)) * (Dk ** -0.5)
    k = l2_normalize(k.astype(jnp.float32))
    v = v.astype(jnp.float32)

    def chunks(t):
        return t.reshape(B, N, C, H, -1).transpose(0, 3, 1, 2, 4)

    q, k, v = chunks(q), chunks(k), chunks(v)
    g = g.astype(jnp.float32).reshape(B, N, C, H).transpose(0, 3, 1, 2)
    beta = beta.astype(jnp.float32).reshape(B, N, C, H).transpose(0, 3, 1, 2)
    g = jnp.cumsum(g, axis=-1)

    causal = jnp.tril(jnp.ones((C, C), dtype=bool))
    strict = jnp.tril(jnp.ones((C, C), dtype=bool), k=-1)
    decay = jnp.exp(jnp.where(causal, g[..., :, None] - g[..., None, :], -jnp.inf))

    k_beta = k * beta[..., None]
    v_beta = v * beta[..., None]
    Lmat = jnp.where(strict, jnp.einsum('bhnid,bhnjd->bhnij', k_beta, k) * decay, 0.0)
    tmat = Lmat + jnp.eye(C, dtype=jnp.float32)
    u = lax.linalg.triangular_solve(tmat, v_beta, left_side=True, lower=True, unit_diagonal=True)
    w = lax.linalg.triangular_solve(tmat, k_beta * jnp.exp(g)[..., None],
                                    left_side=True, lower=True, unit_diagonal=True)
    attn = jnp.where(causal, jnp.einsum('bhnid,bhnjd->bhnij', q, k) * decay, 0.0)
    q_dec = q * jnp.exp(g)[..., None]
    k_tail = k * jnp.exp(g[..., -1:] - g)[..., None]
    g_last = jnp.exp(g[..., -1])

    def to_front(t):
        return jnp.moveaxis(t, 2, 0)

    xs = (to_front(u), to_front(w), to_front(attn), to_front(q_dec), to_front(k_tail),
          jnp.moveaxis(g_last, 2, 0))

    def step(state, inp):
        u_n, w_n, attn_n, qd_n, kt_n, gl_n = inp
        v_new = u_n - jnp.einsum('bhck,bhkv->bhcv', w_n, state)
        o = jnp.einsum('bhck,bhkv->bhcv', qd_n, state) + jnp.einsum('bhij,bhjv->bhiv', attn_n, v_new)
        state = state * gl_n[..., None, None] + jnp.einsum('bhck,bhcv->bhkv', kt_n, v_new)
        return state, o

    state0 = jnp.zeros((B, H, Dk, Dv), jnp.float32)
    _, o = lax.scan(step, state0, xs)
    o = o.transpose(1, 0, 3, 2, 4).reshape(B, S, H, Dv)
    return o.astype(dt)


def _fwd_setup_inputs(seed: int = 0) -> dict:
    key = jax.random.key(seed)
    ks = jax.random.split(key, 24)
    f32 = jnp.float32
    L, D = DEPTH, D_MODEL

    def nrm(k, shape, std):
        return jax.random.normal(k, shape, f32) * std

    x = nrm(ks[0], (BATCH, SEQ, D), 1.0)
    c = nrm(ks[1], (BATCH, D), 1.0)
    norm_mix_w = 1.0 + nrm(ks[2], (L, D), 0.02)
    norm_mlp_w = 1.0 + nrm(ks[3], (L, D), 0.02)
    w_mod = nrm(ks[4], (L, D, N_MOD * D), 0.005)
    gate_offset = jnp.array([0.0, 0.0, 1.0, 0.0, 0.0, 1.0], f32)[None, :, None]
    b_mod = (nrm(ks[5], (L, N_MOD, D), 0.02) + gate_offset).reshape(L, N_MOD * D)
    w_in = nrm(ks[6], (L, D, IN_COLS), D ** -0.5)
    lru_conv_w = nrm(ks[7], (L, CONV_WIDTH, LRU_WIDTH), CONV_WIDTH ** -0.5)
    lru_conv_b = nrm(ks[8], (L, LRU_WIDTH), 0.01)
    lru_gate_a_w = nrm(ks[9], (L, LRU_BLOCKS, LRU_BLOCK, LRU_BLOCK), LRU_BLOCK ** -0.5)
    lru_gate_a_b = nrm(ks[10], (L, LRU_WIDTH), 0.01)
    lru_gate_x_w = nrm(ks[11], (L, LRU_BLOCKS, LRU_BLOCK, LRU_BLOCK), LRU_BLOCK ** -0.5)
    lru_gate_x_b = nrm(ks[12], (L, LRU_WIDTH), 0.01)
    u = jax.random.uniform(ks[13], (L, LRU_WIDTH), f32, 0.9, 0.999)
    p = u ** (1.0 / LRU_C)
    lru_lambda = jnp.log(p) - jnp.log1p(-p)
    lru_norm_w = 1.0 + nrm(ks[14], (L, LRU_WIDTH), 0.02)
    gdn_conv_w = nrm(ks[15], (L, CONV_WIDTH, 3 * GDN_WIDTH), CONV_WIDTH ** -0.5)
    gdn_a_log = jnp.log(jax.random.uniform(ks[16], (L, GDN_HEADS), f32, 1.0, 16.0))
    dt0 = jnp.exp(jax.random.uniform(ks[17], (L, GDN_HEADS), f32, math.log(1e-3), math.log(1e-1)))
    gdn_dt_bias = dt0 + jnp.log(-jnp.expm1(-dt0))
    gdn_norm_w = 1.0 + nrm(ks[18], (L, GDN_HEAD_DIM), 0.02)
    w_out = nrm(ks[19], (L, MIX_WIDTH, D), MIX_WIDTH ** -0.5)
    w_up = nrm(ks[20], (L, D, D_FF), D ** -0.5)
    w_down = nrm(ks[21], (L, D_FF, D), D_FF ** -0.5)
    final_norm_w = 1.0 + nrm(ks[22], (D,), 0.02)
    return {
        "x": x, "c": c,
        "norm_mix_w": norm_mix_w, "norm_mlp_w": norm_mlp_w,
        "w_mod": w_mod, "b_mod": b_mod,
        "w_in": w_in,
        "lru_conv_w": lru_conv_w, "lru_conv_b": lru_conv_b,
        "lru_gate_a_w": lru_gate_a_w, "lru_gate_a_b": lru_gate_a_b,
        "lru_gate_x_w": lru_gate_x_w, "lru_gate_x_b": lru_gate_x_b,
        "lru_lambda": lru_lambda, "lru_norm_w": lru_norm_w,
        "gdn_conv_w": gdn_conv_w, "gdn_a_log": gdn_a_log, "gdn_dt_bias": gdn_dt_bias,
        "gdn_norm_w": gdn_norm_w,
        "w_out": w_out, "w_up": w_up, "w_down": w_down,
        "final_norm_w": final_norm_w,
    }


def _fwd_reference(x, c, norm_mix_w, norm_mlp_w, w_mod, b_mod, w_in,
              lru_conv_w, lru_conv_b, lru_gate_a_w, lru_gate_a_b,
              lru_gate_x_w, lru_gate_x_b, lru_lambda, lru_norm_w,
              gdn_conv_w, gdn_a_log, gdn_dt_bias, gdn_norm_w,
              w_out, w_up, w_down, final_norm_w):
    B, S, D = x.shape
    o_lx = 0
    o_ly = o_lx + LRU_WIDTH
    o_q = o_ly + LRU_WIDTH
    o_v_end = o_q + 3 * GDN_WIDTH
    o_z = o_v_end
    o_beta = o_z + GDN_WIDTH
    o_alpha = o_beta + GDN_HEADS
    c_act = jax.nn.silu(c)

    for l in range(DEPTH):
        mod = c_act @ w_mod[l] + b_mod[l]
        sh1, sc1, g1, sh2, sc2, g2 = jnp.split(mod[:, None, :], N_MOD, axis=-1)

        h = rms_norm(x, norm_mix_w[l]) * (1.0 + sc1) + sh1
        proj = h @ w_in[l]

        x_lru = proj[..., o_lx:o_ly]
        y_lru = proj[..., o_ly:o_q]
        xr = causal_depthwise_conv(x_lru, lru_conv_w[l]) + lru_conv_b[l]
        xb = xr.reshape(B, S, LRU_BLOCKS, LRU_BLOCK)
        r_pre = jnp.einsum('bsgi,gij->bsgj', xb, lru_gate_a_w[l]).reshape(B, S, LRU_WIDTH) + lru_gate_a_b[l]
        i_pre = jnp.einsum('bsgi,gij->bsgj', xb, lru_gate_x_w[l]).reshape(B, S, LRU_WIDTH) + lru_gate_x_b[l]
        h_lru = rg_lru(xr, r_pre, i_pre, lru_lambda[l])
        out_lru = rms_norm(h_lru * jax.nn.gelu(y_lru), lru_norm_w[l])

        qkv = jax.nn.silu(causal_depthwise_conv(proj[..., o_q:o_v_end], gdn_conv_w[l]))
        q, k, v = jnp.split(qkv.reshape(B, S, 3, GDN_HEADS, GDN_HEAD_DIM), 3, axis=2)
        q, k, v = q[:, :, 0], k[:, :, 0], v[:, :, 0]
        z = proj[..., o_z:o_beta].reshape(B, S, GDN_HEADS, GDN_HEAD_DIM)
        beta = jax.nn.sigmoid(proj[..., o_beta:o_alpha].astype(jnp.float32))
        g = -jnp.exp(gdn_a_log[l].astype(jnp.float32)) * jax.nn.softplus(
            proj[..., o_alpha:o_alpha + GDN_HEADS].astype(jnp.float32) + gdn_dt_bias[l].astype(jnp.float32))
        o = gated_delta_rule_chunked(q, k, v, g, beta)
        out_gdn = (rms_norm(o, gdn_norm_w[l]) * jax.nn.silu(z)).reshape(B, S, GDN_WIDTH)

        mix = jnp.concatenate([out_lru, out_gdn], axis=-1) @ w_out[l]
        x = x + g1 * mix

        h = rms_norm(x, norm_mlp_w[l]) * (1.0 + sc2) + sh2
        x = x + g2 * (jnp.square(jax.nn.relu(h @ w_up[l])) @ w_down[l])

    return rms_norm(x, final_norm_w)


import jax as _jax
import jax.numpy as _jnp

TWIN_FORMAT = 'train_step'
FWD_PARAMS = ['x', 'c', 'norm_mix_w', 'norm_mlp_w', 'w_mod', 'b_mod', 'w_in', 'lru_conv_w', 'lru_conv_b', 'lru_gate_a_w', 'lru_gate_a_b', 'lru_gate_x_w', 'lru_gate_x_b', 'lru_lambda', 'lru_norm_w', 'gdn_conv_w', 'gdn_a_log', 'gdn_dt_bias', 'gdn_norm_w', 'w_out', 'w_up', 'w_down', 'final_norm_w']
TWIN_WEIGHTS = ['norm_mix_w', 'norm_mlp_w', 'w_mod', 'b_mod', 'w_in', 'lru_conv_w', 'lru_conv_b', 'lru_gate_a_w', 'lru_gate_a_b', 'lru_gate_x_w', 'lru_gate_x_b', 'lru_lambda', 'lru_norm_w', 'gdn_conv_w', 'gdn_a_log', 'gdn_dt_bias', 'gdn_norm_w', 'w_out', 'w_up', 'w_down', 'final_norm_w']
TWIN_DIFF_INPUT = 'x'
TWIN_INPUTS = ['x', 'c', 'norm_mix_w', 'norm_mlp_w', 'w_mod', 'b_mod', 'w_in', 'lru_conv_w', 'lru_conv_b', 'lru_gate_a_w', 'lru_gate_a_b', 'lru_gate_x_w', 'lru_gate_x_b', 'lru_lambda', 'lru_norm_w', 'gdn_conv_w', 'gdn_a_log', 'gdn_dt_bias', 'gdn_norm_w', 'w_out', 'w_up', 'w_down', 'final_norm_w', 'loss_target', 'm_norm_mix_w', 'm_norm_mlp_w', 'm_w_mod', 'm_b_mod', 'm_w_in', 'm_lru_conv_w', 'm_lru_conv_b', 'm_lru_gate_a_w', 'm_lru_gate_a_b', 'm_lru_gate_x_w', 'm_lru_gate_x_b', 'm_lru_lambda', 'm_lru_norm_w', 'm_gdn_conv_w', 'm_gdn_a_log', 'm_gdn_dt_bias', 'm_gdn_norm_w', 'm_w_out', 'm_w_up', 'm_w_down', 'm_final_norm_w', 'v_norm_mix_w', 'v_norm_mlp_w', 'v_w_mod', 'v_b_mod', 'v_w_in', 'v_lru_conv_w', 'v_lru_conv_b', 'v_lru_gate_a_w', 'v_lru_gate_a_b', 'v_lru_gate_x_w', 'v_lru_gate_x_b', 'v_lru_lambda', 'v_lru_norm_w', 'v_gdn_conv_w', 'v_gdn_a_log', 'v_gdn_dt_bias', 'v_gdn_norm_w', 'v_w_out', 'v_w_up', 'v_w_down', 'v_final_norm_w']
TWIN_OUTPUTS = ['loss', 'grad_x', 'grad_norm_mix_w', 'grad_norm_mlp_w', 'grad_w_mod', 'grad_b_mod', 'grad_w_in', 'grad_lru_conv_w', 'grad_lru_conv_b', 'grad_lru_gate_a_w', 'grad_lru_gate_a_b', 'grad_lru_gate_x_w', 'grad_lru_gate_x_b', 'grad_lru_lambda', 'grad_lru_norm_w', 'grad_gdn_conv_w', 'grad_gdn_a_log', 'grad_gdn_dt_bias', 'grad_gdn_norm_w', 'grad_w_out', 'grad_w_up', 'grad_w_down', 'grad_final_norm_w', 'delta_norm_mix_w', 'delta_norm_mlp_w', 'delta_w_mod', 'delta_b_mod', 'delta_w_in', 'delta_lru_conv_w', 'delta_lru_conv_b', 'delta_lru_gate_a_w', 'delta_lru_gate_a_b', 'delta_lru_gate_x_w', 'delta_lru_gate_x_b', 'delta_lru_lambda', 'delta_lru_norm_w', 'delta_gdn_conv_w', 'delta_gdn_a_log', 'delta_gdn_dt_bias', 'delta_gdn_norm_w', 'delta_w_out', 'delta_w_up', 'delta_w_down', 'delta_final_norm_w', 'new_m_norm_mix_w', 'new_m_norm_mlp_w', 'new_m_w_mod', 'new_m_b_mod', 'new_m_w_in', 'new_m_lru_conv_w', 'new_m_lru_conv_b', 'new_m_lru_gate_a_w', 'new_m_lru_gate_a_b', 'new_m_lru_gate_x_w', 'new_m_lru_gate_x_b', 'new_m_lru_lambda', 'new_m_lru_norm_w', 'new_m_gdn_conv_w', 'new_m_gdn_a_log', 'new_m_gdn_dt_bias', 'new_m_gdn_norm_w', 'new_m_w_out', 'new_m_w_up', 'new_m_w_down', 'new_m_final_norm_w', 'new_v_norm_mix_w', 'new_v_norm_mlp_w', 'new_v_w_mod', 'new_v_b_mod', 'new_v_w_in', 'new_v_lru_conv_w', 'new_v_lru_conv_b', 'new_v_lru_gate_a_w', 'new_v_lru_gate_a_b', 'new_v_lru_gate_x_w', 'new_v_lru_gate_x_b', 'new_v_lru_lambda', 'new_v_lru_norm_w', 'new_v_gdn_conv_w', 'new_v_gdn_a_log', 'new_v_gdn_dt_bias', 'new_v_gdn_norm_w', 'new_v_w_out', 'new_v_w_up', 'new_v_w_down', 'new_v_final_norm_w']
TWIN_LEAF_KINDS = {'loss': 'loss', 'grad_x': 'grad_x', 'grad_norm_mix_w': 'grad_w', 'grad_norm_mlp_w': 'grad_w', 'grad_w_mod': 'grad_w', 'grad_b_mod': 'grad_w', 'grad_w_in': 'grad_w', 'grad_lru_conv_w': 'grad_w', 'grad_lru_conv_b': 'grad_w', 'grad_lru_gate_a_w': 'grad_w', 'grad_lru_gate_a_b': 'grad_w', 'grad_lru_gate_x_w': 'grad_w', 'grad_lru_gate_x_b': 'grad_w', 'grad_lru_lambda': 'grad_w', 'grad_lru_norm_w': 'grad_w', 'grad_gdn_conv_w': 'grad_w', 'grad_gdn_a_log': 'grad_w', 'grad_gdn_dt_bias': 'grad_w', 'grad_gdn_norm_w': 'grad_w', 'grad_w_out': 'grad_w', 'grad_w_up': 'grad_w', 'grad_w_down': 'grad_w', 'grad_final_norm_w': 'grad_w', 'delta_norm_mix_w': 'delta_w', 'delta_norm_mlp_w': 'delta_w', 'delta_w_mod': 'delta_w', 'delta_b_mod': 'delta_w', 'delta_w_in': 'delta_w', 'delta_lru_conv_w': 'delta_w', 'delta_lru_conv_b': 'delta_w', 'delta_lru_gate_a_w': 'delta_w', 'delta_lru_gate_a_b': 'delta_w', 'delta_lru_gate_x_w': 'delta_w', 'delta_lru_gate_x_b': 'delta_w', 'delta_lru_lambda': 'delta_w', 'delta_lru_norm_w': 'delta_w', 'delta_gdn_conv_w': 'delta_w', 'delta_gdn_a_log': 'delta_w', 'delta_gdn_dt_bias': 'delta_w', 'delta_gdn_norm_w': 'delta_w', 'delta_w_out': 'delta_w', 'delta_w_up': 'delta_w', 'delta_w_down': 'delta_w', 'delta_final_norm_w': 'delta_w', 'new_m_norm_mix_w': 'new_m', 'new_m_norm_mlp_w': 'new_m', 'new_m_w_mod': 'new_m', 'new_m_b_mod': 'new_m', 'new_m_w_in': 'new_m', 'new_m_lru_conv_w': 'new_m', 'new_m_lru_conv_b': 'new_m', 'new_m_lru_gate_a_w': 'new_m', 'new_m_lru_gate_a_b': 'new_m', 'new_m_lru_gate_x_w': 'new_m', 'new_m_lru_gate_x_b': 'new_m', 'new_m_lru_lambda': 'new_m', 'new_m_lru_norm_w': 'new_m', 'new_m_gdn_conv_w': 'new_m', 'new_m_gdn_a_log': 'new_m', 'new_m_gdn_dt_bias': 'new_m', 'new_m_gdn_norm_w': 'new_m', 'new_m_w_out': 'new_m', 'new_m_w_up': 'new_m', 'new_m_w_down': 'new_m', 'new_m_final_norm_w': 'new_m', 'new_v_norm_mix_w': 'new_v', 'new_v_norm_mlp_w': 'new_v', 'new_v_w_mod': 'new_v', 'new_v_b_mod': 'new_v', 'new_v_w_in': 'new_v', 'new_v_lru_conv_w': 'new_v', 'new_v_lru_conv_b': 'new_v', 'new_v_lru_gate_a_w': 'new_v', 'new_v_lru_gate_a_b': 'new_v', 'new_v_lru_gate_x_w': 'new_v', 'new_v_lru_gate_x_b': 'new_v', 'new_v_lru_lambda': 'new_v', 'new_v_lru_norm_w': 'new_v', 'new_v_gdn_conv_w': 'new_v', 'new_v_gdn_a_log': 'new_v', 'new_v_gdn_dt_bias': 'new_v', 'new_v_gdn_norm_w': 'new_v', 'new_v_w_out': 'new_v', 'new_v_w_up': 'new_v', 'new_v_w_down': 'new_v', 'new_v_final_norm_w': 'new_v'}


def _forward(args):
    return _fwd_reference(*[args[k] for k in FWD_PARAMS])


def _output_shape():
    def fwd():
        inp = _fwd_setup_inputs(0)
        return _fwd_reference(*[inp[k] for k in FWD_PARAMS])
    out = _jax.eval_shape(fwd)
    return out.shape, out.dtype

N_MICROBATCH = 1
ADAM_LR = 0.001
ADAM_B1 = 0.9
ADAM_B2 = 0.999
ADAM_EPS = 1e-08
ADAM_WD = 0.01
ADAM_STEP = 10
PER_EXAMPLE_BATCH_AXIS = {'x': 0, 'c': 0, 'loss_target': 0}
SHARED_INPUTS = []
_WEIGHT_DTYPES = {'norm_mix_w': _jnp.float32, 'norm_mlp_w': _jnp.float32, 'w_mod': _jnp.float32, 'b_mod': _jnp.float32, 'w_in': _jnp.float32, 'lru_conv_w': _jnp.float32, 'lru_conv_b': _jnp.float32, 'lru_gate_a_w': _jnp.float32, 'lru_gate_a_b': _jnp.float32, 'lru_gate_x_w': _jnp.float32, 'lru_gate_x_b': _jnp.float32, 'lru_lambda': _jnp.float32, 'lru_norm_w': _jnp.float32, 'gdn_conv_w': _jnp.float32, 'gdn_a_log': _jnp.float32, 'gdn_dt_bias': _jnp.float32, 'gdn_norm_w': _jnp.float32, 'w_out': _jnp.float32, 'w_up': _jnp.float32, 'w_down': _jnp.float32, 'final_norm_w': _jnp.float32}
MOMENT_SCALE = {'norm_mix_w': 1.448742e-01, 'norm_mlp_w': 1.349443e-01, 'w_mod': 1.405557e-01, 'b_mod': 2.505699e-01, 'w_in': 8.603589e-02, 'lru_conv_w': 1.287169e-01, 'lru_conv_b': 6.611976e-01, 'lru_gate_a_w': 2.955436e-02, 'lru_gate_a_b': 2.512109e-02, 'lru_gate_x_w': 5.621213e-02, 'lru_gate_x_b': 4.791305e-02, 'lru_lambda': 5.719721e-02, 'lru_norm_w': 1.220497e-01, 'gdn_conv_w': 5.470529e-02, 'gdn_a_log': 4.289691e-01, 'gdn_dt_bias': 4.125608e-01, 'gdn_norm_w': 1.460177e-01, 'w_out': 1.108240e-01, 'w_up': 6.621642e-02, 'w_down': 1.459041e-01, 'final_norm_w': 3.294479e+01}


def _to_microbatches(a, axis):
    t = _jnp.moveaxis(a, axis, 0)
    t = t.reshape((N_MICROBATCH, t.shape[0] // N_MICROBATCH) + t.shape[1:])
    return _jnp.moveaxis(t, 1, axis + 1)


def setup_inputs(seed: int = 0) -> dict:
    inp = _fwd_setup_inputs(seed)
    key = _jax.random.fold_in(_jax.random.key(seed), 7919)
    shape, _ = _output_shape()
    out = dict(inp)
    out["loss_target"] = _jax.random.normal(_jax.random.fold_in(key, 0), shape, _jnp.float32)
    for i, name in enumerate(TWIN_WEIGHTS):
        w = inp[name].astype(_jnp.float32)
        if MOMENT_SCALE is None:
            s = _jnp.sqrt(_jnp.mean(_jnp.square(w)) + 1e-30)
        else:
            s = MOMENT_SCALE[name]
        km, kv = _jax.random.split(_jax.random.fold_in(key, i + 1))
        out[name] = w
        out["m_" + name] = s * _jax.random.normal(km, w.shape, _jnp.float32)
        out["v_" + name] = (s * s) * _jax.random.uniform(kv, w.shape, _jnp.float32, 0.5, 1.5)
    if N_MICROBATCH > 1:
        for name, axis in PER_EXAMPLE_BATCH_AXIS.items():
            out[name] = _to_microbatches(out[name], axis)
    return {'x': out['x'], 'c': out['c'], 'norm_mix_w': out['norm_mix_w'], 'norm_mlp_w': out['norm_mlp_w'], 'w_mod': out['w_mod'], 'b_mod': out['b_mod'], 'w_in': out['w_in'], 'lru_conv_w': out['lru_conv_w'], 'lru_conv_b': out['lru_conv_b'], 'lru_gate_a_w': out['lru_gate_a_w'], 'lru_gate_a_b': out['lru_gate_a_b'], 'lru_gate_x_w': out['lru_gate_x_w'], 'lru_gate_x_b': out['lru_gate_x_b'], 'lru_lambda': out['lru_lambda'], 'lru_norm_w': out['lru_norm_w'], 'gdn_conv_w': out['gdn_conv_w'], 'gdn_a_log': out['gdn_a_log'], 'gdn_dt_bias': out['gdn_dt_bias'], 'gdn_norm_w': out['gdn_norm_w'], 'w_out': out['w_out'], 'w_up': out['w_up'], 'w_down': out['w_down'], 'final_norm_w': out['final_norm_w'], 'loss_target': out['loss_target'], 'm_norm_mix_w': out['m_norm_mix_w'], 'm_norm_mlp_w': out['m_norm_mlp_w'], 'm_w_mod': out['m_w_mod'], 'm_b_mod': out['m_b_mod'], 'm_w_in': out['m_w_in'], 'm_lru_conv_w': out['m_lru_conv_w'], 'm_lru_conv_b': out['m_lru_conv_b'], 'm_lru_gate_a_w': out['m_lru_gate_a_w'], 'm_lru_gate_a_b': out['m_lru_gate_a_b'], 'm_lru_gate_x_w': out['m_lru_gate_x_w'], 'm_lru_gate_x_b': out['m_lru_gate_x_b'], 'm_lru_lambda': out['m_lru_lambda'], 'm_lru_norm_w': out['m_lru_norm_w'], 'm_gdn_conv_w': out['m_gdn_conv_w'], 'm_gdn_a_log': out['m_gdn_a_log'], 'm_gdn_dt_bias': out['m_gdn_dt_bias'], 'm_gdn_norm_w': out['m_gdn_norm_w'], 'm_w_out': out['m_w_out'], 'm_w_up': out['m_w_up'], 'm_w_down': out['m_w_down'], 'm_final_norm_w': out['m_final_norm_w'], 'v_norm_mix_w': out['v_norm_mix_w'], 'v_norm_mlp_w': out['v_norm_mlp_w'], 'v_w_mod': out['v_w_mod'], 'v_b_mod': out['v_b_mod'], 'v_w_in': out['v_w_in'], 'v_lru_conv_w': out['v_lru_conv_w'], 'v_lru_conv_b': out['v_lru_conv_b'], 'v_lru_gate_a_w': out['v_lru_gate_a_w'], 'v_lru_gate_a_b': out['v_lru_gate_a_b'], 'v_lru_gate_x_w': out['v_lru_gate_x_w'], 'v_lru_gate_x_b': out['v_lru_gate_x_b'], 'v_lru_lambda': out['v_lru_lambda'], 'v_lru_norm_w': out['v_lru_norm_w'], 'v_gdn_conv_w': out['v_gdn_conv_w'], 'v_gdn_a_log': out['v_gdn_a_log'], 'v_gdn_dt_bias': out['v_gdn_dt_bias'], 'v_gdn_norm_w': out['v_gdn_norm_w'], 'v_w_out': out['v_w_out'], 'v_w_up': out['v_w_up'], 'v_w_down': out['v_w_down'], 'v_final_norm_w': out['v_final_norm_w']}


def _loss(weights, diff, rest, loss_target):
    with _jax.named_scope("forward"):
        args = {**rest, TWIN_DIFF_INPUT: diff, **{k: w.astype(_WEIGHT_DTYPES[k]) for k, w in weights.items()}}
        y = _forward(args)
    with _jax.named_scope("loss_head"):
        err = _jnp.square(y.astype(_jnp.float32) - loss_target)
        return 0.5 * _jnp.sum(_jnp.mean(err, axis=-1)) if err.ndim else 0.5 * err


def _adamw(w, g, m, v):
    m = ADAM_B1 * m + (1.0 - ADAM_B1) * g
    v = ADAM_B2 * v + (1.0 - ADAM_B2) * _jnp.square(g)
    m_hat = m / (1.0 - ADAM_B1 ** ADAM_STEP)
    v_hat = v / (1.0 - ADAM_B2 ** ADAM_STEP)
    delta = -ADAM_LR * (m_hat / (_jnp.sqrt(v_hat) + ADAM_EPS) + ADAM_WD * w)
    return delta, m, v


def reference(x, c, norm_mix_w, norm_mlp_w, w_mod, b_mod, w_in, lru_conv_w, lru_conv_b, lru_gate_a_w, lru_gate_a_b, lru_gate_x_w, lru_gate_x_b, lru_lambda, lru_norm_w, gdn_conv_w, gdn_a_log, gdn_dt_bias, gdn_norm_w, w_out, w_up, w_down, final_norm_w, loss_target, m_norm_mix_w, m_norm_mlp_w, m_w_mod, m_b_mod, m_w_in, m_lru_conv_w, m_lru_conv_b, m_lru_gate_a_w, m_lru_gate_a_b, m_lru_gate_x_w, m_lru_gate_x_b, m_lru_lambda, m_lru_norm_w, m_gdn_conv_w, m_gdn_a_log, m_gdn_dt_bias, m_gdn_norm_w, m_w_out, m_w_up, m_w_down, m_final_norm_w, v_norm_mix_w, v_norm_mlp_w, v_w_mod, v_b_mod, v_w_in, v_lru_conv_w, v_lru_conv_b, v_lru_gate_a_w, v_lru_gate_a_b, v_lru_gate_x_w, v_lru_gate_x_b, v_lru_lambda, v_lru_norm_w, v_gdn_conv_w, v_gdn_a_log, v_gdn_dt_bias, v_gdn_norm_w, v_w_out, v_w_up, v_w_down, v_final_norm_w):
    given = dict(x=x, c=c, norm_mix_w=norm_mix_w, norm_mlp_w=norm_mlp_w, w_mod=w_mod, b_mod=b_mod, w_in=w_in, lru_conv_w=lru_conv_w, lru_conv_b=lru_conv_b, lru_gate_a_w=lru_gate_a_w, lru_gate_a_b=lru_gate_a_b, lru_gate_x_w=lru_gate_x_w, lru_gate_x_b=lru_gate_x_b, lru_lambda=lru_lambda, lru_norm_w=lru_norm_w, gdn_conv_w=gdn_conv_w, gdn_a_log=gdn_a_log, gdn_dt_bias=gdn_dt_bias, gdn_norm_w=gdn_norm_w, w_out=w_out, w_up=w_up, w_down=w_down, final_norm_w=final_norm_w, loss_target=loss_target, m_norm_mix_w=m_norm_mix_w, m_norm_mlp_w=m_norm_mlp_w, m_w_mod=m_w_mod, m_b_mod=m_b_mod, m_w_in=m_w_in, m_lru_conv_w=m_lru_conv_w, m_lru_conv_b=m_lru_conv_b, m_lru_gate_a_w=m_lru_gate_a_w, m_lru_gate_a_b=m_lru_gate_a_b, m_lru_gate_x_w=m_lru_gate_x_w, m_lru_gate_x_b=m_lru_gate_x_b, m_lru_lambda=m_lru_lambda, m_lru_norm_w=m_lru_norm_w, m_gdn_conv_w=m_gdn_conv_w, m_gdn_a_log=m_gdn_a_log, m_gdn_dt_bias=m_gdn_dt_bias, m_gdn_norm_w=m_gdn_norm_w, m_w_out=m_w_out, m_w_up=m_w_up, m_w_down=m_w_down, m_final_norm_w=m_final_norm_w, v_norm_mix_w=v_norm_mix_w, v_norm_mlp_w=v_norm_mlp_w, v_w_mod=v_w_mod, v_b_mod=v_b_mod, v_w_in=v_w_in, v_lru_conv_w=v_lru_conv_w, v_lru_conv_b=v_lru_conv_b, v_lru_gate_a_w=v_lru_gate_a_w, v_lru_gate_a_b=v_lru_gate_a_b, v_lru_gate_x_w=v_lru_gate_x_w, v_lru_gate_x_b=v_lru_gate_x_b, v_lru_lambda=v_lru_lambda, v_lru_norm_w=v_lru_norm_w, v_gdn_conv_w=v_gdn_conv_w, v_gdn_a_log=v_gdn_a_log, v_gdn_dt_bias=v_gdn_dt_bias, v_gdn_norm_w=v_gdn_norm_w, v_w_out=v_w_out, v_w_up=v_w_up, v_w_down=v_w_down, v_final_norm_w=v_final_norm_w)
    weights = {n: given[n] for n in TWIN_WEIGHTS}
    shared = {n: given[n] for n in SHARED_INPUTS}
    per_example = {n: given[n] for n in ['x', 'c']}
    grad_fn = _jax.value_and_grad(_loss, argnums=(0, 1))

    def one_microbatch(ex, loss_target):
        ex = dict(ex)
        diff = ex.pop(TWIN_DIFF_INPUT)
        return grad_fn(weights, diff, {**shared, **ex}, loss_target)

    if N_MICROBATCH == 1:
        loss, (grad_w, grad_x) = one_microbatch(per_example, given["loss_target"])
    else:
        def body(carry, xs):
            loss_sum, grad_sum = carry
            l_k, (gw_k, gx_k) = one_microbatch(xs[0], xs[1])
            with _jax.named_scope("update"):
                return (loss_sum + l_k, _jax.tree.map(_jnp.add, grad_sum, gw_k)), gx_k

        init = (_jnp.zeros((), _jnp.float32), _jax.tree.map(_jnp.zeros_like, weights))
        (loss, grad_w), grad_x = _jax.lax.scan(body, init, (per_example, given["loss_target"]))
    with _jax.named_scope("update"):
        delta_w, new_m, new_v = {}, {}, {}
        for n in TWIN_WEIGHTS:
            delta_w[n], new_m[n], new_v[n] = _adamw(weights[n], grad_w[n], given["m_" + n], given["v_" + n])
    return (loss, grad_x, *[grad_w[n] for n in TWIN_WEIGHTS], *[delta_w[n] for n in TWIN_WEIGHTS],
            *[new_m[n] for n in TWIN_WEIGHTS], *[new_v[n] for n in TWIN_WEIGHTS])
```

```python
import functools

import jax
import jax.numpy as jnp
from jax import lax
from jax.experimental import pallas as pl
from jax.experimental.pallas import tpu as pltpu

F32 = jnp.float32
BF16 = jnp.bfloat16

D_MODEL = 1024
LRU_W = 512
LRU_BLOCKS = 8
LRU_BLOCK = 64
LRU_C = 8.0
GDN_W = 512
HEADS = 4
HEAD_DIM = 128
CHUNK = 64
STACK = HEADS * CHUNK
CONV_K = 4
D_FF = 4096
N_MOD = 6
IN_COLS = 3080
IN_MAIN = 3072
BA_PAD = 128
EPS = 1e-6
N_DEV = 8
HALO = 8
LANES = 128
ADAM_LR, ADAM_B1, ADAM_B2, ADAM_EPS, ADAM_WD, ADAM_STEP = 0.001, 0.9, 0.999, 1e-08, 0.01, 10
MESH_AXES = ("x", "y", "c")
MESH = pl.DeviceIdType.MESH

NN = (((1,), (0,)), ((), ()))
NT = (((1,), (1,)), ((), ()))
TN = (((0,), (0,)), ((), ()))


def _bdot(a, b, dims=NN):
    return lax.dot_general(a.astype(BF16), b.astype(BF16), dims, preferred_element_type=F32)


def _sdot(a, b):
    ah, bh = a.astype(BF16), b.astype(BF16)
    al, bl = (a - ah.astype(F32)).astype(BF16), (b - bh.astype(F32)).astype(BF16)
    return _bdot(ah, bh) + (_bdot(al, bh) + _bdot(ah, bl))


def _hdot(a, b, dims=NN):
    return lax.dot_general(a, b, dims, precision=lax.Precision.HIGHEST, preferred_element_type=F32)


def _sds(shape, dtype=F32):
    return jax.ShapeDtypeStruct(tuple(shape), dtype)


def _tile(n, t):
    return min(n, t)


def _call(body, name, grid, in_specs, out_specs, out_shape, scratch=(), vmem_mb=48, prefetch=0):
    params = pltpu.CompilerParams(dimension_semantics=("arbitrary",) * len(grid), vmem_limit_bytes=vmem_mb * 2**20)
    if prefetch:
        spec = pltpu.PrefetchScalarGridSpec(num_scalar_prefetch=prefetch, grid=grid, in_specs=in_specs,
                                            out_specs=out_specs, scratch_shapes=list(scratch))
        return pl.pallas_call(body, name=name, grid_spec=spec, out_shape=out_shape, compiler_params=params)
    return pl.pallas_call(body, name=name, grid=grid, in_specs=in_specs, out_specs=out_specs, out_shape=out_shape,
                          scratch_shapes=list(scratch), compiler_params=params)


def _tok(t, n, col=0):
    return pl.BlockSpec((t, n), lambda i, *_: (i, col))


def _vec(n):
    return pl.BlockSpec((1, n), lambda *_: (0, 0))


def _whole(a):
    nd = a.ndim
    return pl.BlockSpec(a.shape, lambda *_: (0,) * nd)


def _gelu(y):
    c0, c1 = 0.7978845608028654, 0.044715
    return 0.5 * y * (1.0 + jnp.tanh(c0 * (y + c1 * y * y * y)))


def _gelu_grad(y):
    c0, c1 = 0.7978845608028654, 0.044715
    t = jnp.tanh(c0 * (y + c1 * y * y * y))
    return 0.5 * (1.0 + t) + 0.5 * y * (1.0 - t * t) * c0 * (1.0 + 3.0 * c1 * y * y)


def _softplus(v):
    return jnp.maximum(v, 0.0) + jnp.log(1.0 + jnp.exp(-jnp.where(v > 0, v, -v)))


@functools.partial(jax.custom_vjp, nondiff_argnums=(1,))
def _roll_rows(v, s):
    s = s % v.shape[0]
    return pltpu.roll(v, s, axis=0) if s else v


def _roll_rows_fwd(v, s):
    return _roll_rows(v, s), None


def _roll_rows_bwd(s, _, g):
    return (_roll_rows(g, -s),)


_roll_rows.defvjp(_roll_rows_fwd, _roll_rows_bwd)


@jax.custom_vjp
def _drop_halo(v):
    return v[HALO:]


def _drop_halo_fwd(v):
    return v[HALO:], None


def _drop_halo_bwd(_, g):
    return (jnp.concatenate([jnp.zeros((HALO, g.shape[1]), g.dtype), g], axis=0),)


_drop_halo.defvjp(_drop_halo_fwd, _drop_halo_bwd)


@functools.partial(jax.custom_vjp, nondiff_argnums=(1, 2))
def _split(v, n, axis):
    w = v.shape[axis] // n
    return tuple(lax.slice_in_dim(v, k * w, (k + 1) * w, axis=axis) for k in range(n))


def _split_fwd(v, n, axis):
    return _split(v, n, axis), None


def _split_bwd(n, axis, _, gs):
    return (jnp.concatenate(list(gs), axis=axis),)


_split.defvjp(_split_fwd, _split_bwd)


def _conv_taps(xw):
    return [_drop_halo(_roll_rows(xw, CONV_K - 1 - k)) for k in range(CONV_K)]


def _modulated_norm(xv, nw, sc, sh):
    r = lax.rsqrt(jnp.mean(xv * xv, axis=-1, keepdims=True) + EPS)
    n = xv * r * nw
    return n * (1.0 + sc) + sh, n, r


def _modulated_norm_bwd(dh, xv, n, r, nw, sc):
    dn = dh * (1.0 + sc)
    dxn = dn * nw
    dx = r * dxn - xv * (r * r * r) * jnp.mean(dxn * xv, axis=-1, keepdims=True)
    return (dx, jnp.sum(dh, axis=0, keepdims=True), jnp.sum(dh * n, axis=0, keepdims=True),
            jnp.sum(dn * xv * r, axis=0, keepdims=True))


def _inproj_fwd(x, nw, sc, sh, win, wba):
    s = x.shape[0]
    t = _tile(s, 256)

    def body(x_ref, nw_ref, sc_ref, sh_ref, win_ref, wba_ref, proj_ref, ba_ref):
        h, _, _ = _modulated_norm(x_ref[...], nw_ref[...], sc_ref[...], sh_ref[...])
        hb = h.astype(BF16)
        proj_ref[...] = _bdot(hb, win_ref[...])
        ba_ref[...] = _bdot(hb, wba_ref[...])

    return _call(body, "inproj_fwd", (s // t,),
                 [_tok(t, D_MODEL), _vec(D_MODEL), _vec(D_MODEL), _vec(D_MODEL), _whole(win), _whole(wba)],
                 [_tok(t, IN_MAIN), _tok(t, BA_PAD)],
                 [_sds((s, IN_MAIN)), _sds((s, BA_PAD))])(x, nw, sc, sh, win, wba)


def _inproj_bwd(dpl, dpq, dpz, dba, x, dx1, nw, sc, sh, win, wba):
    s = x.shape[0]
    t = _tile(s, 256)

    def body(dpl_ref, dpq_ref, dpz_ref, dba_ref, x_ref, dx1_ref, nw_ref, sc_ref, sh_ref, win_ref, wba_ref,
             dx_ref, hb_ref, acc_ref):
        @pl.when(pl.program_id(0) == 0)
        def _():
            acc_ref[...] = jnp.zeros_like(acc_ref)

        dh = (_bdot(dpl_ref[...], win_ref[:, 0:2 * LRU_W], NT)
              + _bdot(dpq_ref[...], win_ref[:, 2 * LRU_W:2 * LRU_W + 3 * GDN_W], NT)
              + _bdot(dpz_ref[...], win_ref[:, 2 * LRU_W + 3 * GDN_W:IN_MAIN], NT)
              + _bdot(dba_ref[...], wba_ref[...], NT))
        xv = x_ref[...]
        h, n, r = _modulated_norm(xv, nw_ref[...], sc_ref[...], sh_ref[...])
        hb_ref[...] = h.astype(BF16)
        dx, dsh, dsc, dnw = _modulated_norm_bwd(dh, xv, n, r, nw_ref[...], sc_ref[...])
        dx_ref[...] = dx1_ref[...] + dx
        acc_ref[0:1, :] += dsh
        acc_ref[1:2, :] += dsc
        acc_ref[2:3, :] += dnw

    return _call(body, "inproj_bwd", (s // t,),
                 [_tok(t, 2 * LRU_W), _tok(t, 3 * GDN_W), _tok(t, GDN_W), _tok(t, BA_PAD), _tok(t, D_MODEL),
                  _tok(t, D_MODEL), _vec(D_MODEL), _vec(D_MODEL), _vec(D_MODEL), _whole(win), _whole(wba)],
                 [_tok(t, D_MODEL), _tok(t, D_MODEL), pl.BlockSpec((8, D_MODEL), lambda i: (0, 0))],
                 [_sds((s, D_MODEL)), _sds((s, D_MODEL), BF16), _sds((8, D_MODEL))])(
                     dpl, dpq, dpz, dba, x, dx1, nw, sc, sh, win, wba)


def _lru_gates(xw, cw_rows, cb, wa, wx, gab, gxb, lam):
    taps = _conv_taps(xw)
    xr = cb + cw_rows[0] * taps[0] + cw_rows[1] * taps[1] + cw_rows[2] * taps[2] + cw_rows[3] * taps[3]
    xb = xr.astype(BF16)
    r = jax.nn.sigmoid(_bdot(xb, wa) + gab)
    i = jax.nn.sigmoid(_bdot(xb, wx) + gxb)
    z = jnp.exp(-jnp.where(lam > 0, lam, -lam))
    w1 = 1.0 + z
    log1p_z = jnp.where(w1 == 1.0, z, jnp.log(w1) * z / (w1 - 1.0))
    ls = jnp.minimum(lam, 0.0) - log1p_z
    la = LRU_C * r * ls
    a = jnp.exp(la)
    x2 = 2.0 * la
    u = jnp.exp(x2)
    mm_raw = jnp.where(u == 1.0, -x2,
                       jnp.where(x2 < -30.0, 1.0, (1.0 - u) * x2 / jnp.log(jnp.maximum(u, 1e-30))))
    mult = jnp.sqrt(jnp.maximum(mm_raw, 1e-12))
    return dict(taps=taps, xr=xr, r=r, i=i, ls=ls, a=a, mm_raw=mm_raw, mult=mult)


def _lru_specs(s, t, tile_of):
    nh = t // HALO
    xl = pl.BlockSpec((t, LRU_W), lambda i: (tile_of(i), 0))
    yl = pl.BlockSpec((t, LRU_W), lambda i: (tile_of(i), 1))
    hx = pl.BlockSpec((HALO, LRU_W), lambda i: (jnp.maximum(tile_of(i) * nh - 1, 0), 0))
    return xl, yl, hx


def _lru_fwd(proj, cw, cb, wa, wx, gab, gxb, lam, lnw):
    s = proj.shape[0]
    t = _tile(s, 256)
    xl, yl, hx = _lru_specs(s, t, lambda i: i)

    def body(xl_ref, yl_ref, hx_ref, cw_ref, cb_ref, wa_ref, wx_ref, gab_ref, gxb_ref, lam_ref, lnw_ref,
             out_ref, h_ref, a_s, b_s, hc):
        i = pl.program_id(0)

        @pl.when(i == 0)
        def _():
            hc[...] = jnp.zeros_like(hc)

        halo = jnp.where(i > 0, hx_ref[...], 0.0)
        xw = jnp.concatenate([halo, xl_ref[...]], axis=0)
        g = _lru_gates(xw, [cw_ref[k:k + 1, :] for k in range(CONV_K)], cb_ref[...], wa_ref[...], wx_ref[...],
                       gab_ref[...], gxb_ref[...], lam_ref[...])
        a_s[...] = g["a"]
        b_s[...] = g["mult"] * (g["i"] * g["xr"])

        def step(k, h):
            h = a_s[pl.ds(k, 1), :] * h + b_s[pl.ds(k, 1), :]
            h_ref[pl.ds(k, 1), :] = h
            return h

        hc[...] = lax.fori_loop(0, t, step, hc[...], unroll=8)
        m = h_ref[...] * _gelu(yl_ref[...])
        out_ref[...] = m * lax.rsqrt(jnp.mean(m * m, axis=-1, keepdims=True) + EPS) * lnw_ref[...]

    return _call(body, "lru_fwd", (s // t,),
                 [xl, yl, hx, _whole(cw), _vec(LRU_W), _whole(wa), _whole(wx)] + [_vec(LRU_W)] * 4,
                 [_tok(t, LRU_W), _tok(t, LRU_W)],
                 [_sds((s, LRU_W)), _sds((s, LRU_W))],
                 scratch=[pltpu.VMEM((t, LRU_W), F32), pltpu.VMEM((t, LRU_W), F32), pltpu.VMEM((1, LRU_W), F32)])(
                     proj, proj, proj, cw, cb, wa, wx, gab, gxb, lam, lnw)


def _lru_bwd(dout, proj, hs, cw, cb, wa, wx, gab, gxb, lam, lnw):
    s = proj.shape[0]
    t = _tile(s, 256)
    nt = s // t
    rev = lambda i: nt - 1 - i
    xl, yl, hx = _lru_specs(s, t, rev)
    nh = t // HALO
    tk = pl.BlockSpec((t, LRU_W), lambda i: (rev(i), 0))
    hh = pl.BlockSpec((HALO, LRU_W), lambda i: (jnp.maximum(rev(i) * nh - 1, 0), 0))

    def body(do_ref, xl_ref, yl_ref, hx_ref, h_ref, hh_ref, cw_ref, cb_ref, wa_ref, wx_ref, gab_ref, gxb_ref,
             lam_ref, lnw_ref, dp_ref, dwa_ref, dwx_ref, rows_ref, dh_s, dhd_s, carry, dxr_next):
        i = pl.program_id(0)
        first_tile = rev(i) == 0

        @pl.when(i == 0)
        def _():
            carry[...] = jnp.zeros_like(carry)
            dxr_next[...] = jnp.zeros_like(dxr_next)
            dwa_ref[...] = jnp.zeros_like(dwa_ref)
            dwx_ref[...] = jnp.zeros_like(dwx_ref)
            rows_ref[...] = jnp.zeros_like(rows_ref)

        halo = jnp.where(first_tile, 0.0, hx_ref[...])
        xw = jnp.concatenate([halo, xl_ref[...]], axis=0)
        cw_rows = [cw_ref[k:k + 1, :] for k in range(CONV_K)]
        lam_v = lam_ref[...]
        g = _lru_gates(xw, cw_rows, cb_ref[...], wa_ref[...], wx_ref[...], gab_ref[...], gxb_ref[...], lam_v)
        a, r, gi, xr, mult = g["a"], g["r"], g["i"], g["xr"], g["mult"]
        hv = h_ref[...]
        yv = yl_ref[...]
        gl = _gelu(yv)
        m = hv * gl
        rn = lax.rsqrt(jnp.mean(m * m, axis=-1, keepdims=True) + EPS)
        dov = do_ref[...]
        dmn = dov * lnw_ref[...]
        rows_ref[4:5, :] += jnp.sum(dov * m * rn, axis=0, keepdims=True)
        dm = rn * dmn - m * (rn * rn * rn) * jnp.mean(dmn * m, axis=-1, keepdims=True)
        dhd_s[...] = dm * gl
        dy = dm * hv * _gelu_grad(yv)
        dh_s[...] = a

        def step(k, c):
            row = t - 1 - k
            d = dhd_s[pl.ds(row, 1), :] + c
            c = dh_s[pl.ds(row, 1), :] * d
            dh_s[pl.ds(row, 1), :] = d
            return c

        carry[...] = lax.fori_loop(0, t, step, carry[...], unroll=8)
        dH = dh_s[...]
        hprev_halo = jnp.where(first_tile, 0.0, hh_ref[...])
        hprev = _drop_halo(_roll_rows(jnp.concatenate([hprev_halo, hv], axis=0), 1))
        da = dH * hprev
        dmult = dH * gi * xr
        di = dH * mult * xr
        dxr = dH * mult * gi
        dla = jnp.where(g["mm_raw"] > 1e-12, dmult * (0.5 / mult) * (-2.0 * a * a), 0.0) + da * a
        dr = dla * (LRU_C * g["ls"])
        sig_neg = jax.nn.sigmoid(-lam_v)
        rows_ref[3:4, :] += jnp.sum(dla * (LRU_C * r), axis=0, keepdims=True) * sig_neg
        drp = dr * r * (1.0 - r)
        dip = di * gi * (1.0 - gi)
        rows_ref[1:2, :] += jnp.sum(drp, axis=0, keepdims=True)
        rows_ref[2:3, :] += jnp.sum(dip, axis=0, keepdims=True)
        xb = xr.astype(BF16)
        drb = drp.astype(BF16)
        dib = dip.astype(BF16)
        dwa_ref[...] += _bdot(xb, drb, TN)
        dwx_ref[...] += _bdot(xb, dib, TN)
        dxr = dxr + _bdot(drb, wa_ref[...], NT) + _bdot(dib, wx_ref[...], NT)
        rows_ref[0:1, :] += jnp.sum(dxr, axis=0, keepdims=True)
        ext = jnp.concatenate([dxr, dxr_next[...]], axis=0)
        dx = cw_rows[CONV_K - 1] * dxr
        for k in range(CONV_K - 1):
            dx = dx + cw_rows[k] * _roll_rows(ext, -(CONV_K - 1 - k))[0:t]
        for k in range(CONV_K):
            rows_ref[8 + k:9 + k, :] += jnp.sum(dxr * g["taps"][k], axis=0, keepdims=True)
        dxr_next[...] = dxr[0:HALO]
        dp_ref[...] = jnp.concatenate([dx, dy], axis=1).astype(BF16)

    acc = lambda shape: pl.BlockSpec(shape, lambda i: (0, 0))
    return _call(body, "lru_bwd", (nt,),
                 [tk, xl, yl, hx, tk, hh, _whole(cw), _vec(LRU_W), _whole(wa), _whole(wx)] + [_vec(LRU_W)] * 4,
                 [pl.BlockSpec((t, 2 * LRU_W), lambda i: (rev(i), 0)), acc((LRU_W, LRU_W)), acc((LRU_W, LRU_W)),
                  acc((16, LRU_W))],
                 [_sds((s, 2 * LRU_W), BF16), _sds((LRU_W, LRU_W)), _sds((LRU_W, LRU_W)), _sds((16, LRU_W))],
                 scratch=[pltpu.VMEM((t, LRU_W), F32), pltpu.VMEM((t, LRU_W), F32), pltpu.VMEM((1, LRU_W), F32),
                          pltpu.VMEM((HALO, LRU_W), F32)])(
                     dout, proj, proj, proj, hs, hs, cw, cb, wa, wx, gab, gxb, lam, lnw)


def _gdn_masks():
    row = lax.broadcasted_iota(jnp.int32, (STACK, STACK), 0)
    col = lax.broadcasted_iota(jnp.int32, (STACK, STACK), 1)
    same = (row >> 6) == (col >> 6)
    ti, tj = row & (CHUNK - 1), col & (CHUNK - 1)
    causal = same & (tj <= ti)
    strict = same & (tj < ti)
    return dict(causal=causal, strict=strict, lower=causal.astype(F32), upper_strict=strict.astype(F32),
                block=same.astype(F32), eye=(row == col).astype(F32), lane=lax.broadcasted_iota(jnp.int32, (CHUNK, LANES), 1))


def _conv_silu(xw, rows):
    taps = _conv_taps(xw)
    y = rows[0] * taps[0] + rows[1] * taps[1] + rows[2] * taps[2] + rows[3] * taps[3]
    return y * jax.nn.sigmoid(y)


def _gdn_prep(xq, xk, xv, ba, cwq, cwk, cwv, pa, pd, cst):
    q = jnp.concatenate(_split(_conv_silu(xq, cwq), HEADS, 1), axis=0)
    k = jnp.concatenate(_split(_conv_silu(xk, cwk), HEADS, 1), axis=0)
    v = jnp.concatenate(_split(_conv_silu(xv, cwv), HEADS, 1), axis=0)
    qn = q * lax.rsqrt(jnp.sum(q * q, axis=-1, keepdims=True) + 1e-6) * (HEAD_DIM ** -0.5)
    kn = k * lax.rsqrt(jnp.sum(k * k, axis=-1, keepdims=True) + 1e-6)
    beta_f = jax.nn.sigmoid(ba)
    g_f = -jnp.exp(pa) * _softplus(ba + pd)

    def col(a, j):
        return jnp.broadcast_to(jnp.sum(jnp.where(cst["lane"] == j, a, 0.0), axis=1, keepdims=True),
                                (CHUNK, HEAD_DIM))

    beta = jnp.concatenate([col(beta_f, h) for h in range(HEADS)], axis=0)
    g = jnp.concatenate([col(g_f, HEADS + h) for h in range(HEADS)], axis=0)
    g2 = jnp.concatenate([g, g], axis=1)
    m = _hdot(cst["lower"], g2 * cst["upper_strict"])
    gc = _hdot(cst["lower"], g)
    gl = _hdot(cst["block"], g)
    decay = jnp.exp(m)
    egc = jnp.exp(gc)
    kb = kn * beta
    n = -jnp.where(cst["strict"], _bdot(kb, kn, NT) * decay, 0.0)
    rhs = jnp.concatenate([v * beta, kb * egc], axis=1)
    tinv = cst["eye"] + n
    p = n
    for _ in range(5):
        p = _bdot(p, p)
        tinv = tinv + _bdot(tinv, p)
    x0 = _bdot(tinv, rhs)
    x1 = x0 + _bdot(tinv, rhs - x0 + _sdot(n, x0))
    u, w = _split(x1, 2, 1)
    attn = jnp.where(cst["causal"], _bdot(qn, kn, NT) * decay, 0.0)
    return u, w, qn * egc, kn * jnp.exp(gl - gc), attn, jnp.exp(gl)


def _gdn_scan(states, u, w, qd, kt, attn, egl, z, nw):
    us, ws, qds, kts, egls = (_split(a, HEADS, 0) for a in (u, w, qd, kt, egl))
    vn = [us[h] - _bdot(ws[h], states[h]) for h in range(HEADS)]
    o = jnp.concatenate([_bdot(qds[h], states[h]) for h in range(HEADS)], axis=0)
    o = o + _bdot(attn, jnp.concatenate(vn, axis=0))
    new = [states[h] * jnp.concatenate([egls[h], egls[h]], axis=0) + _bdot(kts[h], vn[h], TN) for h in range(HEADS)]
    on = o * lax.rsqrt(jnp.mean(o * o, axis=-1, keepdims=True) + EPS) * nw
    return new, on * (z * jax.nn.sigmoid(z))


def _gdn_in_specs(s, chunk_of):
    nh = CHUNK // HALO
    main = [pl.BlockSpec((CHUNK, GDN_W), functools.partial(lambda col, i: (chunk_of(i), col), col))
            for col in (2, 3, 4)]
    halo = [pl.BlockSpec((HALO, GDN_W), functools.partial(lambda col, i: (jnp.maximum(chunk_of(i) * nh - 1, 0), col),
                                                         col)) for col in (2, 3, 4)]
    return main, halo


def _stk(width, chunk_of):
    return pl.BlockSpec((STACK, width), lambda i: (chunk_of(i), 0))


def _gdn_prep_fwd(proj, ba, cw, pa, pd):
    s = proj.shape[0]
    nc = s // CHUNK
    main, halo = _gdn_in_specs(s, lambda i: i)

    def body(xq_ref, xk_ref, xv_ref, hq_ref, hk_ref, hv_ref, ba_ref, cw_ref, pa_ref, pd_ref,
             u_ref, w_ref, qd_ref, kt_ref, attn_ref, egl_ref):
        i = pl.program_id(0)
        xs = [jnp.concatenate([jnp.where(i > 0, h[...], 0.0), m[...]], axis=0)
              for h, m in ((hq_ref, xq_ref), (hk_ref, xk_ref), (hv_ref, xv_ref))]
        rows = [[cw_ref[k:k + 1, j * GDN_W:(j + 1) * GDN_W] for k in range(CONV_K)] for j in range(3)]
        outs = _gdn_prep(xs[0], xs[1], xs[2], ba_ref[...], rows[0], rows[1], rows[2], pa_ref[...], pd_ref[...],
                         _gdn_masks())
        for ref, val in zip((u_ref, w_ref, qd_ref, kt_ref, attn_ref, egl_ref), outs):
            ref[...] = val

    ident = lambda i: i
    return _call(body, "gdn_prep_fwd", (nc,),
                 main + halo + [_tok(CHUNK, BA_PAD), _whole(cw), _vec(BA_PAD), _vec(BA_PAD)],
                 [_stk(HEAD_DIM, ident)] * 4 + [_stk(STACK, ident), _stk(HEAD_DIM, ident)],
                 [_sds((nc * STACK, HEAD_DIM))] * 4 + [_sds((nc * STACK, STACK)), _sds((nc * STACK, HEAD_DIM))])(
                     proj, proj, proj, proj, proj, proj, ba, cw, pa, pd)


def _gdn_prep_bwd(cts, proj, ba, cw, pa, pd):
    s = proj.shape[0]
    nc = s // CHUNK
    rev = lambda i: nc - 1 - i
    main, halo = _gdn_in_specs(s, rev)

    def body(du_ref, dw_ref, dqd_ref, dkt_ref, dattn_ref, degl_ref, xq_ref, xk_ref, xv_ref, hq_ref, hk_ref, hv_ref,
             ba_ref, cw_ref, pa_ref, pd_ref, dp_ref, dba_ref, dcw_ref, dpar_ref, carry):
        i = pl.program_id(0)
        first_chunk = rev(i) == 0

        @pl.when(i == 0)
        def _():
            carry[...] = jnp.zeros_like(carry)
            dcw_ref[...] = jnp.zeros_like(dcw_ref)
            dpar_ref[...] = jnp.zeros_like(dpar_ref)

        xs = [jnp.concatenate([jnp.where(first_chunk, 0.0, h[...]), m[...]], axis=0)
              for h, m in ((hq_ref, xq_ref), (hk_ref, xk_ref), (hv_ref, xv_ref))]
        rows = [[cw_ref[k:k + 1, j * GDN_W:(j + 1) * GDN_W] for k in range(CONV_K)] for j in range(3)]
        cst = _gdn_masks()
        fn = lambda xq, xk, xv, b, rq, rk, rv, a, d: _gdn_prep(xq, xk, xv, b, rq, rk, rv, a, d, cst)
        _, vjp = jax.vjp(fn, xs[0], xs[1], xs[2], ba_ref[...], rows[0], rows[1], rows[2], pa_ref[...], pd_ref[...])
        dxq, dxk, dxv, dba, drq, drk, drv, dpa, dpd = vjp(
            (du_ref[...], dw_ref[...], dqd_ref[...], dkt_ref[...], dattn_ref[...], degl_ref[...]))
        dxw = jnp.concatenate([dxq, dxk, dxv], axis=1)
        tail = jnp.concatenate([jnp.zeros((CHUNK - HALO, 3 * GDN_W), F32), carry[...]], axis=0)
        dp_ref[...] = (dxw[HALO:] + tail).astype(BF16)
        carry[...] = dxw[0:HALO]
        dba_ref[...] = dba.astype(BF16)
        for j, dr in enumerate((drq, drk, drv)):
            for k in range(CONV_K):
                dcw_ref[k:k + 1, j * GDN_W:(j + 1) * GDN_W] += dr[k]
        dpar_ref[0:1, :] += dpa
        dpar_ref[1:2, :] += dpd

    acc = lambda shape: pl.BlockSpec(shape, lambda i: (0, 0))
    return _call(body, "gdn_prep_bwd", (nc,),
                 [_stk(HEAD_DIM, rev)] * 4 + [_stk(STACK, rev), _stk(HEAD_DIM, rev)] + main + halo
                 + [pl.BlockSpec((CHUNK, BA_PAD), lambda i: (rev(i), 0)), _whole(cw), _vec(BA_PAD), _vec(BA_PAD)],
                 [pl.BlockSpec((CHUNK, 3 * GDN_W), lambda i: (rev(i), 0)),
                  pl.BlockSpec((CHUNK, BA_PAD), lambda i: (rev(i), 0)), acc((CONV_K, 3 * GDN_W)), acc((8, BA_PAD))],
                 [_sds((s, 3 * GDN_W), BF16), _sds((s, BA_PAD), BF16), _sds((CONV_K, 3 * GDN_W)), _sds((8, BA_PAD))],
                 scratch=[pltpu.VMEM((HALO, 3 * GDN_W), F32)])(
                     *cts, proj, proj, proj, proj, proj, proj, ba, cw, pa, pd)


def _stack_heads(v):
    return jnp.concatenate(_split(v, HEADS, 1), axis=0)


def _unstack_heads(v):
    return jnp.concatenate(_split(v, HEADS, 0), axis=1)


def _gdn_scan_fwd(prep, proj, nw):
    s = proj.shape[0]
    nc = s // CHUNK
    ident = lambda i: i

    def body(u_ref, w_ref, qd_ref, kt_ref, attn_ref, egl_ref, z_ref, nw_ref, out_ref, st_ref, state):
        @pl.when(pl.program_id(0) == 0)
        def _():
            state[...] = jnp.zeros_like(state)

        st_ref[...] = state[...]
        states = [state[h * HEAD_DIM:(h + 1) * HEAD_DIM, :] for h in range(HEADS)]
        new, out = _gdn_scan(states, u_ref[...], w_ref[...], qd_ref[...], kt_ref[...], attn_ref[...], egl_ref[...],
                             _stack_heads(z_ref[...]), nw_ref[...])
        for h in range(HEADS):
            state[h * HEAD_DIM:(h + 1) * HEAD_DIM, :] = new[h]
        out_ref[...] = _unstack_heads(out)

    return _call(body, "gdn_scan_fwd", (nc,),
                 [_stk(HEAD_DIM, ident)] * 4 + [_stk(STACK, ident), _stk(HEAD_DIM, ident),
                                                _tok(CHUNK, GDN_W, col=5), _vec(HEAD_DIM)],
                 [_tok(CHUNK, GDN_W), pl.BlockSpec((HEADS * HEAD_DIM, HEAD_DIM), lambda i: (i, 0))],
                 [_sds((s, GDN_W)), _sds((nc * HEADS * HEAD_DIM, HEAD_DIM))],
                 scratch=[pltpu.VMEM((HEADS * HEAD_DIM, HEAD_DIM), F32)])(*prep, proj, nw)


def _gdn_scan_bwd(dout, prep, st, proj, nw):
    s = proj.shape[0]
    nc = s // CHUNK
    rev = lambda i: nc - 1 - i

    def body(do_ref, u_ref, w_ref, qd_ref, kt_ref, attn_ref, egl_ref, st_ref, z_ref, nw_ref,
             du_ref, dw_ref, dqd_ref, dkt_ref, dattn_ref, degl_ref, dz_ref, dnw_ref, dstate):
        @pl.when(pl.program_id(0) == 0)
        def _():
            dstate[...] = jnp.zeros_like(dstate)
            dnw_ref[...] = jnp.zeros_like(dnw_ref)

        states = [st_ref[h * HEAD_DIM:(h + 1) * HEAD_DIM, :] for h in range(HEADS)]
        _, vjp = jax.vjp(_gdn_scan, states, u_ref[...], w_ref[...], qd_ref[...], kt_ref[...], attn_ref[...],
                         egl_ref[...], _stack_heads(z_ref[...]), nw_ref[...])
        dnew = [dstate[h * HEAD_DIM:(h + 1) * HEAD_DIM, :] for h in range(HEADS)]
        dst, du, dw, dqd, dkt, dattn, degl, dz, dnw = vjp((dnew, _stack_heads(do_ref[...])))
        for h in range(HEADS):
            dstate[h * HEAD_DIM:(h + 1) * HEAD_DIM, :] = dst[h]
        for ref, val in zip((du_ref, dw_ref, dqd_ref, dkt_ref, dattn_ref, degl_ref), (du, dw, dqd, dkt, dattn, degl)):
            ref[...] = val
        dz_ref[...] = _unstack_heads(dz).astype(BF16)
        dnw_ref[0:1, :] += dnw

    tokr = lambda n, col=0: pl.BlockSpec((CHUNK, n), lambda i: (rev(i), col))
    return _call(body, "gdn_scan_bwd", (nc,),
                 [tokr(GDN_W)] + [_stk(HEAD_DIM, rev)] * 4 + [_stk(STACK, rev), _stk(HEAD_DIM, rev),
                                                             pl.BlockSpec((HEADS * HEAD_DIM, HEAD_DIM),
                                                                          lambda i: (rev(i), 0)),
                                                             tokr(GDN_W, 5), _vec(HEAD_DIM)],
                 [_stk(HEAD_DIM, rev)] * 4 + [_stk(STACK, rev), _stk(HEAD_DIM, rev), tokr(GDN_W),
                                              pl.BlockSpec((8, HEAD_DIM), lambda i: (0, 0))],
                 [_sds((nc * STACK, HEAD_DIM))] * 4 + [_sds((nc * STACK, STACK)), _sds((nc * STACK, HEAD_DIM)),
                                                       _sds((s, GDN_W), BF16), _sds((8, HEAD_DIM))],
                 scratch=[pltpu.VMEM((HEADS * HEAD_DIM, HEAD_DIM), F32)])(dout, *prep, st, proj, nw)


def _out_mlp_fwd(ol, og, x, wo, g1, nw2, sc2, sh2, g2, wup, wdn):
    s = x.shape[0]
    t = _tile(s, 512)
    nj = wup.shape[0]
    fc = wup.shape[2]

    def body(ol_ref, og_ref, x_ref, wo_ref, g1_ref, nw_ref, sc_ref, sh_ref, g2_ref, wup_ref, wdn_ref,
             x1_ref, mix_ref, ff_ref, x2_ref, h2_s, acc_s):
        j = pl.program_id(1)

        @pl.when(j == 0)
        def _():
            mix = _bdot(ol_ref[...], wo_ref[0:LRU_W, :]) + _bdot(og_ref[...], wo_ref[LRU_W:LRU_W + GDN_W, :])
            x1 = x_ref[...] + g1_ref[...] * mix
            mix_ref[...] = mix.astype(BF16)
            x1_ref[...] = x1
            h2, _, _ = _modulated_norm(x1, nw_ref[...], sc_ref[...], sh_ref[...])
            h2_s[...] = h2.astype(BF16)
            acc_s[...] = jnp.zeros_like(acc_s)

        up = _bdot(h2_s[...], wup_ref[0])
        act = jnp.square(jnp.maximum(up, 0.0))
        acc_s[...] += _bdot(act, wdn_ref[0])

        @pl.when(j == nj - 1)
        def _():
            ff_ref[...] = acc_s[...].astype(BF16)
            x2_ref[...] = x1_ref[...] + g2_ref[...] * acc_s[...]

    tk = lambda n: pl.BlockSpec((t, n), lambda i, j: (i, 0))
    return _call(body, "out_mlp_fwd", (s // t, nj),
                 [tk(LRU_W), tk(GDN_W), tk(D_MODEL), _whole(wo)] + [_vec(D_MODEL)] * 5
                 + [pl.BlockSpec((1, D_MODEL, fc), lambda i, j: (j, 0, 0)),
                    pl.BlockSpec((1, fc, D_MODEL), lambda i, j: (j, 0, 0))],
                 [tk(D_MODEL)] * 4,
                 [_sds((s, D_MODEL)), _sds((s, D_MODEL), BF16), _sds((s, D_MODEL), BF16), _sds((s, D_MODEL))],
                 scratch=[pltpu.VMEM((t, D_MODEL), BF16), pltpu.VMEM((t, D_MODEL), F32)])(
                     ol, og, x, wo, g1, nw2, sc2, sh2, g2, wup, wdn)


def _mlp_bwd(dx2, x1, ff, nw2, sc2, sh2, g2, wup, wdn):
    s = x1.shape[0]
    t = _tile(s, 512)
    nj = wup.shape[0]
    fc = wup.shape[2]

    def body(dx2_ref, x1_ref, ff_ref, nw_ref, sc_ref, sh_ref, g2_ref, wup_ref, wdn_ref,
             act_ref, dup_ref, h2_ref, dff_ref, dx1_ref, rows_ref, dh2_s):
        i, j = pl.program_id(0), pl.program_id(1)

        @pl.when((i == 0) & (j == 0))
        def _():
            rows_ref[...] = jnp.zeros_like(rows_ref)

        @pl.when(j == 0)
        def _():
            h2, _, _ = _modulated_norm(x1_ref[...], nw_ref[...], sc_ref[...], sh_ref[...])
            h2_ref[...] = h2.astype(BF16)
            dx2 = dx2_ref[...]
            dff_ref[...] = (dx2 * g2_ref[...]).astype(BF16)
            rows_ref[2:3, :] += jnp.sum(dx2 * ff_ref[...].astype(F32), axis=0, keepdims=True)
            dh2_s[...] = jnp.zeros_like(dh2_s)

        up = _bdot(h2_ref[...], wup_ref[0])
        ru = jnp.maximum(up, 0.0)
        act_ref[...] = (ru * ru).astype(BF16)
        dup = (_bdot(dff_ref[...], wdn_ref[0], NT) * (2.0 * ru)).astype(BF16)
        dup_ref[...] = dup
        dh2_s[...] += _bdot(dup, wup_ref[0], NT)

        @pl.when(j == nj - 1)
        def _():
            xv = x1_ref[...]
            _, n, r = _modulated_norm(xv, nw_ref[...], sc_ref[...], sh_ref[...])
            dx, dsh, dsc, dnw = _modulated_norm_bwd(dh2_s[...], xv, n, r, nw_ref[...], sc_ref[...])
            dx1_ref[...] = dx2_ref[...] + dx
            rows_ref[0:1, :] += dsh
            rows_ref[1:2, :] += dsc
            rows_ref[3:4, :] += dnw

    tk = lambda n: pl.BlockSpec((t, n), lambda i, j: (i, 0))
    tj = pl.BlockSpec((t, fc), lambda i, j: (i, j))
    return _call(body, "mlp_bwd", (s // t, nj),
                 [tk(D_MODEL)] * 3 + [_vec(D_MODEL)] * 4
                 + [pl.BlockSpec((1, D_MODEL, fc), lambda i, j: (j, 0, 0)),
                    pl.BlockSpec((1, fc, D_MODEL), lambda i, j: (j, 0, 0))],
                 [tj, tj, tk(D_MODEL), tk(D_MODEL), tk(D_MODEL), pl.BlockSpec((8, D_MODEL), lambda i, j: (0, 0))],
                 [_sds((s, nj * fc), BF16), _sds((s, nj * fc), BF16), _sds((s, D_MODEL), BF16),
                  _sds((s, D_MODEL), BF16), _sds((s, D_MODEL)), _sds((8, D_MODEL))],
                 scratch=[pltpu.VMEM((t, D_MODEL), F32)])(dx2, x1, ff, nw2, sc2, sh2, g2, wup, wdn)


def _outproj_bwd(dx1, mix, g1, wo):
    s = dx1.shape[0]
    t = _tile(s, 512)

    def body(dx1_ref, mix_ref, g1_ref, wo_ref, dmix_ref, dol_ref, dog_ref, rows_ref):
        @pl.when(pl.program_id(0) == 0)
        def _():
            rows_ref[...] = jnp.zeros_like(rows_ref)

        dx1v = dx1_ref[...]
        rows_ref[0:1, :] += jnp.sum(dx1v * mix_ref[...].astype(F32), axis=0, keepdims=True)
        dmix = (dx1v * g1_ref[...]).astype(BF16)
        dmix_ref[...] = dmix
        dol_ref[...] = _bdot(dmix, wo_ref[0:LRU_W, :], NT)
        dog_ref[...] = _bdot(dmix, wo_ref[LRU_W:LRU_W + GDN_W, :], NT)

    return _call(body, "outproj_bwd", (s // t,),
                 [_tok(t, D_MODEL), _tok(t, D_MODEL), _vec(D_MODEL), _whole(wo)],
                 [_tok(t, D_MODEL), _tok(t, LRU_W), _tok(t, GDN_W), pl.BlockSpec((8, D_MODEL), lambda i: (0, 0))],
                 [_sds((s, D_MODEL), BF16), _sds((s, LRU_W)), _sds((s, GDN_W)), _sds((8, D_MODEL))])(dx1, mix, g1, wo)


def _tn_matmul(a, b, name, blocked=False):
    s, m = a.shape
    n = b.shape[1]
    ts, bm = _tile(s, 512), _tile(m, 1024)
    bn = next(w for w in (512, 640, 384, 256, 128) if n % w == 0)

    def body(a_ref, b_ref, o_ref):
        @pl.when(pl.program_id(2) == 0)
        def _():
            o_ref[...] = jnp.zeros_like(o_ref)

        acc = _bdot(a_ref[...], b_ref[...], TN)
        o_ref[...] += acc.reshape(o_ref.shape)

    if blocked:
        out_spec, out_shape = pl.BlockSpec((1, bm, bn), lambda i, j, k: (j, i, 0)), _sds((n // bn, m, bn))
    else:
        out_spec, out_shape = pl.BlockSpec((bm, bn), lambda i, j, k: (i, j)), _sds((m, n))
    return _call(body, name, (m // bm, n // bn, s // ts),
                 [pl.BlockSpec((ts, bm), lambda i, j, k: (k, i)), pl.BlockSpec((ts, bn), lambda i, j, k: (k, j))],
                 out_spec, out_shape)(a, b)


def _final_fwd_bwd(x, target, fw):
    s = x.shape[0]
    t = _tile(s, 512)

    def body(x_ref, tg_ref, fw_ref, dx_ref, rows_ref):
        @pl.when(pl.program_id(0) == 0)
        def _():
            rows_ref[...] = jnp.zeros_like(rows_ref)

        xv = x_ref[...]
        fwv = fw_ref[...]
        r = lax.rsqrt(jnp.mean(xv * xv, axis=-1, keepdims=True) + EPS)
        err = xv * r * fwv - tg_ref[...]
        part = 0.5 * jnp.sum(jnp.mean(err * err, axis=-1, keepdims=True), axis=0, keepdims=True)
        rows_ref[1:2, :] += jnp.broadcast_to(part, (1, D_MODEL))
        dy = err * (1.0 / D_MODEL)
        rows_ref[0:1, :] += jnp.sum(dy * xv * r, axis=0, keepdims=True)
        dxn = dy * fwv
        dx_ref[...] = r * dxn - xv * (r * r * r) * jnp.mean(dxn * xv, axis=-1, keepdims=True)

    return _call(body, "final_fwd_bwd", (s // t,),
                 [_tok(t, D_MODEL), _tok(t, D_MODEL), _vec(D_MODEL)],
                 [_tok(t, D_MODEL), pl.BlockSpec((8, D_MODEL), lambda i: (0, 0))],
                 [_sds((s, D_MODEL)), _sds((8, D_MODEL))])(x, target, fw)


def _adamw(w, g, m, v):
    m = ADAM_B1 * m + (1.0 - ADAM_B1) * g
    v = ADAM_B2 * v + (1.0 - ADAM_B2) * (g * g)
    m_hat = m / (1.0 - ADAM_B1 ** ADAM_STEP)
    v_hat = v / (1.0 - ADAM_B2 ** ADAM_STEP)
    return -ADAM_LR * (m_hat / (jnp.sqrt(v_hat) + ADAM_EPS) + ADAM_WD * w), m, v


def _mod_local(c_all, wmod, bmod_cols):
    nl, _, cols = wmod.shape

    def body(c_ref, w_ref, b_ref, o_ref):
        cv = c_ref[...]
        o_ref[0] = _bdot(cv * jax.nn.sigmoid(cv), w_ref[0]) + b_ref[0]

    return _call(body, "mod_local", (nl,),
                 [_whole(c_all), pl.BlockSpec((1, D_MODEL, cols), lambda l: (l, 0, 0)),
                  pl.BlockSpec((1, 1, cols), lambda l: (l, 0, 0))],
                 pl.BlockSpec((1, N_DEV, cols), lambda l: (l, 0, 0)), _sds((nl, N_DEV, cols)))(c_all, wmod, bmod_cols)


def _wmod_update(c_all, dmod_cols, w, m, v):
    nl, _, cols = w.shape

    def body(c_ref, d_ref, w_ref, m_ref, v_ref, g_ref, dl_ref, nm_ref, nv_ref):
        cv = c_ref[...]
        g = _bdot(cv * jax.nn.sigmoid(cv), d_ref[0], TN)
        g_ref[0] = g
        dl_ref[0], nm_ref[0], nv_ref[0] = _adamw(w_ref[0], g, m_ref[0], v_ref[0])

    wspec = pl.BlockSpec((1, D_MODEL, cols), lambda l: (l, 0, 0))
    return _call(body, "wmod_update", (nl,),
                 [_whole(c_all), pl.BlockSpec((1, N_DEV, cols), lambda l: (l, 0, 0)), wspec, wspec, wspec],
                 [wspec] * 4, [_sds(w.shape)] * 4)(c_all, dmod_cols, w, m, v)


def _sum_devices(gathered):
    _, r, _ = gathered.shape

    def body(g_ref, o_ref):
        acc = g_ref[0]
        for d in range(1, N_DEV):
            acc = acc + g_ref[d]
        o_ref[...] = acc

    return _call(body, "sum_devices", (1,), [_whole(gathered)], pl.BlockSpec((r, LANES), lambda i: (0, 0)),
                 _sds((r, LANES)))(gathered)


def _adam_flat(w, g, m, v):
    r = w.shape[0]

    def body(w_ref, g_ref, m_ref, v_ref, dl_ref, nm_ref, nv_ref):
        dl_ref[...], nm_ref[...], nv_ref[...] = _adamw(w_ref[...], g_ref[...], m_ref[...], v_ref[...])

    spec = pl.BlockSpec((r, LANES), lambda i: (0, 0))
    return _call(body, "adam_small", (1,), [spec] * 4, [spec] * 3, [_sds((r, LANES))] * 3)(w, g, m, v)


def _pair_add(x, p, core):
    _, r, c = x.shape
    tr = _tile(r, 128 if c > 512 else 256)

    def body(core_ref, x_ref, p_ref, o_ref):
        o_ref[...] = x_ref[...] + p_ref[...]

    return _call(body, "pair_add", (4, r // tr),
                 [pl.BlockSpec((1, tr, c), lambda q, i, core_ref: (2 * q + core_ref[0], i, 0)),
                  pl.BlockSpec((1, tr, c), lambda q, i, core_ref: (q, i, 0))],
                 pl.BlockSpec((1, tr, c), lambda q, i, core_ref: (q, i, 0)), _sds((4, r, c)), prefetch=1)(core, x, p)


def _reduce_adam(y, q, chip, w, m, v):
    _, r, c = y.shape
    tr = _tile(r, 128 if c > 512 else 256)

    def body(chip_ref, y_ref, q_ref, w_ref, m_ref, v_ref, g_ref, dl_ref, nm_ref, nv_ref):
        g = ((y_ref[0] + q_ref[0]) + q_ref[1]) + q_ref[2]
        g_ref[...] = g
        dl_ref[...], nm_ref[...], nv_ref[...] = _adamw(w_ref[...], g, m_ref[...], v_ref[...])

    flat = pl.BlockSpec((tr, c), lambda i, chip_ref: (i, 0))
    return _call(body, "reduce_adam", (r // tr,),
                 [pl.BlockSpec((1, tr, c), lambda i, chip_ref: (chip_ref[0], i, 0)),
                  pl.BlockSpec((3, tr, c), lambda i, chip_ref: (0, i, 0)), flat, flat, flat],
                 [flat] * 4, [_sds((r, c))] * 4, prefetch=1)(chip, y, q, w, m, v)


def _place():
    return lax.axis_index("x"), lax.axis_index("y"), lax.axis_index("c")


def _all_gather(xs, name, space):
    n = len(xs)

    def body(*refs):
        x_refs, o_refs = refs[:n], refs[n:2 * n]
        send_sems, recv_sems, local_sems = refs[2 * n:]
        x, y, c = _place()
        me, sibling = (x, y, c), (x, y, 1 - c)
        chips = [(1 - x, y), (x, 1 - y), (1 - x, 1 - y)]

        def blk(a, p):
            return o_refs[a].at[4 * p[0] + 2 * p[1] + p[2]]

        def copy(a, k, block, to, src=None):
            return pltpu.make_async_remote_copy(
                src_ref=blk(a, block) if src is None else src, dst_ref=blk(a, block),
                send_sem=send_sems.at[a, k], recv_sem=recv_sems.at[a, k], device_id=to, device_id_type=MESH)

        mine = [pltpu.make_async_copy(x_refs[a], blk(a, me), local_sems.at[a]) for a in range(n)]
        for cp in mine:
            cp.start()
        first = []
        for a in range(n):
            first.append(copy(a, 0, me, sibling, src=x_refs[a]))
            first += [copy(a, 1 + j, me, (*chip, c), src=x_refs[a]) for j, chip in enumerate(chips)]
        for cp in first:
            cp.start()
        passed = []
        for j, chip in enumerate(chips):
            for a in range(n):
                copy(a, 1 + j, (*chip, c), me).wait_recv()
                cp = copy(a, 4 + j, (*chip, c), sibling)
                cp.start()
                passed.append(cp)
        for a in range(n):
            copy(a, 0, sibling, me).wait_recv()
        for j, chip in enumerate(chips):
            for a in range(n):
                copy(a, 4 + j, (*chip, 1 - c), me).wait_recv()
        for cp in first + passed:
            cp.wait_send()
        for cp in mine:
            cp.wait()

    spec = pl.BlockSpec(memory_space=space)
    return pl.pallas_call(
        body, name=name, out_shape=[_sds((N_DEV,) + a.shape, a.dtype) for a in xs],
        in_specs=[spec] * n, out_specs=[spec] * n,
        scratch_shapes=[pltpu.SemaphoreType.DMA((n, 7)), pltpu.SemaphoreType.DMA((n, 7)),
                        pltpu.SemaphoreType.DMA((n,))])(*xs)


def _pair_exchange(xs):
    n = len(xs)

    def body(*refs):
        x_refs, o_refs = refs[:n], refs[n:2 * n]
        send_sems, recv_sems = refs[2 * n:]
        x, y, c = _place()
        copies = [pltpu.make_async_remote_copy(
            src_ref=x_refs[a].at[2 * q + (1 - c)], dst_ref=o_refs[a].at[q], send_sem=send_sems.at[a, q],
            recv_sem=recv_sems.at[a, q], device_id=(x, y, 1 - c), device_id_type=MESH)
            for a in range(n) for q in range(4)]
        for cp in copies:
            cp.start()
        for cp in copies:
            cp.wait()

    spec = pl.BlockSpec(memory_space=pl.ANY)
    return pl.pallas_call(
        body, name="pair_exchange", out_shape=[_sds((4,) + a.shape[1:], a.dtype) for a in xs],
        in_specs=[spec] * n, out_specs=[spec] * n,
        scratch_shapes=[pltpu.SemaphoreType.DMA((n, 4)), pltpu.SemaphoreType.DMA((n, 4))])(*xs)


def _chip_exchange(ys):
    n = len(ys)

    def body(*refs):
        y_refs, o_refs = refs[:n], refs[n:2 * n]
        send_sems, recv_sems = refs[2 * n:]
        x, y, c = _place()
        chips = [(1 - x, y), (x, 1 - y), (1 - x, 1 - y)]
        copies = [pltpu.make_async_remote_copy(
            src_ref=y_refs[a].at[2 * chip[0] + chip[1]], dst_ref=o_refs[a].at[r], send_sem=send_sems.at[a, r],
            recv_sem=recv_sems.at[a, r], device_id=(*chip, c), device_id_type=MESH)
            for a in range(n) for r, chip in enumerate(chips)]
        for cp in copies:
            cp.start()
        for cp in copies:
            cp.wait()

    spec = pl.BlockSpec(memory_space=pl.ANY)
    return pl.pallas_call(
        body, name="chip_exchange", out_shape=[_sds((3,) + a.shape[1:], a.dtype) for a in ys],
        in_specs=[spec] * n, out_specs=[spec] * n,
        scratch_shapes=[pltpu.SemaphoreType.DMA((n, 3)), pltpu.SemaphoreType.DMA((n, 3))])(*ys)


def _pack(arrs):
    flat = jnp.concatenate([a.reshape(-1).astype(F32) for a in arrs])
    pad = (-flat.shape[0]) % (8 * LANES)
    return jnp.pad(flat, (0, pad)).reshape(-1, LANES)


def _unpack(slab, shapes):
    flat = slab.reshape(-1)
    out, off = [], 0
    for shp in shapes:
        size = 1
        for d in shp:
            size *= d
        out.append(flat[off:off + size].reshape(shp))
        off += size
    return out


def _dense_blocks(w):
    eye = jnp.eye(LRU_BLOCKS, dtype=w.dtype)
    return (eye[:, None, :, None] * w[:, :, None, :]).reshape(LRU_W, LRU_W)


def _diag_blocks(dense):
    return jnp.stack([dense[g * LRU_BLOCK:(g + 1) * LRU_BLOCK, g * LRU_BLOCK:(g + 1) * LRU_BLOCK]
                      for g in range(LRU_BLOCKS)])


def _alpha_lanes(v):
    return jnp.zeros((1, BA_PAD), F32).at[0, HEADS:2 * HEADS].set(v)


def _local_step(x, target, mod, p):
    nl = mod.shape[0]
    row = lambda v: v.reshape(1, -1)
    saved = []
    xc = x
    for l in range(nl):
        mv = [row(mod[l, k * D_MODEL:(k + 1) * D_MODEL]) for k in range(N_MOD)]
        sh1, sc1, g1, sh2, sc2, g2 = mv
        nw1, nw2 = row(p["norm_mix_w"][l]), row(p["norm_mlp_w"][l])
        wa, wx = _dense_blocks(p["lru_gate_a_w"][l]).astype(BF16), _dense_blocks(p["lru_gate_x_w"][l]).astype(BF16)
        lru_args = (p["lru_conv_w"][l], row(p["lru_conv_b"][l]), wa, wx, row(p["lru_gate_a_b"][l]),
                    row(p["lru_gate_x_b"][l]), row(p["lru_lambda"][l]), row(p["lru_norm_w"][l]))
        gdn_args = (p["gdn_conv_w"][l], _alpha_lanes(p["gdn_a_log"][l]), _alpha_lanes(p["gdn_dt_bias"][l]))
        gnw = row(p["gdn_norm_w"][l])
        proj, ba = _inproj_fwd(xc, nw1, sc1, sh1, p["win"][l], p["wba"][l])
        ol, hs = _lru_fwd(proj, *lru_args)
        prep = _gdn_prep_fwd(proj, ba, *gdn_args)
        og, st = _gdn_scan_fwd(prep, proj, gnw)
        x1, mix, ff, x2 = _out_mlp_fwd(ol, og, xc, p["wo"][l], g1, nw2, sc2, sh2, g2, p["wup"][l], p["wdn"][l])
        saved.append(dict(x=xc, mv=mv, nw1=nw1, nw2=nw2, lru_args=lru_args, gdn_args=gdn_args, gnw=gnw, proj=proj,
                          ba=ba, ol=ol, hs=hs, prep=prep, og=og, st=st, x1=x1, mix=mix, ff=ff))
        xc = x2

    dx, frows = _final_fwd_bwd(xc, target, row(p["final_norm_w"]))
    loss_part = frows[1, 0]
    small = {k: [None] * nl for k in ("norm_mix_w", "norm_mlp_w", "lru_conv_w", "lru_conv_b", "lru_gate_a_w",
                                      "lru_gate_a_b", "lru_gate_x_w", "lru_gate_x_b", "lru_lambda", "lru_norm_w",
                                      "gdn_conv_w", "gdn_a_log", "gdn_dt_bias", "gdn_norm_w")}
    big = {k: [None] * nl for k in ("w_in", "w_out", "w_up", "w_down")}
    dmod = [None] * nl
    for l in reversed(range(nl)):
        sv = saved[l]
        sh1, sc1, g1, sh2, sc2, g2 = sv["mv"]
        act, dup, h2b, dffb, dx1, rows2 = _mlp_bwd(dx, sv["x1"], sv["ff"], sv["nw2"], sc2, sh2, g2, p["wup"][l],
                                                   p["wdn"][l])
        big["w_up"][l] = _tn_matmul(h2b, dup, "grad_w_up", blocked=True)
        big["w_down"][l] = _tn_matmul(act, dffb, "grad_w_down")
        dmix, dol, dog, rows1 = _outproj_bwd(dx1, sv["mix"], g1, p["wo"][l])
        cat = jnp.concatenate([sv["ol"], sv["og"]], axis=1).astype(BF16)
        big["w_out"][l] = _tn_matmul(cat, dmix, "grad_w_out")
        dpl, dwa, dwx, lrows = _lru_bwd(dol, sv["proj"], sv["hs"], *sv["lru_args"])
        *cts, dpz, gnrow = _gdn_scan_bwd(dog, sv["prep"], sv["st"], sv["proj"], sv["gnw"])
        dpq, dba, dcw, dpar = _gdn_prep_bwd(cts, sv["proj"], sv["ba"], *sv["gdn_args"])
        dx, hb, rows0 = _inproj_bwd(dpl, dpq, dpz, dba, sv["x"], dx1, sv["nw1"], sc1, sh1, p["win"][l], p["wba"][l])
        dproj = jnp.concatenate([dpl, dpq, dpz, dba], axis=1)
        big["w_in"][l] = _tn_matmul(hb, dproj, "grad_w_in")[:, :IN_COLS]
        dmod[l] = jnp.concatenate([rows0[0], rows0[1], rows1[0], rows2[0], rows2[1], rows2[2]])
        small["norm_mix_w"][l], small["norm_mlp_w"][l] = rows0[2], rows2[3]
        small["lru_conv_w"][l], small["lru_conv_b"][l] = lrows[8:8 + CONV_K], lrows[0]
        small["lru_gate_a_w"][l], small["lru_gate_x_w"][l] = _diag_blocks(dwa), _diag_blocks(dwx)
        small["lru_gate_a_b"][l], small["lru_gate_x_b"][l] = lrows[1], lrows[2]
        small["lru_lambda"][l], small["lru_norm_w"][l] = lrows[3], lrows[4]
        small["gdn_conv_w"][l] = dcw
        small["gdn_a_log"][l], small["gdn_dt_bias"][l] = dpar[0, HEADS:2 * HEADS], dpar[1, HEADS:2 * HEADS]
        small["gdn_norm_w"][l] = gnrow[0]
    small = {k: jnp.stack(v) for k, v in small.items()}
    small["final_norm_w"] = frows[0]
    big = {k: jnp.stack(v) for k, v in big.items()}
    return loss_part, dx, big, small, jnp.stack(dmod)


SMALL_REPLICATED = ("norm_mix_w", "norm_mlp_w", "b_mod", "lru_conv_b", "lru_gate_a_w", "lru_gate_a_b", "lru_gate_x_w",
                    "lru_gate_x_b", "lru_lambda", "lru_norm_w", "gdn_a_log", "gdn_dt_bias", "gdn_norm_w",
                    "final_norm_w")
SMALL_SHARDED = ("lru_conv_w", "gdn_conv_w")
WEIGHT_ORDER = ("norm_mix_w", "norm_mlp_w", "w_mod", "b_mod", "w_in", "lru_conv_w", "lru_conv_b", "lru_gate_a_w",
                "lru_gate_a_b", "lru_gate_x_w", "lru_gate_x_b", "lru_lambda", "lru_norm_w", "gdn_conv_w", "gdn_a_log",
                "gdn_dt_bias", "gdn_norm_w", "w_out", "w_up", "w_down", "final_norm_w")


def kernel(x, c, norm_mix_w, norm_mlp_w, w_mod, b_mod, w_in, lru_conv_w, lru_conv_b, lru_gate_a_w, lru_gate_a_b, lru_gate_x_w, lru_gate_x_b, lru_lambda, lru_norm_w, gdn_conv_w, gdn_a_log, gdn_dt_bias, gdn_norm_w, w_out, w_up, w_down, final_norm_w, loss_target, m_norm_mix_w, m_norm_mlp_w, m_w_mod, m_b_mod, m_w_in, m_lru_conv_w, m_lru_conv_b, m_lru_gate_a_w, m_lru_gate_a_b, m_lru_gate_x_w, m_lru_gate_x_b, m_lru_lambda, m_lru_norm_w, m_gdn_conv_w, m_gdn_a_log, m_gdn_dt_bias, m_gdn_norm_w, m_w_out, m_w_up, m_w_down, m_final_norm_w, v_norm_mix_w, v_norm_mlp_w, v_w_mod, v_b_mod, v_w_in, v_lru_conv_w, v_lru_conv_b, v_lru_gate_a_w, v_lru_gate_a_b, v_lru_gate_x_w, v_lru_gate_x_b, v_lru_lambda, v_lru_norm_w, v_gdn_conv_w, v_gdn_a_log, v_gdn_dt_bias, v_gdn_norm_w, v_w_out, v_w_up, v_w_down, v_final_norm_w):
    args = dict(locals())
    w = {k: args[k] for k in WEIGHT_ORDER}
    mom = {k: args["m_" + k] for k in WEIGHT_ORDER}
    var = {k: args["v_" + k] for k in WEIGHT_ORDER}
    nl = w_in.shape[0]
    px, py, pc = _place()
    me = 4 * px + 2 * py + pc
    core = jnp.reshape(pc, (1,)).astype(jnp.int32)
    chip = jnp.reshape(2 * px + py, (1,)).astype(jnp.int32)

    shapes0 = [c.shape, lru_conv_w.shape, gdn_conv_w.shape]
    (g0,) = _all_gather([_pack([c, lru_conv_w, gdn_conv_w])], "gather_cond", pltpu.VMEM)
    per_dev = [_unpack(g0[d], shapes0) for d in range(N_DEV)]
    c_all = jnp.concatenate([pd[0] for pd in per_dev], axis=0)
    lru_conv_full = jnp.concatenate([pd[1] for pd in per_dev], axis=-1)
    gdn_conv_full = jnp.concatenate([pd[2] for pd in per_dev], axis=-1)

    cols = w_mod.shape[2]
    bmod_cols = lax.dynamic_slice_in_dim(b_mod, me * cols, cols, axis=1).reshape(nl, 1, cols)
    mod_cols = _mod_local(c_all, w_mod, bmod_cols)
    (g1,) = _all_gather([mod_cols.reshape(nl * N_DEV, cols)], "gather_mod", pltpu.VMEM)
    g1 = g1.reshape(N_DEV, nl, N_DEV, cols)
    mod = jnp.transpose(lax.dynamic_index_in_dim(g1, me, axis=2, keepdims=False), (1, 0, 2)).reshape(nl, N_DEV * cols)

    gin, gout, gup, gdn = _all_gather([w_in.astype(BF16), w_out.astype(BF16), w_up.astype(BF16),
                                       w_down.astype(BF16)], "gather_weights", pl.ANY)
    win_full = jnp.transpose(gin, (1, 2, 0, 3)).reshape(nl, D_MODEL, IN_COLS)
    p = dict(w)
    p["win"] = win_full[:, :, :IN_MAIN]
    p["wba"] = jnp.pad(win_full[:, :, IN_MAIN:], ((0, 0), (0, 0), (0, BA_PAD - (IN_COLS - IN_MAIN))))
    p["wo"] = jnp.transpose(gout, (1, 0, 2, 3)).reshape(nl, D_MODEL, D_MODEL)
    p["wup"] = jnp.transpose(gup, (1, 0, 2, 3))
    p["wdn"] = jnp.transpose(gdn, (1, 0, 2, 3))
    p["lru_conv_w"], p["gdn_conv_w"] = lru_conv_full, gdn_conv_full

    loss_part, grad_x, big, small, dmod = _local_step(x[0], loss_target[0], mod, p)
    loss = lax.psum(loss_part, MESH_AXES)

    small_names = sorted(small)
    slab = _pack([dmod] + [small[k] for k in small_names])
    (gs,) = _all_gather([slab], "gather_small_grads", pltpu.VMEM)
    dmod_all = gs.reshape(N_DEV, -1)[:, :dmod.size].reshape(N_DEV, nl, N_MOD * D_MODEL)
    summed = _unpack(_sum_devices(gs), [dmod.shape] + [small[k].shape for k in small_names])
    grads = dict(zip(small_names, summed[1:]))
    grads["b_mod"] = summed[0]
    for k, width in (("lru_conv_w", LRU_W // N_DEV), ("gdn_conv_w", 3 * GDN_W // N_DEV)):
        grads[k] = lax.dynamic_slice_in_dim(grads[k], me * width, width, axis=2)
    names = SMALL_REPLICATED + SMALL_SHARDED
    shapes = [w[k].shape for k in names]
    dl, nm, nv = _adam_flat(_pack([w[k] for k in names]), _pack([grads[k] for k in names]),
                            _pack([mom[k] for k in names]), _pack([var[k] for k in names]))
    delta = dict(zip(names, _unpack(dl, shapes)))
    new_m = dict(zip(names, _unpack(nm, shapes)))
    new_v = dict(zip(names, _unpack(nv, shapes)))

    dmod_cols = jnp.transpose(lax.dynamic_slice_in_dim(dmod_all, me * cols, cols, axis=2), (1, 0, 2))
    grads["w_mod"], delta["w_mod"], new_m["w_mod"], new_v["w_mod"] = _wmod_update(
        c_all, dmod_cols, w_mod, m_w_mod, v_w_mod)

    gw_in = jnp.transpose(big["w_in"].reshape(nl, D_MODEL, N_DEV, IN_COLS // N_DEV), (2, 0, 1, 3))
    gw_out = jnp.transpose(big["w_out"].reshape(nl, N_DEV, D_MODEL // N_DEV, D_MODEL), (1, 0, 2, 3))
    gw_up = jnp.transpose(big["w_up"], (1, 0, 2, 3))
    gw_down = jnp.transpose(big["w_down"].reshape(nl, N_DEV, D_FF // N_DEV, D_MODEL), (1, 0, 2, 3))
    order = ("w_in", "w_out", "w_up", "w_down")
    xs = [a.reshape(N_DEV, a.shape[1] * a.shape[2], a.shape[3]) for a in (gw_in, gw_out, gw_up, gw_down)]
    ps = _pair_exchange(xs)
    ys = [_pair_add(xk, pk, core) for xk, pk in zip(xs, ps)]
    qs = _chip_exchange(ys)
    for k, yk, qk in zip(order, ys, qs):
        flat = lambda a: a.reshape(-1, a.shape[-1])
        outs = _reduce_adam(yk, qk, chip, flat(w[k]), flat(mom[k]), flat(var[k]))
        grads[k], delta[k], new_m[k], new_v[k] = (o.reshape(w[k].shape) for o in outs)

    return (loss, grad_x[None], *[grads[k] for k in WEIGHT_ORDER], *[delta[k] for k in WEIGHT_ORDER],
            *[new_m[k] for k in WEIGHT_ORDER], *[new_v[k] for k in WEIGHT_ORDER])
```

```python
import functools

import jax
import jax.numpy as jnp
from jax import lax
from jax.experimental import pallas as pl
from jax.experimental.pallas import tpu as pltpu

F32 = jnp.float32
BF16 = jnp.bfloat16

D_MODEL = 1024
LRU_W = 512
LRU_BLOCKS = 8
LRU_BLOCK = 64
LRU_C = 8.0
GDN_W = 512
HEADS = 4
HEAD_DIM = 128
CHUNK = 64
STACK = HEADS * CHUNK
CONV_K = 4
D_FF = 4096
N_MOD = 6
IN_COLS = 3080
IN_MAIN = 3072
BA_PAD = 128
EPS = 1e-6
N_DEV = 8
HALO = 8
LANES = 128
ADAM_LR, ADAM_B1, ADAM_B2, ADAM_EPS, ADAM_WD, ADAM_STEP = 0.001, 0.9, 0.999, 1e-08, 0.01, 10
MESH_AXES = ("x", "y", "c")
MESH = pl.DeviceIdType.MESH

NN = (((1,), (0,)), ((), ()))
NT = (((1,), (1,)), ((), ()))
TN = (((0,), (0,)), ((), ()))


def _bdot(a, b, dims=NN):
    return lax.dot_general(a.astype(BF16), b.astype(BF16), dims, preferred_element_type=F32)


def _sdot(a, b, dims=NN):
    ah, bh = a.astype(BF16), b.astype(BF16)
    al, bl = (a - ah.astype(F32)).astype(BF16), (b - bh.astype(F32)).astype(BF16)
    return _bdot(ah, bh, dims) + (_bdot(al, bh, dims) + _bdot(ah, bl, dims))


def _hdot(a, b, dims=NN):
    return lax.dot_general(a, b, dims, precision=lax.Precision.HIGHEST, preferred_element_type=F32)


def _sds(shape, dtype=F32):
    return jax.ShapeDtypeStruct(tuple(shape), dtype)


def _tile(n, t):
    return min(n, t)


def _call(body, name, grid, in_specs, out_specs, out_shape, scratch=(), vmem_mb=48, prefetch=0, aliases=None):
    params = pltpu.CompilerParams(dimension_semantics=("arbitrary",) * len(grid), vmem_limit_bytes=vmem_mb * 2**20)
    if prefetch:
        spec = pltpu.PrefetchScalarGridSpec(num_scalar_prefetch=prefetch, grid=grid, in_specs=in_specs,
                                            out_specs=out_specs, scratch_shapes=list(scratch))
        return pl.pallas_call(body, name=name, grid_spec=spec, out_shape=out_shape, compiler_params=params)
    return pl.pallas_call(body, name=name, grid=grid, in_specs=in_specs, out_specs=out_specs, out_shape=out_shape,
                          scratch_shapes=list(scratch), compiler_params=params, input_output_aliases=aliases or {})


def _tok(t, n, col=0):
    return pl.BlockSpec((t, n), lambda i, *_: (i, col))


def _vec(n):
    return pl.BlockSpec((1, n), lambda *_: (0, 0))


def _whole(a):
    nd = a.ndim
    return pl.BlockSpec(a.shape, lambda *_: (0,) * nd)


def _layer(l, *dims):
    return pl.BlockSpec((1,) + dims, lambda *_: (l,) + (0,) * len(dims))


def _gelu(y):
    c0, c1 = 0.7978845608028654, 0.044715
    return 0.5 * y * (1.0 + jnp.tanh(c0 * (y + c1 * y * y * y)))


def _gelu_grad(y):
    c0, c1 = 0.7978845608028654, 0.044715
    t = jnp.tanh(c0 * (y + c1 * y * y * y))
    return 0.5 * (1.0 + t) + 0.5 * y * (1.0 - t * t) * c0 * (1.0 + 3.0 * c1 * y * y)


def _softplus(v):
    return jnp.maximum(v, 0.0) + jnp.log(1.0 + jnp.exp(-jnp.where(v > 0, v, -v)))


@functools.partial(jax.custom_vjp, nondiff_argnums=(1,))
def _roll_rows(v, s):
    s = s % v.shape[0]
    return pltpu.roll(v, s, axis=0) if s else v


def _roll_rows_fwd(v, s):
    return _roll_rows(v, s), None


def _roll_rows_bwd(s, _, g):
    return (_roll_rows(g, -s),)


_roll_rows.defvjp(_roll_rows_fwd, _roll_rows_bwd)


@jax.custom_vjp
def _drop_halo(v):
    return v[HALO:]


def _drop_halo_fwd(v):
    return v[HALO:], None


def _drop_halo_bwd(_, g):
    return (jnp.concatenate([jnp.zeros((HALO, g.shape[1]), g.dtype), g], axis=0),)


_drop_halo.defvjp(_drop_halo_fwd, _drop_halo_bwd)


@functools.partial(jax.custom_vjp, nondiff_argnums=(1, 2))
def _split(v, n, axis):
    w = v.shape[axis] // n
    return tuple(lax.slice_in_dim(v, k * w, (k + 1) * w, axis=axis) for k in range(n))


def _split_fwd(v, n, axis):
    return _split(v, n, axis), None


def _split_bwd(n, axis, _, gs):
    return (jnp.concatenate(list(gs), axis=axis),)


_split.defvjp(_split_fwd, _split_bwd)


def _conv_taps(xw):
    return [_drop_halo(_roll_rows(xw, CONV_K - 1 - k)) for k in range(CONV_K)]


def _modulated_norm(xv, nw, sc, sh):
    r = lax.rsqrt(jnp.mean(xv * xv, axis=-1, keepdims=True) + EPS)
    n = xv * r * nw
    return n * (1.0 + sc) + sh, n, r


def _modulated_norm_bwd(dh, xv, n, r, nw, sc):
    dn = dh * (1.0 + sc)
    dxn = dn * nw
    dx = r * dxn - xv * (r * r * r) * jnp.mean(dxn * xv, axis=-1, keepdims=True)
    return (dx, jnp.sum(dh, axis=0, keepdims=True), jnp.sum(dh * n, axis=0, keepdims=True),
            jnp.sum(dn * xv * r, axis=0, keepdims=True))


def _inproj_fwd(x, nw, sc, sh, win, wba, l):
    s = x.shape[0]
    t = _tile(s, 256)

    def body(x_ref, nw_ref, sc_ref, sh_ref, win_ref, wba_ref, proj_ref, ba_ref):
        h, _, _ = _modulated_norm(x_ref[...], nw_ref[...], sc_ref[...], sh_ref[...])
        hb = h.astype(BF16)
        proj_ref[...] = _bdot(hb, win_ref[0])
        ba_ref[...] = _bdot(hb, wba_ref[0])

    return _call(body, "inproj_fwd", (s // t,),
                 [_tok(t, D_MODEL), _vec(D_MODEL), _vec(D_MODEL), _vec(D_MODEL), _layer(l, D_MODEL, IN_MAIN),
                  _layer(l, D_MODEL, BA_PAD)],
                 [_tok(t, IN_MAIN), _tok(t, BA_PAD)],
                 [_sds((s, IN_MAIN)), _sds((s, BA_PAD))])(x, nw, sc, sh, win, wba)


def _inproj_bwd(dpl, dpq, dpz, dba, x, dx1, nw, sc, sh, win, wba, l):
    s = x.shape[0]
    t = _tile(s, 256)

    def body(dpl_ref, dpq_ref, dpz_ref, dba_ref, x_ref, dx1_ref, nw_ref, sc_ref, sh_ref, win_ref, wba_ref,
             dx_ref, hb_ref, acc_ref):
        @pl.when(pl.program_id(0) == 0)
        def _():
            acc_ref[...] = jnp.zeros_like(acc_ref)

        dh = (_bdot(dpl_ref[...], win_ref[0, :, 0:2 * LRU_W], NT)
              + _bdot(dpq_ref[...], win_ref[0, :, 2 * LRU_W:2 * LRU_W + 3 * GDN_W], NT)
              + _bdot(dpz_ref[...], win_ref[0, :, 2 * LRU_W + 3 * GDN_W:IN_MAIN], NT)
              + _bdot(dba_ref[...], wba_ref[0], NT))
        xv = x_ref[...]
        h, n, r = _modulated_norm(xv, nw_ref[...], sc_ref[...], sh_ref[...])
        hb_ref[...] = h.astype(BF16)
        dx, dsh, dsc, dnw = _modulated_norm_bwd(dh, xv, n, r, nw_ref[...], sc_ref[...])
        dx_ref[...] = dx1_ref[...] + dx
        acc_ref[0:1, :] += dsh
        acc_ref[1:2, :] += dsc
        acc_ref[2:3, :] += dnw

    return _call(body, "inproj_bwd", (s // t,),
                 [_tok(t, 2 * LRU_W), _tok(t, 3 * GDN_W), _tok(t, GDN_W), _tok(t, BA_PAD), _tok(t, D_MODEL),
                  _tok(t, D_MODEL), _vec(D_MODEL), _vec(D_MODEL), _vec(D_MODEL), _layer(l, D_MODEL, IN_MAIN),
                  _layer(l, D_MODEL, BA_PAD)],
                 [_tok(t, D_MODEL), _tok(t, D_MODEL), pl.BlockSpec((8, D_MODEL), lambda i: (0, 0))],
                 [_sds((s, D_MODEL)), _sds((s, D_MODEL), BF16), _sds((8, D_MODEL))])(
                     dpl, dpq, dpz, dba, x, dx1, nw, sc, sh, win, wba)


def _lru_gates(xw, cw_rows, cb, wa, wx, gab, gxb, lam):
    taps = _conv_taps(xw)
    xr = cb + cw_rows[0] * taps[0] + cw_rows[1] * taps[1] + cw_rows[2] * taps[2] + cw_rows[3] * taps[3]
    xb = xr.astype(BF16)
    r = jax.nn.sigmoid(_bdot(xb, wa) + gab)
    i = jax.nn.sigmoid(_bdot(xb, wx) + gxb)
    z = jnp.exp(-jnp.where(lam > 0, lam, -lam))
    w1 = 1.0 + z
    log1p_z = jnp.where(w1 == 1.0, z, jnp.log(w1) * z / (w1 - 1.0))
    ls = jnp.minimum(lam, 0.0) - log1p_z
    la = LRU_C * r * ls
    a = jnp.exp(la)
    x2 = 2.0 * la
    u = jnp.exp(x2)
    mm_raw = jnp.where(u == 1.0, -x2,
                       jnp.where(x2 < -30.0, 1.0, (1.0 - u) * x2 / jnp.log(jnp.maximum(u, 1e-30))))
    mult = jnp.sqrt(jnp.maximum(mm_raw, 1e-12))
    return dict(taps=taps, xr=xr, r=r, i=i, ls=ls, a=a, mm_raw=mm_raw, mult=mult)


def _lru_specs(s, t, tile_of):
    nh = t // HALO
    xl = pl.BlockSpec((t, LRU_W), lambda i: (tile_of(i), 0))
    yl = pl.BlockSpec((t, LRU_W), lambda i: (tile_of(i), 1))
    hx = pl.BlockSpec((HALO, LRU_W), lambda i: (jnp.maximum(tile_of(i) * nh - 1, 0), 0))
    return xl, yl, hx


def _lru_fwd(proj, cw, cb, wa, wx, gab, gxb, lam, lnw):
    s = proj.shape[0]
    t = _tile(s, 256)
    xl, yl, hx = _lru_specs(s, t, lambda i: i)

    def body(xl_ref, yl_ref, hx_ref, cw_ref, cb_ref, wa_ref, wx_ref, gab_ref, gxb_ref, lam_ref, lnw_ref,
             out_ref, h_ref, a_s, b_s, hc):
        i = pl.program_id(0)

        @pl.when(i == 0)
        def _():
            hc[...] = jnp.zeros_like(hc)

        halo = jnp.where(i > 0, hx_ref[...], 0.0)
        xw = jnp.concatenate([halo, xl_ref[...]], axis=0)
        g = _lru_gates(xw, [cw_ref[k:k + 1, :] for k in range(CONV_K)], cb_ref[...], wa_ref[...], wx_ref[...],
                       gab_ref[...], gxb_ref[...], lam_ref[...])
        a_s[...] = g["a"]
        b_s[...] = g["mult"] * (g["i"] * g["xr"])

        def step(k, h):
            h = a_s[pl.ds(k, 1), :] * h + b_s[pl.ds(k, 1), :]
            h_ref[pl.ds(k, 1), :] = h
            return h

        hc[...] = lax.fori_loop(0, t, step, hc[...], unroll=8)
        m = h_ref[...] * _gelu(yl_ref[...])
        out_ref[...] = m * lax.rsqrt(jnp.mean(m * m, axis=-1, keepdims=True) + EPS) * lnw_ref[...]

    return _call(body, "lru_fwd", (s // t,),
                 [xl, yl, hx, _whole(cw), _vec(LRU_W), _whole(wa), _whole(wx)] + [_vec(LRU_W)] * 4,
                 [_tok(t, LRU_W), _tok(t, LRU_W)],
                 [_sds((s, LRU_W)), _sds((s, LRU_W))],
                 scratch=[pltpu.VMEM((t, LRU_W), F32), pltpu.VMEM((t, LRU_W), F32), pltpu.VMEM((1, LRU_W), F32)])(
                     proj, proj, proj, cw, cb, wa, wx, gab, gxb, lam, lnw)


def _lru_bwd(dout, proj, hs, cw, cb, wa, wx, gab, gxb, lam, lnw):
    s = proj.shape[0]
    t = _tile(s, 256)
    nt = s // t
    rev = lambda i: nt - 1 - i
    xl, yl, hx = _lru_specs(s, t, rev)
    nh = t // HALO
    tk = pl.BlockSpec((t, LRU_W), lambda i: (rev(i), 0))
    hh = pl.BlockSpec((HALO, LRU_W), lambda i: (jnp.maximum(rev(i) * nh - 1, 0), 0))

    def body(do_ref, xl_ref, yl_ref, hx_ref, h_ref, hh_ref, cw_ref, cb_ref, wa_ref, wx_ref, gab_ref, gxb_ref,
             lam_ref, lnw_ref, dp_ref, dwa_ref, dwx_ref, rows_ref, dh_s, dhd_s, carry, dxr_next):
        i = pl.program_id(0)
        first_tile = rev(i) == 0

        @pl.when(i == 0)
        def _():
            carry[...] = jnp.zeros_like(carry)
            dxr_next[...] = jnp.zeros_like(dxr_next)
            dwa_ref[...] = jnp.zeros_like(dwa_ref)
            dwx_ref[...] = jnp.zeros_like(dwx_ref)
            rows_ref[...] = jnp.zeros_like(rows_ref)

        halo = jnp.where(first_tile, 0.0, hx_ref[...])
        xw = jnp.concatenate([halo, xl_ref[...]], axis=0)
        cw_rows = [cw_ref[k:k + 1, :] for k in range(CONV_K)]
        lam_v = lam_ref[...]
        g = _lru_gates(xw, cw_rows, cb_ref[...], wa_ref[...], wx_ref[...], gab_ref[...], gxb_ref[...], lam_v)
        a, r, gi, xr, mult = g["a"], g["r"], g["i"], g["xr"], g["mult"]
        hv = h_ref[...]
        yv = yl_ref[...]
        gl = _gelu(yv)
        m = hv * gl
        rn = lax.rsqrt(jnp.mean(m * m, axis=-1, keepdims=True) + EPS)
        dov = do_ref[...]
        dmn = dov * lnw_ref[...]
        rows_ref[4:5, :] += jnp.sum(dov * m * rn, axis=0, keepdims=True)
        dm = rn * dmn - m * (rn * rn * rn) * jnp.mean(dmn * m, axis=-1, keepdims=True)
        dhd_s[...] = dm * gl
        dy = dm * hv * _gelu_grad(yv)
        dh_s[...] = a

        def step(k, c):
            row = t - 1 - k
            d = dhd_s[pl.ds(row, 1), :] + c
            c = dh_s[pl.ds(row, 1), :] * d
            dh_s[pl.ds(row, 1), :] = d
            return c

        carry[...] = lax.fori_loop(0, t, step, carry[...], unroll=8)
        dH = dh_s[...]
        hprev_halo = jnp.where(first_tile, 0.0, hh_ref[...])
        hprev = _drop_halo(_roll_rows(jnp.concatenate([hprev_halo, hv], axis=0), 1))
        da = dH * hprev
        dmult = dH * gi * xr
        di = dH * mult * xr
        dxr = dH * mult * gi
        dla = jnp.where(g["mm_raw"] > 1e-12, dmult * (0.5 / mult) * (-2.0 * a * a), 0.0) + da * a
        dr = dla * (LRU_C * g["ls"])
        sig_neg = jax.nn.sigmoid(-lam_v)
        rows_ref[3:4, :] += jnp.sum(dla * (LRU_C * r), axis=0, keepdims=True) * sig_neg
        drp = dr * r * (1.0 - r)
        dip = di * gi * (1.0 - gi)
        rows_ref[1:2, :] += jnp.sum(drp, axis=0, keepdims=True)
        rows_ref[2:3, :] += jnp.sum(dip, axis=0, keepdims=True)
        xb = xr.astype(BF16)
        drb = drp.astype(BF16)
        dib = dip.astype(BF16)
        dwa_ref[...] += _bdot(xb, drb, TN)
        dwx_ref[...] += _bdot(xb, dib, TN)
        dxr = dxr + _bdot(drb, wa_ref[...], NT) + _bdot(dib, wx_ref[...], NT)
        rows_ref[0:1, :] += jnp.sum(dxr, axis=0, keepdims=True)
        ext = jnp.concatenate([dxr, dxr_next[...]], axis=0)
        dx = cw_rows[CONV_K - 1] * dxr
        for k in range(CONV_K - 1):
            dx = dx + cw_rows[k] * _roll_rows(ext, -(CONV_K - 1 - k))[0:t]
        for k in range(CONV_K):
            rows_ref[8 + k:9 + k, :] += jnp.sum(dxr * g["taps"][k], axis=0, keepdims=True)
        dxr_next[...] = dxr[0:HALO]
        dp_ref[...] = jnp.concatenate([dx, dy], axis=1).astype(BF16)

    acc = lambda shape: pl.BlockSpec(shape, lambda i: (0, 0))
    return _call(body, "lru_bwd", (nt,),
                 [tk, xl, yl, hx, tk, hh, _whole(cw), _vec(LRU_W), _whole(wa), _whole(wx)] + [_vec(LRU_W)] * 4,
                 [pl.BlockSpec((t, 2 * LRU_W), lambda i: (rev(i), 0)), acc((LRU_W, LRU_W)), acc((LRU_W, LRU_W)),
                  acc((16, LRU_W))],
                 [_sds((s, 2 * LRU_W), BF16), _sds((LRU_W, LRU_W)), _sds((LRU_W, LRU_W)), _sds((16, LRU_W))],
                 scratch=[pltpu.VMEM((t, LRU_W), F32), pltpu.VMEM((t, LRU_W), F32), pltpu.VMEM((1, LRU_W), F32),
                          pltpu.VMEM((HALO, LRU_W), F32)])(
                     dout, proj, proj, proj, hs, hs, cw, cb, wa, wx, gab, gxb, lam, lnw)


def _gdn_masks():
    row = lax.broadcasted_iota(jnp.int32, (STACK, STACK), 0)
    col = lax.broadcasted_iota(jnp.int32, (STACK, STACK), 1)
    same = (row // CHUNK) == (col // CHUNK)
    return jnp.stack([(same & (col <= row)).astype(F32), (same & (col < row)).astype(F32), (row == col).astype(F32)])


def _conv_silu(xw, rows):
    taps = _conv_taps(xw)
    y = rows[0] * taps[0] + rows[1] * taps[1] + rows[2] * taps[2] + rows[3] * taps[3]
    return y * jax.nn.sigmoid(y)


def _split3(v):
    hi = v.astype(BF16)
    r1 = v - hi.astype(F32)
    mid = r1.astype(BF16)
    return hi, mid, (r1 - mid.astype(F32)).astype(BF16)


def _mask_dot_raw(mask, v, dims):
    parts = _split3(v)
    d = lambda p: lax.dot_general(mask, p, dims, preferred_element_type=F32)
    return d(parts[0]) + (d(parts[1]) + d(parts[2]))


@jax.custom_vjp
def _mask_dot(mask, v):
    return _mask_dot_raw(mask, v, NN)


def _mask_dot_fwd(mask, v):
    return _mask_dot_raw(mask, v, NN), mask


def _mask_dot_bwd(mask, ct):
    return jnp.zeros_like(mask), _mask_dot_raw(mask, ct, TN)


_mask_dot.defvjp(_mask_dot_fwd, _mask_dot_bwd)


def _solve_impl(n, rhs, eye):
    tinv = eye + n
    p = n
    for _ in range(5):
        p = _bdot(p, p)
        tinv = tinv + _bdot(tinv, p)
    x0 = _bdot(tinv, rhs)
    return x0 + _bdot(tinv, rhs - x0 + _sdot(n, x0)), tinv


@jax.custom_vjp
def _unit_lower_solve(n, rhs, eye):
    return _solve_impl(n, rhs, eye)[0]


def _unit_lower_solve_fwd(n, rhs, eye):
    x, tinv = _solve_impl(n, rhs, eye)
    return x, (n, tinv, x, eye)


def _unit_lower_solve_bwd(res, ct):
    n, tinv, x, eye = res
    y0 = _bdot(tinv, ct, TN)
    y = y0 + _bdot(tinv, ct - y0 + _sdot(n, y0, TN), TN)
    return _bdot(y, x, NT), y, jnp.zeros_like(eye)


_unit_lower_solve.defvjp(_unit_lower_solve_fwd, _unit_lower_solve_bwd)


def _gdn_prep(xq, xk, xv, ba, cwq, cwk, cwv, pa, pd, masks):
    lower, strict, eye = masks[0], masks[1], masks[2]
    lower_b = lower.astype(BF16)
    lane = lax.broadcasted_iota(jnp.int32, (CHUNK, LANES), 1)
    q = jnp.concatenate(_split(_conv_silu(xq, cwq), HEADS, 1), axis=0)
    k = jnp.concatenate(_split(_conv_silu(xk, cwk), HEADS, 1), axis=0)
    v = jnp.concatenate(_split(_conv_silu(xv, cwv), HEADS, 1), axis=0)
    qn = q * lax.rsqrt(jnp.sum(q * q, axis=-1, keepdims=True) + 1e-6) * (HEAD_DIM ** -0.5)
    kn = k * lax.rsqrt(jnp.sum(k * k, axis=-1, keepdims=True) + 1e-6)
    beta_f = jax.nn.sigmoid(ba)
    g_f = -jnp.exp(pa) * _softplus(ba + pd)

    def col(a, j):
        return jnp.broadcast_to(jnp.sum(jnp.where(lane == j, a, 0.0), axis=1, keepdims=True), (CHUNK, HEAD_DIM))

    beta = jnp.concatenate([col(beta_f, h) for h in range(HEADS)], axis=0)
    gs = [col(g_f, HEADS + h) for h in range(HEADS)]
    g = jnp.concatenate(gs, axis=0)
    gl = jnp.concatenate([jnp.broadcast_to(jnp.sum(gh, axis=0, keepdims=True), (CHUNK, HEAD_DIM)) for gh in gs], axis=0)
    gc = _mask_dot(lower_b, g)
    gc_rows = jnp.transpose(gc)
    decay = jnp.exp((jnp.concatenate([gc, gc], axis=1) - jnp.concatenate([gc_rows, gc_rows], axis=0)) * lower)
    egc = jnp.exp(gc)
    kb = kn * beta
    n = -(_bdot(kb, kn, NT) * decay * strict)
    u, w = _split(_unit_lower_solve(n, jnp.concatenate([v * beta, kb * egc], axis=1), eye), 2, 1)
    attn = _bdot(qn, kn, NT) * decay * lower
    return u, w, qn * egc, kn * jnp.exp(gl - gc), attn, jnp.exp(gl)


def _gdn_scan(states, u, w, qd, kt, attn, egl, z, nw):
    us, ws, qds, kts, egls = (_split(a, HEADS, 0) for a in (u, w, qd, kt, egl))
    vn = [us[h] - _bdot(ws[h], states[h]) for h in range(HEADS)]
    o = jnp.concatenate([_bdot(qds[h], states[h]) for h in range(HEADS)], axis=0)
    o = o + _bdot(attn, jnp.concatenate(vn, axis=0))
    new = [states[h] * jnp.concatenate([egls[h], egls[h]], axis=0) + _bdot(kts[h], vn[h], TN) for h in range(HEADS)]
    on = o * lax.rsqrt(jnp.mean(o * o, axis=-1, keepdims=True) + EPS) * nw
    return new, on * (z * jax.nn.sigmoid(z))


def _gdn_in_specs(s, chunk_of):
    nh = CHUNK // HALO
    main = [pl.BlockSpec((CHUNK, GDN_W), functools.partial(lambda col, i: (chunk_of(i), col), col))
            for col in (2, 3, 4)]
    halo = [pl.BlockSpec((HALO, GDN_W), functools.partial(lambda col, i: (jnp.maximum(chunk_of(i) * nh - 1, 0), col),
                                                         col)) for col in (2, 3, 4)]
    return main, halo


def _stk(width, chunk_of):
    return pl.BlockSpec((STACK, width), lambda i: (chunk_of(i), 0))


def _gdn_prep_fwd(proj, ba, cw, pa, pd, masks):
    s = proj.shape[0]
    nc = s // CHUNK
    main, halo = _gdn_in_specs(s, lambda i: i)

    def body(xq_ref, xk_ref, xv_ref, hq_ref, hk_ref, hv_ref, ba_ref, cw_ref, pa_ref, pd_ref, mk_ref,
             u_ref, w_ref, qd_ref, kt_ref, attn_ref, egl_ref):
        i = pl.program_id(0)
        xs = [jnp.concatenate([jnp.where(i > 0, h[...], 0.0), m[...]], axis=0)
              for h, m in ((hq_ref, xq_ref), (hk_ref, xk_ref), (hv_ref, xv_ref))]
        rows = [[cw_ref[k:k + 1, j * GDN_W:(j + 1) * GDN_W] for k in range(CONV_K)] for j in range(3)]
        outs = _gdn_prep(xs[0], xs[1], xs[2], ba_ref[...], rows[0], rows[1], rows[2], pa_ref[...], pd_ref[...],
                         [mk_ref[0], mk_ref[1], mk_ref[2]])
        for ref, val in zip((u_ref, w_ref, qd_ref, kt_ref, attn_ref, egl_ref), outs):
            ref[...] = val

    ident = lambda i: i
    return _call(body, "gdn_prep_fwd", (nc,),
                 main + halo + [_tok(CHUNK, BA_PAD), _whole(cw), _vec(BA_PAD), _vec(BA_PAD), _whole(masks)],
                 [_stk(HEAD_DIM, ident)] * 4 + [_stk(STACK, ident), _stk(HEAD_DIM, ident)],
                 [_sds((nc * STACK, HEAD_DIM))] * 4 + [_sds((nc * STACK, STACK)), _sds((nc * STACK, HEAD_DIM))])(
                     proj, proj, proj, proj, proj, proj, ba, cw, pa, pd, masks)


def _gdn_prep_bwd(cts, proj, ba, cw, pa, pd, masks):
    s = proj.shape[0]
    nc = s // CHUNK
    rev = lambda i: nc - 1 - i
    main, halo = _gdn_in_specs(s, rev)

    def body(du_ref, dw_ref, dqd_ref, dkt_ref, dattn_ref, degl_ref, xq_ref, xk_ref, xv_ref, hq_ref, hk_ref, hv_ref,
             ba_ref, cw_ref, pa_ref, pd_ref, mk_ref, dp_ref, dba_ref, dcw_ref, dpar_ref, carry):
        i = pl.program_id(0)
        first_chunk = rev(i) == 0

        @pl.when(i == 0)
        def _():
            carry[...] = jnp.zeros_like(carry)
            dcw_ref[...] = jnp.zeros_like(dcw_ref)
            dpar_ref[...] = jnp.zeros_like(dpar_ref)

        xs = [jnp.concatenate([jnp.where(first_chunk, 0.0, h[...]), m[...]], axis=0)
              for h, m in ((hq_ref, xq_ref), (hk_ref, xk_ref), (hv_ref, xv_ref))]
        rows = [[cw_ref[k:k + 1, j * GDN_W:(j + 1) * GDN_W] for k in range(CONV_K)] for j in range(3)]
        cst = [mk_ref[0], mk_ref[1], mk_ref[2]]
        fn = lambda xq, xk, xv, b, rq, rk, rv, a, d: _gdn_prep(xq, xk, xv, b, rq, rk, rv, a, d, cst)
        _, vjp = jax.vjp(fn, xs[0], xs[1], xs[2], ba_ref[...], rows[0], rows[1], rows[2], pa_ref[...], pd_ref[...])
        dxq, dxk, dxv, dba, drq, drk, drv, dpa, dpd = vjp(
            (du_ref[...], dw_ref[...], dqd_ref[...], dkt_ref[...], dattn_ref[...], degl_ref[...]))
        dxw = jnp.concatenate([dxq, dxk, dxv], axis=1)
        tail = jnp.concatenate([jnp.zeros((CHUNK - HALO, 3 * GDN_W), F32), carry[...]], axis=0)
        dp_ref[...] = (dxw[HALO:] + tail).astype(BF16)
        carry[...] = dxw[0:HALO]
        dba_ref[...] = dba.astype(BF16)
        for j, dr in enumerate((drq, drk, drv)):
            for k in range(CONV_K):
                dcw_ref[k:k + 1, j * GDN_W:(j + 1) * GDN_W] += dr[k]
        dpar_ref[0:1, :] += dpa
        dpar_ref[1:2, :] += dpd

    acc = lambda shape: pl.BlockSpec(shape, lambda i: (0, 0))
    return _call(body, "gdn_prep_bwd", (nc,),
                 [_stk(HEAD_DIM, rev)] * 4 + [_stk(STACK, rev), _stk(HEAD_DIM, rev)] + main + halo
                 + [pl.BlockSpec((CHUNK, BA_PAD), lambda i: (rev(i), 0)), _whole(cw), _vec(BA_PAD), _vec(BA_PAD),
                    _whole(masks)],
                 [pl.BlockSpec((CHUNK, 3 * GDN_W), lambda i: (rev(i), 0)),
                  pl.BlockSpec((CHUNK, BA_PAD), lambda i: (rev(i), 0)), acc((CONV_K, 3 * GDN_W)), acc((8, BA_PAD))],
                 [_sds((s, 3 * GDN_W), BF16), _sds((s, BA_PAD), BF16), _sds((CONV_K, 3 * GDN_W)), _sds((8, BA_PAD))],
                 scratch=[pltpu.VMEM((HALO, 3 * GDN_W), F32)])(
                     *cts, proj, proj, proj, proj, proj, proj, ba, cw, pa, pd, masks)


def _stack_heads(v):
    return jnp.concatenate(_split(v, HEADS, 1), axis=0)


def _unstack_heads(v):
    return jnp.concatenate(_split(v, HEADS, 0), axis=1)


def _gdn_scan_fwd(prep, proj, nw):
    s = proj.shape[0]
    nc = s // CHUNK
    ident = lambda i: i

    def body(u_ref, w_ref, qd_ref, kt_ref, attn_ref, egl_ref, z_ref, nw_ref, out_ref, st_ref, state):
        @pl.when(pl.program_id(0) == 0)
        def _():
            state[...] = jnp.zeros_like(state)

        st_ref[...] = state[...]
        states = [state[h * HEAD_DIM:(h + 1) * HEAD_DIM, :] for h in range(HEADS)]
        new, out = _gdn_scan(states, u_ref[...], w_ref[...], qd_ref[...], kt_ref[...], attn_ref[...], egl_ref[...],
                             _stack_heads(z_ref[...]), nw_ref[...])
        for h in range(HEADS):
            state[h * HEAD_DIM:(h + 1) * HEAD_DIM, :] = new[h]
        out_ref[...] = _unstack_heads(out)

    return _call(body, "gdn_scan_fwd", (nc,),
                 [_stk(HEAD_DIM, ident)] * 4 + [_stk(STACK, ident), _stk(HEAD_DIM, ident),
                                                _tok(CHUNK, GDN_W, col=5), _vec(HEAD_DIM)],
                 [_tok(CHUNK, GDN_W), pl.BlockSpec((HEADS * HEAD_DIM, HEAD_DIM), lambda i: (i, 0))],
                 [_sds((s, GDN_W)), _sds((nc * HEADS * HEAD_DIM, HEAD_DIM))],
                 scratch=[pltpu.VMEM((HEADS * HEAD_DIM, HEAD_DIM), F32)])(*prep, proj, nw)


def _gdn_scan_bwd(dout, prep, st, proj, nw):
    s = proj.shape[0]
    nc = s // CHUNK
    rev = lambda i: nc - 1 - i

    def body(do_ref, u_ref, w_ref, qd_ref, kt_ref, attn_ref, egl_ref, st_ref, z_ref, nw_ref,
             du_ref, dw_ref, dqd_ref, dkt_ref, dattn_ref, degl_ref, dz_ref, dnw_ref, dstate):
        @pl.when(pl.program_id(0) == 0)
        def _():
            dstate[...] = jnp.zeros_like(dstate)
            dnw_ref[...] = jnp.zeros_like(dnw_ref)

        states = [st_ref[h * HEAD_DIM:(h + 1) * HEAD_DIM, :] for h in range(HEADS)]
        _, vjp = jax.vjp(_gdn_scan, states, u_ref[...], w_ref[...], qd_ref[...], kt_ref[...], attn_ref[...],
                         egl_ref[...], _stack_heads(z_ref[...]), nw_ref[...])
        dnew = [dstate[h * HEAD_DIM:(h + 1) * HEAD_DIM, :] for h in range(HEADS)]
        dst, du, dw, dqd, dkt, dattn, degl, dz, dnw = vjp((dnew, _stack_heads(do_ref[...])))
        for h in range(HEADS):
            dstate[h * HEAD_DIM:(h + 1) * HEAD_DIM, :] = dst[h]
        for ref, val in zip((du_ref, dw_ref, dqd_ref, dkt_ref, dattn_ref, degl_ref), (du, dw, dqd, dkt, dattn, degl)):
            ref[...] = val
        dz_ref[...] = _unstack_heads(dz).astype(BF16)
        dnw_ref[0:1, :] += dnw

    tokr = lambda n, col=0: pl.BlockSpec((CHUNK, n), lambda i: (rev(i), col))
    return _call(body, "gdn_scan_bwd", (nc,),
                 [tokr(GDN_W)] + [_stk(HEAD_DIM, rev)] * 4 + [_stk(STACK, rev), _stk(HEAD_DIM, rev),
                                                             pl.BlockSpec((HEADS * HEAD_DIM, HEAD_DIM),
                                                                          lambda i: (rev(i), 0)),
                                                             tokr(GDN_W, 5), _vec(HEAD_DIM)],
                 [_stk(HEAD_DIM, rev)] * 4 + [_stk(STACK, rev), _stk(HEAD_DIM, rev), tokr(GDN_W),
                                              pl.BlockSpec((8, HEAD_DIM), lambda i: (0, 0))],
                 [_sds((nc * STACK, HEAD_DIM))] * 4 + [_sds((nc * STACK, STACK)), _sds((nc * STACK, HEAD_DIM)),
                                                       _sds((s, GDN_W), BF16), _sds((8, HEAD_DIM))],
                 scratch=[pltpu.VMEM((HEADS * HEAD_DIM, HEAD_DIM), F32)])(dout, *prep, st, proj, nw)


def _wo_specs(l):
    half = N_DEV // 2
    return [pl.BlockSpec((half, 1, D_MODEL // N_DEV, D_MODEL), functools.partial(lambda k, *_: (k, l, 0, 0), k))
            for k in range(2)]


def _wo_half(ref):
    return ref[:, 0].reshape(ref.shape[0] * ref.shape[2], ref.shape[3])


def _out_mlp_fwd(ol, og, x, wo, g1, nw2, sc2, sh2, g2, wup, wdn, l):
    s = x.shape[0]
    t = _tile(s, 512)
    nj = wup.shape[0]
    fc = wup.shape[3]

    def body(ol_ref, og_ref, x_ref, wol_ref, wog_ref, g1_ref, nw_ref, sc_ref, sh_ref, g2_ref, wup_ref, wdn_ref,
             x1_ref, mix_ref, ff_ref, x2_ref, h2_s, acc_s):
        j = pl.program_id(1)

        @pl.when(j == 0)
        def _():
            mix = _bdot(ol_ref[...], _wo_half(wol_ref)) + _bdot(og_ref[...], _wo_half(wog_ref))
            x1 = x_ref[...] + g1_ref[...] * mix
            mix_ref[...] = mix.astype(BF16)
            x1_ref[...] = x1
            h2, _, _ = _modulated_norm(x1, nw_ref[...], sc_ref[...], sh_ref[...])
            h2_s[...] = h2.astype(BF16)
            acc_s[...] = jnp.zeros_like(acc_s)

        up = _bdot(h2_s[...], wup_ref[0, 0])
        act = jnp.square(jnp.maximum(up, 0.0))
        acc_s[...] += _bdot(act, wdn_ref[0, 0])

        @pl.when(j == nj - 1)
        def _():
            ff_ref[...] = acc_s[...].astype(BF16)
            x2_ref[...] = x1_ref[...] + g2_ref[...] * acc_s[...]

    tk = lambda n: pl.BlockSpec((t, n), lambda i, j: (i, 0))
    return _call(body, "out_mlp_fwd", (s // t, nj),
                 [tk(LRU_W), tk(GDN_W), tk(D_MODEL)] + _wo_specs(l) + [_vec(D_MODEL)] * 5
                 + [pl.BlockSpec((1, 1, D_MODEL, fc), lambda i, j: (j, l, 0, 0)),
                    pl.BlockSpec((1, 1, fc, D_MODEL), lambda i, j: (j, l, 0, 0))],
                 [tk(D_MODEL)] * 4,
                 [_sds((s, D_MODEL)), _sds((s, D_MODEL), BF16), _sds((s, D_MODEL), BF16), _sds((s, D_MODEL))],
                 scratch=[pltpu.VMEM((t, D_MODEL), BF16), pltpu.VMEM((t, D_MODEL), F32)])(
                     ol, og, x, wo, wo, g1, nw2, sc2, sh2, g2, wup, wdn)


def _mlp_bwd(dx2, x1, ff, nw2, sc2, sh2, g2, wup, wdn, l):
    s = x1.shape[0]
    t = _tile(s, 512)
    nj = wup.shape[0]
    fc = wup.shape[3]

    def body(dx2_ref, x1_ref, ff_ref, nw_ref, sc_ref, sh_ref, g2_ref, wup_ref, wdn_ref,
             act_ref, dup_ref, h2_ref, dff_ref, dx1_ref, rows_ref, dh2_s):
        i, j = pl.program_id(0), pl.program_id(1)

        @pl.when((i == 0) & (j == 0))
        def _():
            rows_ref[...] = jnp.zeros_like(rows_ref)

        @pl.when(j == 0)
        def _():
            h2, _, _ = _modulated_norm(x1_ref[...], nw_ref[...], sc_ref[...], sh_ref[...])
            h2_ref[...] = h2.astype(BF16)
            dx2 = dx2_ref[...]
            dff_ref[...] = (dx2 * g2_ref[...]).astype(BF16)
            rows_ref[2:3, :] += jnp.sum(dx2 * ff_ref[...].astype(F32), axis=0, keepdims=True)
            dh2_s[...] = jnp.zeros_like(dh2_s)

        up = _bdot(h2_ref[...], wup_ref[0, 0])
        ru = jnp.maximum(up, 0.0)
        act_ref[...] = (ru * ru).astype(BF16)
        dup = (_bdot(dff_ref[...], wdn_ref[0, 0], NT) * (2.0 * ru)).astype(BF16)
        dup_ref[...] = dup
        dh2_s[...] += _bdot(dup, wup_ref[0, 0], NT)

        @pl.when(j == nj - 1)
        def _():
            xv = x1_ref[...]
            _, n, r = _modulated_norm(xv, nw_ref[...], sc_ref[...], sh_ref[...])
            dx, dsh, dsc, dnw = _modulated_norm_bwd(dh2_s[...], xv, n, r, nw_ref[...], sc_ref[...])
            dx1_ref[...] = dx2_ref[...] + dx
            rows_ref[0:1, :] += dsh
            rows_ref[1:2, :] += dsc
            rows_ref[3:4, :] += dnw

    tk = lambda n: pl.BlockSpec((t, n), lambda i, j: (i, 0))
    tj = pl.BlockSpec((t, fc), lambda i, j: (i, j))
    return _call(body, "mlp_bwd", (s // t, nj),
                 [tk(D_MODEL)] * 3 + [_vec(D_MODEL)] * 4
                 + [pl.BlockSpec((1, 1, D_MODEL, fc), lambda i, j: (j, l, 0, 0)),
                    pl.BlockSpec((1, 1, fc, D_MODEL), lambda i, j: (j, l, 0, 0))],
                 [tj, tj, tk(D_MODEL), tk(D_MODEL), tk(D_MODEL), pl.BlockSpec((8, D_MODEL), lambda i, j: (0, 0))],
                 [_sds((s, nj * fc), BF16), _sds((s, nj * fc), BF16), _sds((s, D_MODEL), BF16),
                  _sds((s, D_MODEL), BF16), _sds((s, D_MODEL)), _sds((8, D_MODEL))],
                 scratch=[pltpu.VMEM((t, D_MODEL), F32)])(dx2, x1, ff, nw2, sc2, sh2, g2, wup, wdn)


def _outproj_bwd(dx1, mix, g1, wo, l):
    s = dx1.shape[0]
    t = _tile(s, 512)

    def body(dx1_ref, mix_ref, g1_ref, wol_ref, wog_ref, dmix_ref, dol_ref, dog_ref, rows_ref):
        @pl.when(pl.program_id(0) == 0)
        def _():
            rows_ref[...] = jnp.zeros_like(rows_ref)

        dx1v = dx1_ref[...]
        rows_ref[0:1, :] += jnp.sum(dx1v * mix_ref[...].astype(F32), axis=0, keepdims=True)
        dmix = (dx1v * g1_ref[...]).astype(BF16)
        dmix_ref[...] = dmix
        dol_ref[...] = _bdot(dmix, _wo_half(wol_ref), NT)
        dog_ref[...] = _bdot(dmix, _wo_half(wog_ref), NT)

    return _call(body, "outproj_bwd", (s // t,),
                 [_tok(t, D_MODEL), _tok(t, D_MODEL), _vec(D_MODEL)] + _wo_specs(l),
                 [_tok(t, D_MODEL), _tok(t, LRU_W), _tok(t, GDN_W), pl.BlockSpec((8, D_MODEL), lambda i: (0, 0))],
                 [_sds((s, D_MODEL), BF16), _sds((s, LRU_W)), _sds((s, GDN_W)), _sds((8, D_MODEL))])(dx1, mix, g1, wo, wo)


def _tn_matmul(a, b, name, out=None, l=0, blocked=False):
    s, m = a.shape
    n = b.shape[1]
    ts, bm = _tile(s, 512), _tile(m, 1024)
    bn = next(w for w in (512, 640, 384, 256, 128) if n % w == 0)

    def body(a_ref, b_ref, *rest):
        o_ref = rest[-1]

        @pl.when(pl.program_id(2) == 0)
        def _():
            o_ref[...] = jnp.zeros_like(o_ref)

        acc = _bdot(a_ref[...], b_ref[...], TN)
        o_ref[...] += acc.reshape(o_ref.shape)

    in_specs = [pl.BlockSpec((ts, bm), lambda i, j, k: (k, i)), pl.BlockSpec((ts, bn), lambda i, j, k: (k, j))]
    grid = (m // bm, n // bn, s // ts)
    if out is None:
        return _call(body, name, grid, in_specs, pl.BlockSpec((bm, bn), lambda i, j, k: (i, j)), _sds((m, n)))(a, b)
    if blocked:
        out_spec = pl.BlockSpec((1, 1, bm, bn), lambda i, j, k: (l, j, i, 0))
    else:
        out_spec = pl.BlockSpec((1, bm, bn), lambda i, j, k: (l, i, j))
    return _call(body, name, grid, in_specs + [pl.BlockSpec(memory_space=pl.ANY)], out_spec,
                 _sds(out.shape), aliases={2: 0})(a, b, out)


def _final_fwd_bwd(x, target, fw):
    s = x.shape[0]
    t = _tile(s, 512)

    def body(x_ref, tg_ref, fw_ref, dx_ref, rows_ref):
        @pl.when(pl.program_id(0) == 0)
        def _():
            rows_ref[...] = jnp.zeros_like(rows_ref)

        xv = x_ref[...]
        fwv = fw_ref[...]
        r = lax.rsqrt(jnp.mean(xv * xv, axis=-1, keepdims=True) + EPS)
        err = xv * r * fwv - tg_ref[...]
        part = 0.5 * jnp.sum(jnp.mean(err * err, axis=-1, keepdims=True), axis=0, keepdims=True)
        rows_ref[1:2, :] += jnp.broadcast_to(part, (1, D_MODEL))
        dy = err * (1.0 / D_MODEL)
        rows_ref[0:1, :] += jnp.sum(dy * xv * r, axis=0, keepdims=True)
        dxn = dy * fwv
        dx_ref[...] = r * dxn - xv * (r * r * r) * jnp.mean(dxn * xv, axis=-1, keepdims=True)

    return _call(body, "final_fwd_bwd", (s // t,),
                 [_tok(t, D_MODEL), _tok(t, D_MODEL), _vec(D_MODEL)],
                 [_tok(t, D_MODEL), pl.BlockSpec((8, D_MODEL), lambda i: (0, 0))],
                 [_sds((s, D_MODEL)), _sds((8, D_MODEL))])(x, target, fw)


def _adamw(w, g, m, v):
    m = ADAM_B1 * m + (1.0 - ADAM_B1) * g
    v = ADAM_B2 * v + (1.0 - ADAM_B2) * (g * g)
    m_hat = m / (1.0 - ADAM_B1 ** ADAM_STEP)
    v_hat = v / (1.0 - ADAM_B2 ** ADAM_STEP)
    return -ADAM_LR * (m_hat / (jnp.sqrt(v_hat) + ADAM_EPS) + ADAM_WD * w), m, v


def _mod_local(c_all, wmod, bmod_cols):
    nl, _, cols = wmod.shape

    def body(c_ref, w_ref, b_ref, o_ref):
        cv = c_ref[...]
        o_ref[0] = _bdot(cv * jax.nn.sigmoid(cv), w_ref[0]) + b_ref[0]

    return _call(body, "mod_local", (nl,),
                 [_whole(c_all), pl.BlockSpec((1, D_MODEL, cols), lambda l: (l, 0, 0)),
                  pl.BlockSpec((1, 1, cols), lambda l: (l, 0, 0))],
                 pl.BlockSpec((1, N_DEV, cols), lambda l: (l, 0, 0)), _sds((nl, N_DEV, cols)))(c_all, wmod, bmod_cols)


def _wmod_update(c_all, dmod_cols, w, m, v):
    nl, _, cols = w.shape

    def body(c_ref, d_ref, w_ref, m_ref, v_ref, g_ref, dl_ref, nm_ref, nv_ref):
        cv = c_ref[...]
        g = _bdot(cv * jax.nn.sigmoid(cv), d_ref[0], TN)
        g_ref[0] = g
        dl_ref[0], nm_ref[0], nv_ref[0] = _adamw(w_ref[0], g, m_ref[0], v_ref[0])

    wspec = pl.BlockSpec((1, D_MODEL, cols), lambda l: (l, 0, 0))
    return _call(body, "wmod_update", (nl,),
                 [_whole(c_all), pl.BlockSpec((1, N_DEV, cols), lambda l: (l, 0, 0)), wspec, wspec, wspec],
                 [wspec] * 4, [_sds(w.shape)] * 4)(c_all, dmod_cols, w, m, v)


def _sum_devices(gathered):
    _, r, _ = gathered.shape

    def body(g_ref, o_ref):
        acc = g_ref[0]
        for d in range(1, N_DEV):
            acc = acc + g_ref[d]
        o_ref[...] = acc

    return _call(body, "sum_devices", (1,), [_whole(gathered)], pl.BlockSpec((r, LANES), lambda i: (0, 0)),
                 _sds((r, LANES)))(gathered)


def _adam_flat(w, g, m, v):
    r = w.shape[0]

    def body(w_ref, g_ref, m_ref, v_ref, dl_ref, nm_ref, nv_ref):
        dl_ref[...], nm_ref[...], nv_ref[...] = _adamw(w_ref[...], g_ref[...], m_ref[...], v_ref[...])

    spec = pl.BlockSpec((r, LANES), lambda i: (0, 0))
    return _call(body, "adam_small", (1,), [spec] * 4, [spec] * 3, [_sds((r, LANES))] * 3)(w, g, m, v)


def _pair_add(x, p, core):
    nl, _, r, c = x.shape
    tr = _tile(r, 128 if c > 512 else 256)

    def body(core_ref, x_ref, p_ref, o_ref):
        o_ref[...] = (x_ref[...] + p_ref[...]).astype(BF16)

    return _call(body, "pair_add", (nl, 4, r // tr),
                 [pl.BlockSpec((1, 1, tr, c), lambda l, q, i, core_ref: (l, 2 * q + core_ref[0], i, 0)),
                  pl.BlockSpec((1, 1, tr, c), lambda l, q, i, core_ref: (l, q, i, 0))],
                 pl.BlockSpec((1, 1, tr, c), lambda l, q, i, core_ref: (l, q, i, 0)), _sds((nl, 4, r, c), BF16),
                 prefetch=1)(core, x, p)


def _reduce_adam(x, p, q, place, w, m, v):
    nl, _, r, c = x.shape
    tr = _tile(r, 128 if c > 512 else 256)

    def body(place_ref, x_ref, p_ref, q_ref, w_ref, m_ref, v_ref, g_ref, dl_ref, nm_ref, nv_ref):
        g = (((x_ref[0, 0] + p_ref[0, 0]) + q_ref[0, 0].astype(F32)) + q_ref[0, 1].astype(F32)) + q_ref[0, 2].astype(F32)
        g_ref[0] = g
        dl_ref[0], nm_ref[0], nv_ref[0] = _adamw(w_ref[0], g, m_ref[0], v_ref[0])

    flat = pl.BlockSpec((1, tr, c), lambda l, i, place_ref: (l, i, 0))
    return _call(body, "reduce_adam", (nl, r // tr),
                 [pl.BlockSpec((1, 1, tr, c), lambda l, i, place_ref: (l, place_ref[0], i, 0)),
                  pl.BlockSpec((1, 1, tr, c), lambda l, i, place_ref: (l, place_ref[1], i, 0)),
                  pl.BlockSpec((1, 3, tr, c), lambda l, i, place_ref: (l, 0, i, 0)), flat, flat, flat],
                 [flat] * 4, [_sds((nl, r, c))] * 4, prefetch=1)(place, x, p, q, w, m, v)


def _place():
    return lax.axis_index("x"), lax.axis_index("y"), lax.axis_index("c")


def _all_gather(xs, name, space):
    n = len(xs)

    def body(*refs):
        x_refs, o_refs = refs[:n], refs[n:2 * n]
        send_sems, recv_sems, local_sems = refs[2 * n:]
        x, y, c = _place()
        me, sibling = (x, y, c), (x, y, 1 - c)
        chips = [(1 - x, y), (x, 1 - y), (1 - x, 1 - y)]

        def blk(a, p):
            return o_refs[a].at[4 * p[0] + 2 * p[1] + p[2]]

        def copy(a, k, block, to, src=None):
            return pltpu.make_async_remote_copy(
                src_ref=blk(a, block) if src is None else src, dst_ref=blk(a, block),
                send_sem=send_sems.at[a, k], recv_sem=recv_sems.at[a, k], device_id=to, device_id_type=MESH)

        mine = [pltpu.make_async_copy(x_refs[a], blk(a, me), local_sems.at[a]) for a in range(n)]
        for cp in mine:
            cp.start()
        first = []
        for a in range(n):
            first.append(copy(a, 0, me, sibling, src=x_refs[a]))
            first += [copy(a, 1 + j, me, (*chip, c), src=x_refs[a]) for j, chip in enumerate(chips)]
        for cp in first:
            cp.start()
        passed = []
        for j, chip in enumerate(chips):
            for a in range(n):
                copy(a, 1 + j, (*chip, c), me).wait_recv()
                cp = copy(a, 4 + j, (*chip, c), sibling)
                cp.start()
                passed.append(cp)
        for a in range(n):
            copy(a, 0, sibling, me).wait_recv()
        for j, chip in enumerate(chips):
            for a in range(n):
                copy(a, 4 + j, (*chip, 1 - c), me).wait_recv()
        for cp in first + passed:
            cp.wait_send()
        for cp in mine:
            cp.wait()

    spec = pl.BlockSpec(memory_space=space)
    return pl.pallas_call(
        body, name=name, out_shape=[_sds((N_DEV,) + a.shape, a.dtype) for a in xs],
        in_specs=[spec] * n, out_specs=[spec] * n,
        scratch_shapes=[pltpu.SemaphoreType.DMA((n, 7)), pltpu.SemaphoreType.DMA((n, 7)),
                        pltpu.SemaphoreType.DMA((n,))])(*xs)


def _pair_exchange(xs):
    n = len(xs)

    def body(*refs):
        x_refs, o_refs = refs[:n], refs[n:2 * n]
        send_sems, recv_sems = refs[2 * n:]
        x, y, c = _place()
        copies = [pltpu.make_async_remote_copy(
            src_ref=x_refs[a].at[:, 2 * q + (1 - c)], dst_ref=o_refs[a].at[:, q], send_sem=send_sems.at[a, q],
            recv_sem=recv_sems.at[a, q], device_id=(x, y, 1 - c), device_id_type=MESH)
            for a in range(n) for q in range(4)]
        for cp in copies:
            cp.start()
        for cp in copies:
            cp.wait()

    spec = pl.BlockSpec(memory_space=pl.ANY)
    return pl.pallas_call(
        body, name="pair_exchange", out_shape=[_sds((a.shape[0], 4) + a.shape[2:], a.dtype) for a in xs],
        in_specs=[spec] * n, out_specs=[spec] * n,
        scratch_shapes=[pltpu.SemaphoreType.DMA((n, 4)), pltpu.SemaphoreType.DMA((n, 4))])(*xs)


def _chip_exchange(ys):
    n = len(ys)

    def body(*refs):
        y_refs, o_refs = refs[:n], refs[n:2 * n]
        send_sems, recv_sems = refs[2 * n:]
        x, y, c = _place()
        chips = [(1 - x, y), (x, 1 - y), (1 - x, 1 - y)]
        copies = [pltpu.make_async_remote_copy(
            src_ref=y_refs[a].at[:, 2 * chip[0] + chip[1]], dst_ref=o_refs[a].at[:, r], send_sem=send_sems.at[a, r],
            recv_sem=recv_sems.at[a, r], device_id=(*chip, c), device_id_type=MESH)
            for a in range(n) for r, chip in enumerate(chips)]
        for cp in copies:
            cp.start()
        for cp in copies:
            cp.wait()

    spec = pl.BlockSpec(memory_space=pl.ANY)
    return pl.pallas_call(
        body, name="chip_exchange", out_shape=[_sds((a.shape[0], 3) + a.shape[2:], a.dtype) for a in ys],
        in_specs=[spec] * n, out_specs=[spec] * n,
        scratch_shapes=[pltpu.SemaphoreType.DMA((n, 3)), pltpu.SemaphoreType.DMA((n, 3))])(*ys)


def _size(shape):
    size = 1
    for d in shape:
        size *= d
    return size


def _slab_rows(shape):
    return -(-_size(shape) // (8 * LANES)) * 8


def _pack(arrs):
    parts = []
    for a in arrs:
        flat = a.reshape(-1).astype(F32)
        parts.append(jnp.pad(flat, (0, _slab_rows(a.shape) * LANES - flat.shape[0])).reshape(-1, LANES))
    return jnp.concatenate(parts, axis=0)


def _unpack(slab, shapes):
    out, off = [], 0
    for shp in shapes:
        rows = _slab_rows(shp)
        out.append(slab[off:off + rows].reshape(-1)[:_size(shp)].reshape(shp))
        off += rows
    return out


def _dense_blocks(w):
    eye = jnp.eye(LRU_BLOCKS, dtype=w.dtype)
    return (eye[:, None, :, None] * w[:, :, None, :]).reshape(LRU_W, LRU_W)


def _diag_blocks(dense):
    return jnp.stack([dense[g * LRU_BLOCK:(g + 1) * LRU_BLOCK, g * LRU_BLOCK:(g + 1) * LRU_BLOCK]
                      for g in range(LRU_BLOCKS)])


def _alpha_lanes(v):
    return jnp.zeros((1, BA_PAD), F32).at[0, HEADS:2 * HEADS].set(v)


def _local_step(x, target, mod, p):
    nl = mod.shape[0]
    row = lambda v: v.reshape(1, -1)
    masks = _gdn_masks()
    saved = []
    xc = x
    for l in range(nl):
        mv = [row(mod[l, k * D_MODEL:(k + 1) * D_MODEL]) for k in range(N_MOD)]
        sh1, sc1, g1, sh2, sc2, g2 = mv
        nw1, nw2 = row(p["norm_mix_w"][l]), row(p["norm_mlp_w"][l])
        wa, wx = _dense_blocks(p["lru_gate_a_w"][l]).astype(BF16), _dense_blocks(p["lru_gate_x_w"][l]).astype(BF16)
        lru_args = (p["lru_conv_w"][l], row(p["lru_conv_b"][l]), wa, wx, row(p["lru_gate_a_b"][l]),
                    row(p["lru_gate_x_b"][l]), row(p["lru_lambda"][l]), row(p["lru_norm_w"][l]))
        gdn_args = (p["gdn_conv_w"][l], _alpha_lanes(p["gdn_a_log"][l]), _alpha_lanes(p["gdn_dt_bias"][l]), masks)
        gnw = row(p["gdn_norm_w"][l])
        proj, ba = _inproj_fwd(xc, nw1, sc1, sh1, p["win"], p["wba"], l)
        ol, hs = _lru_fwd(proj, *lru_args)
        prep = _gdn_prep_fwd(proj, ba, *gdn_args)
        og, st = _gdn_scan_fwd(prep, proj, gnw)
        x1, mix, ff, x2 = _out_mlp_fwd(ol, og, xc, p["wo"], g1, nw2, sc2, sh2, g2, p["wup"], p["wdn"], l)
        saved.append(dict(x=xc, mv=mv, nw1=nw1, nw2=nw2, lru_args=lru_args, gdn_args=gdn_args, gnw=gnw, proj=proj,
                          ba=ba, ol=ol, hs=hs, prep=prep, og=og, st=st, x1=x1, mix=mix, ff=ff))
        xc = x2

    dx, frows = _final_fwd_bwd(xc, target, row(p["final_norm_w"]))
    loss_part = frows[1, 0]
    small = {k: [None] * nl for k in ("norm_mix_w", "norm_mlp_w", "lru_conv_w", "lru_conv_b", "lru_gate_a_w",
                                      "lru_gate_a_b", "lru_gate_x_w", "lru_gate_x_b", "lru_lambda", "lru_norm_w",
                                      "gdn_conv_w", "gdn_a_log", "gdn_dt_bias", "gdn_norm_w")}
    fc = D_FF // N_DEV
    g_in = [None] * nl
    g_out, g_down = lax.empty((nl, D_MODEL, D_MODEL), F32), lax.empty((nl, D_FF, D_MODEL), F32)
    g_up = lax.empty((nl, N_DEV, D_MODEL, fc), F32)
    dmod = [None] * nl
    for l in reversed(range(nl)):
        sv = saved[l]
        sh1, sc1, g1, sh2, sc2, g2 = sv["mv"]
        act, dup, h2b, dffb, dx1, rows2 = _mlp_bwd(dx, sv["x1"], sv["ff"], sv["nw2"], sc2, sh2, g2, p["wup"],
                                                   p["wdn"], l)
        g_up = _tn_matmul(h2b, dup, "grad_w_up", out=g_up, l=l, blocked=True)
        g_down = _tn_matmul(act, dffb, "grad_w_down", out=g_down, l=l)
        dmix, dol, dog, rows1 = _outproj_bwd(dx1, sv["mix"], g1, p["wo"], l)
        cat = jnp.concatenate([sv["ol"], sv["og"]], axis=1).astype(BF16)
        g_out = _tn_matmul(cat, dmix, "grad_w_out", out=g_out, l=l)
        dpl, dwa, dwx, lrows = _lru_bwd(dol, sv["proj"], sv["hs"], *sv["lru_args"])
        *cts, dpz, gnrow = _gdn_scan_bwd(dog, sv["prep"], sv["st"], sv["proj"], sv["gnw"])
        dpq, dba, dcw, dpar = _gdn_prep_bwd(cts, sv["proj"], sv["ba"], *sv["gdn_args"])
        dx, hb, rows0 = _inproj_bwd(dpl, dpq, dpz, dba, sv["x"], dx1, sv["nw1"], sc1, sh1, p["win"], p["wba"], l)
        dproj = jnp.concatenate([dpl, dpq, dpz, dba], axis=1)
        g_in[l] = jnp.transpose(_tn_matmul(hb, dproj, "grad_w_in")[:, :IN_COLS].reshape(
            D_MODEL, N_DEV, IN_COLS // N_DEV), (1, 0, 2))
        dmod[l] = jnp.concatenate([rows0[0], rows0[1], rows1[0], rows2[0], rows2[1], rows2[2]])
        small["norm_mix_w"][l], small["norm_mlp_w"][l] = rows0[2], rows2[3]
        small["lru_conv_w"][l], small["lru_conv_b"][l] = lrows[8:8 + CONV_K], lrows[0]
        small["lru_gate_a_w"][l], small["lru_gate_x_w"][l] = _diag_blocks(dwa), _diag_blocks(dwx)
        small["lru_gate_a_b"][l], small["lru_gate_x_b"][l] = lrows[1], lrows[2]
        small["lru_lambda"][l], small["lru_norm_w"][l] = lrows[3], lrows[4]
        small["gdn_conv_w"][l] = dcw
        small["gdn_a_log"][l], small["gdn_dt_bias"][l] = dpar[0, HEADS:2 * HEADS], dpar[1, HEADS:2 * HEADS]
        small["gdn_norm_w"][l] = gnrow[0]
    small = {k: jnp.stack(v) for k, v in small.items()}
    small["final_norm_w"] = frows[0]
    big = dict(w_in=jnp.stack(g_in), w_out=g_out.reshape(nl, N_DEV, D_MODEL // N_DEV, D_MODEL), w_up=g_up,
               w_down=g_down.reshape(nl, N_DEV, fc, D_MODEL))
    return loss_part, dx, big, small, jnp.stack(dmod)


SMALL_REPLICATED = ("norm_mix_w", "norm_mlp_w", "b_mod", "lru_conv_b", "lru_gate_a_w", "lru_gate_a_b", "lru_gate_x_w",
                    "lru_gate_x_b", "lru_lambda", "lru_norm_w", "gdn_a_log", "gdn_dt_bias", "gdn_norm_w",
                    "final_norm_w")
SMALL_SHARDED = ("lru_conv_w", "gdn_conv_w")
WEIGHT_ORDER = ("norm_mix_w", "norm_mlp_w", "w_mod", "b_mod", "w_in", "lru_conv_w", "lru_conv_b", "lru_gate_a_w",
                "lru_gate_a_b", "lru_gate_x_w", "lru_gate_x_b", "lru_lambda", "lru_norm_w", "gdn_conv_w", "gdn_a_log",
                "gdn_dt_bias", "gdn_norm_w", "w_out", "w_up", "w_down", "final_norm_w")


def kernel(x, c, norm_mix_w, norm_mlp_w, w_mod, b_mod, w_in, lru_conv_w, lru_conv_b, lru_gate_a_w, lru_gate_a_b, lru_gate_x_w, lru_gate_x_b, lru_lambda, lru_norm_w, gdn_conv_w, gdn_a_log, gdn_dt_bias, gdn_norm_w, w_out, w_up, w_down, final_norm_w, loss_target, m_norm_mix_w, m_norm_mlp_w, m_w_mod, m_b_mod, m_w_in, m_lru_conv_w, m_lru_conv_b, m_lru_gate_a_w, m_lru_gate_a_b, m_lru_gate_x_w, m_lru_gate_x_b, m_lru_lambda, m_lru_norm_w, m_gdn_conv_w, m_gdn_a_log, m_gdn_dt_bias, m_gdn_norm_w, m_w_out, m_w_up, m_w_down, m_final_norm_w, v_norm_mix_w, v_norm_mlp_w, v_w_mod, v_b_mod, v_w_in, v_lru_conv_w, v_lru_conv_b, v_lru_gate_a_w, v_lru_gate_a_b, v_lru_gate_x_w, v_lru_gate_x_b, v_lru_lambda, v_lru_norm_w, v_gdn_conv_w, v_gdn_a_log, v_gdn_dt_bias, v_gdn_norm_w, v_w_out, v_w_up, v_w_down, v_final_norm_w):
    args = dict(locals())
    w = {k: args[k] for k in WEIGHT_ORDER}
    mom = {k: args["m_" + k] for k in WEIGHT_ORDER}
    var = {k: args["v_" + k] for k in WEIGHT_ORDER}
    nl = w_in.shape[0]
    px, py, pc = _place()
    me = 4 * px + 2 * py + pc
    core = jnp.reshape(pc, (1,)).astype(jnp.int32)

    shapes0 = [c.shape, lru_conv_w.shape, gdn_conv_w.shape]
    (g0,) = _all_gather([_pack([c, lru_conv_w, gdn_conv_w])], "gather_cond", pltpu.VMEM)
    per_dev = [_unpack(g0[d], shapes0) for d in range(N_DEV)]
    c_all = jnp.concatenate([pd[0] for pd in per_dev], axis=0)
    lru_conv_full = jnp.concatenate([pd[1] for pd in per_dev], axis=-1)
    gdn_conv_full = jnp.concatenate([pd[2] for pd in per_dev], axis=-1)

    cols = w_mod.shape[2]
    bmod_cols = lax.dynamic_slice_in_dim(b_mod, me * cols, cols, axis=1).reshape(nl, 1, cols)
    mod_cols = _mod_local(c_all, w_mod, bmod_cols)
    (g1,) = _all_gather([mod_cols.reshape(nl * N_DEV, cols)], "gather_mod", pltpu.VMEM)
    g1 = g1.reshape(N_DEV, nl, N_DEV, cols)
    mod = jnp.transpose(lax.dynamic_index_in_dim(g1, me, axis=2, keepdims=False), (1, 0, 2)).reshape(nl, N_DEV * cols)

    gin, gout, gup, gdn = _all_gather([w_in.astype(BF16), w_out.astype(BF16), w_up.astype(BF16),
                                       w_down.astype(BF16)], "gather_weights", pl.ANY)
    win_full = jnp.transpose(gin, (1, 2, 0, 3)).reshape(nl, D_MODEL, IN_COLS)
    p = dict(w)
    p["win"] = win_full
    p["wba"] = jnp.pad(win_full[:, :, IN_MAIN:], ((0, 0), (0, 0), (0, BA_PAD - (IN_COLS - IN_MAIN))))
    p["wo"], p["wup"], p["wdn"] = gout, gup, gdn
    p["lru_conv_w"], p["gdn_conv_w"] = lru_conv_full, gdn_conv_full

    loss_part, grad_x, big, small, dmod = _local_step(x[0], loss_target[0], mod, p)
    loss = lax.psum(loss_part, MESH_AXES)

    small_names = sorted(small)
    slab = _pack([dmod] + [small[k] for k in small_names])
    (gs,) = _all_gather([slab], "gather_small_grads", pltpu.VMEM)
    dmod_all = gs[:, :_slab_rows(dmod.shape)].reshape(N_DEV, nl, N_MOD * D_MODEL)
    summed = _unpack(_sum_devices(gs), [dmod.shape] + [small[k].shape for k in small_names])
    grads = dict(zip(small_names, summed[1:]))
    grads["b_mod"] = summed[0]
    for k, width in (("lru_conv_w", LRU_W // N_DEV), ("gdn_conv_w", 3 * GDN_W // N_DEV)):
        grads[k] = lax.dynamic_slice_in_dim(grads[k], me * width, width, axis=2)
    names = SMALL_REPLICATED + SMALL_SHARDED
    shapes = [w[k].shape for k in names]
    dl, nm, nv = _adam_flat(_pack([w[k] for k in names]), _pack([grads[k] for k in names]),
                            _pack([mom[k] for k in names]), _pack([var[k] for k in names]))
    delta = dict(zip(names, _unpack(dl, shapes)))
    new_m = dict(zip(names, _unpack(nm, shapes)))
    new_v = dict(zip(names, _unpack(nv, shapes)))

    dmod_cols = jnp.transpose(lax.dynamic_slice_in_dim(dmod_all, me * cols, cols, axis=2), (1, 0, 2))
    grads["w_mod"], delta["w_mod"], new_m["w_mod"], new_v["w_mod"] = _wmod_update(
        c_all, dmod_cols, w_mod, m_w_mod, v_w_mod)

    order = ("w_in", "w_out", "w_up", "w_down")
    xs = [big[k] for k in order]
    ps = _pair_exchange(xs)
    ys = [_pair_add(xk, pk, core) for xk, pk in zip(xs, ps)]
    qs = _chip_exchange(ys)
    place = jnp.stack([me, 2 * px + py]).astype(jnp.int32)
    for k, xk, pk, qk in zip(order, xs, ps, qs):
        grads[k], delta[k], new_m[k], new_v[k] = _reduce_adam(xk, pk, qk, place, w[k], mom[k], var[k])

    return (loss, grad_x[None], *[grads[k] for k in WEIGHT_ORDER], *[delta[k] for k in WEIGHT_ORDER],
            *[new_m[k] for k in WEIGHT_ORDER], *[new_v[k] for k in WEIGHT_ORDER])
```

```python
import functools

import jax
import jax.numpy as jnp
from jax import lax
from jax.experimental import pallas as pl
from jax.experimental.pallas import tpu as pltpu

F32 = jnp.float32
BF16 = jnp.bfloat16

D_MODEL = 1024
LRU_W = 512
LRU_BLOCKS = 8
LRU_BLOCK = 64
LRU_C = 8.0
GDN_W = 512
HEADS = 4
HEAD_DIM = 128
CHUNK = 64
STACK = HEADS * CHUNK
CONV_K = 4
D_FF = 4096
N_MOD = 6
IN_COLS = 3080
IN_MAIN = 3072
BA_PAD = 128
EPS = 1e-6
N_DEV = 8
HALO = 8
LANES = 128
ADAM_LR, ADAM_B1, ADAM_B2, ADAM_EPS, ADAM_WD, ADAM_STEP = 0.001, 0.9, 0.999, 1e-08, 0.01, 10
MESH_AXES = ("x", "y", "c")
MESH = pl.DeviceIdType.MESH

NN = (((1,), (0,)), ((), ()))
NT = (((1,), (1,)), ((), ()))
TN = (((0,), (0,)), ((), ()))


def _bdot(a, b, dims=NN):
    return lax.dot_general(a.astype(BF16), b.astype(BF16), dims, preferred_element_type=F32)


def _sdot(a, b, dims=NN):
    ah, bh = a.astype(BF16), b.astype(BF16)
    al, bl = (a - ah.astype(F32)).astype(BF16), (b - bh.astype(F32)).astype(BF16)
    return _bdot(ah, bh, dims) + (_bdot(al, bh, dims) + _bdot(ah, bl, dims))


def _hdot(a, b, dims=NN):
    return lax.dot_general(a, b, dims, precision=lax.Precision.HIGHEST, preferred_element_type=F32)


def _sds(shape, dtype=F32):
    return jax.ShapeDtypeStruct(tuple(shape), dtype)


def _tile(n, t):
    return min(n, t)


def _call(body, name, grid, in_specs, out_specs, out_shape, scratch=(), vmem_mb=48, prefetch=0, aliases=None):
    params = pltpu.CompilerParams(dimension_semantics=("arbitrary",) * len(grid), vmem_limit_bytes=vmem_mb * 2**20)
    if prefetch:
        spec = pltpu.PrefetchScalarGridSpec(num_scalar_prefetch=prefetch, grid=grid, in_specs=in_specs,
                                            out_specs=out_specs, scratch_shapes=list(scratch))
        return pl.pallas_call(body, name=name, grid_spec=spec, out_shape=out_shape, compiler_params=params)
    return pl.pallas_call(body, name=name, grid=grid, in_specs=in_specs, out_specs=out_specs, out_shape=out_shape,
                          scratch_shapes=list(scratch), compiler_params=params, input_output_aliases=aliases or {})


def _tok(t, n, col=0):
    return pl.BlockSpec((t, n), lambda i, *_: (i, col))


def _vec(n):
    return pl.BlockSpec((1, n), lambda *_: (0, 0))


def _whole(a):
    nd = a.ndim
    return pl.BlockSpec(a.shape, lambda *_: (0,) * nd)


def _layer(l, *dims):
    return pl.BlockSpec((1,) + dims, lambda *_: (l,) + (0,) * len(dims))


def _gelu(y):
    c0, c1 = 0.7978845608028654, 0.044715
    return 0.5 * y * (1.0 + jnp.tanh(c0 * (y + c1 * y * y * y)))


def _gelu_grad(y):
    c0, c1 = 0.7978845608028654, 0.044715
    t = jnp.tanh(c0 * (y + c1 * y * y * y))
    return 0.5 * (1.0 + t) + 0.5 * y * (1.0 - t * t) * c0 * (1.0 + 3.0 * c1 * y * y)


def _softplus(v):
    return jnp.maximum(v, 0.0) + jnp.log(1.0 + jnp.exp(-jnp.where(v > 0, v, -v)))


@functools.partial(jax.custom_vjp, nondiff_argnums=(1,))
def _roll_rows(v, s):
    s = s % v.shape[0]
    return pltpu.roll(v, s, axis=0) if s else v


def _roll_rows_fwd(v, s):
    return _roll_rows(v, s), None


def _roll_rows_bwd(s, _, g):
    return (_roll_rows(g, -s),)


_roll_rows.defvjp(_roll_rows_fwd, _roll_rows_bwd)


@jax.custom_vjp
def _drop_halo(v):
    return v[HALO:]


def _drop_halo_fwd(v):
    return v[HALO:], None


def _drop_halo_bwd(_, g):
    return (jnp.concatenate([jnp.zeros((HALO, g.shape[1]), g.dtype), g], axis=0),)


_drop_halo.defvjp(_drop_halo_fwd, _drop_halo_bwd)


@functools.partial(jax.custom_vjp, nondiff_argnums=(1, 2))
def _split(v, n, axis):
    w = v.shape[axis] // n
    return tuple(lax.slice_in_dim(v, k * w, (k + 1) * w, axis=axis) for k in range(n))


def _split_fwd(v, n, axis):
    return _split(v, n, axis), None


def _split_bwd(n, axis, _, gs):
    return (jnp.concatenate(list(gs), axis=axis),)


_split.defvjp(_split_fwd, _split_bwd)


def _conv_taps(xw):
    return [_drop_halo(_roll_rows(xw, CONV_K - 1 - k)) for k in range(CONV_K)]


def _modulated_norm(xv, nw, sc, sh):
    r = lax.rsqrt(jnp.mean(xv * xv, axis=-1, keepdims=True) + EPS)
    n = xv * r * nw
    return n * (1.0 + sc) + sh, n, r


def _modulated_norm_bwd(dh, xv, n, r, nw, sc):
    dn = dh * (1.0 + sc)
    dxn = dn * nw
    dx = r * dxn - xv * (r * r * r) * jnp.mean(dxn * xv, axis=-1, keepdims=True)
    return (dx, jnp.sum(dh, axis=0, keepdims=True), jnp.sum(dh * n, axis=0, keepdims=True),
            jnp.sum(dn * xv * r, axis=0, keepdims=True))


def _inproj_fwd(x, nw, sc, sh, win, wba, l):
    s = x.shape[0]
    t = _tile(s, 256)

    def body(x_ref, nw_ref, sc_ref, sh_ref, win_ref, wba_ref, proj_ref, ba_ref):
        h, _, _ = _modulated_norm(x_ref[...], nw_ref[...], sc_ref[...], sh_ref[...])
        hb = h.astype(BF16)
        proj_ref[...] = _bdot(hb, win_ref[0])
        ba_ref[...] = _bdot(hb, wba_ref[0])

    return _call(body, "inproj_fwd", (s // t,),
                 [_tok(t, D_MODEL), _vec(D_MODEL), _vec(D_MODEL), _vec(D_MODEL), _layer(l, D_MODEL, IN_MAIN),
                  _layer(l, D_MODEL, BA_PAD)],
                 [_tok(t, IN_MAIN), _tok(t, BA_PAD)],
                 [_sds((s, IN_MAIN)), _sds((s, BA_PAD))])(x, nw, sc, sh, win, wba)


def _inproj_bwd(dpl, dpq, dpz, dba, x, dx1, nw, sc, sh, win, wba, l):
    s = x.shape[0]
    t = _tile(s, 256)

    def body(dpl_ref, dpq_ref, dpz_ref, dba_ref, x_ref, dx1_ref, nw_ref, sc_ref, sh_ref, win_ref, wba_ref,
             dx_ref, hb_ref, acc_ref):
        @pl.when(pl.program_id(0) == 0)
        def _():
            acc_ref[...] = jnp.zeros_like(acc_ref)

        dh = (_bdot(dpl_ref[...], win_ref[0, :, 0:2 * LRU_W], NT)
              + _bdot(dpq_ref[...], win_ref[0, :, 2 * LRU_W:2 * LRU_W + 3 * GDN_W], NT)
              + _bdot(dpz_ref[...], win_ref[0, :, 2 * LRU_W + 3 * GDN_W:IN_MAIN], NT)
              + _bdot(dba_ref[...], wba_ref[0], NT))
        xv = x_ref[...]
        h, n, r = _modulated_norm(xv, nw_ref[...], sc_ref[...], sh_ref[...])
        hb_ref[...] = h.astype(BF16)
        dx, dsh, dsc, dnw = _modulated_norm_bwd(dh, xv, n, r, nw_ref[...], sc_ref[...])
        dx_ref[...] = dx1_ref[...] + dx
        acc_ref[0:1, :] += dsh
        acc_ref[1:2, :] += dsc
        acc_ref[2:3, :] += dnw

    return _call(body, "inproj_bwd", (s // t,),
                 [_tok(t, 2 * LRU_W), _tok(t, 3 * GDN_W), _tok(t, GDN_W), _tok(t, BA_PAD), _tok(t, D_MODEL),
                  _tok(t, D_MODEL), _vec(D_MODEL), _vec(D_MODEL), _vec(D_MODEL), _layer(l, D_MODEL, IN_MAIN),
                  _layer(l, D_MODEL, BA_PAD)],
                 [_tok(t, D_MODEL), _tok(t, D_MODEL), pl.BlockSpec((8, D_MODEL), lambda i: (0, 0))],
                 [_sds((s, D_MODEL)), _sds((s, D_MODEL), BF16), _sds((8, D_MODEL))])(
                     dpl, dpq, dpz, dba, x, dx1, nw, sc, sh, win, wba)


def _lru_gates(xw, cw_rows, cb, wa, wx, gab, gxb, lam):
    taps = _conv_taps(xw)
    xr = cb + cw_rows[0] * taps[0] + cw_rows[1] * taps[1] + cw_rows[2] * taps[2] + cw_rows[3] * taps[3]
    xb = xr.astype(BF16)
    r = jax.nn.sigmoid(_bdot(xb, wa) + gab)
    i = jax.nn.sigmoid(_bdot(xb, wx) + gxb)
    z = jnp.exp(-jnp.where(lam > 0, lam, -lam))
    w1 = 1.0 + z
    log1p_z = jnp.where(w1 == 1.0, z, jnp.log(w1) * z / (w1 - 1.0))
    ls = jnp.minimum(lam, 0.0) - log1p_z
    la = LRU_C * r * ls
    a = jnp.exp(la)
    x2 = 2.0 * la
    u = jnp.exp(x2)
    mm_raw = jnp.where(u == 1.0, -x2,
                       jnp.where(x2 < -30.0, 1.0, (1.0 - u) * x2 / jnp.log(jnp.maximum(u, 1e-30))))
    mult = jnp.sqrt(jnp.maximum(mm_raw, 1e-12))
    return dict(taps=taps, xr=xr, r=r, i=i, ls=ls, a=a, mm_raw=mm_raw, mult=mult)


def _lru_specs(s, t, tile_of):
    nh = t // HALO
    xl = pl.BlockSpec((t, LRU_W), lambda i: (tile_of(i), 0))
    yl = pl.BlockSpec((t, LRU_W), lambda i: (tile_of(i), 1))
    hx = pl.BlockSpec((HALO, LRU_W), lambda i: (jnp.maximum(tile_of(i) * nh - 1, 0), 0))
    return xl, yl, hx


def _lru_fwd(proj, cw, cb, wa, wx, gab, gxb, lam, lnw):
    s = proj.shape[0]
    t = _tile(s, 256)
    xl, yl, hx = _lru_specs(s, t, lambda i: i)

    def body(xl_ref, yl_ref, hx_ref, cw_ref, cb_ref, wa_ref, wx_ref, gab_ref, gxb_ref, lam_ref, lnw_ref,
             out_ref, h_ref, a_s, b_s, hc):
        i = pl.program_id(0)

        @pl.when(i == 0)
        def _():
            hc[...] = jnp.zeros_like(hc)

        halo = jnp.where(i > 0, hx_ref[...], 0.0)
        xw = jnp.concatenate([halo, xl_ref[...]], axis=0)
        g = _lru_gates(xw, [cw_ref[k:k + 1, :] for k in range(CONV_K)], cb_ref[...], wa_ref[...], wx_ref[...],
                       gab_ref[...], gxb_ref[...], lam_ref[...])
        a_s[...] = g["a"]
        b_s[...] = g["mult"] * (g["i"] * g["xr"])

        def step(k, h):
            h = a_s[pl.ds(k, 1), :] * h + b_s[pl.ds(k, 1), :]
            h_ref[pl.ds(k, 1), :] = h
            return h

        hc[...] = lax.fori_loop(0, t, step, hc[...], unroll=8)
        m = h_ref[...] * _gelu(yl_ref[...])
        out_ref[...] = m * lax.rsqrt(jnp.mean(m * m, axis=-1, keepdims=True) + EPS) * lnw_ref[...]

    return _call(body, "lru_fwd", (s // t,),
                 [xl, yl, hx, _whole(cw), _vec(LRU_W), _whole(wa), _whole(wx)] + [_vec(LRU_W)] * 4,
                 [_tok(t, LRU_W), _tok(t, LRU_W)],
                 [_sds((s, LRU_W)), _sds((s, LRU_W))],
                 scratch=[pltpu.VMEM((t, LRU_W), F32), pltpu.VMEM((t, LRU_W), F32), pltpu.VMEM((1, LRU_W), F32)])(
                     proj, proj, proj, cw, cb, wa, wx, gab, gxb, lam, lnw)


def _lru_bwd(dout, proj, hs, cw, cb, wa, wx, gab, gxb, lam, lnw):
    s = proj.shape[0]
    t = _tile(s, 256)
    nt = s // t
    rev = lambda i: nt - 1 - i
    xl, yl, hx = _lru_specs(s, t, rev)
    nh = t // HALO
    tk = pl.BlockSpec((t, LRU_W), lambda i: (rev(i), 0))
    hh = pl.BlockSpec((HALO, LRU_W), lambda i: (jnp.maximum(rev(i) * nh - 1, 0), 0))

    def body(do_ref, xl_ref, yl_ref, hx_ref, h_ref, hh_ref, cw_ref, cb_ref, wa_ref, wx_ref, gab_ref, gxb_ref,
             lam_ref, lnw_ref, dp_ref, dwa_ref, dwx_ref, rows_ref, dh_s, dhd_s, carry, dxr_next):
        i = pl.program_id(0)
        first_tile = rev(i) == 0

        @pl.when(i == 0)
        def _():
            carry[...] = jnp.zeros_like(carry)
            dxr_next[...] = jnp.zeros_like(dxr_next)
            dwa_ref[...] = jnp.zeros_like(dwa_ref)
            dwx_ref[...] = jnp.zeros_like(dwx_ref)
            rows_ref[...] = jnp.zeros_like(rows_ref)

        halo = jnp.where(first_tile, 0.0, hx_ref[...])
        xw = jnp.concatenate([halo, xl_ref[...]], axis=0)
        cw_rows = [cw_ref[k:k + 1, :] for k in range(CONV_K)]
        lam_v = lam_ref[...]
        g = _lru_gates(xw, cw_rows, cb_ref[...], wa_ref[...], wx_ref[...], gab_ref[...], gxb_ref[...], lam_v)
        a, r, gi, xr, mult = g["a"], g["r"], g["i"], g["xr"], g["mult"]
        hv = h_ref[...]
        yv = yl_ref[...]
        gl = _gelu(yv)
        m = hv * gl
        rn = lax.rsqrt(jnp.mean(m * m, axis=-1, keepdims=True) + EPS)
        dov = do_ref[...]
        dmn = dov * lnw_ref[...]
        rows_ref[4:5, :] += jnp.sum(dov * m * rn, axis=0, keepdims=True)
        dm = rn * dmn - m * (rn * rn * rn) * jnp.mean(dmn * m, axis=-1, keepdims=True)
        dhd_s[...] = dm * gl
        dy = dm * hv * _gelu_grad(yv)
        dh_s[...] = a

        def step(k, c):
            row = t - 1 - k
            d = dhd_s[pl.ds(row, 1), :] + c
            c = dh_s[pl.ds(row, 1), :] * d
            dh_s[pl.ds(row, 1), :] = d
            return c

        carry[...] = lax.fori_loop(0, t, step, carry[...], unroll=8)
        dH = dh_s[...]
        hprev_halo = jnp.where(first_tile, 0.0, hh_ref[...])
        hprev = _drop_halo(_roll_rows(jnp.concatenate([hprev_halo, hv], axis=0), 1))
        da = dH * hprev
        dmult = dH * gi * xr
        di = dH * mult * xr
        dxr = dH * mult * gi
        dla = jnp.where(g["mm_raw"] > 1e-12, dmult * (0.5 / mult) * (-2.0 * a * a), 0.0) + da * a
        dr = dla * (LRU_C * g["ls"])
        sig_neg = jax.nn.sigmoid(-lam_v)
        rows_ref[3:4, :] += jnp.sum(dla * (LRU_C * r), axis=0, keepdims=True) * sig_neg
        drp = dr * r * (1.0 - r)
        dip = di * gi * (1.0 - gi)
        rows_ref[1:2, :] += jnp.sum(drp, axis=0, keepdims=True)
        rows_ref[2:3, :] += jnp.sum(dip, axis=0, keepdims=True)
        xb = xr.astype(BF16)
        drb = drp.astype(BF16)
        dib = dip.astype(BF16)
        dwa_ref[...] += _bdot(xb, drb, TN)
        dwx_ref[...] += _bdot(xb, dib, TN)
        dxr = dxr + _bdot(drb, wa_ref[...], NT) + _bdot(dib, wx_ref[...], NT)
        rows_ref[0:1, :] += jnp.sum(dxr, axis=0, keepdims=True)
        ext = jnp.concatenate([dxr, dxr_next[...]], axis=0)
        dx = cw_rows[CONV_K - 1] * dxr
        for k in range(CONV_K - 1):
            dx = dx + cw_rows[k] * _roll_rows(ext, -(CONV_K - 1 - k))[0:t]
        for k in range(CONV_K):
            rows_ref[8 + k:9 + k, :] += jnp.sum(dxr * g["taps"][k], axis=0, keepdims=True)
        dxr_next[...] = dxr[0:HALO]
        dp_ref[...] = jnp.concatenate([dx, dy], axis=1).astype(BF16)

    acc = lambda shape: pl.BlockSpec(shape, lambda i: (0, 0))
    return _call(body, "lru_bwd", (nt,),
                 [tk, xl, yl, hx, tk, hh, _whole(cw), _vec(LRU_W), _whole(wa), _whole(wx)] + [_vec(LRU_W)] * 4,
                 [pl.BlockSpec((t, 2 * LRU_W), lambda i: (rev(i), 0)), acc((LRU_W, LRU_W)), acc((LRU_W, LRU_W)),
                  acc((16, LRU_W))],
                 [_sds((s, 2 * LRU_W), BF16), _sds((LRU_W, LRU_W)), _sds((LRU_W, LRU_W)), _sds((16, LRU_W))],
                 scratch=[pltpu.VMEM((t, LRU_W), F32), pltpu.VMEM((t, LRU_W), F32), pltpu.VMEM((1, LRU_W), F32),
                          pltpu.VMEM((HALO, LRU_W), F32)])(
                     dout, proj, proj, proj, hs, hs, cw, cb, wa, wx, gab, gxb, lam, lnw)


def _gdn_masks():
    row = lax.broadcasted_iota(jnp.int32, (STACK, STACK), 0)
    col = lax.broadcasted_iota(jnp.int32, (STACK, STACK), 1)
    same = (row // CHUNK) == (col // CHUNK)
    return jnp.stack([(same & (col <= row)).astype(F32), (same & (col < row)).astype(F32), (row == col).astype(F32)])


def _conv_silu(xw, rows):
    taps = _conv_taps(xw)
    y = rows[0] * taps[0] + rows[1] * taps[1] + rows[2] * taps[2] + rows[3] * taps[3]
    return y * jax.nn.sigmoid(y)


def _split3(v):
    hi = v.astype(BF16)
    r1 = v - hi.astype(F32)
    mid = r1.astype(BF16)
    return hi, mid, (r1 - mid.astype(F32)).astype(BF16)


def _mask_dot_raw(mask, v, dims):
    parts = _split3(v)
    d = lambda p: lax.dot_general(mask, p, dims, preferred_element_type=F32)
    return d(parts[0]) + (d(parts[1]) + d(parts[2]))


@jax.custom_vjp
def _mask_dot(mask, v):
    return _mask_dot_raw(mask, v, NN)


def _mask_dot_fwd(mask, v):
    return _mask_dot_raw(mask, v, NN), mask


def _mask_dot_bwd(mask, ct):
    return jnp.zeros_like(mask), _mask_dot_raw(mask, ct, TN)


_mask_dot.defvjp(_mask_dot_fwd, _mask_dot_bwd)


def _unit_lower_inverse(n, eye):
    tinv = eye + n
    p = n
    for _ in range(5):
        p = _bdot(p, p)
        tinv = tinv + _bdot(tinv, p)
    return tinv.astype(BF16)


@jax.custom_vjp
def _unit_lower_solve(n, rhs, tinv):
    x0 = _bdot(tinv, rhs)
    return x0 + _bdot(tinv, rhs - x0 + _sdot(n, x0))


def _unit_lower_solve_fwd(n, rhs, tinv):
    x = _unit_lower_solve(n, rhs, tinv)
    return x, (n, tinv, x)


def _unit_lower_solve_bwd(res, ct):
    n, tinv, x = res
    y0 = _bdot(tinv, ct, TN)
    y = y0 + _bdot(tinv, ct - y0 + _sdot(n, y0, TN), TN)
    return _bdot(y, x, NT), y, jnp.zeros_like(tinv)


_unit_lower_solve.defvjp(_unit_lower_solve_fwd, _unit_lower_solve_bwd)


def _gdn_prep(xq, xk, xv, ba, cwq, cwk, cwv, pa, pd, masks, tinv=None, with_inverse=False):
    lower, strict, eye = masks[0], masks[1], masks[2]
    lower_b = lower.astype(BF16)
    lane = lax.broadcasted_iota(jnp.int32, (CHUNK, LANES), 1)
    q = jnp.concatenate(_split(_conv_silu(xq, cwq), HEADS, 1), axis=0)
    k = jnp.concatenate(_split(_conv_silu(xk, cwk), HEADS, 1), axis=0)
    v = jnp.concatenate(_split(_conv_silu(xv, cwv), HEADS, 1), axis=0)
    qn = q * lax.rsqrt(jnp.sum(q * q, axis=-1, keepdims=True) + 1e-6) * (HEAD_DIM ** -0.5)
    kn = k * lax.rsqrt(jnp.sum(k * k, axis=-1, keepdims=True) + 1e-6)
    beta_f = jax.nn.sigmoid(ba)
    g_f = -jnp.exp(pa) * _softplus(ba + pd)

    def col(a, j):
        return jnp.broadcast_to(jnp.sum(jnp.where(lane == j, a, 0.0), axis=1, keepdims=True), (CHUNK, HEAD_DIM))

    beta = jnp.concatenate([col(beta_f, h) for h in range(HEADS)], axis=0)
    gs = [col(g_f, HEADS + h) for h in range(HEADS)]
    g = jnp.concatenate(gs, axis=0)
    gl = jnp.concatenate([jnp.broadcast_to(jnp.sum(gh, axis=0, keepdims=True), (CHUNK, HEAD_DIM)) for gh in gs], axis=0)
    gc = _mask_dot(lower_b, g)
    gc_rows = jnp.transpose(gc)
    decay = jnp.exp((jnp.concatenate([gc, gc], axis=1) - jnp.concatenate([gc_rows, gc_rows], axis=0)) * lower)
    egc = jnp.exp(gc)
    kb = kn * beta
    n = -(_bdot(kb, kn, NT) * decay * strict)
    if tinv is None:
        tinv = _unit_lower_inverse(lax.stop_gradient(n), eye)
    u, w = _split(_unit_lower_solve(n, jnp.concatenate([v * beta, kb * egc], axis=1), tinv), 2, 1)
    attn = _bdot(qn, kn, NT) * decay * lower
    outs = (u, w, qn * egc, kn * jnp.exp(gl - gc), attn, jnp.exp(gl))
    return outs + (tinv,) if with_inverse else outs


def _gdn_scan(states, u, w, qd, kt, attn, egl, z, nw):
    us, ws, qds, kts, egls = (_split(a, HEADS, 0) for a in (u, w, qd, kt, egl))
    vn = [us[h] - _bdot(ws[h], states[h]) for h in range(HEADS)]
    o = jnp.concatenate([_bdot(qds[h], states[h]) for h in range(HEADS)], axis=0)
    o = o + _bdot(attn, jnp.concatenate(vn, axis=0))
    new = [states[h] * jnp.concatenate([egls[h], egls[h]], axis=0) + _bdot(kts[h], vn[h], TN) for h in range(HEADS)]
    on = o * lax.rsqrt(jnp.mean(o * o, axis=-1, keepdims=True) + EPS) * nw
    return new, on * (z * jax.nn.sigmoid(z))


def _gdn_in_specs(s, chunk_of):
    nh = CHUNK // HALO
    main = [pl.BlockSpec((CHUNK, GDN_W), functools.partial(lambda col, i: (chunk_of(i), col), col))
            for col in (2, 3, 4)]
    halo = [pl.BlockSpec((HALO, GDN_W), functools.partial(lambda col, i: (jnp.maximum(chunk_of(i) * nh - 1, 0), col),
                                                         col)) for col in (2, 3, 4)]
    return main, halo


def _stk(width, chunk_of):
    return pl.BlockSpec((STACK, width), lambda i: (chunk_of(i), 0))


def _gdn_prep_fwd(proj, ba, cw, pa, pd, masks):
    s = proj.shape[0]
    nc = s // CHUNK
    main, halo = _gdn_in_specs(s, lambda i: i)

    def body(xq_ref, xk_ref, xv_ref, hq_ref, hk_ref, hv_ref, ba_ref, cw_ref, pa_ref, pd_ref, mk_ref,
             u_ref, w_ref, qd_ref, kt_ref, attn_ref, egl_ref, tinv_ref):
        i = pl.program_id(0)
        xs = [jnp.concatenate([jnp.where(i > 0, h[...], 0.0), m[...]], axis=0)
              for h, m in ((hq_ref, xq_ref), (hk_ref, xk_ref), (hv_ref, xv_ref))]
        rows = [[cw_ref[k:k + 1, j * GDN_W:(j + 1) * GDN_W] for k in range(CONV_K)] for j in range(3)]
        outs = _gdn_prep(xs[0], xs[1], xs[2], ba_ref[...], rows[0], rows[1], rows[2], pa_ref[...], pd_ref[...],
                         [mk_ref[0], mk_ref[1], mk_ref[2]], with_inverse=True)
        for ref, val in zip((u_ref, w_ref, qd_ref, kt_ref, attn_ref, egl_ref, tinv_ref), outs):
            ref[...] = val.astype(ref.dtype)

    ident = lambda i: i
    stacked = lambda dt: _sds((nc * STACK, HEAD_DIM), dt)
    return _call(body, "gdn_prep_fwd", (nc,),
                 main + halo + [_tok(CHUNK, BA_PAD), _whole(cw), _vec(BA_PAD), _vec(BA_PAD), _whole(masks)],
                 [_stk(HEAD_DIM, ident)] * 4 + [_stk(STACK, ident), _stk(HEAD_DIM, ident), _stk(STACK, ident)],
                 [stacked(F32), stacked(BF16), stacked(BF16), stacked(BF16), _sds((nc * STACK, STACK), BF16),
                  stacked(F32), _sds((nc * STACK, STACK), BF16)])(
                     proj, proj, proj, proj, proj, proj, ba, cw, pa, pd, masks)


def _gdn_prep_bwd(cts, tinv, proj, ba, cw, pa, pd, masks):
    s = proj.shape[0]
    nc = s // CHUNK
    rev = lambda i: nc - 1 - i
    main, halo = _gdn_in_specs(s, rev)

    def body(du_ref, dw_ref, dqd_ref, dkt_ref, dattn_ref, degl_ref, tinv_ref, xq_ref, xk_ref, xv_ref, hq_ref, hk_ref,
             hv_ref, ba_ref, cw_ref, pa_ref, pd_ref, mk_ref, dp_ref, dba_ref, dcw_ref, dpar_ref, carry):
        i = pl.program_id(0)
        first_chunk = rev(i) == 0

        @pl.when(i == 0)
        def _():
            carry[...] = jnp.zeros_like(carry)
            dcw_ref[...] = jnp.zeros_like(dcw_ref)
            dpar_ref[...] = jnp.zeros_like(dpar_ref)

        xs = [jnp.concatenate([jnp.where(first_chunk, 0.0, h[...]), m[...]], axis=0)
              for h, m in ((hq_ref, xq_ref), (hk_ref, xk_ref), (hv_ref, xv_ref))]
        rows = [[cw_ref[k:k + 1, j * GDN_W:(j + 1) * GDN_W] for k in range(CONV_K)] for j in range(3)]
        cst = [mk_ref[0], mk_ref[1], mk_ref[2]]
        tinv = tinv_ref[...]
        fn = lambda xq, xk, xv, b, rq, rk, rv, a, d: _gdn_prep(xq, xk, xv, b, rq, rk, rv, a, d, cst, tinv=tinv)
        _, vjp = jax.vjp(fn, xs[0], xs[1], xs[2], ba_ref[...], rows[0], rows[1], rows[2], pa_ref[...], pd_ref[...])
        dxq, dxk, dxv, dba, drq, drk, drv, dpa, dpd = vjp(
            (du_ref[...], dw_ref[...], dqd_ref[...], dkt_ref[...], dattn_ref[...], degl_ref[...]))
        dxw = jnp.concatenate([dxq, dxk, dxv], axis=1)
        tail = jnp.concatenate([jnp.zeros((CHUNK - HALO, 3 * GDN_W), F32), carry[...]], axis=0)
        dp_ref[...] = (dxw[HALO:] + tail).astype(BF16)
        carry[...] = dxw[0:HALO]
        dba_ref[...] = dba.astype(BF16)
        for j, dr in enumerate((drq, drk, drv)):
            for k in range(CONV_K):
                dcw_ref[k:k + 1, j * GDN_W:(j + 1) * GDN_W] += dr[k]
        dpar_ref[0:1, :] += dpa
        dpar_ref[1:2, :] += dpd

    acc = lambda shape: pl.BlockSpec(shape, lambda i: (0, 0))
    return _call(body, "gdn_prep_bwd", (nc,),
                 [_stk(HEAD_DIM, rev)] * 4 + [_stk(STACK, rev), _stk(HEAD_DIM, rev), _stk(STACK, rev)] + main + halo
                 + [pl.BlockSpec((CHUNK, BA_PAD), lambda i: (rev(i), 0)), _whole(cw), _vec(BA_PAD), _vec(BA_PAD),
                    _whole(masks)],
                 [pl.BlockSpec((CHUNK, 3 * GDN_W), lambda i: (rev(i), 0)),
                  pl.BlockSpec((CHUNK, BA_PAD), lambda i: (rev(i), 0)), acc((CONV_K, 3 * GDN_W)), acc((8, BA_PAD))],
                 [_sds((s, 3 * GDN_W), BF16), _sds((s, BA_PAD), BF16), _sds((CONV_K, 3 * GDN_W)), _sds((8, BA_PAD))],
                 scratch=[pltpu.VMEM((HALO, 3 * GDN_W), F32)])(
                     *cts, tinv, proj, proj, proj, proj, proj, proj, ba, cw, pa, pd, masks)


def _stack_heads(v):
    return jnp.concatenate(_split(v, HEADS, 1), axis=0)


def _unstack_heads(v):
    return jnp.concatenate(_split(v, HEADS, 0), axis=1)


def _gdn_scan_fwd(prep, proj, nw):
    s = proj.shape[0]
    nc = s // CHUNK
    ident = lambda i: i

    def body(u_ref, w_ref, qd_ref, kt_ref, attn_ref, egl_ref, z_ref, nw_ref, out_ref, st_ref, state):
        @pl.when(pl.program_id(0) == 0)
        def _():
            state[...] = jnp.zeros_like(state)

        st_ref[...] = state[...]
        states = [state[h * HEAD_DIM:(h + 1) * HEAD_DIM, :] for h in range(HEADS)]
        new, out = _gdn_scan(states, u_ref[...], w_ref[...], qd_ref[...], kt_ref[...], attn_ref[...], egl_ref[...],
                             _stack_heads(z_ref[...]), nw_ref[...])
        for h in range(HEADS):
            state[h * HEAD_DIM:(h + 1) * HEAD_DIM, :] = new[h]
        out_ref[...] = _unstack_heads(out)

    return _call(body, "gdn_scan_fwd", (nc,),
                 [_stk(HEAD_DIM, ident)] * 4 + [_stk(STACK, ident), _stk(HEAD_DIM, ident),
                                                _tok(CHUNK, GDN_W, col=5), _vec(HEAD_DIM)],
                 [_tok(CHUNK, GDN_W), pl.BlockSpec((HEADS * HEAD_DIM, HEAD_DIM), lambda i: (i, 0))],
                 [_sds((s, GDN_W)), _sds((nc * HEADS * HEAD_DIM, HEAD_DIM))],
                 scratch=[pltpu.VMEM((HEADS * HEAD_DIM, HEAD_DIM), F32)])(*prep, proj, nw)


def _gdn_scan_bwd(dout, prep, st, proj, nw):
    s = proj.shape[0]
    nc = s // CHUNK
    rev = lambda i: nc - 1 - i

    def body(do_ref, u_ref, w_ref, qd_ref, kt_ref, attn_ref, egl_ref, st_ref, z_ref, nw_ref,
             du_ref, dw_ref, dqd_ref, dkt_ref, dattn_ref, degl_ref, dz_ref, dnw_ref, dstate):
        @pl.when(pl.program_id(0) == 0)
        def _():
            dstate[...] = jnp.zeros_like(dstate)
            dnw_ref[...] = jnp.zeros_like(dnw_ref)

        states = [st_ref[h * HEAD_DIM:(h + 1) * HEAD_DIM, :] for h in range(HEADS)]
        f32 = lambda ref: ref[...].astype(F32)
        _, vjp = jax.vjp(_gdn_scan, states, u_ref[...], f32(w_ref), f32(qd_ref), f32(kt_ref), f32(attn_ref),
                         egl_ref[...], _stack_heads(z_ref[...]), nw_ref[...])
        dnew = [dstate[h * HEAD_DIM:(h + 1) * HEAD_DIM, :] for h in range(HEADS)]
        dst, du, dw, dqd, dkt, dattn, degl, dz, dnw = vjp((dnew, _stack_heads(do_ref[...])))
        for h in range(HEADS):
            dstate[h * HEAD_DIM:(h + 1) * HEAD_DIM, :] = dst[h]
        for ref, val in zip((du_ref, dw_ref, dqd_ref, dkt_ref, dattn_ref, degl_ref), (du, dw, dqd, dkt, dattn, degl)):
            ref[...] = val
        dz_ref[...] = _unstack_heads(dz).astype(BF16)
        dnw_ref[0:1, :] += dnw

    tokr = lambda n, col=0: pl.BlockSpec((CHUNK, n), lambda i: (rev(i), col))
    return _call(body, "gdn_scan_bwd", (nc,),
                 [tokr(GDN_W)] + [_stk(HEAD_DIM, rev)] * 4 + [_stk(STACK, rev), _stk(HEAD_DIM, rev),
                                                             pl.BlockSpec((HEADS * HEAD_DIM, HEAD_DIM),
                                                                          lambda i: (rev(i), 0)),
                                                             tokr(GDN_W, 5), _vec(HEAD_DIM)],
                 [_stk(HEAD_DIM, rev)] * 4 + [_stk(STACK, rev), _stk(HEAD_DIM, rev), tokr(GDN_W),
                                              pl.BlockSpec((8, HEAD_DIM), lambda i: (0, 0))],
                 [_sds((nc * STACK, HEAD_DIM))] * 4 + [_sds((nc * STACK, STACK)), _sds((nc * STACK, HEAD_DIM)),
                                                       _sds((s, GDN_W), BF16), _sds((8, HEAD_DIM))],
                 scratch=[pltpu.VMEM((HEADS * HEAD_DIM, HEAD_DIM), F32)])(dout, *prep, st, proj, nw)


def _wo_specs(l):
    half = N_DEV // 2
    return [pl.BlockSpec((half, 1, D_MODEL // N_DEV, D_MODEL), functools.partial(lambda k, *_: (k, l, 0, 0), k))
            for k in range(2)]


def _wo_half(ref):
    return ref[:, 0].reshape(ref.shape[0] * ref.shape[2], ref.shape[3])


def _out_mlp_fwd(ol, og, x, wo, g1, nw2, sc2, sh2, g2, wup, wdn, l):
    s = x.shape[0]
    t = _tile(s, 512)
    nj = wup.shape[0]
    fc = wup.shape[3]

    def body(ol_ref, og_ref, x_ref, wol_ref, wog_ref, g1_ref, nw_ref, sc_ref, sh_ref, g2_ref, wup_ref, wdn_ref,
             x1_ref, mix_ref, ff_ref, x2_ref, h2_s, acc_s):
        j = pl.program_id(1)

        @pl.when(j == 0)
        def _():
            mix = _bdot(ol_ref[...], _wo_half(wol_ref)) + _bdot(og_ref[...], _wo_half(wog_ref))
            x1 = x_ref[...] + g1_ref[...] * mix
            mix_ref[...] = mix.astype(BF16)
            x1_ref[...] = x1
            h2, _, _ = _modulated_norm(x1, nw_ref[...], sc_ref[...], sh_ref[...])
            h2_s[...] = h2.astype(BF16)
            acc_s[...] = jnp.zeros_like(acc_s)

        up = _bdot(h2_s[...], wup_ref[0, 0])
        act = jnp.square(jnp.maximum(up, 0.0))
        acc_s[...] += _bdot(act, wdn_ref[0, 0])

        @pl.when(j == nj - 1)
        def _():
            ff_ref[...] = acc_s[...].astype(BF16)
            x2_ref[...] = x1_ref[...] + g2_ref[...] * acc_s[...]

    tk = lambda n: pl.BlockSpec((t, n), lambda i, j: (i, 0))
    return _call(body, "out_mlp_fwd", (s // t, nj),
                 [tk(LRU_W), tk(GDN_W), tk(D_MODEL)] + _wo_specs(l) + [_vec(D_MODEL)] * 5
                 + [pl.BlockSpec((1, 1, D_MODEL, fc), lambda i, j: (j, l, 0, 0)),
                    pl.BlockSpec((1, 1, fc, D_MODEL), lambda i, j: (j, l, 0, 0))],
                 [tk(D_MODEL)] * 4,
                 [_sds((s, D_MODEL)), _sds((s, D_MODEL), BF16), _sds((s, D_MODEL), BF16), _sds((s, D_MODEL))],
                 scratch=[pltpu.VMEM((t, D_MODEL), BF16), pltpu.VMEM((t, D_MODEL), F32)])(
                     ol, og, x, wo, wo, g1, nw2, sc2, sh2, g2, wup, wdn)


def _mlp_bwd(dx2, x1, ff, nw2, sc2, sh2, g2, wup, wdn, l):
    s = x1.shape[0]
    t = _tile(s, 512)
    nj = wup.shape[0]
    fc = wup.shape[3]

    def body(dx2_ref, x1_ref, ff_ref, nw_ref, sc_ref, sh_ref, g2_ref, wup_ref, wdn_ref,
             act_ref, dup_ref, h2_ref, dff_ref, dx1_ref, rows_ref, dh2_s):
        i, j = pl.program_id(0), pl.program_id(1)

        @pl.when((i == 0) & (j == 0))
        def _():
            rows_ref[...] = jnp.zeros_like(rows_ref)

        @pl.when(j == 0)
        def _():
            h2, _, _ = _modulated_norm(x1_ref[...], nw_ref[...], sc_ref[...], sh_ref[...])
            h2_ref[...] = h2.astype(BF16)
            dx2 = dx2_ref[...]
            dff_ref[...] = (dx2 * g2_ref[...]).astype(BF16)
            rows_ref[2:3, :] += jnp.sum(dx2 * ff_ref[...].astype(F32), axis=0, keepdims=True)
            dh2_s[...] = jnp.zeros_like(dh2_s)

        up = _bdot(h2_ref[...], wup_ref[0, 0])
        ru = jnp.maximum(up, 0.0)
        act_ref[...] = (ru * ru).astype(BF16)
        dup = (_bdot(dff_ref[...], wdn_ref[0, 0], NT) * (2.0 * ru)).astype(BF16)
        dup_ref[...] = dup
        dh2_s[...] += _bdot(dup, wup_ref[0, 0], NT)

        @pl.when(j == nj - 1)
        def _():
            xv = x1_ref[...]
            _, n, r = _modulated_norm(xv, nw_ref[...], sc_ref[...], sh_ref[...])
            dx, dsh, dsc, dnw = _modulated_norm_bwd(dh2_s[...], xv, n, r, nw_ref[...], sc_ref[...])
            dx1_ref[...] = dx2_ref[...] + dx
            rows_ref[0:1, :] += dsh
            rows_ref[1:2, :] += dsc
            rows_ref[3:4, :] += dnw

    tk = lambda n: pl.BlockSpec((t, n), lambda i, j: (i, 0))
    tj = pl.BlockSpec((t, fc), lambda i, j: (i, j))
    return _call(body, "mlp_bwd", (s // t, nj),
                 [tk(D_MODEL)] * 3 + [_vec(D_MODEL)] * 4
                 + [pl.BlockSpec((1, 1, D_MODEL, fc), lambda i, j: (j, l, 0, 0)),
                    pl.BlockSpec((1, 1, fc, D_MODEL), lambda i, j: (j, l, 0, 0))],
                 [tj, tj, tk(D_MODEL), tk(D_MODEL), tk(D_MODEL), pl.BlockSpec((8, D_MODEL), lambda i, j: (0, 0))],
                 [_sds((s, nj * fc), BF16), _sds((s, nj * fc), BF16), _sds((s, D_MODEL), BF16),
                  _sds((s, D_MODEL), BF16), _sds((s, D_MODEL)), _sds((8, D_MODEL))],
                 scratch=[pltpu.VMEM((t, D_MODEL), F32)])(dx2, x1, ff, nw2, sc2, sh2, g2, wup, wdn)


def _outproj_bwd(dx1, mix, g1, wo, l):
    s = dx1.shape[0]
    t = _tile(s, 512)

    def body(dx1_ref, mix_ref, g1_ref, wol_ref, wog_ref, dmix_ref, dol_ref, dog_ref, rows_ref):
        @pl.when(pl.program_id(0) == 0)
        def _():
            rows_ref[...] = jnp.zeros_like(rows_ref)

        dx1v = dx1_ref[...]
        rows_ref[0:1, :] += jnp.sum(dx1v * mix_ref[...].astype(F32), axis=0, keepdims=True)
        dmix = (dx1v * g1_ref[...]).astype(BF16)
        dmix_ref[...] = dmix
        dol_ref[...] = _bdot(dmix, _wo_half(wol_ref), NT)
        dog_ref[...] = _bdot(dmix, _wo_half(wog_ref), NT)

    return _call(body, "outproj_bwd", (s // t,),
                 [_tok(t, D_MODEL), _tok(t, D_MODEL), _vec(D_MODEL)] + _wo_specs(l),
                 [_tok(t, D_MODEL), _tok(t, LRU_W), _tok(t, GDN_W), pl.BlockSpec((8, D_MODEL), lambda i: (0, 0))],
                 [_sds((s, D_MODEL), BF16), _sds((s, LRU_W)), _sds((s, GDN_W)), _sds((8, D_MODEL))])(dx1, mix, g1, wo, wo)


def _tn_matmul(a, b, name, out=None, l=0, blocked=False, row_block=0):
    s, m = a.shape
    n = b.shape[1]
    ts, bm = _tile(s, 2048), _tile(m, 1024)
    bn = next(w for w in (512, 640, 384, 256, 128) if n % w == 0)

    def body(a_ref, b_ref, *rest):
        o_ref = rest[-1]

        @pl.when(pl.program_id(2) == 0)
        def _():
            o_ref[...] = jnp.zeros_like(o_ref)

        acc = _bdot(a_ref[...], b_ref[...], TN)
        o_ref[...] += acc.reshape(o_ref.shape)

    in_specs = [pl.BlockSpec((ts, bm), lambda i, j, k: (k, i)), pl.BlockSpec((ts, bn), lambda i, j, k: (k, j))]
    grid = (m // bm, n // bn, s // ts)
    if out is None:
        return _call(body, name, grid, in_specs, pl.BlockSpec((bm, bn), lambda i, j, k: (i, j)), _sds((m, n)))(a, b)
    if blocked:
        out_spec = pl.BlockSpec((1, 1, bm, bn), lambda i, j, k: (l, j, i, 0))
    else:
        out_spec = pl.BlockSpec((1, bm, bn), lambda i, j, k: (l, i + row_block * (m // bm), j))
    return _call(body, name, grid, in_specs + [pl.BlockSpec(memory_space=pl.ANY)], out_spec,
                 _sds(out.shape), aliases={2: 0})(a, b, out)


def _final_fwd_bwd(x, target, fw):
    s = x.shape[0]
    t = _tile(s, 512)

    def body(x_ref, tg_ref, fw_ref, dx_ref, rows_ref):
        @pl.when(pl.program_id(0) == 0)
        def _():
            rows_ref[...] = jnp.zeros_like(rows_ref)

        xv = x_ref[...]
        fwv = fw_ref[...]
        r = lax.rsqrt(jnp.mean(xv * xv, axis=-1, keepdims=True) + EPS)
        err = xv * r * fwv - tg_ref[...]
        part = 0.5 * jnp.sum(jnp.mean(err * err, axis=-1, keepdims=True), axis=0, keepdims=True)
        rows_ref[1:2, :] += jnp.broadcast_to(part, (1, D_MODEL))
        dy = err * (1.0 / D_MODEL)
        rows_ref[0:1, :] += jnp.sum(dy * xv * r, axis=0, keepdims=True)
        dxn = dy * fwv
        dx_ref[...] = r * dxn - xv * (r * r * r) * jnp.mean(dxn * xv, axis=-1, keepdims=True)

    return _call(body, "final_fwd_bwd", (s // t,),
                 [_tok(t, D_MODEL), _tok(t, D_MODEL), _vec(D_MODEL)],
                 [_tok(t, D_MODEL), pl.BlockSpec((8, D_MODEL), lambda i: (0, 0))],
                 [_sds((s, D_MODEL)), _sds((8, D_MODEL))])(x, target, fw)


def _adamw(w, g, m, v):
    m = ADAM_B1 * m + (1.0 - ADAM_B1) * g
    v = ADAM_B2 * v + (1.0 - ADAM_B2) * (g * g)
    m_hat = m / (1.0 - ADAM_B1 ** ADAM_STEP)
    v_hat = v / (1.0 - ADAM_B2 ** ADAM_STEP)
    return -ADAM_LR * (m_hat / (jnp.sqrt(v_hat) + ADAM_EPS) + ADAM_WD * w), m, v


def _mod_local(c_all, wmod, bmod_cols):
    nl, _, cols = wmod.shape

    def body(c_ref, w_ref, b_ref, o_ref):
        cv = c_ref[...]
        o_ref[0] = _bdot(cv * jax.nn.sigmoid(cv), w_ref[0]) + b_ref[0]

    return _call(body, "mod_local", (nl,),
                 [_whole(c_all), pl.BlockSpec((1, D_MODEL, cols), lambda l: (l, 0, 0)),
                  pl.BlockSpec((1, 1, cols), lambda l: (l, 0, 0))],
                 pl.BlockSpec((1, N_DEV, cols), lambda l: (l, 0, 0)), _sds((nl, N_DEV, cols)))(c_all, wmod, bmod_cols)


def _wmod_update(c_all, dmod_cols, w, m, v):
    nl, _, cols = w.shape

    def body(c_ref, d_ref, w_ref, m_ref, v_ref, g_ref, dl_ref, nm_ref, nv_ref):
        cv = c_ref[...]
        g = _bdot(cv * jax.nn.sigmoid(cv), d_ref[0], TN)
        g_ref[0] = g
        dl_ref[0], nm_ref[0], nv_ref[0] = _adamw(w_ref[0], g, m_ref[0], v_ref[0])

    wspec = pl.BlockSpec((1, D_MODEL, cols), lambda l: (l, 0, 0))
    return _call(body, "wmod_update", (nl,),
                 [_whole(c_all), pl.BlockSpec((1, N_DEV, cols), lambda l: (l, 0, 0)), wspec, wspec, wspec],
                 [wspec] * 4, [_sds(w.shape)] * 4)(c_all, dmod_cols, w, m, v)


def _sum_devices(gathered):
    _, r, _ = gathered.shape

    def body(g_ref, o_ref):
        acc = g_ref[0]
        for d in range(1, N_DEV):
            acc = acc + g_ref[d]
        o_ref[...] = acc

    return _call(body, "sum_devices", (1,), [_whole(gathered)], pl.BlockSpec((r, LANES), lambda i: (0, 0)),
                 _sds((r, LANES)))(gathered)


def _adam_flat(w, g, m, v):
    r = w.shape[0]

    def body(w_ref, g_ref, m_ref, v_ref, dl_ref, nm_ref, nv_ref):
        dl_ref[...], nm_ref[...], nv_ref[...] = _adamw(w_ref[...], g_ref[...], m_ref[...], v_ref[...])

    spec = pl.BlockSpec((r, LANES), lambda i: (0, 0))
    return _call(body, "adam_small", (1,), [spec] * 4, [spec] * 3, [_sds((r, LANES))] * 3)(w, g, m, v)


def _pair_add(x, p, core):
    nl, _, r, c = x.shape
    tr = _tile(r, 128 if c > 512 else 256)

    def body(core_ref, x_ref, p_ref, o_ref):
        o_ref[...] = (x_ref[...] + p_ref[...]).astype(BF16)

    return _call(body, "pair_add", (nl, 4, r // tr),
                 [pl.BlockSpec((1, 1, tr, c), lambda l, q, i, core_ref: (l, 2 * q + core_ref[0], i, 0)),
                  pl.BlockSpec((1, 1, tr, c), lambda l, q, i, core_ref: (l, q, i, 0))],
                 pl.BlockSpec((1, 1, tr, c), lambda l, q, i, core_ref: (l, q, i, 0)), _sds((nl, 4, r, c), BF16),
                 prefetch=1)(core, x, p)


def _reduce_adam(x, p, q, place, w, m, v):
    nl, _, r, c = x.shape
    tr = _tile(r, 128 if c > 512 else 256)

    def body(place_ref, x_ref, p_ref, q_ref, w_ref, m_ref, v_ref, g_ref, dl_ref, nm_ref, nv_ref):
        g = (((x_ref[0, 0] + p_ref[0, 0]) + q_ref[0, 0].astype(F32)) + q_ref[0, 1].astype(F32)) + q_ref[0, 2].astype(F32)
        g_ref[0] = g
        dl_ref[0], nm_ref[0], nv_ref[0] = _adamw(w_ref[0], g, m_ref[0], v_ref[0])

    flat = pl.BlockSpec((1, tr, c), lambda l, i, place_ref: (l, i, 0))
    return _call(body, "reduce_adam", (nl, r // tr),
                 [pl.BlockSpec((1, 1, tr, c), lambda l, i, place_ref: (l, place_ref[0], i, 0)),
                  pl.BlockSpec((1, 1, tr, c), lambda l, i, place_ref: (l, place_ref[1], i, 0)),
                  pl.BlockSpec((1, 3, tr, c), lambda l, i, place_ref: (l, 0, i, 0)), flat, flat, flat],
                 [flat] * 4, [_sds((nl, r, c))] * 4, prefetch=1)(place, x, p, q, w, m, v)


def _place():
    return lax.axis_index("x"), lax.axis_index("y"), lax.axis_index("c")


def _all_gather(xs, name, space):
    n = len(xs)

    def body(*refs):
        x_refs, o_refs = refs[:n], refs[n:2 * n]
        send_sems, recv_sems, local_sems = refs[2 * n:]
        x, y, c = _place()
        me, sibling = (x, y, c), (x, y, 1 - c)
        chips = [(1 - x, y), (x, 1 - y), (1 - x, 1 - y)]

        def blk(a, p):
            return o_refs[a].at[4 * p[0] + 2 * p[1] + p[2]]

        def copy(a, k, block, to, src=None):
            return pltpu.make_async_remote_copy(
                src_ref=blk(a, block) if src is None else src, dst_ref=blk(a, block),
                send_sem=send_sems.at[a, k], recv_sem=recv_sems.at[a, k], device_id=to, device_id_type=MESH)

        mine = [pltpu.make_async_copy(x_refs[a], blk(a, me), local_sems.at[a]) for a in range(n)]
        for cp in mine:
            cp.start()
        first = []
        for a in range(n):
            first.append(copy(a, 0, me, sibling, src=x_refs[a]))
            first += [copy(a, 1 + j, me, (*chip, c), src=x_refs[a]) for j, chip in enumerate(chips)]
        for cp in first:
            cp.start()
        passed = []
        for j, chip in enumerate(chips):
            for a in range(n):
                copy(a, 1 + j, (*chip, c), me).wait_recv()
                cp = copy(a, 4 + j, (*chip, c), sibling)
                cp.start()
                passed.append(cp)
        for a in range(n):
            copy(a, 0, sibling, me).wait_recv()
        for j, chip in enumerate(chips):
            for a in range(n):
                copy(a, 4 + j, (*chip, 1 - c), me).wait_recv()
        for cp in first + passed:
            cp.wait_send()
        for cp in mine:
            cp.wait()

    spec = pl.BlockSpec(memory_space=space)
    return pl.pallas_call(
        body, name=name, out_shape=[_sds((N_DEV,) + a.shape, a.dtype) for a in xs],
        in_specs=[spec] * n, out_specs=[spec] * n,
        scratch_shapes=[pltpu.SemaphoreType.DMA((n, 7)), pltpu.SemaphoreType.DMA((n, 7)),
                        pltpu.SemaphoreType.DMA((n,))])(*xs)


def _pair_exchange(xs):
    n = len(xs)

    def body(*refs):
        x_refs, o_refs = refs[:n], refs[n:2 * n]
        send_sems, recv_sems = refs[2 * n:]
        x, y, c = _place()
        copies = [pltpu.make_async_remote_copy(
            src_ref=x_refs[a].at[:, 2 * q + (1 - c)], dst_ref=o_refs[a].at[:, q], send_sem=send_sems.at[a, q],
            recv_sem=recv_sems.at[a, q], device_id=(x, y, 1 - c), device_id_type=MESH)
            for a in range(n) for q in range(4)]
        for cp in copies:
            cp.start()
        for cp in copies:
            cp.wait()

    spec = pl.BlockSpec(memory_space=pl.ANY)
    return pl.pallas_call(
        body, name="pair_exchange", out_shape=[_sds((a.shape[0], 4) + a.shape[2:], a.dtype) for a in xs],
        in_specs=[spec] * n, out_specs=[spec] * n,
        scratch_shapes=[pltpu.SemaphoreType.DMA((n, 4)), pltpu.SemaphoreType.DMA((n, 4))])(*xs)


def _chip_exchange(ys):
    n = len(ys)

    def body(*refs):
        y_refs, o_refs = refs[:n], refs[n:2 * n]
        send_sems, recv_sems = refs[2 * n:]
        x, y, c = _place()
        chips = [(1 - x, y), (x, 1 - y), (1 - x, 1 - y)]
        copies = [pltpu.make_async_remote_copy(
            src_ref=y_refs[a].at[:, 2 * chip[0] + chip[1]], dst_ref=o_refs[a].at[:, r], send_sem=send_sems.at[a, r],
            recv_sem=recv_sems.at[a, r], device_id=(*chip, c), device_id_type=MESH)
            for a in range(n) for r, chip in enumerate(chips)]
        for cp in copies:
            cp.start()
        for cp in copies:
            cp.wait()

    spec = pl.BlockSpec(memory_space=pl.ANY)
    return pl.pallas_call(
        body, name="chip_exchange", out_shape=[_sds((a.shape[0], 3) + a.shape[2:], a.dtype) for a in ys],
        in_specs=[spec] * n, out_specs=[spec] * n,
        scratch_shapes=[pltpu.SemaphoreType.DMA((n, 3)), pltpu.SemaphoreType.DMA((n, 3))])(*ys)


def _size(shape):
    size = 1
    for d in shape:
        size *= d
    return size


def _slab_rows(shape):
    return -(-_size(shape) // (8 * LANES)) * 8


def _pack(arrs):
    parts = []
    for a in arrs:
        flat = a.reshape(-1).astype(F32)
        parts.append(jnp.pad(flat, (0, _slab_rows(a.shape) * LANES - flat.shape[0])).reshape(-1, LANES))
    return jnp.concatenate(parts, axis=0)


def _unpack(slab, shapes):
    out, off = [], 0
    for shp in shapes:
        rows = _slab_rows(shp)
        out.append(slab[off:off + rows].reshape(-1)[:_size(shp)].reshape(shp))
        off += rows
    return out


def _dense_blocks(w):
    eye = jnp.eye(LRU_BLOCKS, dtype=w.dtype)
    return (eye[:, None, :, None] * w[:, :, None, :]).reshape(LRU_W, LRU_W)


def _diag_blocks(dense):
    return jnp.stack([dense[g * LRU_BLOCK:(g + 1) * LRU_BLOCK, g * LRU_BLOCK:(g + 1) * LRU_BLOCK]
                      for g in range(LRU_BLOCKS)])


def _alpha_lanes(v):
    return jnp.zeros((1, BA_PAD), F32).at[0, HEADS:2 * HEADS].set(v)


def _local_step(x, target, mod, p):
    nl = mod.shape[0]
    row = lambda v: v.reshape(1, -1)
    masks = _gdn_masks()
    saved = []
    xc = x
    for l in range(nl):
        mv = [row(mod[l, k * D_MODEL:(k + 1) * D_MODEL]) for k in range(N_MOD)]
        sh1, sc1, g1, sh2, sc2, g2 = mv
        nw1, nw2 = row(p["norm_mix_w"][l]), row(p["norm_mlp_w"][l])
        wa, wx = _dense_blocks(p["lru_gate_a_w"][l]).astype(BF16), _dense_blocks(p["lru_gate_x_w"][l]).astype(BF16)
        lru_args = (p["lru_conv_w"][l], row(p["lru_conv_b"][l]), wa, wx, row(p["lru_gate_a_b"][l]),
                    row(p["lru_gate_x_b"][l]), row(p["lru_lambda"][l]), row(p["lru_norm_w"][l]))
        gdn_args = (p["gdn_conv_w"][l], _alpha_lanes(p["gdn_a_log"][l]), _alpha_lanes(p["gdn_dt_bias"][l]), masks)
        gnw = row(p["gdn_norm_w"][l])
        proj, ba = _inproj_fwd(xc, nw1, sc1, sh1, p["win"], p["wba"], l)
        ol, hs = _lru_fwd(proj, *lru_args)
        *prep, tinv = _gdn_prep_fwd(proj, ba, *gdn_args)
        og, st = _gdn_scan_fwd(prep, proj, gnw)
        x1, mix, ff, x2 = _out_mlp_fwd(ol, og, xc, p["wo"], g1, nw2, sc2, sh2, g2, p["wup"], p["wdn"], l)
        saved.append(dict(x=xc, mv=mv, nw1=nw1, nw2=nw2, lru_args=lru_args, gdn_args=gdn_args, gnw=gnw, proj=proj,
                          ba=ba, ol=ol, hs=hs, prep=prep, tinv=tinv, og=og, st=st, x1=x1, mix=mix, ff=ff))
        xc = x2

    dx, frows = _final_fwd_bwd(xc, target, row(p["final_norm_w"]))
    loss_part = frows[1, 0]
    small = {k: [None] * nl for k in ("norm_mix_w", "norm_mlp_w", "lru_conv_w", "lru_conv_b", "lru_gate_a_w",
                                      "lru_gate_a_b", "lru_gate_x_w", "lru_gate_x_b", "lru_lambda", "lru_norm_w",
                                      "gdn_conv_w", "gdn_a_log", "gdn_dt_bias", "gdn_norm_w")}
    fc = D_FF // N_DEV
    g_in = [None] * nl
    g_out, g_down = lax.empty((nl, D_MODEL, D_MODEL), F32), lax.empty((nl, D_FF, D_MODEL), F32)
    g_up = lax.empty((nl, N_DEV, D_MODEL, fc), F32)
    dmod = [None] * nl
    for l in reversed(range(nl)):
        sv = saved[l]
        sh1, sc1, g1, sh2, sc2, g2 = sv["mv"]
        act, dup, h2b, dffb, dx1, rows2 = _mlp_bwd(dx, sv["x1"], sv["ff"], sv["nw2"], sc2, sh2, g2, p["wup"],
                                                   p["wdn"], l)
        g_up = _tn_matmul(h2b, dup, "grad_w_up", out=g_up, l=l, blocked=True)
        g_down = _tn_matmul(act, dffb, "grad_w_down", out=g_down, l=l)
        dmix, dol, dog, rows1 = _outproj_bwd(dx1, sv["mix"], g1, p["wo"], l)
        g_out = _tn_matmul(sv["ol"], dmix, "grad_w_out_lru", out=g_out, l=l)
        g_out = _tn_matmul(sv["og"], dmix, "grad_w_out_gdn", out=g_out, l=l, row_block=1)
        dpl, dwa, dwx, lrows = _lru_bwd(dol, sv["proj"], sv["hs"], *sv["lru_args"])
        *cts, dpz, gnrow = _gdn_scan_bwd(dog, sv["prep"], sv["st"], sv["proj"], sv["gnw"])
        dpq, dba, dcw, dpar = _gdn_prep_bwd(cts, sv["tinv"], sv["proj"], sv["ba"], *sv["gdn_args"])
        dx, hb, rows0 = _inproj_bwd(dpl, dpq, dpz, dba, sv["x"], dx1, sv["nw1"], sc1, sh1, p["win"], p["wba"], l)
        dproj = jnp.concatenate([dpl, dpq, dpz, dba], axis=1)
        g_in[l] = jnp.transpose(_tn_matmul(hb, dproj, "grad_w_in")[:, :IN_COLS].reshape(
            D_MODEL, N_DEV, IN_COLS // N_DEV), (1, 0, 2))
        dmod[l] = jnp.concatenate([rows0[0], rows0[1], rows1[0], rows2[0], rows2[1], rows2[2]])
        small["norm_mix_w"][l], small["norm_mlp_w"][l] = rows0[2], rows2[3]
        small["lru_conv_w"][l], small["lru_conv_b"][l] = lrows[8:8 + CONV_K], lrows[0]
        small["lru_gate_a_w"][l], small["lru_gate_x_w"][l] = _diag_blocks(dwa), _diag_blocks(dwx)
        small["lru_gate_a_b"][l], small["lru_gate_x_b"][l] = lrows[1], lrows[2]
        small["lru_lambda"][l], small["lru_norm_w"][l] = lrows[3], lrows[4]
        small["gdn_conv_w"][l] = dcw
        small["gdn_a_log"][l], small["gdn_dt_bias"][l] = dpar[0, HEADS:2 * HEADS], dpar[1, HEADS:2 * HEADS]
        small["gdn_norm_w"][l] = gnrow[0]
    small = {k: jnp.stack(v) for k, v in small.items()}
    small["final_norm_w"] = frows[0]
    big = dict(w_in=jnp.stack(g_in), w_out=g_out.reshape(nl, N_DEV, D_MODEL // N_DEV, D_MODEL), w_up=g_up,
               w_down=g_down.reshape(nl, N_DEV, fc, D_MODEL))
    return loss_part, dx, big, small, jnp.stack(dmod)


SMALL_REPLICATED = ("norm_mix_w", "norm_mlp_w", "b_mod", "lru_conv_b", "lru_gate_a_w", "lru_gate_a_b", "lru_gate_x_w",
                    "lru_gate_x_b", "lru_lambda", "lru_norm_w", "gdn_a_log", "gdn_dt_bias", "gdn_norm_w",
                    "final_norm_w")
SMALL_SHARDED = ("lru_conv_w", "gdn_conv_w")
WEIGHT_ORDER = ("norm_mix_w", "norm_mlp_w", "w_mod", "b_mod", "w_in", "lru_conv_w", "lru_conv_b", "lru_gate_a_w",
                "lru_gate_a_b", "lru_gate_x_w", "lru_gate_x_b", "lru_lambda", "lru_norm_w", "gdn_conv_w", "gdn_a_log",
                "gdn_dt_bias", "gdn_norm_w", "w_out", "w_up", "w_down", "final_norm_w")


def kernel(x, c, norm_mix_w, norm_mlp_w, w_mod, b_mod, w_in, lru_conv_w, lru_conv_b, lru_gate_a_w, lru_gate_a_b, lru_gate_x_w, lru_gate_x_b, lru_lambda, lru_norm_w, gdn_conv_w, gdn_a_log, gdn_dt_bias, gdn_norm_w, w_out, w_up, w_down, final_norm_w, loss_target, m_norm_mix_w, m_norm_mlp_w, m_w_mod, m_b_mod, m_w_in, m_lru_conv_w, m_lru_conv_b, m_lru_gate_a_w, m_lru_gate_a_b, m_lru_gate_x_w, m_lru_gate_x_b, m_lru_lambda, m_lru_norm_w, m_gdn_conv_w, m_gdn_a_log, m_gdn_dt_bias, m_gdn_norm_w, m_w_out, m_w_up, m_w_down, m_final_norm_w, v_norm_mix_w, v_norm_mlp_w, v_w_mod, v_b_mod, v_w_in, v_lru_conv_w, v_lru_conv_b, v_lru_gate_a_w, v_lru_gate_a_b, v_lru_gate_x_w, v_lru_gate_x_b, v_lru_lambda, v_lru_norm_w, v_gdn_conv_w, v_gdn_a_log, v_gdn_dt_bias, v_gdn_norm_w, v_w_out, v_w_up, v_w_down, v_final_norm_w):
    args = dict(locals())
    w = {k: args[k] for k in WEIGHT_ORDER}
    mom = {k: args["m_" + k] for k in WEIGHT_ORDER}
    var = {k: args["v_" + k] for k in WEIGHT_ORDER}
    nl = w_in.shape[0]
    px, py, pc = _place()
    me = 4 * px + 2 * py + pc
    core = jnp.reshape(pc, (1,)).astype(jnp.int32)

    shapes0 = [c.shape, lru_conv_w.shape, gdn_conv_w.shape]
    (g0,) = _all_gather([_pack([c, lru_conv_w, gdn_conv_w])], "gather_cond", pltpu.VMEM)
    per_dev = [_unpack(g0[d], shapes0) for d in range(N_DEV)]
    c_all = jnp.concatenate([pd[0] for pd in per_dev], axis=0)
    lru_conv_full = jnp.concatenate([pd[1] for pd in per_dev], axis=-1)
    gdn_conv_full = jnp.concatenate([pd[2] for pd in per_dev], axis=-1)

    cols = w_mod.shape[2]
    bmod_cols = lax.dynamic_slice_in_dim(b_mod, me * cols, cols, axis=1).reshape(nl, 1, cols)
    mod_cols = _mod_local(c_all, w_mod, bmod_cols)
    (g1,) = _all_gather([mod_cols.reshape(nl * N_DEV, cols)], "gather_mod", pltpu.VMEM)
    g1 = g1.reshape(N_DEV, nl, N_DEV, cols)
    mod = jnp.transpose(lax.dynamic_index_in_dim(g1, me, axis=2, keepdims=False), (1, 0, 2)).reshape(nl, N_DEV * cols)

    gin, gout, gup, gdn = _all_gather([w_in.astype(BF16), w_out.astype(BF16), w_up.astype(BF16),
                                       w_down.astype(BF16)], "gather_weights", pl.ANY)
    win_full = jnp.transpose(gin, (1, 2, 0, 3)).reshape(nl, D_MODEL, IN_COLS)
    p = dict(w)
    p["win"] = win_full
    p["wba"] = jnp.pad(win_full[:, :, IN_MAIN:], ((0, 0), (0, 0), (0, BA_PAD - (IN_COLS - IN_MAIN))))
    p["wo"], p["wup"], p["wdn"] = gout, gup, gdn
    p["lru_conv_w"], p["gdn_conv_w"] = lru_conv_full, gdn_conv_full

    loss_part, grad_x, big, small, dmod = _local_step(x[0], loss_target[0], mod, p)
    loss = lax.psum(loss_part, MESH_AXES)

    small_names = sorted(small)
    slab = _pack([dmod] + [small[k] for k in small_names])
    (gs,) = _all_gather([slab], "gather_small_grads", pltpu.VMEM)
    dmod_all = gs[:, :_slab_rows(dmod.shape)].reshape(N_DEV, nl, N_MOD * D_MODEL)
    summed = _unpack(_sum_devices(gs), [dmod.shape] + [small[k].shape for k in small_names])
    grads = dict(zip(small_names, summed[1:]))
    grads["b_mod"] = summed[0]
    for k, width in (("lru_conv_w", LRU_W // N_DEV), ("gdn_conv_w", 3 * GDN_W // N_DEV)):
        grads[k] = lax.dynamic_slice_in_dim(grads[k], me * width, width, axis=2)
    names = SMALL_REPLICATED + SMALL_SHARDED
    shapes = [w[k].shape for k in names]
    dl, nm, nv = _adam_flat(_pack([w[k] for k in names]), _pack([grads[k] for k in names]),
                            _pack([mom[k] for k in names]), _pack([var[k] for k in names]))
    delta = dict(zip(names, _unpack(dl, shapes)))
    new_m = dict(zip(names, _unpack(nm, shapes)))
    new_v = dict(zip(names, _unpack(nv, shapes)))

    dmod_cols = jnp.transpose(lax.dynamic_slice_in_dim(dmod_all, me * cols, cols, axis=2), (1, 0, 2))
    grads["w_mod"], delta["w_mod"], new_m["w_mod"], new_v["w_mod"] = _wmod_update(
        c_all, dmod_cols, w_mod, m_w_mod, v_w_mod)

    order = ("w_in", "w_out", "w_up", "w_down")
    xs = [big[k] for k in order]
    ps = _pair_exchange(xs)
    ys = [_pair_add(xk, pk, core) for xk, pk in zip(xs, ps)]
    qs = _chip_exchange(ys)
    place = jnp.stack([me, 2 * px + py]).astype(jnp.int32)
    for k, xk, pk, qk in zip(order, xs, ps, qs):
        grads[k], delta[k], new_m[k], new_v[k] = _reduce_adam(xk, pk, qk, place, w[k], mom[k], var[k])

    return (loss, grad_x[None], *[grads[k] for k in WEIGHT_ORDER], *[delta[k] for k in WEIGHT_ORDER],
            *[new_m[k] for k in WEIGHT_ORDER], *[new_v[k] for k in WEIGHT_ORDER])
```

```python
import functools

import jax
import jax.numpy as jnp
from jax import lax
from jax.experimental import pallas as pl
from jax.experimental.pallas import tpu as pltpu

F32 = jnp.float32
BF16 = jnp.bfloat16

D_MODEL = 1024
LRU_W = 512
LRU_BLOCKS = 8
LRU_BLOCK = 64
LRU_C = 8.0
GDN_W = 512
HEADS = 4
HEAD_DIM = 128
CHUNK = 64
STACK = HEADS * CHUNK
CONV_K = 4
D_FF = 4096
N_MOD = 6
IN_COLS = 3080
IN_MAIN = 3072
BA_PAD = 128
EPS = 1e-6
N_DEV = 8
HALO = 8
LANES = 128
ADAM_LR, ADAM_B1, ADAM_B2, ADAM_EPS, ADAM_WD, ADAM_STEP = 0.001, 0.9, 0.999, 1e-08, 0.01, 10
MESH_AXES = ("x", "y", "c")
MESH = pl.DeviceIdType.MESH

NN = (((1,), (0,)), ((), ()))
NT = (((1,), (1,)), ((), ()))
TN = (((0,), (0,)), ((), ()))


def _bdot(a, b, dims=NN):
    return lax.dot_general(a.astype(BF16), b.astype(BF16), dims, preferred_element_type=F32)


def _sdot(a, b, dims=NN):
    ah, bh = a.astype(BF16), b.astype(BF16)
    al, bl = (a - ah.astype(F32)).astype(BF16), (b - bh.astype(F32)).astype(BF16)
    return _bdot(ah, bh, dims) + (_bdot(al, bh, dims) + _bdot(ah, bl, dims))


def _hdot(a, b, dims=NN):
    return lax.dot_general(a, b, dims, precision=lax.Precision.HIGHEST, preferred_element_type=F32)


def _sds(shape, dtype=F32):
    return jax.ShapeDtypeStruct(tuple(shape), dtype)


def _tile(n, t):
    return min(n, t)


def _call(body, name, grid, in_specs, out_specs, out_shape, scratch=(), vmem_mb=48, prefetch=0, aliases=None):
    params = pltpu.CompilerParams(dimension_semantics=("arbitrary",) * len(grid), vmem_limit_bytes=vmem_mb * 2**20)
    if prefetch:
        spec = pltpu.PrefetchScalarGridSpec(num_scalar_prefetch=prefetch, grid=grid, in_specs=in_specs,
                                            out_specs=out_specs, scratch_shapes=list(scratch))
        return pl.pallas_call(body, name=name, grid_spec=spec, out_shape=out_shape, compiler_params=params)
    return pl.pallas_call(body, name=name, grid=grid, in_specs=in_specs, out_specs=out_specs, out_shape=out_shape,
                          scratch_shapes=list(scratch), compiler_params=params, input_output_aliases=aliases or {})


def _tok(t, n, col=0):
    return pl.BlockSpec((t, n), lambda i, *_: (i, col))


def _vec(n):
    return pl.BlockSpec((1, n), lambda *_: (0, 0))


def _whole(a):
    nd = a.ndim
    return pl.BlockSpec(a.shape, lambda *_: (0,) * nd)


def _layer(l, *dims):
    return pl.BlockSpec((1,) + dims, lambda *_: (l,) + (0,) * len(dims))


def _gelu(y):
    c0, c1 = 0.7978845608028654, 0.044715
    return 0.5 * y * (1.0 + jnp.tanh(c0 * (y + c1 * y * y * y)))


def _gelu_grad(y):
    c0, c1 = 0.7978845608028654, 0.044715
    t = jnp.tanh(c0 * (y + c1 * y * y * y))
    return 0.5 * (1.0 + t) + 0.5 * y * (1.0 - t * t) * c0 * (1.0 + 3.0 * c1 * y * y)


def _softplus(v):
    return jnp.maximum(v, 0.0) + jnp.log(1.0 + jnp.exp(-jnp.where(v > 0, v, -v)))


@functools.partial(jax.custom_vjp, nondiff_argnums=(1,))
def _roll_rows(v, s):
    s = s % v.shape[0]
    return pltpu.roll(v, s, axis=0) if s else v


def _roll_rows_fwd(v, s):
    return _roll_rows(v, s), None


def _roll_rows_bwd(s, _, g):
    return (_roll_rows(g, -s),)


_roll_rows.defvjp(_roll_rows_fwd, _roll_rows_bwd)


@jax.custom_vjp
def _drop_halo(v):
    return v[HALO:]


def _drop_halo_fwd(v):
    return v[HALO:], None


def _drop_halo_bwd(_, g):
    return (jnp.concatenate([jnp.zeros((HALO, g.shape[1]), g.dtype), g], axis=0),)


_drop_halo.defvjp(_drop_halo_fwd, _drop_halo_bwd)


@functools.partial(jax.custom_vjp, nondiff_argnums=(1, 2))
def _split(v, n, axis):
    w = v.shape[axis] // n
    return tuple(lax.slice_in_dim(v, k * w, (k + 1) * w, axis=axis) for k in range(n))


def _split_fwd(v, n, axis):
    return _split(v, n, axis), None


def _split_bwd(n, axis, _, gs):
    return (jnp.concatenate(list(gs), axis=axis),)


_split.defvjp(_split_fwd, _split_bwd)


def _conv_taps(xw):
    return [_drop_halo(_roll_rows(xw, CONV_K - 1 - k)) for k in range(CONV_K)]


def _modulated_norm(xv, nw, sc, sh):
    r = lax.rsqrt(jnp.mean(xv * xv, axis=-1, keepdims=True) + EPS)
    n = xv * r * nw
    return n * (1.0 + sc) + sh, n, r


def _modulated_norm_bwd(dh, xv, n, r, nw, sc):
    dn = dh * (1.0 + sc)
    dxn = dn * nw
    dx = r * dxn - xv * (r * r * r) * jnp.mean(dxn * xv, axis=-1, keepdims=True)
    return (dx, jnp.sum(dh, axis=0, keepdims=True), jnp.sum(dh * n, axis=0, keepdims=True),
            jnp.sum(dn * xv * r, axis=0, keepdims=True))


def _inproj_fwd(x, nw, sc, sh, win, wba, l):
    s = x.shape[0]
    t = _tile(s, 256)

    def body(x_ref, nw_ref, sc_ref, sh_ref, win_ref, wba_ref, proj_ref, ba_ref):
        h, _, _ = _modulated_norm(x_ref[...], nw_ref[...], sc_ref[...], sh_ref[...])
        hb = h.astype(BF16)
        proj_ref[...] = _bdot(hb, win_ref[0])
        ba_ref[...] = _bdot(hb, wba_ref[0])

    return _call(body, "inproj_fwd", (s // t,),
                 [_tok(t, D_MODEL), _vec(D_MODEL), _vec(D_MODEL), _vec(D_MODEL), _layer(l, D_MODEL, IN_MAIN),
                  _layer(l, D_MODEL, BA_PAD)],
                 [_tok(t, IN_MAIN), _tok(t, BA_PAD)],
                 [_sds((s, IN_MAIN)), _sds((s, BA_PAD))])(x, nw, sc, sh, win, wba)


def _inproj_bwd(dpl, dpq, dpz, dba, x, dx1, nw, sc, sh, win, wba, l):
    s = x.shape[0]
    t = _tile(s, 256)

    def body(dpl_ref, dpq_ref, dpz_ref, dba_ref, x_ref, dx1_ref, nw_ref, sc_ref, sh_ref, win_ref, wba_ref,
             dx_ref, hb_ref, acc_ref):
        @pl.when(pl.program_id(0) == 0)
        def _():
            acc_ref[...] = jnp.zeros_like(acc_ref)

        dh = (_bdot(dpl_ref[...], win_ref[0, :, 0:2 * LRU_W], NT)
              + _bdot(dpq_ref[...], win_ref[0, :, 2 * LRU_W:2 * LRU_W + 3 * GDN_W], NT)
              + _bdot(dpz_ref[...], win_ref[0, :, 2 * LRU_W + 3 * GDN_W:IN_MAIN], NT)
              + _bdot(dba_ref[...], wba_ref[0], NT))
        xv = x_ref[...]
        h, n, r = _modulated_norm(xv, nw_ref[...], sc_ref[...], sh_ref[...])
        hb_ref[...] = h.astype(BF16)
        dx, dsh, dsc, dnw = _modulated_norm_bwd(dh, xv, n, r, nw_ref[...], sc_ref[...])
        dx_ref[...] = dx1_ref[...] + dx
        acc_ref[0:1, :] += dsh
        acc_ref[1:2, :] += dsc
        acc_ref[2:3, :] += dnw

    return _call(body, "inproj_bwd", (s // t,),
                 [_tok(t, 2 * LRU_W), _tok(t, 3 * GDN_W), _tok(t, GDN_W), _tok(t, BA_PAD), _tok(t, D_MODEL),
                  _tok(t, D_MODEL), _vec(D_MODEL), _vec(D_MODEL), _vec(D_MODEL), _layer(l, D_MODEL, IN_MAIN),
                  _layer(l, D_MODEL, BA_PAD)],
                 [_tok(t, D_MODEL), _tok(t, D_MODEL), pl.BlockSpec((8, D_MODEL), lambda i: (0, 0))],
                 [_sds((s, D_MODEL)), _sds((s, D_MODEL), BF16), _sds((8, D_MODEL))])(
                     dpl, dpq, dpz, dba, x, dx1, nw, sc, sh, win, wba)


def _lru_gates(xw, cw_rows, cb, wa, wx, gab, gxb, lam):
    taps = _conv_taps(xw)
    xr = cb + cw_rows[0] * taps[0] + cw_rows[1] * taps[1] + cw_rows[2] * taps[2] + cw_rows[3] * taps[3]
    xb = xr.astype(BF16)
    r = jax.nn.sigmoid(_bdot(xb, wa) + gab)
    i = jax.nn.sigmoid(_bdot(xb, wx) + gxb)
    z = jnp.exp(-jnp.where(lam > 0, lam, -lam))
    w1 = 1.0 + z
    log1p_z = jnp.where(w1 == 1.0, z, jnp.log(w1) * z / (w1 - 1.0))
    ls = jnp.minimum(lam, 0.0) - log1p_z
    la = LRU_C * r * ls
    a = jnp.exp(la)
    x2 = 2.0 * la
    u = jnp.exp(x2)
    mm_raw = jnp.where(u == 1.0, -x2,
                       jnp.where(x2 < -30.0, 1.0, (1.0 - u) * x2 / jnp.log(jnp.maximum(u, 1e-30))))
    mult = jnp.sqrt(jnp.maximum(mm_raw, 1e-12))
    return dict(taps=taps, xr=xr, r=r, i=i, ls=ls, a=a, mm_raw=mm_raw, mult=mult)


def _lru_specs(s, t, tile_of):
    nh = t // HALO
    xl = pl.BlockSpec((t, LRU_W), lambda i: (tile_of(i), 0))
    yl = pl.BlockSpec((t, LRU_W), lambda i: (tile_of(i), 1))
    hx = pl.BlockSpec((HALO, LRU_W), lambda i: (jnp.maximum(tile_of(i) * nh - 1, 0), 0))
    return xl, yl, hx


def _lru_fwd(proj, cw, cb, wa, wx, gab, gxb, lam, lnw):
    s = proj.shape[0]
    t = _tile(s, 256)
    xl, yl, hx = _lru_specs(s, t, lambda i: i)

    def body(xl_ref, yl_ref, hx_ref, cw_ref, cb_ref, wa_ref, wx_ref, gab_ref, gxb_ref, lam_ref, lnw_ref,
             out_ref, h_ref, a_s, b_s, hc):
        i = pl.program_id(0)

        @pl.when(i == 0)
        def _():
            hc[...] = jnp.zeros_like(hc)

        halo = jnp.where(i > 0, hx_ref[...], 0.0)
        xw = jnp.concatenate([halo, xl_ref[...]], axis=0)
        g = _lru_gates(xw, [cw_ref[k:k + 1, :] for k in range(CONV_K)], cb_ref[...], wa_ref[...], wx_ref[...],
                       gab_ref[...], gxb_ref[...], lam_ref[...])
        a_s[...] = g["a"]
        b_s[...] = g["mult"] * (g["i"] * g["xr"])

        def step(k, h):
            h = a_s[pl.ds(k, 1), :] * h + b_s[pl.ds(k, 1), :]
            h_ref[pl.ds(k, 1), :] = h
            return h

        hc[...] = lax.fori_loop(0, t, step, hc[...], unroll=8)
        m = h_ref[...] * _gelu(yl_ref[...])
        out_ref[...] = m * lax.rsqrt(jnp.mean(m * m, axis=-1, keepdims=True) + EPS) * lnw_ref[...]

    return _call(body, "lru_fwd", (s // t,),
                 [xl, yl, hx, _whole(cw), _vec(LRU_W), _whole(wa), _whole(wx)] + [_vec(LRU_W)] * 4,
                 [_tok(t, LRU_W), _tok(t, LRU_W)],
                 [_sds((s, LRU_W)), _sds((s, LRU_W))],
                 scratch=[pltpu.VMEM((t, LRU_W), F32), pltpu.VMEM((t, LRU_W), F32), pltpu.VMEM((1, LRU_W), F32)])(
                     proj, proj, proj, cw, cb, wa, wx, gab, gxb, lam, lnw)


def _lru_bwd(dout, proj, hs, cw, cb, wa, wx, gab, gxb, lam, lnw):
    s = proj.shape[0]
    t = _tile(s, 256)
    nt = s // t
    rev = lambda i: nt - 1 - i
    xl, yl, hx = _lru_specs(s, t, rev)
    nh = t // HALO
    tk = pl.BlockSpec((t, LRU_W), lambda i: (rev(i), 0))
    hh = pl.BlockSpec((HALO, LRU_W), lambda i: (jnp.maximum(rev(i) * nh - 1, 0), 0))

    def body(do_ref, xl_ref, yl_ref, hx_ref, h_ref, hh_ref, cw_ref, cb_ref, wa_ref, wx_ref, gab_ref, gxb_ref,
             lam_ref, lnw_ref, dp_ref, dwa_ref, dwx_ref, rows_ref, dh_s, dhd_s, carry, dxr_next):
        i = pl.program_id(0)
        first_tile = rev(i) == 0

        @pl.when(i == 0)
        def _():
            carry[...] = jnp.zeros_like(carry)
            dxr_next[...] = jnp.zeros_like(dxr_next)
            dwa_ref[...] = jnp.zeros_like(dwa_ref)
            dwx_ref[...] = jnp.zeros_like(dwx_ref)
            rows_ref[...] = jnp.zeros_like(rows_ref)

        halo = jnp.where(first_tile, 0.0, hx_ref[...])
        xw = jnp.concatenate([halo, xl_ref[...]], axis=0)
        cw_rows = [cw_ref[k:k + 1, :] for k in range(CONV_K)]
        lam_v = lam_ref[...]
        g = _lru_gates(xw, cw_rows, cb_ref[...], wa_ref[...], wx_ref[...], gab_ref[...], gxb_ref[...], lam_v)
        a, r, gi, xr, mult = g["a"], g["r"], g["i"], g["xr"], g["mult"]
        hv = h_ref[...]
        yv = yl_ref[...]
        gl = _gelu(yv)
        m = hv * gl
        rn = lax.rsqrt(jnp.mean(m * m, axis=-1, keepdims=True) + EPS)
        dov = do_ref[...]
        dmn = dov * lnw_ref[...]
        rows_ref[4:5, :] += jnp.sum(dov * m * rn, axis=0, keepdims=True)
        dm = rn * dmn - m * (rn * rn * rn) * jnp.mean(dmn * m, axis=-1, keepdims=True)
        dhd_s[...] = dm * gl
        dy = dm * hv * _gelu_grad(yv)
        dh_s[...] = a

        def step(k, c):
            row = t - 1 - k
            d = dhd_s[pl.ds(row, 1), :] + c
            c = dh_s[pl.ds(row, 1), :] * d
            dh_s[pl.ds(row, 1), :] = d
            return c

        carry[...] = lax.fori_loop(0, t, step, carry[...], unroll=8)
        dH = dh_s[...]
        hprev_halo = jnp.where(first_tile, 0.0, hh_ref[...])
        hprev = _drop_halo(_roll_rows(jnp.concatenate([hprev_halo, hv], axis=0), 1))
        da = dH * hprev
        dmult = dH * gi * xr
        di = dH * mult * xr
        dxr = dH * mult * gi
        dla = jnp.where(g["mm_raw"] > 1e-12, dmult * (0.5 / mult) * (-2.0 * a * a), 0.0) + da * a
        dr = dla * (LRU_C * g["ls"])
        sig_neg = jax.nn.sigmoid(-lam_v)
        rows_ref[3:4, :] += jnp.sum(dla * (LRU_C * r), axis=0, keepdims=True) * sig_neg
        drp = dr * r * (1.0 - r)
        dip = di * gi * (1.0 - gi)
        rows_ref[1:2, :] += jnp.sum(drp, axis=0, keepdims=True)
        rows_ref[2:3, :] += jnp.sum(dip, axis=0, keepdims=True)
        xb = xr.astype(BF16)
        drb = drp.astype(BF16)
        dib = dip.astype(BF16)
        dwa_ref[...] += _bdot(xb, drb, TN)
        dwx_ref[...] += _bdot(xb, dib, TN)
        dxr = dxr + _bdot(drb, wa_ref[...], NT) + _bdot(dib, wx_ref[...], NT)
        rows_ref[0:1, :] += jnp.sum(dxr, axis=0, keepdims=True)
        ext = jnp.concatenate([dxr, dxr_next[...]], axis=0)
        dx = cw_rows[CONV_K - 1] * dxr
        for k in range(CONV_K - 1):
            dx = dx + cw_rows[k] * _roll_rows(ext, -(CONV_K - 1 - k))[0:t]
        for k in range(CONV_K):
            rows_ref[8 + k:9 + k, :] += jnp.sum(dxr * g["taps"][k], axis=0, keepdims=True)
        dxr_next[...] = dxr[0:HALO]
        dp_ref[...] = jnp.concatenate([dx, dy], axis=1).astype(BF16)

    acc = lambda shape: pl.BlockSpec(shape, lambda i: (0, 0))
    return _call(body, "lru_bwd", (nt,),
                 [tk, xl, yl, hx, tk, hh, _whole(cw), _vec(LRU_W), _whole(wa), _whole(wx)] + [_vec(LRU_W)] * 4,
                 [pl.BlockSpec((t, 2 * LRU_W), lambda i: (rev(i), 0)), acc((LRU_W, LRU_W)), acc((LRU_W, LRU_W)),
                  acc((16, LRU_W))],
                 [_sds((s, 2 * LRU_W), BF16), _sds((LRU_W, LRU_W)), _sds((LRU_W, LRU_W)), _sds((16, LRU_W))],
                 scratch=[pltpu.VMEM((t, LRU_W), F32), pltpu.VMEM((t, LRU_W), F32), pltpu.VMEM((1, LRU_W), F32),
                          pltpu.VMEM((HALO, LRU_W), F32)])(
                     dout, proj, proj, proj, hs, hs, cw, cb, wa, wx, gab, gxb, lam, lnw)


def _gdn_masks():
    row = lax.broadcasted_iota(jnp.int32, (STACK, STACK), 0)
    col = lax.broadcasted_iota(jnp.int32, (STACK, STACK), 1)
    same = (row // CHUNK) == (col // CHUNK)
    return jnp.stack([(same & (col <= row)).astype(F32), (same & (col < row)).astype(F32), (row == col).astype(F32)])


def _conv_silu(xw, rows):
    taps = _conv_taps(xw)
    y = rows[0] * taps[0] + rows[1] * taps[1] + rows[2] * taps[2] + rows[3] * taps[3]
    return y * jax.nn.sigmoid(y)


def _split3(v):
    hi = v.astype(BF16)
    r1 = v - hi.astype(F32)
    mid = r1.astype(BF16)
    return hi, mid, (r1 - mid.astype(F32)).astype(BF16)


def _mask_dot_raw(mask, v, dims):
    parts = _split3(v)
    d = lambda p: lax.dot_general(mask, p, dims, preferred_element_type=F32)
    return d(parts[0]) + (d(parts[1]) + d(parts[2]))


@jax.custom_vjp
def _mask_dot(mask, v):
    return _mask_dot_raw(mask, v, NN)


def _mask_dot_fwd(mask, v):
    return _mask_dot_raw(mask, v, NN), mask


def _mask_dot_bwd(mask, ct):
    return jnp.zeros_like(mask), _mask_dot_raw(mask, ct, TN)


_mask_dot.defvjp(_mask_dot_fwd, _mask_dot_bwd)


def _unit_lower_inverse(n, eye):
    tinv = eye + n
    p = n
    for _ in range(5):
        p = _bdot(p, p)
        tinv = tinv + _bdot(tinv, p)
    return tinv.astype(BF16)


@jax.custom_vjp
def _unit_lower_solve(n, rhs, tinv):
    x0 = _bdot(tinv, rhs)
    return x0 + _bdot(tinv, rhs - x0 + _sdot(n, x0))


def _unit_lower_solve_fwd(n, rhs, tinv):
    x = _unit_lower_solve(n, rhs, tinv)
    return x, (n, tinv, x)


def _unit_lower_solve_bwd(res, ct):
    n, tinv, x = res
    y0 = _bdot(tinv, ct, TN)
    y = y0 + _bdot(tinv, ct - y0 + _sdot(n, y0, TN), TN)
    return _bdot(y, x, NT), y, jnp.zeros_like(tinv)


_unit_lower_solve.defvjp(_unit_lower_solve_fwd, _unit_lower_solve_bwd)


def _gdn_prep(xq, xk, xv, ba, cwq, cwk, cwv, pa, pd, masks, tinv=None, with_inverse=False):
    lower, strict, eye = masks[0], masks[1], masks[2]
    lower_b = lower.astype(BF16)
    lane = lax.broadcasted_iota(jnp.int32, (CHUNK, LANES), 1)
    q = jnp.concatenate(_split(_conv_silu(xq, cwq), HEADS, 1), axis=0)
    k = jnp.concatenate(_split(_conv_silu(xk, cwk), HEADS, 1), axis=0)
    v = jnp.concatenate(_split(_conv_silu(xv, cwv), HEADS, 1), axis=0)
    qn = q * lax.rsqrt(jnp.sum(q * q, axis=-1, keepdims=True) + 1e-6) * (HEAD_DIM ** -0.5)
    kn = k * lax.rsqrt(jnp.sum(k * k, axis=-1, keepdims=True) + 1e-6)
    beta_f = jax.nn.sigmoid(ba)
    g_f = -jnp.exp(pa) * _softplus(ba + pd)

    def col(a, j):
        return jnp.broadcast_to(jnp.sum(jnp.where(lane == j, a, 0.0), axis=1, keepdims=True), (CHUNK, HEAD_DIM))

    beta = jnp.concatenate([col(beta_f, h) for h in range(HEADS)], axis=0)
    gs = [col(g_f, HEADS + h) for h in range(HEADS)]
    g = jnp.concatenate(gs, axis=0)
    gl = jnp.concatenate([jnp.broadcast_to(jnp.sum(gh, axis=0, keepdims=True), (CHUNK, HEAD_DIM)) for gh in gs], axis=0)
    gc = _mask_dot(lower_b, g)
    gc_rows = jnp.transpose(gc)
    decay = jnp.exp((jnp.concatenate([gc, gc], axis=1) - jnp.concatenate([gc_rows, gc_rows], axis=0)) * lower)
    egc = jnp.exp(gc)
    kb = kn * beta
    n = -(_bdot(kb, kn, NT) * decay * strict)
    if tinv is None:
        tinv = _unit_lower_inverse(lax.stop_gradient(n), eye)
    u, w = _split(_unit_lower_solve(n, jnp.concatenate([v * beta, kb * egc], axis=1), tinv), 2, 1)
    attn = _bdot(qn, kn, NT) * decay * lower
    outs = (u, w, qn * egc, kn * jnp.exp(gl - gc), attn, jnp.exp(gl))
    return outs + (tinv,) if with_inverse else outs


def _gdn_scan(states, u, w, qd, kt, attn, egl, z, nw):
    us, ws, qds, kts, egls = (_split(a, HEADS, 0) for a in (u, w, qd, kt, egl))
    vn = [us[h] - _bdot(ws[h], states[h]) for h in range(HEADS)]
    o = jnp.concatenate([_bdot(qds[h], states[h]) for h in range(HEADS)], axis=0)
    o = o + _bdot(attn, jnp.concatenate(vn, axis=0))
    new = [states[h] * jnp.concatenate([egls[h], egls[h]], axis=0) + _bdot(kts[h], vn[h], TN) for h in range(HEADS)]
    on = o * lax.rsqrt(jnp.mean(o * o, axis=-1, keepdims=True) + EPS) * nw
    return new, on * (z * jax.nn.sigmoid(z))


def _gdn_in_specs(s, chunk_of):
    nh = CHUNK // HALO
    main = [pl.BlockSpec((CHUNK, GDN_W), functools.partial(lambda col, i: (chunk_of(i), col), col))
            for col in (2, 3, 4)]
    halo = [pl.BlockSpec((HALO, GDN_W), functools.partial(lambda col, i: (jnp.maximum(chunk_of(i) * nh - 1, 0), col),
                                                         col)) for col in (2, 3, 4)]
    return main, halo


def _stk(width, chunk_of):
    return pl.BlockSpec((STACK, width), lambda i: (chunk_of(i), 0))


def _gdn_prep_fwd(proj, ba, cw, pa, pd, masks):
    s = proj.shape[0]
    nc = s // CHUNK
    main, halo = _gdn_in_specs(s, lambda i: i)

    def body(xq_ref, xk_ref, xv_ref, hq_ref, hk_ref, hv_ref, ba_ref, cw_ref, pa_ref, pd_ref, mk_ref,
             u_ref, w_ref, qd_ref, kt_ref, attn_ref, egl_ref, tinv_ref):
        i = pl.program_id(0)
        xs = [jnp.concatenate([jnp.where(i > 0, h[...], 0.0), m[...]], axis=0)
              for h, m in ((hq_ref, xq_ref), (hk_ref, xk_ref), (hv_ref, xv_ref))]
        rows = [[cw_ref[k:k + 1, j * GDN_W:(j + 1) * GDN_W] for k in range(CONV_K)] for j in range(3)]
        outs = _gdn_prep(xs[0], xs[1], xs[2], ba_ref[...], rows[0], rows[1], rows[2], pa_ref[...], pd_ref[...],
                         [mk_ref[0], mk_ref[1], mk_ref[2]], with_inverse=True)
        for ref, val in zip((u_ref, w_ref, qd_ref, kt_ref, attn_ref, egl_ref, tinv_ref), outs):
            ref[...] = val.astype(ref.dtype)

    ident = lambda i: i
    stacked = lambda dt: _sds((nc * STACK, HEAD_DIM), dt)
    return _call(body, "gdn_prep_fwd", (nc,),
                 main + halo + [_tok(CHUNK, BA_PAD), _whole(cw), _vec(BA_PAD), _vec(BA_PAD), _whole(masks)],
                 [_stk(HEAD_DIM, ident)] * 4 + [_stk(STACK, ident), _stk(HEAD_DIM, ident), _stk(STACK, ident)],
                 [stacked(F32), stacked(BF16), stacked(BF16), stacked(BF16), _sds((nc * STACK, STACK), BF16),
                  stacked(F32), _sds((nc * STACK, STACK), BF16)])(
                     proj, proj, proj, proj, proj, proj, ba, cw, pa, pd, masks)


def _gdn_prep_bwd(cts, tinv, proj, ba, cw, pa, pd, masks):
    s = proj.shape[0]
    nc = s // CHUNK
    rev = lambda i: nc - 1 - i
    main, halo = _gdn_in_specs(s, rev)

    def body(du_ref, dw_ref, dqd_ref, dkt_ref, dattn_ref, degl_ref, tinv_ref, xq_ref, xk_ref, xv_ref, hq_ref, hk_ref,
             hv_ref, ba_ref, cw_ref, pa_ref, pd_ref, mk_ref, dp_ref, dba_ref, dcw_ref, dpar_ref, carry):
        i = pl.program_id(0)
        first_chunk = rev(i) == 0

        @pl.when(i == 0)
        def _():
            carry[...] = jnp.zeros_like(carry)
            dcw_ref[...] = jnp.zeros_like(dcw_ref)
            dpar_ref[...] = jnp.zeros_like(dpar_ref)

        xs = [jnp.concatenate([jnp.where(first_chunk, 0.0, h[...]), m[...]], axis=0)
              for h, m in ((hq_ref, xq_ref), (hk_ref, xk_ref), (hv_ref, xv_ref))]
        rows = [[cw_ref[k:k + 1, j * GDN_W:(j + 1) * GDN_W] for k in range(CONV_K)] for j in range(3)]
        cst = [mk_ref[0], mk_ref[1], mk_ref[2]]
        tinv = tinv_ref[...]
        fn = lambda xq, xk, xv, b, rq, rk, rv, a, d: _gdn_prep(xq, xk, xv, b, rq, rk, rv, a, d, cst, tinv=tinv)
        _, vjp = jax.vjp(fn, xs[0], xs[1], xs[2], ba_ref[...], rows[0], rows[1], rows[2], pa_ref[...], pd_ref[...])
        dxq, dxk, dxv, dba, drq, drk, drv, dpa, dpd = vjp(
            (du_ref[...], dw_ref[...], dqd_ref[...], dkt_ref[...], dattn_ref[...], degl_ref[...]))
        dxw = jnp.concatenate([dxq, dxk, dxv], axis=1)
        tail = jnp.concatenate([jnp.zeros((CHUNK - HALO, 3 * GDN_W), F32), carry[...]], axis=0)
        dp_ref[...] = (dxw[HALO:] + tail).astype(BF16)
        carry[...] = dxw[0:HALO]
        dba_ref[...] = dba.astype(BF16)
        for j, dr in enumerate((drq, drk, drv)):
            for k in range(CONV_K):
                dcw_ref[k:k + 1, j * GDN_W:(j + 1) * GDN_W] += dr[k]
        dpar_ref[0:1, :] += dpa
        dpar_ref[1:2, :] += dpd

    acc = lambda shape: pl.BlockSpec(shape, lambda i: (0, 0))
    return _call(body, "gdn_prep_bwd", (nc,),
                 [_stk(HEAD_DIM, rev)] * 4 + [_stk(STACK, rev), _stk(HEAD_DIM, rev), _stk(STACK, rev)] + main + halo
                 + [pl.BlockSpec((CHUNK, BA_PAD), lambda i: (rev(i), 0)), _whole(cw), _vec(BA_PAD), _vec(BA_PAD),
                    _whole(masks)],
                 [pl.BlockSpec((CHUNK, 3 * GDN_W), lambda i: (rev(i), 0)),
                  pl.BlockSpec((CHUNK, BA_PAD), lambda i: (rev(i), 0)), acc((CONV_K, 3 * GDN_W)), acc((8, BA_PAD))],
                 [_sds((s, 3 * GDN_W), BF16), _sds((s, BA_PAD), BF16), _sds((CONV_K, 3 * GDN_W)), _sds((8, BA_PAD))],
                 scratch=[pltpu.VMEM((HALO, 3 * GDN_W), F32)])(
                     *cts, tinv, proj, proj, proj, proj, proj, proj, ba, cw, pa, pd, masks)


def _stack_heads(v):
    return jnp.concatenate(_split(v, HEADS, 1), axis=0)


def _unstack_heads(v):
    return jnp.concatenate(_split(v, HEADS, 0), axis=1)


def _gdn_scan_fwd(prep, proj, nw):
    s = proj.shape[0]
    nc = s // CHUNK
    ident = lambda i: i

    def body(u_ref, w_ref, qd_ref, kt_ref, attn_ref, egl_ref, z_ref, nw_ref, out_ref, st_ref, state):
        @pl.when(pl.program_id(0) == 0)
        def _():
            state[...] = jnp.zeros_like(state)

        st_ref[...] = state[...]
        states = [state[h * HEAD_DIM:(h + 1) * HEAD_DIM, :] for h in range(HEADS)]
        new, out = _gdn_scan(states, u_ref[...], w_ref[...], qd_ref[...], kt_ref[...], attn_ref[...], egl_ref[...],
                             _stack_heads(z_ref[...]), nw_ref[...])
        for h in range(HEADS):
            state[h * HEAD_DIM:(h + 1) * HEAD_DIM, :] = new[h]
        out_ref[...] = _unstack_heads(out)

    return _call(body, "gdn_scan_fwd", (nc,),
                 [_stk(HEAD_DIM, ident)] * 4 + [_stk(STACK, ident), _stk(HEAD_DIM, ident),
                                                _tok(CHUNK, GDN_W, col=5), _vec(HEAD_DIM)],
                 [_tok(CHUNK, GDN_W), pl.BlockSpec((HEADS * HEAD_DIM, HEAD_DIM), lambda i: (i, 0))],
                 [_sds((s, GDN_W)), _sds((nc * HEADS * HEAD_DIM, HEAD_DIM))],
                 scratch=[pltpu.VMEM((HEADS * HEAD_DIM, HEAD_DIM), F32)])(*prep, proj, nw)


def _gdn_scan_bwd(dout, prep, st, proj, nw):
    s = proj.shape[0]
    nc = s // CHUNK
    rev = lambda i: nc - 1 - i

    def body(do_ref, u_ref, w_ref, qd_ref, kt_ref, attn_ref, egl_ref, st_ref, z_ref, nw_ref,
             du_ref, dw_ref, dqd_ref, dkt_ref, dattn_ref, degl_ref, dz_ref, dnw_ref, dstate):
        @pl.when(pl.program_id(0) == 0)
        def _():
            dstate[...] = jnp.zeros_like(dstate)
            dnw_ref[...] = jnp.zeros_like(dnw_ref)

        states = [st_ref[h * HEAD_DIM:(h + 1) * HEAD_DIM, :] for h in range(HEADS)]
        f32 = lambda ref: ref[...].astype(F32)
        _, vjp = jax.vjp(_gdn_scan, states, u_ref[...], f32(w_ref), f32(qd_ref), f32(kt_ref), f32(attn_ref),
                         egl_ref[...], _stack_heads(z_ref[...]), nw_ref[...])
        dnew = [dstate[h * HEAD_DIM:(h + 1) * HEAD_DIM, :] for h in range(HEADS)]
        dst, du, dw, dqd, dkt, dattn, degl, dz, dnw = vjp((dnew, _stack_heads(do_ref[...])))
        for h in range(HEADS):
            dstate[h * HEAD_DIM:(h + 1) * HEAD_DIM, :] = dst[h]
        for ref, val in zip((du_ref, dw_ref, dqd_ref, dkt_ref, dattn_ref, degl_ref), (du, dw, dqd, dkt, dattn, degl)):
            ref[...] = val
        dz_ref[...] = _unstack_heads(dz).astype(BF16)
        dnw_ref[0:1, :] += dnw

    tokr = lambda n, col=0: pl.BlockSpec((CHUNK, n), lambda i: (rev(i), col))
    return _call(body, "gdn_scan_bwd", (nc,),
                 [tokr(GDN_W)] + [_stk(HEAD_DIM, rev)] * 4 + [_stk(STACK, rev), _stk(HEAD_DIM, rev),
                                                             pl.BlockSpec((HEADS * HEAD_DIM, HEAD_DIM),
                                                                          lambda i: (rev(i), 0)),
                                                             tokr(GDN_W, 5), _vec(HEAD_DIM)],
                 [_stk(HEAD_DIM, rev)] * 4 + [_stk(STACK, rev), _stk(HEAD_DIM, rev), tokr(GDN_W),
                                              pl.BlockSpec((8, HEAD_DIM), lambda i: (0, 0))],
                 [_sds((nc * STACK, HEAD_DIM))] * 4 + [_sds((nc * STACK, STACK)), _sds((nc * STACK, HEAD_DIM)),
                                                       _sds((s, GDN_W), BF16), _sds((8, HEAD_DIM))],
                 scratch=[pltpu.VMEM((HEADS * HEAD_DIM, HEAD_DIM), F32)])(dout, *prep, st, proj, nw)


def _wo_specs(l):
    half = N_DEV // 2
    return [pl.BlockSpec((half, 1, D_MODEL // N_DEV, D_MODEL), functools.partial(lambda k, *_: (k, l, 0, 0), k))
            for k in range(2)]


def _wo_half(ref):
    return ref[:, 0].reshape(ref.shape[0] * ref.shape[2], ref.shape[3])


def _out_mlp_fwd(ol, og, x, wo, g1, nw2, sc2, sh2, g2, wup, wdn, l):
    s = x.shape[0]
    t = _tile(s, 512)
    nj = wup.shape[0]
    fc = wup.shape[3]

    def body(ol_ref, og_ref, x_ref, wol_ref, wog_ref, g1_ref, nw_ref, sc_ref, sh_ref, g2_ref, wup_ref, wdn_ref,
             x1_ref, mix_ref, ff_ref, x2_ref, h2_s, acc_s):
        j = pl.program_id(1)

        @pl.when(j == 0)
        def _():
            mix = _bdot(ol_ref[...], _wo_half(wol_ref)) + _bdot(og_ref[...], _wo_half(wog_ref))
            x1 = x_ref[...] + g1_ref[...] * mix
            mix_ref[...] = mix.astype(BF16)
            x1_ref[...] = x1
            h2, _, _ = _modulated_norm(x1, nw_ref[...], sc_ref[...], sh_ref[...])
            h2_s[...] = h2.astype(BF16)
            acc_s[...] = jnp.zeros_like(acc_s)

        up = _bdot(h2_s[...], wup_ref[0, 0])
        act = jnp.square(jnp.maximum(up, 0.0))
        acc_s[...] += _bdot(act, wdn_ref[0, 0])

        @pl.when(j == nj - 1)
        def _():
            ff_ref[...] = acc_s[...].astype(BF16)
            x2_ref[...] = x1_ref[...] + g2_ref[...] * acc_s[...]

    tk = lambda n: pl.BlockSpec((t, n), lambda i, j: (i, 0))
    return _call(body, "out_mlp_fwd", (s // t, nj),
                 [tk(LRU_W), tk(GDN_W), tk(D_MODEL)] + _wo_specs(l) + [_vec(D_MODEL)] * 5
                 + [pl.BlockSpec((1, 1, D_MODEL, fc), lambda i, j: (j, l, 0, 0)),
                    pl.BlockSpec((1, 1, fc, D_MODEL), lambda i, j: (j, l, 0, 0))],
                 [tk(D_MODEL)] * 4,
                 [_sds((s, D_MODEL)), _sds((s, D_MODEL), BF16), _sds((s, D_MODEL), BF16), _sds((s, D_MODEL))],
                 scratch=[pltpu.VMEM((t, D_MODEL), BF16), pltpu.VMEM((t, D_MODEL), F32)])(
                     ol, og, x, wo, wo, g1, nw2, sc2, sh2, g2, wup, wdn)


def _mlp_bwd(dx2, x1, ff, nw2, sc2, sh2, g2, wup, wdn, l):
    s = x1.shape[0]
    t = _tile(s, 512)
    nj = wup.shape[0]
    fc = wup.shape[3]

    def body(dx2_ref, x1_ref, ff_ref, nw_ref, sc_ref, sh_ref, g2_ref, wup_ref, wdn_ref,
             act_ref, dup_ref, h2_ref, dff_ref, dx1_ref, rows_ref, dh2_s):
        i, j = pl.program_id(0), pl.program_id(1)

        @pl.when((i == 0) & (j == 0))
        def _():
            rows_ref[...] = jnp.zeros_like(rows_ref)

        @pl.when(j == 0)
        def _():
            h2, _, _ = _modulated_norm(x1_ref[...], nw_ref[...], sc_ref[...], sh_ref[...])
            h2_ref[...] = h2.astype(BF16)
            dx2 = dx2_ref[...]
            dff_ref[...] = (dx2 * g2_ref[...]).astype(BF16)
            rows_ref[2:3, :] += jnp.sum(dx2 * ff_ref[...].astype(F32), axis=0, keepdims=True)
            dh2_s[...] = jnp.zeros_like(dh2_s)

        up = _bdot(h2_ref[...], wup_ref[0, 0])
        ru = jnp.maximum(up, 0.0)
        act_ref[...] = (ru * ru).astype(BF16)
        dup = (_bdot(dff_ref[...], wdn_ref[0, 0], NT) * (2.0 * ru)).astype(BF16)
        dup_ref[...] = dup
        dh2_s[...] += _bdot(dup, wup_ref[0, 0], NT)

        @pl.when(j == nj - 1)
        def _():
            xv = x1_ref[...]
            _, n, r = _modulated_norm(xv, nw_ref[...], sc_ref[...], sh_ref[...])
            dx, dsh, dsc, dnw = _modulated_norm_bwd(dh2_s[...], xv, n, r, nw_ref[...], sc_ref[...])
            dx1_ref[...] = dx2_ref[...] + dx
            rows_ref[0:1, :] += dsh
            rows_ref[1:2, :] += dsc
            rows_ref[3:4, :] += dnw

    tk = lambda n: pl.BlockSpec((t, n), lambda i, j: (i, 0))
    tj = pl.BlockSpec((t, fc), lambda i, j: (i, j))
    return _call(body, "mlp_bwd", (s // t, nj),
                 [tk(D_MODEL)] * 3 + [_vec(D_MODEL)] * 4
                 + [pl.BlockSpec((1, 1, D_MODEL, fc), lambda i, j: (j, l, 0, 0)),
                    pl.BlockSpec((1, 1, fc, D_MODEL), lambda i, j: (j, l, 0, 0))],
                 [tj, tj, tk(D_MODEL), tk(D_MODEL), tk(D_MODEL), pl.BlockSpec((8, D_MODEL), lambda i, j: (0, 0))],
                 [_sds((s, nj * fc), BF16), _sds((s, nj * fc), BF16), _sds((s, D_MODEL), BF16),
                  _sds((s, D_MODEL), BF16), _sds((s, D_MODEL)), _sds((8, D_MODEL))],
                 scratch=[pltpu.VMEM((t, D_MODEL), F32)])(dx2, x1, ff, nw2, sc2, sh2, g2, wup, wdn)


def _outproj_bwd(dx1, mix, g1, wo, l):
    s = dx1.shape[0]
    t = _tile(s, 512)

    def body(dx1_ref, mix_ref, g1_ref, wol_ref, wog_ref, dmix_ref, dol_ref, dog_ref, rows_ref):
        @pl.when(pl.program_id(0) == 0)
        def _():
            rows_ref[...] = jnp.zeros_like(rows_ref)

        dx1v = dx1_ref[...]
        rows_ref[0:1, :] += jnp.sum(dx1v * mix_ref[...].astype(F32), axis=0, keepdims=True)
        dmix = (dx1v * g1_ref[...]).astype(BF16)
        dmix_ref[...] = dmix
        dol_ref[...] = _bdot(dmix, _wo_half(wol_ref), NT)
        dog_ref[...] = _bdot(dmix, _wo_half(wog_ref), NT)

    return _call(body, "outproj_bwd", (s // t,),
                 [_tok(t, D_MODEL), _tok(t, D_MODEL), _vec(D_MODEL)] + _wo_specs(l),
                 [_tok(t, D_MODEL), _tok(t, LRU_W), _tok(t, GDN_W), pl.BlockSpec((8, D_MODEL), lambda i: (0, 0))],
                 [_sds((s, D_MODEL), BF16), _sds((s, LRU_W)), _sds((s, GDN_W)), _sds((8, D_MODEL))])(dx1, mix, g1, wo, wo)


def _tn_matmul(a, b, name, out=None, l=0, blocked=False, row_block=0):
    s, m = a.shape
    n = b.shape[1]
    ts, bm = _tile(s, 2048), _tile(m, 1024)
    bn = next(w for w in (512, 640, 384, 256, 128) if n % w == 0)

    def body(a_ref, b_ref, *rest):
        o_ref = rest[-1]

        @pl.when(pl.program_id(2) == 0)
        def _():
            o_ref[...] = jnp.zeros_like(o_ref)

        acc = _bdot(a_ref[...], b_ref[...], TN)
        o_ref[...] += acc.reshape(o_ref.shape)

    in_specs = [pl.BlockSpec((ts, bm), lambda i, j, k: (k, i)), pl.BlockSpec((ts, bn), lambda i, j, k: (k, j))]
    grid = (m // bm, n // bn, s // ts)
    if out is None:
        return _call(body, name, grid, in_specs, pl.BlockSpec((bm, bn), lambda i, j, k: (i, j)), _sds((m, n)))(a, b)
    if blocked:
        out_spec = pl.BlockSpec((1, 1, bm, bn), lambda i, j, k: (l, j, i, 0))
    else:
        out_spec = pl.BlockSpec((1, bm, bn), lambda i, j, k: (l, i + row_block * (m // bm), j))
    return _call(body, name, grid, in_specs + [pl.BlockSpec(memory_space=pl.ANY)], out_spec,
                 _sds(out.shape), aliases={2: 0})(a, b, out)


def _final_fwd_bwd(x, target, fw):
    s = x.shape[0]
    t = _tile(s, 512)

    def body(x_ref, tg_ref, fw_ref, dx_ref, rows_ref):
        @pl.when(pl.program_id(0) == 0)
        def _():
            rows_ref[...] = jnp.zeros_like(rows_ref)

        xv = x_ref[...]
        fwv = fw_ref[...]
        r = lax.rsqrt(jnp.mean(xv * xv, axis=-1, keepdims=True) + EPS)
        err = xv * r * fwv - tg_ref[...]
        part = 0.5 * jnp.sum(jnp.mean(err * err, axis=-1, keepdims=True), axis=0, keepdims=True)
        rows_ref[1:2, :] += jnp.broadcast_to(part, (1, D_MODEL))
        dy = err * (1.0 / D_MODEL)
        rows_ref[0:1, :] += jnp.sum(dy * xv * r, axis=0, keepdims=True)
        dxn = dy * fwv
        dx_ref[...] = r * dxn - xv * (r * r * r) * jnp.mean(dxn * xv, axis=-1, keepdims=True)

    return _call(body, "final_fwd_bwd", (s // t,),
                 [_tok(t, D_MODEL), _tok(t, D_MODEL), _vec(D_MODEL)],
                 [_tok(t, D_MODEL), pl.BlockSpec((8, D_MODEL), lambda i: (0, 0))],
                 [_sds((s, D_MODEL)), _sds((8, D_MODEL))])(x, target, fw)


def _adamw(w, g, m, v):
    m = ADAM_B1 * m + (1.0 - ADAM_B1) * g
    v = ADAM_B2 * v + (1.0 - ADAM_B2) * (g * g)
    m_hat = m / (1.0 - ADAM_B1 ** ADAM_STEP)
    v_hat = v / (1.0 - ADAM_B2 ** ADAM_STEP)
    return -ADAM_LR * (m_hat / (jnp.sqrt(v_hat) + ADAM_EPS) + ADAM_WD * w), m, v


def _mod_local(c_all, wmod, bmod_cols):
    nl, _, cols = wmod.shape

    def body(c_ref, w_ref, b_ref, o_ref):
        cv = c_ref[...]
        o_ref[0] = _bdot(cv * jax.nn.sigmoid(cv), w_ref[0]) + b_ref[0]

    return _call(body, "mod_local", (nl,),
                 [_whole(c_all), pl.BlockSpec((1, D_MODEL, cols), lambda l: (l, 0, 0)),
                  pl.BlockSpec((1, 1, cols), lambda l: (l, 0, 0))],
                 pl.BlockSpec((1, N_DEV, cols), lambda l: (l, 0, 0)), _sds((nl, N_DEV, cols)))(c_all, wmod, bmod_cols)


def _wmod_update(c_all, dmod_cols, w, m, v):
    nl, _, cols = w.shape

    def body(c_ref, d_ref, w_ref, m_ref, v_ref, g_ref, dl_ref, nm_ref, nv_ref):
        cv = c_ref[...]
        g = _bdot(cv * jax.nn.sigmoid(cv), d_ref[0], TN)
        g_ref[0] = g
        dl_ref[0], nm_ref[0], nv_ref[0] = _adamw(w_ref[0], g, m_ref[0], v_ref[0])

    wspec = pl.BlockSpec((1, D_MODEL, cols), lambda l: (l, 0, 0))
    return _call(body, "wmod_update", (nl,),
                 [_whole(c_all), pl.BlockSpec((1, N_DEV, cols), lambda l: (l, 0, 0)), wspec, wspec, wspec],
                 [wspec] * 4, [_sds(w.shape)] * 4)(c_all, dmod_cols, w, m, v)


def _sum_devices(gathered):
    _, r, _ = gathered.shape

    def body(g_ref, o_ref):
        acc = g_ref[0]
        for d in range(1, N_DEV):
            acc = acc + g_ref[d]
        o_ref[...] = acc

    return _call(body, "sum_devices", (1,), [_whole(gathered)], pl.BlockSpec((r, LANES), lambda i: (0, 0)),
                 _sds((r, LANES)))(gathered)


def _adam_flat(w, g, m, v):
    r = w.shape[0]

    def body(w_ref, g_ref, m_ref, v_ref, dl_ref, nm_ref, nv_ref):
        dl_ref[...], nm_ref[...], nv_ref[...] = _adamw(w_ref[...], g_ref[...], m_ref[...], v_ref[...])

    spec = pl.BlockSpec((r, LANES), lambda i: (0, 0))
    return _call(body, "adam_small", (1,), [spec] * 4, [spec] * 3, [_sds((r, LANES))] * 3)(w, g, m, v)


def _pair_add(x, p, core):
    nl, _, r, c = x.shape
    tr = _tile(r, 128 if c > 512 else 256)

    def body(core_ref, x_ref, p_ref, o_ref):
        o_ref[...] = (x_ref[...] + p_ref[...]).astype(BF16)

    return _call(body, "pair_add", (nl, 4, r // tr),
                 [pl.BlockSpec((1, 1, tr, c), lambda l, q, i, core_ref: (l, 2 * q + core_ref[0], i, 0)),
                  pl.BlockSpec((1, 1, tr, c), lambda l, q, i, core_ref: (l, q, i, 0))],
                 pl.BlockSpec((1, 1, tr, c), lambda l, q, i, core_ref: (l, q, i, 0)), _sds((nl, 4, r, c), BF16),
                 prefetch=1)(core, x, p)


def _reduce_adam(x, p, q, place, w, m, v):
    nl, _, r, c = x.shape
    tr = _tile(r, 128 if c > 512 else 256)

    def body(place_ref, x_ref, p_ref, q_ref, w_ref, m_ref, v_ref, g_ref, dl_ref, nm_ref, nv_ref):
        g = (((x_ref[0, 0] + p_ref[0, 0]) + q_ref[0, 0].astype(F32)) + q_ref[0, 1].astype(F32)) + q_ref[0, 2].astype(F32)
        g_ref[0] = g
        dl_ref[0], nm_ref[0], nv_ref[0] = _adamw(w_ref[0], g, m_ref[0], v_ref[0])

    flat = pl.BlockSpec((1, tr, c), lambda l, i, place_ref: (l, i, 0))
    return _call(body, "reduce_adam", (nl, r // tr),
                 [pl.BlockSpec((1, 1, tr, c), lambda l, i, place_ref: (l, place_ref[0], i, 0)),
                  pl.BlockSpec((1, 1, tr, c), lambda l, i, place_ref: (l, place_ref[1], i, 0)),
                  pl.BlockSpec((1, 3, tr, c), lambda l, i, place_ref: (l, 0, i, 0)), flat, flat, flat],
                 [flat] * 4, [_sds((nl, r, c))] * 4, prefetch=1)(place, x, p, q, w, m, v)


def _place():
    return lax.axis_index("x"), lax.axis_index("y"), lax.axis_index("c")


def _all_gather(xs, name, space):
    n = len(xs)

    def body(*refs):
        x_refs, o_refs = refs[:n], refs[n:2 * n]
        send_sems, recv_sems, local_sems = refs[2 * n:]
        x, y, c = _place()
        me, sibling = (x, y, c), (x, y, 1 - c)
        chips = [(1 - x, y), (x, 1 - y), (1 - x, 1 - y)]

        def blk(a, p):
            return o_refs[a].at[4 * p[0] + 2 * p[1] + p[2]]

        def copy(a, k, block, to, src=None):
            return pltpu.make_async_remote_copy(
                src_ref=blk(a, block) if src is None else src, dst_ref=blk(a, block),
                send_sem=send_sems.at[a, k], recv_sem=recv_sems.at[a, k], device_id=to, device_id_type=MESH)

        mine = [pltpu.make_async_copy(x_refs[a], blk(a, me), local_sems.at[a]) for a in range(n)]
        for cp in mine:
            cp.start()
        first = []
        for a in range(n):
            first.append(copy(a, 0, me, sibling, src=x_refs[a]))
            first += [copy(a, 1 + j, me, (*chip, c), src=x_refs[a]) for j, chip in enumerate(chips)]
        for cp in first:
            cp.start()
        passed = []
        for j, chip in enumerate(chips):
            for a in range(n):
                copy(a, 1 + j, (*chip, c), me).wait_recv()
                cp = copy(a, 4 + j, (*chip, c), sibling)
                cp.start()
                passed.append(cp)
        for a in range(n):
            copy(a, 0, sibling, me).wait_recv()
        for j, chip in enumerate(chips):
            for a in range(n):
                copy(a, 4 + j, (*chip, 1 - c), me).wait_recv()
        for cp in first + passed:
            cp.wait_send()
        for cp in mine:
            cp.wait()

    spec = pl.BlockSpec(memory_space=space)
    return pl.pallas_call(
        body, name=name, out_shape=[_sds((N_DEV,) + a.shape, a.dtype) for a in xs],
        in_specs=[spec] * n, out_specs=[spec] * n,
        scratch_shapes=[pltpu.SemaphoreType.DMA((n, 7)), pltpu.SemaphoreType.DMA((n, 7)),
                        pltpu.SemaphoreType.DMA((n,))])(*xs)


def _pair_exchange(xs):
    n = len(xs)

    def body(*refs):
        x_refs, o_refs = refs[:n], refs[n:2 * n]
        send_sems, recv_sems = refs[2 * n:]
        x, y, c = _place()
        copies = [pltpu.make_async_remote_copy(
            src_ref=x_refs[a].at[:, 2 * q + (1 - c)], dst_ref=o_refs[a].at[:, q], send_sem=send_sems.at[a, q],
            recv_sem=recv_sems.at[a, q], device_id=(x, y, 1 - c), device_id_type=MESH)
            for a in range(n) for q in range(4)]
        for cp in copies:
            cp.start()
        for cp in copies:
            cp.wait()

    spec = pl.BlockSpec(memory_space=pl.ANY)
    return pl.pallas_call(
        body, name="pair_exchange", out_shape=[_sds((a.shape[0], 4) + a.shape[2:], a.dtype) for a in xs],
        in_specs=[spec] * n, out_specs=[spec] * n,
        scratch_shapes=[pltpu.SemaphoreType.DMA((n, 4)), pltpu.SemaphoreType.DMA((n, 4))])(*xs)


def _chip_exchange(ys):
    n = len(ys)

    def body(*refs):
        y_refs, o_refs = refs[:n], refs[n:2 * n]
        send_sems, recv_sems = refs[2 * n:]
        x, y, c = _place()
        chips = [(1 - x, y), (x, 1 - y), (1 - x, 1 - y)]
        copies = [pltpu.make_async_remote_copy(
            src_ref=y_refs[a].at[:, 2 * chip[0] + chip[1]], dst_ref=o_refs[a].at[:, r], send_sem=send_sems.at[a, r],
            recv_sem=recv_sems.at[a, r], device_id=(*chip, c), device_id_type=MESH)
            for a in range(n) for r, chip in enumerate(chips)]
        for cp in copies:
            cp.start()
        for cp in copies:
            cp.wait()

    spec = pl.BlockSpec(memory_space=pl.ANY)
    return pl.pallas_call(
        body, name="chip_exchange", out_shape=[_sds((a.shape[0], 3) + a.shape[2:], a.dtype) for a in ys],
        in_specs=[spec] * n, out_specs=[spec] * n,
        scratch_shapes=[pltpu.SemaphoreType.DMA((n, 3)), pltpu.SemaphoreType.DMA((n, 3))])(*ys)


_HBM_SPEC = pl.BlockSpec(memory_space=pltpu.HBM)
_SEM_SPEC = pl.BlockSpec(memory_space=pltpu.SEMAPHORE)
_EFFECT = pltpu.SideEffectType.DATAFLOW_SIDE_EFFECTING


def _descriptors(plan, src_refs, land_refs, send_sems, recv_sems):
    return [pltpu.make_async_remote_copy(src_ref=s, dst_ref=d, send_sem=send_sems.at[k], recv_sem=recv_sems.at[k],
                                         device_id=dev, device_id_type=MESH)
            for k, (s, d, dev) in enumerate(plan(src_refs, land_refs))]


def _split_start(name, plan, n, srcs, lands):
    ns, nb = len(srcs), len(srcs) + len(lands)

    def body(*refs):
        for cp in _descriptors(plan, refs[:ns], refs[ns:nb], refs[nb], refs[nb + 1]):
            cp.start()
        refs[-1][...] = jnp.zeros_like(refs[-1])

    bufs = [pltpu.with_memory_space_constraint(a, pltpu.HBM) for a in list(srcs) + list(lands)]
    outs = pl.pallas_call(
        body, name=name,
        out_shape=(pltpu.SemaphoreType.DMA((n,)), pltpu.SemaphoreType.DMA((n,)))
        + tuple(pltpu.HBM(a.shape, a.dtype) for a in bufs) + (_sds((8, LANES)),),
        in_specs=[_HBM_SPEC] * nb,
        out_specs=(_SEM_SPEC, _SEM_SPEC) + (_HBM_SPEC,) * nb + (pl.BlockSpec(memory_space=pltpu.VMEM),),
        input_output_aliases={i: 2 + i for i in range(nb)},
        compiler_params=pltpu.CompilerParams(has_side_effects=_EFFECT))(*bufs)
    return dict(send=outs[0], recv=outs[1], srcs=list(outs[2:2 + ns]), lands=list(outs[2 + ns:2 + nb]), token=outs[-1])


def _split_wait(name, plan, flight, which, after):
    srcs, lands = flight["srcs"], flight["lands"]
    ns, nb = len(srcs), len(srcs) + len(lands)

    def body(*refs):
        copies = _descriptors(plan, refs[:ns], refs[ns:nb], refs[nb], refs[nb + 1])
        for k in which:
            copies[k].wait_send()
            copies[k].wait_recv()

    outs = pl.pallas_call(
        body, name=name, out_shape=tuple(pltpu.HBM(a.shape, a.dtype) for a in srcs + lands),
        in_specs=[_HBM_SPEC] * nb + [_SEM_SPEC, _SEM_SPEC, pl.BlockSpec(memory_space=pl.ANY)],
        out_specs=(_HBM_SPEC,) * nb, input_output_aliases={i: i for i in range(nb)},
        compiler_params=pltpu.CompilerParams(has_side_effects=_EFFECT))(*srcs, *lands, flight["send"], flight["recv"],
                                                                       after)
    return dict(flight, srcs=list(outs[:ns]), lands=list(outs[ns:nb]))


GATHER_PEERS = N_DEV - 1


def _gather_plan(nl, narr):
    def plan(src_refs, land_refs):
        x, y, c = _place()
        me = 4 * x + 2 * y + c
        out = []
        for l in range(nl):
            for a in range(narr):
                for r in range(1, N_DEV):
                    peer = (1 - x if r & 4 else x, 1 - y if r & 2 else y, 1 - c if r & 1 else c)
                    out.append((src_refs[a].at[l], land_refs[a].at[me, l], peer))
        return out

    return plan


def _size(shape):
    size = 1
    for d in shape:
        size *= d
    return size


def _slab_rows(shape):
    return -(-_size(shape) // (8 * LANES)) * 8


def _pack(arrs):
    parts = []
    for a in arrs:
        flat = a.reshape(-1).astype(F32)
        parts.append(jnp.pad(flat, (0, _slab_rows(a.shape) * LANES - flat.shape[0])).reshape(-1, LANES))
    return jnp.concatenate(parts, axis=0)


def _unpack(slab, shapes):
    out, off = [], 0
    for shp in shapes:
        rows = _slab_rows(shp)
        out.append(slab[off:off + rows].reshape(-1)[:_size(shp)].reshape(shp))
        off += rows
    return out


def _dense_blocks(w):
    eye = jnp.eye(LRU_BLOCKS, dtype=w.dtype)
    return (eye[:, None, :, None] * w[:, :, None, :]).reshape(LRU_W, LRU_W)


def _diag_blocks(dense):
    return jnp.stack([dense[g * LRU_BLOCK:(g + 1) * LRU_BLOCK, g * LRU_BLOCK:(g + 1) * LRU_BLOCK]
                      for g in range(LRU_BLOCKS)])


def _alpha_lanes(v):
    return jnp.zeros((1, BA_PAD), F32).at[0, HEADS:2 * HEADS].set(v)


def _local_step(x, target, mod, p, fetch):
    nl = mod.shape[0]
    row = lambda v: v.reshape(1, -1)
    masks = _gdn_masks()
    saved = []
    xc = x
    for l in range(nl):
        win, wba, lin, wo, wup, wdn = fetch(l, xc)
        mv = [row(mod[l, k * D_MODEL:(k + 1) * D_MODEL]) for k in range(N_MOD)]
        sh1, sc1, g1, sh2, sc2, g2 = mv
        nw1, nw2 = row(p["norm_mix_w"][l]), row(p["norm_mlp_w"][l])
        wa, wx = _dense_blocks(p["lru_gate_a_w"][l]).astype(BF16), _dense_blocks(p["lru_gate_x_w"][l]).astype(BF16)
        lru_args = (p["lru_conv_w"][l], row(p["lru_conv_b"][l]), wa, wx, row(p["lru_gate_a_b"][l]),
                    row(p["lru_gate_x_b"][l]), row(p["lru_lambda"][l]), row(p["lru_norm_w"][l]))
        gdn_args = (p["gdn_conv_w"][l], _alpha_lanes(p["gdn_a_log"][l]), _alpha_lanes(p["gdn_dt_bias"][l]), masks)
        gnw = row(p["gdn_norm_w"][l])
        proj, ba = _inproj_fwd(xc, nw1, sc1, sh1, win, wba, lin)
        ol, hs = _lru_fwd(proj, *lru_args)
        *prep, tinv = _gdn_prep_fwd(proj, ba, *gdn_args)
        og, st = _gdn_scan_fwd(prep, proj, gnw)
        x1, mix, ff, x2 = _out_mlp_fwd(ol, og, xc, wo, g1, nw2, sc2, sh2, g2, wup, wdn, l)
        saved.append(dict(x=xc, mv=mv, nw1=nw1, nw2=nw2, lru_args=lru_args, gdn_args=gdn_args, gnw=gnw, proj=proj,
                          ba=ba, ol=ol, hs=hs, prep=prep, tinv=tinv, og=og, st=st, x1=x1, mix=mix, ff=ff,
                          win=win, wba=wba, lin=lin))
        xc = x2

    dx, frows = _final_fwd_bwd(xc, target, row(p["final_norm_w"]))
    loss_part = frows[1, 0]
    small = {k: [None] * nl for k in ("norm_mix_w", "norm_mlp_w", "lru_conv_w", "lru_conv_b", "lru_gate_a_w",
                                      "lru_gate_a_b", "lru_gate_x_w", "lru_gate_x_b", "lru_lambda", "lru_norm_w",
                                      "gdn_conv_w", "gdn_a_log", "gdn_dt_bias", "gdn_norm_w")}
    fc = D_FF // N_DEV
    g_in = [None] * nl
    g_out, g_down = lax.empty((nl, D_MODEL, D_MODEL), F32), lax.empty((nl, D_FF, D_MODEL), F32)
    g_up = lax.empty((nl, N_DEV, D_MODEL, fc), F32)
    dmod = [None] * nl
    for l in reversed(range(nl)):
        sv = saved[l]
        sh1, sc1, g1, sh2, sc2, g2 = sv["mv"]
        act, dup, h2b, dffb, dx1, rows2 = _mlp_bwd(dx, sv["x1"], sv["ff"], sv["nw2"], sc2, sh2, g2, wup, wdn, l)
        g_up = _tn_matmul(h2b, dup, "grad_w_up", out=g_up, l=l, blocked=True)
        g_down = _tn_matmul(act, dffb, "grad_w_down", out=g_down, l=l)
        dmix, dol, dog, rows1 = _outproj_bwd(dx1, sv["mix"], g1, wo, l)
        g_out = _tn_matmul(sv["ol"], dmix, "grad_w_out_lru", out=g_out, l=l)
        g_out = _tn_matmul(sv["og"], dmix, "grad_w_out_gdn", out=g_out, l=l, row_block=1)
        dpl, dwa, dwx, lrows = _lru_bwd(dol, sv["proj"], sv["hs"], *sv["lru_args"])
        *cts, dpz, gnrow = _gdn_scan_bwd(dog, sv["prep"], sv["st"], sv["proj"], sv["gnw"])
        dpq, dba, dcw, dpar = _gdn_prep_bwd(cts, sv["tinv"], sv["proj"], sv["ba"], *sv["gdn_args"])
        dx, hb, rows0 = _inproj_bwd(dpl, dpq, dpz, dba, sv["x"], dx1, sv["nw1"], sc1, sh1, sv["win"], sv["wba"],
                                    sv["lin"])
        dproj = jnp.concatenate([dpl, dpq, dpz, dba], axis=1)
        g_in[l] = jnp.transpose(_tn_matmul(hb, dproj, "grad_w_in")[:, :IN_COLS].reshape(
            D_MODEL, N_DEV, IN_COLS // N_DEV), (1, 0, 2))
        dmod[l] = jnp.concatenate([rows0[0], rows0[1], rows1[0], rows2[0], rows2[1], rows2[2]])
        small["norm_mix_w"][l], small["norm_mlp_w"][l] = rows0[2], rows2[3]
        small["lru_conv_w"][l], small["lru_conv_b"][l] = lrows[8:8 + CONV_K], lrows[0]
        small["lru_gate_a_w"][l], small["lru_gate_x_w"][l] = _diag_blocks(dwa), _diag_blocks(dwx)
        small["lru_gate_a_b"][l], small["lru_gate_x_b"][l] = lrows[1], lrows[2]
        small["lru_lambda"][l], small["lru_norm_w"][l] = lrows[3], lrows[4]
        small["gdn_conv_w"][l] = dcw
        small["gdn_a_log"][l], small["gdn_dt_bias"][l] = dpar[0, HEADS:2 * HEADS], dpar[1, HEADS:2 * HEADS]
        small["gdn_norm_w"][l] = gnrow[0]
    small = {k: jnp.stack(v) for k, v in small.items()}
    small["final_norm_w"] = frows[0]
    big = dict(w_in=jnp.stack(g_in), w_out=g_out.reshape(nl, N_DEV, D_MODEL // N_DEV, D_MODEL), w_up=g_up,
               w_down=g_down.reshape(nl, N_DEV, fc, D_MODEL))
    return loss_part, dx, big, small, jnp.stack(dmod)


SMALL_REPLICATED = ("norm_mix_w", "norm_mlp_w", "b_mod", "lru_conv_b", "lru_gate_a_w", "lru_gate_a_b", "lru_gate_x_w",
                    "lru_gate_x_b", "lru_lambda", "lru_norm_w", "gdn_a_log", "gdn_dt_bias", "gdn_norm_w",
                    "final_norm_w")
SMALL_SHARDED = ("lru_conv_w", "gdn_conv_w")
WEIGHT_ORDER = ("norm_mix_w", "norm_mlp_w", "w_mod", "b_mod", "w_in", "lru_conv_w", "lru_conv_b", "lru_gate_a_w",
                "lru_gate_a_b", "lru_gate_x_w", "lru_gate_x_b", "lru_lambda", "lru_norm_w", "gdn_conv_w", "gdn_a_log",
                "gdn_dt_bias", "gdn_norm_w", "w_out", "w_up", "w_down", "final_norm_w")


def kernel(x, c, norm_mix_w, norm_mlp_w, w_mod, b_mod, w_in, lru_conv_w, lru_conv_b, lru_gate_a_w, lru_gate_a_b, lru_gate_x_w, lru_gate_x_b, lru_lambda, lru_norm_w, gdn_conv_w, gdn_a_log, gdn_dt_bias, gdn_norm_w, w_out, w_up, w_down, final_norm_w, loss_target, m_norm_mix_w, m_norm_mlp_w, m_w_mod, m_b_mod, m_w_in, m_lru_conv_w, m_lru_conv_b, m_lru_gate_a_w, m_lru_gate_a_b, m_lru_gate_x_w, m_lru_gate_x_b, m_lru_lambda, m_lru_norm_w, m_gdn_conv_w, m_gdn_a_log, m_gdn_dt_bias, m_gdn_norm_w, m_w_out, m_w_up, m_w_down, m_final_norm_w, v_norm_mix_w, v_norm_mlp_w, v_w_mod, v_b_mod, v_w_in, v_lru_conv_w, v_lru_conv_b, v_lru_gate_a_w, v_lru_gate_a_b, v_lru_gate_x_w, v_lru_gate_x_b, v_lru_lambda, v_lru_norm_w, v_gdn_conv_w, v_gdn_a_log, v_gdn_dt_bias, v_gdn_norm_w, v_w_out, v_w_up, v_w_down, v_final_norm_w):
    args = dict(locals())
    w = {k: args[k] for k in WEIGHT_ORDER}
    mom = {k: args["m_" + k] for k in WEIGHT_ORDER}
    var = {k: args["v_" + k] for k in WEIGHT_ORDER}
    nl = w_in.shape[0]
    px, py, pc = _place()
    me = 4 * px + 2 * py + pc
    core = jnp.reshape(pc, (1,)).astype(jnp.int32)

    shapes0 = [c.shape, lru_conv_w.shape, gdn_conv_w.shape]
    (g0,) = _all_gather([_pack([c, lru_conv_w, gdn_conv_w])], "gather_cond", pltpu.VMEM)
    per_dev = [_unpack(g0[d], shapes0) for d in range(N_DEV)]
    c_all = jnp.concatenate([pd[0] for pd in per_dev], axis=0)
    lru_conv_full = jnp.concatenate([pd[1] for pd in per_dev], axis=-1)
    gdn_conv_full = jnp.concatenate([pd[2] for pd in per_dev], axis=-1)

    cols = w_mod.shape[2]
    bmod_cols = lax.dynamic_slice_in_dim(b_mod, me * cols, cols, axis=1).reshape(nl, 1, cols)
    mod_cols = _mod_local(c_all, w_mod, bmod_cols)
    (g1,) = _all_gather([mod_cols.reshape(nl * N_DEV, cols)], "gather_mod", pltpu.VMEM)
    g1 = g1.reshape(N_DEV, nl, N_DEV, cols)
    mod = jnp.transpose(lax.dynamic_index_in_dim(g1, me, axis=2, keepdims=False), (1, 0, 2)).reshape(nl, N_DEV * cols)

    shards = [a.astype(BF16) for a in (w_in, w_out, w_up, w_down)]
    lands = [lax.dynamic_update_slice_in_dim(lax.empty((N_DEV,) + a.shape, BF16), a[None], me, axis=0) for a in shards]
    plan = _gather_plan(nl, len(shards))
    flight = [_split_start("gather_weights_start", plan, nl * len(shards) * GATHER_PEERS, shards, lands)]
    mod = mod + flight[0]["token"][0, 0]

    def fetch(l, after):
        per_layer = len(shards) * GATHER_PEERS
        flight[0] = _split_wait(f"gather_weights_wait{l}", plan, flight[0],
                                range(l * per_layer, (l + 1) * per_layer), after)
        gin, gout, gup, gdn = flight[0]["lands"]
        win = jnp.transpose(gin[:, l], (1, 0, 2)).reshape(1, D_MODEL, IN_COLS)
        wba = jnp.pad(win[:, :, IN_MAIN:], ((0, 0), (0, 0), (0, BA_PAD - (IN_COLS - IN_MAIN))))
        return win, wba, 0, gout, gup, gdn

    p = dict(w)
    p["lru_conv_w"], p["gdn_conv_w"] = lru_conv_full, gdn_conv_full

    loss_part, grad_x, big, small, dmod = _local_step(x[0], loss_target[0], mod, p, fetch)
    loss = lax.psum(loss_part, MESH_AXES)

    small_names = sorted(small)
    slab = _pack([dmod] + [small[k] for k in small_names])
    (gs,) = _all_gather([slab], "gather_small_grads", pltpu.VMEM)
    dmod_all = gs[:, :_slab_rows(dmod.shape)].reshape(N_DEV, nl, N_MOD * D_MODEL)
    summed = _unpack(_sum_devices(gs), [dmod.shape] + [small[k].shape for k in small_names])
    grads = dict(zip(small_names, summed[1:]))
    grads["b_mod"] = summed[0]
    for k, width in (("lru_conv_w", LRU_W // N_DEV), ("gdn_conv_w", 3 * GDN_W // N_DEV)):
        grads[k] = lax.dynamic_slice_in_dim(grads[k], me * width, width, axis=2)
    names = SMALL_REPLICATED + SMALL_SHARDED
    shapes = [w[k].shape for k in names]
    dl, nm, nv = _adam_flat(_pack([w[k] for k in names]), _pack([grads[k] for k in names]),
                            _pack([mom[k] for k in names]), _pack([var[k] for k in names]))
    delta = dict(zip(names, _unpack(dl, shapes)))
    new_m = dict(zip(names, _unpack(nm, shapes)))
    new_v = dict(zip(names, _unpack(nv, shapes)))

    dmod_cols = jnp.transpose(lax.dynamic_slice_in_dim(dmod_all, me * cols, cols, axis=2), (1, 0, 2))
    grads["w_mod"], delta["w_mod"], new_m["w_mod"], new_v["w_mod"] = _wmod_update(
        c_all, dmod_cols, w_mod, m_w_mod, v_w_mod)

    order = ("w_in", "w_out", "w_up", "w_down")
    xs = [big[k] for k in order]
    ps = _pair_exchange(xs)
    ys = [_pair_add(xk, pk, core) for xk, pk in zip(xs, ps)]
    qs = _chip_exchange(ys)
    place = jnp.stack([me, 2 * px + py]).astype(jnp.int32)
    for k, xk, pk, qk in zip(order, xs, ps, qs):
        grads[k], delta[k], new_m[k], new_v[k] = _reduce_adam(xk, pk, qk, place, w[k], mom[k], var[k])

    return (loss, grad_x[None], *[grads[k] for k in WEIGHT_ORDER], *[delta[k] for k in WEIGHT_ORDER],
            *[new_m[k] for k in WEIGHT_ORDER], *[new_v[k] for k in WEIGHT_ORDER])
```

```python
import functools

import jax
import jax.numpy as jnp
from jax import lax
from jax.experimental import pallas as pl
from jax.experimental.pallas import tpu as pltpu

F32 = jnp.float32
BF16 = jnp.bfloat16

D_MODEL = 1024
LRU_W = 512
LRU_BLOCKS = 8
LRU_BLOCK = 64
LRU_C = 8.0
GDN_W = 512
HEADS = 4
HEAD_DIM = 128
CHUNK = 64
STACK = HEADS * CHUNK
CONV_K = 4
D_FF = 4096
N_MOD = 6
IN_COLS = 3080
IN_MAIN = 3072
BA_PAD = 128
EPS = 1e-6
N_DEV = 8
HALO = 8
LANES = 128
ADAM_LR, ADAM_B1, ADAM_B2, ADAM_EPS, ADAM_WD, ADAM_STEP = 0.001, 0.9, 0.999, 1e-08, 0.01, 10
MESH_AXES = ("x", "y", "c")
MESH = pl.DeviceIdType.MESH

NN = (((1,), (0,)), ((), ()))
NT = (((1,), (1,)), ((), ()))
TN = (((0,), (0,)), ((), ()))


def _bdot(a, b, dims=NN):
    return lax.dot_general(a.astype(BF16), b.astype(BF16), dims, preferred_element_type=F32)


def _sdot(a, b, dims=NN):
    ah, bh = a.astype(BF16), b.astype(BF16)
    al, bl = (a - ah.astype(F32)).astype(BF16), (b - bh.astype(F32)).astype(BF16)
    return _bdot(ah, bh, dims) + (_bdot(al, bh, dims) + _bdot(ah, bl, dims))


def _hdot(a, b, dims=NN):
    return lax.dot_general(a, b, dims, precision=lax.Precision.HIGHEST, preferred_element_type=F32)


def _sds(shape, dtype=F32):
    return jax.ShapeDtypeStruct(tuple(shape), dtype)


def _tile(n, t):
    return min(n, t)


def _call(body, name, grid, in_specs, out_specs, out_shape, scratch=(), vmem_mb=48, prefetch=0, aliases=None):
    params = pltpu.CompilerParams(dimension_semantics=("arbitrary",) * len(grid), vmem_limit_bytes=vmem_mb * 2**20)
    if prefetch:
        spec = pltpu.PrefetchScalarGridSpec(num_scalar_prefetch=prefetch, grid=grid, in_specs=in_specs,
                                            out_specs=out_specs, scratch_shapes=list(scratch))
        return pl.pallas_call(body, name=name, grid_spec=spec, out_shape=out_shape, compiler_params=params)
    return pl.pallas_call(body, name=name, grid=grid, in_specs=in_specs, out_specs=out_specs, out_shape=out_shape,
                          scratch_shapes=list(scratch), compiler_params=params, input_output_aliases=aliases or {})


def _tok(t, n, col=0):
    return pl.BlockSpec((t, n), lambda i, *_: (i, col))


def _vec(n):
    return pl.BlockSpec((1, n), lambda *_: (0, 0))


def _whole(a):
    nd = a.ndim
    return pl.BlockSpec(a.shape, lambda *_: (0,) * nd)


def _layer(l, *dims):
    return pl.BlockSpec((1,) + dims, lambda *_: (l,) + (0,) * len(dims))


def _gelu(y):
    c0, c1 = 0.7978845608028654, 0.044715
    return 0.5 * y * (1.0 + jnp.tanh(c0 * (y + c1 * y * y * y)))


def _gelu_grad(y):
    c0, c1 = 0.7978845608028654, 0.044715
    t = jnp.tanh(c0 * (y + c1 * y * y * y))
    return 0.5 * (1.0 + t) + 0.5 * y * (1.0 - t * t) * c0 * (1.0 + 3.0 * c1 * y * y)


def _softplus(v):
    return jnp.maximum(v, 0.0) + jnp.log(1.0 + jnp.exp(-jnp.where(v > 0, v, -v)))


@functools.partial(jax.custom_vjp, nondiff_argnums=(1,))
def _roll_rows(v, s):
    s = s % v.shape[0]
    return pltpu.roll(v, s, axis=0) if s else v


def _roll_rows_fwd(v, s):
    return _roll_rows(v, s), None


def _roll_rows_bwd(s, _, g):
    return (_roll_rows(g, -s),)


_roll_rows.defvjp(_roll_rows_fwd, _roll_rows_bwd)


@jax.custom_vjp
def _drop_halo(v):
    return v[HALO:]


def _drop_halo_fwd(v):
    return v[HALO:], None


def _drop_halo_bwd(_, g):
    return (jnp.concatenate([jnp.zeros((HALO, g.shape[1]), g.dtype), g], axis=0),)


_drop_halo.defvjp(_drop_halo_fwd, _drop_halo_bwd)


@functools.partial(jax.custom_vjp, nondiff_argnums=(1, 2))
def _split(v, n, axis):
    w = v.shape[axis] // n
    return tuple(lax.slice_in_dim(v, k * w, (k + 1) * w, axis=axis) for k in range(n))


def _split_fwd(v, n, axis):
    return _split(v, n, axis), None


def _split_bwd(n, axis, _, gs):
    return (jnp.concatenate(list(gs), axis=axis),)


_split.defvjp(_split_fwd, _split_bwd)


def _conv_taps(xw):
    return [_drop_halo(_roll_rows(xw, CONV_K - 1 - k)) for k in range(CONV_K)]


def _modulated_norm(xv, nw, sc, sh):
    r = lax.rsqrt(jnp.mean(xv * xv, axis=-1, keepdims=True) + EPS)
    n = xv * r * nw
    return n * (1.0 + sc) + sh, n, r


def _modulated_norm_bwd(dh, xv, n, r, nw, sc):
    dn = dh * (1.0 + sc)
    dxn = dn * nw
    dx = r * dxn - xv * (r * r * r) * jnp.mean(dxn * xv, axis=-1, keepdims=True)
    return (dx, jnp.sum(dh, axis=0, keepdims=True), jnp.sum(dh * n, axis=0, keepdims=True),
            jnp.sum(dn * xv * r, axis=0, keepdims=True))


def _inproj_fwd(x, nw, sc, sh, win, wba, l):
    s = x.shape[0]
    t = _tile(s, 256)

    def body(x_ref, nw_ref, sc_ref, sh_ref, win_ref, wba_ref, proj_ref, ba_ref):
        h, _, _ = _modulated_norm(x_ref[...], nw_ref[...], sc_ref[...], sh_ref[...])
        hb = h.astype(BF16)
        proj_ref[...] = _bdot(hb, win_ref[0])
        ba_ref[...] = _bdot(hb, wba_ref[0])

    return _call(body, "inproj_fwd", (s // t,),
                 [_tok(t, D_MODEL), _vec(D_MODEL), _vec(D_MODEL), _vec(D_MODEL), _layer(l, D_MODEL, IN_MAIN),
                  _layer(l, D_MODEL, BA_PAD)],
                 [_tok(t, IN_MAIN), _tok(t, BA_PAD)],
                 [_sds((s, IN_MAIN)), _sds((s, BA_PAD))])(x, nw, sc, sh, win, wba)


def _inproj_bwd(dpl, dpq, dpz, dba, x, dx1, nw, sc, sh, win, wba, l):
    s = x.shape[0]
    t = _tile(s, 256)

    def body(dpl_ref, dpq_ref, dpz_ref, dba_ref, x_ref, dx1_ref, nw_ref, sc_ref, sh_ref, win_ref, wba_ref,
             dx_ref, hb_ref, acc_ref):
        @pl.when(pl.program_id(0) == 0)
        def _():
            acc_ref[...] = jnp.zeros_like(acc_ref)

        dh = (_bdot(dpl_ref[...], win_ref[0, :, 0:2 * LRU_W], NT)
              + _bdot(dpq_ref[...], win_ref[0, :, 2 * LRU_W:2 * LRU_W + 3 * GDN_W], NT)
              + _bdot(dpz_ref[...], win_ref[0, :, 2 * LRU_W + 3 * GDN_W:IN_MAIN], NT)
              + _bdot(dba_ref[...], wba_ref[0], NT))
        xv = x_ref[...]
        h, n, r = _modulated_norm(xv, nw_ref[...], sc_ref[...], sh_ref[...])
        hb_ref[...] = h.astype(BF16)
        dx, dsh, dsc, dnw = _modulated_norm_bwd(dh, xv, n, r, nw_ref[...], sc_ref[...])
        dx_ref[...] = dx1_ref[...] + dx
        acc_ref[0:1, :] += dsh
        acc_ref[1:2, :] += dsc
        acc_ref[2:3, :] += dnw

    return _call(body, "inproj_bwd", (s // t,),
                 [_tok(t, 2 * LRU_W), _tok(t, 3 * GDN_W), _tok(t, GDN_W), _tok(t, BA_PAD), _tok(t, D_MODEL),
                  _tok(t, D_MODEL), _vec(D_MODEL), _vec(D_MODEL), _vec(D_MODEL), _layer(l, D_MODEL, IN_MAIN),
                  _layer(l, D_MODEL, BA_PAD)],
                 [_tok(t, D_MODEL), _tok(t, D_MODEL), pl.BlockSpec((8, D_MODEL), lambda i: (0, 0))],
                 [_sds((s, D_MODEL)), _sds((s, D_MODEL), BF16), _sds((8, D_MODEL))])(
                     dpl, dpq, dpz, dba, x, dx1, nw, sc, sh, win, wba)


def _lru_gates(xw, cw_rows, cb, wa, wx, gab, gxb, lam):
    taps = _conv_taps(xw)
    xr = cb + cw_rows[0] * taps[0] + cw_rows[1] * taps[1] + cw_rows[2] * taps[2] + cw_rows[3] * taps[3]
    xb = xr.astype(BF16)
    r = jax.nn.sigmoid(_bdot(xb, wa) + gab)
    i = jax.nn.sigmoid(_bdot(xb, wx) + gxb)
    z = jnp.exp(-jnp.where(lam > 0, lam, -lam))
    w1 = 1.0 + z
    log1p_z = jnp.where(w1 == 1.0, z, jnp.log(w1) * z / (w1 - 1.0))
    ls = jnp.minimum(lam, 0.0) - log1p_z
    la = LRU_C * r * ls
    a = jnp.exp(la)
    x2 = 2.0 * la
    u = jnp.exp(x2)
    mm_raw = jnp.where(u == 1.0, -x2,
                       jnp.where(x2 < -30.0, 1.0, (1.0 - u) * x2 / jnp.log(jnp.maximum(u, 1e-30))))
    mult = jnp.sqrt(jnp.maximum(mm_raw, 1e-12))
    return dict(taps=taps, xr=xr, r=r, i=i, ls=ls, a=a, mm_raw=mm_raw, mult=mult)


def _lru_specs(s, t, tile_of):
    nh = t // HALO
    xl = pl.BlockSpec((t, LRU_W), lambda i: (tile_of(i), 0))
    yl = pl.BlockSpec((t, LRU_W), lambda i: (tile_of(i), 1))
    hx = pl.BlockSpec((HALO, LRU_W), lambda i: (jnp.maximum(tile_of(i) * nh - 1, 0), 0))
    return xl, yl, hx


def _lru_fwd(proj, cw, cb, wa, wx, gab, gxb, lam, lnw):
    s = proj.shape[0]
    t = _tile(s, 256)
    xl, yl, hx = _lru_specs(s, t, lambda i: i)

    def body(xl_ref, yl_ref, hx_ref, cw_ref, cb_ref, wa_ref, wx_ref, gab_ref, gxb_ref, lam_ref, lnw_ref,
             out_ref, h_ref, a_s, b_s, hc):
        i = pl.program_id(0)

        @pl.when(i == 0)
        def _():
            hc[...] = jnp.zeros_like(hc)

        halo = jnp.where(i > 0, hx_ref[...], 0.0)
        xw = jnp.concatenate([halo, xl_ref[...]], axis=0)
        g = _lru_gates(xw, [cw_ref[k:k + 1, :] for k in range(CONV_K)], cb_ref[...], wa_ref[...], wx_ref[...],
                       gab_ref[...], gxb_ref[...], lam_ref[...])
        a_s[...] = g["a"]
        b_s[...] = g["mult"] * (g["i"] * g["xr"])

        def step(k, h):
            h = a_s[pl.ds(k, 1), :] * h + b_s[pl.ds(k, 1), :]
            h_ref[pl.ds(k, 1), :] = h
            return h

        hc[...] = lax.fori_loop(0, t, step, hc[...], unroll=8)
        m = h_ref[...] * _gelu(yl_ref[...])
        out_ref[...] = m * lax.rsqrt(jnp.mean(m * m, axis=-1, keepdims=True) + EPS) * lnw_ref[...]

    return _call(body, "lru_fwd", (s // t,),
                 [xl, yl, hx, _whole(cw), _vec(LRU_W), _whole(wa), _whole(wx)] + [_vec(LRU_W)] * 4,
                 [_tok(t, LRU_W), _tok(t, LRU_W)],
                 [_sds((s, LRU_W)), _sds((s, LRU_W))],
                 scratch=[pltpu.VMEM((t, LRU_W), F32), pltpu.VMEM((t, LRU_W), F32), pltpu.VMEM((1, LRU_W), F32)])(
                     proj, proj, proj, cw, cb, wa, wx, gab, gxb, lam, lnw)


def _lru_bwd(dout, proj, hs, cw, cb, wa, wx, gab, gxb, lam, lnw):
    s = proj.shape[0]
    t = _tile(s, 256)
    nt = s // t
    rev = lambda i: nt - 1 - i
    xl, yl, hx = _lru_specs(s, t, rev)
    nh = t // HALO
    tk = pl.BlockSpec((t, LRU_W), lambda i: (rev(i), 0))
    hh = pl.BlockSpec((HALO, LRU_W), lambda i: (jnp.maximum(rev(i) * nh - 1, 0), 0))

    def body(do_ref, xl_ref, yl_ref, hx_ref, h_ref, hh_ref, cw_ref, cb_ref, wa_ref, wx_ref, gab_ref, gxb_ref,
             lam_ref, lnw_ref, dp_ref, dwa_ref, dwx_ref, rows_ref, dh_s, dhd_s, carry, dxr_next):
        i = pl.program_id(0)
        first_tile = rev(i) == 0

        @pl.when(i == 0)
        def _():
            carry[...] = jnp.zeros_like(carry)
            dxr_next[...] = jnp.zeros_like(dxr_next)
            dwa_ref[...] = jnp.zeros_like(dwa_ref)
            dwx_ref[...] = jnp.zeros_like(dwx_ref)
            rows_ref[...] = jnp.zeros_like(rows_ref)

        halo = jnp.where(first_tile, 0.0, hx_ref[...])
        xw = jnp.concatenate([halo, xl_ref[...]], axis=0)
        cw_rows = [cw_ref[k:k + 1, :] for k in range(CONV_K)]
        lam_v = lam_ref[...]
        g = _lru_gates(xw, cw_rows, cb_ref[...], wa_ref[...], wx_ref[...], gab_ref[...], gxb_ref[...], lam_v)
        a, r, gi, xr, mult = g["a"], g["r"], g["i"], g["xr"], g["mult"]
        hv = h_ref[...]
        yv = yl_ref[...]
        gl = _gelu(yv)
        m = hv * gl
        rn = lax.rsqrt(jnp.mean(m * m, axis=-1, keepdims=True) + EPS)
        dov = do_ref[...]
        dmn = dov * lnw_ref[...]
        rows_ref[4:5, :] += jnp.sum(dov * m * rn, axis=0, keepdims=True)
        dm = rn * dmn - m * (rn * rn * rn) * jnp.mean(dmn * m, axis=-1, keepdims=True)
        dhd_s[...] = dm * gl
        dy = dm * hv * _gelu_grad(yv)
        dh_s[...] = a

        def step(k, c):
            row = t - 1 - k
            d = dhd_s[pl.ds(row, 1), :] + c
            c = dh_s[pl.ds(row, 1), :] * d
            dh_s[pl.ds(row, 1), :] = d
            return c

        carry[...] = lax.fori_loop(0, t, step, carry[...], unroll=8)
        dH = dh_s[...]
        hprev_halo = jnp.where(first_tile, 0.0, hh_ref[...])
        hprev = _drop_halo(_roll_rows(jnp.concatenate([hprev_halo, hv], axis=0), 1))
        da = dH * hprev
        dmult = dH * gi * xr
        di = dH * mult * xr
        dxr = dH * mult * gi
        dla = jnp.where(g["mm_raw"] > 1e-12, dmult * (0.5 / mult) * (-2.0 * a * a), 0.0) + da * a
        dr = dla * (LRU_C * g["ls"])
        sig_neg = jax.nn.sigmoid(-lam_v)
        rows_ref[3:4, :] += jnp.sum(dla * (LRU_C * r), axis=0, keepdims=True) * sig_neg
        drp = dr * r * (1.0 - r)
        dip = di * gi * (1.0 - gi)
        rows_ref[1:2, :] += jnp.sum(drp, axis=0, keepdims=True)
        rows_ref[2:3, :] += jnp.sum(dip, axis=0, keepdims=True)
        xb = xr.astype(BF16)
        drb = drp.astype(BF16)
        dib = dip.astype(BF16)
        dwa_ref[...] += _bdot(xb, drb, TN)
        dwx_ref[...] += _bdot(xb, dib, TN)
        dxr = dxr + _bdot(drb, wa_ref[...], NT) + _bdot(dib, wx_ref[...], NT)
        rows_ref[0:1, :] += jnp.sum(dxr, axis=0, keepdims=True)
        ext = jnp.concatenate([dxr, dxr_next[...]], axis=0)
        dx = cw_rows[CONV_K - 1] * dxr
        for k in range(CONV_K - 1):
            dx = dx + cw_rows[k] * _roll_rows(ext, -(CONV_K - 1 - k))[0:t]
        for k in range(CONV_K):
            rows_ref[8 + k:9 + k, :] += jnp.sum(dxr * g["taps"][k], axis=0, keepdims=True)
        dxr_next[...] = dxr[0:HALO]
        dp_ref[...] = jnp.concatenate([dx, dy], axis=1).astype(BF16)

    acc = lambda shape: pl.BlockSpec(shape, lambda i: (0, 0))
    return _call(body, "lru_bwd", (nt,),
                 [tk, xl, yl, hx, tk, hh, _whole(cw), _vec(LRU_W), _whole(wa), _whole(wx)] + [_vec(LRU_W)] * 4,
                 [pl.BlockSpec((t, 2 * LRU_W), lambda i: (rev(i), 0)), acc((LRU_W, LRU_W)), acc((LRU_W, LRU_W)),
                  acc((16, LRU_W))],
                 [_sds((s, 2 * LRU_W), BF16), _sds((LRU_W, LRU_W)), _sds((LRU_W, LRU_W)), _sds((16, LRU_W))],
                 scratch=[pltpu.VMEM((t, LRU_W), F32), pltpu.VMEM((t, LRU_W), F32), pltpu.VMEM((1, LRU_W), F32),
                          pltpu.VMEM((HALO, LRU_W), F32)])(
                     dout, proj, proj, proj, hs, hs, cw, cb, wa, wx, gab, gxb, lam, lnw)


def _gdn_masks():
    row = lax.broadcasted_iota(jnp.int32, (STACK, STACK), 0)
    col = lax.broadcasted_iota(jnp.int32, (STACK, STACK), 1)
    same = (row // CHUNK) == (col // CHUNK)
    return jnp.stack([(same & (col <= row)).astype(F32), (same & (col < row)).astype(F32), (row == col).astype(F32)])


def _conv_silu(xw, rows):
    taps = _conv_taps(xw)
    y = rows[0] * taps[0] + rows[1] * taps[1] + rows[2] * taps[2] + rows[3] * taps[3]
    return y * jax.nn.sigmoid(y)


def _split3(v):
    hi = v.astype(BF16)
    r1 = v - hi.astype(F32)
    mid = r1.astype(BF16)
    return hi, mid, (r1 - mid.astype(F32)).astype(BF16)


def _mask_dot_raw(mask, v, dims):
    parts = _split3(v)
    d = lambda p: lax.dot_general(mask, p, dims, preferred_element_type=F32)
    return d(parts[0]) + (d(parts[1]) + d(parts[2]))


@jax.custom_vjp
def _mask_dot(mask, v):
    return _mask_dot_raw(mask, v, NN)


def _mask_dot_fwd(mask, v):
    return _mask_dot_raw(mask, v, NN), mask


def _mask_dot_bwd(mask, ct):
    return jnp.zeros_like(mask), _mask_dot_raw(mask, ct, TN)


_mask_dot.defvjp(_mask_dot_fwd, _mask_dot_bwd)


def _unit_lower_inverse(n, eye):
    tinv = eye + n
    p = n
    for _ in range(5):
        p = _bdot(p, p)
        tinv = tinv + _bdot(tinv, p)
    return tinv.astype(BF16)


@jax.custom_vjp
def _unit_lower_solve(n, rhs, tinv):
    x0 = _bdot(tinv, rhs)
    return x0 + _bdot(tinv, rhs - x0 + _sdot(n, x0))


def _unit_lower_solve_fwd(n, rhs, tinv):
    x = _unit_lower_solve(n, rhs, tinv)
    return x, (n, tinv, x)


def _unit_lower_solve_bwd(res, ct):
    n, tinv, x = res
    y0 = _bdot(tinv, ct, TN)
    y = y0 + _bdot(tinv, ct - y0 + _sdot(n, y0, TN), TN)
    return _bdot(y, x, NT), y, jnp.zeros_like(tinv)


_unit_lower_solve.defvjp(_unit_lower_solve_fwd, _unit_lower_solve_bwd)


def _gdn_prep(xq, xk, xv, ba, cwq, cwk, cwv, pa, pd, masks, tinv=None, with_inverse=False):
    lower, strict, eye = masks[0], masks[1], masks[2]
    lower_b = lower.astype(BF16)
    lane = lax.broadcasted_iota(jnp.int32, (CHUNK, LANES), 1)
    q = jnp.concatenate(_split(_conv_silu(xq, cwq), HEADS, 1), axis=0)
    k = jnp.concatenate(_split(_conv_silu(xk, cwk), HEADS, 1), axis=0)
    v = jnp.concatenate(_split(_conv_silu(xv, cwv), HEADS, 1), axis=0)
    qn = q * lax.rsqrt(jnp.sum(q * q, axis=-1, keepdims=True) + 1e-6) * (HEAD_DIM ** -0.5)
    kn = k * lax.rsqrt(jnp.sum(k * k, axis=-1, keepdims=True) + 1e-6)
    beta_f = jax.nn.sigmoid(ba)
    g_f = -jnp.exp(pa) * _softplus(ba + pd)

    def col(a, j):
        return jnp.broadcast_to(jnp.sum(jnp.where(lane == j, a, 0.0), axis=1, keepdims=True), (CHUNK, HEAD_DIM))

    beta = jnp.concatenate([col(beta_f, h) for h in range(HEADS)], axis=0)
    gs = [col(g_f, HEADS + h) for h in range(HEADS)]
    g = jnp.concatenate(gs, axis=0)
    gl = jnp.concatenate([jnp.broadcast_to(jnp.sum(gh, axis=0, keepdims=True), (CHUNK, HEAD_DIM)) for gh in gs], axis=0)
    gc = _mask_dot(lower_b, g)
    gc_rows = jnp.transpose(gc)
    decay = jnp.exp((jnp.concatenate([gc, gc], axis=1) - jnp.concatenate([gc_rows, gc_rows], axis=0)) * lower)
    egc = jnp.exp(gc)
    kb = kn * beta
    n = -(_bdot(kb, kn, NT) * decay * strict)
    if tinv is None:
        tinv = _unit_lower_inverse(lax.stop_gradient(n), eye)
    u, w = _split(_unit_lower_solve(n, jnp.concatenate([v * beta, kb * egc], axis=1), tinv), 2, 1)
    attn = _bdot(qn, kn, NT) * decay * lower
    outs = (u, w, qn * egc, kn * jnp.exp(gl - gc), attn, jnp.exp(gl))
    return outs + (tinv,) if with_inverse else outs


def _gdn_scan(states, u, w, qd, kt, attn, egl, z, nw):
    us, ws, qds, kts, egls = (_split(a, HEADS, 0) for a in (u, w, qd, kt, egl))
    vn = [us[h] - _bdot(ws[h], states[h]) for h in range(HEADS)]
    o = jnp.concatenate([_bdot(qds[h], states[h]) for h in range(HEADS)], axis=0)
    o = o + _bdot(attn, jnp.concatenate(vn, axis=0))
    new = [states[h] * jnp.concatenate([egls[h], egls[h]], axis=0) + _bdot(kts[h], vn[h], TN) for h in range(HEADS)]
    on = o * lax.rsqrt(jnp.mean(o * o, axis=-1, keepdims=True) + EPS) * nw
    return new, on * (z * jax.nn.sigmoid(z))


def _gdn_in_specs(s, chunk_of):
    nh = CHUNK // HALO
    main = [pl.BlockSpec((CHUNK, GDN_W), functools.partial(lambda col, i: (chunk_of(i), col), col))
            for col in (2, 3, 4)]
    halo = [pl.BlockSpec((HALO, GDN_W), functools.partial(lambda col, i: (jnp.maximum(chunk_of(i) * nh - 1, 0), col),
                                                         col)) for col in (2, 3, 4)]
    return main, halo


def _stk(width, chunk_of):
    return pl.BlockSpec((STACK, width), lambda i: (chunk_of(i), 0))


def _gdn_prep_fwd(proj, ba, cw, pa, pd, masks):
    s = proj.shape[0]
    nc = s // CHUNK
    main, halo = _gdn_in_specs(s, lambda i: i)

    def body(xq_ref, xk_ref, xv_ref, hq_ref, hk_ref, hv_ref, ba_ref, cw_ref, pa_ref, pd_ref, mk_ref,
             u_ref, w_ref, qd_ref, kt_ref, attn_ref, egl_ref, tinv_ref):
        i = pl.program_id(0)
        xs = [jnp.concatenate([jnp.where(i > 0, h[...], 0.0), m[...]], axis=0)
              for h, m in ((hq_ref, xq_ref), (hk_ref, xk_ref), (hv_ref, xv_ref))]
        rows = [[cw_ref[k:k + 1, j * GDN_W:(j + 1) * GDN_W] for k in range(CONV_K)] for j in range(3)]
        outs = _gdn_prep(xs[0], xs[1], xs[2], ba_ref[...], rows[0], rows[1], rows[2], pa_ref[...], pd_ref[...],
                         [mk_ref[0], mk_ref[1], mk_ref[2]], with_inverse=True)
        for ref, val in zip((u_ref, w_ref, qd_ref, kt_ref, attn_ref, egl_ref, tinv_ref), outs):
            ref[...] = val.astype(ref.dtype)

    ident = lambda i: i
    stacked = lambda dt: _sds((nc * STACK, HEAD_DIM), dt)
    return _call(body, "gdn_prep_fwd", (nc,),
                 main + halo + [_tok(CHUNK, BA_PAD), _whole(cw), _vec(BA_PAD), _vec(BA_PAD), _whole(masks)],
                 [_stk(HEAD_DIM, ident)] * 4 + [_stk(STACK, ident), _stk(HEAD_DIM, ident), _stk(STACK, ident)],
                 [stacked(F32), stacked(BF16), stacked(BF16), stacked(BF16), _sds((nc * STACK, STACK), BF16),
                  stacked(F32), _sds((nc * STACK, STACK), BF16)])(
                     proj, proj, proj, proj, proj, proj, ba, cw, pa, pd, masks)


def _gdn_prep_bwd(cts, tinv, proj, ba, cw, pa, pd, masks):
    s = proj.shape[0]
    nc = s // CHUNK
    rev = lambda i: nc - 1 - i
    main, halo = _gdn_in_specs(s, rev)

    def body(du_ref, dw_ref, dqd_ref, dkt_ref, dattn_ref, degl_ref, tinv_ref, xq_ref, xk_ref, xv_ref, hq_ref, hk_ref,
             hv_ref, ba_ref, cw_ref, pa_ref, pd_ref, mk_ref, dp_ref, dba_ref, dcw_ref, dpar_ref, carry):
        i = pl.program_id(0)
        first_chunk = rev(i) == 0

        @pl.when(i == 0)
        def _():
            carry[...] = jnp.zeros_like(carry)
            dcw_ref[...] = jnp.zeros_like(dcw_ref)
            dpar_ref[...] = jnp.zeros_like(dpar_ref)

        xs = [jnp.concatenate([jnp.where(first_chunk, 0.0, h[...]), m[...]], axis=0)
              for h, m in ((hq_ref, xq_ref), (hk_ref, xk_ref), (hv_ref, xv_ref))]
        rows = [[cw_ref[k:k + 1, j * GDN_W:(j + 1) * GDN_W] for k in range(CONV_K)] for j in range(3)]
        cst = [mk_ref[0], mk_ref[1], mk_ref[2]]
        tinv = tinv_ref[...]
        fn = lambda xq, xk, xv, b, rq, rk, rv, a, d: _gdn_prep(xq, xk, xv, b, rq, rk, rv, a, d, cst, tinv=tinv)
        _, vjp = jax.vjp(fn, xs[0], xs[1], xs[2], ba_ref[...], rows[0], rows[1], rows[2], pa_ref[...], pd_ref[...])
        dxq, dxk, dxv, dba, drq, drk, drv, dpa, dpd = vjp(
            (du_ref[...], dw_ref[...], dqd_ref[...], dkt_ref[...], dattn_ref[...], degl_ref[...]))
        dxw = jnp.concatenate([dxq, dxk, dxv], axis=1)
        tail = jnp.concatenate([jnp.zeros((CHUNK - HALO, 3 * GDN_W), F32), carry[...]], axis=0)
        dp_ref[...] = (dxw[HALO:] + tail).astype(BF16)
        carry[...] = dxw[0:HALO]
        dba_ref[...] = dba.astype(BF16)
        for j, dr in enumerate((drq, drk, drv)):
            for k in range(CONV_K):
                dcw_ref[k:k + 1, j * GDN_W:(j + 1) * GDN_W] += dr[k]
        dpar_ref[0:1, :] += dpa
        dpar_ref[1:2, :] += dpd

    acc = lambda shape: pl.BlockSpec(shape, lambda i: (0, 0))
    return _call(body, "gdn_prep_bwd", (nc,),
                 [_stk(HEAD_DIM, rev)] * 4 + [_stk(STACK, rev), _stk(HEAD_DIM, rev), _stk(STACK, rev)] + main + halo
                 + [pl.BlockSpec((CHUNK, BA_PAD), lambda i: (rev(i), 0)), _whole(cw), _vec(BA_PAD), _vec(BA_PAD),
                    _whole(masks)],
                 [pl.BlockSpec((CHUNK, 3 * GDN_W), lambda i: (rev(i), 0)),
                  pl.BlockSpec((CHUNK, BA_PAD), lambda i: (rev(i), 0)), acc((CONV_K, 3 * GDN_W)), acc((8, BA_PAD))],
                 [_sds((s, 3 * GDN_W), BF16), _sds((s, BA_PAD), BF16), _sds((CONV_K, 3 * GDN_W)), _sds((8, BA_PAD))],
                 scratch=[pltpu.VMEM((HALO, 3 * GDN_W), F32)])(
                     *cts, tinv, proj, proj, proj, proj, proj, proj, ba, cw, pa, pd, masks)


def _stack_heads(v):
    return jnp.concatenate(_split(v, HEADS, 1), axis=0)


def _unstack_heads(v):
    return jnp.concatenate(_split(v, HEADS, 0), axis=1)


def _gdn_scan_fwd(prep, proj, nw):
    s = proj.shape[0]
    nc = s // CHUNK
    ident = lambda i: i

    def body(u_ref, w_ref, qd_ref, kt_ref, attn_ref, egl_ref, z_ref, nw_ref, out_ref, st_ref, state):
        @pl.when(pl.program_id(0) == 0)
        def _():
            state[...] = jnp.zeros_like(state)

        st_ref[...] = state[...]
        states = [state[h * HEAD_DIM:(h + 1) * HEAD_DIM, :] for h in range(HEADS)]
        new, out = _gdn_scan(states, u_ref[...], w_ref[...], qd_ref[...], kt_ref[...], attn_ref[...], egl_ref[...],
                             _stack_heads(z_ref[...]), nw_ref[...])
        for h in range(HEADS):
            state[h * HEAD_DIM:(h + 1) * HEAD_DIM, :] = new[h]
        out_ref[...] = _unstack_heads(out)

    return _call(body, "gdn_scan_fwd", (nc,),
                 [_stk(HEAD_DIM, ident)] * 4 + [_stk(STACK, ident), _stk(HEAD_DIM, ident),
                                                _tok(CHUNK, GDN_W, col=5), _vec(HEAD_DIM)],
                 [_tok(CHUNK, GDN_W), pl.BlockSpec((HEADS * HEAD_DIM, HEAD_DIM), lambda i: (i, 0))],
                 [_sds((s, GDN_W)), _sds((nc * HEADS * HEAD_DIM, HEAD_DIM))],
                 scratch=[pltpu.VMEM((HEADS * HEAD_DIM, HEAD_DIM), F32)])(*prep, proj, nw)


def _gdn_scan_bwd(dout, prep, st, proj, nw):
    s = proj.shape[0]
    nc = s // CHUNK
    rev = lambda i: nc - 1 - i

    def body(do_ref, u_ref, w_ref, qd_ref, kt_ref, attn_ref, egl_ref, st_ref, z_ref, nw_ref,
             du_ref, dw_ref, dqd_ref, dkt_ref, dattn_ref, degl_ref, dz_ref, dnw_ref, dstate):
        @pl.when(pl.program_id(0) == 0)
        def _():
            dstate[...] = jnp.zeros_like(dstate)
            dnw_ref[...] = jnp.zeros_like(dnw_ref)

        states = [st_ref[h * HEAD_DIM:(h + 1) * HEAD_DIM, :] for h in range(HEADS)]
        f32 = lambda ref: ref[...].astype(F32)
        _, vjp = jax.vjp(_gdn_scan, states, u_ref[...], f32(w_ref), f32(qd_ref), f32(kt_ref), f32(attn_ref),
                         egl_ref[...], _stack_heads(z_ref[...]), nw_ref[...])
        dnew = [dstate[h * HEAD_DIM:(h + 1) * HEAD_DIM, :] for h in range(HEADS)]
        dst, du, dw, dqd, dkt, dattn, degl, dz, dnw = vjp((dnew, _stack_heads(do_ref[...])))
        for h in range(HEADS):
            dstate[h * HEAD_DIM:(h + 1) * HEAD_DIM, :] = dst[h]
        for ref, val in zip((du_ref, dw_ref, dqd_ref, dkt_ref, dattn_ref, degl_ref), (du, dw, dqd, dkt, dattn, degl)):
            ref[...] = val
        dz_ref[...] = _unstack_heads(dz).astype(BF16)
        dnw_ref[0:1, :] += dnw

    tokr = lambda n, col=0: pl.BlockSpec((CHUNK, n), lambda i: (rev(i), col))
    return _call(body, "gdn_scan_bwd", (nc,),
                 [tokr(GDN_W)] + [_stk(HEAD_DIM, rev)] * 4 + [_stk(STACK, rev), _stk(HEAD_DIM, rev),
                                                             pl.BlockSpec((HEADS * HEAD_DIM, HEAD_DIM),
                                                                          lambda i: (rev(i), 0)),
                                                             tokr(GDN_W, 5), _vec(HEAD_DIM)],
                 [_stk(HEAD_DIM, rev)] * 4 + [_stk(STACK, rev), _stk(HEAD_DIM, rev), tokr(GDN_W),
                                              pl.BlockSpec((8, HEAD_DIM), lambda i: (0, 0))],
                 [_sds((nc * STACK, HEAD_DIM))] * 4 + [_sds((nc * STACK, STACK)), _sds((nc * STACK, HEAD_DIM)),
                                                       _sds((s, GDN_W), BF16), _sds((8, HEAD_DIM))],
                 scratch=[pltpu.VMEM((HEADS * HEAD_DIM, HEAD_DIM), F32)])(dout, *prep, st, proj, nw)


def _wo_specs(l):
    half = N_DEV // 2
    return [pl.BlockSpec((half, 1, D_MODEL // N_DEV, D_MODEL), functools.partial(lambda k, *_: (k, l, 0, 0), k))
            for k in range(2)]


def _wo_half(ref):
    return ref[:, 0].reshape(ref.shape[0] * ref.shape[2], ref.shape[3])


def _out_mlp_fwd(ol, og, x, wo, g1, nw2, sc2, sh2, g2, wup, wdn, l):
    s = x.shape[0]
    t = _tile(s, 512)
    nj = wup.shape[0]
    fc = wup.shape[3]

    def body(ol_ref, og_ref, x_ref, wol_ref, wog_ref, g1_ref, nw_ref, sc_ref, sh_ref, g2_ref, wup_ref, wdn_ref,
             x1_ref, mix_ref, ff_ref, x2_ref, h2_s, acc_s):
        j = pl.program_id(1)

        @pl.when(j == 0)
        def _():
            mix = _bdot(ol_ref[...], _wo_half(wol_ref)) + _bdot(og_ref[...], _wo_half(wog_ref))
            x1 = x_ref[...] + g1_ref[...] * mix
            mix_ref[...] = mix.astype(BF16)
            x1_ref[...] = x1
            h2, _, _ = _modulated_norm(x1, nw_ref[...], sc_ref[...], sh_ref[...])
            h2_s[...] = h2.astype(BF16)
            acc_s[...] = jnp.zeros_like(acc_s)

        up = _bdot(h2_s[...], wup_ref[0, 0])
        act = jnp.square(jnp.maximum(up, 0.0))
        acc_s[...] += _bdot(act, wdn_ref[0, 0])

        @pl.when(j == nj - 1)
        def _():
            ff_ref[...] = acc_s[...].astype(BF16)
            x2_ref[...] = x1_ref[...] + g2_ref[...] * acc_s[...]

    tk = lambda n: pl.BlockSpec((t, n), lambda i, j: (i, 0))
    return _call(body, "out_mlp_fwd", (s // t, nj),
                 [tk(LRU_W), tk(GDN_W), tk(D_MODEL)] + _wo_specs(l) + [_vec(D_MODEL)] * 5
                 + [pl.BlockSpec((1, 1, D_MODEL, fc), lambda i, j: (j, l, 0, 0)),
                    pl.BlockSpec((1, 1, fc, D_MODEL), lambda i, j: (j, l, 0, 0))],
                 [tk(D_MODEL)] * 4,
                 [_sds((s, D_MODEL)), _sds((s, D_MODEL), BF16), _sds((s, D_MODEL), BF16), _sds((s, D_MODEL))],
                 scratch=[pltpu.VMEM((t, D_MODEL), BF16), pltpu.VMEM((t, D_MODEL), F32)])(
                     ol, og, x, wo, wo, g1, nw2, sc2, sh2, g2, wup, wdn)


def _mlp_bwd(dx2, x1, ff, nw2, sc2, sh2, g2, wup, wdn, l):
    s = x1.shape[0]
    t = _tile(s, 512)
    nj = wup.shape[0]
    fc = wup.shape[3]

    def body(dx2_ref, x1_ref, ff_ref, nw_ref, sc_ref, sh_ref, g2_ref, wup_ref, wdn_ref,
             act_ref, dup_ref, h2_ref, dff_ref, dx1_ref, rows_ref, dh2_s):
        i, j = pl.program_id(0), pl.program_id(1)

        @pl.when((i == 0) & (j == 0))
        def _():
            rows_ref[...] = jnp.zeros_like(rows_ref)

        @pl.when(j == 0)
        def _():
            h2, _, _ = _modulated_norm(x1_ref[...], nw_ref[...], sc_ref[...], sh_ref[...])
            h2_ref[...] = h2.astype(BF16)
            dx2 = dx2_ref[...]
            dff_ref[...] = (dx2 * g2_ref[...]).astype(BF16)
            rows_ref[2:3, :] += jnp.sum(dx2 * ff_ref[...].astype(F32), axis=0, keepdims=True)
            dh2_s[...] = jnp.zeros_like(dh2_s)

        up = _bdot(h2_ref[...], wup_ref[0, 0])
        ru = jnp.maximum(up, 0.0)
        act_ref[...] = (ru * ru).astype(BF16)
        dup = (_bdot(dff_ref[...], wdn_ref[0, 0], NT) * (2.0 * ru)).astype(BF16)
        dup_ref[...] = dup
        dh2_s[...] += _bdot(dup, wup_ref[0, 0], NT)

        @pl.when(j == nj - 1)
        def _():
            xv = x1_ref[...]
            _, n, r = _modulated_norm(xv, nw_ref[...], sc_ref[...], sh_ref[...])
            dx, dsh, dsc, dnw = _modulated_norm_bwd(dh2_s[...], xv, n, r, nw_ref[...], sc_ref[...])
            dx1_ref[...] = dx2_ref[...] + dx
            rows_ref[0:1, :] += dsh
            rows_ref[1:2, :] += dsc
            rows_ref[3:4, :] += dnw

    tk = lambda n: pl.BlockSpec((t, n), lambda i, j: (i, 0))
    tj = pl.BlockSpec((t, fc), lambda i, j: (i, j))
    return _call(body, "mlp_bwd", (s // t, nj),
                 [tk(D_MODEL)] * 3 + [_vec(D_MODEL)] * 4
                 + [pl.BlockSpec((1, 1, D_MODEL, fc), lambda i, j: (j, l, 0, 0)),
                    pl.BlockSpec((1, 1, fc, D_MODEL), lambda i, j: (j, l, 0, 0))],
                 [tj, tj, tk(D_MODEL), tk(D_MODEL), tk(D_MODEL), pl.BlockSpec((8, D_MODEL), lambda i, j: (0, 0))],
                 [_sds((s, nj * fc), BF16), _sds((s, nj * fc), BF16), _sds((s, D_MODEL), BF16),
                  _sds((s, D_MODEL), BF16), _sds((s, D_MODEL)), _sds((8, D_MODEL))],
                 scratch=[pltpu.VMEM((t, D_MODEL), F32)])(dx2, x1, ff, nw2, sc2, sh2, g2, wup, wdn)


def _outproj_bwd(dx1, mix, g1, wo, l):
    s = dx1.shape[0]
    t = _tile(s, 512)

    def body(dx1_ref, mix_ref, g1_ref, wol_ref, wog_ref, dmix_ref, dol_ref, dog_ref, rows_ref):
        @pl.when(pl.program_id(0) == 0)
        def _():
            rows_ref[...] = jnp.zeros_like(rows_ref)

        dx1v = dx1_ref[...]
        rows_ref[0:1, :] += jnp.sum(dx1v * mix_ref[...].astype(F32), axis=0, keepdims=True)
        dmix = (dx1v * g1_ref[...]).astype(BF16)
        dmix_ref[...] = dmix
        dol_ref[...] = _bdot(dmix, _wo_half(wol_ref), NT)
        dog_ref[...] = _bdot(dmix, _wo_half(wog_ref), NT)

    return _call(body, "outproj_bwd", (s // t,),
                 [_tok(t, D_MODEL), _tok(t, D_MODEL), _vec(D_MODEL)] + _wo_specs(l),
                 [_tok(t, D_MODEL), _tok(t, LRU_W), _tok(t, GDN_W), pl.BlockSpec((8, D_MODEL), lambda i: (0, 0))],
                 [_sds((s, D_MODEL), BF16), _sds((s, LRU_W)), _sds((s, GDN_W)), _sds((8, D_MODEL))])(dx1, mix, g1, wo, wo)


def _tn_matmul(a, b, name, out=None, l=0, blocked=False, row_block=0):
    s, m = a.shape
    n = b.shape[1]
    ts, bm = _tile(s, 2048), _tile(m, 1024)
    bn = next(w for w in (512, 640, 384, 256, 128) if n % w == 0)

    def body(a_ref, b_ref, *rest):
        o_ref = rest[-1]

        @pl.when(pl.program_id(2) == 0)
        def _():
            o_ref[...] = jnp.zeros_like(o_ref)

        acc = _bdot(a_ref[...], b_ref[...], TN)
        o_ref[...] += acc.reshape(o_ref.shape)

    in_specs = [pl.BlockSpec((ts, bm), lambda i, j, k: (k, i)), pl.BlockSpec((ts, bn), lambda i, j, k: (k, j))]
    grid = (m // bm, n // bn, s // ts)
    if out is None:
        return _call(body, name, grid, in_specs, pl.BlockSpec((bm, bn), lambda i, j, k: (i, j)), _sds((m, n)))(a, b)
    if blocked:
        out_spec = pl.BlockSpec((1, 1, bm, bn), lambda i, j, k: (l, j, i, 0))
    else:
        out_spec = pl.BlockSpec((1, bm, bn), lambda i, j, k: (l, i + row_block * (m // bm), j))
    return _call(body, name, grid, in_specs + [pl.BlockSpec(memory_space=pl.ANY)], out_spec,
                 _sds(out.shape), aliases={2: 0})(a, b, out)


def _final_fwd_bwd(x, target, fw):
    s = x.shape[0]
    t = _tile(s, 512)

    def body(x_ref, tg_ref, fw_ref, dx_ref, rows_ref):
        @pl.when(pl.program_id(0) == 0)
        def _():
            rows_ref[...] = jnp.zeros_like(rows_ref)

        xv = x_ref[...]
        fwv = fw_ref[...]
        r = lax.rsqrt(jnp.mean(xv * xv, axis=-1, keepdims=True) + EPS)
        err = xv * r * fwv - tg_ref[...]
        part = 0.5 * jnp.sum(jnp.mean(err * err, axis=-1, keepdims=True), axis=0, keepdims=True)
        rows_ref[1:2, :] += jnp.broadcast_to(part, (1, D_MODEL))
        dy = err * (1.0 / D_MODEL)
        rows_ref[0:1, :] += jnp.sum(dy * xv * r, axis=0, keepdims=True)
        dxn = dy * fwv
        dx_ref[...] = r * dxn - xv * (r * r * r) * jnp.mean(dxn * xv, axis=-1, keepdims=True)

    return _call(body, "final_fwd_bwd", (s // t,),
                 [_tok(t, D_MODEL), _tok(t, D_MODEL), _vec(D_MODEL)],
                 [_tok(t, D_MODEL), pl.BlockSpec((8, D_MODEL), lambda i: (0, 0))],
                 [_sds((s, D_MODEL)), _sds((8, D_MODEL))])(x, target, fw)


def _adamw(w, g, m, v):
    m = ADAM_B1 * m + (1.0 - ADAM_B1) * g
    v = ADAM_B2 * v + (1.0 - ADAM_B2) * (g * g)
    m_hat = m / (1.0 - ADAM_B1 ** ADAM_STEP)
    v_hat = v / (1.0 - ADAM_B2 ** ADAM_STEP)
    return -ADAM_LR * (m_hat / (jnp.sqrt(v_hat) + ADAM_EPS) + ADAM_WD * w), m, v


def _mod_local(c_all, wmod, bmod_cols):
    nl, _, cols = wmod.shape

    def body(c_ref, w_ref, b_ref, o_ref):
        cv = c_ref[...]
        o_ref[0] = _bdot(cv * jax.nn.sigmoid(cv), w_ref[0]) + b_ref[0]

    return _call(body, "mod_local", (nl,),
                 [_whole(c_all), pl.BlockSpec((1, D_MODEL, cols), lambda l: (l, 0, 0)),
                  pl.BlockSpec((1, 1, cols), lambda l: (l, 0, 0))],
                 pl.BlockSpec((1, N_DEV, cols), lambda l: (l, 0, 0)), _sds((nl, N_DEV, cols)))(c_all, wmod, bmod_cols)


def _wmod_update(c_all, dmod_cols, w, m, v):
    nl, _, cols = w.shape

    def body(c_ref, d_ref, w_ref, m_ref, v_ref, g_ref, dl_ref, nm_ref, nv_ref):
        cv = c_ref[...]
        g = _bdot(cv * jax.nn.sigmoid(cv), d_ref[0], TN)
        g_ref[0] = g
        dl_ref[0], nm_ref[0], nv_ref[0] = _adamw(w_ref[0], g, m_ref[0], v_ref[0])

    wspec = pl.BlockSpec((1, D_MODEL, cols), lambda l: (l, 0, 0))
    return _call(body, "wmod_update", (nl,),
                 [_whole(c_all), pl.BlockSpec((1, N_DEV, cols), lambda l: (l, 0, 0)), wspec, wspec, wspec],
                 [wspec] * 4, [_sds(w.shape)] * 4)(c_all, dmod_cols, w, m, v)


def _sum_devices(gathered):
    _, r, _ = gathered.shape

    def body(g_ref, o_ref):
        acc = g_ref[0]
        for d in range(1, N_DEV):
            acc = acc + g_ref[d]
        o_ref[...] = acc

    return _call(body, "sum_devices", (1,), [_whole(gathered)], pl.BlockSpec((r, LANES), lambda i: (0, 0)),
                 _sds((r, LANES)))(gathered)


def _adam_flat(w, g, m, v):
    r = w.shape[0]

    def body(w_ref, g_ref, m_ref, v_ref, dl_ref, nm_ref, nv_ref):
        dl_ref[...], nm_ref[...], nv_ref[...] = _adamw(w_ref[...], g_ref[...], m_ref[...], v_ref[...])

    spec = pl.BlockSpec((r, LANES), lambda i: (0, 0))
    return _call(body, "adam_small", (1,), [spec] * 4, [spec] * 3, [_sds((r, LANES))] * 3)(w, g, m, v)


def _pair_add(x, p, core):
    nl, _, r, c = x.shape
    tr = _tile(r, 128 if c > 512 else 256)

    def body(core_ref, x_ref, p_ref, o_ref):
        o_ref[...] = (x_ref[...] + p_ref[...]).astype(BF16)

    return _call(body, "pair_add", (nl, 4, r // tr),
                 [pl.BlockSpec((1, 1, tr, c), lambda l, q, i, core_ref: (l, 2 * q + core_ref[0], i, 0)),
                  pl.BlockSpec((1, 1, tr, c), lambda l, q, i, core_ref: (l, q, i, 0))],
                 pl.BlockSpec((1, 1, tr, c), lambda l, q, i, core_ref: (l, q, i, 0)), _sds((nl, 4, r, c), BF16),
                 prefetch=1)(core, x, p)


def _reduce_adam(x, p, q, place, w, m, v):
    nl, _, r, c = x.shape
    tr = _tile(r, 128 if c > 512 else 256)

    def body(place_ref, x_ref, p_ref, q_ref, w_ref, m_ref, v_ref, g_ref, dl_ref, nm_ref, nv_ref):
        g = (((x_ref[0, 0] + p_ref[0, 0]) + q_ref[0, 0].astype(F32)) + q_ref[0, 1].astype(F32)) + q_ref[0, 2].astype(F32)
        g_ref[0] = g
        dl_ref[0], nm_ref[0], nv_ref[0] = _adamw(w_ref[0], g, m_ref[0], v_ref[0])

    flat = pl.BlockSpec((1, tr, c), lambda l, i, place_ref: (l, i, 0))
    return _call(body, "reduce_adam", (nl, r // tr),
                 [pl.BlockSpec((1, 1, tr, c), lambda l, i, place_ref: (l, place_ref[0], i, 0)),
                  pl.BlockSpec((1, 1, tr, c), lambda l, i, place_ref: (l, place_ref[1], i, 0)),
                  pl.BlockSpec((1, 3, tr, c), lambda l, i, place_ref: (l, 0, i, 0)), flat, flat, flat],
                 [flat] * 4, [_sds((nl, r, c))] * 4, prefetch=1)(place, x, p, q, w, m, v)


def _place():
    return lax.axis_index("x"), lax.axis_index("y"), lax.axis_index("c")


def _all_gather(xs, name, space):
    n = len(xs)

    def body(*refs):
        x_refs, o_refs = refs[:n], refs[n:2 * n]
        send_sems, recv_sems, local_sems = refs[2 * n:]
        x, y, c = _place()
        me, sibling = (x, y, c), (x, y, 1 - c)
        chips = [(1 - x, y), (x, 1 - y), (1 - x, 1 - y)]

        def blk(a, p):
            return o_refs[a].at[4 * p[0] + 2 * p[1] + p[2]]

        def copy(a, k, block, to, src=None):
            return pltpu.make_async_remote_copy(
                src_ref=blk(a, block) if src is None else src, dst_ref=blk(a, block),
                send_sem=send_sems.at[a, k], recv_sem=recv_sems.at[a, k], device_id=to, device_id_type=MESH)

        mine = [pltpu.make_async_copy(x_refs[a], blk(a, me), local_sems.at[a]) for a in range(n)]
        for cp in mine:
            cp.start()
        first = []
        for a in range(n):
            first.append(copy(a, 0, me, sibling, src=x_refs[a]))
            first += [copy(a, 1 + j, me, (*chip, c), src=x_refs[a]) for j, chip in enumerate(chips)]
        for cp in first:
            cp.start()
        passed = []
        for j, chip in enumerate(chips):
            for a in range(n):
                copy(a, 1 + j, (*chip, c), me).wait_recv()
                cp = copy(a, 4 + j, (*chip, c), sibling)
                cp.start()
                passed.append(cp)
        for a in range(n):
            copy(a, 0, sibling, me).wait_recv()
        for j, chip in enumerate(chips):
            for a in range(n):
                copy(a, 4 + j, (*chip, 1 - c), me).wait_recv()
        for cp in first + passed:
            cp.wait_send()
        for cp in mine:
            cp.wait()

    spec = pl.BlockSpec(memory_space=space)
    return pl.pallas_call(
        body, name=name, out_shape=[_sds((N_DEV,) + a.shape, a.dtype) for a in xs],
        in_specs=[spec] * n, out_specs=[spec] * n,
        scratch_shapes=[pltpu.SemaphoreType.DMA((n, 7)), pltpu.SemaphoreType.DMA((n, 7)),
                        pltpu.SemaphoreType.DMA((n,))])(*xs)


def _pair_exchange(xs):
    n = len(xs)

    def body(*refs):
        x_refs, o_refs = refs[:n], refs[n:2 * n]
        send_sems, recv_sems = refs[2 * n:]
        x, y, c = _place()
        copies = [pltpu.make_async_remote_copy(
            src_ref=x_refs[a].at[:, 2 * q + (1 - c)], dst_ref=o_refs[a].at[:, q], send_sem=send_sems.at[a, q],
            recv_sem=recv_sems.at[a, q], device_id=(x, y, 1 - c), device_id_type=MESH)
            for a in range(n) for q in range(4)]
        for cp in copies:
            cp.start()
        for cp in copies:
            cp.wait()

    spec = pl.BlockSpec(memory_space=pl.ANY)
    return pl.pallas_call(
        body, name="pair_exchange", out_shape=[_sds((a.shape[0], 4) + a.shape[2:], a.dtype) for a in xs],
        in_specs=[spec] * n, out_specs=[spec] * n,
        scratch_shapes=[pltpu.SemaphoreType.DMA((n, 4)), pltpu.SemaphoreType.DMA((n, 4))])(*xs)


def _chip_exchange(ys):
    n = len(ys)

    def body(*refs):
        y_refs, o_refs = refs[:n], refs[n:2 * n]
        send_sems, recv_sems = refs[2 * n:]
        x, y, c = _place()
        chips = [(1 - x, y), (x, 1 - y), (1 - x, 1 - y)]
        copies = [pltpu.make_async_remote_copy(
            src_ref=y_refs[a].at[:, 2 * chip[0] + chip[1]], dst_ref=o_refs[a].at[:, r], send_sem=send_sems.at[a, r],
            recv_sem=recv_sems.at[a, r], device_id=(*chip, c), device_id_type=MESH)
            for a in range(n) for r, chip in enumerate(chips)]
        for cp in copies:
            cp.start()
        for cp in copies:
            cp.wait()

    spec = pl.BlockSpec(memory_space=pl.ANY)
    return pl.pallas_call(
        body, name="chip_exchange", out_shape=[_sds((a.shape[0], 3) + a.shape[2:], a.dtype) for a in ys],
        in_specs=[spec] * n, out_specs=[spec] * n,
        scratch_shapes=[pltpu.SemaphoreType.DMA((n, 3)), pltpu.SemaphoreType.DMA((n, 3))])(*ys)


_HBM_SPEC = pl.BlockSpec(memory_space=pltpu.HBM)
_SEM_SPEC = pl.BlockSpec(memory_space=pltpu.SEMAPHORE)
_EFFECT = pltpu.SideEffectType.DATAFLOW_SIDE_EFFECTING


def _descriptors(plan, src_refs, land_refs, send_sems, recv_sems):
    return [pltpu.make_async_remote_copy(src_ref=s, dst_ref=d, send_sem=send_sems.at[k], recv_sem=recv_sems.at[k],
                                         device_id=dev, device_id_type=MESH)
            for k, (s, d, dev) in enumerate(plan(src_refs, land_refs))]


def _split_start(name, plan, n, srcs, lands, after):
    ns, nb = len(srcs), len(srcs) + len(lands)

    def body(*refs):
        for cp in _descriptors(plan, refs[:ns], refs[ns:nb], refs[nb + 1], refs[nb + 2]):
            cp.start()
        refs[-1][...] = jnp.zeros_like(refs[-1])

    bufs = [pltpu.with_memory_space_constraint(a, pltpu.HBM) for a in list(srcs) + list(lands)]
    outs = pl.pallas_call(
        body, name=name,
        out_shape=(pltpu.SemaphoreType.DMA((n,)), pltpu.SemaphoreType.DMA((n,)))
        + tuple(pltpu.HBM(a.shape, a.dtype) for a in bufs) + (_sds((8, LANES)),),
        in_specs=[_HBM_SPEC] * nb + [pl.BlockSpec(memory_space=pl.ANY)],
        out_specs=(_SEM_SPEC, _SEM_SPEC) + (_HBM_SPEC,) * nb + (pl.BlockSpec(memory_space=pltpu.VMEM),),
        input_output_aliases={i: 2 + i for i in range(nb)},
        compiler_params=pltpu.CompilerParams(has_side_effects=_EFFECT))(*bufs, after)
    return dict(send=outs[0], recv=outs[1], srcs=list(outs[2:2 + ns]), lands=list(outs[2 + ns:2 + nb]), token=outs[-1])


def _split_wait(name, plan, flight, which, after):
    srcs, lands = flight["srcs"], flight["lands"]
    ns, nb = len(srcs), len(srcs) + len(lands)

    def body(*refs):
        copies = _descriptors(plan, refs[:ns], refs[ns:nb], refs[nb], refs[nb + 1])
        for k in which:
            copies[k].wait_send()
            copies[k].wait_recv()

    outs = pl.pallas_call(
        body, name=name, out_shape=tuple(pltpu.HBM(a.shape, a.dtype) for a in srcs + lands),
        in_specs=[_HBM_SPEC] * nb + [_SEM_SPEC, _SEM_SPEC, pl.BlockSpec(memory_space=pl.ANY)],
        out_specs=(_HBM_SPEC,) * nb, input_output_aliases={i: i for i in range(nb)},
        compiler_params=pltpu.CompilerParams(has_side_effects=_EFFECT))(*srcs, *lands, flight["send"], flight["recv"],
                                                                       after)
    return dict(flight, srcs=list(outs[:ns]), lands=list(outs[ns:nb]))


GATHER_PEERS = N_DEV - 1


def _gather_plan(nl, narr):
    def plan(src_refs, land_refs):
        x, y, c = _place()
        me = 4 * x + 2 * y + c
        out = []
        for l in range(nl):
            for a in range(narr):
                for r in range(1, N_DEV):
                    peer = (1 - x if r & 4 else x, 1 - y if r & 2 else y, 1 - c if r & 1 else c)
                    out.append((src_refs[a].at[l], land_refs[a].at[me, l], peer))
        return out

    return plan


def _size(shape):
    size = 1
    for d in shape:
        size *= d
    return size


def _slab_rows(shape):
    return -(-_size(shape) // (8 * LANES)) * 8


def _pack(arrs):
    parts = []
    for a in arrs:
        flat = a.reshape(-1).astype(F32)
        parts.append(jnp.pad(flat, (0, _slab_rows(a.shape) * LANES - flat.shape[0])).reshape(-1, LANES))
    return jnp.concatenate(parts, axis=0)


def _unpack(slab, shapes):
    out, off = [], 0
    for shp in shapes:
        rows = _slab_rows(shp)
        out.append(slab[off:off + rows].reshape(-1)[:_size(shp)].reshape(shp))
        off += rows
    return out


def _dense_blocks(w):
    eye = jnp.eye(LRU_BLOCKS, dtype=w.dtype)
    return (eye[:, None, :, None] * w[:, :, None, :]).reshape(LRU_W, LRU_W)


def _diag_blocks(dense):
    return jnp.stack([dense[g * LRU_BLOCK:(g + 1) * LRU_BLOCK, g * LRU_BLOCK:(g + 1) * LRU_BLOCK]
                      for g in range(LRU_BLOCKS)])


def _alpha_lanes(v):
    return jnp.zeros((1, BA_PAD), F32).at[0, HEADS:2 * HEADS].set(v)


def _local_step(x, target, mod, p, fetch):
    nl = mod.shape[0]
    row = lambda v: v.reshape(1, -1)
    masks = _gdn_masks()
    saved = []
    xc = x
    for l in range(nl):
        win, wba, lin, wo, wup, wdn, lw, keep = fetch(l, xc)
        mv = [row(mod[l, k * D_MODEL:(k + 1) * D_MODEL]) for k in range(N_MOD)]
        sh1, sc1, g1, sh2, sc2, g2 = mv
        nw1, nw2 = row(p["norm_mix_w"][l]), row(p["norm_mlp_w"][l])
        wa, wx = _dense_blocks(p["lru_gate_a_w"][l]).astype(BF16), _dense_blocks(p["lru_gate_x_w"][l]).astype(BF16)
        lru_args = (p["lru_conv_w"][l], row(p["lru_conv_b"][l]), wa, wx, row(p["lru_gate_a_b"][l]),
                    row(p["lru_gate_x_b"][l]), row(p["lru_lambda"][l]), row(p["lru_norm_w"][l]))
        gdn_args = (p["gdn_conv_w"][l], _alpha_lanes(p["gdn_a_log"][l]), _alpha_lanes(p["gdn_dt_bias"][l]), masks)
        gnw = row(p["gdn_norm_w"][l])
        proj, ba = _inproj_fwd(xc, nw1, sc1, sh1, win, wba, lin)
        ol, hs = _lru_fwd(proj, *lru_args)
        *prep, tinv = _gdn_prep_fwd(proj, ba, *gdn_args)
        og, st = _gdn_scan_fwd(prep, proj, gnw)
        x1, mix, ff, x2 = _out_mlp_fwd(ol, og, xc, wo, g1, nw2, sc2, sh2, g2, wup, wdn, lw)
        saved.append(dict(x=xc, mv=mv, nw1=nw1, nw2=nw2, lru_args=lru_args, gdn_args=gdn_args, gnw=gnw, proj=proj,
                          ba=ba, ol=ol, hs=hs, prep=prep, tinv=tinv, og=og, st=st, x1=x1, mix=mix, ff=ff,
                          win=win, wba=wba, lin=lin, lw=lw, kept=(wo, wup, wdn) if keep else None))
        xc = x2

    dx, frows = _final_fwd_bwd(xc, target, row(p["final_norm_w"]))
    loss_part = frows[1, 0]
    small = {k: [None] * nl for k in ("norm_mix_w", "norm_mlp_w", "lru_conv_w", "lru_conv_b", "lru_gate_a_w",
                                      "lru_gate_a_b", "lru_gate_x_w", "lru_gate_x_b", "lru_lambda", "lru_norm_w",
                                      "gdn_conv_w", "gdn_a_log", "gdn_dt_bias", "gdn_norm_w")}
    fc = D_FF // N_DEV
    g_in = [None] * nl
    g_out, g_down = lax.empty((nl, D_MODEL, D_MODEL), F32), lax.empty((nl, D_FF, D_MODEL), F32)
    g_up = lax.empty((nl, N_DEV, D_MODEL, fc), F32)
    dmod = [None] * nl
    for l in reversed(range(nl)):
        sv = saved[l]
        sh1, sc1, g1, sh2, sc2, g2 = sv["mv"]
        bwo, bwup, bwdn = sv["kept"] or (wo, wup, wdn)
        act, dup, h2b, dffb, dx1, rows2 = _mlp_bwd(dx, sv["x1"], sv["ff"], sv["nw2"], sc2, sh2, g2, bwup, bwdn,
                                                   sv["lw"])
        g_up = _tn_matmul(h2b, dup, "grad_w_up", out=g_up, l=l, blocked=True)
        g_down = _tn_matmul(act, dffb, "grad_w_down", out=g_down, l=l)
        dmix, dol, dog, rows1 = _outproj_bwd(dx1, sv["mix"], g1, bwo, sv["lw"])
        g_out = _tn_matmul(sv["ol"], dmix, "grad_w_out_lru", out=g_out, l=l)
        g_out = _tn_matmul(sv["og"], dmix, "grad_w_out_gdn", out=g_out, l=l, row_block=1)
        dpl, dwa, dwx, lrows = _lru_bwd(dol, sv["proj"], sv["hs"], *sv["lru_args"])
        *cts, dpz, gnrow = _gdn_scan_bwd(dog, sv["prep"], sv["st"], sv["proj"], sv["gnw"])
        dpq, dba, dcw, dpar = _gdn_prep_bwd(cts, sv["tinv"], sv["proj"], sv["ba"], *sv["gdn_args"])
        dx, hb, rows0 = _inproj_bwd(dpl, dpq, dpz, dba, sv["x"], dx1, sv["nw1"], sc1, sh1, sv["win"], sv["wba"],
                                    sv["lin"])
        dproj = jnp.concatenate([dpl, dpq, dpz, dba], axis=1)
        g_in[l] = jnp.transpose(_tn_matmul(hb, dproj, "grad_w_in")[:, :IN_COLS].reshape(
            D_MODEL, N_DEV, IN_COLS // N_DEV), (1, 0, 2))
        dmod[l] = jnp.concatenate([rows0[0], rows0[1], rows1[0], rows2[0], rows2[1], rows2[2]])
        small["norm_mix_w"][l], small["norm_mlp_w"][l] = rows0[2], rows2[3]
        small["lru_conv_w"][l], small["lru_conv_b"][l] = lrows[8:8 + CONV_K], lrows[0]
        small["lru_gate_a_w"][l], small["lru_gate_x_w"][l] = _diag_blocks(dwa), _diag_blocks(dwx)
        small["lru_gate_a_b"][l], small["lru_gate_x_b"][l] = lrows[1], lrows[2]
        small["lru_lambda"][l], small["lru_norm_w"][l] = lrows[3], lrows[4]
        small["gdn_conv_w"][l] = dcw
        small["gdn_a_log"][l], small["gdn_dt_bias"][l] = dpar[0, HEADS:2 * HEADS], dpar[1, HEADS:2 * HEADS]
        small["gdn_norm_w"][l] = gnrow[0]
    small = {k: jnp.stack(v) for k, v in small.items()}
    small["final_norm_w"] = frows[0]
    big = dict(w_in=jnp.stack(g_in), w_out=g_out.reshape(nl, N_DEV, D_MODEL // N_DEV, D_MODEL), w_up=g_up,
               w_down=g_down.reshape(nl, N_DEV, fc, D_MODEL))
    return loss_part, dx, big, small, jnp.stack(dmod)


SMALL_REPLICATED = ("norm_mix_w", "norm_mlp_w", "b_mod", "lru_conv_b", "lru_gate_a_w", "lru_gate_a_b", "lru_gate_x_w",
                    "lru_gate_x_b", "lru_lambda", "lru_norm_w", "gdn_a_log", "gdn_dt_bias", "gdn_norm_w",
                    "final_norm_w")
SMALL_SHARDED = ("lru_conv_w", "gdn_conv_w")
WEIGHT_ORDER = ("norm_mix_w", "norm_mlp_w", "w_mod", "b_mod", "w_in", "lru_conv_w", "lru_conv_b", "lru_gate_a_w",
                "lru_gate_a_b", "lru_gate_x_w", "lru_gate_x_b", "lru_lambda", "lru_norm_w", "gdn_conv_w", "gdn_a_log",
                "gdn_dt_bias", "gdn_norm_w", "w_out", "w_up", "w_down", "final_norm_w")


def kernel(x, c, norm_mix_w, norm_mlp_w, w_mod, b_mod, w_in, lru_conv_w, lru_conv_b, lru_gate_a_w, lru_gate_a_b, lru_gate_x_w, lru_gate_x_b, lru_lambda, lru_norm_w, gdn_conv_w, gdn_a_log, gdn_dt_bias, gdn_norm_w, w_out, w_up, w_down, final_norm_w, loss_target, m_norm_mix_w, m_norm_mlp_w, m_w_mod, m_b_mod, m_w_in, m_lru_conv_w, m_lru_conv_b, m_lru_gate_a_w, m_lru_gate_a_b, m_lru_gate_x_w, m_lru_gate_x_b, m_lru_lambda, m_lru_norm_w, m_gdn_conv_w, m_gdn_a_log, m_gdn_dt_bias, m_gdn_norm_w, m_w_out, m_w_up, m_w_down, m_final_norm_w, v_norm_mix_w, v_norm_mlp_w, v_w_mod, v_b_mod, v_w_in, v_lru_conv_w, v_lru_conv_b, v_lru_gate_a_w, v_lru_gate_a_b, v_lru_gate_x_w, v_lru_gate_x_b, v_lru_lambda, v_lru_norm_w, v_gdn_conv_w, v_gdn_a_log, v_gdn_dt_bias, v_gdn_norm_w, v_w_out, v_w_up, v_w_down, v_final_norm_w):
    args = dict(locals())
    w = {k: args[k] for k in WEIGHT_ORDER}
    mom = {k: args["m_" + k] for k in WEIGHT_ORDER}
    var = {k: args["v_" + k] for k in WEIGHT_ORDER}
    nl = w_in.shape[0]
    px, py, pc = _place()
    me = 4 * px + 2 * py + pc
    core = jnp.reshape(pc, (1,)).astype(jnp.int32)

    shapes0 = [c.shape, lru_conv_w.shape, gdn_conv_w.shape]
    (g0,) = _all_gather([_pack([c, lru_conv_w, gdn_conv_w])], "gather_cond", pltpu.VMEM)
    per_dev = [_unpack(g0[d], shapes0) for d in range(N_DEV)]
    c_all = jnp.concatenate([pd[0] for pd in per_dev], axis=0)
    lru_conv_full = jnp.concatenate([pd[1] for pd in per_dev], axis=-1)
    gdn_conv_full = jnp.concatenate([pd[2] for pd in per_dev], axis=-1)

    cols = w_mod.shape[2]
    bmod_cols = lax.dynamic_slice_in_dim(b_mod, me * cols, cols, axis=1).reshape(nl, 1, cols)
    mod_cols = _mod_local(c_all, w_mod, bmod_cols)
    (g1,) = _all_gather([mod_cols.reshape(nl * N_DEV, cols)], "gather_mod", pltpu.VMEM)
    g1 = g1.reshape(N_DEV, nl, N_DEV, cols)
    mod = jnp.transpose(lax.dynamic_index_in_dim(g1, me, axis=2, keepdims=False), (1, 0, 2)).reshape(nl, N_DEV * cols)

    shards = [a.astype(BF16) for a in (w_in, w_out, w_up, w_down)]
    first = _all_gather([a[:1] for a in shards], "gather_weights_first", pl.ANY)
    plan = _gather_plan(nl - 1, len(shards))
    per_layer = len(shards) * GATHER_PEERS
    flight = []
    if nl > 1:
        rest = [a[1:] for a in shards]
        lands = [lax.dynamic_update_slice_in_dim(lax.empty((N_DEV,) + a.shape, BF16), a[None], me, axis=0)
                 for a in rest]
        flight.append(_split_start("gather_weights_start", plan, (nl - 1) * per_layer, rest, lands, first[0]))
        mod = mod + flight[0]["token"][0, 0]

    def fetch(l, after):
        if l == 0:
            gin, gout, gup, gdn = first
        else:
            flight[0] = _split_wait(f"gather_weights_wait{l}", plan, flight[0],
                                    range((l - 1) * per_layer, l * per_layer), after)
            gin, gout, gup, gdn = flight[0]["lands"]
        lw = max(l - 1, 0)
        win = jnp.transpose(gin[:, lw], (1, 0, 2)).reshape(1, D_MODEL, IN_COLS)
        wba = jnp.pad(win[:, :, IN_MAIN:], ((0, 0), (0, 0), (0, BA_PAD - (IN_COLS - IN_MAIN))))
        return win, wba, 0, gout, gup, gdn, lw, l == 0

    p = dict(w)
    p["lru_conv_w"], p["gdn_conv_w"] = lru_conv_full, gdn_conv_full

    loss_part, grad_x, big, small, dmod = _local_step(x[0], loss_target[0], mod, p, fetch)
    loss = lax.psum(loss_part, MESH_AXES)

    small_names = sorted(small)
    slab = _pack([dmod] + [small[k] for k in small_names])
    (gs,) = _all_gather([slab], "gather_small_grads", pltpu.VMEM)
    dmod_all = gs[:, :_slab_rows(dmod.shape)].reshape(N_DEV, nl, N_MOD * D_MODEL)
    summed = _unpack(_sum_devices(gs), [dmod.shape] + [small[k].shape for k in small_names])
    grads = dict(zip(small_names, summed[1:]))
    grads["b_mod"] = summed[0]
    for k, width in (("lru_conv_w", LRU_W // N_DEV), ("gdn_conv_w", 3 * GDN_W // N_DEV)):
        grads[k] = lax.dynamic_slice_in_dim(grads[k], me * width, width, axis=2)
    names = SMALL_REPLICATED + SMALL_SHARDED
    shapes = [w[k].shape for k in names]
    dl, nm, nv = _adam_flat(_pack([w[k] for k in names]), _pack([grads[k] for k in names]),
                            _pack([mom[k] for k in names]), _pack([var[k] for k in names]))
    delta = dict(zip(names, _unpack(dl, shapes)))
    new_m = dict(zip(names, _unpack(nm, shapes)))
    new_v = dict(zip(names, _unpack(nv, shapes)))

    dmod_cols = jnp.transpose(lax.dynamic_slice_in_dim(dmod_all, me * cols, cols, axis=2), (1, 0, 2))
    grads["w_mod"], delta["w_mod"], new_m["w_mod"], new_v["w_mod"] = _wmod_update(
        c_all, dmod_cols, w_mod, m_w_mod, v_w_mod)

    order = ("w_in", "w_out", "w_up", "w_down")
    xs = [big[k] for k in order]
    ps = _pair_exchange(xs)
    ys = [_pair_add(xk, pk, core) for xk, pk in zip(xs, ps)]
    qs = _chip_exchange(ys)
    place = jnp.stack([me, 2 * px + py]).astype(jnp.int32)
    for k, xk, pk, qk in zip(order, xs, ps, qs):
        grads[k], delta[k], new_m[k], new_v[k] = _reduce_adam(xk, pk, qk, place, w[k], mom[k], var[k])

    return (loss, grad_x[None], *[grads[k] for k in WEIGHT_ORDER], *[delta[k] for k in WEIGHT_ORDER],
            *[new_m[k] for k in WEIGHT_ORDER], *[new_v[k] for k in WEIGHT_ORDER])
```

```python
import functools

import jax
import jax.numpy as jnp
from jax import lax
from jax.experimental import pallas as pl
from jax.experimental.pallas import tpu as pltpu

F32 = jnp.float32
BF16 = jnp.bfloat16

D_MODEL = 1024
LRU_W = 512
LRU_BLOCKS = 8
LRU_BLOCK = 64
LRU_C = 8.0
GDN_W = 512
HEADS = 4
HEAD_DIM = 128
CHUNK = 64
STACK = HEADS * CHUNK
CONV_K = 4
D_FF = 4096
N_MOD = 6
IN_COLS = 3080
IN_MAIN = 3072
BA_PAD = 128
EPS = 1e-6
N_DEV = 8
HALO = 8
LANES = 128
ADAM_LR, ADAM_B1, ADAM_B2, ADAM_EPS, ADAM_WD, ADAM_STEP = 0.001, 0.9, 0.999, 1e-08, 0.01, 10
MESH_AXES = ("x", "y", "c")
MESH = pl.DeviceIdType.MESH

NN = (((1,), (0,)), ((), ()))
NT = (((1,), (1,)), ((), ()))
TN = (((0,), (0,)), ((), ()))


def _bdot(a, b, dims=NN):
    return lax.dot_general(a.astype(BF16), b.astype(BF16), dims, preferred_element_type=F32)


def _sdot(a, b, dims=NN):
    ah, bh = a.astype(BF16), b.astype(BF16)
    al, bl = (a - ah.astype(F32)).astype(BF16), (b - bh.astype(F32)).astype(BF16)
    return _bdot(ah, bh, dims) + (_bdot(al, bh, dims) + _bdot(ah, bl, dims))


def _hdot(a, b, dims=NN):
    return lax.dot_general(a, b, dims, precision=lax.Precision.HIGHEST, preferred_element_type=F32)


def _sds(shape, dtype=F32):
    return jax.ShapeDtypeStruct(tuple(shape), dtype)


def _tile(n, t):
    return min(n, t)


def _call(body, name, grid, in_specs, out_specs, out_shape, scratch=(), vmem_mb=48, prefetch=0, aliases=None):
    params = pltpu.CompilerParams(dimension_semantics=("arbitrary",) * len(grid), vmem_limit_bytes=vmem_mb * 2**20)
    if prefetch:
        spec = pltpu.PrefetchScalarGridSpec(num_scalar_prefetch=prefetch, grid=grid, in_specs=in_specs,
                                            out_specs=out_specs, scratch_shapes=list(scratch))
        return pl.pallas_call(body, name=name, grid_spec=spec, out_shape=out_shape, compiler_params=params,
                              input_output_aliases=aliases or {})
    return pl.pallas_call(body, name=name, grid=grid, in_specs=in_specs, out_specs=out_specs, out_shape=out_shape,
                          scratch_shapes=list(scratch), compiler_params=params, input_output_aliases=aliases or {})


def _tok(t, n, col=0):
    return pl.BlockSpec((t, n), lambda i, *_: (i, col))


def _vec(n):
    return pl.BlockSpec((1, n), lambda *_: (0, 0))


def _whole(a):
    nd = a.ndim
    return pl.BlockSpec(a.shape, lambda *_: (0,) * nd)


def _layer(l, *dims):
    return pl.BlockSpec((1,) + dims, lambda *_: (l,) + (0,) * len(dims))


def _gelu(y):
    c0, c1 = 0.7978845608028654, 0.044715
    return 0.5 * y * (1.0 + jnp.tanh(c0 * (y + c1 * y * y * y)))


def _gelu_grad(y):
    c0, c1 = 0.7978845608028654, 0.044715
    t = jnp.tanh(c0 * (y + c1 * y * y * y))
    return 0.5 * (1.0 + t) + 0.5 * y * (1.0 - t * t) * c0 * (1.0 + 3.0 * c1 * y * y)


def _softplus(v):
    return jnp.maximum(v, 0.0) + jnp.log(1.0 + jnp.exp(-jnp.where(v > 0, v, -v)))


@functools.partial(jax.custom_vjp, nondiff_argnums=(1,))
def _roll_rows(v, s):
    s = s % v.shape[0]
    return pltpu.roll(v, s, axis=0) if s else v


def _roll_rows_fwd(v, s):
    return _roll_rows(v, s), None


def _roll_rows_bwd(s, _, g):
    return (_roll_rows(g, -s),)


_roll_rows.defvjp(_roll_rows_fwd, _roll_rows_bwd)


@jax.custom_vjp
def _drop_halo(v):
    return v[HALO:]


def _drop_halo_fwd(v):
    return v[HALO:], None


def _drop_halo_bwd(_, g):
    return (jnp.concatenate([jnp.zeros((HALO, g.shape[1]), g.dtype), g], axis=0),)


_drop_halo.defvjp(_drop_halo_fwd, _drop_halo_bwd)


@functools.partial(jax.custom_vjp, nondiff_argnums=(1, 2))
def _split(v, n, axis):
    w = v.shape[axis] // n
    return tuple(lax.slice_in_dim(v, k * w, (k + 1) * w, axis=axis) for k in range(n))


def _split_fwd(v, n, axis):
    return _split(v, n, axis), None


def _split_bwd(n, axis, _, gs):
    return (jnp.concatenate(list(gs), axis=axis),)


_split.defvjp(_split_fwd, _split_bwd)


def _conv_taps(xw):
    return [_drop_halo(_roll_rows(xw, CONV_K - 1 - k)) for k in range(CONV_K)]


def _modulated_norm(xv, nw, sc, sh):
    r = lax.rsqrt(jnp.mean(xv * xv, axis=-1, keepdims=True) + EPS)
    n = xv * r * nw
    return n * (1.0 + sc) + sh, n, r


def _modulated_norm_bwd(dh, xv, n, r, nw, sc):
    dn = dh * (1.0 + sc)
    dxn = dn * nw
    dx = r * dxn - xv * (r * r * r) * jnp.mean(dxn * xv, axis=-1, keepdims=True)
    return (dx, jnp.sum(dh, axis=0, keepdims=True), jnp.sum(dh * n, axis=0, keepdims=True),
            jnp.sum(dn * xv * r, axis=0, keepdims=True))


def _inproj_fwd(x, nw, sc, sh, win, wba, l):
    s = x.shape[0]
    t = _tile(s, 256)

    def body(x_ref, nw_ref, sc_ref, sh_ref, win_ref, wba_ref, proj_ref, ba_ref):
        h, _, _ = _modulated_norm(x_ref[...], nw_ref[...], sc_ref[...], sh_ref[...])
        hb = h.astype(BF16)
        proj_ref[...] = _bdot(hb, win_ref[0])
        ba_ref[...] = _bdot(hb, wba_ref[0])

    return _call(body, "inproj_fwd", (s // t,),
                 [_tok(t, D_MODEL), _vec(D_MODEL), _vec(D_MODEL), _vec(D_MODEL), _layer(l, D_MODEL, IN_MAIN),
                  _layer(l, D_MODEL, BA_PAD)],
                 [_tok(t, IN_MAIN), _tok(t, BA_PAD)],
                 [_sds((s, IN_MAIN)), _sds((s, BA_PAD))])(x, nw, sc, sh, win, wba)


def _inproj_bwd(dpl, dpq, dpz, dba, x, dx1, nw, sc, sh, win, wba, l):
    s = x.shape[0]
    t = _tile(s, 256)

    def body(dpl_ref, dpq_ref, dpz_ref, dba_ref, x_ref, dx1_ref, nw_ref, sc_ref, sh_ref, win_ref, wba_ref,
             dx_ref, hb_ref, acc_ref):
        @pl.when(pl.program_id(0) == 0)
        def _():
            acc_ref[...] = jnp.zeros_like(acc_ref)

        dh = (_bdot(dpl_ref[...], win_ref[0, :, 0:2 * LRU_W], NT)
              + _bdot(dpq_ref[...], win_ref[0, :, 2 * LRU_W:2 * LRU_W + 3 * GDN_W], NT)
              + _bdot(dpz_ref[...], win_ref[0, :, 2 * LRU_W + 3 * GDN_W:IN_MAIN], NT)
              + _bdot(dba_ref[...], wba_ref[0], NT))
        xv = x_ref[...]
        h, n, r = _modulated_norm(xv, nw_ref[...], sc_ref[...], sh_ref[...])
        hb_ref[...] = h.astype(BF16)
        dx, dsh, dsc, dnw = _modulated_norm_bwd(dh, xv, n, r, nw_ref[...], sc_ref[...])
        dx_ref[...] = dx1_ref[...] + dx
        acc_ref[0:1, :] += dsh
        acc_ref[1:2, :] += dsc
        acc_ref[2:3, :] += dnw

    return _call(body, "inproj_bwd", (s // t,),
                 [_tok(t, 2 * LRU_W), _tok(t, 3 * GDN_W), _tok(t, GDN_W), _tok(t, BA_PAD), _tok(t, D_MODEL),
                  _tok(t, D_MODEL), _vec(D_MODEL), _vec(D_MODEL), _vec(D_MODEL), _layer(l, D_MODEL, IN_MAIN),
                  _layer(l, D_MODEL, BA_PAD)],
                 [_tok(t, D_MODEL), _tok(t, D_MODEL), pl.BlockSpec((8, D_MODEL), lambda i: (0, 0))],
                 [_sds((s, D_MODEL)), _sds((s, D_MODEL), BF16), _sds((8, D_MODEL))])(
                     dpl, dpq, dpz, dba, x, dx1, nw, sc, sh, win, wba)


def _lru_gates(xw, cw_rows, cb, wa, wx, gab, gxb, lam):
    taps = _conv_taps(xw)
    xr = cb + cw_rows[0] * taps[0] + cw_rows[1] * taps[1] + cw_rows[2] * taps[2] + cw_rows[3] * taps[3]
    xb = xr.astype(BF16)
    r = jax.nn.sigmoid(_bdot(xb, wa) + gab)
    i = jax.nn.sigmoid(_bdot(xb, wx) + gxb)
    z = jnp.exp(-jnp.where(lam > 0, lam, -lam))
    w1 = 1.0 + z
    log1p_z = jnp.where(w1 == 1.0, z, jnp.log(w1) * z / (w1 - 1.0))
    ls = jnp.minimum(lam, 0.0) - log1p_z
    la = LRU_C * r * ls
    a = jnp.exp(la)
    x2 = 2.0 * la
    u = jnp.exp(x2)
    mm_raw = jnp.where(u == 1.0, -x2,
                       jnp.where(x2 < -30.0, 1.0, (1.0 - u) * x2 / jnp.log(jnp.maximum(u, 1e-30))))
    mult = jnp.sqrt(jnp.maximum(mm_raw, 1e-12))
    return dict(taps=taps, xr=xr, r=r, i=i, ls=ls, a=a, mm_raw=mm_raw, mult=mult)


def _lru_specs(s, t, tile_of):
    nh = t // HALO
    xl = pl.BlockSpec((t, LRU_W), lambda i: (tile_of(i), 0))
    yl = pl.BlockSpec((t, LRU_W), lambda i: (tile_of(i), 1))
    hx = pl.BlockSpec((HALO, LRU_W), lambda i: (jnp.maximum(tile_of(i) * nh - 1, 0), 0))
    return xl, yl, hx


def _lru_fwd(proj, cw, cb, wa, wx, gab, gxb, lam, lnw):
    s = proj.shape[0]
    t = _tile(s, 256)
    xl, yl, hx = _lru_specs(s, t, lambda i: i)

    def body(xl_ref, yl_ref, hx_ref, cw_ref, cb_ref, wa_ref, wx_ref, gab_ref, gxb_ref, lam_ref, lnw_ref,
             out_ref, h_ref, a_s, b_s, hc):
        i = pl.program_id(0)

        @pl.when(i == 0)
        def _():
            hc[...] = jnp.zeros_like(hc)

        halo = jnp.where(i > 0, hx_ref[...], 0.0)
        xw = jnp.concatenate([halo, xl_ref[...]], axis=0)
        g = _lru_gates(xw, [cw_ref[k:k + 1, :] for k in range(CONV_K)], cb_ref[...], wa_ref[...], wx_ref[...],
                       gab_ref[...], gxb_ref[...], lam_ref[...])
        a_s[...] = g["a"]
        b_s[...] = g["mult"] * (g["i"] * g["xr"])

        def step(k, h):
            h = a_s[pl.ds(k, 1), :] * h + b_s[pl.ds(k, 1), :]
            h_ref[pl.ds(k, 1), :] = h
            return h

        hc[...] = lax.fori_loop(0, t, step, hc[...], unroll=8)
        m = h_ref[...] * _gelu(yl_ref[...])
        out_ref[...] = m * lax.rsqrt(jnp.mean(m * m, axis=-1, keepdims=True) + EPS) * lnw_ref[...]

    return _call(body, "lru_fwd", (s // t,),
                 [xl, yl, hx, _whole(cw), _vec(LRU_W), _whole(wa), _whole(wx)] + [_vec(LRU_W)] * 4,
                 [_tok(t, LRU_W), _tok(t, LRU_W)],
                 [_sds((s, LRU_W)), _sds((s, LRU_W))],
                 scratch=[pltpu.VMEM((t, LRU_W), F32), pltpu.VMEM((t, LRU_W), F32), pltpu.VMEM((1, LRU_W), F32)])(
                     proj, proj, proj, cw, cb, wa, wx, gab, gxb, lam, lnw)


def _lru_bwd(dout, proj, hs, cw, cb, wa, wx, gab, gxb, lam, lnw):
    s = proj.shape[0]
    t = _tile(s, 256)
    nt = s // t
    rev = lambda i: nt - 1 - i
    xl, yl, hx = _lru_specs(s, t, rev)
    nh = t // HALO
    tk = pl.BlockSpec((t, LRU_W), lambda i: (rev(i), 0))
    hh = pl.BlockSpec((HALO, LRU_W), lambda i: (jnp.maximum(rev(i) * nh - 1, 0), 0))

    def body(do_ref, xl_ref, yl_ref, hx_ref, h_ref, hh_ref, cw_ref, cb_ref, wa_ref, wx_ref, gab_ref, gxb_ref,
             lam_ref, lnw_ref, dp_ref, dwa_ref, dwx_ref, rows_ref, dh_s, dhd_s, carry, dxr_next):
        i = pl.program_id(0)
        first_tile = rev(i) == 0

        @pl.when(i == 0)
        def _():
            carry[...] = jnp.zeros_like(carry)
            dxr_next[...] = jnp.zeros_like(dxr_next)
            dwa_ref[...] = jnp.zeros_like(dwa_ref)
            dwx_ref[...] = jnp.zeros_like(dwx_ref)
            rows_ref[...] = jnp.zeros_like(rows_ref)

        halo = jnp.where(first_tile, 0.0, hx_ref[...])
        xw = jnp.concatenate([halo, xl_ref[...]], axis=0)
        cw_rows = [cw_ref[k:k + 1, :] for k in range(CONV_K)]
        lam_v = lam_ref[...]
        g = _lru_gates(xw, cw_rows, cb_ref[...], wa_ref[...], wx_ref[...], gab_ref[...], gxb_ref[...], lam_v)
        a, r, gi, xr, mult = g["a"], g["r"], g["i"], g["xr"], g["mult"]
        hv = h_ref[...]
        yv = yl_ref[...]
        gl = _gelu(yv)
        m = hv * gl
        rn = lax.rsqrt(jnp.mean(m * m, axis=-1, keepdims=True) + EPS)
        dov = do_ref[...]
        dmn = dov * lnw_ref[...]
        rows_ref[4:5, :] += jnp.sum(dov * m * rn, axis=0, keepdims=True)
        dm = rn * dmn - m * (rn * rn * rn) * jnp.mean(dmn * m, axis=-1, keepdims=True)
        dhd_s[...] = dm * gl
        dy = dm * hv * _gelu_grad(yv)
        dh_s[...] = a

        def step(k, c):
            row = t - 1 - k
            d = dhd_s[pl.ds(row, 1), :] + c
            c = dh_s[pl.ds(row, 1), :] * d
            dh_s[pl.ds(row, 1), :] = d
            return c

        carry[...] = lax.fori_loop(0, t, step, carry[...], unroll=8)
        dH = dh_s[...]
        hprev_halo = jnp.where(first_tile, 0.0, hh_ref[...])
        hprev = _drop_halo(_roll_rows(jnp.concatenate([hprev_halo, hv], axis=0), 1))
        da = dH * hprev
        dmult = dH * gi * xr
        di = dH * mult * xr
        dxr = dH * mult * gi
        dla = jnp.where(g["mm_raw"] > 1e-12, dmult * (0.5 / mult) * (-2.0 * a * a), 0.0) + da * a
        dr = dla * (LRU_C * g["ls"])
        sig_neg = jax.nn.sigmoid(-lam_v)
        rows_ref[3:4, :] += jnp.sum(dla * (LRU_C * r), axis=0, keepdims=True) * sig_neg
        drp = dr * r * (1.0 - r)
        dip = di * gi * (1.0 - gi)
        rows_ref[1:2, :] += jnp.sum(drp, axis=0, keepdims=True)
        rows_ref[2:3, :] += jnp.sum(dip, axis=0, keepdims=True)
        xb = xr.astype(BF16)
        drb = drp.astype(BF16)
        dib = dip.astype(BF16)
        dwa_ref[...] += _bdot(xb, drb, TN)
        dwx_ref[...] += _bdot(xb, dib, TN)
        dxr = dxr + _bdot(drb, wa_ref[...], NT) + _bdot(dib, wx_ref[...], NT)
        rows_ref[0:1, :] += jnp.sum(dxr, axis=0, keepdims=True)
        ext = jnp.concatenate([dxr, dxr_next[...]], axis=0)
        dx = cw_rows[CONV_K - 1] * dxr
        for k in range(CONV_K - 1):
            dx = dx + cw_rows[k] * _roll_rows(ext, -(CONV_K - 1 - k))[0:t]
        for k in range(CONV_K):
            rows_ref[8 + k:9 + k, :] += jnp.sum(dxr * g["taps"][k], axis=0, keepdims=True)
        dxr_next[...] = dxr[0:HALO]
        dp_ref[...] = jnp.concatenate([dx, dy], axis=1).astype(BF16)

    acc = lambda shape: pl.BlockSpec(shape, lambda i: (0, 0))
    return _call(body, "lru_bwd", (nt,),
                 [tk, xl, yl, hx, tk, hh, _whole(cw), _vec(LRU_W), _whole(wa), _whole(wx)] + [_vec(LRU_W)] * 4,
                 [pl.BlockSpec((t, 2 * LRU_W), lambda i: (rev(i), 0)), acc((LRU_W, LRU_W)), acc((LRU_W, LRU_W)),
                  acc((16, LRU_W))],
                 [_sds((s, 2 * LRU_W), BF16), _sds((LRU_W, LRU_W)), _sds((LRU_W, LRU_W)), _sds((16, LRU_W))],
                 scratch=[pltpu.VMEM((t, LRU_W), F32), pltpu.VMEM((t, LRU_W), F32), pltpu.VMEM((1, LRU_W), F32),
                          pltpu.VMEM((HALO, LRU_W), F32)])(
                     dout, proj, proj, proj, hs, hs, cw, cb, wa, wx, gab, gxb, lam, lnw)


def _gdn_masks():
    row = lax.broadcasted_iota(jnp.int32, (STACK, STACK), 0)
    col = lax.broadcasted_iota(jnp.int32, (STACK, STACK), 1)
    same = (row // CHUNK) == (col // CHUNK)
    return jnp.stack([(same & (col <= row)).astype(F32), (same & (col < row)).astype(F32), (row == col).astype(F32)])


def _conv_silu(xw, rows):
    taps = _conv_taps(xw)
    y = rows[0] * taps[0] + rows[1] * taps[1] + rows[2] * taps[2] + rows[3] * taps[3]
    return y * jax.nn.sigmoid(y)


def _split3(v):
    hi = v.astype(BF16)
    r1 = v - hi.astype(F32)
    mid = r1.astype(BF16)
    return hi, mid, (r1 - mid.astype(F32)).astype(BF16)


def _mask_dot_raw(mask, v, dims):
    parts = _split3(v)
    d = lambda p: lax.dot_general(mask, p, dims, preferred_element_type=F32)
    return d(parts[0]) + (d(parts[1]) + d(parts[2]))


@jax.custom_vjp
def _mask_dot(mask, v):
    return _mask_dot_raw(mask, v, NN)


def _mask_dot_fwd(mask, v):
    return _mask_dot_raw(mask, v, NN), mask


def _mask_dot_bwd(mask, ct):
    return jnp.zeros_like(mask), _mask_dot_raw(mask, ct, TN)


_mask_dot.defvjp(_mask_dot_fwd, _mask_dot_bwd)


def _unit_lower_inverse(n, eye):
    tinv = eye + n
    p = n
    for _ in range(5):
        p = _bdot(p, p)
        tinv = tinv + _bdot(tinv, p)
    return tinv.astype(BF16)


@jax.custom_vjp
def _unit_lower_solve(n, rhs, tinv):
    x0 = _bdot(tinv, rhs)
    return x0 + _bdot(tinv, rhs - x0 + _sdot(n, x0))


def _unit_lower_solve_fwd(n, rhs, tinv):
    x = _unit_lower_solve(n, rhs, tinv)
    return x, (n, tinv, x)


def _unit_lower_solve_bwd(res, ct):
    n, tinv, x = res
    y0 = _bdot(tinv, ct, TN)
    y = y0 + _bdot(tinv, ct - y0 + _sdot(n, y0, TN), TN)
    return _bdot(y, x, NT), y, jnp.zeros_like(tinv)


_unit_lower_solve.defvjp(_unit_lower_solve_fwd, _unit_lower_solve_bwd)


def _gdn_prep(xq, xk, xv, ba, cwq, cwk, cwv, pa, pd, masks, tinv=None, with_inverse=False):
    lower, strict, eye = masks[0], masks[1], masks[2]
    lower_b = lower.astype(BF16)
    lane = lax.broadcasted_iota(jnp.int32, (CHUNK, LANES), 1)
    q = jnp.concatenate(_split(_conv_silu(xq, cwq), HEADS, 1), axis=0)
    k = jnp.concatenate(_split(_conv_silu(xk, cwk), HEADS, 1), axis=0)
    v = jnp.concatenate(_split(_conv_silu(xv, cwv), HEADS, 1), axis=0)
    qn = q * lax.rsqrt(jnp.sum(q * q, axis=-1, keepdims=True) + 1e-6) * (HEAD_DIM ** -0.5)
    kn = k * lax.rsqrt(jnp.sum(k * k, axis=-1, keepdims=True) + 1e-6)
    beta_f = jax.nn.sigmoid(ba)
    g_f = -jnp.exp(pa) * _softplus(ba + pd)

    def col(a, j):
        return jnp.broadcast_to(jnp.sum(jnp.where(lane == j, a, 0.0), axis=1, keepdims=True), (CHUNK, HEAD_DIM))

    beta = jnp.concatenate([col(beta_f, h) for h in range(HEADS)], axis=0)
    gs = [col(g_f, HEADS + h) for h in range(HEADS)]
    g = jnp.concatenate(gs, axis=0)
    gl = jnp.concatenate([jnp.broadcast_to(jnp.sum(gh, axis=0, keepdims=True), (CHUNK, HEAD_DIM)) for gh in gs], axis=0)
    gc = _mask_dot(lower_b, g)
    gc_rows = jnp.transpose(gc)
    decay = jnp.exp((jnp.concatenate([gc, gc], axis=1) - jnp.concatenate([gc_rows, gc_rows], axis=0)) * lower)
    egc = jnp.exp(gc)
    kb = kn * beta
    n = -(_bdot(kb, kn, NT) * decay * strict)
    if tinv is None:
        tinv = _unit_lower_inverse(lax.stop_gradient(n), eye)
    u, w = _split(_unit_lower_solve(n, jnp.concatenate([v * beta, kb * egc], axis=1), tinv), 2, 1)
    attn = _bdot(qn, kn, NT) * decay * lower
    outs = (u, w, qn * egc, kn * jnp.exp(gl - gc), attn, jnp.exp(gl))
    return outs + (tinv,) if with_inverse else outs


def _gdn_scan(states, u, w, qd, kt, attn, egl, z, nw):
    us, ws, qds, kts, egls = (_split(a, HEADS, 0) for a in (u, w, qd, kt, egl))
    vn = [us[h] - _bdot(ws[h], states[h]) for h in range(HEADS)]
    o = jnp.concatenate([_bdot(qds[h], states[h]) for h in range(HEADS)], axis=0)
    o = o + _bdot(attn, jnp.concatenate(vn, axis=0))
    new = [states[h] * jnp.concatenate([egls[h], egls[h]], axis=0) + _bdot(kts[h], vn[h], TN) for h in range(HEADS)]
    on = o * lax.rsqrt(jnp.mean(o * o, axis=-1, keepdims=True) + EPS) * nw
    return new, on * (z * jax.nn.sigmoid(z))


def _gdn_in_specs(s, chunk_of):
    nh = CHUNK // HALO
    main = [pl.BlockSpec((CHUNK, GDN_W), functools.partial(lambda col, i: (chunk_of(i), col), col))
            for col in (2, 3, 4)]
    halo = [pl.BlockSpec((HALO, GDN_W), functools.partial(lambda col, i: (jnp.maximum(chunk_of(i) * nh - 1, 0), col),
                                                         col)) for col in (2, 3, 4)]
    return main, halo


def _stk(width, chunk_of):
    return pl.BlockSpec((STACK, width), lambda i: (chunk_of(i), 0))


def _gdn_prep_fwd(proj, ba, cw, pa, pd, masks):
    s = proj.shape[0]
    nc = s // CHUNK
    main, halo = _gdn_in_specs(s, lambda i: i)

    def body(xq_ref, xk_ref, xv_ref, hq_ref, hk_ref, hv_ref, ba_ref, cw_ref, pa_ref, pd_ref, mk_ref,
             u_ref, w_ref, qd_ref, kt_ref, attn_ref, egl_ref, tinv_ref):
        i = pl.program_id(0)
        xs = [jnp.concatenate([jnp.where(i > 0, h[...], 0.0), m[...]], axis=0)
              for h, m in ((hq_ref, xq_ref), (hk_ref, xk_ref), (hv_ref, xv_ref))]
        rows = [[cw_ref[k:k + 1, j * GDN_W:(j + 1) * GDN_W] for k in range(CONV_K)] for j in range(3)]
        outs = _gdn_prep(xs[0], xs[1], xs[2], ba_ref[...], rows[0], rows[1], rows[2], pa_ref[...], pd_ref[...],
                         [mk_ref[0], mk_ref[1], mk_ref[2]], with_inverse=True)
        for ref, val in zip((u_ref, w_ref, qd_ref, kt_ref, attn_ref, egl_ref, tinv_ref), outs):
            ref[...] = val.astype(ref.dtype)

    ident = lambda i: i
    stacked = lambda dt: _sds((nc * STACK, HEAD_DIM), dt)
    return _call(body, "gdn_prep_fwd", (nc,),
                 main + halo + [_tok(CHUNK, BA_PAD), _whole(cw), _vec(BA_PAD), _vec(BA_PAD), _whole(masks)],
                 [_stk(HEAD_DIM, ident)] * 4 + [_stk(STACK, ident), _stk(HEAD_DIM, ident), _stk(STACK, ident)],
                 [stacked(F32), stacked(BF16), stacked(BF16), stacked(BF16), _sds((nc * STACK, STACK), BF16),
                  stacked(F32), _sds((nc * STACK, STACK), BF16)])(
                     proj, proj, proj, proj, proj, proj, ba, cw, pa, pd, masks)


def _gdn_prep_bwd(cts, tinv, proj, ba, cw, pa, pd, masks):
    s = proj.shape[0]
    nc = s // CHUNK
    rev = lambda i: nc - 1 - i
    main, halo = _gdn_in_specs(s, rev)

    def body(du_ref, dw_ref, dqd_ref, dkt_ref, dattn_ref, degl_ref, tinv_ref, xq_ref, xk_ref, xv_ref, hq_ref, hk_ref,
             hv_ref, ba_ref, cw_ref, pa_ref, pd_ref, mk_ref, dp_ref, dba_ref, dcw_ref, dpar_ref, carry):
        i = pl.program_id(0)
        first_chunk = rev(i) == 0

        @pl.when(i == 0)
        def _():
            carry[...] = jnp.zeros_like(carry)
            dcw_ref[...] = jnp.zeros_like(dcw_ref)
            dpar_ref[...] = jnp.zeros_like(dpar_ref)

        xs = [jnp.concatenate([jnp.where(first_chunk, 0.0, h[...]), m[...]], axis=0)
              for h, m in ((hq_ref, xq_ref), (hk_ref, xk_ref), (hv_ref, xv_ref))]
        rows = [[cw_ref[k:k + 1, j * GDN_W:(j + 1) * GDN_W] for k in range(CONV_K)] for j in range(3)]
        cst = [mk_ref[0], mk_ref[1], mk_ref[2]]
        tinv = tinv_ref[...]
        fn = lambda xq, xk, xv, b, rq, rk, rv, a, d: _gdn_prep(xq, xk, xv, b, rq, rk, rv, a, d, cst, tinv=tinv)
        _, vjp = jax.vjp(fn, xs[0], xs[1], xs[2], ba_ref[...], rows[0], rows[1], rows[2], pa_ref[...], pd_ref[...])
        dxq, dxk, dxv, dba, drq, drk, drv, dpa, dpd = vjp(
            (du_ref[...], dw_ref[...], dqd_ref[...], dkt_ref[...], dattn_ref[...], degl_ref[...]))
        dxw = jnp.concatenate([dxq, dxk, dxv], axis=1)
        tail = jnp.concatenate([jnp.zeros((CHUNK - HALO, 3 * GDN_W), F32), carry[...]], axis=0)
        dp_ref[...] = (dxw[HALO:] + tail).astype(BF16)
        carry[...] = dxw[0:HALO]
        dba_ref[...] = dba.astype(BF16)
        for j, dr in enumerate((drq, drk, drv)):
            for k in range(CONV_K):
                dcw_ref[k:k + 1, j * GDN_W:(j + 1) * GDN_W] += dr[k]
        dpar_ref[0:1, :] += dpa
        dpar_ref[1:2, :] += dpd

    acc = lambda shape: pl.BlockSpec(shape, lambda i: (0, 0))
    return _call(body, "gdn_prep_bwd", (nc,),
                 [_stk(HEAD_DIM, rev)] * 4 + [_stk(STACK, rev), _stk(HEAD_DIM, rev), _stk(STACK, rev)] + main + halo
                 + [pl.BlockSpec((CHUNK, BA_PAD), lambda i: (rev(i), 0)), _whole(cw), _vec(BA_PAD), _vec(BA_PAD),
                    _whole(masks)],
                 [pl.BlockSpec((CHUNK, 3 * GDN_W), lambda i: (rev(i), 0)),
                  pl.BlockSpec((CHUNK, BA_PAD), lambda i: (rev(i), 0)), acc((CONV_K, 3 * GDN_W)), acc((8, BA_PAD))],
                 [_sds((s, 3 * GDN_W), BF16), _sds((s, BA_PAD), BF16), _sds((CONV_K, 3 * GDN_W)), _sds((8, BA_PAD))],
                 scratch=[pltpu.VMEM((HALO, 3 * GDN_W), F32)])(
                     *cts, tinv, proj, proj, proj, proj, proj, proj, ba, cw, pa, pd, masks)


def _stack_heads(v):
    return jnp.concatenate(_split(v, HEADS, 1), axis=0)


def _unstack_heads(v):
    return jnp.concatenate(_split(v, HEADS, 0), axis=1)


def _gdn_scan_fwd(prep, proj, nw):
    s = proj.shape[0]
    nc = s // CHUNK
    ident = lambda i: i

    def body(u_ref, w_ref, qd_ref, kt_ref, attn_ref, egl_ref, z_ref, nw_ref, out_ref, st_ref, state):
        @pl.when(pl.program_id(0) == 0)
        def _():
            state[...] = jnp.zeros_like(state)

        st_ref[...] = state[...]
        states = [state[h * HEAD_DIM:(h + 1) * HEAD_DIM, :] for h in range(HEADS)]
        new, out = _gdn_scan(states, u_ref[...], w_ref[...], qd_ref[...], kt_ref[...], attn_ref[...], egl_ref[...],
                             _stack_heads(z_ref[...]), nw_ref[...])
        for h in range(HEADS):
            state[h * HEAD_DIM:(h + 1) * HEAD_DIM, :] = new[h]
        out_ref[...] = _unstack_heads(out)

    return _call(body, "gdn_scan_fwd", (nc,),
                 [_stk(HEAD_DIM, ident)] * 4 + [_stk(STACK, ident), _stk(HEAD_DIM, ident),
                                                _tok(CHUNK, GDN_W, col=5), _vec(HEAD_DIM)],
                 [_tok(CHUNK, GDN_W), pl.BlockSpec((HEADS * HEAD_DIM, HEAD_DIM), lambda i: (i, 0))],
                 [_sds((s, GDN_W)), _sds((nc * HEADS * HEAD_DIM, HEAD_DIM))],
                 scratch=[pltpu.VMEM((HEADS * HEAD_DIM, HEAD_DIM), F32)])(*prep, proj, nw)


def _gdn_scan_bwd(dout, prep, st, proj, nw):
    s = proj.shape[0]
    nc = s // CHUNK
    rev = lambda i: nc - 1 - i

    def body(do_ref, u_ref, w_ref, qd_ref, kt_ref, attn_ref, egl_ref, st_ref, z_ref, nw_ref,
             du_ref, dw_ref, dqd_ref, dkt_ref, dattn_ref, degl_ref, dz_ref, dnw_ref, dstate):
        @pl.when(pl.program_id(0) == 0)
        def _():
            dstate[...] = jnp.zeros_like(dstate)
            dnw_ref[...] = jnp.zeros_like(dnw_ref)

        states = [st_ref[h * HEAD_DIM:(h + 1) * HEAD_DIM, :] for h in range(HEADS)]
        f32 = lambda ref: ref[...].astype(F32)
        _, vjp = jax.vjp(_gdn_scan, states, u_ref[...], f32(w_ref), f32(qd_ref), f32(kt_ref), f32(attn_ref),
                         egl_ref[...], _stack_heads(z_ref[...]), nw_ref[...])
        dnew = [dstate[h * HEAD_DIM:(h + 1) * HEAD_DIM, :] for h in range(HEADS)]
        dst, du, dw, dqd, dkt, dattn, degl, dz, dnw = vjp((dnew, _stack_heads(do_ref[...])))
        for h in range(HEADS):
            dstate[h * HEAD_DIM:(h + 1) * HEAD_DIM, :] = dst[h]
        for ref, val in zip((du_ref, dw_ref, dqd_ref, dkt_ref, dattn_ref, degl_ref), (du, dw, dqd, dkt, dattn, degl)):
            ref[...] = val
        dz_ref[...] = _unstack_heads(dz).astype(BF16)
        dnw_ref[0:1, :] += dnw

    tokr = lambda n, col=0: pl.BlockSpec((CHUNK, n), lambda i: (rev(i), col))
    return _call(body, "gdn_scan_bwd", (nc,),
                 [tokr(GDN_W)] + [_stk(HEAD_DIM, rev)] * 4 + [_stk(STACK, rev), _stk(HEAD_DIM, rev),
                                                             pl.BlockSpec((HEADS * HEAD_DIM, HEAD_DIM),
                                                                          lambda i: (rev(i), 0)),
                                                             tokr(GDN_W, 5), _vec(HEAD_DIM)],
                 [_stk(HEAD_DIM, rev)] * 4 + [_stk(STACK, rev), _stk(HEAD_DIM, rev), tokr(GDN_W),
                                              pl.BlockSpec((8, HEAD_DIM), lambda i: (0, 0))],
                 [_sds((nc * STACK, HEAD_DIM))] * 4 + [_sds((nc * STACK, STACK)), _sds((nc * STACK, HEAD_DIM)),
                                                       _sds((s, GDN_W), BF16), _sds((8, HEAD_DIM))],
                 scratch=[pltpu.VMEM((HEADS * HEAD_DIM, HEAD_DIM), F32)])(dout, *prep, st, proj, nw)


def _wo_specs(l):
    half = N_DEV // 2
    return [pl.BlockSpec((half, 1, D_MODEL // N_DEV, D_MODEL), functools.partial(lambda k, *_: (k, l, 0, 0), k))
            for k in range(2)]


def _wo_half(ref):
    return ref[:, 0].reshape(ref.shape[0] * ref.shape[2], ref.shape[3])


def _out_mlp_fwd(ol, og, x, wo, g1, nw2, sc2, sh2, g2, wup, wdn, l):
    s = x.shape[0]
    t = _tile(s, 512)
    nj = wup.shape[0]
    fc = wup.shape[3]

    def body(ol_ref, og_ref, x_ref, wol_ref, wog_ref, g1_ref, nw_ref, sc_ref, sh_ref, g2_ref, wup_ref, wdn_ref,
             x1_ref, mix_ref, ff_ref, x2_ref, h2_s, acc_s):
        j = pl.program_id(1)

        @pl.when(j == 0)
        def _():
            mix = _bdot(ol_ref[...], _wo_half(wol_ref)) + _bdot(og_ref[...], _wo_half(wog_ref))
            x1 = x_ref[...] + g1_ref[...] * mix
            mix_ref[...] = mix.astype(BF16)
            x1_ref[...] = x1
            h2, _, _ = _modulated_norm(x1, nw_ref[...], sc_ref[...], sh_ref[...])
            h2_s[...] = h2.astype(BF16)
            acc_s[...] = jnp.zeros_like(acc_s)

        up = _bdot(h2_s[...], wup_ref[0, 0])
        act = jnp.square(jnp.maximum(up, 0.0))
        acc_s[...] += _bdot(act, wdn_ref[0, 0])

        @pl.when(j == nj - 1)
        def _():
            ff_ref[...] = acc_s[...].astype(BF16)
            x2_ref[...] = x1_ref[...] + g2_ref[...] * acc_s[...]

    tk = lambda n: pl.BlockSpec((t, n), lambda i, j: (i, 0))
    return _call(body, "out_mlp_fwd", (s // t, nj),
                 [tk(LRU_W), tk(GDN_W), tk(D_MODEL)] + _wo_specs(l) + [_vec(D_MODEL)] * 5
                 + [pl.BlockSpec((1, 1, D_MODEL, fc), lambda i, j: (j, l, 0, 0)),
                    pl.BlockSpec((1, 1, fc, D_MODEL), lambda i, j: (j, l, 0, 0))],
                 [tk(D_MODEL)] * 4,
                 [_sds((s, D_MODEL)), _sds((s, D_MODEL), BF16), _sds((s, D_MODEL), BF16), _sds((s, D_MODEL))],
                 scratch=[pltpu.VMEM((t, D_MODEL), BF16), pltpu.VMEM((t, D_MODEL), F32)])(
                     ol, og, x, wo, wo, g1, nw2, sc2, sh2, g2, wup, wdn)


def _mlp_bwd(dx2, x1, ff, nw2, sc2, sh2, g2, wup, wdn, l):
    s = x1.shape[0]
    t = _tile(s, 512)
    nj = wup.shape[0]
    fc = wup.shape[3]

    def body(dx2_ref, x1_ref, ff_ref, nw_ref, sc_ref, sh_ref, g2_ref, wup_ref, wdn_ref,
             act_ref, dup_ref, h2_ref, dff_ref, dx1_ref, rows_ref, dh2_s):
        i, j = pl.program_id(0), pl.program_id(1)

        @pl.when((i == 0) & (j == 0))
        def _():
            rows_ref[...] = jnp.zeros_like(rows_ref)

        @pl.when(j == 0)
        def _():
            h2, _, _ = _modulated_norm(x1_ref[...], nw_ref[...], sc_ref[...], sh_ref[...])
            h2_ref[...] = h2.astype(BF16)
            dx2 = dx2_ref[...]
            dff_ref[...] = (dx2 * g2_ref[...]).astype(BF16)
            rows_ref[2:3, :] += jnp.sum(dx2 * ff_ref[...].astype(F32), axis=0, keepdims=True)
            dh2_s[...] = jnp.zeros_like(dh2_s)

        up = _bdot(h2_ref[...], wup_ref[0, 0])
        ru = jnp.maximum(up, 0.0)
        act_ref[...] = (ru * ru).astype(BF16)
        dup = (_bdot(dff_ref[...], wdn_ref[0, 0], NT) * (2.0 * ru)).astype(BF16)
        dup_ref[...] = dup
        dh2_s[...] += _bdot(dup, wup_ref[0, 0], NT)

        @pl.when(j == nj - 1)
        def _():
            xv = x1_ref[...]
            _, n, r = _modulated_norm(xv, nw_ref[...], sc_ref[...], sh_ref[...])
            dx, dsh, dsc, dnw = _modulated_norm_bwd(dh2_s[...], xv, n, r, nw_ref[...], sc_ref[...])
            dx1_ref[...] = dx2_ref[...] + dx
            rows_ref[0:1, :] += dsh
            rows_ref[1:2, :] += dsc
            rows_ref[3:4, :] += dnw

    tk = lambda n: pl.BlockSpec((t, n), lambda i, j: (i, 0))
    tj = pl.BlockSpec((t, fc), lambda i, j: (i, j))
    return _call(body, "mlp_bwd", (s // t, nj),
                 [tk(D_MODEL)] * 3 + [_vec(D_MODEL)] * 4
                 + [pl.BlockSpec((1, 1, D_MODEL, fc), lambda i, j: (j, l, 0, 0)),
                    pl.BlockSpec((1, 1, fc, D_MODEL), lambda i, j: (j, l, 0, 0))],
                 [tj, tj, tk(D_MODEL), tk(D_MODEL), tk(D_MODEL), pl.BlockSpec((8, D_MODEL), lambda i, j: (0, 0))],
                 [_sds((s, nj * fc), BF16), _sds((s, nj * fc), BF16), _sds((s, D_MODEL), BF16),
                  _sds((s, D_MODEL), BF16), _sds((s, D_MODEL)), _sds((8, D_MODEL))],
                 scratch=[pltpu.VMEM((t, D_MODEL), F32)])(dx2, x1, ff, nw2, sc2, sh2, g2, wup, wdn)


def _outproj_bwd(dx1, mix, g1, wo, l):
    s = dx1.shape[0]
    t = _tile(s, 512)

    def body(dx1_ref, mix_ref, g1_ref, wol_ref, wog_ref, dmix_ref, dol_ref, dog_ref, rows_ref):
        @pl.when(pl.program_id(0) == 0)
        def _():
            rows_ref[...] = jnp.zeros_like(rows_ref)

        dx1v = dx1_ref[...]
        rows_ref[0:1, :] += jnp.sum(dx1v * mix_ref[...].astype(F32), axis=0, keepdims=True)
        dmix = (dx1v * g1_ref[...]).astype(BF16)
        dmix_ref[...] = dmix
        dol_ref[...] = _bdot(dmix, _wo_half(wol_ref), NT)
        dog_ref[...] = _bdot(dmix, _wo_half(wog_ref), NT)

    return _call(body, "outproj_bwd", (s // t,),
                 [_tok(t, D_MODEL), _tok(t, D_MODEL), _vec(D_MODEL)] + _wo_specs(l),
                 [_tok(t, D_MODEL), _tok(t, LRU_W), _tok(t, GDN_W), pl.BlockSpec((8, D_MODEL), lambda i: (0, 0))],
                 [_sds((s, D_MODEL), BF16), _sds((s, LRU_W)), _sds((s, GDN_W)), _sds((8, D_MODEL))])(dx1, mix, g1, wo, wo)


def _tn_matmul(a, b, name, out=None, l=0, blocked=False, row_block=0):
    s, m = a.shape
    n = b.shape[1]
    ts, bm = _tile(s, 2048), _tile(m, 1024)
    bn = next(w for w in (512, 640, 384, 256, 128) if n % w == 0)

    def body(a_ref, b_ref, *rest):
        o_ref = rest[-1]

        @pl.when(pl.program_id(2) == 0)
        def _():
            o_ref[...] = jnp.zeros_like(o_ref)

        acc = _bdot(a_ref[...], b_ref[...], TN)
        o_ref[...] += acc.reshape(o_ref.shape)

    in_specs = [pl.BlockSpec((ts, bm), lambda i, j, k: (k, i)), pl.BlockSpec((ts, bn), lambda i, j, k: (k, j))]
    grid = (m // bm, n // bn, s // ts)
    if out is None:
        return _call(body, name, grid, in_specs, pl.BlockSpec((bm, bn), lambda i, j, k: (i, j)), _sds((m, n)))(a, b)
    if blocked:
        out_spec = pl.BlockSpec((1, 1, bm, bn), lambda i, j, k: (l, j, i, 0))
    else:
        out_spec = pl.BlockSpec((1, bm, bn), lambda i, j, k: (l, i + row_block * (m // bm), j))
    return _call(body, name, grid, in_specs + [pl.BlockSpec(memory_space=pl.ANY)], out_spec,
                 _sds(out.shape), aliases={2: 0})(a, b, out)


def _final_fwd_bwd(x, target, fw):
    s = x.shape[0]
    t = _tile(s, 512)

    def body(x_ref, tg_ref, fw_ref, dx_ref, rows_ref):
        @pl.when(pl.program_id(0) == 0)
        def _():
            rows_ref[...] = jnp.zeros_like(rows_ref)

        xv = x_ref[...]
        fwv = fw_ref[...]
        r = lax.rsqrt(jnp.mean(xv * xv, axis=-1, keepdims=True) + EPS)
        err = xv * r * fwv - tg_ref[...]
        part = 0.5 * jnp.sum(jnp.mean(err * err, axis=-1, keepdims=True), axis=0, keepdims=True)
        rows_ref[1:2, :] += jnp.broadcast_to(part, (1, D_MODEL))
        dy = err * (1.0 / D_MODEL)
        rows_ref[0:1, :] += jnp.sum(dy * xv * r, axis=0, keepdims=True)
        dxn = dy * fwv
        dx_ref[...] = r * dxn - xv * (r * r * r) * jnp.mean(dxn * xv, axis=-1, keepdims=True)

    return _call(body, "final_fwd_bwd", (s // t,),
                 [_tok(t, D_MODEL), _tok(t, D_MODEL), _vec(D_MODEL)],
                 [_tok(t, D_MODEL), pl.BlockSpec((8, D_MODEL), lambda i: (0, 0))],
                 [_sds((s, D_MODEL)), _sds((8, D_MODEL))])(x, target, fw)


def _adamw(w, g, m, v):
    m = ADAM_B1 * m + (1.0 - ADAM_B1) * g
    v = ADAM_B2 * v + (1.0 - ADAM_B2) * (g * g)
    m_hat = m / (1.0 - ADAM_B1 ** ADAM_STEP)
    v_hat = v / (1.0 - ADAM_B2 ** ADAM_STEP)
    return -ADAM_LR * (m_hat / (jnp.sqrt(v_hat) + ADAM_EPS) + ADAM_WD * w), m, v


def _mod_local(c_all, wmod, bmod_cols):
    nl, _, cols = wmod.shape

    def body(c_ref, w_ref, b_ref, o_ref):
        cv = c_ref[...]
        o_ref[0] = _bdot(cv * jax.nn.sigmoid(cv), w_ref[0]) + b_ref[0]

    return _call(body, "mod_local", (nl,),
                 [_whole(c_all), pl.BlockSpec((1, D_MODEL, cols), lambda l: (l, 0, 0)),
                  pl.BlockSpec((1, 1, cols), lambda l: (l, 0, 0))],
                 pl.BlockSpec((1, N_DEV, cols), lambda l: (l, 0, 0)), _sds((nl, N_DEV, cols)))(c_all, wmod, bmod_cols)


def _wmod_update(c_all, dmod_cols, w, m, v):
    nl, _, cols = w.shape

    def body(c_ref, d_ref, w_ref, m_ref, v_ref, g_ref, dl_ref, nm_ref, nv_ref):
        cv = c_ref[...]
        g = _bdot(cv * jax.nn.sigmoid(cv), d_ref[0], TN)
        g_ref[0] = g
        dl_ref[0], nm_ref[0], nv_ref[0] = _adamw(w_ref[0], g, m_ref[0], v_ref[0])

    wspec = pl.BlockSpec((1, D_MODEL, cols), lambda l: (l, 0, 0))
    return _call(body, "wmod_update", (nl,),
                 [_whole(c_all), pl.BlockSpec((1, N_DEV, cols), lambda l: (l, 0, 0)), wspec, wspec, wspec],
                 [wspec] * 4, [_sds(w.shape)] * 4)(c_all, dmod_cols, w, m, v)


def _sum_devices(gathered):
    _, r, _ = gathered.shape

    def body(g_ref, o_ref):
        acc = g_ref[0]
        for d in range(1, N_DEV):
            acc = acc + g_ref[d]
        o_ref[...] = acc

    return _call(body, "sum_devices", (1,), [_whole(gathered)], pl.BlockSpec((r, LANES), lambda i: (0, 0)),
                 _sds((r, LANES)))(gathered)


def _adam_flat(w, g, m, v):
    r = w.shape[0]

    def body(w_ref, g_ref, m_ref, v_ref, dl_ref, nm_ref, nv_ref):
        dl_ref[...], nm_ref[...], nv_ref[...] = _adamw(w_ref[...], g_ref[...], m_ref[...], v_ref[...])

    spec = pl.BlockSpec((r, LANES), lambda i: (0, 0))
    return _call(body, "adam_small", (1,), [spec] * 4, [spec] * 3, [_sds((r, LANES))] * 3)(w, g, m, v)


def _pair_add(x, p, core):
    _, r, c = x.shape
    tr = _tile(r, 128 if c > 512 else 256)

    def body(core_ref, x_ref, p_ref, o_ref):
        o_ref[...] = (x_ref[...] + p_ref[...]).astype(BF16)

    return _call(body, "pair_add", (4, r // tr),
                 [pl.BlockSpec((1, tr, c), lambda q, i, core_ref: (2 * q + core_ref[0], i, 0)),
                  pl.BlockSpec((1, tr, c), lambda q, i, core_ref: (q, i, 0))],
                 pl.BlockSpec((1, tr, c), lambda q, i, core_ref: (q, i, 0)), _sds((4, r, c), BF16),
                 prefetch=1)(core, x, p)


def _reduce_adam(x, p, q, place, w, m, v, l, outs):
    _, r, c = x.shape
    tr = _tile(r, 128 if c > 512 else 256)

    def body(place_ref, x_ref, p_ref, q_ref, w_ref, m_ref, v_ref, *rest):
        g_ref, dl_ref, nm_ref, nv_ref = rest[-4:]
        g = (((x_ref[0] + p_ref[0]) + q_ref[0].astype(F32)) + q_ref[1].astype(F32)) + q_ref[2].astype(F32)
        g_ref[0] = g
        dl_ref[0], nm_ref[0], nv_ref[0] = _adamw(w_ref[0], g, m_ref[0], v_ref[0])

    flat = pl.BlockSpec((1, tr, c), lambda i, place_ref: (l, i, 0))
    through = pl.BlockSpec(memory_space=pl.ANY)
    return _call(body, "reduce_adam", (r // tr,),
                 [pl.BlockSpec((1, tr, c), lambda i, place_ref: (place_ref[0], i, 0)),
                  pl.BlockSpec((1, tr, c), lambda i, place_ref: (place_ref[1], i, 0)),
                  pl.BlockSpec((3, tr, c), lambda i, place_ref: (0, i, 0)), flat, flat, flat] + [through] * 4,
                 [flat] * 4, [_sds(w.shape)] * 4, prefetch=1, aliases={7 + k: k for k in range(4)})(
                     place, x, p, q, w, m, v, *outs)


def _place():
    return lax.axis_index("x"), lax.axis_index("y"), lax.axis_index("c")


def _all_gather(xs, name, space):
    n = len(xs)

    def body(*refs):
        x_refs, o_refs = refs[:n], refs[n:2 * n]
        send_sems, recv_sems, local_sems = refs[2 * n:]
        x, y, c = _place()
        me, sibling = (x, y, c), (x, y, 1 - c)
        chips = [(1 - x, y), (x, 1 - y), (1 - x, 1 - y)]

        def blk(a, p):
            return o_refs[a].at[4 * p[0] + 2 * p[1] + p[2]]

        def copy(a, k, block, to, src=None):
            return pltpu.make_async_remote_copy(
                src_ref=blk(a, block) if src is None else src, dst_ref=blk(a, block),
                send_sem=send_sems.at[a, k], recv_sem=recv_sems.at[a, k], device_id=to, device_id_type=MESH)

        mine = [pltpu.make_async_copy(x_refs[a], blk(a, me), local_sems.at[a]) for a in range(n)]
        for cp in mine:
            cp.start()
        first = []
        for a in range(n):
            first.append(copy(a, 0, me, sibling, src=x_refs[a]))
            first += [copy(a, 1 + j, me, (*chip, c), src=x_refs[a]) for j, chip in enumerate(chips)]
        for cp in first:
            cp.start()
        passed = []
        for j, chip in enumerate(chips):
            for a in range(n):
                copy(a, 1 + j, (*chip, c), me).wait_recv()
                cp = copy(a, 4 + j, (*chip, c), sibling)
                cp.start()
                passed.append(cp)
        for a in range(n):
            copy(a, 0, sibling, me).wait_recv()
        for j, chip in enumerate(chips):
            for a in range(n):
                copy(a, 4 + j, (*chip, 1 - c), me).wait_recv()
        for cp in first + passed:
            cp.wait_send()
        for cp in mine:
            cp.wait()

    spec = pl.BlockSpec(memory_space=space)
    return pl.pallas_call(
        body, name=name, out_shape=[_sds((N_DEV,) + a.shape, a.dtype) for a in xs],
        in_specs=[spec] * n, out_specs=[spec] * n,
        scratch_shapes=[pltpu.SemaphoreType.DMA((n, 7)), pltpu.SemaphoreType.DMA((n, 7)),
                        pltpu.SemaphoreType.DMA((n,))])(*xs)


_HBM_SPEC = pl.BlockSpec(memory_space=pltpu.HBM)
_SEM_SPEC = pl.BlockSpec(memory_space=pltpu.SEMAPHORE)
_EFFECT = pltpu.SideEffectType.DATAFLOW_SIDE_EFFECTING


def _descriptors(plan, src_refs, land_refs, send_sems, recv_sems):
    return [pltpu.make_async_remote_copy(src_ref=s, dst_ref=d, send_sem=send_sems.at[k], recv_sem=recv_sems.at[k],
                                         device_id=dev, device_id_type=MESH)
            for k, (s, d, dev) in enumerate(plan(src_refs, land_refs))]


def _split_start(name, plan, n, srcs, lands, after):
    ns, nb = len(srcs), len(srcs) + len(lands)

    def body(*refs):
        for cp in _descriptors(plan, refs[:ns], refs[ns:nb], refs[nb + 1], refs[nb + 2]):
            cp.start()
        refs[-1][...] = jnp.zeros_like(refs[-1])

    bufs = [pltpu.with_memory_space_constraint(a, pltpu.HBM) for a in list(srcs) + list(lands)]
    outs = pl.pallas_call(
        body, name=name,
        out_shape=(pltpu.SemaphoreType.DMA((n,)), pltpu.SemaphoreType.DMA((n,)))
        + tuple(pltpu.HBM(a.shape, a.dtype) for a in bufs) + (_sds((8, LANES)),),
        in_specs=[_HBM_SPEC] * nb + [pl.BlockSpec(memory_space=pl.ANY)],
        out_specs=(_SEM_SPEC, _SEM_SPEC) + (_HBM_SPEC,) * nb + (pl.BlockSpec(memory_space=pltpu.VMEM),),
        input_output_aliases={i: 2 + i for i in range(nb)},
        compiler_params=pltpu.CompilerParams(has_side_effects=_EFFECT))(*bufs, after)
    return dict(send=outs[0], recv=outs[1], srcs=list(outs[2:2 + ns]), lands=list(outs[2 + ns:2 + nb]), token=outs[-1])


def _split_wait(name, plan, flight, which, after):
    srcs, lands = flight["srcs"], flight["lands"]
    ns, nb = len(srcs), len(srcs) + len(lands)

    def body(*refs):
        copies = _descriptors(plan, refs[:ns], refs[ns:nb], refs[nb], refs[nb + 1])
        for k in which:
            copies[k].wait_send()
            copies[k].wait_recv()

    outs = pl.pallas_call(
        body, name=name, out_shape=tuple(pltpu.HBM(a.shape, a.dtype) for a in srcs + lands),
        in_specs=[_HBM_SPEC] * nb + [_SEM_SPEC, _SEM_SPEC, pl.BlockSpec(memory_space=pl.ANY)],
        out_specs=(_HBM_SPEC,) * nb, input_output_aliases={i: i for i in range(nb)},
        compiler_params=pltpu.CompilerParams(has_side_effects=_EFFECT))(*srcs, *lands, flight["send"], flight["recv"],
                                                                       after)
    return dict(flight, srcs=list(outs[:ns]), lands=list(outs[ns:nb]))


GATHER_PEERS = N_DEV - 1


def _gather_plan(nl, narr):
    def plan(src_refs, land_refs):
        x, y, c = _place()
        me = 4 * x + 2 * y + c
        out = []
        for l in range(nl):
            for a in range(narr):
                for r in range(1, N_DEV):
                    peer = (1 - x if r & 4 else x, 1 - y if r & 2 else y, 1 - c if r & 1 else c)
                    out.append((src_refs[a].at[l], land_refs[a].at[me, l], peer))
        return out

    return plan


def _pair_plan(narr):
    def plan(src_refs, land_refs):
        x, y, c = _place()
        return [(src_refs[a].at[2 * q + (1 - c)], land_refs[a].at[q], (x, y, 1 - c))
                for a in range(narr) for q in range(4)]

    return plan


def _chip_plan(narr):
    def plan(src_refs, land_refs):
        x, y, c = _place()
        chips = [(1 - x, y), (x, 1 - y), (1 - x, 1 - y)]
        return [(src_refs[a].at[2 * chip[0] + chip[1]], land_refs[a].at[r], (*chip, c))
                for a in range(narr) for r, chip in enumerate(chips)]

    return plan


class _GradReducer:
    def __init__(self, names, w, mom, var, place, core):
        self.names, self.w, self.mom, self.var, self.place, self.core = names, w, mom, var, place, core
        self.outs = {k: [lax.empty(w[k].shape, F32) for _ in range(4)] for k in names}
        self.n = len(names)

    def start(self, l, grads, after):
        self.l, self.xs = l, [grads[k] for k in self.names]
        lands = [lax.empty((4,) + a.shape[1:], F32) for a in self.xs]
        self.pair = _split_start(f"grad_pair_start{l}", _pair_plan(self.n), 4 * self.n, self.xs, lands, after)
        return self.pair["token"][0, 0]

    def middle(self, after):
        self.pair = _split_wait(f"grad_pair_wait{self.l}", _pair_plan(self.n), self.pair, range(4 * self.n), after)
        self.xs, self.ps = self.pair["srcs"], self.pair["lands"]
        ys = [_pair_add(x, p, self.core) for x, p in zip(self.xs, self.ps)]
        lands = [lax.empty((3,) + a.shape[1:], BF16) for a in ys]
        self.chip = _split_start(f"grad_chip_start{self.l}", _chip_plan(self.n), 3 * self.n, ys, lands, ys[-1])
        return self.chip["token"][0, 0]

    def finish(self, after):
        chip = _split_wait(f"grad_chip_wait{self.l}", _chip_plan(self.n), self.chip, range(3 * self.n), after)
        for k, x, p, q in zip(self.names, self.xs, self.ps, chip["lands"]):
            self.outs[k] = _reduce_adam(x, p, q, self.place, self.w[k], self.mom[k], self.var[k], self.l, self.outs[k])


def _size(shape):
    size = 1
    for d in shape:
        size *= d
    return size


def _slab_rows(shape):
    return -(-_size(shape) // (8 * LANES)) * 8


def _pack(arrs):
    parts = []
    for a in arrs:
        flat = a.reshape(-1).astype(F32)
        parts.append(jnp.pad(flat, (0, _slab_rows(a.shape) * LANES - flat.shape[0])).reshape(-1, LANES))
    return jnp.concatenate(parts, axis=0)


def _unpack(slab, shapes):
    out, off = [], 0
    for shp in shapes:
        rows = _slab_rows(shp)
        out.append(slab[off:off + rows].reshape(-1)[:_size(shp)].reshape(shp))
        off += rows
    return out


def _dense_blocks(w):
    eye = jnp.eye(LRU_BLOCKS, dtype=w.dtype)
    return (eye[:, None, :, None] * w[:, :, None, :]).reshape(LRU_W, LRU_W)


def _diag_blocks(dense):
    return jnp.stack([dense[g * LRU_BLOCK:(g + 1) * LRU_BLOCK, g * LRU_BLOCK:(g + 1) * LRU_BLOCK]
                      for g in range(LRU_BLOCKS)])


def _alpha_lanes(v):
    return jnp.zeros((1, BA_PAD), F32).at[0, HEADS:2 * HEADS].set(v)


def _local_step(x, target, mod, p, fetch, reducer=None):
    nl = mod.shape[0]
    row = lambda v: v.reshape(1, -1)
    masks = _gdn_masks()
    saved = []
    xc = x
    for l in range(nl):
        win, wba, lin, wo, wup, wdn, lw, keep = fetch(l, xc)
        mv = [row(mod[l, k * D_MODEL:(k + 1) * D_MODEL]) for k in range(N_MOD)]
        sh1, sc1, g1, sh2, sc2, g2 = mv
        nw1, nw2 = row(p["norm_mix_w"][l]), row(p["norm_mlp_w"][l])
        wa, wx = _dense_blocks(p["lru_gate_a_w"][l]).astype(BF16), _dense_blocks(p["lru_gate_x_w"][l]).astype(BF16)
        lru_args = (p["lru_conv_w"][l], row(p["lru_conv_b"][l]), wa, wx, row(p["lru_gate_a_b"][l]),
                    row(p["lru_gate_x_b"][l]), row(p["lru_lambda"][l]), row(p["lru_norm_w"][l]))
        gdn_args = (p["gdn_conv_w"][l], _alpha_lanes(p["gdn_a_log"][l]), _alpha_lanes(p["gdn_dt_bias"][l]), masks)
        gnw = row(p["gdn_norm_w"][l])
        proj, ba = _inproj_fwd(xc, nw1, sc1, sh1, win, wba, lin)
        ol, hs = _lru_fwd(proj, *lru_args)
        *prep, tinv = _gdn_prep_fwd(proj, ba, *gdn_args)
        og, st = _gdn_scan_fwd(prep, proj, gnw)
        x1, mix, ff, x2 = _out_mlp_fwd(ol, og, xc, wo, g1, nw2, sc2, sh2, g2, wup, wdn, lw)
        saved.append(dict(x=xc, mv=mv, nw1=nw1, nw2=nw2, lru_args=lru_args, gdn_args=gdn_args, gnw=gnw, proj=proj,
                          ba=ba, ol=ol, hs=hs, prep=prep, tinv=tinv, og=og, st=st, x1=x1, mix=mix, ff=ff,
                          win=win, wba=wba, lin=lin, lw=lw, kept=(wo, wup, wdn) if keep else None))
        xc = x2

    dx, frows = _final_fwd_bwd(xc, target, row(p["final_norm_w"]))
    loss_part = frows[1, 0]
    small = {k: [None] * nl for k in ("norm_mix_w", "norm_mlp_w", "lru_conv_w", "lru_conv_b", "lru_gate_a_w",
                                      "lru_gate_a_b", "lru_gate_x_w", "lru_gate_x_b", "lru_lambda", "lru_norm_w",
                                      "gdn_conv_w", "gdn_a_log", "gdn_dt_bias", "gdn_norm_w")}
    fc = D_FF // N_DEV
    big = [None] * nl
    dmod = [None] * nl
    busy = False
    for l in reversed(range(nl)):
        sv = saved[l]
        sh1, sc1, g1, sh2, sc2, g2 = sv["mv"]
        if busy:
            g2 = g2 + started
        bwo, bwup, bwdn = sv["kept"] or (wo, wup, wdn)
        act, dup, h2b, dffb, dx1, rows2 = _mlp_bwd(dx, sv["x1"], sv["ff"], sv["nw2"], sc2, sh2, g2, bwup, bwdn,
                                                   sv["lw"])
        if busy:
            g1 = g1 + reducer.middle(dx1)
        g_up = _tn_matmul(h2b, dup, "grad_w_up", out=lax.empty((1, N_DEV, D_MODEL, fc), F32), blocked=True)[0]
        g_down = _tn_matmul(act, dffb, "grad_w_down", out=lax.empty((1, D_FF, D_MODEL), F32))
        dmix, dol, dog, rows1 = _outproj_bwd(dx1, sv["mix"], g1, bwo, sv["lw"])
        g_out = _tn_matmul(sv["ol"], dmix, "grad_w_out_lru", out=lax.empty((1, D_MODEL, D_MODEL), F32))
        g_out = _tn_matmul(sv["og"], dmix, "grad_w_out_gdn", out=g_out, row_block=1)
        dpl, dwa, dwx, lrows = _lru_bwd(dol, sv["proj"], sv["hs"], *sv["lru_args"])
        *cts, dpz, gnrow = _gdn_scan_bwd(dog, sv["prep"], sv["st"], sv["proj"], sv["gnw"])
        dpq, dba, dcw, dpar = _gdn_prep_bwd(cts, sv["tinv"], sv["proj"], sv["ba"], *sv["gdn_args"])
        dx, hb, rows0 = _inproj_bwd(dpl, dpq, dpz, dba, sv["x"], dx1, sv["nw1"], sc1, sh1, sv["win"], sv["wba"],
                                    sv["lin"])
        if busy:
            reducer.finish(dx)
        dproj = jnp.concatenate([dpl, dpq, dpz, dba], axis=1)
        g_in = jnp.transpose(_tn_matmul(hb, dproj, "grad_w_in")[:, :IN_COLS].reshape(
            D_MODEL, N_DEV, IN_COLS // N_DEV), (1, 0, 2))
        big[l] = dict(w_in=g_in, w_out=g_out.reshape(N_DEV, D_MODEL // N_DEV, D_MODEL), w_up=g_up,
                      w_down=g_down.reshape(N_DEV, fc, D_MODEL))
        if reducer is not None:
            started, busy = reducer.start(l, big[l], g_in), True
        dmod[l] = jnp.concatenate([rows0[0], rows0[1], rows1[0], rows2[0], rows2[1], rows2[2]])
        small["norm_mix_w"][l], small["norm_mlp_w"][l] = rows0[2], rows2[3]
        small["lru_conv_w"][l], small["lru_conv_b"][l] = lrows[8:8 + CONV_K], lrows[0]
        small["lru_gate_a_w"][l], small["lru_gate_x_w"][l] = _diag_blocks(dwa), _diag_blocks(dwx)
        small["lru_gate_a_b"][l], small["lru_gate_x_b"][l] = lrows[1], lrows[2]
        small["lru_lambda"][l], small["lru_norm_w"][l] = lrows[3], lrows[4]
        small["gdn_conv_w"][l] = dcw
        small["gdn_a_log"][l], small["gdn_dt_bias"][l] = dpar[0, HEADS:2 * HEADS], dpar[1, HEADS:2 * HEADS]
        small["gdn_norm_w"][l] = gnrow[0]
    if busy:
        reducer.middle(dx)
        reducer.finish(dx)
    small = {k: jnp.stack(v) for k, v in small.items()}
    small["final_norm_w"] = frows[0]
    return loss_part, dx, big, small, jnp.stack(dmod)


SMALL_REPLICATED = ("norm_mix_w", "norm_mlp_w", "b_mod", "lru_conv_b", "lru_gate_a_w", "lru_gate_a_b", "lru_gate_x_w",
                    "lru_gate_x_b", "lru_lambda", "lru_norm_w", "gdn_a_log", "gdn_dt_bias", "gdn_norm_w",
                    "final_norm_w")
SMALL_SHARDED = ("lru_conv_w", "gdn_conv_w")
WEIGHT_ORDER = ("norm_mix_w", "norm_mlp_w", "w_mod", "b_mod", "w_in", "lru_conv_w", "lru_conv_b", "lru_gate_a_w",
                "lru_gate_a_b", "lru_gate_x_w", "lru_gate_x_b", "lru_lambda", "lru_norm_w", "gdn_conv_w", "gdn_a_log",
                "gdn_dt_bias", "gdn_norm_w", "w_out", "w_up", "w_down", "final_norm_w")


def kernel(x, c, norm_mix_w, norm_mlp_w, w_mod, b_mod, w_in, lru_conv_w, lru_conv_b, lru_gate_a_w, lru_gate_a_b, lru_gate_x_w, lru_gate_x_b, lru_lambda, lru_norm_w, gdn_conv_w, gdn_a_log, gdn_dt_bias, gdn_norm_w, w_out, w_up, w_down, final_norm_w, loss_target, m_norm_mix_w, m_norm_mlp_w, m_w_mod, m_b_mod, m_w_in, m_lru_conv_w, m_lru_conv_b, m_lru_gate_a_w, m_lru_gate_a_b, m_lru_gate_x_w, m_lru_gate_x_b, m_lru_lambda, m_lru_norm_w, m_gdn_conv_w, m_gdn_a_log, m_gdn_dt_bias, m_gdn_norm_w, m_w_out, m_w_up, m_w_down, m_final_norm_w, v_norm_mix_w, v_norm_mlp_w, v_w_mod, v_b_mod, v_w_in, v_lru_conv_w, v_lru_conv_b, v_lru_gate_a_w, v_lru_gate_a_b, v_lru_gate_x_w, v_lru_gate_x_b, v_lru_lambda, v_lru_norm_w, v_gdn_conv_w, v_gdn_a_log, v_gdn_dt_bias, v_gdn_norm_w, v_w_out, v_w_up, v_w_down, v_final_norm_w):
    args = dict(locals())
    w = {k: args[k] for k in WEIGHT_ORDER}
    mom = {k: args["m_" + k] for k in WEIGHT_ORDER}
    var = {k: args["v_" + k] for k in WEIGHT_ORDER}
    nl = w_in.shape[0]
    px, py, pc = _place()
    me = 4 * px + 2 * py + pc
    core = jnp.reshape(pc, (1,)).astype(jnp.int32)

    shapes0 = [c.shape, lru_conv_w.shape, gdn_conv_w.shape]
    (g0,) = _all_gather([_pack([c, lru_conv_w, gdn_conv_w])], "gather_cond", pltpu.VMEM)
    per_dev = [_unpack(g0[d], shapes0) for d in range(N_DEV)]
    c_all = jnp.concatenate([pd[0] for pd in per_dev], axis=0)
    lru_conv_full = jnp.concatenate([pd[1] for pd in per_dev], axis=-1)
    gdn_conv_full = jnp.concatenate([pd[2] for pd in per_dev], axis=-1)

    cols = w_mod.shape[2]
    bmod_cols = lax.dynamic_slice_in_dim(b_mod, me * cols, cols, axis=1).reshape(nl, 1, cols)
    mod_cols = _mod_local(c_all, w_mod, bmod_cols)
    (g1,) = _all_gather([mod_cols.reshape(nl * N_DEV, cols)], "gather_mod", pltpu.VMEM)
    g1 = g1.reshape(N_DEV, nl, N_DEV, cols)
    mod = jnp.transpose(lax.dynamic_index_in_dim(g1, me, axis=2, keepdims=False), (1, 0, 2)).reshape(nl, N_DEV * cols)

    shards = [a.astype(BF16) for a in (w_in, w_out, w_up, w_down)]
    first = _all_gather([a[:1] for a in shards], "gather_weights_first", pl.ANY)
    plan = _gather_plan(nl - 1, len(shards))
    per_layer = len(shards) * GATHER_PEERS
    flight = []
    if nl > 1:
        rest = [a[1:] for a in shards]
        lands = [lax.dynamic_update_slice_in_dim(lax.empty((N_DEV,) + a.shape, BF16), a[None], me, axis=0)
                 for a in rest]
        flight.append(_split_start("gather_weights_start", plan, (nl - 1) * per_layer, rest, lands, first[0]))
        mod = mod + flight[0]["token"][0, 0]

    def fetch(l, after):
        if l == 0:
            gin, gout, gup, gdn = first
        else:
            flight[0] = _split_wait(f"gather_weights_wait{l}", plan, flight[0],
                                    range((l - 1) * per_layer, l * per_layer), after)
            gin, gout, gup, gdn = flight[0]["lands"]
        lw = max(l - 1, 0)
        win = jnp.transpose(gin[:, lw], (1, 0, 2)).reshape(1, D_MODEL, IN_COLS)
        wba = jnp.pad(win[:, :, IN_MAIN:], ((0, 0), (0, 0), (0, BA_PAD - (IN_COLS - IN_MAIN))))
        return win, wba, 0, gout, gup, gdn, lw, l == 0

    p = dict(w)
    p["lru_conv_w"], p["gdn_conv_w"] = lru_conv_full, gdn_conv_full

    order = ("w_in", "w_out", "w_up", "w_down")
    place = jnp.stack([me, 2 * px + py]).astype(jnp.int32)
    reducer = _GradReducer(order, w, mom, var, place, core)
    loss_part, grad_x, _, small, dmod = _local_step(x[0], loss_target[0], mod, p, fetch, reducer)
    loss = lax.psum(loss_part, MESH_AXES)

    small_names = sorted(small)
    slab = _pack([dmod] + [small[k] for k in small_names])
    (gs,) = _all_gather([slab], "gather_small_grads", pltpu.VMEM)
    dmod_all = gs[:, :_slab_rows(dmod.shape)].reshape(N_DEV, nl, N_MOD * D_MODEL)
    summed = _unpack(_sum_devices(gs), [dmod.shape] + [small[k].shape for k in small_names])
    grads = dict(zip(small_names, summed[1:]))
    grads["b_mod"] = summed[0]
    for k, width in (("lru_conv_w", LRU_W // N_DEV), ("gdn_conv_w", 3 * GDN_W // N_DEV)):
        grads[k] = lax.dynamic_slice_in_dim(grads[k], me * width, width, axis=2)
    names = SMALL_REPLICATED + SMALL_SHARDED
    shapes = [w[k].shape for k in names]
    dl, nm, nv = _adam_flat(_pack([w[k] for k in names]), _pack([grads[k] for k in names]),
                            _pack([mom[k] for k in names]), _pack([var[k] for k in names]))
    delta = dict(zip(names, _unpack(dl, shapes)))
    new_m = dict(zip(names, _unpack(nm, shapes)))
    new_v = dict(zip(names, _unpack(nv, shapes)))

    dmod_cols = jnp.transpose(lax.dynamic_slice_in_dim(dmod_all, me * cols, cols, axis=2), (1, 0, 2))
    grads["w_mod"], delta["w_mod"], new_m["w_mod"], new_v["w_mod"] = _wmod_update(
        c_all, dmod_cols, w_mod, m_w_mod, v_w_mod)

    for k in order:
        grads[k], delta[k], new_m[k], new_v[k] = reducer.outs[k]

    return (loss, grad_x[None], *[grads[k] for k in WEIGHT_ORDER], *[delta[k] for k in WEIGHT_ORDER],
            *[new_m[k] for k in WEIGHT_ORDER], *[new_v[k] for k in WEIGHT_ORDER])
```

```python
import functools

import jax
import jax.numpy as jnp
from jax import lax
from jax.experimental import pallas as pl
from jax.experimental.pallas import tpu as pltpu

F32 = jnp.float32
BF16 = jnp.bfloat16

D_MODEL = 1024
LRU_W = 512
LRU_BLOCKS = 8
LRU_BLOCK = 64
LRU_C = 8.0
GDN_W = 512
HEADS = 4
HEAD_DIM = 128
CHUNK = 64
STACK = HEADS * CHUNK
CONV_K = 4
D_FF = 4096
N_MOD = 6
IN_COLS = 3080
IN_MAIN = 3072
BA_PAD = 128
EPS = 1e-6
N_DEV = 8
HALO = 8
MLP_BLOCKS = 2
PREP_CHUNKS = 2
LANES = 128
ADAM_LR, ADAM_B1, ADAM_B2, ADAM_EPS, ADAM_WD, ADAM_STEP = 0.001, 0.9, 0.999, 1e-08, 0.01, 10
MESH_AXES = ("x", "y", "c")
MESH = pl.DeviceIdType.MESH

NN = (((1,), (0,)), ((), ()))
NT = (((1,), (1,)), ((), ()))
TN = (((0,), (0,)), ((), ()))


def _bdot(a, b, dims=NN):
    return lax.dot_general(a.astype(BF16), b.astype(BF16), dims, preferred_element_type=F32)


def _sdot(a, b, dims=NN):
    ah, bh = a.astype(BF16), b.astype(BF16)
    al, bl = (a - ah.astype(F32)).astype(BF16), (b - bh.astype(F32)).astype(BF16)
    return _bdot(ah, bh, dims) + (_bdot(al, bh, dims) + _bdot(ah, bl, dims))


def _hdot(a, b, dims=NN):
    return lax.dot_general(a, b, dims, precision=lax.Precision.HIGHEST, preferred_element_type=F32)


def _sds(shape, dtype=F32):
    return jax.ShapeDtypeStruct(tuple(shape), dtype)


def _tile(n, t):
    return min(n, t)


def _call(body, name, grid, in_specs, out_specs, out_shape, scratch=(), vmem_mb=48, prefetch=0, aliases=None):
    params = pltpu.CompilerParams(dimension_semantics=("arbitrary",) * len(grid), vmem_limit_bytes=vmem_mb * 2**20)
    if prefetch:
        spec = pltpu.PrefetchScalarGridSpec(num_scalar_prefetch=prefetch, grid=grid, in_specs=in_specs,
                                            out_specs=out_specs, scratch_shapes=list(scratch))
        return pl.pallas_call(body, name=name, grid_spec=spec, out_shape=out_shape, compiler_params=params,
                              input_output_aliases=aliases or {})
    return pl.pallas_call(body, name=name, grid=grid, in_specs=in_specs, out_specs=out_specs, out_shape=out_shape,
                          scratch_shapes=list(scratch), compiler_params=params, input_output_aliases=aliases or {})


def _tok(t, n, col=0):
    return pl.BlockSpec((t, n), lambda i, *_: (i, col))


def _vec(n):
    return pl.BlockSpec((1, n), lambda *_: (0, 0))


def _whole(a):
    nd = a.ndim
    return pl.BlockSpec(a.shape, lambda *_: (0,) * nd)


def _layer(l, *dims):
    return pl.BlockSpec((1,) + dims, lambda *_: (l,) + (0,) * len(dims))


def _gelu(y):
    c0, c1 = 0.7978845608028654, 0.044715
    return 0.5 * y * (1.0 + jnp.tanh(c0 * (y + c1 * y * y * y)))


def _gelu_grad(y):
    c0, c1 = 0.7978845608028654, 0.044715
    t = jnp.tanh(c0 * (y + c1 * y * y * y))
    return 0.5 * (1.0 + t) + 0.5 * y * (1.0 - t * t) * c0 * (1.0 + 3.0 * c1 * y * y)


def _softplus(v):
    return jnp.maximum(v, 0.0) + jnp.log(1.0 + jnp.exp(-jnp.where(v > 0, v, -v)))


@functools.partial(jax.custom_vjp, nondiff_argnums=(1,))
def _roll_rows(v, s):
    s = s % v.shape[0]
    return pltpu.roll(v, s, axis=0) if s else v


def _roll_rows_fwd(v, s):
    return _roll_rows(v, s), None


def _roll_rows_bwd(s, _, g):
    return (_roll_rows(g, -s),)


_roll_rows.defvjp(_roll_rows_fwd, _roll_rows_bwd)


@jax.custom_vjp
def _drop_halo(v):
    return v[HALO:]


def _drop_halo_fwd(v):
    return v[HALO:], None


def _drop_halo_bwd(_, g):
    return (jnp.concatenate([jnp.zeros((HALO, g.shape[1]), g.dtype), g], axis=0),)


_drop_halo.defvjp(_drop_halo_fwd, _drop_halo_bwd)


@functools.partial(jax.custom_vjp, nondiff_argnums=(1, 2))
def _split(v, n, axis):
    w = v.shape[axis] // n
    return tuple(lax.slice_in_dim(v, k * w, (k + 1) * w, axis=axis) for k in range(n))


def _split_fwd(v, n, axis):
    return _split(v, n, axis), None


def _split_bwd(n, axis, _, gs):
    return (jnp.concatenate(list(gs), axis=axis),)


_split.defvjp(_split_fwd, _split_bwd)


def _conv_taps(xw):
    return [_drop_halo(_roll_rows(xw, CONV_K - 1 - k)) for k in range(CONV_K)]


def _modulated_norm(xv, nw, sc, sh):
    r = lax.rsqrt(jnp.mean(xv * xv, axis=-1, keepdims=True) + EPS)
    n = xv * r * nw
    return n * (1.0 + sc) + sh, n, r


def _modulated_norm_bwd(dh, xv, n, r, nw, sc):
    dn = dh * (1.0 + sc)
    dxn = dn * nw
    dx = r * dxn - xv * (r * r * r) * jnp.mean(dxn * xv, axis=-1, keepdims=True)
    return (dx, jnp.sum(dh, axis=0, keepdims=True), jnp.sum(dh * n, axis=0, keepdims=True),
            jnp.sum(dn * xv * r, axis=0, keepdims=True))


def _inproj_fwd(x, nw, sc, sh, win, wba, l):
    s = x.shape[0]
    t = _tile(s, 256)

    def body(x_ref, nw_ref, sc_ref, sh_ref, win_ref, wba_ref, proj_ref, ba_ref):
        h, _, _ = _modulated_norm(x_ref[...], nw_ref[...], sc_ref[...], sh_ref[...])
        hb = h.astype(BF16)
        proj_ref[...] = _bdot(hb, win_ref[0])
        ba_ref[...] = _bdot(hb, wba_ref[0])

    return _call(body, "inproj_fwd", (s // t,),
                 [_tok(t, D_MODEL), _vec(D_MODEL), _vec(D_MODEL), _vec(D_MODEL), _layer(l, D_MODEL, IN_MAIN),
                  _layer(l, D_MODEL, BA_PAD)],
                 [_tok(t, IN_MAIN), _tok(t, BA_PAD)],
                 [_sds((s, IN_MAIN)), _sds((s, BA_PAD))])(x, nw, sc, sh, win, wba)


def _inproj_bwd(dpl, dpq, dpz, dba, x, dx1, nw, sc, sh, win, wba, l):
    s = x.shape[0]
    t = _tile(s, 256)

    def body(dpl_ref, dpq_ref, dpz_ref, dba_ref, x_ref, dx1_ref, nw_ref, sc_ref, sh_ref, win_ref, wba_ref,
             dx_ref, hb_ref, acc_ref):
        @pl.when(pl.program_id(0) == 0)
        def _():
            acc_ref[...] = jnp.zeros_like(acc_ref)

        dh = (_bdot(dpl_ref[...], win_ref[0, :, 0:2 * LRU_W], NT)
              + _bdot(dpq_ref[...], win_ref[0, :, 2 * LRU_W:2 * LRU_W + 3 * GDN_W], NT)
              + _bdot(dpz_ref[...], win_ref[0, :, 2 * LRU_W + 3 * GDN_W:IN_MAIN], NT)
              + _bdot(dba_ref[...], wba_ref[0], NT))
        xv = x_ref[...]
        h, n, r = _modulated_norm(xv, nw_ref[...], sc_ref[...], sh_ref[...])
        hb_ref[...] = h.astype(BF16)
        dx, dsh, dsc, dnw = _modulated_norm_bwd(dh, xv, n, r, nw_ref[...], sc_ref[...])
        dx_ref[...] = dx1_ref[...] + dx
        acc_ref[0:1, :] += dsh
        acc_ref[1:2, :] += dsc
        acc_ref[2:3, :] += dnw

    return _call(body, "inproj_bwd", (s // t,),
                 [_tok(t, 2 * LRU_W), _tok(t, 3 * GDN_W), _tok(t, GDN_W), _tok(t, BA_PAD), _tok(t, D_MODEL),
                  _tok(t, D_MODEL), _vec(D_MODEL), _vec(D_MODEL), _vec(D_MODEL), _layer(l, D_MODEL, IN_MAIN),
                  _layer(l, D_MODEL, BA_PAD)],
                 [_tok(t, D_MODEL), _tok(t, D_MODEL), pl.BlockSpec((8, D_MODEL), lambda i: (0, 0))],
                 [_sds((s, D_MODEL)), _sds((s, D_MODEL), BF16), _sds((8, D_MODEL))])(
                     dpl, dpq, dpz, dba, x, dx1, nw, sc, sh, win, wba)


def _lru_gates(xw, cw_rows, cb, wa, wx, gab, gxb, lam):
    taps = _conv_taps(xw)
    xr = cb + cw_rows[0] * taps[0] + cw_rows[1] * taps[1] + cw_rows[2] * taps[2] + cw_rows[3] * taps[3]
    xb = xr.astype(BF16)
    r = jax.nn.sigmoid(_bdot(xb, wa) + gab)
    i = jax.nn.sigmoid(_bdot(xb, wx) + gxb)
    z = jnp.exp(-jnp.where(lam > 0, lam, -lam))
    w1 = 1.0 + z
    log1p_z = jnp.where(w1 == 1.0, z, jnp.log(w1) * z / (w1 - 1.0))
    ls = jnp.minimum(lam, 0.0) - log1p_z
    la = LRU_C * r * ls
    a = jnp.exp(la)
    x2 = 2.0 * la
    u = jnp.exp(x2)
    mm_raw = jnp.where(u == 1.0, -x2,
                       jnp.where(x2 < -30.0, 1.0, (1.0 - u) * x2 / jnp.log(jnp.maximum(u, 1e-30))))
    mult = jnp.sqrt(jnp.maximum(mm_raw, 1e-12))
    return dict(taps=taps, xr=xr, r=r, i=i, ls=ls, a=a, mm_raw=mm_raw, mult=mult)


def _lru_specs(s, t, tile_of):
    nh = t // HALO
    xl = pl.BlockSpec((t, LRU_W), lambda i: (tile_of(i), 0))
    yl = pl.BlockSpec((t, LRU_W), lambda i: (tile_of(i), 1))
    hx = pl.BlockSpec((HALO, LRU_W), lambda i: (jnp.maximum(tile_of(i) * nh - 1, 0), 0))
    return xl, yl, hx


def _lru_fwd(proj, cw, cb, wa, wx, gab, gxb, lam, lnw):
    s = proj.shape[0]
    t = _tile(s, 256)
    xl, yl, hx = _lru_specs(s, t, lambda i: i)

    def body(xl_ref, yl_ref, hx_ref, cw_ref, cb_ref, wa_ref, wx_ref, gab_ref, gxb_ref, lam_ref, lnw_ref,
             out_ref, h_ref, a_s, b_s, hc):
        i = pl.program_id(0)

        @pl.when(i == 0)
        def _():
            hc[...] = jnp.zeros_like(hc)

        halo = jnp.where(i > 0, hx_ref[...], 0.0)
        xw = jnp.concatenate([halo, xl_ref[...]], axis=0)
        g = _lru_gates(xw, [cw_ref[k:k + 1, :] for k in range(CONV_K)], cb_ref[...], wa_ref[...], wx_ref[...],
                       gab_ref[...], gxb_ref[...], lam_ref[...])
        a_s[...] = g["a"]
        b_s[...] = g["mult"] * (g["i"] * g["xr"])

        def step(k, h):
            h = a_s[pl.ds(k, 1), :] * h + b_s[pl.ds(k, 1), :]
            h_ref[pl.ds(k, 1), :] = h
            return h

        hc[...] = lax.fori_loop(0, t, step, hc[...], unroll=8)
        m = h_ref[...] * _gelu(yl_ref[...])
        out_ref[...] = m * lax.rsqrt(jnp.mean(m * m, axis=-1, keepdims=True) + EPS) * lnw_ref[...]

    return _call(body, "lru_fwd", (s // t,),
                 [xl, yl, hx, _whole(cw), _vec(LRU_W), _whole(wa), _whole(wx)] + [_vec(LRU_W)] * 4,
                 [_tok(t, LRU_W), _tok(t, LRU_W)],
                 [_sds((s, LRU_W)), _sds((s, LRU_W))],
                 scratch=[pltpu.VMEM((t, LRU_W), F32), pltpu.VMEM((t, LRU_W), F32), pltpu.VMEM((1, LRU_W), F32)])(
                     proj, proj, proj, cw, cb, wa, wx, gab, gxb, lam, lnw)


def _lru_bwd(dout, proj, hs, cw, cb, wa, wx, gab, gxb, lam, lnw):
    s = proj.shape[0]
    t = _tile(s, 256)
    nt = s // t
    rev = lambda i: nt - 1 - i
    xl, yl, hx = _lru_specs(s, t, rev)
    nh = t // HALO
    tk = pl.BlockSpec((t, LRU_W), lambda i: (rev(i), 0))
    hh = pl.BlockSpec((HALO, LRU_W), lambda i: (jnp.maximum(rev(i) * nh - 1, 0), 0))

    def body(do_ref, xl_ref, yl_ref, hx_ref, h_ref, hh_ref, cw_ref, cb_ref, wa_ref, wx_ref, gab_ref, gxb_ref,
             lam_ref, lnw_ref, dp_ref, dwa_ref, dwx_ref, rows_ref, dh_s, dhd_s, carry, dxr_next):
        i = pl.program_id(0)
        first_tile = rev(i) == 0

        @pl.when(i == 0)
        def _():
            carry[...] = jnp.zeros_like(carry)
            dxr_next[...] = jnp.zeros_like(dxr_next)
            dwa_ref[...] = jnp.zeros_like(dwa_ref)
            dwx_ref[...] = jnp.zeros_like(dwx_ref)
            rows_ref[...] = jnp.zeros_like(rows_ref)

        halo = jnp.where(first_tile, 0.0, hx_ref[...])
        xw = jnp.concatenate([halo, xl_ref[...]], axis=0)
        cw_rows = [cw_ref[k:k + 1, :] for k in range(CONV_K)]
        lam_v = lam_ref[...]
        g = _lru_gates(xw, cw_rows, cb_ref[...], wa_ref[...], wx_ref[...], gab_ref[...], gxb_ref[...], lam_v)
        a, r, gi, xr, mult = g["a"], g["r"], g["i"], g["xr"], g["mult"]
        hv = h_ref[...]
        yv = yl_ref[...]
        gl = _gelu(yv)
        m = hv * gl
        rn = lax.rsqrt(jnp.mean(m * m, axis=-1, keepdims=True) + EPS)
        dov = do_ref[...]
        dmn = dov * lnw_ref[...]
        rows_ref[4:5, :] += jnp.sum(dov * m * rn, axis=0, keepdims=True)
        dm = rn * dmn - m * (rn * rn * rn) * jnp.mean(dmn * m, axis=-1, keepdims=True)
        dhd_s[...] = dm * gl
        dy = dm * hv * _gelu_grad(yv)
        dh_s[...] = a

        def step(k, c):
            row = t - 1 - k
            d = dhd_s[pl.ds(row, 1), :] + c
            c = dh_s[pl.ds(row, 1), :] * d
            dh_s[pl.ds(row, 1), :] = d
            return c

        carry[...] = lax.fori_loop(0, t, step, carry[...], unroll=8)
        dH = dh_s[...]
        hprev_halo = jnp.where(first_tile, 0.0, hh_ref[...])
        hprev = _drop_halo(_roll_rows(jnp.concatenate([hprev_halo, hv], axis=0), 1))
        da = dH * hprev
        dmult = dH * gi * xr
        di = dH * mult * xr
        dxr = dH * mult * gi
        dla = jnp.where(g["mm_raw"] > 1e-12, dmult * (0.5 / mult) * (-2.0 * a * a), 0.0) + da * a
        dr = dla * (LRU_C * g["ls"])
        sig_neg = jax.nn.sigmoid(-lam_v)
        rows_ref[3:4, :] += jnp.sum(dla * (LRU_C * r), axis=0, keepdims=True) * sig_neg
        drp = dr * r * (1.0 - r)
        dip = di * gi * (1.0 - gi)
        rows_ref[1:2, :] += jnp.sum(drp, axis=0, keepdims=True)
        rows_ref[2:3, :] += jnp.sum(dip, axis=0, keepdims=True)
        xb = xr.astype(BF16)
        drb = drp.astype(BF16)
        dib = dip.astype(BF16)
        dwa_ref[...] += _bdot(xb, drb, TN)
        dwx_ref[...] += _bdot(xb, dib, TN)
        dxr = dxr + _bdot(drb, wa_ref[...], NT) + _bdot(dib, wx_ref[...], NT)
        rows_ref[0:1, :] += jnp.sum(dxr, axis=0, keepdims=True)
        ext = jnp.concatenate([dxr, dxr_next[...]], axis=0)
        dx = cw_rows[CONV_K - 1] * dxr
        for k in range(CONV_K - 1):
            dx = dx + cw_rows[k] * _roll_rows(ext, -(CONV_K - 1 - k))[0:t]
        for k in range(CONV_K):
            rows_ref[8 + k:9 + k, :] += jnp.sum(dxr * g["taps"][k], axis=0, keepdims=True)
        dxr_next[...] = dxr[0:HALO]
        dp_ref[...] = jnp.concatenate([dx, dy], axis=1).astype(BF16)

    acc = lambda shape: pl.BlockSpec(shape, lambda i: (0, 0))
    return _call(body, "lru_bwd", (nt,),
                 [tk, xl, yl, hx, tk, hh, _whole(cw), _vec(LRU_W), _whole(wa), _whole(wx)] + [_vec(LRU_W)] * 4,
                 [pl.BlockSpec((t, 2 * LRU_W), lambda i: (rev(i), 0)), acc((LRU_W, LRU_W)), acc((LRU_W, LRU_W)),
                  acc((16, LRU_W))],
                 [_sds((s, 2 * LRU_W), BF16), _sds((LRU_W, LRU_W)), _sds((LRU_W, LRU_W)), _sds((16, LRU_W))],
                 scratch=[pltpu.VMEM((t, LRU_W), F32), pltpu.VMEM((t, LRU_W), F32), pltpu.VMEM((1, LRU_W), F32),
                          pltpu.VMEM((HALO, LRU_W), F32)])(
                     dout, proj, proj, proj, hs, hs, cw, cb, wa, wx, gab, gxb, lam, lnw)


def _gdn_masks():
    row = lax.broadcasted_iota(jnp.int32, (STACK, STACK), 0)
    col = lax.broadcasted_iota(jnp.int32, (STACK, STACK), 1)
    same = (row // CHUNK) == (col // CHUNK)
    return jnp.stack([(same & (col <= row)).astype(F32), (same & (col < row)).astype(F32), (row == col).astype(F32)])


def _conv_silu(xw, rows):
    taps = _conv_taps(xw)
    y = rows[0] * taps[0] + rows[1] * taps[1] + rows[2] * taps[2] + rows[3] * taps[3]
    return y * jax.nn.sigmoid(y)


def _split3(v):
    hi = v.astype(BF16)
    r1 = v - hi.astype(F32)
    mid = r1.astype(BF16)
    return hi, mid, (r1 - mid.astype(F32)).astype(BF16)


def _mask_dot_raw(mask, v, dims):
    parts = _split3(v)
    d = lambda p: lax.dot_general(mask, p, dims, preferred_element_type=F32)
    return d(parts[0]) + (d(parts[1]) + d(parts[2]))


@jax.custom_vjp
def _mask_dot(mask, v):
    return _mask_dot_raw(mask, v, NN)


def _mask_dot_fwd(mask, v):
    return _mask_dot_raw(mask, v, NN), mask


def _mask_dot_bwd(mask, ct):
    return jnp.zeros_like(mask), _mask_dot_raw(mask, ct, TN)


_mask_dot.defvjp(_mask_dot_fwd, _mask_dot_bwd)


def _unit_lower_inverse(n, eye):
    tinv = eye + n
    p = n
    for _ in range(5):
        p = _bdot(p, p)
        tinv = tinv + _bdot(tinv, p)
    return tinv.astype(BF16)


@jax.custom_vjp
def _unit_lower_solve(n, rhs, tinv):
    x0 = _bdot(tinv, rhs)
    return x0 + _bdot(tinv, rhs - x0 + _sdot(n, x0))


def _unit_lower_solve_fwd(n, rhs, tinv):
    x = _unit_lower_solve(n, rhs, tinv)
    return x, (n, tinv, x)


def _unit_lower_solve_bwd(res, ct):
    n, tinv, x = res
    y0 = _bdot(tinv, ct, TN)
    y = y0 + _bdot(tinv, ct - y0 + _sdot(n, y0, TN), TN)
    return _bdot(y, x, NT), y, jnp.zeros_like(tinv)


_unit_lower_solve.defvjp(_unit_lower_solve_fwd, _unit_lower_solve_bwd)


def _gdn_prep(xq, xk, xv, ba, cwq, cwk, cwv, pa, pd, masks, tinv=None, with_inverse=False):
    lower, strict, eye = masks[0], masks[1], masks[2]
    lower_b = lower.astype(BF16)
    lane = lax.broadcasted_iota(jnp.int32, (CHUNK, LANES), 1)
    q = jnp.concatenate(_split(_conv_silu(xq, cwq), HEADS, 1), axis=0)
    k = jnp.concatenate(_split(_conv_silu(xk, cwk), HEADS, 1), axis=0)
    v = jnp.concatenate(_split(_conv_silu(xv, cwv), HEADS, 1), axis=0)
    qn = q * lax.rsqrt(jnp.sum(q * q, axis=-1, keepdims=True) + 1e-6) * (HEAD_DIM ** -0.5)
    kn = k * lax.rsqrt(jnp.sum(k * k, axis=-1, keepdims=True) + 1e-6)
    beta_f = jax.nn.sigmoid(ba)
    g_f = -jnp.exp(pa) * _softplus(ba + pd)

    def col(a, j):
        return jnp.broadcast_to(jnp.sum(jnp.where(lane == j, a, 0.0), axis=1, keepdims=True), (CHUNK, HEAD_DIM))

    beta = jnp.concatenate([col(beta_f, h) for h in range(HEADS)], axis=0)
    gs = [col(g_f, HEADS + h) for h in range(HEADS)]
    g = jnp.concatenate(gs, axis=0)
    gl = jnp.concatenate([jnp.broadcast_to(jnp.sum(gh, axis=0, keepdims=True), (CHUNK, HEAD_DIM)) for gh in gs], axis=0)
    gc = _mask_dot(lower_b, g)
    gc_rows = jnp.transpose(gc)
    decay = jnp.exp((jnp.concatenate([gc, gc], axis=1) - jnp.concatenate([gc_rows, gc_rows], axis=0)) * lower)
    egc = jnp.exp(gc)
    kb = kn * beta
    n = -(_bdot(kb, kn, NT) * decay * strict)
    if tinv is None:
        tinv = _unit_lower_inverse(lax.stop_gradient(n), eye)
    u, w = _split(_unit_lower_solve(n, jnp.concatenate([v * beta, kb * egc], axis=1), tinv), 2, 1)
    attn = _bdot(qn, kn, NT) * decay * lower
    outs = (u, w, qn * egc, kn * jnp.exp(gl - gc), attn, jnp.exp(gl))
    return outs + (tinv,) if with_inverse else outs


def _gdn_scan(states, u, w, qd, kt, attn, egl, z, nw):
    us, ws, qds, kts, egls = (_split(a, HEADS, 0) for a in (u, w, qd, kt, egl))
    vn = [us[h] - _bdot(ws[h], states[h]) for h in range(HEADS)]
    o = jnp.concatenate([_bdot(qds[h], states[h]) for h in range(HEADS)], axis=0)
    o = o + _bdot(attn, jnp.concatenate(vn, axis=0))
    new = [states[h] * jnp.concatenate([egls[h], egls[h]], axis=0) + _bdot(kts[h], vn[h], TN) for h in range(HEADS)]
    on = o * lax.rsqrt(jnp.mean(o * o, axis=-1, keepdims=True) + EPS) * nw
    return new, on * (z * jax.nn.sigmoid(z))


def _gdn_in_specs(step_of, chunks):
    nh = chunks * CHUNK // HALO
    main = [pl.BlockSpec((chunks * CHUNK, GDN_W), functools.partial(lambda col, i: (step_of(i), col), col))
            for col in (2, 3, 4)]
    halo = [pl.BlockSpec((HALO, GDN_W), functools.partial(lambda col, i: (jnp.maximum(step_of(i) * nh - 1, 0), col),
                                                         col)) for col in (2, 3, 4)]
    return main, halo


def _stk(width, step_of, chunks=1):
    return pl.BlockSpec((chunks * STACK, width), lambda i: (step_of(i), 0))


def _chunk_inputs(main_refs, halo_refs, k, first_step):
    rows = slice(k * CHUNK, (k + 1) * CHUNK)
    if k == 0:
        halos = [jnp.where(first_step, 0.0, h[...]) for h in halo_refs]
    else:
        halos = [m[k * CHUNK - HALO:k * CHUNK, :] for m in main_refs]
    return [jnp.concatenate([h, m[rows, :]], axis=0) for h, m in zip(halos, main_refs)]


def _gdn_prep_fwd(proj, ba, cw, pa, pd, masks):
    s = proj.shape[0]
    nc = s // CHUNK
    per = min(PREP_CHUNKS, nc)
    main, halo = _gdn_in_specs(lambda i: i, per)

    def body(xq_ref, xk_ref, xv_ref, hq_ref, hk_ref, hv_ref, ba_ref, cw_ref, pa_ref, pd_ref, mk_ref, *out_refs):
        first_step = pl.program_id(0) == 0
        rows = [[cw_ref[k:k + 1, j * GDN_W:(j + 1) * GDN_W] for k in range(CONV_K)] for j in range(3)]
        cst = [mk_ref[0], mk_ref[1], mk_ref[2]]
        for k in range(per):
            xs = _chunk_inputs((xq_ref, xk_ref, xv_ref), (hq_ref, hk_ref, hv_ref), k, first_step)
            outs = _gdn_prep(xs[0], xs[1], xs[2], ba_ref[k * CHUNK:(k + 1) * CHUNK, :], rows[0], rows[1], rows[2],
                             pa_ref[...], pd_ref[...], cst, with_inverse=True)
            for ref, val in zip(out_refs, outs):
                ref[k * STACK:(k + 1) * STACK, :] = val.astype(ref.dtype)

    ident = lambda i: i
    stacked = lambda dt: _sds((nc * STACK, HEAD_DIM), dt)
    wide, thin = _stk(STACK, ident, per), _stk(HEAD_DIM, ident, per)
    return _call(body, "gdn_prep_fwd", (nc // per,),
                 main + halo + [_tok(per * CHUNK, BA_PAD), _whole(cw), _vec(BA_PAD), _vec(BA_PAD), _whole(masks)],
                 [thin] * 4 + [wide, thin, wide],
                 [stacked(F32), stacked(BF16), stacked(BF16), stacked(BF16), _sds((nc * STACK, STACK), BF16),
                  stacked(F32), _sds((nc * STACK, STACK), BF16)])(
                     proj, proj, proj, proj, proj, proj, ba, cw, pa, pd, masks)


def _gdn_prep_bwd(cts, tinv, proj, ba, cw, pa, pd, masks):
    s = proj.shape[0]
    nc = s // CHUNK
    per = min(PREP_CHUNKS, nc)
    steps = nc // per
    rev = lambda i: steps - 1 - i
    main, halo = _gdn_in_specs(rev, per)

    def body(du_ref, dw_ref, dqd_ref, dkt_ref, dattn_ref, degl_ref, tinv_ref, xq_ref, xk_ref, xv_ref, hq_ref, hk_ref,
             hv_ref, ba_ref, cw_ref, pa_ref, pd_ref, mk_ref, dp_ref, dba_ref, dcw_ref, dpar_ref, carry):
        i = pl.program_id(0)
        first_step = rev(i) == 0

        @pl.when(i == 0)
        def _():
            carry[...] = jnp.zeros_like(carry)
            dcw_ref[...] = jnp.zeros_like(dcw_ref)
            dpar_ref[...] = jnp.zeros_like(dpar_ref)

        rows = [[cw_ref[k:k + 1, j * GDN_W:(j + 1) * GDN_W] for k in range(CONV_K)] for j in range(3)]
        cst = [mk_ref[0], mk_ref[1], mk_ref[2]]
        dxws = []
        for k in range(per):
            xs = _chunk_inputs((xq_ref, xk_ref, xv_ref), (hq_ref, hk_ref, hv_ref), k, first_step)
            stk = slice(k * STACK, (k + 1) * STACK)
            tinv_k = tinv_ref[stk, :]
            fn = lambda xq, xk, xv, b, rq, rk, rv, a, d: _gdn_prep(xq, xk, xv, b, rq, rk, rv, a, d, cst, tinv=tinv_k)
            _, vjp = jax.vjp(fn, xs[0], xs[1], xs[2], ba_ref[k * CHUNK:(k + 1) * CHUNK, :], rows[0], rows[1], rows[2],
                             pa_ref[...], pd_ref[...])
            dxq, dxk, dxv, dba, drq, drk, drv, dpa, dpd = vjp(
                (du_ref[stk, :], dw_ref[stk, :], dqd_ref[stk, :], dkt_ref[stk, :], dattn_ref[stk, :], degl_ref[stk, :]))
            dxws.append(jnp.concatenate([dxq, dxk, dxv], axis=1))
            dba_ref[k * CHUNK:(k + 1) * CHUNK, :] = dba.astype(BF16)
            for j, dr in enumerate((drq, drk, drv)):
                for kk in range(CONV_K):
                    dcw_ref[kk:kk + 1, j * GDN_W:(j + 1) * GDN_W] += dr[kk]
            dpar_ref[0:1, :] += dpa
            dpar_ref[1:2, :] += dpd
        pad = jnp.zeros((CHUNK - HALO, 3 * GDN_W), F32)
        for k in range(per):
            late = carry[...] if k == per - 1 else dxws[k + 1][0:HALO]
            dp_ref[k * CHUNK:(k + 1) * CHUNK, :] = (dxws[k][HALO:] + jnp.concatenate([pad, late], axis=0)).astype(BF16)
        carry[...] = dxws[0][0:HALO]

    acc = lambda shape: pl.BlockSpec(shape, lambda i: (0, 0))
    wide, thin = _stk(STACK, rev, per), _stk(HEAD_DIM, rev, per)
    return _call(body, "gdn_prep_bwd", (steps,),
                 [thin] * 4 + [wide, thin, wide] + main + halo
                 + [pl.BlockSpec((per * CHUNK, BA_PAD), lambda i: (rev(i), 0)), _whole(cw), _vec(BA_PAD), _vec(BA_PAD),
                    _whole(masks)],
                 [pl.BlockSpec((per * CHUNK, 3 * GDN_W), lambda i: (rev(i), 0)),
                  pl.BlockSpec((per * CHUNK, BA_PAD), lambda i: (rev(i), 0)), acc((CONV_K, 3 * GDN_W)),
                  acc((8, BA_PAD))],
                 [_sds((s, 3 * GDN_W), BF16), _sds((s, BA_PAD), BF16), _sds((CONV_K, 3 * GDN_W)), _sds((8, BA_PAD))],
                 scratch=[pltpu.VMEM((HALO, 3 * GDN_W), F32)])(
                     *cts, tinv, proj, proj, proj, proj, proj, proj, ba, cw, pa, pd, masks)


def _stack_heads(v):
    return jnp.concatenate(_split(v, HEADS, 1), axis=0)


def _unstack_heads(v):
    return jnp.concatenate(_split(v, HEADS, 0), axis=1)


def _gdn_scan_fwd(prep, proj, nw):
    s = proj.shape[0]
    nc = s // CHUNK
    ident = lambda i: i

    def body(u_ref, w_ref, qd_ref, kt_ref, attn_ref, egl_ref, z_ref, nw_ref, out_ref, st_ref, state):
        @pl.when(pl.program_id(0) == 0)
        def _():
            state[...] = jnp.zeros_like(state)

        st_ref[...] = state[...]
        states = [state[h * HEAD_DIM:(h + 1) * HEAD_DIM, :] for h in range(HEADS)]
        new, out = _gdn_scan(states, u_ref[...], w_ref[...], qd_ref[...], kt_ref[...], attn_ref[...], egl_ref[...],
                             _stack_heads(z_ref[...]), nw_ref[...])
        for h in range(HEADS):
            state[h * HEAD_DIM:(h + 1) * HEAD_DIM, :] = new[h]
        out_ref[...] = _unstack_heads(out)

    return _call(body, "gdn_scan_fwd", (nc,),
                 [_stk(HEAD_DIM, ident)] * 4 + [_stk(STACK, ident), _stk(HEAD_DIM, ident),
                                                _tok(CHUNK, GDN_W, col=5), _vec(HEAD_DIM)],
                 [_tok(CHUNK, GDN_W), pl.BlockSpec((HEADS * HEAD_DIM, HEAD_DIM), lambda i: (i, 0))],
                 [_sds((s, GDN_W)), _sds((nc * HEADS * HEAD_DIM, HEAD_DIM))],
                 scratch=[pltpu.VMEM((HEADS * HEAD_DIM, HEAD_DIM), F32)])(*prep, proj, nw)


def _gdn_scan_bwd(dout, prep, st, proj, nw):
    s = proj.shape[0]
    nc = s // CHUNK
    rev = lambda i: nc - 1 - i

    def body(do_ref, u_ref, w_ref, qd_ref, kt_ref, attn_ref, egl_ref, st_ref, z_ref, nw_ref,
             du_ref, dw_ref, dqd_ref, dkt_ref, dattn_ref, degl_ref, dz_ref, dnw_ref, dstate):
        @pl.when(pl.program_id(0) == 0)
        def _():
            dstate[...] = jnp.zeros_like(dstate)
            dnw_ref[...] = jnp.zeros_like(dnw_ref)

        states = [st_ref[h * HEAD_DIM:(h + 1) * HEAD_DIM, :] for h in range(HEADS)]
        f32 = lambda ref: ref[...].astype(F32)
        _, vjp = jax.vjp(_gdn_scan, states, u_ref[...], f32(w_ref), f32(qd_ref), f32(kt_ref), f32(attn_ref),
                         egl_ref[...], _stack_heads(z_ref[...]), nw_ref[...])
        dnew = [dstate[h * HEAD_DIM:(h + 1) * HEAD_DIM, :] for h in range(HEADS)]
        dst, du, dw, dqd, dkt, dattn, degl, dz, dnw = vjp((dnew, _stack_heads(do_ref[...])))
        for h in range(HEADS):
            dstate[h * HEAD_DIM:(h + 1) * HEAD_DIM, :] = dst[h]
        for ref, val in zip((du_ref, dw_ref, dqd_ref, dkt_ref, dattn_ref, degl_ref), (du, dw, dqd, dkt, dattn, degl)):
            ref[...] = val
        dz_ref[...] = _unstack_heads(dz).astype(BF16)
        dnw_ref[0:1, :] += dnw

    tokr = lambda n, col=0: pl.BlockSpec((CHUNK, n), lambda i: (rev(i), col))
    return _call(body, "gdn_scan_bwd", (nc,),
                 [tokr(GDN_W)] + [_stk(HEAD_DIM, rev)] * 4 + [_stk(STACK, rev), _stk(HEAD_DIM, rev),
                                                             pl.BlockSpec((HEADS * HEAD_DIM, HEAD_DIM),
                                                                          lambda i: (rev(i), 0)),
                                                             tokr(GDN_W, 5), _vec(HEAD_DIM)],
                 [_stk(HEAD_DIM, rev)] * 4 + [_stk(STACK, rev), _stk(HEAD_DIM, rev), tokr(GDN_W),
                                              pl.BlockSpec((8, HEAD_DIM), lambda i: (0, 0))],
                 [_sds((nc * STACK, HEAD_DIM))] * 4 + [_sds((nc * STACK, STACK)), _sds((nc * STACK, HEAD_DIM)),
                                                       _sds((s, GDN_W), BF16), _sds((8, HEAD_DIM))],
                 scratch=[pltpu.VMEM((HEADS * HEAD_DIM, HEAD_DIM), F32)])(dout, *prep, st, proj, nw)


def _wo_specs(l):
    half = N_DEV // 2
    return [pl.BlockSpec((half, 1, D_MODEL // N_DEV, D_MODEL), functools.partial(lambda k, *_: (k, l, 0, 0), k))
            for k in range(2)]


def _wo_half(ref):
    return ref[:, 0].reshape(ref.shape[0] * ref.shape[2], ref.shape[3])


def _out_mlp_fwd(ol, og, x, wo, g1, nw2, sc2, sh2, g2, wup, wdn, l):
    s = x.shape[0]
    t = _tile(s, 512)
    nj = wup.shape[0] // MLP_BLOCKS
    fc = wup.shape[3]

    def body(ol_ref, og_ref, x_ref, wol_ref, wog_ref, g1_ref, nw_ref, sc_ref, sh_ref, g2_ref, wup_ref, wdn_ref,
             x1_ref, mix_ref, ff_ref, x2_ref, h2_s, acc_s):
        j = pl.program_id(1)

        @pl.when(j == 0)
        def _():
            mix = _bdot(ol_ref[...], _wo_half(wol_ref)) + _bdot(og_ref[...], _wo_half(wog_ref))
            x1 = x_ref[...] + g1_ref[...] * mix
            mix_ref[...] = mix.astype(BF16)
            x1_ref[...] = x1
            h2, _, _ = _modulated_norm(x1, nw_ref[...], sc_ref[...], sh_ref[...])
            h2_s[...] = h2.astype(BF16)
            acc_s[...] = jnp.zeros_like(acc_s)

        part = None
        for b in range(MLP_BLOCKS):
            up = _bdot(h2_s[...], wup_ref[b, 0])
            down = _bdot(jnp.square(jnp.maximum(up, 0.0)), wdn_ref[b, 0])
            part = down if part is None else part + down
        acc_s[...] += part

        @pl.when(j == nj - 1)
        def _():
            ff_ref[...] = acc_s[...].astype(BF16)
            x2_ref[...] = x1_ref[...] + g2_ref[...] * acc_s[...]

    tk = lambda n: pl.BlockSpec((t, n), lambda i, j: (i, 0))
    return _call(body, "out_mlp_fwd", (s // t, nj),
                 [tk(LRU_W), tk(GDN_W), tk(D_MODEL)] + _wo_specs(l) + [_vec(D_MODEL)] * 5
                 + [pl.BlockSpec((MLP_BLOCKS, 1, D_MODEL, fc), lambda i, j: (j, l, 0, 0)),
                    pl.BlockSpec((MLP_BLOCKS, 1, fc, D_MODEL), lambda i, j: (j, l, 0, 0))],
                 [tk(D_MODEL)] * 4,
                 [_sds((s, D_MODEL)), _sds((s, D_MODEL), BF16), _sds((s, D_MODEL), BF16), _sds((s, D_MODEL))],
                 scratch=[pltpu.VMEM((t, D_MODEL), BF16), pltpu.VMEM((t, D_MODEL), F32)])(
                     ol, og, x, wo, wo, g1, nw2, sc2, sh2, g2, wup, wdn)


def _mlp_bwd(dx2, x1, ff, nw2, sc2, sh2, g2, wup, wdn, l):
    s = x1.shape[0]
    t = _tile(s, 512)
    nj = wup.shape[0] // MLP_BLOCKS
    fc = wup.shape[3]

    def body(dx2_ref, x1_ref, ff_ref, nw_ref, sc_ref, sh_ref, g2_ref, wup_ref, wdn_ref,
             act_ref, dup_ref, h2_ref, dff_ref, dx1_ref, rows_ref, dh2_s):
        i, j = pl.program_id(0), pl.program_id(1)

        @pl.when((i == 0) & (j == 0))
        def _():
            rows_ref[...] = jnp.zeros_like(rows_ref)

        @pl.when(j == 0)
        def _():
            h2, _, _ = _modulated_norm(x1_ref[...], nw_ref[...], sc_ref[...], sh_ref[...])
            h2_ref[...] = h2.astype(BF16)
            dx2 = dx2_ref[...]
            dff_ref[...] = (dx2 * g2_ref[...]).astype(BF16)
            rows_ref[2:3, :] += jnp.sum(dx2 * ff_ref[...].astype(F32), axis=0, keepdims=True)
            dh2_s[...] = jnp.zeros_like(dh2_s)

        part = None
        for b in range(MLP_BLOCKS):
            cols = slice(b * fc, (b + 1) * fc)
            up = _bdot(h2_ref[...], wup_ref[b, 0])
            ru = jnp.maximum(up, 0.0)
            act_ref[:, cols] = (ru * ru).astype(BF16)
            dup = (_bdot(dff_ref[...], wdn_ref[b, 0], NT) * (2.0 * ru)).astype(BF16)
            dup_ref[:, cols] = dup
            back = _bdot(dup, wup_ref[b, 0], NT)
            part = back if part is None else part + back
        dh2_s[...] += part

        @pl.when(j == nj - 1)
        def _():
            xv = x1_ref[...]
            _, n, r = _modulated_norm(xv, nw_ref[...], sc_ref[...], sh_ref[...])
            dx, dsh, dsc, dnw = _modulated_norm_bwd(dh2_s[...], xv, n, r, nw_ref[...], sc_ref[...])
            dx1_ref[...] = dx2_ref[...] + dx
            rows_ref[0:1, :] += dsh
            rows_ref[1:2, :] += dsc
            rows_ref[3:4, :] += dnw

    tk = lambda n: pl.BlockSpec((t, n), lambda i, j: (i, 0))
    tj = pl.BlockSpec((t, MLP_BLOCKS * fc), lambda i, j: (i, j))
    return _call(body, "mlp_bwd", (s // t, nj),
                 [tk(D_MODEL)] * 3 + [_vec(D_MODEL)] * 4
                 + [pl.BlockSpec((MLP_BLOCKS, 1, D_MODEL, fc), lambda i, j: (j, l, 0, 0)),
                    pl.BlockSpec((MLP_BLOCKS, 1, fc, D_MODEL), lambda i, j: (j, l, 0, 0))],
                 [tj, tj, tk(D_MODEL), tk(D_MODEL), tk(D_MODEL), pl.BlockSpec((8, D_MODEL), lambda i, j: (0, 0))],
                 [_sds((s, D_FF), BF16), _sds((s, D_FF), BF16), _sds((s, D_MODEL), BF16),
                  _sds((s, D_MODEL), BF16), _sds((s, D_MODEL)), _sds((8, D_MODEL))],
                 scratch=[pltpu.VMEM((t, D_MODEL), F32)])(dx2, x1, ff, nw2, sc2, sh2, g2, wup, wdn)


def _outproj_bwd(dx1, mix, g1, wo, l):
    s = dx1.shape[0]
    t = _tile(s, 512)

    def body(dx1_ref, mix_ref, g1_ref, wol_ref, wog_ref, dmix_ref, dol_ref, dog_ref, rows_ref):
        @pl.when(pl.program_id(0) == 0)
        def _():
            rows_ref[...] = jnp.zeros_like(rows_ref)

        dx1v = dx1_ref[...]
        rows_ref[0:1, :] += jnp.sum(dx1v * mix_ref[...].astype(F32), axis=0, keepdims=True)
        dmix = (dx1v * g1_ref[...]).astype(BF16)
        dmix_ref[...] = dmix
        dol_ref[...] = _bdot(dmix, _wo_half(wol_ref), NT)
        dog_ref[...] = _bdot(dmix, _wo_half(wog_ref), NT)

    return _call(body, "outproj_bwd", (s // t,),
                 [_tok(t, D_MODEL), _tok(t, D_MODEL), _vec(D_MODEL)] + _wo_specs(l),
                 [_tok(t, D_MODEL), _tok(t, LRU_W), _tok(t, GDN_W), pl.BlockSpec((8, D_MODEL), lambda i: (0, 0))],
                 [_sds((s, D_MODEL), BF16), _sds((s, LRU_W)), _sds((s, GDN_W)), _sds((8, D_MODEL))])(dx1, mix, g1, wo, wo)


def _tn_matmul(a, b, name, out=None, l=0, blocked=False, row_block=0):
    s, m = a.shape
    n = b.shape[1]
    ts, bm = _tile(s, 2048), _tile(m, 1024)
    bn = next(w for w in (512, 640, 384, 256, 128) if n % w == 0)

    def body(a_ref, b_ref, *rest):
        o_ref = rest[-1]

        @pl.when(pl.program_id(2) == 0)
        def _():
            o_ref[...] = jnp.zeros_like(o_ref)

        acc = _bdot(a_ref[...], b_ref[...], TN)
        o_ref[...] += acc.reshape(o_ref.shape)

    in_specs = [pl.BlockSpec((ts, bm), lambda i, j, k: (k, i)), pl.BlockSpec((ts, bn), lambda i, j, k: (k, j))]
    grid = (m // bm, n // bn, s // ts)
    if out is None:
        return _call(body, name, grid, in_specs, pl.BlockSpec((bm, bn), lambda i, j, k: (i, j)), _sds((m, n)))(a, b)
    if blocked:
        out_spec = pl.BlockSpec((1, 1, bm, bn), lambda i, j, k: (l, j, i, 0))
    else:
        out_spec = pl.BlockSpec((1, bm, bn), lambda i, j, k: (l, i + row_block * (m // bm), j))
    return _call(body, name, grid, in_specs + [pl.BlockSpec(memory_space=pl.ANY)], out_spec,
                 _sds(out.shape), aliases={2: 0})(a, b, out)


def _final_fwd_bwd(x, target, fw):
    s = x.shape[0]
    t = _tile(s, 512)

    def body(x_ref, tg_ref, fw_ref, dx_ref, rows_ref):
        @pl.when(pl.program_id(0) == 0)
        def _():
            rows_ref[...] = jnp.zeros_like(rows_ref)

        xv = x_ref[...]
        fwv = fw_ref[...]
        r = lax.rsqrt(jnp.mean(xv * xv, axis=-1, keepdims=True) + EPS)
        err = xv * r * fwv - tg_ref[...]
        part = 0.5 * jnp.sum(jnp.mean(err * err, axis=-1, keepdims=True), axis=0, keepdims=True)
        rows_ref[1:2, :] += jnp.broadcast_to(part, (1, D_MODEL))
        dy = err * (1.0 / D_MODEL)
        rows_ref[0:1, :] += jnp.sum(dy * xv * r, axis=0, keepdims=True)
        dxn = dy * fwv
        dx_ref[...] = r * dxn - xv * (r * r * r) * jnp.mean(dxn * xv, axis=-1, keepdims=True)

    return _call(body, "final_fwd_bwd", (s // t,),
                 [_tok(t, D_MODEL), _tok(t, D_MODEL), _vec(D_MODEL)],
                 [_tok(t, D_MODEL), pl.BlockSpec((8, D_MODEL), lambda i: (0, 0))],
                 [_sds((s, D_MODEL)), _sds((8, D_MODEL))])(x, target, fw)


def _adamw(w, g, m, v):
    m = ADAM_B1 * m + (1.0 - ADAM_B1) * g
    v = ADAM_B2 * v + (1.0 - ADAM_B2) * (g * g)
    m_hat = m / (1.0 - ADAM_B1 ** ADAM_STEP)
    v_hat = v / (1.0 - ADAM_B2 ** ADAM_STEP)
    return -ADAM_LR * (m_hat / (jnp.sqrt(v_hat) + ADAM_EPS) + ADAM_WD * w), m, v


def _mod_local(c_all, wmod, bmod_cols):
    nl, _, cols = wmod.shape

    def body(c_ref, w_ref, b_ref, o_ref):
        cv = c_ref[...]
        o_ref[0] = _bdot(cv * jax.nn.sigmoid(cv), w_ref[0]) + b_ref[0]

    return _call(body, "mod_local", (nl,),
                 [_whole(c_all), pl.BlockSpec((1, D_MODEL, cols), lambda l: (l, 0, 0)),
                  pl.BlockSpec((1, 1, cols), lambda l: (l, 0, 0))],
                 pl.BlockSpec((1, N_DEV, cols), lambda l: (l, 0, 0)), _sds((nl, N_DEV, cols)))(c_all, wmod, bmod_cols)


def _wmod_update(c_all, dmod_cols, w, m, v):
    nl, _, cols = w.shape

    def body(c_ref, d_ref, w_ref, m_ref, v_ref, g_ref, dl_ref, nm_ref, nv_ref):
        cv = c_ref[...]
        g = _bdot(cv * jax.nn.sigmoid(cv), d_ref[0], TN)
        g_ref[0] = g
        dl_ref[0], nm_ref[0], nv_ref[0] = _adamw(w_ref[0], g, m_ref[0], v_ref[0])

    wspec = pl.BlockSpec((1, D_MODEL, cols), lambda l: (l, 0, 0))
    return _call(body, "wmod_update", (nl,),
                 [_whole(c_all), pl.BlockSpec((1, N_DEV, cols), lambda l: (l, 0, 0)), wspec, wspec, wspec],
                 [wspec] * 4, [_sds(w.shape)] * 4)(c_all, dmod_cols, w, m, v)


def _sum_devices(gathered):
    _, r, _ = gathered.shape

    def body(g_ref, o_ref):
        acc = g_ref[0]
        for d in range(1, N_DEV):
            acc = acc + g_ref[d]
        o_ref[...] = acc

    return _call(body, "sum_devices", (1,), [_whole(gathered)], pl.BlockSpec((r, LANES), lambda i: (0, 0)),
                 _sds((r, LANES)))(gathered)


def _adam_flat(w, g, m, v):
    r = w.shape[0]

    def body(w_ref, g_ref, m_ref, v_ref, dl_ref, nm_ref, nv_ref):
        dl_ref[...], nm_ref[...], nv_ref[...] = _adamw(w_ref[...], g_ref[...], m_ref[...], v_ref[...])

    spec = pl.BlockSpec((r, LANES), lambda i: (0, 0))
    return _call(body, "adam_small", (1,), [spec] * 4, [spec] * 3, [_sds((r, LANES))] * 3)(w, g, m, v)


def _pair_add(x, p, core):
    _, r, c = x.shape
    tr = _tile(r, 128 if c > 512 else 256)

    def body(core_ref, x_ref, p_ref, o_ref):
        o_ref[...] = (x_ref[...] + p_ref[...]).astype(BF16)

    return _call(body, "pair_add", (4, r // tr),
                 [pl.BlockSpec((1, tr, c), lambda q, i, core_ref: (2 * q + core_ref[0], i, 0)),
                  pl.BlockSpec((1, tr, c), lambda q, i, core_ref: (q, i, 0))],
                 pl.BlockSpec((1, tr, c), lambda q, i, core_ref: (q, i, 0)), _sds((4, r, c), BF16),
                 prefetch=1)(core, x, p)


def _reduce_adam(x, p, q, place, w, m, v, l, outs):
    _, r, c = x.shape
    tr = _tile(r, 128 if c > 512 else 256)

    def body(place_ref, x_ref, p_ref, q_ref, w_ref, m_ref, v_ref, *rest):
        g_ref, dl_ref, nm_ref, nv_ref = rest[-4:]
        g = (((x_ref[0] + p_ref[0]) + q_ref[0].astype(F32)) + q_ref[1].astype(F32)) + q_ref[2].astype(F32)
        g_ref[0] = g
        dl_ref[0], nm_ref[0], nv_ref[0] = _adamw(w_ref[0], g, m_ref[0], v_ref[0])

    flat = pl.BlockSpec((1, tr, c), lambda i, place_ref: (l, i, 0))
    through = pl.BlockSpec(memory_space=pl.ANY)
    return _call(body, "reduce_adam", (r // tr,),
                 [pl.BlockSpec((1, tr, c), lambda i, place_ref: (place_ref[0], i, 0)),
                  pl.BlockSpec((1, tr, c), lambda i, place_ref: (place_ref[1], i, 0)),
                  pl.BlockSpec((3, tr, c), lambda i, place_ref: (0, i, 0)), flat, flat, flat] + [through] * 4,
                 [flat] * 4, [_sds(w.shape)] * 4, prefetch=1, aliases={7 + k: k for k in range(4)})(
                     place, x, p, q, w, m, v, *outs)


def _place():
    return lax.axis_index("x"), lax.axis_index("y"), lax.axis_index("c")


def _all_gather(xs, name, space):
    n = len(xs)

    def body(*refs):
        x_refs, o_refs = refs[:n], refs[n:2 * n]
        send_sems, recv_sems, local_sems = refs[2 * n:]
        x, y, c = _place()
        me, sibling = (x, y, c), (x, y, 1 - c)
        chips = [(1 - x, y), (x, 1 - y), (1 - x, 1 - y)]

        def blk(a, p):
            return o_refs[a].at[4 * p[0] + 2 * p[1] + p[2]]

        def copy(a, k, block, to, src=None):
            return pltpu.make_async_remote_copy(
                src_ref=blk(a, block) if src is None else src, dst_ref=blk(a, block),
                send_sem=send_sems.at[a, k], recv_sem=recv_sems.at[a, k], device_id=to, device_id_type=MESH)

        mine = [pltpu.make_async_copy(x_refs[a], blk(a, me), local_sems.at[a]) for a in range(n)]
        for cp in mine:
            cp.start()
        first = []
        for a in range(n):
            first.append(copy(a, 0, me, sibling, src=x_refs[a]))
            first += [copy(a, 1 + j, me, (*chip, c), src=x_refs[a]) for j, chip in enumerate(chips)]
        for cp in first:
            cp.start()
        passed = []
        for j, chip in enumerate(chips):
            for a in range(n):
                copy(a, 1 + j, (*chip, c), me).wait_recv()
                cp = copy(a, 4 + j, (*chip, c), sibling)
                cp.start()
                passed.append(cp)
        for a in range(n):
            copy(a, 0, sibling, me).wait_recv()
        for j, chip in enumerate(chips):
            for a in range(n):
                copy(a, 4 + j, (*chip, 1 - c), me).wait_recv()
        for cp in first + passed:
            cp.wait_send()
        for cp in mine:
            cp.wait()

    spec = pl.BlockSpec(memory_space=space)
    return pl.pallas_call(
        body, name=name, out_shape=[_sds((N_DEV,) + a.shape, a.dtype) for a in xs],
        in_specs=[spec] * n, out_specs=[spec] * n,
        scratch_shapes=[pltpu.SemaphoreType.DMA((n, 7)), pltpu.SemaphoreType.DMA((n, 7)),
                        pltpu.SemaphoreType.DMA((n,))])(*xs)


_HBM_SPEC = pl.BlockSpec(memory_space=pltpu.HBM)
_SEM_SPEC = pl.BlockSpec(memory_space=pltpu.SEMAPHORE)
_EFFECT = pltpu.SideEffectType.DATAFLOW_SIDE_EFFECTING


def _descriptors(plan, src_refs, land_refs, send_sems, recv_sems, which=None):
    return [pltpu.make_async_remote_copy(src_ref=s, dst_ref=d, send_sem=send_sems.at[k], recv_sem=recv_sems.at[k],
                                         device_id=dev, device_id_type=MESH)
            for k, (s, d, dev) in enumerate(plan(src_refs, land_refs)) if which is None or k in which]


def _split_start(name, plan, n, srcs, lands, after):
    ns, nb = len(srcs), len(srcs) + len(lands)

    def body(*refs):
        for cp in _descriptors(plan, refs[:ns], refs[ns:nb], refs[nb + 1], refs[nb + 2]):
            cp.start()
        refs[-1][...] = jnp.zeros_like(refs[-1])

    bufs = [pltpu.with_memory_space_constraint(a, pltpu.HBM) for a in list(srcs) + list(lands)]
    outs = pl.pallas_call(
        body, name=name,
        out_shape=(pltpu.SemaphoreType.DMA((n,)), pltpu.SemaphoreType.DMA((n,)))
        + tuple(pltpu.HBM(a.shape, a.dtype) for a in bufs) + (_sds((8, LANES)),),
        in_specs=[_HBM_SPEC] * nb + [pl.BlockSpec(memory_space=pl.ANY)],
        out_specs=(_SEM_SPEC, _SEM_SPEC) + (_HBM_SPEC,) * nb + (pl.BlockSpec(memory_space=pltpu.VMEM),),
        input_output_aliases={i: 2 + i for i in range(nb)},
        compiler_params=pltpu.CompilerParams(has_side_effects=_EFFECT))(*bufs, after)
    return dict(send=outs[0], recv=outs[1], srcs=list(outs[2:2 + ns]), lands=list(outs[2 + ns:2 + nb]), token=outs[-1])


def _split_wait(name, plan, flight, which, after):
    srcs, lands = flight["srcs"], flight["lands"]
    ns, nb = len(srcs), len(srcs) + len(lands)

    def body(*refs):
        for cp in _descriptors(plan, refs[:ns], refs[ns:nb], refs[nb], refs[nb + 1], set(which)):
            cp.wait_send()
            cp.wait_recv()

    outs = pl.pallas_call(
        body, name=name, out_shape=tuple(pltpu.HBM(a.shape, a.dtype) for a in srcs + lands),
        in_specs=[_HBM_SPEC] * nb + [_SEM_SPEC, _SEM_SPEC, pl.BlockSpec(memory_space=pl.ANY)],
        out_specs=(_HBM_SPEC,) * nb, input_output_aliases={i: i for i in range(nb)},
        compiler_params=pltpu.CompilerParams(has_side_effects=_EFFECT))(*srcs, *lands, flight["send"], flight["recv"],
                                                                       after)
    return dict(flight, srcs=list(outs[:ns]), lands=list(outs[ns:nb]))


GATHER_PEERS = N_DEV - 1


def _gather_plan(items):
    def plan(src_refs, land_refs):
        x, y, c = _place()
        me = 4 * x + 2 * y + c
        out = []
        for a, l in items:
            for r in range(1, N_DEV):
                peer = (1 - x if r & 4 else x, 1 - y if r & 2 else y, 1 - c if r & 1 else c)
                out.append((src_refs[a].at[l], land_refs[a].at[me, l], peer))
        return out

    return plan


def _pair_plan(narr):
    def plan(src_refs, land_refs):
        x, y, c = _place()
        return [(src_refs[a].at[2 * q + (1 - c)], land_refs[a].at[q], (x, y, 1 - c))
                for a in range(narr) for q in range(4)]

    return plan


def _chip_plan(narr):
    def plan(src_refs, land_refs):
        x, y, c = _place()
        chips = [(1 - x, y), (x, 1 - y), (1 - x, 1 - y)]
        return [(src_refs[a].at[2 * chip[0] + chip[1]], land_refs[a].at[r], (*chip, c))
                for a in range(narr) for r, chip in enumerate(chips)]

    return plan


class _GradReducer:
    def __init__(self, names, w, mom, var, place, core):
        self.names, self.w, self.mom, self.var, self.place, self.core = names, w, mom, var, place, core
        self.outs = {k: [lax.empty(w[k].shape, F32) for _ in range(4)] for k in names}
        self.n = len(names)

    def start(self, l, grads, after):
        self.l, self.xs = l, [grads[k] for k in self.names]
        lands = [lax.empty((4,) + a.shape[1:], F32) for a in self.xs]
        self.pair = _split_start(f"grad_pair_start{l}", _pair_plan(self.n), 4 * self.n, self.xs, lands, after)
        return self.pair["token"][0, 0]

    def middle(self, after):
        self.pair = _split_wait(f"grad_pair_wait{self.l}", _pair_plan(self.n), self.pair, range(4 * self.n), after)
        self.xs, self.ps = self.pair["srcs"], self.pair["lands"]
        ys = [_pair_add(x, p, self.core) for x, p in zip(self.xs, self.ps)]
        lands = [lax.empty((3,) + a.shape[1:], BF16) for a in ys]
        self.chip = _split_start(f"grad_chip_start{self.l}", _chip_plan(self.n), 3 * self.n, ys, lands, ys[-1])
        return self.chip["token"][0, 0]

    def finish(self, after):
        chip = _split_wait(f"grad_chip_wait{self.l}", _chip_plan(self.n), self.chip, range(3 * self.n), after)
        for k, x, p, q in zip(self.names, self.xs, self.ps, chip["lands"]):
            self.outs[k] = _reduce_adam(x, p, q, self.place, self.w[k], self.mom[k], self.var[k], self.l, self.outs[k])


def _size(shape):
    size = 1
    for d in shape:
        size *= d
    return size


def _slab_rows(shape):
    return -(-_size(shape) // (8 * LANES)) * 8


def _pack(arrs):
    parts = []
    for a in arrs:
        flat = a.reshape(-1).astype(F32)
        parts.append(jnp.pad(flat, (0, _slab_rows(a.shape) * LANES - flat.shape[0])).reshape(-1, LANES))
    return jnp.concatenate(parts, axis=0)


def _unpack(slab, shapes):
    out, off = [], 0
    for shp in shapes:
        rows = _slab_rows(shp)
        out.append(slab[off:off + rows].reshape(-1)[:_size(shp)].reshape(shp))
        off += rows
    return out


def _dense_blocks(w):
    eye = jnp.eye(LRU_BLOCKS, dtype=w.dtype)
    return (eye[:, None, :, None] * w[:, :, None, :]).reshape(LRU_W, LRU_W)


def _diag_blocks(dense):
    return jnp.stack([dense[g * LRU_BLOCK:(g + 1) * LRU_BLOCK, g * LRU_BLOCK:(g + 1) * LRU_BLOCK]
                      for g in range(LRU_BLOCKS)])


def _alpha_lanes(v):
    return jnp.zeros((1, BA_PAD), F32).at[0, HEADS:2 * HEADS].set(v)


def _local_step(x, target, mod, p, fetch, reducer=None):
    nl = mod.shape[0]
    row = lambda v: v.reshape(1, -1)
    masks = _gdn_masks()
    saved = []
    xc = x
    for l in range(nl):
        win, wba, lin = fetch(l, "in", xc)
        mv = [row(mod[l, k * D_MODEL:(k + 1) * D_MODEL]) for k in range(N_MOD)]
        sh1, sc1, g1, sh2, sc2, g2 = mv
        nw1, nw2 = row(p["norm_mix_w"][l]), row(p["norm_mlp_w"][l])
        wa, wx = _dense_blocks(p["lru_gate_a_w"][l]).astype(BF16), _dense_blocks(p["lru_gate_x_w"][l]).astype(BF16)
        lru_args = (p["lru_conv_w"][l], row(p["lru_conv_b"][l]), wa, wx, row(p["lru_gate_a_b"][l]),
                    row(p["lru_gate_x_b"][l]), row(p["lru_lambda"][l]), row(p["lru_norm_w"][l]))
        gdn_args = (p["gdn_conv_w"][l], _alpha_lanes(p["gdn_a_log"][l]), _alpha_lanes(p["gdn_dt_bias"][l]), masks)
        gnw = row(p["gdn_norm_w"][l])
        proj, ba = _inproj_fwd(xc, nw1, sc1, sh1, win, wba, lin)
        ol, hs = _lru_fwd(proj, *lru_args)
        *prep, tinv = _gdn_prep_fwd(proj, ba, *gdn_args)
        og, st = _gdn_scan_fwd(prep, proj, gnw)
        wo, wup, wdn = fetch(l, "rest", og)
        x1, mix, ff, x2 = _out_mlp_fwd(ol, og, xc, wo, g1, nw2, sc2, sh2, g2, wup, wdn, l)
        saved.append(dict(x=xc, mv=mv, nw1=nw1, nw2=nw2, lru_args=lru_args, gdn_args=gdn_args, gnw=gnw, proj=proj,
                          ba=ba, ol=ol, hs=hs, prep=prep, tinv=tinv, og=og, st=st, x1=x1, mix=mix, ff=ff,
                          win=win, wba=wba, lin=lin))
        xc = x2

    dx, frows = _final_fwd_bwd(xc, target, row(p["final_norm_w"]))
    loss_part = frows[1, 0]
    small = {k: [None] * nl for k in ("norm_mix_w", "norm_mlp_w", "lru_conv_w", "lru_conv_b", "lru_gate_a_w",
                                      "lru_gate_a_b", "lru_gate_x_w", "lru_gate_x_b", "lru_lambda", "lru_norm_w",
                                      "gdn_conv_w", "gdn_a_log", "gdn_dt_bias", "gdn_norm_w")}
    fc = D_FF // N_DEV
    big = [None] * nl
    dmod = [None] * nl
    busy = False
    for l in reversed(range(nl)):
        sv = saved[l]
        sh1, sc1, g1, sh2, sc2, g2 = sv["mv"]
        if busy:
            g2 = g2 + started
        act, dup, h2b, dffb, dx1, rows2 = _mlp_bwd(dx, sv["x1"], sv["ff"], sv["nw2"], sc2, sh2, g2, wup, wdn, l)
        if busy:
            g1 = g1 + reducer.middle(dx1)
        g_up = _tn_matmul(h2b, dup, "grad_w_up", out=lax.empty((1, N_DEV, D_MODEL, fc), F32), blocked=True)[0]
        g_down = _tn_matmul(act, dffb, "grad_w_down", out=lax.empty((1, D_FF, D_MODEL), F32))
        dmix, dol, dog, rows1 = _outproj_bwd(dx1, sv["mix"], g1, wo, l)
        g_out = _tn_matmul(sv["ol"], dmix, "grad_w_out_lru", out=lax.empty((1, D_MODEL, D_MODEL), F32))
        g_out = _tn_matmul(sv["og"], dmix, "grad_w_out_gdn", out=g_out, row_block=1)
        dpl, dwa, dwx, lrows = _lru_bwd(dol, sv["proj"], sv["hs"], *sv["lru_args"])
        *cts, dpz, gnrow = _gdn_scan_bwd(dog, sv["prep"], sv["st"], sv["proj"], sv["gnw"])
        dpq, dba, dcw, dpar = _gdn_prep_bwd(cts, sv["tinv"], sv["proj"], sv["ba"], *sv["gdn_args"])
        dx, hb, rows0 = _inproj_bwd(dpl, dpq, dpz, dba, sv["x"], dx1, sv["nw1"], sc1, sh1, sv["win"], sv["wba"],
                                    sv["lin"])
        if busy:
            reducer.finish(dx)
        dproj = jnp.concatenate([dpl, dpq, dpz, dba], axis=1)
        g_in = jnp.transpose(_tn_matmul(hb, dproj, "grad_w_in")[:, :IN_COLS].reshape(
            D_MODEL, N_DEV, IN_COLS // N_DEV), (1, 0, 2))
        big[l] = dict(w_in=g_in, w_out=g_out.reshape(N_DEV, D_MODEL // N_DEV, D_MODEL), w_up=g_up,
                      w_down=g_down.reshape(N_DEV, fc, D_MODEL))
        if reducer is not None:
            started, busy = reducer.start(l, big[l], g_in), True
        dmod[l] = jnp.concatenate([rows0[0], rows0[1], rows1[0], rows2[0], rows2[1], rows2[2]])
        small["norm_mix_w"][l], small["norm_mlp_w"][l] = rows0[2], rows2[3]
        small["lru_conv_w"][l], small["lru_conv_b"][l] = lrows[8:8 + CONV_K], lrows[0]
        small["lru_gate_a_w"][l], small["lru_gate_x_w"][l] = _diag_blocks(dwa), _diag_blocks(dwx)
        small["lru_gate_a_b"][l], small["lru_gate_x_b"][l] = lrows[1], lrows[2]
        small["lru_lambda"][l], small["lru_norm_w"][l] = lrows[3], lrows[4]
        small["gdn_conv_w"][l] = dcw
        small["gdn_a_log"][l], small["gdn_dt_bias"][l] = dpar[0, HEADS:2 * HEADS], dpar[1, HEADS:2 * HEADS]
        small["gdn_norm_w"][l] = gnrow[0]
    if busy:
        reducer.middle(dx)
        reducer.finish(dx)
    small = {k: jnp.stack(v) for k, v in small.items()}
    small["final_norm_w"] = frows[0]
    return loss_part, dx, big, small, jnp.stack(dmod)


SMALL_REPLICATED = ("norm_mix_w", "norm_mlp_w", "b_mod", "lru_conv_b", "lru_gate_a_w", "lru_gate_a_b", "lru_gate_x_w",
                    "lru_gate_x_b", "lru_lambda", "lru_norm_w", "gdn_a_log", "gdn_dt_bias", "gdn_norm_w",
                    "final_norm_w")
SMALL_SHARDED = ("lru_conv_w", "gdn_conv_w")
WEIGHT_ORDER = ("norm_mix_w", "norm_mlp_w", "w_mod", "b_mod", "w_in", "lru_conv_w", "lru_conv_b", "lru_gate_a_w",
                "lru_gate_a_b", "lru_gate_x_w", "lru_gate_x_b", "lru_lambda", "lru_norm_w", "gdn_conv_w", "gdn_a_log",
                "gdn_dt_bias", "gdn_norm_w", "w_out", "w_up", "w_down", "final_norm_w")


def kernel(x, c, norm_mix_w, norm_mlp_w, w_mod, b_mod, w_in, lru_conv_w, lru_conv_b, lru_gate_a_w, lru_gate_a_b, lru_gate_x_w, lru_gate_x_b, lru_lambda, lru_norm_w, gdn_conv_w, gdn_a_log, gdn_dt_bias, gdn_norm_w, w_out, w_up, w_down, final_norm_w, loss_target, m_norm_mix_w, m_norm_mlp_w, m_w_mod, m_b_mod, m_w_in, m_lru_conv_w, m_lru_conv_b, m_lru_gate_a_w, m_lru_gate_a_b, m_lru_gate_x_w, m_lru_gate_x_b, m_lru_lambda, m_lru_norm_w, m_gdn_conv_w, m_gdn_a_log, m_gdn_dt_bias, m_gdn_norm_w, m_w_out, m_w_up, m_w_down, m_final_norm_w, v_norm_mix_w, v_norm_mlp_w, v_w_mod, v_b_mod, v_w_in, v_lru_conv_w, v_lru_conv_b, v_lru_gate_a_w, v_lru_gate_a_b, v_lru_gate_x_w, v_lru_gate_x_b, v_lru_lambda, v_lru_norm_w, v_gdn_conv_w, v_gdn_a_log, v_gdn_dt_bias, v_gdn_norm_w, v_w_out, v_w_up, v_w_down, v_final_norm_w):
    args = dict(locals())
    w = {k: args[k] for k in WEIGHT_ORDER}
    mom = {k: args["m_" + k] for k in WEIGHT_ORDER}
    var = {k: args["v_" + k] for k in WEIGHT_ORDER}
    nl = w_in.shape[0]
    px, py, pc = _place()
    me = 4 * px + 2 * py + pc
    core = jnp.reshape(pc, (1,)).astype(jnp.int32)

    shapes0 = [c.shape, lru_conv_w.shape, gdn_conv_w.shape]
    (g0,) = _all_gather([_pack([c, lru_conv_w, gdn_conv_w])], "gather_cond", pltpu.VMEM)
    per_dev = [_unpack(g0[d], shapes0) for d in range(N_DEV)]
    c_all = jnp.concatenate([pd[0] for pd in per_dev], axis=0)
    lru_conv_full = jnp.concatenate([pd[1] for pd in per_dev], axis=-1)
    gdn_conv_full = jnp.concatenate([pd[2] for pd in per_dev], axis=-1)

    cols = w_mod.shape[2]
    bmod_cols = lax.dynamic_slice_in_dim(b_mod, me * cols, cols, axis=1).reshape(nl, 1, cols)
    mod_cols = _mod_local(c_all, w_mod, bmod_cols)
    (g1,) = _all_gather([mod_cols.reshape(nl * N_DEV, cols)], "gather_mod", pltpu.VMEM)
    g1 = g1.reshape(N_DEV, nl, N_DEV, cols)
    mod = jnp.transpose(lax.dynamic_index_in_dim(g1, me, axis=2, keepdims=False), (1, 0, 2)).reshape(nl, N_DEV * cols)

    shards = [a.astype(BF16) for a in (w_in, w_out, w_up, w_down)]
    (first_in,) = _all_gather([shards[0][:1]], "gather_w_in_first", pl.ANY)
    items = [(a, 0) for a in (1, 2, 3)] + [(a, l) for l in range(1, nl) for a in range(4)]
    plan = _gather_plan(items)
    lands = [lax.dynamic_update_slice_in_dim(lax.empty((N_DEV,) + a.shape, BF16), a[None], me, axis=0) for a in shards]
    flight = [_split_start("gather_weights_start", plan, len(items) * GATHER_PEERS, shards, lands, first_in)]
    mod = mod + flight[0]["token"][0, 0]

    def fetch(l, what, after):
        wanted = [k for k, (a, ll) in enumerate(items) if ll == l and (a == 0) == (what == "in")]
        if wanted:
            flight[0] = _split_wait(f"gather_weights_wait_{what}{l}", plan, flight[0],
                                    [k * GATHER_PEERS + r for k in wanted for r in range(GATHER_PEERS)], after)
        gin, gout, gup, gdn = flight[0]["lands"]
        if what == "rest":
            return gout, gup, gdn
        gin = first_in[:, 0] if l == 0 else gin[:, l]
        win = jnp.transpose(gin, (1, 0, 2)).reshape(1, D_MODEL, IN_COLS)
        wba = jnp.pad(win[:, :, IN_MAIN:], ((0, 0), (0, 0), (0, BA_PAD - (IN_COLS - IN_MAIN))))
        return win, wba, 0

    p = dict(w)
    p["lru_conv_w"], p["gdn_conv_w"] = lru_conv_full, gdn_conv_full

    order = ("w_in", "w_out", "w_up", "w_down")
    place = jnp.stack([me, 2 * px + py]).astype(jnp.int32)
    reducer = _GradReducer(order, w, mom, var, place, core)
    loss_part, grad_x, _, small, dmod = _local_step(x[0], loss_target[0], mod, p, fetch, reducer)
    loss = lax.psum(loss_part, MESH_AXES)

    small_names = sorted(small)
    slab = _pack([dmod] + [small[k] for k in small_names])
    (gs,) = _all_gather([slab], "gather_small_grads", pltpu.VMEM)
    dmod_all = gs[:, :_slab_rows(dmod.shape)].reshape(N_DEV, nl, N_MOD * D_MODEL)
    summed = _unpack(_sum_devices(gs), [dmod.shape] + [small[k].shape for k in small_names])
    grads = dict(zip(small_names, summed[1:]))
    grads["b_mod"] = summed[0]
    for k, width in (("lru_conv_w", LRU_W // N_DEV), ("gdn_conv_w", 3 * GDN_W // N_DEV)):
        grads[k] = lax.dynamic_slice_in_dim(grads[k], me * width, width, axis=2)
    names = SMALL_REPLICATED + SMALL_SHARDED
    shapes = [w[k].shape for k in names]
    dl, nm, nv = _adam_flat(_pack([w[k] for k in names]), _pack([grads[k] for k in names]),
                            _pack([mom[k] for k in names]), _pack([var[k] for k in names]))
    delta = dict(zip(names, _unpack(dl, shapes)))
    new_m = dict(zip(names, _unpack(nm, shapes)))
    new_v = dict(zip(names, _unpack(nv, shapes)))

    dmod_cols = jnp.transpose(lax.dynamic_slice_in_dim(dmod_all, me * cols, cols, axis=2), (1, 0, 2))
    grads["w_mod"], delta["w_mod"], new_m["w_mod"], new_v["w_mod"] = _wmod_update(
        c_all, dmod_cols, w_mod, m_w_mod, v_w_mod)

    for k in order:
        grads[k], delta[k], new_m[k], new_v[k] = reducer.outs[k]

    return (loss, grad_x[None], *[grads[k] for k in WEIGHT_ORDER], *[delta[k] for k in WEIGHT_ORDER],
            *[new_m[k] for k in WEIGHT_ORDER], *[new_v[k] for k in WEIGHT_ORDER])
```

```python
import functools

import jax
import jax.numpy as jnp
from jax import lax
from jax.experimental import pallas as pl
from jax.experimental.pallas import tpu as pltpu

F32 = jnp.float32
BF16 = jnp.bfloat16

D_MODEL = 1024
LRU_W = 512
LRU_BLOCKS = 8
LRU_BLOCK = 64
LRU_C = 8.0
GDN_W = 512
HEADS = 4
HEAD_DIM = 128
CHUNK = 64
STACK = HEADS * CHUNK
CONV_K = 4
D_FF = 4096
N_MOD = 6
IN_COLS = 3080
IN_MAIN = 3072
BA_PAD = 128
EPS = 1e-6
N_DEV = 8
HALO = 8
MLP_BLOCKS = 2
PREP_CHUNKS = 2
LANES = 128
ADAM_LR, ADAM_B1, ADAM_B2, ADAM_EPS, ADAM_WD, ADAM_STEP = 0.001, 0.9, 0.999, 1e-08, 0.01, 10
MESH_AXES = ("x", "y", "c")
MESH = pl.DeviceIdType.MESH

NN = (((1,), (0,)), ((), ()))
NT = (((1,), (1,)), ((), ()))
TN = (((0,), (0,)), ((), ()))


def _bdot(a, b, dims=NN):
    return lax.dot_general(a.astype(BF16), b.astype(BF16), dims, preferred_element_type=F32)


def _sdot(a, b, dims=NN):
    ah, bh = a.astype(BF16), b.astype(BF16)
    al, bl = (a - ah.astype(F32)).astype(BF16), (b - bh.astype(F32)).astype(BF16)
    return _bdot(ah, bh, dims) + (_bdot(al, bh, dims) + _bdot(ah, bl, dims))


def _hdot(a, b, dims=NN):
    return lax.dot_general(a, b, dims, precision=lax.Precision.HIGHEST, preferred_element_type=F32)


def _sds(shape, dtype=F32):
    return jax.ShapeDtypeStruct(tuple(shape), dtype)


def _tile(n, t):
    return min(n, t)


def _call(body, name, grid, in_specs, out_specs, out_shape, scratch=(), vmem_mb=48, prefetch=0, aliases=None):
    params = pltpu.CompilerParams(dimension_semantics=("arbitrary",) * len(grid), vmem_limit_bytes=vmem_mb * 2**20)
    if prefetch:
        spec = pltpu.PrefetchScalarGridSpec(num_scalar_prefetch=prefetch, grid=grid, in_specs=in_specs,
                                            out_specs=out_specs, scratch_shapes=list(scratch))
        return pl.pallas_call(body, name=name, grid_spec=spec, out_shape=out_shape, compiler_params=params,
                              input_output_aliases=aliases or {})
    return pl.pallas_call(body, name=name, grid=grid, in_specs=in_specs, out_specs=out_specs, out_shape=out_shape,
                          scratch_shapes=list(scratch), compiler_params=params, input_output_aliases=aliases or {})


def _tok(t, n, col=0):
    return pl.BlockSpec((t, n), lambda i, *_: (i, col))


def _vec(n):
    return pl.BlockSpec((1, n), lambda *_: (0, 0))


def _whole(a):
    nd = a.ndim
    return pl.BlockSpec(a.shape, lambda *_: (0,) * nd)


def _layer(l, *dims):
    return pl.BlockSpec((1,) + dims, lambda *_: (l,) + (0,) * len(dims))


def _gelu(y):
    c0, c1 = 0.7978845608028654, 0.044715
    return 0.5 * y * (1.0 + jnp.tanh(c0 * (y + c1 * y * y * y)))


def _gelu_grad(y):
    c0, c1 = 0.7978845608028654, 0.044715
    t = jnp.tanh(c0 * (y + c1 * y * y * y))
    return 0.5 * (1.0 + t) + 0.5 * y * (1.0 - t * t) * c0 * (1.0 + 3.0 * c1 * y * y)


def _softplus(v):
    return jnp.maximum(v, 0.0) + jnp.log(1.0 + jnp.exp(-jnp.where(v > 0, v, -v)))


@functools.partial(jax.custom_vjp, nondiff_argnums=(1,))
def _roll_rows(v, s):
    s = s % v.shape[0]
    return pltpu.roll(v, s, axis=0) if s else v


def _roll_rows_fwd(v, s):
    return _roll_rows(v, s), None


def _roll_rows_bwd(s, _, g):
    return (_roll_rows(g, -s),)


_roll_rows.defvjp(_roll_rows_fwd, _roll_rows_bwd)


@jax.custom_vjp
def _drop_halo(v):
    return v[HALO:]


def _drop_halo_fwd(v):
    return v[HALO:], None


def _drop_halo_bwd(_, g):
    return (jnp.concatenate([jnp.zeros((HALO, g.shape[1]), g.dtype), g], axis=0),)


_drop_halo.defvjp(_drop_halo_fwd, _drop_halo_bwd)


@functools.partial(jax.custom_vjp, nondiff_argnums=(1, 2))
def _split(v, n, axis):
    w = v.shape[axis] // n
    return tuple(lax.slice_in_dim(v, k * w, (k + 1) * w, axis=axis) for k in range(n))


def _split_fwd(v, n, axis):
    return _split(v, n, axis), None


def _split_bwd(n, axis, _, gs):
    return (jnp.concatenate(list(gs), axis=axis),)


_split.defvjp(_split_fwd, _split_bwd)


def _conv_taps(xw):
    return [_drop_halo(_roll_rows(xw, CONV_K - 1 - k)) for k in range(CONV_K)]


def _modulated_norm(xv, nw, sc, sh):
    r = lax.rsqrt(jnp.mean(xv * xv, axis=-1, keepdims=True) + EPS)
    n = xv * r * nw
    return n * (1.0 + sc) + sh, n, r


def _modulated_norm_bwd(dh, xv, n, r, nw, sc):
    dn = dh * (1.0 + sc)
    dxn = dn * nw
    dx = r * dxn - xv * (r * r * r) * jnp.mean(dxn * xv, axis=-1, keepdims=True)
    return (dx, jnp.sum(dh, axis=0, keepdims=True), jnp.sum(dh * n, axis=0, keepdims=True),
            jnp.sum(dn * xv * r, axis=0, keepdims=True))


def _inproj_fwd(x, nw, sc, sh, win, wba, l):
    s = x.shape[0]
    t = _tile(s, 256)

    def body(x_ref, nw_ref, sc_ref, sh_ref, win_ref, wba_ref, proj_ref, ba_ref):
        h, _, _ = _modulated_norm(x_ref[...], nw_ref[...], sc_ref[...], sh_ref[...])
        hb = h.astype(BF16)
        proj_ref[...] = _bdot(hb, win_ref[0])
        ba_ref[...] = _bdot(hb, wba_ref[0])

    return _call(body, "inproj_fwd", (s // t,),
                 [_tok(t, D_MODEL), _vec(D_MODEL), _vec(D_MODEL), _vec(D_MODEL), _layer(l, D_MODEL, IN_MAIN),
                  _layer(l, D_MODEL, BA_PAD)],
                 [_tok(t, IN_MAIN), _tok(t, BA_PAD)],
                 [_sds((s, IN_MAIN)), _sds((s, BA_PAD))])(x, nw, sc, sh, win, wba)


def _inproj_bwd(dpl, dpq, dpz, dba, x, dx1, nw, sc, sh, win, wba, l):
    s = x.shape[0]
    t = _tile(s, 256)

    def body(dpl_ref, dpq_ref, dpz_ref, dba_ref, x_ref, dx1_ref, nw_ref, sc_ref, sh_ref, win_ref, wba_ref,
             dx_ref, hb_ref, acc_ref):
        @pl.when(pl.program_id(0) == 0)
        def _():
            acc_ref[...] = jnp.zeros_like(acc_ref)

        dh = (_bdot(dpl_ref[...], win_ref[0, :, 0:2 * LRU_W], NT)
              + _bdot(dpq_ref[...], win_ref[0, :, 2 * LRU_W:2 * LRU_W + 3 * GDN_W], NT)
              + _bdot(dpz_ref[...], win_ref[0, :, 2 * LRU_W + 3 * GDN_W:IN_MAIN], NT)
              + _bdot(dba_ref[...], wba_ref[0], NT))
        xv = x_ref[...]
        h, n, r = _modulated_norm(xv, nw_ref[...], sc_ref[...], sh_ref[...])
        hb_ref[...] = h.astype(BF16)
        dx, dsh, dsc, dnw = _modulated_norm_bwd(dh, xv, n, r, nw_ref[...], sc_ref[...])
        dx_ref[...] = dx1_ref[...] + dx
        acc_ref[0:1, :] += dsh
        acc_ref[1:2, :] += dsc
        acc_ref[2:3, :] += dnw

    return _call(body, "inproj_bwd", (s // t,),
                 [_tok(t, 2 * LRU_W), _tok(t, 3 * GDN_W), _tok(t, GDN_W), _tok(t, BA_PAD), _tok(t, D_MODEL),
                  _tok(t, D_MODEL), _vec(D_MODEL), _vec(D_MODEL), _vec(D_MODEL), _layer(l, D_MODEL, IN_MAIN),
                  _layer(l, D_MODEL, BA_PAD)],
                 [_tok(t, D_MODEL), _tok(t, D_MODEL), pl.BlockSpec((8, D_MODEL), lambda i: (0, 0))],
                 [_sds((s, D_MODEL)), _sds((s, D_MODEL), BF16), _sds((8, D_MODEL))])(
                     dpl, dpq, dpz, dba, x, dx1, nw, sc, sh, win, wba)


def _lru_gates(xw, cw_rows, cb, wa, wx, gab, gxb, lam):
    taps = _conv_taps(xw)
    xr = cb + cw_rows[0] * taps[0] + cw_rows[1] * taps[1] + cw_rows[2] * taps[2] + cw_rows[3] * taps[3]
    xb = xr.astype(BF16)
    r = jax.nn.sigmoid(_bdot(xb, wa) + gab)
    i = jax.nn.sigmoid(_bdot(xb, wx) + gxb)
    z = jnp.exp(-jnp.where(lam > 0, lam, -lam))
    w1 = 1.0 + z
    log1p_z = jnp.where(w1 == 1.0, z, jnp.log(w1) * z / (w1 - 1.0))
    ls = jnp.minimum(lam, 0.0) - log1p_z
    la = LRU_C * r * ls
    a = jnp.exp(la)
    x2 = 2.0 * la
    u = jnp.exp(x2)
    mm_raw = jnp.where(u == 1.0, -x2,
                       jnp.where(x2 < -30.0, 1.0, (1.0 - u) * x2 / jnp.log(jnp.maximum(u, 1e-30))))
    mult = jnp.sqrt(jnp.maximum(mm_raw, 1e-12))
    return dict(taps=taps, xr=xr, r=r, i=i, ls=ls, a=a, mm_raw=mm_raw, mult=mult)


def _lru_specs(s, t, tile_of):
    nh = t // HALO
    xl = pl.BlockSpec((t, LRU_W), lambda i: (tile_of(i), 0))
    yl = pl.BlockSpec((t, LRU_W), lambda i: (tile_of(i), 1))
    hx = pl.BlockSpec((HALO, LRU_W), lambda i: (jnp.maximum(tile_of(i) * nh - 1, 0), 0))
    return xl, yl, hx


def _lru_fwd(proj, cw, cb, wa, wx, gab, gxb, lam, lnw):
    s = proj.shape[0]
    t = _tile(s, 256)
    xl, yl, hx = _lru_specs(s, t, lambda i: i)

    def body(xl_ref, yl_ref, hx_ref, cw_ref, cb_ref, wa_ref, wx_ref, gab_ref, gxb_ref, lam_ref, lnw_ref,
             out_ref, h_ref, a_s, b_s, hc):
        i = pl.program_id(0)

        @pl.when(i == 0)
        def _():
            hc[...] = jnp.zeros_like(hc)

        halo = jnp.where(i > 0, hx_ref[...], 0.0)
        xw = jnp.concatenate([halo, xl_ref[...]], axis=0)
        g = _lru_gates(xw, [cw_ref[k:k + 1, :] for k in range(CONV_K)], cb_ref[...], wa_ref[...], wx_ref[...],
                       gab_ref[...], gxb_ref[...], lam_ref[...])
        a_s[...] = g["a"]
        b_s[...] = g["mult"] * (g["i"] * g["xr"])

        def step(k, h):
            h = a_s[pl.ds(k, 1), :] * h + b_s[pl.ds(k, 1), :]
            h_ref[pl.ds(k, 1), :] = h
            return h

        hc[...] = lax.fori_loop(0, t, step, hc[...], unroll=8)
        m = h_ref[...] * _gelu(yl_ref[...])
        out_ref[...] = m * lax.rsqrt(jnp.mean(m * m, axis=-1, keepdims=True) + EPS) * lnw_ref[...]

    return _call(body, "lru_fwd", (s // t,),
                 [xl, yl, hx, _whole(cw), _vec(LRU_W), _whole(wa), _whole(wx)] + [_vec(LRU_W)] * 4,
                 [_tok(t, LRU_W), _tok(t, LRU_W)],
                 [_sds((s, LRU_W)), _sds((s, LRU_W))],
                 scratch=[pltpu.VMEM((t, LRU_W), F32), pltpu.VMEM((t, LRU_W), F32), pltpu.VMEM((1, LRU_W), F32)])(
                     proj, proj, proj, cw, cb, wa, wx, gab, gxb, lam, lnw)


def _lru_bwd(dout, proj, hs, cw, cb, wa, wx, gab, gxb, lam, lnw):
    s = proj.shape[0]
    t = _tile(s, 256)
    nt = s // t
    rev = lambda i: nt - 1 - i
    xl, yl, hx = _lru_specs(s, t, rev)
    nh = t // HALO
    tk = pl.BlockSpec((t, LRU_W), lambda i: (rev(i), 0))
    hh = pl.BlockSpec((HALO, LRU_W), lambda i: (jnp.maximum(rev(i) * nh - 1, 0), 0))

    def body(do_ref, xl_ref, yl_ref, hx_ref, h_ref, hh_ref, cw_ref, cb_ref, wa_ref, wx_ref, gab_ref, gxb_ref,
             lam_ref, lnw_ref, dp_ref, dwa_ref, dwx_ref, rows_ref, dh_s, dhd_s, carry, dxr_next):
        i = pl.program_id(0)
        first_tile = rev(i) == 0

        @pl.when(i == 0)
        def _():
            carry[...] = jnp.zeros_like(carry)
            dxr_next[...] = jnp.zeros_like(dxr_next)
            dwa_ref[...] = jnp.zeros_like(dwa_ref)
            dwx_ref[...] = jnp.zeros_like(dwx_ref)
            rows_ref[...] = jnp.zeros_like(rows_ref)

        halo = jnp.where(first_tile, 0.0, hx_ref[...])
        xw = jnp.concatenate([halo, xl_ref[...]], axis=0)
        cw_rows = [cw_ref[k:k + 1, :] for k in range(CONV_K)]
        lam_v = lam_ref[...]
        g = _lru_gates(xw, cw_rows, cb_ref[...], wa_ref[...], wx_ref[...], gab_ref[...], gxb_ref[...], lam_v)
        a, r, gi, xr, mult = g["a"], g["r"], g["i"], g["xr"], g["mult"]
        hv = h_ref[...]
        yv = yl_ref[...]
        gl = _gelu(yv)
        m = hv * gl
        rn = lax.rsqrt(jnp.mean(m * m, axis=-1, keepdims=True) + EPS)
        dov = do_ref[...]
        dmn = dov * lnw_ref[...]
        rows_ref[4:5, :] += jnp.sum(dov * m * rn, axis=0, keepdims=True)
        dm = rn * dmn - m * (rn * rn * rn) * jnp.mean(dmn * m, axis=-1, keepdims=True)
        dhd_s[...] = dm * gl
        dy = dm * hv * _gelu_grad(yv)
        dh_s[...] = a

        def step(k, c):
            row = t - 1 - k
            d = dhd_s[pl.ds(row, 1), :] + c
            c = dh_s[pl.ds(row, 1), :] * d
            dh_s[pl.ds(row, 1), :] = d
            return c

        carry[...] = lax.fori_loop(0, t, step, carry[...], unroll=8)
        dH = dh_s[...]
        hprev_halo = jnp.where(first_tile, 0.0, hh_ref[...])
        hprev = _drop_halo(_roll_rows(jnp.concatenate([hprev_halo, hv], axis=0), 1))
        da = dH * hprev
        dmult = dH * gi * xr
        di = dH * mult * xr
        dxr = dH * mult * gi
        dla = jnp.where(g["mm_raw"] > 1e-12, dmult * (0.5 / mult) * (-2.0 * a * a), 0.0) + da * a
        dr = dla * (LRU_C * g["ls"])
        sig_neg = jax.nn.sigmoid(-lam_v)
        rows_ref[3:4, :] += jnp.sum(dla * (LRU_C * r), axis=0, keepdims=True) * sig_neg
        drp = dr * r * (1.0 - r)
        dip = di * gi * (1.0 - gi)
        rows_ref[1:2, :] += jnp.sum(drp, axis=0, keepdims=True)
        rows_ref[2:3, :] += jnp.sum(dip, axis=0, keepdims=True)
        xb = xr.astype(BF16)
        drb = drp.astype(BF16)
        dib = dip.astype(BF16)
        dwa_ref[...] += _bdot(xb, drb, TN)
        dwx_ref[...] += _bdot(xb, dib, TN)
        dxr = dxr + _bdot(drb, wa_ref[...], NT) + _bdot(dib, wx_ref[...], NT)
        rows_ref[0:1, :] += jnp.sum(dxr, axis=0, keepdims=True)
        ext = jnp.concatenate([dxr, dxr_next[...]], axis=0)
        dx = cw_rows[CONV_K - 1] * dxr
        for k in range(CONV_K - 1):
            dx = dx + cw_rows[k] * _roll_rows(ext, -(CONV_K - 1 - k))[0:t]
        for k in range(CONV_K):
            rows_ref[8 + k:9 + k, :] += jnp.sum(dxr * g["taps"][k], axis=0, keepdims=True)
        dxr_next[...] = dxr[0:HALO]
        dp_ref[...] = jnp.concatenate([dx, dy], axis=1).astype(BF16)

    acc = lambda shape: pl.BlockSpec(shape, lambda i: (0, 0))
    return _call(body, "lru_bwd", (nt,),
                 [tk, xl, yl, hx, tk, hh, _whole(cw), _vec(LRU_W), _whole(wa), _whole(wx)] + [_vec(LRU_W)] * 4,
                 [pl.BlockSpec((t, 2 * LRU_W), lambda i: (rev(i), 0)), acc((LRU_W, LRU_W)), acc((LRU_W, LRU_W)),
                  acc((16, LRU_W))],
                 [_sds((s, 2 * LRU_W), BF16), _sds((LRU_W, LRU_W)), _sds((LRU_W, LRU_W)), _sds((16, LRU_W))],
                 scratch=[pltpu.VMEM((t, LRU_W), F32), pltpu.VMEM((t, LRU_W), F32), pltpu.VMEM((1, LRU_W), F32),
                          pltpu.VMEM((HALO, LRU_W), F32)])(
                     dout, proj, proj, proj, hs, hs, cw, cb, wa, wx, gab, gxb, lam, lnw)


def _gdn_masks():
    row = lax.broadcasted_iota(jnp.int32, (STACK, STACK), 0)
    col = lax.broadcasted_iota(jnp.int32, (STACK, STACK), 1)
    same = (row // CHUNK) == (col // CHUNK)
    return jnp.stack([(same & (col <= row)).astype(F32), (same & (col < row)).astype(F32), (row == col).astype(F32)])


def _conv_silu(xw, rows):
    taps = _conv_taps(xw)
    y = rows[0] * taps[0] + rows[1] * taps[1] + rows[2] * taps[2] + rows[3] * taps[3]
    return y * jax.nn.sigmoid(y)


def _split3(v):
    hi = v.astype(BF16)
    r1 = v - hi.astype(F32)
    mid = r1.astype(BF16)
    return hi, mid, (r1 - mid.astype(F32)).astype(BF16)


def _mask_dot_raw(mask, v, dims):
    parts = _split3(v)
    d = lambda p: lax.dot_general(mask, p, dims, preferred_element_type=F32)
    return d(parts[0]) + (d(parts[1]) + d(parts[2]))


@jax.custom_vjp
def _mask_dot(mask, v):
    return _mask_dot_raw(mask, v, NN)


def _mask_dot_fwd(mask, v):
    return _mask_dot_raw(mask, v, NN), mask


def _mask_dot_bwd(mask, ct):
    return jnp.zeros_like(mask), _mask_dot_raw(mask, ct, TN)


_mask_dot.defvjp(_mask_dot_fwd, _mask_dot_bwd)


def _unit_lower_inverse(n, eye):
    tinv = eye + n
    p = n
    for _ in range(5):
        p = _bdot(p, p)
        tinv = tinv + _bdot(tinv, p)
    return tinv.astype(BF16)


@jax.custom_vjp
def _unit_lower_solve(n, rhs, tinv):
    x0 = _bdot(tinv, rhs)
    return x0 + _bdot(tinv, rhs - x0 + _sdot(n, x0))


def _unit_lower_solve_fwd(n, rhs, tinv):
    x = _unit_lower_solve(n, rhs, tinv)
    return x, (n, tinv, x)


def _unit_lower_solve_bwd(res, ct):
    n, tinv, x = res
    y0 = _bdot(tinv, ct, TN)
    y = y0 + _bdot(tinv, ct - y0 + _sdot(n, y0, TN), TN)
    return _bdot(y, x, NT), y, jnp.zeros_like(tinv)


_unit_lower_solve.defvjp(_unit_lower_solve_fwd, _unit_lower_solve_bwd)


def _gdn_prep(xq, xk, xv, ba, cwq, cwk, cwv, pa, pd, masks, tinv=None, with_inverse=False):
    lower, strict, eye = masks[0], masks[1], masks[2]
    lower_b = lower.astype(BF16)
    lane = lax.broadcasted_iota(jnp.int32, (CHUNK, LANES), 1)
    q = jnp.concatenate(_split(_conv_silu(xq, cwq), HEADS, 1), axis=0)
    k = jnp.concatenate(_split(_conv_silu(xk, cwk), HEADS, 1), axis=0)
    v = jnp.concatenate(_split(_conv_silu(xv, cwv), HEADS, 1), axis=0)
    qn = q * lax.rsqrt(jnp.sum(q * q, axis=-1, keepdims=True) + 1e-6) * (HEAD_DIM ** -0.5)
    kn = k * lax.rsqrt(jnp.sum(k * k, axis=-1, keepdims=True) + 1e-6)
    beta_f = jax.nn.sigmoid(ba)
    g_f = -jnp.exp(pa) * _softplus(ba + pd)

    def col(a, j):
        return jnp.broadcast_to(jnp.sum(jnp.where(lane == j, a, 0.0), axis=1, keepdims=True), (CHUNK, HEAD_DIM))

    beta = jnp.concatenate([col(beta_f, h) for h in range(HEADS)], axis=0)
    gs = [col(g_f, HEADS + h) for h in range(HEADS)]
    g = jnp.concatenate(gs, axis=0)
    gl = jnp.concatenate([jnp.broadcast_to(jnp.sum(gh, axis=0, keepdims=True), (CHUNK, HEAD_DIM)) for gh in gs], axis=0)
    gc = _mask_dot(lower_b, g)
    gc_rows = jnp.transpose(gc)
    decay = jnp.exp((jnp.concatenate([gc, gc], axis=1) - jnp.concatenate([gc_rows, gc_rows], axis=0)) * lower)
    egc = jnp.exp(gc)
    kb = kn * beta
    n = -(_bdot(kb, kn, NT) * decay * strict)
    if tinv is None:
        tinv = _unit_lower_inverse(lax.stop_gradient(n), eye)
    u, w = _split(_unit_lower_solve(n, jnp.concatenate([v * beta, kb * egc], axis=1), tinv), 2, 1)
    attn = _bdot(qn, kn, NT) * decay * lower
    outs = (u, w, qn * egc, kn * jnp.exp(gl - gc), attn, jnp.exp(gl))
    return outs + (tinv,) if with_inverse else outs


def _gdn_scan(states, u, w, qd, kt, attn, egl, z, nw):
    us, ws, qds, kts, egls = (_split(a, HEADS, 0) for a in (u, w, qd, kt, egl))
    vn = [us[h] - _bdot(ws[h], states[h]) for h in range(HEADS)]
    o = jnp.concatenate([_bdot(qds[h], states[h]) for h in range(HEADS)], axis=0)
    o = o + _bdot(attn, jnp.concatenate(vn, axis=0))
    new = [states[h] * jnp.concatenate([egls[h], egls[h]], axis=0) + _bdot(kts[h], vn[h], TN) for h in range(HEADS)]
    on = o * lax.rsqrt(jnp.mean(o * o, axis=-1, keepdims=True) + EPS) * nw
    return new, on * (z * jax.nn.sigmoid(z))


def _gdn_in_specs(step_of, chunks):
    nh = chunks * CHUNK // HALO
    main = [pl.BlockSpec((chunks * CHUNK, GDN_W), functools.partial(lambda col, i: (step_of(i), col), col))
            for col in (2, 3, 4)]
    halo = [pl.BlockSpec((HALO, GDN_W), functools.partial(lambda col, i: (jnp.maximum(step_of(i) * nh - 1, 0), col),
                                                         col)) for col in (2, 3, 4)]
    return main, halo


def _stk(width, step_of, chunks=1):
    return pl.BlockSpec((chunks * STACK, width), lambda i: (step_of(i), 0))


def _chunk_inputs(main_refs, halo_refs, k, first_step):
    rows = slice(k * CHUNK, (k + 1) * CHUNK)
    if k == 0:
        halos = [jnp.where(first_step, 0.0, h[...]) for h in halo_refs]
    else:
        halos = [m[k * CHUNK - HALO:k * CHUNK, :] for m in main_refs]
    return [jnp.concatenate([h, m[rows, :]], axis=0) for h, m in zip(halos, main_refs)]


def _gdn_prep_fwd(proj, ba, cw, pa, pd, masks):
    s = proj.shape[0]
    nc = s // CHUNK
    per = min(PREP_CHUNKS, nc)
    main, halo = _gdn_in_specs(lambda i: i, per)

    def body(xq_ref, xk_ref, xv_ref, hq_ref, hk_ref, hv_ref, ba_ref, cw_ref, pa_ref, pd_ref, mk_ref, *out_refs):
        first_step = pl.program_id(0) == 0
        rows = [[cw_ref[k:k + 1, j * GDN_W:(j + 1) * GDN_W] for k in range(CONV_K)] for j in range(3)]
        cst = [mk_ref[0], mk_ref[1], mk_ref[2]]
        for k in range(per):
            xs = _chunk_inputs((xq_ref, xk_ref, xv_ref), (hq_ref, hk_ref, hv_ref), k, first_step)
            outs = _gdn_prep(xs[0], xs[1], xs[2], ba_ref[k * CHUNK:(k + 1) * CHUNK, :], rows[0], rows[1], rows[2],
                             pa_ref[...], pd_ref[...], cst, with_inverse=True)
            for ref, val in zip(out_refs, outs):
                ref[k * STACK:(k + 1) * STACK, :] = val.astype(ref.dtype)

    ident = lambda i: i
    stacked = lambda dt: _sds((nc * STACK, HEAD_DIM), dt)
    wide, thin = _stk(STACK, ident, per), _stk(HEAD_DIM, ident, per)
    return _call(body, "gdn_prep_fwd", (nc // per,),
                 main + halo + [_tok(per * CHUNK, BA_PAD), _whole(cw), _vec(BA_PAD), _vec(BA_PAD), _whole(masks)],
                 [thin] * 4 + [wide, thin, wide],
                 [stacked(F32), stacked(BF16), stacked(BF16), stacked(BF16), _sds((nc * STACK, STACK), BF16),
                  stacked(F32), _sds((nc * STACK, STACK), BF16)])(
                     proj, proj, proj, proj, proj, proj, ba, cw, pa, pd, masks)


def _gdn_prep_bwd(cts, tinv, proj, ba, cw, pa, pd, masks):
    s = proj.shape[0]
    nc = s // CHUNK
    per = min(PREP_CHUNKS, nc)
    steps = nc // per
    rev = lambda i: steps - 1 - i
    main, halo = _gdn_in_specs(rev, per)

    def body(du_ref, dw_ref, dqd_ref, dkt_ref, dattn_ref, degl_ref, tinv_ref, xq_ref, xk_ref, xv_ref, hq_ref, hk_ref,
             hv_ref, ba_ref, cw_ref, pa_ref, pd_ref, mk_ref, dp_ref, dba_ref, dcw_ref, dpar_ref, carry):
        i = pl.program_id(0)
        first_step = rev(i) == 0

        @pl.when(i == 0)
        def _():
            carry[...] = jnp.zeros_like(carry)
            dcw_ref[...] = jnp.zeros_like(dcw_ref)
            dpar_ref[...] = jnp.zeros_like(dpar_ref)

        rows = [[cw_ref[k:k + 1, j * GDN_W:(j + 1) * GDN_W] for k in range(CONV_K)] for j in range(3)]
        cst = [mk_ref[0], mk_ref[1], mk_ref[2]]
        dxws = []
        for k in range(per):
            xs = _chunk_inputs((xq_ref, xk_ref, xv_ref), (hq_ref, hk_ref, hv_ref), k, first_step)
            stk = slice(k * STACK, (k + 1) * STACK)
            tinv_k = tinv_ref[stk, :]
            fn = lambda xq, xk, xv, b, rq, rk, rv, a, d: _gdn_prep(xq, xk, xv, b, rq, rk, rv, a, d, cst, tinv=tinv_k)
            _, vjp = jax.vjp(fn, xs[0], xs[1], xs[2], ba_ref[k * CHUNK:(k + 1) * CHUNK, :], rows[0], rows[1], rows[2],
                             pa_ref[...], pd_ref[...])
            dxq, dxk, dxv, dba, drq, drk, drv, dpa, dpd = vjp(
                (du_ref[stk, :], dw_ref[stk, :], dqd_ref[stk, :], dkt_ref[stk, :], dattn_ref[stk, :], degl_ref[stk, :]))
            dxws.append(jnp.concatenate([dxq, dxk, dxv], axis=1))
            dba_ref[k * CHUNK:(k + 1) * CHUNK, :] = dba.astype(BF16)
            for j, dr in enumerate((drq, drk, drv)):
                for kk in range(CONV_K):
                    dcw_ref[kk:kk + 1, j * GDN_W:(j + 1) * GDN_W] += dr[kk]
            dpar_ref[0:1, :] += dpa
            dpar_ref[1:2, :] += dpd
        pad = jnp.zeros((CHUNK - HALO, 3 * GDN_W), F32)
        for k in range(per):
            late = carry[...] if k == per - 1 else dxws[k + 1][0:HALO]
            dp_ref[k * CHUNK:(k + 1) * CHUNK, :] = (dxws[k][HALO:] + jnp.concatenate([pad, late], axis=0)).astype(BF16)
        carry[...] = dxws[0][0:HALO]

    acc = lambda shape: pl.BlockSpec(shape, lambda i: (0, 0))
    wide, thin = _stk(STACK, rev, per), _stk(HEAD_DIM, rev, per)
    return _call(body, "gdn_prep_bwd", (steps,),
                 [thin] * 4 + [wide, thin, wide] + main + halo
                 + [pl.BlockSpec((per * CHUNK, BA_PAD), lambda i: (rev(i), 0)), _whole(cw), _vec(BA_PAD), _vec(BA_PAD),
                    _whole(masks)],
                 [pl.BlockSpec((per * CHUNK, 3 * GDN_W), lambda i: (rev(i), 0)),
                  pl.BlockSpec((per * CHUNK, BA_PAD), lambda i: (rev(i), 0)), acc((CONV_K, 3 * GDN_W)),
                  acc((8, BA_PAD))],
                 [_sds((s, 3 * GDN_W), BF16), _sds((s, BA_PAD), BF16), _sds((CONV_K, 3 * GDN_W)), _sds((8, BA_PAD))],
                 scratch=[pltpu.VMEM((HALO, 3 * GDN_W), F32)])(
                     *cts, tinv, proj, proj, proj, proj, proj, proj, ba, cw, pa, pd, masks)


def _stack_heads(v):
    return jnp.concatenate(_split(v, HEADS, 1), axis=0)


def _unstack_heads(v):
    return jnp.concatenate(_split(v, HEADS, 0), axis=1)


def _gdn_scan_fwd(prep, proj, nw):
    s = proj.shape[0]
    nc = s // CHUNK
    ident = lambda i: i

    def body(u_ref, w_ref, qd_ref, kt_ref, attn_ref, egl_ref, z_ref, nw_ref, out_ref, st_ref, state):
        @pl.when(pl.program_id(0) == 0)
        def _():
            state[...] = jnp.zeros_like(state)

        st_ref[...] = state[...]
        states = [state[h * HEAD_DIM:(h + 1) * HEAD_DIM, :] for h in range(HEADS)]
        new, out = _gdn_scan(states, u_ref[...], w_ref[...], qd_ref[...], kt_ref[...], attn_ref[...], egl_ref[...],
                             _stack_heads(z_ref[...]), nw_ref[...])
        for h in range(HEADS):
            state[h * HEAD_DIM:(h + 1) * HEAD_DIM, :] = new[h]
        out_ref[...] = _unstack_heads(out)

    return _call(body, "gdn_scan_fwd", (nc,),
                 [_stk(HEAD_DIM, ident)] * 4 + [_stk(STACK, ident), _stk(HEAD_DIM, ident),
                                                _tok(CHUNK, GDN_W, col=5), _vec(HEAD_DIM)],
                 [_tok(CHUNK, GDN_W), pl.BlockSpec((HEADS * HEAD_DIM, HEAD_DIM), lambda i: (i, 0))],
                 [_sds((s, GDN_W)), _sds((nc * HEADS * HEAD_DIM, HEAD_DIM))],
                 scratch=[pltpu.VMEM((HEADS * HEAD_DIM, HEAD_DIM), F32)])(*prep, proj, nw)


def _gdn_scan_bwd(dout, prep, st, proj, nw):
    s = proj.shape[0]
    nc = s // CHUNK
    rev = lambda i: nc - 1 - i

    def body(do_ref, u_ref, w_ref, qd_ref, kt_ref, attn_ref, egl_ref, st_ref, z_ref, nw_ref,
             du_ref, dw_ref, dqd_ref, dkt_ref, dattn_ref, degl_ref, dz_ref, dnw_ref, dstate):
        @pl.when(pl.program_id(0) == 0)
        def _():
            dstate[...] = jnp.zeros_like(dstate)
            dnw_ref[...] = jnp.zeros_like(dnw_ref)

        states = [st_ref[h * HEAD_DIM:(h + 1) * HEAD_DIM, :] for h in range(HEADS)]
        f32 = lambda ref: ref[...].astype(F32)
        _, vjp = jax.vjp(_gdn_scan, states, u_ref[...], f32(w_ref), f32(qd_ref), f32(kt_ref), f32(attn_ref),
                         egl_ref[...], _stack_heads(z_ref[...]), nw_ref[...])
        dnew = [dstate[h * HEAD_DIM:(h + 1) * HEAD_DIM, :] for h in range(HEADS)]
        dst, du, dw, dqd, dkt, dattn, degl, dz, dnw = vjp((dnew, _stack_heads(do_ref[...])))
        for h in range(HEADS):
            dstate[h * HEAD_DIM:(h + 1) * HEAD_DIM, :] = dst[h]
        for ref, val in zip((du_ref, dw_ref, dqd_ref, dkt_ref, dattn_ref, degl_ref), (du, dw, dqd, dkt, dattn, degl)):
            ref[...] = val
        dz_ref[...] = _unstack_heads(dz).astype(BF16)
        dnw_ref[0:1, :] += dnw

    tokr = lambda n, col=0: pl.BlockSpec((CHUNK, n), lambda i: (rev(i), col))
    return _call(body, "gdn_scan_bwd", (nc,),
                 [tokr(GDN_W)] + [_stk(HEAD_DIM, rev)] * 4 + [_stk(STACK, rev), _stk(HEAD_DIM, rev),
                                                             pl.BlockSpec((HEADS * HEAD_DIM, HEAD_DIM),
                                                                          lambda i: (rev(i), 0)),
                                                             tokr(GDN_W, 5), _vec(HEAD_DIM)],
                 [_stk(HEAD_DIM, rev)] * 4 + [_stk(STACK, rev), _stk(HEAD_DIM, rev), tokr(GDN_W),
                                              pl.BlockSpec((8, HEAD_DIM), lambda i: (0, 0))],
                 [_sds((nc * STACK, HEAD_DIM))] * 4 + [_sds((nc * STACK, STACK)), _sds((nc * STACK, HEAD_DIM)),
                                                       _sds((s, GDN_W), BF16), _sds((8, HEAD_DIM))],
                 scratch=[pltpu.VMEM((HEADS * HEAD_DIM, HEAD_DIM), F32)])(dout, *prep, st, proj, nw)


def _wo_specs(l):
    half = N_DEV // 2
    return [pl.BlockSpec((half, 1, D_MODEL // N_DEV, D_MODEL), functools.partial(lambda k, *_: (k, l, 0, 0), k))
            for k in range(2)]


def _wo_half(ref):
    return ref[:, 0].reshape(ref.shape[0] * ref.shape[2], ref.shape[3])


def _out_mlp_fwd(ol, og, x, wo, g1, nw2, sc2, sh2, g2, wup, wdn, l):
    s = x.shape[0]
    t = _tile(s, 512)
    nj = wup.shape[0] // MLP_BLOCKS
    fc = wup.shape[3]

    def body(ol_ref, og_ref, x_ref, wol_ref, wog_ref, g1_ref, nw_ref, sc_ref, sh_ref, g2_ref, wup_ref, wdn_ref,
             x1_ref, mix_ref, ff_ref, x2_ref, h2_s, acc_s):
        j = pl.program_id(1)

        @pl.when(j == 0)
        def _():
            mix = _bdot(ol_ref[...], _wo_half(wol_ref)) + _bdot(og_ref[...], _wo_half(wog_ref))
            x1 = x_ref[...] + g1_ref[...] * mix
            mix_ref[...] = mix.astype(BF16)
            x1_ref[...] = x1
            h2, _, _ = _modulated_norm(x1, nw_ref[...], sc_ref[...], sh_ref[...])
            h2_s[...] = h2.astype(BF16)
            acc_s[...] = jnp.zeros_like(acc_s)

        part = None
        for b in range(MLP_BLOCKS):
            up = _bdot(h2_s[...], wup_ref[b, 0])
            down = _bdot(jnp.square(jnp.maximum(up, 0.0)), wdn_ref[b, 0])
            part = down if part is None else part + down
        acc_s[...] += part

        @pl.when(j == nj - 1)
        def _():
            ff_ref[...] = acc_s[...].astype(BF16)
            x2_ref[...] = x1_ref[...] + g2_ref[...] * acc_s[...]

    tk = lambda n: pl.BlockSpec((t, n), lambda i, j: (i, 0))
    return _call(body, "out_mlp_fwd", (s // t, nj),
                 [tk(LRU_W), tk(GDN_W), tk(D_MODEL)] + _wo_specs(l) + [_vec(D_MODEL)] * 5
                 + [pl.BlockSpec((MLP_BLOCKS, 1, D_MODEL, fc), lambda i, j: (j, l, 0, 0)),
                    pl.BlockSpec((MLP_BLOCKS, 1, fc, D_MODEL), lambda i, j: (j, l, 0, 0))],
                 [tk(D_MODEL)] * 4,
                 [_sds((s, D_MODEL)), _sds((s, D_MODEL), BF16), _sds((s, D_MODEL), BF16), _sds((s, D_MODEL))],
                 scratch=[pltpu.VMEM((t, D_MODEL), BF16), pltpu.VMEM((t, D_MODEL), F32)])(
                     ol, og, x, wo, wo, g1, nw2, sc2, sh2, g2, wup, wdn)


def _mlp_bwd(dx2, x1, ff, nw2, sc2, sh2, g2, wup, wdn, l):
    s = x1.shape[0]
    t = _tile(s, 512)
    nj = wup.shape[0] // MLP_BLOCKS
    fc = wup.shape[3]

    def body(dx2_ref, x1_ref, ff_ref, nw_ref, sc_ref, sh_ref, g2_ref, wup_ref, wdn_ref,
             act_ref, dup_ref, h2_ref, dff_ref, dx1_ref, rows_ref, dh2_s):
        i, j = pl.program_id(0), pl.program_id(1)

        @pl.when((i == 0) & (j == 0))
        def _():
            rows_ref[...] = jnp.zeros_like(rows_ref)

        @pl.when(j == 0)
        def _():
            h2, _, _ = _modulated_norm(x1_ref[...], nw_ref[...], sc_ref[...], sh_ref[...])
            h2_ref[...] = h2.astype(BF16)
            dx2 = dx2_ref[...]
            dff_ref[...] = (dx2 * g2_ref[...]).astype(BF16)
            rows_ref[2:3, :] += jnp.sum(dx2 * ff_ref[...].astype(F32), axis=0, keepdims=True)
            dh2_s[...] = jnp.zeros_like(dh2_s)

        part = None
        for b in range(MLP_BLOCKS):
            cols = slice(b * fc, (b + 1) * fc)
            up = _bdot(h2_ref[...], wup_ref[b, 0])
            ru = jnp.maximum(up, 0.0)
            act_ref[:, cols] = (ru * ru).astype(BF16)
            dup = (_bdot(dff_ref[...], wdn_ref[b, 0], NT) * (2.0 * ru)).astype(BF16)
            dup_ref[:, cols] = dup
            back = _bdot(dup, wup_ref[b, 0], NT)
            part = back if part is None else part + back
        dh2_s[...] += part

        @pl.when(j == nj - 1)
        def _():
            xv = x1_ref[...]
            _, n, r = _modulated_norm(xv, nw_ref[...], sc_ref[...], sh_ref[...])
            dx, dsh, dsc, dnw = _modulated_norm_bwd(dh2_s[...], xv, n, r, nw_ref[...], sc_ref[...])
            dx1_ref[...] = dx2_ref[...] + dx
            rows_ref[0:1, :] += dsh
            rows_ref[1:2, :] += dsc
            rows_ref[3:4, :] += dnw

    tk = lambda n: pl.BlockSpec((t, n), lambda i, j: (i, 0))
    tj = pl.BlockSpec((t, MLP_BLOCKS * fc), lambda i, j: (i, j))
    return _call(body, "mlp_bwd", (s // t, nj),
                 [tk(D_MODEL)] * 3 + [_vec(D_MODEL)] * 4
                 + [pl.BlockSpec((MLP_BLOCKS, 1, D_MODEL, fc), lambda i, j: (j, l, 0, 0)),
                    pl.BlockSpec((MLP_BLOCKS, 1, fc, D_MODEL), lambda i, j: (j, l, 0, 0))],
                 [tj, tj, tk(D_MODEL), tk(D_MODEL), tk(D_MODEL), pl.BlockSpec((8, D_MODEL), lambda i, j: (0, 0))],
                 [_sds((s, D_FF), BF16), _sds((s, D_FF), BF16), _sds((s, D_MODEL), BF16),
                  _sds((s, D_MODEL), BF16), _sds((s, D_MODEL)), _sds((8, D_MODEL))],
                 scratch=[pltpu.VMEM((t, D_MODEL), F32)])(dx2, x1, ff, nw2, sc2, sh2, g2, wup, wdn)


def _outproj_bwd(dx1, mix, g1, wo, l):
    s = dx1.shape[0]
    t = _tile(s, 512)

    def body(dx1_ref, mix_ref, g1_ref, wol_ref, wog_ref, dmix_ref, dol_ref, dog_ref, rows_ref):
        @pl.when(pl.program_id(0) == 0)
        def _():
            rows_ref[...] = jnp.zeros_like(rows_ref)

        dx1v = dx1_ref[...]
        rows_ref[0:1, :] += jnp.sum(dx1v * mix_ref[...].astype(F32), axis=0, keepdims=True)
        dmix = (dx1v * g1_ref[...]).astype(BF16)
        dmix_ref[...] = dmix
        dol_ref[...] = _bdot(dmix, _wo_half(wol_ref), NT)
        dog_ref[...] = _bdot(dmix, _wo_half(wog_ref), NT)

    return _call(body, "outproj_bwd", (s // t,),
                 [_tok(t, D_MODEL), _tok(t, D_MODEL), _vec(D_MODEL)] + _wo_specs(l),
                 [_tok(t, D_MODEL), _tok(t, LRU_W), _tok(t, GDN_W), pl.BlockSpec((8, D_MODEL), lambda i: (0, 0))],
                 [_sds((s, D_MODEL), BF16), _sds((s, LRU_W)), _sds((s, GDN_W)), _sds((8, D_MODEL))])(dx1, mix, g1, wo, wo)


def _tn_matmul(a, b, name, out=None, l=0, blocked=False, row_block=0):
    s, m = a.shape
    n = b.shape[1]
    ts, bm = _tile(s, 2048), _tile(m, 1024)
    bn = next(w for w in (512, 640, 384, 256, 128) if n % w == 0)

    def body(a_ref, b_ref, *rest):
        o_ref = rest[-1]

        @pl.when(pl.program_id(2) == 0)
        def _():
            o_ref[...] = jnp.zeros_like(o_ref)

        acc = _bdot(a_ref[...], b_ref[...], TN)
        o_ref[...] += acc.reshape(o_ref.shape)

    in_specs = [pl.BlockSpec((ts, bm), lambda i, j, k: (k, i)), pl.BlockSpec((ts, bn), lambda i, j, k: (k, j))]
    grid = (m // bm, n // bn, s // ts)
    if out is None:
        return _call(body, name, grid, in_specs, pl.BlockSpec((bm, bn), lambda i, j, k: (i, j)), _sds((m, n)))(a, b)
    if blocked:
        out_spec = pl.BlockSpec((1, 1, bm, bn), lambda i, j, k: (l, j, i, 0))
    else:
        out_spec = pl.BlockSpec((1, bm, bn), lambda i, j, k: (l, i + row_block * (m // bm), j))
    return _call(body, name, grid, in_specs + [pl.BlockSpec(memory_space=pl.ANY)], out_spec,
                 _sds(out.shape), aliases={2: 0})(a, b, out)


def _final_fwd_bwd(x, target, fw):
    s = x.shape[0]
    t = _tile(s, 512)

    def body(x_ref, tg_ref, fw_ref, dx_ref, rows_ref):
        @pl.when(pl.program_id(0) == 0)
        def _():
            rows_ref[...] = jnp.zeros_like(rows_ref)

        xv = x_ref[...]
        fwv = fw_ref[...]
        r = lax.rsqrt(jnp.mean(xv * xv, axis=-1, keepdims=True) + EPS)
        err = xv * r * fwv - tg_ref[...]
        part = 0.5 * jnp.sum(jnp.mean(err * err, axis=-1, keepdims=True), axis=0, keepdims=True)
        rows_ref[1:2, :] += jnp.broadcast_to(part, (1, D_MODEL))
        dy = err * (1.0 / D_MODEL)
        rows_ref[0:1, :] += jnp.sum(dy * xv * r, axis=0, keepdims=True)
        dxn = dy * fwv
        dx_ref[...] = r * dxn - xv * (r * r * r) * jnp.mean(dxn * xv, axis=-1, keepdims=True)

    return _call(body, "final_fwd_bwd", (s // t,),
                 [_tok(t, D_MODEL), _tok(t, D_MODEL), _vec(D_MODEL)],
                 [_tok(t, D_MODEL), pl.BlockSpec((8, D_MODEL), lambda i: (0, 0))],
                 [_sds((s, D_MODEL)), _sds((8, D_MODEL))])(x, target, fw)


def _adamw(w, g, m, v):
    m = ADAM_B1 * m + (1.0 - ADAM_B1) * g
    v = ADAM_B2 * v + (1.0 - ADAM_B2) * (g * g)
    m_hat = m / (1.0 - ADAM_B1 ** ADAM_STEP)
    v_hat = v / (1.0 - ADAM_B2 ** ADAM_STEP)
    return -ADAM_LR * (m_hat / (jnp.sqrt(v_hat) + ADAM_EPS) + ADAM_WD * w), m, v


def _mod_local(c_all, wmod, bmod_cols):
    nl, _, cols = wmod.shape

    def body(c_ref, w_ref, b_ref, o_ref):
        cv = c_ref[...]
        o_ref[0] = _bdot(cv * jax.nn.sigmoid(cv), w_ref[0]) + b_ref[0]

    return _call(body, "mod_local", (nl,),
                 [_whole(c_all), pl.BlockSpec((1, D_MODEL, cols), lambda l: (l, 0, 0)),
                  pl.BlockSpec((1, 1, cols), lambda l: (l, 0, 0))],
                 pl.BlockSpec((1, N_DEV, cols), lambda l: (l, 0, 0)), _sds((nl, N_DEV, cols)))(c_all, wmod, bmod_cols)


def _wmod_update(c_all, dmod_cols, w, m, v):
    nl, _, cols = w.shape

    def body(c_ref, d_ref, w_ref, m_ref, v_ref, g_ref, dl_ref, nm_ref, nv_ref):
        cv = c_ref[...]
        g = _bdot(cv * jax.nn.sigmoid(cv), d_ref[0], TN)
        g_ref[0] = g
        dl_ref[0], nm_ref[0], nv_ref[0] = _adamw(w_ref[0], g, m_ref[0], v_ref[0])

    wspec = pl.BlockSpec((1, D_MODEL, cols), lambda l: (l, 0, 0))
    return _call(body, "wmod_update", (nl,),
                 [_whole(c_all), pl.BlockSpec((1, N_DEV, cols), lambda l: (l, 0, 0)), wspec, wspec, wspec],
                 [wspec] * 4, [_sds(w.shape)] * 4)(c_all, dmod_cols, w, m, v)


def _sum_devices(gathered):
    _, r, _ = gathered.shape

    def body(g_ref, o_ref):
        acc = g_ref[0]
        for d in range(1, N_DEV):
            acc = acc + g_ref[d]
        o_ref[...] = acc

    return _call(body, "sum_devices", (1,), [_whole(gathered)], pl.BlockSpec((r, LANES), lambda i: (0, 0)),
                 _sds((r, LANES)))(gathered)


def _adam_flat(w, g, m, v):
    r = w.shape[0]

    def body(w_ref, g_ref, m_ref, v_ref, dl_ref, nm_ref, nv_ref):
        dl_ref[...], nm_ref[...], nv_ref[...] = _adamw(w_ref[...], g_ref[...], m_ref[...], v_ref[...])

    spec = pl.BlockSpec((r, LANES), lambda i: (0, 0))
    return _call(body, "adam_small", (1,), [spec] * 4, [spec] * 3, [_sds((r, LANES))] * 3)(w, g, m, v)


def _pair_add(x, p, core):
    _, r, c = x.shape
    tr = _tile(r, 128 if c > 512 else 256)

    def body(core_ref, x_ref, p_ref, o_ref):
        o_ref[...] = (x_ref[...] + p_ref[...]).astype(BF16)

    return _call(body, "pair_add", (4, r // tr),
                 [pl.BlockSpec((1, tr, c), lambda q, i, core_ref: (2 * q + core_ref[0], i, 0)),
                  pl.BlockSpec((1, tr, c), lambda q, i, core_ref: (q, i, 0))],
                 pl.BlockSpec((1, tr, c), lambda q, i, core_ref: (q, i, 0)), _sds((4, r, c), BF16),
                 prefetch=1)(core, x, p)


def _reduce_adam(x, p, q, place, w, m, v, l, outs):
    _, r, c = x.shape
    tr = _tile(r, 128 if c > 512 else 256)

    def body(place_ref, x_ref, p_ref, q_ref, w_ref, m_ref, v_ref, *rest):
        g_ref, dl_ref, nm_ref, nv_ref = rest[-4:]
        g = (((x_ref[0] + p_ref[0]) + q_ref[0].astype(F32)) + q_ref[1].astype(F32)) + q_ref[2].astype(F32)
        g_ref[0] = g
        dl_ref[0], nm_ref[0], nv_ref[0] = _adamw(w_ref[0], g, m_ref[0], v_ref[0])

    flat = pl.BlockSpec((1, tr, c), lambda i, place_ref: (l, i, 0))
    through = pl.BlockSpec(memory_space=pl.ANY)
    return _call(body, "reduce_adam", (r // tr,),
                 [pl.BlockSpec((1, tr, c), lambda i, place_ref: (place_ref[0], i, 0)),
                  pl.BlockSpec((1, tr, c), lambda i, place_ref: (place_ref[1], i, 0)),
                  pl.BlockSpec((3, tr, c), lambda i, place_ref: (0, i, 0)), flat, flat, flat] + [through] * 4,
                 [flat] * 4, [_sds(w.shape)] * 4, prefetch=1, aliases={7 + k: k for k in range(4)})(
                     place, x, p, q, w, m, v, *outs)


def _place():
    return lax.axis_index("x"), lax.axis_index("y"), lax.axis_index("c")


def _all_gather(xs, name, space):
    n = len(xs)

    def body(*refs):
        x_refs, o_refs = refs[:n], refs[n:2 * n]
        send_sems, recv_sems, local_sems = refs[2 * n:]
        x, y, c = _place()
        me, sibling = (x, y, c), (x, y, 1 - c)
        chips = [(1 - x, y), (x, 1 - y), (1 - x, 1 - y)]

        def blk(a, p):
            return o_refs[a].at[4 * p[0] + 2 * p[1] + p[2]]

        def copy(a, k, block, to, src=None):
            return pltpu.make_async_remote_copy(
                src_ref=blk(a, block) if src is None else src, dst_ref=blk(a, block),
                send_sem=send_sems.at[a, k], recv_sem=recv_sems.at[a, k], device_id=to, device_id_type=MESH)

        mine = [pltpu.make_async_copy(x_refs[a], blk(a, me), local_sems.at[a]) for a in range(n)]
        for cp in mine:
            cp.start()
        first = []
        for a in range(n):
            first.append(copy(a, 0, me, sibling, src=x_refs[a]))
            first += [copy(a, 1 + j, me, (*chip, c), src=x_refs[a]) for j, chip in enumerate(chips)]
        for cp in first:
            cp.start()
        passed = []
        for j, chip in enumerate(chips):
            for a in range(n):
                copy(a, 1 + j, (*chip, c), me).wait_recv()
                cp = copy(a, 4 + j, (*chip, c), sibling)
                cp.start()
                passed.append(cp)
        for a in range(n):
            copy(a, 0, sibling, me).wait_recv()
        for j, chip in enumerate(chips):
            for a in range(n):
                copy(a, 4 + j, (*chip, 1 - c), me).wait_recv()
        for cp in first + passed:
            cp.wait_send()
        for cp in mine:
            cp.wait()

    spec = pl.BlockSpec(memory_space=space)
    return pl.pallas_call(
        body, name=name, out_shape=[_sds((N_DEV,) + a.shape, a.dtype) for a in xs],
        in_specs=[spec] * n, out_specs=[spec] * n,
        scratch_shapes=[pltpu.SemaphoreType.DMA((n, 7)), pltpu.SemaphoreType.DMA((n, 7)),
                        pltpu.SemaphoreType.DMA((n,))])(*xs)


_HBM_SPEC = pl.BlockSpec(memory_space=pltpu.HBM)
_SEM_SPEC = pl.BlockSpec(memory_space=pltpu.SEMAPHORE)
_EFFECT = pltpu.SideEffectType.DATAFLOW_SIDE_EFFECTING


def _descriptors(plan, src_refs, land_refs, send_sems, recv_sems, which=None):
    return [pltpu.make_async_remote_copy(src_ref=s, dst_ref=d, send_sem=send_sems.at[k], recv_sem=recv_sems.at[k],
                                         device_id=dev, device_id_type=MESH)
            for k, (s, d, dev) in enumerate(plan(src_refs, land_refs)) if which is None or k in which]


def _split_start(name, plan, n, srcs, lands, after):
    ns, nb = len(srcs), len(srcs) + len(lands)
    after = list(after) if isinstance(after, (list, tuple)) else [after]
    sems = nb + len(after)

    def body(*refs):
        for cp in _descriptors(plan, refs[:ns], refs[ns:nb], refs[sems], refs[sems + 1]):
            cp.start()
        refs[-1][...] = jnp.zeros_like(refs[-1])

    bufs = [pltpu.with_memory_space_constraint(a, pltpu.HBM) for a in list(srcs) + list(lands)]
    outs = pl.pallas_call(
        body, name=name,
        out_shape=(pltpu.SemaphoreType.DMA((n,)), pltpu.SemaphoreType.DMA((n,)))
        + tuple(pltpu.HBM(a.shape, a.dtype) for a in bufs) + (_sds((8, LANES)),),
        in_specs=[_HBM_SPEC] * nb + [pl.BlockSpec(memory_space=pl.ANY)] * len(after),
        out_specs=(_SEM_SPEC, _SEM_SPEC) + (_HBM_SPEC,) * nb + (pl.BlockSpec(memory_space=pltpu.VMEM),),
        input_output_aliases={i: 2 + i for i in range(nb)},
        compiler_params=pltpu.CompilerParams(has_side_effects=_EFFECT))(*bufs, *after)
    return dict(send=outs[0], recv=outs[1], srcs=list(outs[2:2 + ns]), lands=list(outs[2 + ns:2 + nb]), token=outs[-1])


def _split_wait(name, plan, flight, which, after):
    srcs, lands = flight["srcs"], flight["lands"]
    ns, nb = len(srcs), len(srcs) + len(lands)

    def body(*refs):
        for cp in _descriptors(plan, refs[:ns], refs[ns:nb], refs[nb], refs[nb + 1], set(which)):
            cp.wait_send()
            cp.wait_recv()

    outs = pl.pallas_call(
        body, name=name, out_shape=tuple(pltpu.HBM(a.shape, a.dtype) for a in srcs + lands),
        in_specs=[_HBM_SPEC] * nb + [_SEM_SPEC, _SEM_SPEC, pl.BlockSpec(memory_space=pl.ANY)],
        out_specs=(_HBM_SPEC,) * nb, input_output_aliases={i: i for i in range(nb)},
        compiler_params=pltpu.CompilerParams(has_side_effects=_EFFECT))(*srcs, *lands, flight["send"], flight["recv"],
                                                                       after)
    return dict(flight, srcs=list(outs[:ns]), lands=list(outs[ns:nb]))


GATHER_PEERS = N_DEV - 1


def _gather_plan(items):
    def plan(src_refs, land_refs):
        x, y, c = _place()
        me = 4 * x + 2 * y + c
        out = []
        for a, l in items:
            for r in range(1, N_DEV):
                peer = (1 - x if r & 4 else x, 1 - y if r & 2 else y, 1 - c if r & 1 else c)
                out.append((src_refs[a].at[l], land_refs[a].at[me, l], peer))
        return out

    return plan


def _pair_plan(narr):
    def plan(src_refs, land_refs):
        x, y, c = _place()
        return [(src_refs[a].at[2 * q + (1 - c)], land_refs[a].at[q], (x, y, 1 - c))
                for a in range(narr) for q in range(4)]

    return plan


def _chip_plan(narr):
    def plan(src_refs, land_refs):
        x, y, c = _place()
        chips = [(1 - x, y), (x, 1 - y), (1 - x, 1 - y)]
        return [(src_refs[a].at[2 * chip[0] + chip[1]], land_refs[a].at[r], (*chip, c))
                for a in range(narr) for r, chip in enumerate(chips)]

    return plan


class _GradReducer:
    def __init__(self, tag, names, w, mom, var, place, core):
        self.tag, self.names, self.w, self.mom, self.var, self.place, self.core = tag, names, w, mom, var, place, core
        self.outs = {k: [lax.empty(w[k].shape, F32) for _ in range(4)] for k in names}
        self.n = len(names)

    def start(self, l, grads, after):
        self.l, self.xs = l, [grads[k] for k in self.names]
        lands = [lax.empty((4,) + a.shape[1:], F32) for a in self.xs]
        self.pair = _split_start(f"{self.tag}_pair_start{l}", _pair_plan(self.n), 4 * self.n, self.xs, lands, after)
        return self.pair["token"][0, 0]

    def middle(self, after):
        self.pair = _split_wait(f"{self.tag}_pair_wait{self.l}", _pair_plan(self.n), self.pair, range(4 * self.n),
                                after)
        self.xs, self.ps = self.pair["srcs"], self.pair["lands"]
        ys = [_pair_add(x, p, self.core) for x, p in zip(self.xs, self.ps)]
        lands = [lax.empty((3,) + a.shape[1:], BF16) for a in ys]
        self.chip = _split_start(f"{self.tag}_chip_start{self.l}", _chip_plan(self.n), 3 * self.n, ys, lands, ys[-1])
        return self.chip["token"][0, 0]

    def finish(self, after):
        chip = _split_wait(f"{self.tag}_chip_wait{self.l}", _chip_plan(self.n), self.chip, range(3 * self.n), after)
        for k, x, p, q in zip(self.names, self.xs, self.ps, chip["lands"]):
            self.outs[k] = _reduce_adam(x, p, q, self.place, self.w[k], self.mom[k], self.var[k], self.l, self.outs[k])


def _size(shape):
    size = 1
    for d in shape:
        size *= d
    return size


def _slab_rows(shape):
    return -(-_size(shape) // (8 * LANES)) * 8


def _pack(arrs):
    parts = []
    for a in arrs:
        flat = a.reshape(-1).astype(F32)
        parts.append(jnp.pad(flat, (0, _slab_rows(a.shape) * LANES - flat.shape[0])).reshape(-1, LANES))
    return jnp.concatenate(parts, axis=0)


def _unpack(slab, shapes):
    out, off = [], 0
    for shp in shapes:
        rows = _slab_rows(shp)
        out.append(slab[off:off + rows].reshape(-1)[:_size(shp)].reshape(shp))
        off += rows
    return out


def _dense_blocks(w):
    eye = jnp.eye(LRU_BLOCKS, dtype=w.dtype)
    return (eye[:, None, :, None] * w[:, :, None, :]).reshape(LRU_W, LRU_W)


def _diag_blocks(dense):
    return jnp.stack([dense[g * LRU_BLOCK:(g + 1) * LRU_BLOCK, g * LRU_BLOCK:(g + 1) * LRU_BLOCK]
                      for g in range(LRU_BLOCKS)])


def _alpha_lanes(v):
    return jnp.zeros((1, BA_PAD), F32).at[0, HEADS:2 * HEADS].set(v)


def _local_step(x, target, mod, p, fetch, reducers=None):
    nl = mod.shape[0]
    row = lambda v: v.reshape(1, -1)
    masks = _gdn_masks()
    saved = []
    xc = x
    for l in range(nl):
        win, wba, lin = fetch(l, "in", xc)
        mv = [row(mod[l, k * D_MODEL:(k + 1) * D_MODEL]) for k in range(N_MOD)]
        sh1, sc1, g1, sh2, sc2, g2 = mv
        nw1, nw2 = row(p["norm_mix_w"][l]), row(p["norm_mlp_w"][l])
        wa, wx = _dense_blocks(p["lru_gate_a_w"][l]).astype(BF16), _dense_blocks(p["lru_gate_x_w"][l]).astype(BF16)
        lru_args = (p["lru_conv_w"][l], row(p["lru_conv_b"][l]), wa, wx, row(p["lru_gate_a_b"][l]),
                    row(p["lru_gate_x_b"][l]), row(p["lru_lambda"][l]), row(p["lru_norm_w"][l]))
        gdn_args = (p["gdn_conv_w"][l], _alpha_lanes(p["gdn_a_log"][l]), _alpha_lanes(p["gdn_dt_bias"][l]), masks)
        gnw = row(p["gdn_norm_w"][l])
        proj, ba = _inproj_fwd(xc, nw1, sc1, sh1, win, wba, lin)
        ol, hs = _lru_fwd(proj, *lru_args)
        *prep, tinv = _gdn_prep_fwd(proj, ba, *gdn_args)
        og, st = _gdn_scan_fwd(prep, proj, gnw)
        wo, wup, wdn = fetch(l, "rest", og)
        x1, mix, ff, x2 = _out_mlp_fwd(ol, og, xc, wo, g1, nw2, sc2, sh2, g2, wup, wdn, l)
        saved.append(dict(x=xc, mv=mv, nw1=nw1, nw2=nw2, lru_args=lru_args, gdn_args=gdn_args, gnw=gnw, proj=proj,
                          ba=ba, ol=ol, hs=hs, prep=prep, tinv=tinv, og=og, st=st, x1=x1, mix=mix, ff=ff,
                          win=win, wba=wba, lin=lin))
        xc = x2

    dx, frows = _final_fwd_bwd(xc, target, row(p["final_norm_w"]))
    loss_part = frows[1, 0]
    small = {k: [None] * nl for k in ("norm_mix_w", "norm_mlp_w", "lru_conv_w", "lru_conv_b", "lru_gate_a_w",
                                      "lru_gate_a_b", "lru_gate_x_w", "lru_gate_x_b", "lru_lambda", "lru_norm_w",
                                      "gdn_conv_w", "gdn_a_log", "gdn_dt_bias", "gdn_norm_w")}
    fc = D_FF // N_DEV
    big = [None] * nl
    dmod = [None] * nl
    mlp_red, mix_red = reducers or (None, None)
    busy = False
    for l in reversed(range(nl)):
        sv = saved[l]
        sh1, sc1, g1, sh2, sc2, g2 = sv["mv"]
        gnw = sv["gnw"]
        if busy:
            g2 = g2 + started
        act, dup, h2b, dffb, dx1, rows2 = _mlp_bwd(dx, sv["x1"], sv["ff"], sv["nw2"], sc2, sh2, g2, wup, wdn, l)
        if busy:
            g1 = g1 + mix_red.middle(dx1)
        g_up = _tn_matmul(h2b, dup, "grad_w_up", out=lax.empty((1, N_DEV, D_MODEL, fc), F32), blocked=True)[0]
        g_down = _tn_matmul(act, dffb, "grad_w_down", out=lax.empty((1, D_FF, D_MODEL), F32))
        g_down = g_down.reshape(N_DEV, fc, D_MODEL)
        if mlp_red is not None:
            g1 = g1 + mlp_red.start(l, dict(w_up=g_up, w_down=g_down), g_down)
        dmix, dol, dog, rows1 = _outproj_bwd(dx1, sv["mix"], g1, wo, l)
        g_out = _tn_matmul(sv["ol"], dmix, "grad_w_out_lru", out=lax.empty((1, D_MODEL, D_MODEL), F32))
        g_out = _tn_matmul(sv["og"], dmix, "grad_w_out_gdn", out=g_out, row_block=1)
        dpl, dwa, dwx, lrows = _lru_bwd(dol, sv["proj"], sv["hs"], *sv["lru_args"])
        if mlp_red is not None:
            gnw = gnw + mlp_red.middle(dpl)
        *cts, dpz, gnrow = _gdn_scan_bwd(dog, sv["prep"], sv["st"], sv["proj"], gnw)
        dpq, dba, dcw, dpar = _gdn_prep_bwd(cts, sv["tinv"], sv["proj"], sv["ba"], *sv["gdn_args"])
        dx, hb, rows0 = _inproj_bwd(dpl, dpq, dpz, dba, sv["x"], dx1, sv["nw1"], sc1, sh1, sv["win"], sv["wba"],
                                    sv["lin"])
        if busy:
            mix_red.finish(dx)
        if mlp_red is not None:
            mlp_red.finish(dx)
        dproj = jnp.concatenate([dpl, dpq, dpz, dba], axis=1)
        g_in = jnp.transpose(_tn_matmul(hb, dproj, "grad_w_in")[:, :IN_COLS].reshape(
            D_MODEL, N_DEV, IN_COLS // N_DEV), (1, 0, 2))
        big[l] = dict(w_in=g_in, w_out=g_out.reshape(N_DEV, D_MODEL // N_DEV, D_MODEL), w_up=g_up,
                      w_down=g_down)
        if mix_red is not None:
            started, busy = mix_red.start(l, big[l], g_in), True
        dmod[l] = jnp.concatenate([rows0[0], rows0[1], rows1[0], rows2[0], rows2[1], rows2[2]])
        small["norm_mix_w"][l], small["norm_mlp_w"][l] = rows0[2], rows2[3]
        small["lru_conv_w"][l], small["lru_conv_b"][l] = lrows[8:8 + CONV_K], lrows[0]
        small["lru_gate_a_w"][l], small["lru_gate_x_w"][l] = _diag_blocks(dwa), _diag_blocks(dwx)
        small["lru_gate_a_b"][l], small["lru_gate_x_b"][l] = lrows[1], lrows[2]
        small["lru_lambda"][l], small["lru_norm_w"][l] = lrows[3], lrows[4]
        small["gdn_conv_w"][l] = dcw
        small["gdn_a_log"][l], small["gdn_dt_bias"][l] = dpar[0, HEADS:2 * HEADS], dpar[1, HEADS:2 * HEADS]
        small["gdn_norm_w"][l] = gnrow[0]
    if busy:
        mix_red.middle(dx)
        mix_red.finish(dx)
    small = {k: jnp.stack(v) for k, v in small.items()}
    small["final_norm_w"] = frows[0]
    return loss_part, dx, big, small, jnp.stack(dmod)


SMALL_REPLICATED = ("norm_mix_w", "norm_mlp_w", "b_mod", "lru_conv_b", "lru_gate_a_w", "lru_gate_a_b", "lru_gate_x_w",
                    "lru_gate_x_b", "lru_lambda", "lru_norm_w", "gdn_a_log", "gdn_dt_bias", "gdn_norm_w",
                    "final_norm_w")
SMALL_SHARDED = ("lru_conv_w", "gdn_conv_w")
WEIGHT_ORDER = ("norm_mix_w", "norm_mlp_w", "w_mod", "b_mod", "w_in", "lru_conv_w", "lru_conv_b", "lru_gate_a_w",
                "lru_gate_a_b", "lru_gate_x_w", "lru_gate_x_b", "lru_lambda", "lru_norm_w", "gdn_conv_w", "gdn_a_log",
                "gdn_dt_bias", "gdn_norm_w", "w_out", "w_up", "w_down", "final_norm_w")


def kernel(x, c, norm_mix_w, norm_mlp_w, w_mod, b_mod, w_in, lru_conv_w, lru_conv_b, lru_gate_a_w, lru_gate_a_b, lru_gate_x_w, lru_gate_x_b, lru_lambda, lru_norm_w, gdn_conv_w, gdn_a_log, gdn_dt_bias, gdn_norm_w, w_out, w_up, w_down, final_norm_w, loss_target, m_norm_mix_w, m_norm_mlp_w, m_w_mod, m_b_mod, m_w_in, m_lru_conv_w, m_lru_conv_b, m_lru_gate_a_w, m_lru_gate_a_b, m_lru_gate_x_w, m_lru_gate_x_b, m_lru_lambda, m_lru_norm_w, m_gdn_conv_w, m_gdn_a_log, m_gdn_dt_bias, m_gdn_norm_w, m_w_out, m_w_up, m_w_down, m_final_norm_w, v_norm_mix_w, v_norm_mlp_w, v_w_mod, v_b_mod, v_w_in, v_lru_conv_w, v_lru_conv_b, v_lru_gate_a_w, v_lru_gate_a_b, v_lru_gate_x_w, v_lru_gate_x_b, v_lru_lambda, v_lru_norm_w, v_gdn_conv_w, v_gdn_a_log, v_gdn_dt_bias, v_gdn_norm_w, v_w_out, v_w_up, v_w_down, v_final_norm_w):
    args = dict(locals())
    w = {k: args[k] for k in WEIGHT_ORDER}
    mom = {k: args["m_" + k] for k in WEIGHT_ORDER}
    var = {k: args["v_" + k] for k in WEIGHT_ORDER}
    nl = w_in.shape[0]
    px, py, pc = _place()
    me = 4 * px + 2 * py + pc
    core = jnp.reshape(pc, (1,)).astype(jnp.int32)

    shapes0 = [c.shape, lru_conv_w.shape, gdn_conv_w.shape]
    (g0,) = _all_gather([_pack([c, lru_conv_w, gdn_conv_w])], "gather_cond", pltpu.VMEM)
    per_dev = [_unpack(g0[d], shapes0) for d in range(N_DEV)]
    c_all = jnp.concatenate([pd[0] for pd in per_dev], axis=0)
    lru_conv_full = jnp.concatenate([pd[1] for pd in per_dev], axis=-1)
    gdn_conv_full = jnp.concatenate([pd[2] for pd in per_dev], axis=-1)

    cols = w_mod.shape[2]
    bmod_cols = lax.dynamic_slice_in_dim(b_mod, me * cols, cols, axis=1).reshape(nl, 1, cols)
    mod_cols = _mod_local(c_all, w_mod, bmod_cols)
    (g1,) = _all_gather([mod_cols.reshape(nl * N_DEV, cols)], "gather_mod", pltpu.VMEM)
    g1 = g1.reshape(N_DEV, nl, N_DEV, cols)
    mod = jnp.transpose(lax.dynamic_index_in_dim(g1, me, axis=2, keepdims=False), (1, 0, 2)).reshape(nl, N_DEV * cols)

    shards = [a.astype(BF16) for a in (w_in, w_out, w_up, w_down)]
    (first_in,) = _all_gather([shards[0][:1]], "gather_w_in_first", pl.ANY)
    items = [(a, 0) for a in (1, 2, 3)] + [(a, l) for l in range(1, nl) for a in range(4)]
    plan = _gather_plan(items)
    lands = [lax.dynamic_update_slice_in_dim(lax.empty((N_DEV,) + a.shape, BF16), a[None], me, axis=0) for a in shards]
    flight = [_split_start("gather_weights_start", plan, len(items) * GATHER_PEERS, shards, lands, [first_in, mod])]
    mod = mod + flight[0]["token"][0, 0]

    def fetch(l, what, after):
        wanted = [k for k, (a, ll) in enumerate(items) if ll == l and (a == 0) == (what == "in")]
        if wanted:
            flight[0] = _split_wait(f"gather_weights_wait_{what}{l}", plan, flight[0],
                                    [k * GATHER_PEERS + r for k in wanted for r in range(GATHER_PEERS)], after)
        gin, gout, gup, gdn = flight[0]["lands"]
        if what == "rest":
            return gout, gup, gdn
        gin = first_in[:, 0] if l == 0 else gin[:, l]
        win = jnp.transpose(gin, (1, 0, 2)).reshape(1, D_MODEL, IN_COLS)
        wba = jnp.pad(win[:, :, IN_MAIN:], ((0, 0), (0, 0), (0, BA_PAD - (IN_COLS - IN_MAIN))))
        return win, wba, 0

    p = dict(w)
    p["lru_conv_w"], p["gdn_conv_w"] = lru_conv_full, gdn_conv_full

    order = ("w_in", "w_out", "w_up", "w_down")
    place = jnp.stack([me, 2 * px + py]).astype(jnp.int32)
    reducers = (_GradReducer("mlp_grad", ("w_up", "w_down"), w, mom, var, place, core),
                _GradReducer("mix_grad", ("w_in", "w_out"), w, mom, var, place, core))
    loss_part, grad_x, _, small, dmod = _local_step(x[0], loss_target[0], mod, p, fetch, reducers)
    loss = lax.psum(loss_part, MESH_AXES)

    small_names = sorted(small)
    slab = _pack([dmod] + [small[k] for k in small_names])
    (gs,) = _all_gather([slab], "gather_small_grads", pltpu.VMEM)
    dmod_all = gs[:, :_slab_rows(dmod.shape)].reshape(N_DEV, nl, N_MOD * D_MODEL)
    summed = _unpack(_sum_devices(gs), [dmod.shape] + [small[k].shape for k in small_names])
    grads = dict(zip(small_names, summed[1:]))
    grads["b_mod"] = summed[0]
    for k, width in (("lru_conv_w", LRU_W // N_DEV), ("gdn_conv_w", 3 * GDN_W // N_DEV)):
        grads[k] = lax.dynamic_slice_in_dim(grads[k], me * width, width, axis=2)
    names = SMALL_REPLICATED + SMALL_SHARDED
    shapes = [w[k].shape for k in names]
    dl, nm, nv = _adam_flat(_pack([w[k] for k in names]), _pack([grads[k] for k in names]),
                            _pack([mom[k] for k in names]), _pack([var[k] for k in names]))
    delta = dict(zip(names, _unpack(dl, shapes)))
    new_m = dict(zip(names, _unpack(nm, shapes)))
    new_v = dict(zip(names, _unpack(nv, shapes)))

    dmod_cols = jnp.transpose(lax.dynamic_slice_in_dim(dmod_all, me * cols, cols, axis=2), (1, 0, 2))
    grads["w_mod"], delta["w_mod"], new_m["w_mod"], new_v["w_mod"] = _wmod_update(
        c_all, dmod_cols, w_mod, m_w_mod, v_w_mod)

    for red in reducers:
        for k in red.names:
            grads[k], delta[k], new_m[k], new_v[k] = red.outs[k]

    return (loss, grad_x[None], *[grads[k] for k in WEIGHT_ORDER], *[delta[k] for k in WEIGHT_ORDER],
            *[new_m[k] for k in WEIGHT_ORDER], *[new_v[k] for k in WEIGHT_ORDER])
```

```python
import functools

import jax
import jax.numpy as jnp
from jax import lax
from jax.experimental import pallas as pl
from jax.experimental.pallas import tpu as pltpu

F32 = jnp.float32
BF16 = jnp.bfloat16

D_MODEL = 1024
LRU_W = 512
LRU_BLOCKS = 8
LRU_BLOCK = 64
LRU_C = 8.0
GDN_W = 512
HEADS = 4
HEAD_DIM = 128
CHUNK = 64
STACK = HEADS * CHUNK
CONV_K = 4
D_FF = 4096
N_MOD = 6
IN_COLS = 3080
IN_MAIN = 3072
BA_PAD = 128
EPS = 1e-6
N_DEV = 8
HALO = 8
MLP_BLOCKS = 4
PREP_CHUNKS = 2
SCAN_CHUNKS = 2
LANES = 128
ADAM_LR, ADAM_B1, ADAM_B2, ADAM_EPS, ADAM_WD, ADAM_STEP = 0.001, 0.9, 0.999, 1e-08, 0.01, 10
MESH_AXES = ("x", "y", "c")
MESH = pl.DeviceIdType.MESH

NN = (((1,), (0,)), ((), ()))
NT = (((1,), (1,)), ((), ()))
TN = (((0,), (0,)), ((), ()))


def _bdot(a, b, dims=NN):
    return lax.dot_general(a.astype(BF16), b.astype(BF16), dims, preferred_element_type=F32)


def _sdot(a, b, dims=NN):
    ah, bh = a.astype(BF16), b.astype(BF16)
    al, bl = (a - ah.astype(F32)).astype(BF16), (b - bh.astype(F32)).astype(BF16)
    return _bdot(ah, bh, dims) + (_bdot(al, bh, dims) + _bdot(ah, bl, dims))


def _hdot(a, b, dims=NN):
    return lax.dot_general(a, b, dims, precision=lax.Precision.HIGHEST, preferred_element_type=F32)


def _sds(shape, dtype=F32):
    return jax.ShapeDtypeStruct(tuple(shape), dtype)


def _tile(n, t):
    return min(n, t)


def _call(body, name, grid, in_specs, out_specs, out_shape, scratch=(), vmem_mb=48, prefetch=0, aliases=None):
    params = pltpu.CompilerParams(dimension_semantics=("arbitrary",) * len(grid), vmem_limit_bytes=vmem_mb * 2**20)
    if prefetch:
        spec = pltpu.PrefetchScalarGridSpec(num_scalar_prefetch=prefetch, grid=grid, in_specs=in_specs,
                                            out_specs=out_specs, scratch_shapes=list(scratch))
        return pl.pallas_call(body, name=name, grid_spec=spec, out_shape=out_shape, compiler_params=params,
                              input_output_aliases=aliases or {})
    return pl.pallas_call(body, name=name, grid=grid, in_specs=in_specs, out_specs=out_specs, out_shape=out_shape,
                          scratch_shapes=list(scratch), compiler_params=params, input_output_aliases=aliases or {})


def _tok(t, n, col=0):
    return pl.BlockSpec((t, n), lambda i, *_: (i, col))


def _vec(n):
    return pl.BlockSpec((1, n), lambda *_: (0, 0))


def _whole(a):
    nd = a.ndim
    return pl.BlockSpec(a.shape, lambda *_: (0,) * nd)


def _layer(l, *dims):
    return pl.BlockSpec((1,) + dims, lambda *_: (l,) + (0,) * len(dims))


def _gelu(y):
    c0, c1 = 0.7978845608028654, 0.044715
    return 0.5 * y * (1.0 + jnp.tanh(c0 * (y + c1 * y * y * y)))


def _gelu_grad(y):
    c0, c1 = 0.7978845608028654, 0.044715
    t = jnp.tanh(c0 * (y + c1 * y * y * y))
    return 0.5 * (1.0 + t) + 0.5 * y * (1.0 - t * t) * c0 * (1.0 + 3.0 * c1 * y * y)


def _softplus(v):
    return jnp.maximum(v, 0.0) + jnp.log(1.0 + jnp.exp(-jnp.where(v > 0, v, -v)))


@functools.partial(jax.custom_vjp, nondiff_argnums=(1,))
def _roll_rows(v, s):
    s = s % v.shape[0]
    return pltpu.roll(v, s, axis=0) if s else v


def _roll_rows_fwd(v, s):
    return _roll_rows(v, s), None


def _roll_rows_bwd(s, _, g):
    return (_roll_rows(g, -s),)


_roll_rows.defvjp(_roll_rows_fwd, _roll_rows_bwd)


@jax.custom_vjp
def _drop_halo(v):
    return v[HALO:]


def _drop_halo_fwd(v):
    return v[HALO:], None


def _drop_halo_bwd(_, g):
    return (jnp.concatenate([jnp.zeros((HALO, g.shape[1]), g.dtype), g], axis=0),)


_drop_halo.defvjp(_drop_halo_fwd, _drop_halo_bwd)


@functools.partial(jax.custom_vjp, nondiff_argnums=(1, 2))
def _split(v, n, axis):
    w = v.shape[axis] // n
    return tuple(lax.slice_in_dim(v, k * w, (k + 1) * w, axis=axis) for k in range(n))


def _split_fwd(v, n, axis):
    return _split(v, n, axis), None


def _split_bwd(n, axis, _, gs):
    return (jnp.concatenate(list(gs), axis=axis),)


_split.defvjp(_split_fwd, _split_bwd)


def _conv_taps(xw):
    return [_drop_halo(_roll_rows(xw, CONV_K - 1 - k)) for k in range(CONV_K)]


def _modulated_norm(xv, nw, sc, sh):
    r = lax.rsqrt(jnp.mean(xv * xv, axis=-1, keepdims=True) + EPS)
    n = xv * r * nw
    return n * (1.0 + sc) + sh, n, r


def _modulated_norm_bwd(dh, xv, n, r, nw, sc):
    dn = dh * (1.0 + sc)
    dxn = dn * nw
    dx = r * dxn - xv * (r * r * r) * jnp.mean(dxn * xv, axis=-1, keepdims=True)
    return (dx, jnp.sum(dh, axis=0, keepdims=True), jnp.sum(dh * n, axis=0, keepdims=True),
            jnp.sum(dn * xv * r, axis=0, keepdims=True))


def _inproj_fwd(x, nw, sc, sh, win, wba, l):
    s = x.shape[0]
    t = _tile(s, 512)

    def body(x_ref, nw_ref, sc_ref, sh_ref, win_ref, wba_ref, proj_ref, ba_ref):
        h, _, _ = _modulated_norm(x_ref[...], nw_ref[...], sc_ref[...], sh_ref[...])
        hb = h.astype(BF16)
        proj_ref[...] = _bdot(hb, win_ref[0])
        ba_ref[...] = _bdot(hb, wba_ref[0])

    return _call(body, "inproj_fwd", (s // t,),
                 [_tok(t, D_MODEL), _vec(D_MODEL), _vec(D_MODEL), _vec(D_MODEL), _layer(l, D_MODEL, IN_MAIN),
                  _layer(l, D_MODEL, BA_PAD)],
                 [_tok(t, IN_MAIN), _tok(t, BA_PAD)],
                 [_sds((s, IN_MAIN)), _sds((s, BA_PAD))])(x, nw, sc, sh, win, wba)


def _inproj_bwd(dpl, dpq, dpz, dba, x, dx1, nw, sc, sh, win, wba, l):
    s = x.shape[0]
    t = _tile(s, 512)

    def body(dpl_ref, dpq_ref, dpz_ref, dba_ref, x_ref, dx1_ref, nw_ref, sc_ref, sh_ref, win_ref, wba_ref,
             dx_ref, hb_ref, acc_ref):
        @pl.when(pl.program_id(0) == 0)
        def _():
            acc_ref[...] = jnp.zeros_like(acc_ref)

        dh = (_bdot(dpl_ref[...], win_ref[0, :, 0:2 * LRU_W], NT)
              + _bdot(dpq_ref[...], win_ref[0, :, 2 * LRU_W:2 * LRU_W + 3 * GDN_W], NT)
              + _bdot(dpz_ref[...], win_ref[0, :, 2 * LRU_W + 3 * GDN_W:IN_MAIN], NT)
              + _bdot(dba_ref[...], wba_ref[0], NT))
        xv = x_ref[...]
        h, n, r = _modulated_norm(xv, nw_ref[...], sc_ref[...], sh_ref[...])
        hb_ref[...] = h.astype(BF16)
        dx, dsh, dsc, dnw = _modulated_norm_bwd(dh, xv, n, r, nw_ref[...], sc_ref[...])
        dx_ref[...] = dx1_ref[...] + dx
        acc_ref[0:1, :] += dsh
        acc_ref[1:2, :] += dsc
        acc_ref[2:3, :] += dnw

    return _call(body, "inproj_bwd", (s // t,),
                 [_tok(t, 2 * LRU_W), _tok(t, 3 * GDN_W), _tok(t, GDN_W), _tok(t, BA_PAD), _tok(t, D_MODEL),
                  _tok(t, D_MODEL), _vec(D_MODEL), _vec(D_MODEL), _vec(D_MODEL), _layer(l, D_MODEL, IN_MAIN),
                  _layer(l, D_MODEL, BA_PAD)],
                 [_tok(t, D_MODEL), _tok(t, D_MODEL), pl.BlockSpec((8, D_MODEL), lambda i: (0, 0))],
                 [_sds((s, D_MODEL)), _sds((s, D_MODEL), BF16), _sds((8, D_MODEL))])(
                     dpl, dpq, dpz, dba, x, dx1, nw, sc, sh, win, wba)


def _lru_gates(xw, cw_rows, cb, wa, wx, gab, gxb, lam):
    taps = _conv_taps(xw)
    xr = cb + cw_rows[0] * taps[0] + cw_rows[1] * taps[1] + cw_rows[2] * taps[2] + cw_rows[3] * taps[3]
    xb = xr.astype(BF16)
    r = jax.nn.sigmoid(_bdot(xb, wa) + gab)
    i = jax.nn.sigmoid(_bdot(xb, wx) + gxb)
    z = jnp.exp(-jnp.where(lam > 0, lam, -lam))
    w1 = 1.0 + z
    log1p_z = jnp.where(w1 == 1.0, z, jnp.log(w1) * z / (w1 - 1.0))
    ls = jnp.minimum(lam, 0.0) - log1p_z
    la = LRU_C * r * ls
    a = jnp.exp(la)
    x2 = 2.0 * la
    u = jnp.exp(x2)
    mm_raw = jnp.where(u == 1.0, -x2,
                       jnp.where(x2 < -30.0, 1.0, (1.0 - u) * x2 / jnp.log(jnp.maximum(u, 1e-30))))
    mult = jnp.sqrt(jnp.maximum(mm_raw, 1e-12))
    return dict(taps=taps, xr=xr, r=r, i=i, ls=ls, a=a, mm_raw=mm_raw, mult=mult)


def _lru_specs(s, t, tile_of):
    nh = t // HALO
    xl = pl.BlockSpec((t, LRU_W), lambda i: (tile_of(i), 0))
    yl = pl.BlockSpec((t, LRU_W), lambda i: (tile_of(i), 1))
    hx = pl.BlockSpec((HALO, LRU_W), lambda i: (jnp.maximum(tile_of(i) * nh - 1, 0), 0))
    return xl, yl, hx


def _lru_fwd(proj, cw, cb, wa, wx, gab, gxb, lam, lnw):
    s = proj.shape[0]
    t = _tile(s, 256)
    xl, yl, hx = _lru_specs(s, t, lambda i: i)

    def body(xl_ref, yl_ref, hx_ref, cw_ref, cb_ref, wa_ref, wx_ref, gab_ref, gxb_ref, lam_ref, lnw_ref,
             out_ref, h_ref, a_s, b_s, hc):
        i = pl.program_id(0)

        @pl.when(i == 0)
        def _():
            hc[...] = jnp.zeros_like(hc)

        halo = jnp.where(i > 0, hx_ref[...], 0.0)
        xw = jnp.concatenate([halo, xl_ref[...]], axis=0)
        g = _lru_gates(xw, [cw_ref[k:k + 1, :] for k in range(CONV_K)], cb_ref[...], wa_ref[...], wx_ref[...],
                       gab_ref[...], gxb_ref[...], lam_ref[...])
        a_s[...] = g["a"]
        b_s[...] = g["mult"] * (g["i"] * g["xr"])

        def step(k, h):
            h = a_s[pl.ds(k, 1), :] * h + b_s[pl.ds(k, 1), :]
            h_ref[pl.ds(k, 1), :] = h
            return h

        hc[...] = lax.fori_loop(0, t, step, hc[...], unroll=8)
        m = h_ref[...] * _gelu(yl_ref[...])
        out_ref[...] = m * lax.rsqrt(jnp.mean(m * m, axis=-1, keepdims=True) + EPS) * lnw_ref[...]

    return _call(body, "lru_fwd", (s // t,),
                 [xl, yl, hx, _whole(cw), _vec(LRU_W), _whole(wa), _whole(wx)] + [_vec(LRU_W)] * 4,
                 [_tok(t, LRU_W), _tok(t, LRU_W)],
                 [_sds((s, LRU_W)), _sds((s, LRU_W))],
                 scratch=[pltpu.VMEM((t, LRU_W), F32), pltpu.VMEM((t, LRU_W), F32), pltpu.VMEM((1, LRU_W), F32)])(
                     proj, proj, proj, cw, cb, wa, wx, gab, gxb, lam, lnw)


def _lru_bwd(dout, proj, hs, cw, cb, wa, wx, gab, gxb, lam, lnw):
    s = proj.shape[0]
    t = _tile(s, 256)
    nt = s // t
    rev = lambda i: nt - 1 - i
    xl, yl, hx = _lru_specs(s, t, rev)
    nh = t // HALO
    tk = pl.BlockSpec((t, LRU_W), lambda i: (rev(i), 0))
    hh = pl.BlockSpec((HALO, LRU_W), lambda i: (jnp.maximum(rev(i) * nh - 1, 0), 0))

    def body(do_ref, xl_ref, yl_ref, hx_ref, h_ref, hh_ref, cw_ref, cb_ref, wa_ref, wx_ref, gab_ref, gxb_ref,
             lam_ref, lnw_ref, dp_ref, dwa_ref, dwx_ref, rows_ref, dh_s, dhd_s, carry, dxr_next):
        i = pl.program_id(0)
        first_tile = rev(i) == 0

        @pl.when(i == 0)
        def _():
            carry[...] = jnp.zeros_like(carry)
            dxr_next[...] = jnp.zeros_like(dxr_next)
            dwa_ref[...] = jnp.zeros_like(dwa_ref)
            dwx_ref[...] = jnp.zeros_like(dwx_ref)
            rows_ref[...] = jnp.zeros_like(rows_ref)

        halo = jnp.where(first_tile, 0.0, hx_ref[...])
        xw = jnp.concatenate([halo, xl_ref[...]], axis=0)
        cw_rows = [cw_ref[k:k + 1, :] for k in range(CONV_K)]
        lam_v = lam_ref[...]
        g = _lru_gates(xw, cw_rows, cb_ref[...], wa_ref[...], wx_ref[...], gab_ref[...], gxb_ref[...], lam_v)
        a, r, gi, xr, mult = g["a"], g["r"], g["i"], g["xr"], g["mult"]
        hv = h_ref[...]
        yv = yl_ref[...]
        gl = _gelu(yv)
        m = hv * gl
        rn = lax.rsqrt(jnp.mean(m * m, axis=-1, keepdims=True) + EPS)
        dov = do_ref[...]
        dmn = dov * lnw_ref[...]
        rows_ref[4:5, :] += jnp.sum(dov * m * rn, axis=0, keepdims=True)
        dm = rn * dmn - m * (rn * rn * rn) * jnp.mean(dmn * m, axis=-1, keepdims=True)
        dhd_s[...] = dm * gl
        dy = dm * hv * _gelu_grad(yv)
        dh_s[...] = a

        def step(k, c):
            row = t - 1 - k
            d = dhd_s[pl.ds(row, 1), :] + c
            c = dh_s[pl.ds(row, 1), :] * d
            dh_s[pl.ds(row, 1), :] = d
            return c

        carry[...] = lax.fori_loop(0, t, step, carry[...], unroll=8)
        dH = dh_s[...]
        hprev_halo = jnp.where(first_tile, 0.0, hh_ref[...])
        hprev = _drop_halo(_roll_rows(jnp.concatenate([hprev_halo, hv], axis=0), 1))
        da = dH * hprev
        dmult = dH * gi * xr
        di = dH * mult * xr
        dxr = dH * mult * gi
        dla = jnp.where(g["mm_raw"] > 1e-12, dmult * (0.5 / mult) * (-2.0 * a * a), 0.0) + da * a
        dr = dla * (LRU_C * g["ls"])
        sig_neg = jax.nn.sigmoid(-lam_v)
        rows_ref[3:4, :] += jnp.sum(dla * (LRU_C * r), axis=0, keepdims=True) * sig_neg
        drp = dr * r * (1.0 - r)
        dip = di * gi * (1.0 - gi)
        rows_ref[1:2, :] += jnp.sum(drp, axis=0, keepdims=True)
        rows_ref[2:3, :] += jnp.sum(dip, axis=0, keepdims=True)
        xb = xr.astype(BF16)
        drb = drp.astype(BF16)
        dib = dip.astype(BF16)
        dwa_ref[...] += _bdot(xb, drb, TN)
        dwx_ref[...] += _bdot(xb, dib, TN)
        dxr = dxr + _bdot(drb, wa_ref[...], NT) + _bdot(dib, wx_ref[...], NT)
        rows_ref[0:1, :] += jnp.sum(dxr, axis=0, keepdims=True)
        ext = jnp.concatenate([dxr, dxr_next[...]], axis=0)
        dx = cw_rows[CONV_K - 1] * dxr
        for k in range(CONV_K - 1):
            dx = dx + cw_rows[k] * _roll_rows(ext, -(CONV_K - 1 - k))[0:t]
        for k in range(CONV_K):
            rows_ref[8 + k:9 + k, :] += jnp.sum(dxr * g["taps"][k], axis=0, keepdims=True)
        dxr_next[...] = dxr[0:HALO]
        dp_ref[...] = jnp.concatenate([dx, dy], axis=1).astype(BF16)

    acc = lambda shape: pl.BlockSpec(shape, lambda i: (0, 0))
    return _call(body, "lru_bwd", (nt,),
                 [tk, xl, yl, hx, tk, hh, _whole(cw), _vec(LRU_W), _whole(wa), _whole(wx)] + [_vec(LRU_W)] * 4,
                 [pl.BlockSpec((t, 2 * LRU_W), lambda i: (rev(i), 0)), acc((LRU_W, LRU_W)), acc((LRU_W, LRU_W)),
                  acc((16, LRU_W))],
                 [_sds((s, 2 * LRU_W), BF16), _sds((LRU_W, LRU_W)), _sds((LRU_W, LRU_W)), _sds((16, LRU_W))],
                 scratch=[pltpu.VMEM((t, LRU_W), F32), pltpu.VMEM((t, LRU_W), F32), pltpu.VMEM((1, LRU_W), F32),
                          pltpu.VMEM((HALO, LRU_W), F32)])(
                     dout, proj, proj, proj, hs, hs, cw, cb, wa, wx, gab, gxb, lam, lnw)


def _gdn_masks():
    row = lax.broadcasted_iota(jnp.int32, (STACK, STACK), 0)
    col = lax.broadcasted_iota(jnp.int32, (STACK, STACK), 1)
    same = (row // CHUNK) == (col // CHUNK)
    return jnp.stack([(same & (col <= row)).astype(F32), (same & (col < row)).astype(F32), (row == col).astype(F32)])


def _conv_silu(xw, rows):
    taps = _conv_taps(xw)
    y = rows[0] * taps[0] + rows[1] * taps[1] + rows[2] * taps[2] + rows[3] * taps[3]
    return y * jax.nn.sigmoid(y)


def _split3(v):
    hi = v.astype(BF16)
    r1 = v - hi.astype(F32)
    mid = r1.astype(BF16)
    return hi, mid, (r1 - mid.astype(F32)).astype(BF16)


def _mask_dot_raw(mask, v, dims):
    parts = _split3(v)
    d = lambda p: lax.dot_general(mask, p, dims, preferred_element_type=F32)
    return d(parts[0]) + (d(parts[1]) + d(parts[2]))


@jax.custom_vjp
def _mask_dot(mask, v):
    return _mask_dot_raw(mask, v, NN)


def _mask_dot_fwd(mask, v):
    return _mask_dot_raw(mask, v, NN), mask


def _mask_dot_bwd(mask, ct):
    return jnp.zeros_like(mask), _mask_dot_raw(mask, ct, TN)


_mask_dot.defvjp(_mask_dot_fwd, _mask_dot_bwd)


def _unit_lower_inverse(n, eye):
    tinv = eye + n
    p = n
    for _ in range(5):
        p = _bdot(p, p)
        tinv = tinv + _bdot(tinv, p)
    return tinv.astype(BF16)


@jax.custom_vjp
def _unit_lower_solve(n, rhs, tinv):
    x0 = _bdot(tinv, rhs)
    return x0 + _bdot(tinv, rhs - x0 + _sdot(n, x0))


def _unit_lower_solve_fwd(n, rhs, tinv):
    x = _unit_lower_solve(n, rhs, tinv)
    return x, (n, tinv, x)


def _unit_lower_solve_bwd(res, ct):
    n, tinv, x = res
    y0 = _bdot(tinv, ct, TN)
    y = y0 + _bdot(tinv, ct - y0 + _sdot(n, y0, TN), TN)
    return _bdot(y, x, NT), y, jnp.zeros_like(tinv)


_unit_lower_solve.defvjp(_unit_lower_solve_fwd, _unit_lower_solve_bwd)


def _gdn_prep(xq, xk, xv, ba, cwq, cwk, cwv, pa, pd, masks, tinv=None, with_inverse=False):
    lower, strict, eye = masks[0], masks[1], masks[2]
    lower_b = lower.astype(BF16)
    lane = lax.broadcasted_iota(jnp.int32, (CHUNK, LANES), 1)
    q = jnp.concatenate(_split(_conv_silu(xq, cwq), HEADS, 1), axis=0)
    k = jnp.concatenate(_split(_conv_silu(xk, cwk), HEADS, 1), axis=0)
    v = jnp.concatenate(_split(_conv_silu(xv, cwv), HEADS, 1), axis=0)
    qn = q * lax.rsqrt(jnp.sum(q * q, axis=-1, keepdims=True) + 1e-6) * (HEAD_DIM ** -0.5)
    kn = k * lax.rsqrt(jnp.sum(k * k, axis=-1, keepdims=True) + 1e-6)
    beta_f = jax.nn.sigmoid(ba)
    g_f = -jnp.exp(pa) * _softplus(ba + pd)

    def col(a, j):
        return jnp.broadcast_to(jnp.sum(jnp.where(lane == j, a, 0.0), axis=1, keepdims=True), (CHUNK, HEAD_DIM))

    beta = jnp.concatenate([col(beta_f, h) for h in range(HEADS)], axis=0)
    gs = [col(g_f, HEADS + h) for h in range(HEADS)]
    g = jnp.concatenate(gs, axis=0)
    gl = jnp.concatenate([jnp.broadcast_to(jnp.sum(gh, axis=0, keepdims=True), (CHUNK, HEAD_DIM)) for gh in gs], axis=0)
    gc = _mask_dot(lower_b, g)
    gc_rows = jnp.transpose(gc)
    decay = jnp.exp((jnp.concatenate([gc, gc], axis=1) - jnp.concatenate([gc_rows, gc_rows], axis=0)) * lower)
    egc = jnp.exp(gc)
    kb = kn * beta
    n = -(_bdot(kb, kn, NT) * decay * strict)
    if tinv is None:
        tinv = _unit_lower_inverse(lax.stop_gradient(n), eye)
    u, w = _split(_unit_lower_solve(n, jnp.concatenate([v * beta, kb * egc], axis=1), tinv), 2, 1)
    attn = _bdot(qn, kn, NT) * decay * lower
    outs = (u, w, qn * egc, kn * jnp.exp(gl - gc), attn, jnp.exp(gl))
    return outs + (tinv,) if with_inverse else outs


def _gdn_scan(states, u, w, qd, kt, attn, egl, z, nw):
    us, ws, qds, kts, egls = (_split(a, HEADS, 0) for a in (u, w, qd, kt, egl))
    vn = [us[h] - _bdot(ws[h], states[h]) for h in range(HEADS)]
    o = jnp.concatenate([_bdot(qds[h], states[h]) for h in range(HEADS)], axis=0)
    o = o + _bdot(attn, jnp.concatenate(vn, axis=0))
    new = [states[h] * jnp.concatenate([egls[h], egls[h]], axis=0) + _bdot(kts[h], vn[h], TN) for h in range(HEADS)]
    on = o * lax.rsqrt(jnp.mean(o * o, axis=-1, keepdims=True) + EPS) * nw
    return new, on * (z * jax.nn.sigmoid(z))


def _gdn_in_specs(step_of, chunks):
    nh = chunks * CHUNK // HALO
    main = [pl.BlockSpec((chunks * CHUNK, GDN_W), functools.partial(lambda col, i: (step_of(i), col), col))
            for col in (2, 3, 4)]
    halo = [pl.BlockSpec((HALO, GDN_W), functools.partial(lambda col, i: (jnp.maximum(step_of(i) * nh - 1, 0), col),
                                                         col)) for col in (2, 3, 4)]
    return main, halo


def _stk(width, step_of, chunks=1):
    return pl.BlockSpec((chunks * STACK, width), lambda i: (step_of(i), 0))


def _chunk_inputs(main_refs, halo_refs, k, first_step):
    rows = slice(k * CHUNK, (k + 1) * CHUNK)
    if k == 0:
        halos = [jnp.where(first_step, 0.0, h[...]) for h in halo_refs]
    else:
        halos = [m[k * CHUNK - HALO:k * CHUNK, :] for m in main_refs]
    return [jnp.concatenate([h, m[rows, :]], axis=0) for h, m in zip(halos, main_refs)]


def _gdn_prep_fwd(proj, ba, cw, pa, pd, masks):
    s = proj.shape[0]
    nc = s // CHUNK
    per = min(PREP_CHUNKS, nc)
    main, halo = _gdn_in_specs(lambda i: i, per)

    def body(xq_ref, xk_ref, xv_ref, hq_ref, hk_ref, hv_ref, ba_ref, cw_ref, pa_ref, pd_ref, mk_ref, *out_refs):
        first_step = pl.program_id(0) == 0
        rows = [[cw_ref[k:k + 1, j * GDN_W:(j + 1) * GDN_W] for k in range(CONV_K)] for j in range(3)]
        cst = [mk_ref[0], mk_ref[1], mk_ref[2]]
        for k in range(per):
            xs = _chunk_inputs((xq_ref, xk_ref, xv_ref), (hq_ref, hk_ref, hv_ref), k, first_step)
            outs = _gdn_prep(xs[0], xs[1], xs[2], ba_ref[k * CHUNK:(k + 1) * CHUNK, :], rows[0], rows[1], rows[2],
                             pa_ref[...], pd_ref[...], cst, with_inverse=True)
            for ref, val in zip(out_refs, outs):
                ref[k * STACK:(k + 1) * STACK, :] = val.astype(ref.dtype)

    ident = lambda i: i
    stacked = lambda dt: _sds((nc * STACK, HEAD_DIM), dt)
    wide, thin = _stk(STACK, ident, per), _stk(HEAD_DIM, ident, per)
    return _call(body, "gdn_prep_fwd", (nc // per,),
                 main + halo + [_tok(per * CHUNK, BA_PAD), _whole(cw), _vec(BA_PAD), _vec(BA_PAD), _whole(masks)],
                 [thin] * 4 + [wide, thin, wide],
                 [stacked(F32), stacked(BF16), stacked(BF16), stacked(BF16), _sds((nc * STACK, STACK), BF16),
                  stacked(F32), _sds((nc * STACK, STACK), BF16)])(
                     proj, proj, proj, proj, proj, proj, ba, cw, pa, pd, masks)


def _gdn_prep_bwd(cts, tinv, proj, ba, cw, pa, pd, masks):
    s = proj.shape[0]
    nc = s // CHUNK
    per = min(PREP_CHUNKS, nc)
    steps = nc // per
    rev = lambda i: steps - 1 - i
    main, halo = _gdn_in_specs(rev, per)

    def body(du_ref, dw_ref, dqd_ref, dkt_ref, dattn_ref, degl_ref, tinv_ref, xq_ref, xk_ref, xv_ref, hq_ref, hk_ref,
             hv_ref, ba_ref, cw_ref, pa_ref, pd_ref, mk_ref, dp_ref, dba_ref, dcw_ref, dpar_ref, carry):
        i = pl.program_id(0)
        first_step = rev(i) == 0

        @pl.when(i == 0)
        def _():
            carry[...] = jnp.zeros_like(carry)
            dcw_ref[...] = jnp.zeros_like(dcw_ref)
            dpar_ref[...] = jnp.zeros_like(dpar_ref)

        rows = [[cw_ref[k:k + 1, j * GDN_W:(j + 1) * GDN_W] for k in range(CONV_K)] for j in range(3)]
        cst = [mk_ref[0], mk_ref[1], mk_ref[2]]
        dxws = []
        for k in range(per):
            xs = _chunk_inputs((xq_ref, xk_ref, xv_ref), (hq_ref, hk_ref, hv_ref), k, first_step)
            stk = slice(k * STACK, (k + 1) * STACK)
            tinv_k = tinv_ref[stk, :]
            fn = lambda xq, xk, xv, b, rq, rk, rv, a, d: _gdn_prep(xq, xk, xv, b, rq, rk, rv, a, d, cst, tinv=tinv_k)
            _, vjp = jax.vjp(fn, xs[0], xs[1], xs[2], ba_ref[k * CHUNK:(k + 1) * CHUNK, :], rows[0], rows[1], rows[2],
                             pa_ref[...], pd_ref[...])
            dxq, dxk, dxv, dba, drq, drk, drv, dpa, dpd = vjp(
                (du_ref[stk, :], dw_ref[stk, :], dqd_ref[stk, :], dkt_ref[stk, :], dattn_ref[stk, :], degl_ref[stk, :]))
            dxws.append(jnp.concatenate([dxq, dxk, dxv], axis=1))
            dba_ref[k * CHUNK:(k + 1) * CHUNK, :] = dba.astype(BF16)
            for j, dr in enumerate((drq, drk, drv)):
                for kk in range(CONV_K):
                    dcw_ref[kk:kk + 1, j * GDN_W:(j + 1) * GDN_W] += dr[kk]
            dpar_ref[0:1, :] += dpa
            dpar_ref[1:2, :] += dpd
        pad = jnp.zeros((CHUNK - HALO, 3 * GDN_W), F32)
        for k in range(per):
            late = carry[...] if k == per - 1 else dxws[k + 1][0:HALO]
            dp_ref[k * CHUNK:(k + 1) * CHUNK, :] = (dxws[k][HALO:] + jnp.concatenate([pad, late], axis=0)).astype(BF16)
        carry[...] = dxws[0][0:HALO]

    acc = lambda shape: pl.BlockSpec(shape, lambda i: (0, 0))
    wide, thin = _stk(STACK, rev, per), _stk(HEAD_DIM, rev, per)
    return _call(body, "gdn_prep_bwd", (steps,),
                 [thin] * 4 + [wide, thin, wide] + main + halo
                 + [pl.BlockSpec((per * CHUNK, BA_PAD), lambda i: (rev(i), 0)), _whole(cw), _vec(BA_PAD), _vec(BA_PAD),
                    _whole(masks)],
                 [pl.BlockSpec((per * CHUNK, 3 * GDN_W), lambda i: (rev(i), 0)),
                  pl.BlockSpec((per * CHUNK, BA_PAD), lambda i: (rev(i), 0)), acc((CONV_K, 3 * GDN_W)),
                  acc((8, BA_PAD))],
                 [_sds((s, 3 * GDN_W), BF16), _sds((s, BA_PAD), BF16), _sds((CONV_K, 3 * GDN_W)), _sds((8, BA_PAD))],
                 scratch=[pltpu.VMEM((HALO, 3 * GDN_W), F32)])(
                     *cts, tinv, proj, proj, proj, proj, proj, proj, ba, cw, pa, pd, masks)


def _stack_heads(v):
    return jnp.concatenate(_split(v, HEADS, 1), axis=0)


def _unstack_heads(v):
    return jnp.concatenate(_split(v, HEADS, 0), axis=1)


def _gdn_scan_fwd(prep, proj, nw):
    s = proj.shape[0]
    nc = s // CHUNK
    per = min(SCAN_CHUNKS, nc)
    ident = lambda i: i
    srows = HEADS * HEAD_DIM

    def body(u_ref, w_ref, qd_ref, kt_ref, attn_ref, egl_ref, z_ref, nw_ref, out_ref, st_ref, state):
        @pl.when(pl.program_id(0) == 0)
        def _():
            state[...] = jnp.zeros_like(state)

        states = [state[h * HEAD_DIM:(h + 1) * HEAD_DIM, :] for h in range(HEADS)]
        for k in range(per):
            stk, tok = slice(k * STACK, (k + 1) * STACK), slice(k * CHUNK, (k + 1) * CHUNK)
            for h in range(HEADS):
                st_ref[k * srows + h * HEAD_DIM:k * srows + (h + 1) * HEAD_DIM, :] = states[h]
            states, out = _gdn_scan(states, u_ref[stk, :], w_ref[stk, :], qd_ref[stk, :], kt_ref[stk, :],
                                    attn_ref[stk, :], egl_ref[stk, :], _stack_heads(z_ref[tok, :]), nw_ref[...])
            out_ref[tok, :] = _unstack_heads(out)
        for h in range(HEADS):
            state[h * HEAD_DIM:(h + 1) * HEAD_DIM, :] = states[h]

    thin, wide = _stk(HEAD_DIM, ident, per), _stk(STACK, ident, per)
    return _call(body, "gdn_scan_fwd", (nc // per,),
                 [thin] * 4 + [wide, thin, _tok(per * CHUNK, GDN_W, col=5), _vec(HEAD_DIM)],
                 [_tok(per * CHUNK, GDN_W), pl.BlockSpec((per * srows, HEAD_DIM), lambda i: (i, 0))],
                 [_sds((s, GDN_W)), _sds((nc * srows, HEAD_DIM))],
                 scratch=[pltpu.VMEM((srows, HEAD_DIM), F32)])(*prep, proj, nw)


def _gdn_scan_bwd(dout, prep, st, proj, nw):
    s = proj.shape[0]
    nc = s // CHUNK
    per = min(SCAN_CHUNKS, nc)
    steps = nc // per
    rev = lambda i: steps - 1 - i
    srows = HEADS * HEAD_DIM

    def body(do_ref, u_ref, w_ref, qd_ref, kt_ref, attn_ref, egl_ref, st_ref, z_ref, nw_ref,
             du_ref, dw_ref, dqd_ref, dkt_ref, dattn_ref, degl_ref, dz_ref, dnw_ref, dstate):
        @pl.when(pl.program_id(0) == 0)
        def _():
            dstate[...] = jnp.zeros_like(dstate)
            dnw_ref[...] = jnp.zeros_like(dnw_ref)

        dnew = [dstate[h * HEAD_DIM:(h + 1) * HEAD_DIM, :] for h in range(HEADS)]
        for k in reversed(range(per)):
            stk, tok = slice(k * STACK, (k + 1) * STACK), slice(k * CHUNK, (k + 1) * CHUNK)
            states = [st_ref[k * srows + h * HEAD_DIM:k * srows + (h + 1) * HEAD_DIM, :] for h in range(HEADS)]
            f32 = lambda ref: ref[stk, :].astype(F32)
            _, vjp = jax.vjp(_gdn_scan, states, u_ref[stk, :], f32(w_ref), f32(qd_ref), f32(kt_ref), f32(attn_ref),
                             egl_ref[stk, :], _stack_heads(z_ref[tok, :]), nw_ref[...])
            dnew, du, dw, dqd, dkt, dattn, degl, dz, dnw = vjp((dnew, _stack_heads(do_ref[tok, :])))
            for ref, val in zip((du_ref, dw_ref, dqd_ref, dkt_ref, dattn_ref, degl_ref),
                                (du, dw, dqd, dkt, dattn, degl)):
                ref[stk, :] = val
            dz_ref[tok, :] = _unstack_heads(dz).astype(BF16)
            dnw_ref[0:1, :] += dnw
        for h in range(HEADS):
            dstate[h * HEAD_DIM:(h + 1) * HEAD_DIM, :] = dnew[h]

    tokr = lambda n, col=0: pl.BlockSpec((per * CHUNK, n), lambda i: (rev(i), col))
    thin, wide = _stk(HEAD_DIM, rev, per), _stk(STACK, rev, per)
    return _call(body, "gdn_scan_bwd", (steps,),
                 [tokr(GDN_W)] + [thin] * 4 + [wide, thin, pl.BlockSpec((per * srows, HEAD_DIM), lambda i: (rev(i), 0)),
                                               tokr(GDN_W, 5), _vec(HEAD_DIM)],
                 [thin] * 4 + [wide, thin, tokr(GDN_W), pl.BlockSpec((8, HEAD_DIM), lambda i: (0, 0))],
                 [_sds((nc * STACK, HEAD_DIM))] * 4 + [_sds((nc * STACK, STACK)), _sds((nc * STACK, HEAD_DIM)),
                                                       _sds((s, GDN_W), BF16), _sds((8, HEAD_DIM))],
                 scratch=[pltpu.VMEM((srows, HEAD_DIM), F32)])(dout, *prep, st, proj, nw)


def _wo_specs(l):
    half = N_DEV // 2
    return [pl.BlockSpec((half, 1, D_MODEL // N_DEV, D_MODEL), functools.partial(lambda k, *_: (k, l, 0, 0), k))
            for k in range(2)]


def _wo_half(ref):
    return ref[:, 0].reshape(ref.shape[0] * ref.shape[2], ref.shape[3])


def _out_mlp_fwd(ol, og, x, wo, g1, nw2, sc2, sh2, g2, wup, wdn, l):
    s = x.shape[0]
    t = _tile(s, 512)
    nj = wup.shape[0] // MLP_BLOCKS
    fc = wup.shape[3]

    def body(ol_ref, og_ref, x_ref, wol_ref, wog_ref, g1_ref, nw_ref, sc_ref, sh_ref, g2_ref, wup_ref, wdn_ref,
             x1_ref, mix_ref, ff_ref, x2_ref, h2_s, acc_s):
        j = pl.program_id(1)

        @pl.when(j == 0)
        def _():
            mix = _bdot(ol_ref[...], _wo_half(wol_ref)) + _bdot(og_ref[...], _wo_half(wog_ref))
            x1 = x_ref[...] + g1_ref[...] * mix
            mix_ref[...] = mix.astype(BF16)
            x1_ref[...] = x1
            h2, _, _ = _modulated_norm(x1, nw_ref[...], sc_ref[...], sh_ref[...])
            h2_s[...] = h2.astype(BF16)
            acc_s[...] = jnp.zeros_like(acc_s)

        part = None
        for b in range(MLP_BLOCKS):
            up = _bdot(h2_s[...], wup_ref[b, 0])
            down = _bdot(jnp.square(jnp.maximum(up, 0.0)), wdn_ref[b, 0])
            part = down if part is None else part + down
        acc_s[...] += part

        @pl.when(j == nj - 1)
        def _():
            ff_ref[...] = acc_s[...].astype(BF16)
            x2_ref[...] = x1_ref[...] + g2_ref[...] * acc_s[...]

    tk = lambda n: pl.BlockSpec((t, n), lambda i, j: (i, 0))
    return _call(body, "out_mlp_fwd", (s // t, nj),
                 [tk(LRU_W), tk(GDN_W), tk(D_MODEL)] + _wo_specs(l) + [_vec(D_MODEL)] * 5
                 + [pl.BlockSpec((MLP_BLOCKS, 1, D_MODEL, fc), lambda i, j: (j, l, 0, 0)),
                    pl.BlockSpec((MLP_BLOCKS, 1, fc, D_MODEL), lambda i, j: (j, l, 0, 0))],
                 [tk(D_MODEL)] * 4,
                 [_sds((s, D_MODEL)), _sds((s, D_MODEL), BF16), _sds((s, D_MODEL), BF16), _sds((s, D_MODEL))],
                 scratch=[pltpu.VMEM((t, D_MODEL), BF16), pltpu.VMEM((t, D_MODEL), F32)])(
                     ol, og, x, wo, wo, g1, nw2, sc2, sh2, g2, wup, wdn)


def _mlp_bwd(dx2, x1, ff, nw2, sc2, sh2, g2, wup, wdn, l):
    s = x1.shape[0]
    t = _tile(s, 512)
    nj = wup.shape[0] // MLP_BLOCKS
    fc = wup.shape[3]

    def body(dx2_ref, x1_ref, ff_ref, nw_ref, sc_ref, sh_ref, g2_ref, wup_ref, wdn_ref,
             act_ref, dup_ref, h2_ref, dff_ref, dx1_ref, rows_ref, dh2_s):
        i, j = pl.program_id(0), pl.program_id(1)

        @pl.when((i == 0) & (j == 0))
        def _():
            rows_ref[...] = jnp.zeros_like(rows_ref)

        @pl.when(j == 0)
        def _():
            h2, _, _ = _modulated_norm(x1_ref[...], nw_ref[...], sc_ref[...], sh_ref[...])
            h2_ref[...] = h2.astype(BF16)
            dx2 = dx2_ref[...]
            dff_ref[...] = (dx2 * g2_ref[...]).astype(BF16)
            rows_ref[2:3, :] += jnp.sum(dx2 * ff_ref[...].astype(F32), axis=0, keepdims=True)
            dh2_s[...] = jnp.zeros_like(dh2_s)

        part = None
        for b in range(MLP_BLOCKS):
            cols = slice(b * fc, (b + 1) * fc)
            up = _bdot(h2_ref[...], wup_ref[b, 0])
            ru = jnp.maximum(up, 0.0)
            act_ref[:, cols] = (ru * ru).astype(BF16)
            dup = (_bdot(dff_ref[...], wdn_ref[b, 0], NT) * (2.0 * ru)).astype(BF16)
            dup_ref[:, cols] = dup
            back = _bdot(dup, wup_ref[b, 0], NT)
            part = back if part is None else part + back
        dh2_s[...] += part

        @pl.when(j == nj - 1)
        def _():
            xv = x1_ref[...]
            _, n, r = _modulated_norm(xv, nw_ref[...], sc_ref[...], sh_ref[...])
            dx, dsh, dsc, dnw = _modulated_norm_bwd(dh2_s[...], xv, n, r, nw_ref[...], sc_ref[...])
            dx1_ref[...] = dx2_ref[...] + dx
            rows_ref[0:1, :] += dsh
            rows_ref[1:2, :] += dsc
            rows_ref[3:4, :] += dnw

    tk = lambda n: pl.BlockSpec((t, n), lambda i, j: (i, 0))
    tj = pl.BlockSpec((t, MLP_BLOCKS * fc), lambda i, j: (i, j))
    return _call(body, "mlp_bwd", (s // t, nj),
                 [tk(D_MODEL)] * 3 + [_vec(D_MODEL)] * 4
                 + [pl.BlockSpec((MLP_BLOCKS, 1, D_MODEL, fc), lambda i, j: (j, l, 0, 0)),
                    pl.BlockSpec((MLP_BLOCKS, 1, fc, D_MODEL), lambda i, j: (j, l, 0, 0))],
                 [tj, tj, tk(D_MODEL), tk(D_MODEL), tk(D_MODEL), pl.BlockSpec((8, D_MODEL), lambda i, j: (0, 0))],
                 [_sds((s, D_FF), BF16), _sds((s, D_FF), BF16), _sds((s, D_MODEL), BF16),
                  _sds((s, D_MODEL), BF16), _sds((s, D_MODEL)), _sds((8, D_MODEL))],
                 scratch=[pltpu.VMEM((t, D_MODEL), F32)], vmem_mb=56)(dx2, x1, ff, nw2, sc2, sh2, g2, wup, wdn)


def _outproj_bwd(dx1, mix, g1, wo, l):
    s = dx1.shape[0]
    t = _tile(s, 512)

    def body(dx1_ref, mix_ref, g1_ref, wol_ref, wog_ref, dmix_ref, dol_ref, dog_ref, rows_ref):
        @pl.when(pl.program_id(0) == 0)
        def _():
            rows_ref[...] = jnp.zeros_like(rows_ref)

        dx1v = dx1_ref[...]
        rows_ref[0:1, :] += jnp.sum(dx1v * mix_ref[...].astype(F32), axis=0, keepdims=True)
        dmix = (dx1v * g1_ref[...]).astype(BF16)
        dmix_ref[...] = dmix
        dol_ref[...] = _bdot(dmix, _wo_half(wol_ref), NT)
        dog_ref[...] = _bdot(dmix, _wo_half(wog_ref), NT)

    return _call(body, "outproj_bwd", (s // t,),
                 [_tok(t, D_MODEL), _tok(t, D_MODEL), _vec(D_MODEL)] + _wo_specs(l),
                 [_tok(t, D_MODEL), _tok(t, LRU_W), _tok(t, GDN_W), pl.BlockSpec((8, D_MODEL), lambda i: (0, 0))],
                 [_sds((s, D_MODEL), BF16), _sds((s, LRU_W)), _sds((s, GDN_W)), _sds((8, D_MODEL))])(dx1, mix, g1, wo, wo)


def _tn_matmul(a, b, name, out=None, l=0, blocked=False, row_block=0):
    s, m = a.shape
    n = b.shape[1]
    ts, bm = _tile(s, 2048), _tile(m, 1024)
    bn = next(w for w in (512, 640, 384, 256, 128) if n % w == 0)

    def body(a_ref, b_ref, *rest):
        o_ref = rest[-1]

        @pl.when(pl.program_id(2) == 0)
        def _():
            o_ref[...] = jnp.zeros_like(o_ref)

        acc = _bdot(a_ref[...], b_ref[...], TN)
        o_ref[...] += acc.reshape(o_ref.shape)

    in_specs = [pl.BlockSpec((ts, bm), lambda i, j, k: (k, i)), pl.BlockSpec((ts, bn), lambda i, j, k: (k, j))]
    grid = (m // bm, n // bn, s // ts)
    if out is None:
        return _call(body, name, grid, in_specs, pl.BlockSpec((bm, bn), lambda i, j, k: (i, j)), _sds((m, n)))(a, b)
    if blocked:
        out_spec = pl.BlockSpec((1, 1, bm, bn), lambda i, j, k: (l, j, i, 0))
    else:
        out_spec = pl.BlockSpec((1, bm, bn), lambda i, j, k: (l, i + row_block * (m // bm), j))
    return _call(body, name, grid, in_specs + [pl.BlockSpec(memory_space=pl.ANY)], out_spec,
                 _sds(out.shape), aliases={2: 0})(a, b, out)


def _final_fwd_bwd(x, target, fw):
    s = x.shape[0]
    t = _tile(s, 512)

    def body(x_ref, tg_ref, fw_ref, dx_ref, rows_ref):
        @pl.when(pl.program_id(0) == 0)
        def _():
            rows_ref[...] = jnp.zeros_like(rows_ref)

        xv = x_ref[...]
        fwv = fw_ref[...]
        r = lax.rsqrt(jnp.mean(xv * xv, axis=-1, keepdims=True) + EPS)
        err = xv * r * fwv - tg_ref[...]
        part = 0.5 * jnp.sum(jnp.mean(err * err, axis=-1, keepdims=True), axis=0, keepdims=True)
        rows_ref[1:2, :] += jnp.broadcast_to(part, (1, D_MODEL))
        dy = err * (1.0 / D_MODEL)
        rows_ref[0:1, :] += jnp.sum(dy * xv * r, axis=0, keepdims=True)
        dxn = dy * fwv
        dx_ref[...] = r * dxn - xv * (r * r * r) * jnp.mean(dxn * xv, axis=-1, keepdims=True)

    return _call(body, "final_fwd_bwd", (s // t,),
                 [_tok(t, D_MODEL), _tok(t, D_MODEL), _vec(D_MODEL)],
                 [_tok(t, D_MODEL), pl.BlockSpec((8, D_MODEL), lambda i: (0, 0))],
                 [_sds((s, D_MODEL)), _sds((8, D_MODEL))])(x, target, fw)


def _adamw(w, g, m, v):
    m = ADAM_B1 * m + (1.0 - ADAM_B1) * g
    v = ADAM_B2 * v + (1.0 - ADAM_B2) * (g * g)
    m_hat = m / (1.0 - ADAM_B1 ** ADAM_STEP)
    v_hat = v / (1.0 - ADAM_B2 ** ADAM_STEP)
    return -ADAM_LR * (m_hat / (jnp.sqrt(v_hat) + ADAM_EPS) + ADAM_WD * w), m, v


def _mod_local(c_all, wmod, bmod_cols):
    nl, _, cols = wmod.shape

    def body(c_ref, w_ref, b_ref, o_ref):
        cv = c_ref[...]
        o_ref[0] = _bdot(cv * jax.nn.sigmoid(cv), w_ref[0]) + b_ref[0]

    return _call(body, "mod_local", (nl,),
                 [_whole(c_all), pl.BlockSpec((1, D_MODEL, cols), lambda l: (l, 0, 0)),
                  pl.BlockSpec((1, 1, cols), lambda l: (l, 0, 0))],
                 pl.BlockSpec((1, N_DEV, cols), lambda l: (l, 0, 0)), _sds((nl, N_DEV, cols)))(c_all, wmod, bmod_cols)


def _wmod_update(c_all, dmod_cols, w, m, v):
    nl, _, cols = w.shape

    def body(c_ref, d_ref, w_ref, m_ref, v_ref, g_ref, dl_ref, nm_ref, nv_ref):
        cv = c_ref[...]
        g = _bdot(cv * jax.nn.sigmoid(cv), d_ref[0], TN)
        g_ref[0] = g
        dl_ref[0], nm_ref[0], nv_ref[0] = _adamw(w_ref[0], g, m_ref[0], v_ref[0])

    wspec = pl.BlockSpec((1, D_MODEL, cols), lambda l: (l, 0, 0))
    return _call(body, "wmod_update", (nl,),
                 [_whole(c_all), pl.BlockSpec((1, N_DEV, cols), lambda l: (l, 0, 0)), wspec, wspec, wspec],
                 [wspec] * 4, [_sds(w.shape)] * 4)(c_all, dmod_cols, w, m, v)


def _sum_devices(gathered):
    _, r, _ = gathered.shape

    def body(g_ref, o_ref):
        acc = g_ref[0]
        for d in range(1, N_DEV):
            acc = acc + g_ref[d]
        o_ref[...] = acc

    return _call(body, "sum_devices", (1,), [_whole(gathered)], pl.BlockSpec((r, LANES), lambda i: (0, 0)),
                 _sds((r, LANES)))(gathered)


def _adam_flat(w, g, m, v):
    r = w.shape[0]

    def body(w_ref, g_ref, m_ref, v_ref, dl_ref, nm_ref, nv_ref):
        dl_ref[...], nm_ref[...], nv_ref[...] = _adamw(w_ref[...], g_ref[...], m_ref[...], v_ref[...])

    spec = pl.BlockSpec((r, LANES), lambda i: (0, 0))
    return _call(body, "adam_small", (1,), [spec] * 4, [spec] * 3, [_sds((r, LANES))] * 3)(w, g, m, v)


def _pair_add(x, p, core):
    _, r, c = x.shape
    tr = _tile(r, 128 if c > 512 else 256)

    def body(core_ref, x_ref, p_ref, o_ref):
        o_ref[...] = (x_ref[...] + p_ref[...]).astype(BF16)

    return _call(body, "pair_add", (4, r // tr),
                 [pl.BlockSpec((1, tr, c), lambda q, i, core_ref: (2 * q + core_ref[0], i, 0)),
                  pl.BlockSpec((1, tr, c), lambda q, i, core_ref: (q, i, 0))],
                 pl.BlockSpec((1, tr, c), lambda q, i, core_ref: (q, i, 0)), _sds((4, r, c), BF16),
                 prefetch=1)(core, x, p)


def _reduce_adam(x, p, q, place, w, m, v, l, outs):
    _, r, c = x.shape
    tr = _tile(r, 128 if c > 512 else 256)

    def body(place_ref, x_ref, p_ref, q_ref, w_ref, m_ref, v_ref, *rest):
        g_ref, dl_ref, nm_ref, nv_ref = rest[-4:]
        g = (((x_ref[0] + p_ref[0]) + q_ref[0].astype(F32)) + q_ref[1].astype(F32)) + q_ref[2].astype(F32)
        g_ref[0] = g
        dl_ref[0], nm_ref[0], nv_ref[0] = _adamw(w_ref[0], g, m_ref[0], v_ref[0])

    flat = pl.BlockSpec((1, tr, c), lambda i, place_ref: (l, i, 0))
    through = pl.BlockSpec(memory_space=pl.ANY)
    return _call(body, "reduce_adam", (r // tr,),
                 [pl.BlockSpec((1, tr, c), lambda i, place_ref: (place_ref[0], i, 0)),
                  pl.BlockSpec((1, tr, c), lambda i, place_ref: (place_ref[1], i, 0)),
                  pl.BlockSpec((3, tr, c), lambda i, place_ref: (0, i, 0)), flat, flat, flat] + [through] * 4,
                 [flat] * 4, [_sds(w.shape)] * 4, prefetch=1, aliases={7 + k: k for k in range(4)})(
                     place, x, p, q, w, m, v, *outs)


def _place():
    return lax.axis_index("x"), lax.axis_index("y"), lax.axis_index("c")


def _all_gather(xs, name, space):
    n = len(xs)

    def body(*refs):
        x_refs, o_refs = refs[:n], refs[n:2 * n]
        send_sems, recv_sems, local_sems = refs[2 * n:]
        x, y, c = _place()
        me, sibling = (x, y, c), (x, y, 1 - c)
        chips = [(1 - x, y), (x, 1 - y), (1 - x, 1 - y)]

        def blk(a, p):
            return o_refs[a].at[4 * p[0] + 2 * p[1] + p[2]]

        def copy(a, k, block, to, src=None):
            return pltpu.make_async_remote_copy(
                src_ref=blk(a, block) if src is None else src, dst_ref=blk(a, block),
                send_sem=send_sems.at[a, k], recv_sem=recv_sems.at[a, k], device_id=to, device_id_type=MESH)

        mine = [pltpu.make_async_copy(x_refs[a], blk(a, me), local_sems.at[a]) for a in range(n)]
        for cp in mine:
            cp.start()
        first = []
        for a in range(n):
            first.append(copy(a, 0, me, sibling, src=x_refs[a]))
            first += [copy(a, 1 + j, me, (*chip, c), src=x_refs[a]) for j, chip in enumerate(chips)]
        for cp in first:
            cp.start()
        passed = []
        for j, chip in enumerate(chips):
            for a in range(n):
                copy(a, 1 + j, (*chip, c), me).wait_recv()
                cp = copy(a, 4 + j, (*chip, c), sibling)
                cp.start()
                passed.append(cp)
        for a in range(n):
            copy(a, 0, sibling, me).wait_recv()
        for j, chip in enumerate(chips):
            for a in range(n):
                copy(a, 4 + j, (*chip, 1 - c), me).wait_recv()
        for cp in first + passed:
            cp.wait_send()
        for cp in mine:
            cp.wait()

    spec = pl.BlockSpec(memory_space=space)
    return pl.pallas_call(
        body, name=name, out_shape=[_sds((N_DEV,) + a.shape, a.dtype) for a in xs],
        in_specs=[spec] * n, out_specs=[spec] * n,
        scratch_shapes=[pltpu.SemaphoreType.DMA((n, 7)), pltpu.SemaphoreType.DMA((n, 7)),
                        pltpu.SemaphoreType.DMA((n,))])(*xs)


_HBM_SPEC = pl.BlockSpec(memory_space=pltpu.HBM)
_SEM_SPEC = pl.BlockSpec(memory_space=pltpu.SEMAPHORE)
_EFFECT = pltpu.SideEffectType.DATAFLOW_SIDE_EFFECTING


def _descriptors(plan, src_refs, land_refs, send_sems, recv_sems, which=None):
    return [pltpu.make_async_remote_copy(src_ref=s, dst_ref=d, send_sem=send_sems.at[k], recv_sem=recv_sems.at[k],
                                         device_id=dev, device_id_type=MESH)
            for k, (s, d, dev) in enumerate(plan(src_refs, land_refs)) if which is None or k in which]


def _split_start(name, plan, n, srcs, lands, after):
    ns, nb = len(srcs), len(srcs) + len(lands)
    after = list(after) if isinstance(after, (list, tuple)) else [after]
    sems = nb + len(after)

    def body(*refs):
        for cp in _descriptors(plan, refs[:ns], refs[ns:nb], refs[sems], refs[sems + 1]):
            cp.start()
        refs[-1][...] = jnp.zeros_like(refs[-1])

    bufs = [pltpu.with_memory_space_constraint(a, pltpu.HBM) for a in list(srcs) + list(lands)]
    outs = pl.pallas_call(
        body, name=name,
        out_shape=(pltpu.SemaphoreType.DMA((n,)), pltpu.SemaphoreType.DMA((n,)))
        + tuple(pltpu.HBM(a.shape, a.dtype) for a in bufs) + (_sds((8, LANES)),),
        in_specs=[_HBM_SPEC] * nb + [pl.BlockSpec(memory_space=pl.ANY)] * len(after),
        out_specs=(_SEM_SPEC, _SEM_SPEC) + (_HBM_SPEC,) * nb + (pl.BlockSpec(memory_space=pltpu.VMEM),),
        input_output_aliases={i: 2 + i for i in range(nb)},
        compiler_params=pltpu.CompilerParams(has_side_effects=_EFFECT))(*bufs, *after)
    return dict(send=outs[0], recv=outs[1], srcs=list(outs[2:2 + ns]), lands=list(outs[2 + ns:2 + nb]), token=outs[-1])


def _split_wait(name, plan, flight, which, after):
    srcs, lands = flight["srcs"], flight["lands"]
    ns, nb = len(srcs), len(srcs) + len(lands)

    def body(*refs):
        for cp in _descriptors(plan, refs[:ns], refs[ns:nb], refs[nb], refs[nb + 1], set(which)):
            cp.wait_send()
            cp.wait_recv()

    outs = pl.pallas_call(
        body, name=name, out_shape=tuple(pltpu.HBM(a.shape, a.dtype) for a in srcs + lands),
        in_specs=[_HBM_SPEC] * nb + [_SEM_SPEC, _SEM_SPEC, pl.BlockSpec(memory_space=pl.ANY)],
        out_specs=(_HBM_SPEC,) * nb, input_output_aliases={i: i for i in range(nb)},
        compiler_params=pltpu.CompilerParams(has_side_effects=_EFFECT))(*srcs, *lands, flight["send"], flight["recv"],
                                                                       after)
    return dict(flight, srcs=list(outs[:ns]), lands=list(outs[ns:nb]))


GATHER_PEERS = N_DEV - 1


def _gather_plan(items):
    def plan(src_refs, land_refs):
        x, y, c = _place()
        me = 4 * x + 2 * y + c
        out = []
        for a, l in items:
            for r in range(1, N_DEV):
                peer = (1 - x if r & 4 else x, 1 - y if r & 2 else y, 1 - c if r & 1 else c)
                out.append((src_refs[a].at[l], land_refs[a].at[me, l], peer))
        return out

    return plan


def _pair_plan(narr):
    def plan(src_refs, land_refs):
        x, y, c = _place()
        return [(src_refs[a].at[2 * q + (1 - c)], land_refs[a].at[q], (x, y, 1 - c))
                for a in range(narr) for q in range(4)]

    return plan


def _chip_plan(narr):
    def plan(src_refs, land_refs):
        x, y, c = _place()
        chips = [(1 - x, y), (x, 1 - y), (1 - x, 1 - y)]
        return [(src_refs[a].at[2 * chip[0] + chip[1]], land_refs[a].at[r], (*chip, c))
                for a in range(narr) for r, chip in enumerate(chips)]

    return plan


class _GradReducer:
    def __init__(self, tag, names, w, mom, var, place, core):
        self.tag, self.names, self.w, self.mom, self.var, self.place, self.core = tag, names, w, mom, var, place, core
        self.outs = {k: [lax.empty(w[k].shape, F32) for _ in range(4)] for k in names}
        self.n = len(names)

    def start(self, l, grads):
        self.l, self.xs = l, [grads[k] for k in self.names]
        lands = [lax.empty((4,) + a.shape[1:], F32) for a in self.xs]
        self.pair = _split_start(f"{self.tag}_pair_start{l}", _pair_plan(self.n), 4 * self.n, self.xs, lands, ())
        return self.pair["token"][0, 0]

    def middle(self, after):
        self.pair = _split_wait(f"{self.tag}_pair_wait{self.l}", _pair_plan(self.n), self.pair, range(4 * self.n),
                                after)
        self.xs, self.ps = self.pair["srcs"], self.pair["lands"]
        ys = [_pair_add(x, p, self.core) for x, p in zip(self.xs, self.ps)]
        lands = [lax.empty((3,) + a.shape[1:], BF16) for a in ys]
        self.chip = _split_start(f"{self.tag}_chip_start{self.l}", _chip_plan(self.n), 3 * self.n, ys, lands, ())
        return self.chip["token"][0, 0]

    def finish(self, after):
        chip = _split_wait(f"{self.tag}_chip_wait{self.l}", _chip_plan(self.n), self.chip, range(3 * self.n), after)
        for k, x, p, q in zip(self.names, self.xs, self.ps, chip["lands"]):
            self.outs[k] = _reduce_adam(x, p, q, self.place, self.w[k], self.mom[k], self.var[k], self.l, self.outs[k])


def _size(shape):
    size = 1
    for d in shape:
        size *= d
    return size


def _slab_rows(shape):
    return -(-_size(shape) // (8 * LANES)) * 8


def _pack(arrs):
    parts = []
    for a in arrs:
        flat = a.reshape(-1).astype(F32)
        parts.append(jnp.pad(flat, (0, _slab_rows(a.shape) * LANES - flat.shape[0])).reshape(-1, LANES))
    return jnp.concatenate(parts, axis=0)


def _unpack(slab, shapes):
    out, off = [], 0
    for shp in shapes:
        rows = _slab_rows(shp)
        out.append(slab[off:off + rows].reshape(-1)[:_size(shp)].reshape(shp))
        off += rows
    return out


def _dense_blocks(w):
    eye = jnp.eye(LRU_BLOCKS, dtype=w.dtype)
    return (eye[:, None, :, None] * w[:, :, None, :]).reshape(LRU_W, LRU_W)


def _diag_blocks(dense):
    return jnp.stack([dense[g * LRU_BLOCK:(g + 1) * LRU_BLOCK, g * LRU_BLOCK:(g + 1) * LRU_BLOCK]
                      for g in range(LRU_BLOCKS)])


def _alpha_lanes(v):
    return jnp.zeros((1, BA_PAD), F32).at[0, HEADS:2 * HEADS].set(v)


def _local_step(x, target, mod, p, fetch, reducers=None):
    nl = mod.shape[0]
    row = lambda v: v.reshape(1, -1)
    masks = _gdn_masks()
    saved = []
    xc = x
    for l in range(nl):
        win, wba, lin = fetch(l, "in", xc)
        mv = [row(mod[l, k * D_MODEL:(k + 1) * D_MODEL]) for k in range(N_MOD)]
        sh1, sc1, g1, sh2, sc2, g2 = mv
        nw1, nw2 = row(p["norm_mix_w"][l]), row(p["norm_mlp_w"][l])
        wa, wx = _dense_blocks(p["lru_gate_a_w"][l]).astype(BF16), _dense_blocks(p["lru_gate_x_w"][l]).astype(BF16)
        lru_args = (p["lru_conv_w"][l], row(p["lru_conv_b"][l]), wa, wx, row(p["lru_gate_a_b"][l]),
                    row(p["lru_gate_x_b"][l]), row(p["lru_lambda"][l]), row(p["lru_norm_w"][l]))
        gdn_args = (p["gdn_conv_w"][l], _alpha_lanes(p["gdn_a_log"][l]), _alpha_lanes(p["gdn_dt_bias"][l]), masks)
        gnw = row(p["gdn_norm_w"][l])
        proj, ba = _inproj_fwd(xc, nw1, sc1, sh1, win, wba, lin)
        ol, hs = _lru_fwd(proj, *lru_args)
        *prep, tinv = _gdn_prep_fwd(proj, ba, *gdn_args)
        og, st = _gdn_scan_fwd(prep, proj, gnw)
        wo, wup, wdn = fetch(l, "rest", og)
        x1, mix, ff, x2 = _out_mlp_fwd(ol, og, xc, wo, g1, nw2, sc2, sh2, g2, wup, wdn, l)
        saved.append(dict(x=xc, mv=mv, nw1=nw1, nw2=nw2, lru_args=lru_args, gdn_args=gdn_args, gnw=gnw, proj=proj,
                          ba=ba, ol=ol, hs=hs, prep=prep, tinv=tinv, og=og, st=st, x1=x1, mix=mix, ff=ff,
                          win=win, wba=wba, lin=lin))
        xc = x2

    dx, frows = _final_fwd_bwd(xc, target, row(p["final_norm_w"]))
    loss_part = frows[1, 0]
    small = {k: [None] * nl for k in ("norm_mix_w", "norm_mlp_w", "lru_conv_w", "lru_conv_b", "lru_gate_a_w",
                                      "lru_gate_a_b", "lru_gate_x_w", "lru_gate_x_b", "lru_lambda", "lru_norm_w",
                                      "gdn_conv_w", "gdn_a_log", "gdn_dt_bias", "gdn_norm_w")}
    fc = D_FF // N_DEV
    big = [None] * nl
    dmod = [None] * nl
    mlp_red, mix_red = reducers or (None, None)
    busy = False
    for l in reversed(range(nl)):
        sv = saved[l]
        sh1, sc1, g1, sh2, sc2, g2 = sv["mv"]
        gnw = sv["gnw"]
        if busy:
            g2 = g2 + started
        act, dup, h2b, dffb, dx1, rows2 = _mlp_bwd(dx, sv["x1"], sv["ff"], sv["nw2"], sc2, sh2, g2, wup, wdn, l)
        if busy:
            g1 = g1 + mix_red.middle(dx1)
        g_up = _tn_matmul(h2b, dup, "grad_w_up", out=lax.empty((1, N_DEV, D_MODEL, fc), F32), blocked=True)[0]
        g_down = _tn_matmul(act, dffb, "grad_w_down", out=lax.empty((1, D_FF, D_MODEL), F32))
        g_down = g_down.reshape(N_DEV, fc, D_MODEL)
        if mlp_red is not None:
            g1 = g1 + mlp_red.start(l, dict(w_up=g_up, w_down=g_down))
        dmix, dol, dog, rows1 = _outproj_bwd(dx1, sv["mix"], g1, wo, l)
        g_out = _tn_matmul(sv["ol"], dmix, "grad_w_out_lru", out=lax.empty((1, D_MODEL, D_MODEL), F32))
        g_out = _tn_matmul(sv["og"], dmix, "grad_w_out_gdn", out=g_out, row_block=1)
        dpl, dwa, dwx, lrows = _lru_bwd(dol, sv["proj"], sv["hs"], *sv["lru_args"])
        if mlp_red is not None:
            gnw = gnw + mlp_red.middle(dpl)
        *cts, dpz, gnrow = _gdn_scan_bwd(dog, sv["prep"], sv["st"], sv["proj"], gnw)
        dpq, dba, dcw, dpar = _gdn_prep_bwd(cts, sv["tinv"], sv["proj"], sv["ba"], *sv["gdn_args"])
        dx, hb, rows0 = _inproj_bwd(dpl, dpq, dpz, dba, sv["x"], dx1, sv["nw1"], sc1, sh1, sv["win"], sv["wba"],
                                    sv["lin"])
        if busy:
            mix_red.finish(dx)
        if mlp_red is not None:
            mlp_red.finish(dx)
        dproj = jnp.concatenate([dpl, dpq, dpz, dba], axis=1)
        g_in = jnp.transpose(_tn_matmul(hb, dproj, "grad_w_in")[:, :IN_COLS].reshape(
            D_MODEL, N_DEV, IN_COLS // N_DEV), (1, 0, 2))
        big[l] = dict(w_in=g_in, w_out=g_out.reshape(N_DEV, D_MODEL // N_DEV, D_MODEL), w_up=g_up,
                      w_down=g_down)
        if mix_red is not None:
            started, busy = mix_red.start(l, big[l]), True
        dmod[l] = jnp.concatenate([rows0[0], rows0[1], rows1[0], rows2[0], rows2[1], rows2[2]])
        small["norm_mix_w"][l], small["norm_mlp_w"][l] = rows0[2], rows2[3]
        small["lru_conv_w"][l], small["lru_conv_b"][l] = lrows[8:8 + CONV_K], lrows[0]
        small["lru_gate_a_w"][l], small["lru_gate_x_w"][l] = _diag_blocks(dwa), _diag_blocks(dwx)
        small["lru_gate_a_b"][l], small["lru_gate_x_b"][l] = lrows[1], lrows[2]
        small["lru_lambda"][l], small["lru_norm_w"][l] = lrows[3], lrows[4]
        small["gdn_conv_w"][l] = dcw
        small["gdn_a_log"][l], small["gdn_dt_bias"][l] = dpar[0, HEADS:2 * HEADS], dpar[1, HEADS:2 * HEADS]
        small["gdn_norm_w"][l] = gnrow[0]
    if busy:
        mix_red.middle(dx)
        mix_red.finish(dx)
    small = {k: jnp.stack(v) for k, v in small.items()}
    small["final_norm_w"] = frows[0]
    return loss_part, dx, big, small, jnp.stack(dmod)


SMALL_REPLICATED = ("norm_mix_w", "norm_mlp_w", "b_mod", "lru_conv_b", "lru_gate_a_w", "lru_gate_a_b", "lru_gate_x_w",
                    "lru_gate_x_b", "lru_lambda", "lru_norm_w", "gdn_a_log", "gdn_dt_bias", "gdn_norm_w",
                    "final_norm_w")
SMALL_SHARDED = ("lru_conv_w", "gdn_conv_w")
WEIGHT_ORDER = ("norm_mix_w", "norm_mlp_w", "w_mod", "b_mod", "w_in", "lru_conv_w", "lru_conv_b", "lru_gate_a_w",
                "lru_gate_a_b", "lru_gate_x_w", "lru_gate_x_b", "lru_lambda", "lru_norm_w", "gdn_conv_w", "gdn_a_log",
                "gdn_dt_bias", "gdn_norm_w", "w_out", "w_up", "w_down", "final_norm_w")


def kernel(x, c, norm_mix_w, norm_mlp_w, w_mod, b_mod, w_in, lru_conv_w, lru_conv_b, lru_gate_a_w, lru_gate_a_b, lru_gate_x_w, lru_gate_x_b, lru_lambda, lru_norm_w, gdn_conv_w, gdn_a_log, gdn_dt_bias, gdn_norm_w, w_out, w_up, w_down, final_norm_w, loss_target, m_norm_mix_w, m_norm_mlp_w, m_w_mod, m_b_mod, m_w_in, m_lru_conv_w, m_lru_conv_b, m_lru_gate_a_w, m_lru_gate_a_b, m_lru_gate_x_w, m_lru_gate_x_b, m_lru_lambda, m_lru_norm_w, m_gdn_conv_w, m_gdn_a_log, m_gdn_dt_bias, m_gdn_norm_w, m_w_out, m_w_up, m_w_down, m_final_norm_w, v_norm_mix_w, v_norm_mlp_w, v_w_mod, v_b_mod, v_w_in, v_lru_conv_w, v_lru_conv_b, v_lru_gate_a_w, v_lru_gate_a_b, v_lru_gate_x_w, v_lru_gate_x_b, v_lru_lambda, v_lru_norm_w, v_gdn_conv_w, v_gdn_a_log, v_gdn_dt_bias, v_gdn_norm_w, v_w_out, v_w_up, v_w_down, v_final_norm_w):
    args = dict(locals())
    w = {k: args[k] for k in WEIGHT_ORDER}
    mom = {k: args["m_" + k] for k in WEIGHT_ORDER}
    var = {k: args["v_" + k] for k in WEIGHT_ORDER}
    nl = w_in.shape[0]
    px, py, pc = _place()
    me = 4 * px + 2 * py + pc
    core = jnp.reshape(pc, (1,)).astype(jnp.int32)

    shapes0 = [c.shape, lru_conv_w.shape, gdn_conv_w.shape]
    (g0,) = _all_gather([_pack([c, lru_conv_w, gdn_conv_w])], "gather_cond", pltpu.VMEM)
    per_dev = [_unpack(g0[d], shapes0) for d in range(N_DEV)]
    c_all = jnp.concatenate([pd[0] for pd in per_dev], axis=0)
    lru_conv_full = jnp.concatenate([pd[1] for pd in per_dev], axis=-1)
    gdn_conv_full = jnp.concatenate([pd[2] for pd in per_dev], axis=-1)

    cols = w_mod.shape[2]
    bmod_cols = lax.dynamic_slice_in_dim(b_mod, me * cols, cols, axis=1).reshape(nl, 1, cols)
    mod_cols = _mod_local(c_all, w_mod, bmod_cols)
    (g1,) = _all_gather([mod_cols.reshape(nl * N_DEV, cols)], "gather_mod", pltpu.VMEM)
    g1 = g1.reshape(N_DEV, nl, N_DEV, cols)
    mod = jnp.transpose(lax.dynamic_index_in_dim(g1, me, axis=2, keepdims=False), (1, 0, 2)).reshape(nl, N_DEV * cols)

    shards = [a.astype(BF16) for a in (w_in, w_out, w_up, w_down)]
    (first_in,) = _all_gather([shards[0][:1]], "gather_w_in_first", pl.ANY)
    items = [(a, 0) for a in (1, 2, 3)] + [(a, l) for l in range(1, nl) for a in range(4)]
    plan = _gather_plan(items)
    lands = [lax.dynamic_update_slice_in_dim(lax.empty((N_DEV,) + a.shape, BF16), a[None], me, axis=0) for a in shards]
    flight = [_split_start("gather_weights_start", plan, len(items) * GATHER_PEERS, shards, lands, [first_in, mod])]
    mod = mod + flight[0]["token"][0, 0]

    def fetch(l, what, after):
        wanted = [k for k, (a, ll) in enumerate(items) if ll == l and (a == 0) == (what == "in")]
        if wanted:
            flight[0] = _split_wait(f"gather_weights_wait_{what}{l}", plan, flight[0],
                                    [k * GATHER_PEERS + r for k in wanted for r in range(GATHER_PEERS)], after)
        gin, gout, gup, gdn = flight[0]["lands"]
        if what == "rest":
            return gout, gup, gdn
        gin = first_in[:, 0] if l == 0 else gin[:, l]
        win = jnp.transpose(gin, (1, 0, 2)).reshape(1, D_MODEL, IN_COLS)
        wba = jnp.pad(win[:, :, IN_MAIN:], ((0, 0), (0, 0), (0, BA_PAD - (IN_COLS - IN_MAIN))))
        return win, wba, 0

    p = dict(w)
    p["lru_conv_w"], p["gdn_conv_w"] = lru_conv_full, gdn_conv_full

    order = ("w_in", "w_out", "w_up", "w_down")
    place = jnp.stack([me, 2 * px + py]).astype(jnp.int32)
    reducers = (_GradReducer("mlp_grad", ("w_up", "w_down"), w, mom, var, place, core),
                _GradReducer("mix_grad", ("w_in", "w_out"), w, mom, var, place, core))
    loss_part, grad_x, _, small, dmod = _local_step(x[0], loss_target[0], mod, p, fetch, reducers)
    loss = lax.psum(loss_part, MESH_AXES)

    small_names = sorted(small)
    slab = _pack([dmod] + [small[k] for k in small_names])
    (gs,) = _all_gather([slab], "gather_small_grads", pltpu.VMEM)
    dmod_all = gs[:, :_slab_rows(dmod.shape)].reshape(N_DEV, nl, N_MOD * D_MODEL)
    summed = _unpack(_sum_devices(gs), [dmod.shape] + [small[k].shape for k in small_names])
    grads = dict(zip(small_names, summed[1:]))
    grads["b_mod"] = summed[0]
    for k, width in (("lru_conv_w", LRU_W // N_DEV), ("gdn_conv_w", 3 * GDN_W // N_DEV)):
        grads[k] = lax.dynamic_slice_in_dim(grads[k], me * width, width, axis=2)
    names = SMALL_REPLICATED + SMALL_SHARDED
    shapes = [w[k].shape for k in names]
    dl, nm, nv = _adam_flat(_pack([w[k] for k in names]), _pack([grads[k] for k in names]),
                            _pack([mom[k] for k in names]), _pack([var[k] for k in names]))
    delta = dict(zip(names, _unpack(dl, shapes)))
    new_m = dict(zip(names, _unpack(nm, shapes)))
    new_v = dict(zip(names, _unpack(nv, shapes)))

    dmod_cols = jnp.transpose(lax.dynamic_slice_in_dim(dmod_all, me * cols, cols, axis=2), (1, 0, 2))
    grads["w_mod"], delta["w_mod"], new_m["w_mod"], new_v["w_mod"] = _wmod_update(
        c_all, dmod_cols, w_mod, m_w_mod, v_w_mod)

    for red in reducers:
        for k in red.names:
            grads[k], delta[k], new_m[k], new_v[k] = red.outs[k]

    return (loss, grad_x[None], *[grads[k] for k in WEIGHT_ORDER], *[delta[k] for k in WEIGHT_ORDER],
            *[new_m[k] for k in WEIGHT_ORDER], *[new_v[k] for k in WEIGHT_ORDER])
```

```python
import functools

import jax
import jax.numpy as jnp
from jax import lax
from jax.experimental import pallas as pl
from jax.experimental.pallas import tpu as pltpu

F32 = jnp.float32
BF16 = jnp.bfloat16

D_MODEL = 1024
LRU_W = 512
LRU_BLOCKS = 8
LRU_BLOCK = 64
LRU_C = 8.0
GDN_W = 512
HEADS = 4
HEAD_DIM = 128
CHUNK = 64
STACK = HEADS * CHUNK
CONV_K = 4
D_FF = 4096
N_MOD = 6
IN_COLS = 3080
IN_MAIN = 3072
BA_PAD = 128
EPS = 1e-6
N_DEV = 8
HALO = 8
MLP_BLOCKS = 4
PREP_CHUNKS = 2
SCAN_CHUNKS = 2
LANES = 128
ADAM_LR, ADAM_B1, ADAM_B2, ADAM_EPS, ADAM_WD, ADAM_STEP = 0.001, 0.9, 0.999, 1e-08, 0.01, 10
MESH_AXES = ("x", "y", "c")
MESH = pl.DeviceIdType.MESH

NN = (((1,), (0,)), ((), ()))
NT = (((1,), (1,)), ((), ()))
TN = (((0,), (0,)), ((), ()))


def _bdot(a, b, dims=NN):
    return lax.dot_general(a.astype(BF16), b.astype(BF16), dims, preferred_element_type=F32)


def _sdot(a, b, dims=NN):
    ah, bh = a.astype(BF16), b.astype(BF16)
    al, bl = (a - ah.astype(F32)).astype(BF16), (b - bh.astype(F32)).astype(BF16)
    return _bdot(ah, bh, dims) + (_bdot(al, bh, dims) + _bdot(ah, bl, dims))


def _hdot(a, b, dims=NN):
    return lax.dot_general(a, b, dims, precision=lax.Precision.HIGHEST, preferred_element_type=F32)


def _sds(shape, dtype=F32):
    return jax.ShapeDtypeStruct(tuple(shape), dtype)


def _tile(n, t):
    return min(n, t)


def _call(body, name, grid, in_specs, out_specs, out_shape, scratch=(), vmem_mb=48, prefetch=0, aliases=None):
    params = pltpu.CompilerParams(dimension_semantics=("arbitrary",) * len(grid), vmem_limit_bytes=vmem_mb * 2**20)
    if prefetch:
        spec = pltpu.PrefetchScalarGridSpec(num_scalar_prefetch=prefetch, grid=grid, in_specs=in_specs,
                                            out_specs=out_specs, scratch_shapes=list(scratch))
        return pl.pallas_call(body, name=name, grid_spec=spec, out_shape=out_shape, compiler_params=params,
                              input_output_aliases=aliases or {})
    return pl.pallas_call(body, name=name, grid=grid, in_specs=in_specs, out_specs=out_specs, out_shape=out_shape,
                          scratch_shapes=list(scratch), compiler_params=params, input_output_aliases=aliases or {})


def _tok(t, n, col=0):
    return pl.BlockSpec((t, n), lambda i, *_: (i, col))


def _vec(n):
    return pl.BlockSpec((1, n), lambda *_: (0, 0))


def _whole(a):
    nd = a.ndim
    return pl.BlockSpec(a.shape, lambda *_: (0,) * nd)


def _layer(l, *dims):
    return pl.BlockSpec((1,) + dims, lambda *_: (l,) + (0,) * len(dims))


def _gelu(y):
    c0, c1 = 0.7978845608028654, 0.044715
    return 0.5 * y * (1.0 + jnp.tanh(c0 * (y + c1 * y * y * y)))


def _gelu_grad(y):
    c0, c1 = 0.7978845608028654, 0.044715
    t = jnp.tanh(c0 * (y + c1 * y * y * y))
    return 0.5 * (1.0 + t) + 0.5 * y * (1.0 - t * t) * c0 * (1.0 + 3.0 * c1 * y * y)


def _softplus(v):
    return jnp.maximum(v, 0.0) + jnp.log(1.0 + jnp.exp(-jnp.where(v > 0, v, -v)))


@functools.partial(jax.custom_vjp, nondiff_argnums=(1,))
def _roll_rows(v, s):
    s = s % v.shape[0]
    return pltpu.roll(v, s, axis=0) if s else v


def _roll_rows_fwd(v, s):
    return _roll_rows(v, s), None


def _roll_rows_bwd(s, _, g):
    return (_roll_rows(g, -s),)


_roll_rows.defvjp(_roll_rows_fwd, _roll_rows_bwd)


@jax.custom_vjp
def _drop_halo(v):
    return v[HALO:]


def _drop_halo_fwd(v):
    return v[HALO:], None


def _drop_halo_bwd(_, g):
    return (jnp.concatenate([jnp.zeros((HALO, g.shape[1]), g.dtype), g], axis=0),)


_drop_halo.defvjp(_drop_halo_fwd, _drop_halo_bwd)


@functools.partial(jax.custom_vjp, nondiff_argnums=(1, 2))
def _split(v, n, axis):
    w = v.shape[axis] // n
    return tuple(lax.slice_in_dim(v, k * w, (k + 1) * w, axis=axis) for k in range(n))


def _split_fwd(v, n, axis):
    return _split(v, n, axis), None


def _split_bwd(n, axis, _, gs):
    return (jnp.concatenate(list(gs), axis=axis),)


_split.defvjp(_split_fwd, _split_bwd)


def _conv_taps(xw):
    return [_drop_halo(_roll_rows(xw, CONV_K - 1 - k)) for k in range(CONV_K)]


def _modulated_norm(xv, nw, sc, sh):
    r = lax.rsqrt(jnp.mean(xv * xv, axis=-1, keepdims=True) + EPS)
    n = xv * r * nw
    return n * (1.0 + sc) + sh, n, r


def _modulated_norm_bwd(dh, xv, n, r, nw, sc):
    dn = dh * (1.0 + sc)
    dxn = dn * nw
    dx = r * dxn - xv * (r * r * r) * jnp.mean(dxn * xv, axis=-1, keepdims=True)
    return (dx, jnp.sum(dh, axis=0, keepdims=True), jnp.sum(dh * n, axis=0, keepdims=True),
            jnp.sum(dn * xv * r, axis=0, keepdims=True))


def _inproj_fwd(x, nw, sc, sh, win, wba, l):
    s = x.shape[0]
    t = _tile(s, 512)

    def body(x_ref, nw_ref, sc_ref, sh_ref, win_ref, wba_ref, proj_ref, ba_ref):
        h, _, _ = _modulated_norm(x_ref[...], nw_ref[...], sc_ref[...], sh_ref[...])
        hb = h.astype(BF16)
        proj_ref[...] = _bdot(hb, win_ref[0])
        ba_ref[...] = _bdot(hb, wba_ref[0])

    return _call(body, "inproj_fwd", (s // t,),
                 [_tok(t, D_MODEL), _vec(D_MODEL), _vec(D_MODEL), _vec(D_MODEL), _layer(l, D_MODEL, IN_MAIN),
                  _layer(l, D_MODEL, BA_PAD)],
                 [_tok(t, IN_MAIN), _tok(t, BA_PAD)],
                 [_sds((s, IN_MAIN)), _sds((s, BA_PAD))])(x, nw, sc, sh, win, wba)


def _inproj_bwd(dpl, dpq, dpz, dba, x, dx1, nw, sc, sh, win, wba, l):
    s = x.shape[0]
    t = _tile(s, 512)

    def body(dpl_ref, dpq_ref, dpz_ref, dba_ref, x_ref, dx1_ref, nw_ref, sc_ref, sh_ref, win_ref, wba_ref,
             dx_ref, hb_ref, acc_ref):
        @pl.when(pl.program_id(0) == 0)
        def _():
            acc_ref[...] = jnp.zeros_like(acc_ref)

        dh = (_bdot(dpl_ref[...], win_ref[0, :, 0:2 * LRU_W], NT)
              + _bdot(dpq_ref[...], win_ref[0, :, 2 * LRU_W:2 * LRU_W + 3 * GDN_W], NT)
              + _bdot(dpz_ref[...], win_ref[0, :, 2 * LRU_W + 3 * GDN_W:IN_MAIN], NT)
              + _bdot(dba_ref[...], wba_ref[0], NT))
        xv = x_ref[...]
        h, n, r = _modulated_norm(xv, nw_ref[...], sc_ref[...], sh_ref[...])
        hb_ref[...] = h.astype(BF16)
        dx, dsh, dsc, dnw = _modulated_norm_bwd(dh, xv, n, r, nw_ref[...], sc_ref[...])
        dx_ref[...] = dx1_ref[...] + dx
        acc_ref[0:1, :] += dsh
        acc_ref[1:2, :] += dsc
        acc_ref[2:3, :] += dnw

    return _call(body, "inproj_bwd", (s // t,),
                 [_tok(t, 2 * LRU_W), _tok(t, 3 * GDN_W), _tok(t, GDN_W), _tok(t, BA_PAD), _tok(t, D_MODEL),
                  _tok(t, D_MODEL), _vec(D_MODEL), _vec(D_MODEL), _vec(D_MODEL), _layer(l, D_MODEL, IN_MAIN),
                  _layer(l, D_MODEL, BA_PAD)],
                 [_tok(t, D_MODEL), _tok(t, D_MODEL), pl.BlockSpec((8, D_MODEL), lambda i: (0, 0))],
                 [_sds((s, D_MODEL)), _sds((s, D_MODEL), BF16), _sds((8, D_MODEL))])(
                     dpl, dpq, dpz, dba, x, dx1, nw, sc, sh, win, wba)


def _lru_gates(xw, cw_rows, cb, wa, wx, gab, gxb, lam):
    taps = _conv_taps(xw)
    xr = cb + cw_rows[0] * taps[0] + cw_rows[1] * taps[1] + cw_rows[2] * taps[2] + cw_rows[3] * taps[3]
    xb = xr.astype(BF16)
    r = jax.nn.sigmoid(_bdot(xb, wa) + gab)
    i = jax.nn.sigmoid(_bdot(xb, wx) + gxb)
    z = jnp.exp(-jnp.where(lam > 0, lam, -lam))
    w1 = 1.0 + z
    log1p_z = jnp.where(w1 == 1.0, z, jnp.log(w1) * z / (w1 - 1.0))
    ls = jnp.minimum(lam, 0.0) - log1p_z
    la = LRU_C * r * ls
    a = jnp.exp(la)
    x2 = 2.0 * la
    u = jnp.exp(x2)
    mm_raw = jnp.where(u == 1.0, -x2,
                       jnp.where(x2 < -30.0, 1.0, (1.0 - u) * x2 / jnp.log(jnp.maximum(u, 1e-30))))
    mult = jnp.sqrt(jnp.maximum(mm_raw, 1e-12))
    return dict(taps=taps, xr=xr, r=r, i=i, ls=ls, a=a, mm_raw=mm_raw, mult=mult)


def _lru_specs(s, t, tile_of):
    nh = t // HALO
    xl = pl.BlockSpec((t, LRU_W), lambda i: (tile_of(i), 0))
    yl = pl.BlockSpec((t, LRU_W), lambda i: (tile_of(i), 1))
    hx = pl.BlockSpec((HALO, LRU_W), lambda i: (jnp.maximum(tile_of(i) * nh - 1, 0), 0))
    return xl, yl, hx


def _lru_fwd(proj, cw, cb, wa, wx, gab, gxb, lam, lnw):
    s = proj.shape[0]
    t = _tile(s, 256)
    xl, yl, hx = _lru_specs(s, t, lambda i: i)

    def body(xl_ref, yl_ref, hx_ref, cw_ref, cb_ref, wa_ref, wx_ref, gab_ref, gxb_ref, lam_ref, lnw_ref,
             out_ref, h_ref, a_s, b_s, hc):
        i = pl.program_id(0)

        @pl.when(i == 0)
        def _():
            hc[...] = jnp.zeros_like(hc)

        halo = jnp.where(i > 0, hx_ref[...], 0.0)
        xw = jnp.concatenate([halo, xl_ref[...]], axis=0)
        g = _lru_gates(xw, [cw_ref[k:k + 1, :] for k in range(CONV_K)], cb_ref[...], wa_ref[...], wx_ref[...],
                       gab_ref[...], gxb_ref[...], lam_ref[...])
        a_s[...] = g["a"]
        b_s[...] = g["mult"] * (g["i"] * g["xr"])

        def step(k, h):
            h = a_s[pl.ds(k, 1), :] * h + b_s[pl.ds(k, 1), :]
            h_ref[pl.ds(k, 1), :] = h
            return h

        hc[...] = lax.fori_loop(0, t, step, hc[...], unroll=8)
        m = h_ref[...] * _gelu(yl_ref[...])
        out_ref[...] = m * lax.rsqrt(jnp.mean(m * m, axis=-1, keepdims=True) + EPS) * lnw_ref[...]

    return _call(body, "lru_fwd", (s // t,),
                 [xl, yl, hx, _whole(cw), _vec(LRU_W), _whole(wa), _whole(wx)] + [_vec(LRU_W)] * 4,
                 [_tok(t, LRU_W), _tok(t, LRU_W)],
                 [_sds((s, LRU_W)), _sds((s, LRU_W))],
                 scratch=[pltpu.VMEM((t, LRU_W), F32), pltpu.VMEM((t, LRU_W), F32), pltpu.VMEM((1, LRU_W), F32)])(
                     proj, proj, proj, cw, cb, wa, wx, gab, gxb, lam, lnw)


def _lru_bwd(dout, proj, hs, cw, cb, wa, wx, gab, gxb, lam, lnw):
    s = proj.shape[0]
    t = _tile(s, 256)
    nt = s // t
    rev = lambda i: nt - 1 - i
    xl, yl, hx = _lru_specs(s, t, rev)
    nh = t // HALO
    tk = pl.BlockSpec((t, LRU_W), lambda i: (rev(i), 0))
    hh = pl.BlockSpec((HALO, LRU_W), lambda i: (jnp.maximum(rev(i) * nh - 1, 0), 0))

    def body(do_ref, xl_ref, yl_ref, hx_ref, h_ref, hh_ref, cw_ref, cb_ref, wa_ref, wx_ref, gab_ref, gxb_ref,
             lam_ref, lnw_ref, dp_ref, dwa_ref, dwx_ref, rows_ref, dh_s, dhd_s, carry, dxr_next):
        i = pl.program_id(0)
        first_tile = rev(i) == 0

        @pl.when(i == 0)
        def _():
            carry[...] = jnp.zeros_like(carry)
            dxr_next[...] = jnp.zeros_like(dxr_next)
            dwa_ref[...] = jnp.zeros_like(dwa_ref)
            dwx_ref[...] = jnp.zeros_like(dwx_ref)
            rows_ref[...] = jnp.zeros_like(rows_ref)

        halo = jnp.where(first_tile, 0.0, hx_ref[...])
        xw = jnp.concatenate([halo, xl_ref[...]], axis=0)
        cw_rows = [cw_ref[k:k + 1, :] for k in range(CONV_K)]
        lam_v = lam_ref[...]
        g = _lru_gates(xw, cw_rows, cb_ref[...], wa_ref[...], wx_ref[...], gab_ref[...], gxb_ref[...], lam_v)
        a, r, gi, xr, mult = g["a"], g["r"], g["i"], g["xr"], g["mult"]
        hv = h_ref[...]
        yv = yl_ref[...]
        gl = _gelu(yv)
        m = hv * gl
        rn = lax.rsqrt(jnp.mean(m * m, axis=-1, keepdims=True) + EPS)
        dov = do_ref[...]
        dmn = dov * lnw_ref[...]
        rows_ref[4:5, :] += jnp.sum(dov * m * rn, axis=0, keepdims=True)
        dm = rn * dmn - m * (rn * rn * rn) * jnp.mean(dmn * m, axis=-1, keepdims=True)
        dhd_s[...] = dm * gl
        dy = dm * hv * _gelu_grad(yv)
        dh_s[...] = a

        def step(k, c):
            row = t - 1 - k
            d = dhd_s[pl.ds(row, 1), :] + c
            c = dh_s[pl.ds(row, 1), :] * d
            dh_s[pl.ds(row, 1), :] = d
            return c

        carry[...] = lax.fori_loop(0, t, step, carry[...], unroll=8)
        dH = dh_s[...]
        hprev_halo = jnp.where(first_tile, 0.0, hh_ref[...])
        hprev = _drop_halo(_roll_rows(jnp.concatenate([hprev_halo, hv], axis=0), 1))
        da = dH * hprev
        dmult = dH * gi * xr
        di = dH * mult * xr
        dxr = dH * mult * gi
        dla = jnp.where(g["mm_raw"] > 1e-12, dmult * (0.5 / mult) * (-2.0 * a * a), 0.0) + da * a
        dr = dla * (LRU_C * g["ls"])
        sig_neg = jax.nn.sigmoid(-lam_v)
        rows_ref[3:4, :] += jnp.sum(dla * (LRU_C * r), axis=0, keepdims=True) * sig_neg
        drp = dr * r * (1.0 - r)
        dip = di * gi * (1.0 - gi)
        rows_ref[1:2, :] += jnp.sum(drp, axis=0, keepdims=True)
        rows_ref[2:3, :] += jnp.sum(dip, axis=0, keepdims=True)
        xb = xr.astype(BF16)
        drb = drp.astype(BF16)
        dib = dip.astype(BF16)
        dwa_ref[...] += _bdot(xb, drb, TN)
        dwx_ref[...] += _bdot(xb, dib, TN)
        dxr = dxr + _bdot(drb, wa_ref[...], NT) + _bdot(dib, wx_ref[...], NT)
        rows_ref[0:1, :] += jnp.sum(dxr, axis=0, keepdims=True)
        ext = jnp.concatenate([dxr, dxr_next[...]], axis=0)
        dx = cw_rows[CONV_K - 1] * dxr
        for k in range(CONV_K - 1):
            dx = dx + cw_rows[k] * _roll_rows(ext, -(CONV_K - 1 - k))[0:t]
        for k in range(CONV_K):
            rows_ref[8 + k:9 + k, :] += jnp.sum(dxr * g["taps"][k], axis=0, keepdims=True)
        dxr_next[...] = dxr[0:HALO]
        dp_ref[...] = jnp.concatenate([dx, dy], axis=1).astype(BF16)

    acc = lambda shape: pl.BlockSpec(shape, lambda i: (0, 0))
    return _call(body, "lru_bwd", (nt,),
                 [tk, xl, yl, hx, tk, hh, _whole(cw), _vec(LRU_W), _whole(wa), _whole(wx)] + [_vec(LRU_W)] * 4,
                 [pl.BlockSpec((t, 2 * LRU_W), lambda i: (rev(i), 0)), acc((LRU_W, LRU_W)), acc((LRU_W, LRU_W)),
                  acc((16, LRU_W))],
                 [_sds((s, 2 * LRU_W), BF16), _sds((LRU_W, LRU_W)), _sds((LRU_W, LRU_W)), _sds((16, LRU_W))],
                 scratch=[pltpu.VMEM((t, LRU_W), F32), pltpu.VMEM((t, LRU_W), F32), pltpu.VMEM((1, LRU_W), F32),
                          pltpu.VMEM((HALO, LRU_W), F32)])(
                     dout, proj, proj, proj, hs, hs, cw, cb, wa, wx, gab, gxb, lam, lnw)


def _gdn_masks():
    row = lax.broadcasted_iota(jnp.int32, (STACK, STACK), 0)
    col = lax.broadcasted_iota(jnp.int32, (STACK, STACK), 1)
    same = (row // CHUNK) == (col // CHUNK)
    return jnp.stack([(same & (col <= row)).astype(F32), (same & (col < row)).astype(F32), (row == col).astype(F32)])


def _conv_silu(xw, rows):
    taps = _conv_taps(xw)
    y = rows[0] * taps[0] + rows[1] * taps[1] + rows[2] * taps[2] + rows[3] * taps[3]
    return y * jax.nn.sigmoid(y)


def _split3(v):
    hi = v.astype(BF16)
    r1 = v - hi.astype(F32)
    mid = r1.astype(BF16)
    return hi, mid, (r1 - mid.astype(F32)).astype(BF16)


def _mask_dot_raw(mask, v, dims):
    parts = _split3(v)
    d = lambda p: lax.dot_general(mask, p, dims, preferred_element_type=F32)
    return d(parts[0]) + (d(parts[1]) + d(parts[2]))


@jax.custom_vjp
def _mask_dot(mask, v):
    return _mask_dot_raw(mask, v, NN)


def _mask_dot_fwd(mask, v):
    return _mask_dot_raw(mask, v, NN), mask


def _mask_dot_bwd(mask, ct):
    return jnp.zeros_like(mask), _mask_dot_raw(mask, ct, TN)


_mask_dot.defvjp(_mask_dot_fwd, _mask_dot_bwd)


def _unit_lower_inverse(ns, eye):
    tinvs = [eye + n for n in ns]
    ps = list(ns)
    for _ in range(5):
        ps = [_bdot(p, p) for p in ps]
        tinvs = [t + _bdot(t, p) for t, p in zip(tinvs, ps)]
    return tuple(t.astype(BF16) for t in tinvs)


def _refined(ns, rhss, tinvs, dims):
    x0s = [_bdot(t, r, dims) for t, r in zip(tinvs, rhss)]
    ress = [r - x0 + _sdot(n, x0, dims) for n, r, x0 in zip(ns, rhss, x0s)]
    return tuple(x0 + _bdot(t, res, dims) for t, x0, res in zip(tinvs, x0s, ress))


@jax.custom_vjp
def _unit_lower_solve(ns, rhss, tinvs):
    return _refined(ns, rhss, tinvs, NN)


def _unit_lower_solve_fwd(ns, rhss, tinvs):
    xs = _unit_lower_solve(ns, rhss, tinvs)
    return xs, (ns, tinvs, xs)


def _unit_lower_solve_bwd(res, cts):
    ns, tinvs, xs = res
    ys = _refined(ns, cts, tinvs, TN)
    return (tuple(_bdot(y, x, NT) for y, x in zip(ys, xs)), ys, tuple(jnp.zeros_like(t) for t in tinvs))


_unit_lower_solve.defvjp(_unit_lower_solve_fwd, _unit_lower_solve_bwd)


def _gdn_prep(xqs, xks, xvs, bas, cwq, cwk, cwv, pa, pd, masks, tinvs=None, with_inverse=False):
    lower, strict, eye = masks[0], masks[1], masks[2]
    lower_b = lower.astype(BF16)
    lane = lax.broadcasted_iota(jnp.int32, (CHUNK, LANES), 1)
    each = lambda f, *lists: [f(*vals) for vals in zip(*lists)]
    stack = lambda xw, rows: jnp.concatenate(_split(_conv_silu(xw, rows), HEADS, 1), axis=0)
    qs, ks, vs = (each(lambda xw: stack(xw, cw), xs) for xs, cw in ((xqs, cwq), (xks, cwk), (xvs, cwv)))
    qns = each(lambda q: q * lax.rsqrt(jnp.sum(q * q, axis=-1, keepdims=True) + 1e-6) * (HEAD_DIM ** -0.5), qs)
    kns = each(lambda k: k * lax.rsqrt(jnp.sum(k * k, axis=-1, keepdims=True) + 1e-6), ks)

    def col(a, j):
        return jnp.broadcast_to(jnp.sum(jnp.where(lane == j, a, 0.0), axis=1, keepdims=True), (CHUNK, HEAD_DIM))

    betas = each(lambda ba: jnp.concatenate([col(jax.nn.sigmoid(ba), h) for h in range(HEADS)], axis=0), bas)
    g_heads = each(lambda ba: [col(-jnp.exp(pa) * _softplus(ba + pd), HEADS + h) for h in range(HEADS)], bas)
    gs = each(lambda gh: jnp.concatenate(gh, axis=0), g_heads)
    gls = each(lambda gh: jnp.concatenate([jnp.broadcast_to(jnp.sum(g, axis=0, keepdims=True), (CHUNK, HEAD_DIM))
                                           for g in gh], axis=0), g_heads)
    gcs = each(lambda g: _mask_dot(lower_b, g), gs)

    def decay_of(gc):
        gc_rows = jnp.transpose(gc)
        return jnp.exp((jnp.concatenate([gc, gc], axis=1) - jnp.concatenate([gc_rows, gc_rows], axis=0)) * lower)

    decays = each(decay_of, gcs)
    egcs = each(jnp.exp, gcs)
    kbs = each(lambda kn, beta: kn * beta, kns, betas)
    ns = tuple(each(lambda kb, kn, decay: -(_bdot(kb, kn, NT) * decay * strict), kbs, kns, decays))
    if tinvs is None:
        tinvs = _unit_lower_inverse([lax.stop_gradient(n) for n in ns], eye)
    rhss = tuple(each(lambda v, beta, kb, egc: jnp.concatenate([v * beta, kb * egc], axis=1), vs, betas, kbs, egcs))
    sols = _unit_lower_solve(ns, rhss, tuple(tinvs))
    attns = each(lambda qn, kn, decay: _bdot(qn, kn, NT) * decay * lower, qns, kns, decays)
    outs = []
    for sol, qn, kn, egc, gl, gc, attn, tinv in zip(sols, qns, kns, egcs, gls, gcs, attns, tinvs):
        u, w = _split(sol, 2, 1)
        out = (u, w, qn * egc, kn * jnp.exp(gl - gc), attn, jnp.exp(gl))
        outs.append(out + (tinv,) if with_inverse else out)
    return outs


def _gdn_scan(states, u, w, qd, kt, attn, egl, z, nw):
    us, ws, qds, kts, egls = (_split(a, HEADS, 0) for a in (u, w, qd, kt, egl))
    vn = [us[h] - _bdot(ws[h], states[h]) for h in range(HEADS)]
    o = jnp.concatenate([_bdot(qds[h], states[h]) for h in range(HEADS)], axis=0)
    o = o + _bdot(attn, jnp.concatenate(vn, axis=0))
    new = [states[h] * jnp.concatenate([egls[h], egls[h]], axis=0) + _bdot(kts[h], vn[h], TN) for h in range(HEADS)]
    on = o * lax.rsqrt(jnp.mean(o * o, axis=-1, keepdims=True) + EPS) * nw
    return new, on * (z * jax.nn.sigmoid(z))


def _gdn_in_specs(step_of, chunks):
    nh = chunks * CHUNK // HALO
    main = [pl.BlockSpec((chunks * CHUNK, GDN_W), functools.partial(lambda col, i: (step_of(i), col), col))
            for col in (2, 3, 4)]
    halo = [pl.BlockSpec((HALO, GDN_W), functools.partial(lambda col, i: (jnp.maximum(step_of(i) * nh - 1, 0), col),
                                                         col)) for col in (2, 3, 4)]
    return main, halo


def _stk(width, step_of, chunks=1):
    return pl.BlockSpec((chunks * STACK, width), lambda i: (step_of(i), 0))


def _chunk_inputs(main_refs, halo_refs, k, first_step):
    rows = slice(k * CHUNK, (k + 1) * CHUNK)
    if k == 0:
        halos = [jnp.where(first_step, 0.0, h[...]) for h in halo_refs]
    else:
        halos = [m[k * CHUNK - HALO:k * CHUNK, :] for m in main_refs]
    return [jnp.concatenate([h, m[rows, :]], axis=0) for h, m in zip(halos, main_refs)]


def _gdn_prep_fwd(proj, ba, cw, pa, pd, masks):
    s = proj.shape[0]
    nc = s // CHUNK
    per = min(PREP_CHUNKS, nc)
    main, halo = _gdn_in_specs(lambda i: i, per)

    def body(xq_ref, xk_ref, xv_ref, hq_ref, hk_ref, hv_ref, ba_ref, cw_ref, pa_ref, pd_ref, mk_ref, *out_refs):
        first_step = pl.program_id(0) == 0
        rows = [[cw_ref[k:k + 1, j * GDN_W:(j + 1) * GDN_W] for k in range(CONV_K)] for j in range(3)]
        cst = [mk_ref[0], mk_ref[1], mk_ref[2]]
        xs = [_chunk_inputs((xq_ref, xk_ref, xv_ref), (hq_ref, hk_ref, hv_ref), k, first_step) for k in range(per)]
        outs = _gdn_prep([x[0] for x in xs], [x[1] for x in xs], [x[2] for x in xs],
                         [ba_ref[k * CHUNK:(k + 1) * CHUNK, :] for k in range(per)], rows[0], rows[1], rows[2],
                         pa_ref[...], pd_ref[...], cst, with_inverse=True)
        for k, out in enumerate(outs):
            for ref, val in zip(out_refs, out):
                ref[k * STACK:(k + 1) * STACK, :] = val.astype(ref.dtype)

    ident = lambda i: i
    stacked = lambda dt: _sds((nc * STACK, HEAD_DIM), dt)
    wide, thin = _stk(STACK, ident, per), _stk(HEAD_DIM, ident, per)
    return _call(body, "gdn_prep_fwd", (nc // per,),
                 main + halo + [_tok(per * CHUNK, BA_PAD), _whole(cw), _vec(BA_PAD), _vec(BA_PAD), _whole(masks)],
                 [thin] * 4 + [wide, thin, wide],
                 [stacked(F32), stacked(BF16), stacked(BF16), stacked(BF16), _sds((nc * STACK, STACK), BF16),
                  stacked(F32), _sds((nc * STACK, STACK), BF16)])(
                     proj, proj, proj, proj, proj, proj, ba, cw, pa, pd, masks)


def _gdn_prep_bwd(cts, tinv, proj, ba, cw, pa, pd, masks):
    s = proj.shape[0]
    nc = s // CHUNK
    per = min(PREP_CHUNKS, nc)
    steps = nc // per
    rev = lambda i: steps - 1 - i
    main, halo = _gdn_in_specs(rev, per)

    def body(du_ref, dw_ref, dqd_ref, dkt_ref, dattn_ref, degl_ref, tinv_ref, xq_ref, xk_ref, xv_ref, hq_ref, hk_ref,
             hv_ref, ba_ref, cw_ref, pa_ref, pd_ref, mk_ref, dp_ref, dba_ref, dcw_ref, dpar_ref, carry):
        i = pl.program_id(0)
        first_step = rev(i) == 0

        @pl.when(i == 0)
        def _():
            carry[...] = jnp.zeros_like(carry)
            dcw_ref[...] = jnp.zeros_like(dcw_ref)
            dpar_ref[...] = jnp.zeros_like(dpar_ref)

        rows = [[cw_ref[k:k + 1, j * GDN_W:(j + 1) * GDN_W] for k in range(CONV_K)] for j in range(3)]
        cst = [mk_ref[0], mk_ref[1], mk_ref[2]]
        xs = [_chunk_inputs((xq_ref, xk_ref, xv_ref), (hq_ref, hk_ref, hv_ref), k, first_step) for k in range(per)]
        stks = [slice(k * STACK, (k + 1) * STACK) for k in range(per)]
        tinvs = [tinv_ref[stk, :] for stk in stks]
        fn = lambda xqs, xks, xvs, bs, rq, rk, rv, a, d: _gdn_prep(xqs, xks, xvs, bs, rq, rk, rv, a, d, cst, tinvs=tinvs)
        _, vjp = jax.vjp(fn, [x[0] for x in xs], [x[1] for x in xs], [x[2] for x in xs],
                         [ba_ref[k * CHUNK:(k + 1) * CHUNK, :] for k in range(per)], rows[0], rows[1], rows[2],
                         pa_ref[...], pd_ref[...])
        dxqs, dxks, dxvs, dbas, drq, drk, drv, dpa, dpd = vjp(
            [tuple(ref[stk, :] for ref in (du_ref, dw_ref, dqd_ref, dkt_ref, dattn_ref, degl_ref)) for stk in stks])
        dxws = [jnp.concatenate(parts, axis=1) for parts in zip(dxqs, dxks, dxvs)]
        for k in range(per):
            dba_ref[k * CHUNK:(k + 1) * CHUNK, :] = dbas[k].astype(BF16)
        for j, dr in enumerate((drq, drk, drv)):
            for kk in range(CONV_K):
                dcw_ref[kk:kk + 1, j * GDN_W:(j + 1) * GDN_W] += dr[kk]
        dpar_ref[0:1, :] += dpa
        dpar_ref[1:2, :] += dpd
        pad = jnp.zeros((CHUNK - HALO, 3 * GDN_W), F32)
        for k in range(per):
            late = carry[...] if k == per - 1 else dxws[k + 1][0:HALO]
            dp_ref[k * CHUNK:(k + 1) * CHUNK, :] = (dxws[k][HALO:] + jnp.concatenate([pad, late], axis=0)).astype(BF16)
        carry[...] = dxws[0][0:HALO]

    acc = lambda shape: pl.BlockSpec(shape, lambda i: (0, 0))
    wide, thin = _stk(STACK, rev, per), _stk(HEAD_DIM, rev, per)
    return _call(body, "gdn_prep_bwd", (steps,),
                 [thin] * 4 + [wide, thin, wide] + main + halo
                 + [pl.BlockSpec((per * CHUNK, BA_PAD), lambda i: (rev(i), 0)), _whole(cw), _vec(BA_PAD), _vec(BA_PAD),
                    _whole(masks)],
                 [pl.BlockSpec((per * CHUNK, 3 * GDN_W), lambda i: (rev(i), 0)),
                  pl.BlockSpec((per * CHUNK, BA_PAD), lambda i: (rev(i), 0)), acc((CONV_K, 3 * GDN_W)),
                  acc((8, BA_PAD))],
                 [_sds((s, 3 * GDN_W), BF16), _sds((s, BA_PAD), BF16), _sds((CONV_K, 3 * GDN_W)), _sds((8, BA_PAD))],
                 scratch=[pltpu.VMEM((HALO, 3 * GDN_W), F32)])(
                     *cts, tinv, proj, proj, proj, proj, proj, proj, ba, cw, pa, pd, masks)


def _stack_heads(v):
    return jnp.concatenate(_split(v, HEADS, 1), axis=0)


def _unstack_heads(v):
    return jnp.concatenate(_split(v, HEADS, 0), axis=1)


def _gdn_scan_fwd(prep, proj, nw):
    s = proj.shape[0]
    nc = s // CHUNK
    per = min(SCAN_CHUNKS, nc)
    ident = lambda i: i
    srows = HEADS * HEAD_DIM

    def body(u_ref, w_ref, qd_ref, kt_ref, attn_ref, egl_ref, z_ref, nw_ref, out_ref, st_ref, state):
        @pl.when(pl.program_id(0) == 0)
        def _():
            state[...] = jnp.zeros_like(state)

        states = [state[h * HEAD_DIM:(h + 1) * HEAD_DIM, :] for h in range(HEADS)]
        for k in range(per):
            stk, tok = slice(k * STACK, (k + 1) * STACK), slice(k * CHUNK, (k + 1) * CHUNK)
            for h in range(HEADS):
                st_ref[k * srows + h * HEAD_DIM:k * srows + (h + 1) * HEAD_DIM, :] = states[h]
            states, out = _gdn_scan(states, u_ref[stk, :], w_ref[stk, :], qd_ref[stk, :], kt_ref[stk, :],
                                    attn_ref[stk, :], egl_ref[stk, :], _stack_heads(z_ref[tok, :]), nw_ref[...])
            out_ref[tok, :] = _unstack_heads(out)
        for h in range(HEADS):
            state[h * HEAD_DIM:(h + 1) * HEAD_DIM, :] = states[h]

    thin, wide = _stk(HEAD_DIM, ident, per), _stk(STACK, ident, per)
    return _call(body, "gdn_scan_fwd", (nc // per,),
                 [thin] * 4 + [wide, thin, _tok(per * CHUNK, GDN_W, col=5), _vec(HEAD_DIM)],
                 [_tok(per * CHUNK, GDN_W), pl.BlockSpec((per * srows, HEAD_DIM), lambda i: (i, 0))],
                 [_sds((s, GDN_W)), _sds((nc * srows, HEAD_DIM))],
                 scratch=[pltpu.VMEM((srows, HEAD_DIM), F32)])(*prep, proj, nw)


def _gdn_scan_bwd(dout, prep, st, proj, nw):
    s = proj.shape[0]
    nc = s // CHUNK
    per = min(SCAN_CHUNKS, nc)
    steps = nc // per
    rev = lambda i: steps - 1 - i
    srows = HEADS * HEAD_DIM

    def body(do_ref, u_ref, w_ref, qd_ref, kt_ref, attn_ref, egl_ref, st_ref, z_ref, nw_ref,
             du_ref, dw_ref, dqd_ref, dkt_ref, dattn_ref, degl_ref, dz_ref, dnw_ref, dstate):
        @pl.when(pl.program_id(0) == 0)
        def _():
            dstate[...] = jnp.zeros_like(dstate)
            dnw_ref[...] = jnp.zeros_like(dnw_ref)

        dnew = [dstate[h * HEAD_DIM:(h + 1) * HEAD_DIM, :] for h in range(HEADS)]
        for k in reversed(range(per)):
            stk, tok = slice(k * STACK, (k + 1) * STACK), slice(k * CHUNK, (k + 1) * CHUNK)
            states = [st_ref[k * srows + h * HEAD_DIM:k * srows + (h + 1) * HEAD_DIM, :] for h in range(HEADS)]
            f32 = lambda ref: ref[stk, :].astype(F32)
            _, vjp = jax.vjp(_gdn_scan, states, u_ref[stk, :], f32(w_ref), f32(qd_ref), f32(kt_ref), f32(attn_ref),
                             egl_ref[stk, :], _stack_heads(z_ref[tok, :]), nw_ref[...])
            dnew, du, dw, dqd, dkt, dattn, degl, dz, dnw = vjp((dnew, _stack_heads(do_ref[tok, :])))
            for ref, val in zip((du_ref, dw_ref, dqd_ref, dkt_ref, dattn_ref, degl_ref),
                                (du, dw, dqd, dkt, dattn, degl)):
                ref[stk, :] = val
            dz_ref[tok, :] = _unstack_heads(dz).astype(BF16)
            dnw_ref[0:1, :] += dnw
        for h in range(HEADS):
            dstate[h * HEAD_DIM:(h + 1) * HEAD_DIM, :] = dnew[h]

    tokr = lambda n, col=0: pl.BlockSpec((per * CHUNK, n), lambda i: (rev(i), col))
    thin, wide = _stk(HEAD_DIM, rev, per), _stk(STACK, rev, per)
    return _call(body, "gdn_scan_bwd", (steps,),
                 [tokr(GDN_W)] + [thin] * 4 + [wide, thin, pl.BlockSpec((per * srows, HEAD_DIM), lambda i: (rev(i), 0)),
                                               tokr(GDN_W, 5), _vec(HEAD_DIM)],
                 [thin] * 4 + [wide, thin, tokr(GDN_W), pl.BlockSpec((8, HEAD_DIM), lambda i: (0, 0))],
                 [_sds((nc * STACK, HEAD_DIM))] * 4 + [_sds((nc * STACK, STACK)), _sds((nc * STACK, HEAD_DIM)),
                                                       _sds((s, GDN_W), BF16), _sds((8, HEAD_DIM))],
                 scratch=[pltpu.VMEM((srows, HEAD_DIM), F32)])(dout, *prep, st, proj, nw)


def _wo_specs(l):
    half = N_DEV // 2
    return [pl.BlockSpec((half, 1, D_MODEL // N_DEV, D_MODEL), functools.partial(lambda k, *_: (k, l, 0, 0), k))
            for k in range(2)]


def _wo_half(ref):
    return ref[:, 0].reshape(ref.shape[0] * ref.shape[2], ref.shape[3])


def _out_mlp_fwd(ol, og, x, wo, g1, nw2, sc2, sh2, g2, wup, wdn, l):
    s = x.shape[0]
    t = _tile(s, 512)
    nj = wup.shape[0] // MLP_BLOCKS
    fc = wup.shape[3]

    def body(ol_ref, og_ref, x_ref, wol_ref, wog_ref, g1_ref, nw_ref, sc_ref, sh_ref, g2_ref, wup_ref, wdn_ref,
             x1_ref, mix_ref, ff_ref, x2_ref, h2_s, acc_s):
        j = pl.program_id(1)

        @pl.when(j == 0)
        def _():
            mix = _bdot(ol_ref[...], _wo_half(wol_ref)) + _bdot(og_ref[...], _wo_half(wog_ref))
            x1 = x_ref[...] + g1_ref[...] * mix
            mix_ref[...] = mix.astype(BF16)
            x1_ref[...] = x1
            h2, _, _ = _modulated_norm(x1, nw_ref[...], sc_ref[...], sh_ref[...])
            h2_s[...] = h2.astype(BF16)
            acc_s[...] = jnp.zeros_like(acc_s)

        part = None
        for b in range(MLP_BLOCKS):
            up = _bdot(h2_s[...], wup_ref[b, 0])
            down = _bdot(jnp.square(jnp.maximum(up, 0.0)), wdn_ref[b, 0])
            part = down if part is None else part + down
        acc_s[...] += part

        @pl.when(j == nj - 1)
        def _():
            ff_ref[...] = acc_s[...].astype(BF16)
            x2_ref[...] = x1_ref[...] + g2_ref[...] * acc_s[...]

    tk = lambda n: pl.BlockSpec((t, n), lambda i, j: (i, 0))
    return _call(body, "out_mlp_fwd", (s // t, nj),
                 [tk(LRU_W), tk(GDN_W), tk(D_MODEL)] + _wo_specs(l) + [_vec(D_MODEL)] * 5
                 + [pl.BlockSpec((MLP_BLOCKS, 1, D_MODEL, fc), lambda i, j: (j, l, 0, 0)),
                    pl.BlockSpec((MLP_BLOCKS, 1, fc, D_MODEL), lambda i, j: (j, l, 0, 0))],
                 [tk(D_MODEL)] * 4,
                 [_sds((s, D_MODEL)), _sds((s, D_MODEL), BF16), _sds((s, D_MODEL), BF16), _sds((s, D_MODEL))],
                 scratch=[pltpu.VMEM((t, D_MODEL), BF16), pltpu.VMEM((t, D_MODEL), F32)])(
                     ol, og, x, wo, wo, g1, nw2, sc2, sh2, g2, wup, wdn)


def _mlp_bwd(dx2, x1, ff, nw2, sc2, sh2, g2, wup, wdn, l):
    s = x1.shape[0]
    t = _tile(s, 512)
    nj = wup.shape[0] // MLP_BLOCKS
    fc = wup.shape[3]

    def body(dx2_ref, x1_ref, ff_ref, nw_ref, sc_ref, sh_ref, g2_ref, wup_ref, wdn_ref,
             act_ref, dup_ref, h2_ref, dff_ref, dx1_ref, rows_ref, dh2_s):
        i, j = pl.program_id(0), pl.program_id(1)

        @pl.when((i == 0) & (j == 0))
        def _():
            rows_ref[...] = jnp.zeros_like(rows_ref)

        @pl.when(j == 0)
        def _():
            h2, _, _ = _modulated_norm(x1_ref[...], nw_ref[...], sc_ref[...], sh_ref[...])
            h2_ref[...] = h2.astype(BF16)
            dx2 = dx2_ref[...]
            dff_ref[...] = (dx2 * g2_ref[...]).astype(BF16)
            rows_ref[2:3, :] += jnp.sum(dx2 * ff_ref[...].astype(F32), axis=0, keepdims=True)
            dh2_s[...] = jnp.zeros_like(dh2_s)

        part = None
        for b in range(MLP_BLOCKS):
            cols = slice(b * fc, (b + 1) * fc)
            up = _bdot(h2_ref[...], wup_ref[b, 0])
            ru = jnp.maximum(up, 0.0)
            act_ref[:, cols] = (ru * ru).astype(BF16)
            dup = (_bdot(dff_ref[...], wdn_ref[b, 0], NT) * (2.0 * ru)).astype(BF16)
            dup_ref[:, cols] = dup
            back = _bdot(dup, wup_ref[b, 0], NT)
            part = back if part is None else part + back
        dh2_s[...] += part

        @pl.when(j == nj - 1)
        def _():
            xv = x1_ref[...]
            _, n, r = _modulated_norm(xv, nw_ref[...], sc_ref[...], sh_ref[...])
            dx, dsh, dsc, dnw = _modulated_norm_bwd(dh2_s[...], xv, n, r, nw_ref[...], sc_ref[...])
            dx1_ref[...] = dx2_ref[...] + dx
            rows_ref[0:1, :] += dsh
            rows_ref[1:2, :] += dsc
            rows_ref[3:4, :] += dnw

    tk = lambda n: pl.BlockSpec((t, n), lambda i, j: (i, 0))
    tj = pl.BlockSpec((t, MLP_BLOCKS * fc), lambda i, j: (i, j))
    return _call(body, "mlp_bwd", (s // t, nj),
                 [tk(D_MODEL)] * 3 + [_vec(D_MODEL)] * 4
                 + [pl.BlockSpec((MLP_BLOCKS, 1, D_MODEL, fc), lambda i, j: (j, l, 0, 0)),
                    pl.BlockSpec((MLP_BLOCKS, 1, fc, D_MODEL), lambda i, j: (j, l, 0, 0))],
                 [tj, tj, tk(D_MODEL), tk(D_MODEL), tk(D_MODEL), pl.BlockSpec((8, D_MODEL), lambda i, j: (0, 0))],
                 [_sds((s, D_FF), BF16), _sds((s, D_FF), BF16), _sds((s, D_MODEL), BF16),
                  _sds((s, D_MODEL), BF16), _sds((s, D_MODEL)), _sds((8, D_MODEL))],
                 scratch=[pltpu.VMEM((t, D_MODEL), F32)], vmem_mb=56)(dx2, x1, ff, nw2, sc2, sh2, g2, wup, wdn)


def _outproj_bwd(dx1, mix, g1, wo, l):
    s = dx1.shape[0]
    t = _tile(s, 512)

    def body(dx1_ref, mix_ref, g1_ref, wol_ref, wog_ref, dmix_ref, dol_ref, dog_ref, rows_ref):
        @pl.when(pl.program_id(0) == 0)
        def _():
            rows_ref[...] = jnp.zeros_like(rows_ref)

        dx1v = dx1_ref[...]
        rows_ref[0:1, :] += jnp.sum(dx1v * mix_ref[...].astype(F32), axis=0, keepdims=True)
        dmix = (dx1v * g1_ref[...]).astype(BF16)
        dmix_ref[...] = dmix
        dol_ref[...] = _bdot(dmix, _wo_half(wol_ref), NT)
        dog_ref[...] = _bdot(dmix, _wo_half(wog_ref), NT)

    return _call(body, "outproj_bwd", (s // t,),
                 [_tok(t, D_MODEL), _tok(t, D_MODEL), _vec(D_MODEL)] + _wo_specs(l),
                 [_tok(t, D_MODEL), _tok(t, LRU_W), _tok(t, GDN_W), pl.BlockSpec((8, D_MODEL), lambda i: (0, 0))],
                 [_sds((s, D_MODEL), BF16), _sds((s, LRU_W)), _sds((s, GDN_W)), _sds((8, D_MODEL))])(dx1, mix, g1, wo, wo)


def _tn_matmul(a, b, name, out=None, l=0, blocked=False, row_block=0):
    s, m = a.shape
    n = b.shape[1]
    ts, bm = _tile(s, 2048), _tile(m, 1024)
    bn = next(w for w in (512, 640, 384, 256, 128) if n % w == 0)

    def body(a_ref, b_ref, *rest):
        o_ref = rest[-1]

        @pl.when(pl.program_id(2) == 0)
        def _():
            o_ref[...] = jnp.zeros_like(o_ref)

        acc = _bdot(a_ref[...], b_ref[...], TN)
        o_ref[...] += acc.reshape(o_ref.shape)

    in_specs = [pl.BlockSpec((ts, bm), lambda i, j, k: (k, i)), pl.BlockSpec((ts, bn), lambda i, j, k: (k, j))]
    grid = (m // bm, n // bn, s // ts)
    if out is None:
        return _call(body, name, grid, in_specs, pl.BlockSpec((bm, bn), lambda i, j, k: (i, j)), _sds((m, n)))(a, b)
    if blocked:
        out_spec = pl.BlockSpec((1, 1, bm, bn), lambda i, j, k: (l, j, i, 0))
    else:
        out_spec = pl.BlockSpec((1, bm, bn), lambda i, j, k: (l, i + row_block * (m // bm), j))
    return _call(body, name, grid, in_specs + [pl.BlockSpec(memory_space=pl.ANY)], out_spec,
                 _sds(out.shape), aliases={2: 0})(a, b, out)


def _final_fwd_bwd(x, target, fw):
    s = x.shape[0]
    t = _tile(s, 512)

    def body(x_ref, tg_ref, fw_ref, dx_ref, rows_ref):
        @pl.when(pl.program_id(0) == 0)
        def _():
            rows_ref[...] = jnp.zeros_like(rows_ref)

        xv = x_ref[...]
        fwv = fw_ref[...]
        r = lax.rsqrt(jnp.mean(xv * xv, axis=-1, keepdims=True) + EPS)
        err = xv * r * fwv - tg_ref[...]
        part = 0.5 * jnp.sum(jnp.mean(err * err, axis=-1, keepdims=True), axis=0, keepdims=True)
        rows_ref[1:2, :] += jnp.broadcast_to(part, (1, D_MODEL))
        dy = err * (1.0 / D_MODEL)
        rows_ref[0:1, :] += jnp.sum(dy * xv * r, axis=0, keepdims=True)
        dxn = dy * fwv
        dx_ref[...] = r * dxn - xv * (r * r * r) * jnp.mean(dxn * xv, axis=-1, keepdims=True)

    return _call(body, "final_fwd_bwd", (s // t,),
                 [_tok(t, D_MODEL), _tok(t, D_MODEL), _vec(D_MODEL)],
                 [_tok(t, D_MODEL), pl.BlockSpec((8, D_MODEL), lambda i: (0, 0))],
                 [_sds((s, D_MODEL)), _sds((8, D_MODEL))])(x, target, fw)


def _adamw(w, g, m, v):
    m = ADAM_B1 * m + (1.0 - ADAM_B1) * g
    v = ADAM_B2 * v + (1.0 - ADAM_B2) * (g * g)
    m_hat = m / (1.0 - ADAM_B1 ** ADAM_STEP)
    v_hat = v / (1.0 - ADAM_B2 ** ADAM_STEP)
    return -ADAM_LR * (m_hat / (jnp.sqrt(v_hat) + ADAM_EPS) + ADAM_WD * w), m, v


def _mod_local(c_all, wmod, bmod_cols):
    nl, _, cols = wmod.shape

    def body(c_ref, w_ref, b_ref, o_ref):
        cv = c_ref[...]
        o_ref[0] = _bdot(cv * jax.nn.sigmoid(cv), w_ref[0]) + b_ref[0]

    return _call(body, "mod_local", (nl,),
                 [_whole(c_all), pl.BlockSpec((1, D_MODEL, cols), lambda l: (l, 0, 0)),
                  pl.BlockSpec((1, 1, cols), lambda l: (l, 0, 0))],
                 pl.BlockSpec((1, N_DEV, cols), lambda l: (l, 0, 0)), _sds((nl, N_DEV, cols)))(c_all, wmod, bmod_cols)


def _wmod_update(c_all, dmod_cols, w, m, v):
    nl, _, cols = w.shape

    def body(c_ref, d_ref, w_ref, m_ref, v_ref, g_ref, dl_ref, nm_ref, nv_ref):
        cv = c_ref[...]
        g = _bdot(cv * jax.nn.sigmoid(cv), d_ref[0], TN)
        g_ref[0] = g
        dl_ref[0], nm_ref[0], nv_ref[0] = _adamw(w_ref[0], g, m_ref[0], v_ref[0])

    wspec = pl.BlockSpec((1, D_MODEL, cols), lambda l: (l, 0, 0))
    return _call(body, "wmod_update", (nl,),
                 [_whole(c_all), pl.BlockSpec((1, N_DEV, cols), lambda l: (l, 0, 0)), wspec, wspec, wspec],
                 [wspec] * 4, [_sds(w.shape)] * 4)(c_all, dmod_cols, w, m, v)


def _sum_devices(gathered):
    _, r, _ = gathered.shape

    def body(g_ref, o_ref):
        acc = g_ref[0]
        for d in range(1, N_DEV):
            acc = acc + g_ref[d]
        o_ref[...] = acc

    return _call(body, "sum_devices", (1,), [_whole(gathered)], pl.BlockSpec((r, LANES), lambda i: (0, 0)),
                 _sds((r, LANES)))(gathered)


def _adam_flat(w, g, m, v):
    r = w.shape[0]

    def body(w_ref, g_ref, m_ref, v_ref, dl_ref, nm_ref, nv_ref):
        dl_ref[...], nm_ref[...], nv_ref[...] = _adamw(w_ref[...], g_ref[...], m_ref[...], v_ref[...])

    spec = pl.BlockSpec((r, LANES), lambda i: (0, 0))
    return _call(body, "adam_small", (1,), [spec] * 4, [spec] * 3, [_sds((r, LANES))] * 3)(w, g, m, v)


def _pair_add(x, p, others):
    _, r, c = x.shape
    tr = _tile(r, 128 if c > 512 else 256)

    def body(others_ref, x_ref, p_ref, o_ref):
        o_ref[...] = (x_ref[...] + p_ref[...]).astype(BF16)

    return _call(body, "pair_add", (3, r // tr),
                 [pl.BlockSpec((1, tr, c), lambda q, i, others_ref: (others_ref[q], i, 0)),
                  pl.BlockSpec((1, tr, c), lambda q, i, others_ref: (others_ref[3 + q], i, 0))],
                 pl.BlockSpec((1, tr, c), lambda q, i, others_ref: (q, i, 0)), _sds((3, r, c), BF16),
                 prefetch=1)(others, x, p)


def _reduce_adam(x, p, q, place, w, m, v, l, outs):
    _, r, c = x.shape
    tr = _tile(r, 128 if c > 512 else 256)

    def body(place_ref, x_ref, p_ref, q_ref, w_ref, m_ref, v_ref, *rest):
        g_ref, dl_ref, nm_ref, nv_ref = rest[-4:]
        g = (((x_ref[0] + p_ref[0]) + q_ref[0].astype(F32)) + q_ref[1].astype(F32)) + q_ref[2].astype(F32)
        g_ref[0] = g
        dl_ref[0], nm_ref[0], nv_ref[0] = _adamw(w_ref[0], g, m_ref[0], v_ref[0])

    flat = pl.BlockSpec((1, tr, c), lambda i, place_ref: (l, i, 0))
    through = pl.BlockSpec(memory_space=pl.ANY)
    return _call(body, "reduce_adam", (r // tr,),
                 [pl.BlockSpec((1, tr, c), lambda i, place_ref: (place_ref[0], i, 0)),
                  pl.BlockSpec((1, tr, c), lambda i, place_ref: (place_ref[1], i, 0)),
                  pl.BlockSpec((3, tr, c), lambda i, place_ref: (0, i, 0)), flat, flat, flat] + [through] * 4,
                 [flat] * 4, [_sds(w.shape)] * 4, prefetch=1, aliases={7 + k: k for k in range(4)})(
                     place, x, p, q, w, m, v, *outs)


def _place():
    return lax.axis_index("x"), lax.axis_index("y"), lax.axis_index("c")


def _all_gather(xs, name, space):
    n = len(xs)

    def body(*refs):
        x_refs, o_refs = refs[:n], refs[n:2 * n]
        send_sems, recv_sems, local_sems = refs[2 * n:]
        x, y, c = _place()
        me, sibling = (x, y, c), (x, y, 1 - c)
        chips = [(1 - x, y), (x, 1 - y), (1 - x, 1 - y)]

        def blk(a, p):
            return o_refs[a].at[4 * p[0] + 2 * p[1] + p[2]]

        def copy(a, k, block, to, src=None):
            return pltpu.make_async_remote_copy(
                src_ref=blk(a, block) if src is None else src, dst_ref=blk(a, block),
                send_sem=send_sems.at[a, k], recv_sem=recv_sems.at[a, k], device_id=to, device_id_type=MESH)

        mine = [pltpu.make_async_copy(x_refs[a], blk(a, me), local_sems.at[a]) for a in range(n)]
        for cp in mine:
            cp.start()
        first = []
        for a in range(n):
            first.append(copy(a, 0, me, sibling, src=x_refs[a]))
            first += [copy(a, 1 + j, me, (*chip, c), src=x_refs[a]) for j, chip in enumerate(chips)]
        for cp in first:
            cp.start()
        passed = []
        for j, chip in enumerate(chips):
            for a in range(n):
                copy(a, 1 + j, (*chip, c), me).wait_recv()
                cp = copy(a, 4 + j, (*chip, c), sibling)
                cp.start()
                passed.append(cp)
        for a in range(n):
            copy(a, 0, sibling, me).wait_recv()
        for j, chip in enumerate(chips):
            for a in range(n):
                copy(a, 4 + j, (*chip, 1 - c), me).wait_recv()
        for cp in first + passed:
            cp.wait_send()
        for cp in mine:
            cp.wait()

    spec = pl.BlockSpec(memory_space=space)
    return pl.pallas_call(
        body, name=name, out_shape=[_sds((N_DEV,) + a.shape, a.dtype) for a in xs],
        in_specs=[spec] * n, out_specs=[spec] * n,
        scratch_shapes=[pltpu.SemaphoreType.DMA((n, 7)), pltpu.SemaphoreType.DMA((n, 7)),
                        pltpu.SemaphoreType.DMA((n,))])(*xs)


_HBM_SPEC = pl.BlockSpec(memory_space=pltpu.HBM)
_SEM_SPEC = pl.BlockSpec(memory_space=pltpu.SEMAPHORE)
_EFFECT = pltpu.SideEffectType.DATAFLOW_SIDE_EFFECTING


def _descriptors(plan, src_refs, land_refs, send_sems, recv_sems, which=None):
    return [pltpu.make_async_remote_copy(src_ref=s, dst_ref=d, send_sem=send_sems.at[k], recv_sem=recv_sems.at[k],
                                         device_id=dev, device_id_type=MESH)
            for k, (s, d, dev) in enumerate(plan(src_refs, land_refs)) if which is None or k in which]


def _split_start(name, plan, n, srcs, lands, after):
    ns, nb = len(srcs), len(srcs) + len(lands)
    after = list(after) if isinstance(after, (list, tuple)) else [after]
    sems = nb + len(after)

    def body(*refs):
        for cp in _descriptors(plan, refs[:ns], refs[ns:nb], refs[sems], refs[sems + 1]):
            cp.start()
        refs[-1][...] = jnp.zeros_like(refs[-1])

    bufs = [pltpu.with_memory_space_constraint(a, pltpu.HBM) for a in list(srcs) + list(lands)]
    outs = pl.pallas_call(
        body, name=name,
        out_shape=(pltpu.SemaphoreType.DMA((n,)), pltpu.SemaphoreType.DMA((n,)))
        + tuple(pltpu.HBM(a.shape, a.dtype) for a in bufs) + (_sds((8, LANES)),),
        in_specs=[_HBM_SPEC] * nb + [pl.BlockSpec(memory_space=pl.ANY)] * len(after),
        out_specs=(_SEM_SPEC, _SEM_SPEC) + (_HBM_SPEC,) * nb + (pl.BlockSpec(memory_space=pltpu.VMEM),),
        input_output_aliases={i: 2 + i for i in range(nb)},
        compiler_params=pltpu.CompilerParams(has_side_effects=_EFFECT))(*bufs, *after)
    return dict(send=outs[0], recv=outs[1], srcs=list(outs[2:2 + ns]), lands=list(outs[2 + ns:2 + nb]), token=outs[-1])


def _split_wait(name, plan, flight, which, after):
    srcs, lands = flight["srcs"], flight["lands"]
    ns, nb = len(srcs), len(srcs) + len(lands)

    def body(*refs):
        for cp in _descriptors(plan, refs[:ns], refs[ns:nb], refs[nb], refs[nb + 1], set(which)):
            cp.wait_send()
            cp.wait_recv()

    outs = pl.pallas_call(
        body, name=name, out_shape=tuple(pltpu.HBM(a.shape, a.dtype) for a in srcs + lands),
        in_specs=[_HBM_SPEC] * nb + [_SEM_SPEC, _SEM_SPEC, pl.BlockSpec(memory_space=pl.ANY)],
        out_specs=(_HBM_SPEC,) * nb, input_output_aliases={i: i for i in range(nb)},
        compiler_params=pltpu.CompilerParams(has_side_effects=_EFFECT))(*srcs, *lands, flight["send"], flight["recv"],
                                                                       after)
    return dict(flight, srcs=list(outs[:ns]), lands=list(outs[ns:nb]))


GATHER_PEERS = N_DEV - 1


def _gather_plan(items):
    def plan(src_refs, land_refs):
        x, y, c = _place()
        me = 4 * x + 2 * y + c
        out = []
        for a, l in items:
            for r in range(1, N_DEV):
                peer = (1 - x if r & 4 else x, 1 - y if r & 2 else y, 1 - c if r & 1 else c)
                out.append((src_refs[a].at[l], land_refs[a].at[me, l], peer))
        return out

    return plan


def _pair_plan(narr):
    def plan(src_refs, land_refs):
        x, y, c = _place()
        return [(src_refs[a].at[2 * q + (1 - c)], land_refs[a].at[q], (x, y, 1 - c))
                for a in range(narr) for q in range(4)]

    return plan


def _chip_plan(narr):
    def plan(src_refs, land_refs):
        x, y, c = _place()
        chips = [(1 - x, y), (x, 1 - y), (1 - x, 1 - y)]
        return [(src_refs[a].at[r], land_refs[a].at[r], (*chip, c)) for a in range(narr) for r, chip in enumerate(chips)]

    return plan


class _GradReducer:
    def __init__(self, tag, names, w, mom, var, place, others):
        self.tag, self.names, self.w, self.mom, self.var, self.place = tag, names, w, mom, var, place
        self.others = others
        self.outs = {k: [lax.empty(w[k].shape, F32) for _ in range(4)] for k in names}
        self.n = len(names)

    def start(self, l, grads):
        self.l, self.xs = l, [grads[k] for k in self.names]
        lands = [lax.empty((4,) + a.shape[1:], F32) for a in self.xs]
        self.pair = _split_start(f"{self.tag}_pair_start{l}", _pair_plan(self.n), 4 * self.n, self.xs, lands, ())
        return self.pair["token"][0, 0]

    def middle(self, after):
        self.pair = _split_wait(f"{self.tag}_pair_wait{self.l}", _pair_plan(self.n), self.pair, range(4 * self.n),
                                after)
        self.xs, self.ps = self.pair["srcs"], self.pair["lands"]
        ys = [_pair_add(x, p, self.others) for x, p in zip(self.xs, self.ps)]
        lands = [lax.empty((3,) + a.shape[1:], BF16) for a in ys]
        self.chip = _split_start(f"{self.tag}_chip_start{self.l}", _chip_plan(self.n), 3 * self.n, ys, lands, ())
        return self.chip["token"][0, 0]

    def finish(self, after):
        chip = _split_wait(f"{self.tag}_chip_wait{self.l}", _chip_plan(self.n), self.chip, range(3 * self.n), after)
        for k, x, p, q in zip(self.names, self.xs, self.ps, chip["lands"]):
            self.outs[k] = _reduce_adam(x, p, q, self.place, self.w[k], self.mom[k], self.var[k], self.l, self.outs[k])


def _size(shape):
    size = 1
    for d in shape:
        size *= d
    return size


def _slab_rows(shape):
    return -(-_size(shape) // (8 * LANES)) * 8


def _pack(arrs):
    parts = []
    for a in arrs:
        flat = a.reshape(-1).astype(F32)
        parts.append(jnp.pad(flat, (0, _slab_rows(a.shape) * LANES - flat.shape[0])).reshape(-1, LANES))
    return jnp.concatenate(parts, axis=0)


def _unpack(slab, shapes):
    out, off = [], 0
    for shp in shapes:
        rows = _slab_rows(shp)
        out.append(slab[off:off + rows].reshape(-1)[:_size(shp)].reshape(shp))
        off += rows
    return out


def _dense_blocks(w):
    eye = jnp.eye(LRU_BLOCKS, dtype=w.dtype)
    return (eye[:, None, :, None] * w[:, :, None, :]).reshape(LRU_W, LRU_W)


def _diag_blocks(dense):
    return jnp.stack([dense[g * LRU_BLOCK:(g + 1) * LRU_BLOCK, g * LRU_BLOCK:(g + 1) * LRU_BLOCK]
                      for g in range(LRU_BLOCKS)])


def _alpha_lanes(v):
    return jnp.zeros((1, BA_PAD), F32).at[0, HEADS:2 * HEADS].set(v)


def _local_step(x, target, mod, p, fetch, reducers=None):
    nl = mod.shape[0]
    row = lambda v: v.reshape(1, -1)
    masks = _gdn_masks()
    saved = []
    xc = x
    for l in range(nl):
        win, wba, lin = fetch(l, "in", xc)
        mv = [row(mod[l, k * D_MODEL:(k + 1) * D_MODEL]) for k in range(N_MOD)]
        sh1, sc1, g1, sh2, sc2, g2 = mv
        nw1, nw2 = row(p["norm_mix_w"][l]), row(p["norm_mlp_w"][l])
        wa, wx = _dense_blocks(p["lru_gate_a_w"][l]).astype(BF16), _dense_blocks(p["lru_gate_x_w"][l]).astype(BF16)
        lru_args = (p["lru_conv_w"][l], row(p["lru_conv_b"][l]), wa, wx, row(p["lru_gate_a_b"][l]),
                    row(p["lru_gate_x_b"][l]), row(p["lru_lambda"][l]), row(p["lru_norm_w"][l]))
        gdn_args = (p["gdn_conv_w"][l], _alpha_lanes(p["gdn_a_log"][l]), _alpha_lanes(p["gdn_dt_bias"][l]), masks)
        gnw = row(p["gdn_norm_w"][l])
        proj, ba = _inproj_fwd(xc, nw1, sc1, sh1, win, wba, lin)
        ol, hs = _lru_fwd(proj, *lru_args)
        *prep, tinv = _gdn_prep_fwd(proj, ba, *gdn_args)
        og, st = _gdn_scan_fwd(prep, proj, gnw)
        wo, wup, wdn = fetch(l, "rest", og)
        x1, mix, ff, x2 = _out_mlp_fwd(ol, og, xc, wo, g1, nw2, sc2, sh2, g2, wup, wdn, l)
        saved.append(dict(x=xc, mv=mv, nw1=nw1, nw2=nw2, lru_args=lru_args, gdn_args=gdn_args, gnw=gnw, proj=proj,
                          ba=ba, ol=ol, hs=hs, prep=prep, tinv=tinv, og=og, st=st, x1=x1, mix=mix, ff=ff,
                          win=win, wba=wba, lin=lin))
        xc = x2

    dx, frows = _final_fwd_bwd(xc, target, row(p["final_norm_w"]))
    loss_part = frows[1, 0]
    small = {k: [None] * nl for k in ("norm_mix_w", "norm_mlp_w", "lru_conv_w", "lru_conv_b", "lru_gate_a_w",
                                      "lru_gate_a_b", "lru_gate_x_w", "lru_gate_x_b", "lru_lambda", "lru_norm_w",
                                      "gdn_conv_w", "gdn_a_log", "gdn_dt_bias", "gdn_norm_w")}
    fc = D_FF // N_DEV
    big = [None] * nl
    dmod = [None] * nl
    mlp_red, mix_red = reducers or (None, None)
    busy = False
    for l in reversed(range(nl)):
        sv = saved[l]
        sh1, sc1, g1, sh2, sc2, g2 = sv["mv"]
        gnw = sv["gnw"]
        if busy:
            g2 = g2 + started
        act, dup, h2b, dffb, dx1, rows2 = _mlp_bwd(dx, sv["x1"], sv["ff"], sv["nw2"], sc2, sh2, g2, wup, wdn, l)
        if busy:
            g1 = g1 + mix_red.middle(dx1)
        g_up = _tn_matmul(h2b, dup, "grad_w_up", out=lax.empty((1, N_DEV, D_MODEL, fc), F32), blocked=True)[0]
        g_down = _tn_matmul(act, dffb, "grad_w_down", out=lax.empty((1, D_FF, D_MODEL), F32))
        g_down = g_down.reshape(N_DEV, fc, D_MODEL)
        if mlp_red is not None:
            g1 = g1 + mlp_red.start(l, dict(w_up=g_up, w_down=g_down))
        dmix, dol, dog, rows1 = _outproj_bwd(dx1, sv["mix"], g1, wo, l)
        g_out = _tn_matmul(sv["ol"], dmix, "grad_w_out_lru", out=lax.empty((1, D_MODEL, D_MODEL), F32))
        g_out = _tn_matmul(sv["og"], dmix, "grad_w_out_gdn", out=g_out, row_block=1)
        dpl, dwa, dwx, lrows = _lru_bwd(dol, sv["proj"], sv["hs"], *sv["lru_args"])
        if mlp_red is not None:
            gnw = gnw + mlp_red.middle(dpl)
        *cts, dpz, gnrow = _gdn_scan_bwd(dog, sv["prep"], sv["st"], sv["proj"], gnw)
        dpq, dba, dcw, dpar = _gdn_prep_bwd(cts, sv["tinv"], sv["proj"], sv["ba"], *sv["gdn_args"])
        dx, hb, rows0 = _inproj_bwd(dpl, dpq, dpz, dba, sv["x"], dx1, sv["nw1"], sc1, sh1, sv["win"], sv["wba"],
                                    sv["lin"])
        if busy:
            mix_red.finish(dx)
        if mlp_red is not None:
            mlp_red.finish(dx)
        dproj = jnp.concatenate([dpl, dpq, dpz, dba], axis=1)
        g_in = jnp.transpose(_tn_matmul(hb, dproj, "grad_w_in")[:, :IN_COLS].reshape(
            D_MODEL, N_DEV, IN_COLS // N_DEV), (1, 0, 2))
        big[l] = dict(w_in=g_in, w_out=g_out.reshape(N_DEV, D_MODEL // N_DEV, D_MODEL), w_up=g_up,
                      w_down=g_down)
        if mix_red is not None:
            started, busy = mix_red.start(l, big[l]), True
        dmod[l] = jnp.concatenate([rows0[0], rows0[1], rows1[0], rows2[0], rows2[1], rows2[2]])
        small["norm_mix_w"][l], small["norm_mlp_w"][l] = rows0[2], rows2[3]
        small["lru_conv_w"][l], small["lru_conv_b"][l] = lrows[8:8 + CONV_K], lrows[0]
        small["lru_gate_a_w"][l], small["lru_gate_x_w"][l] = _diag_blocks(dwa), _diag_blocks(dwx)
        small["lru_gate_a_b"][l], small["lru_gate_x_b"][l] = lrows[1], lrows[2]
        small["lru_lambda"][l], small["lru_norm_w"][l] = lrows[3], lrows[4]
        small["gdn_conv_w"][l] = dcw
        small["gdn_a_log"][l], small["gdn_dt_bias"][l] = dpar[0, HEADS:2 * HEADS], dpar[1, HEADS:2 * HEADS]
        small["gdn_norm_w"][l] = gnrow[0]
    if busy:
        mix_red.middle(dx)
        mix_red.finish(dx)
    small = {k: jnp.stack(v) for k, v in small.items()}
    small["final_norm_w"] = frows[0]
    return loss_part, dx, big, small, jnp.stack(dmod)


SMALL_REPLICATED = ("norm_mix_w", "norm_mlp_w", "b_mod", "lru_conv_b", "lru_gate_a_w", "lru_gate_a_b", "lru_gate_x_w",
                    "lru_gate_x_b", "lru_lambda", "lru_norm_w", "gdn_a_log", "gdn_dt_bias", "gdn_norm_w",
                    "final_norm_w")
SMALL_SHARDED = ("lru_conv_w", "gdn_conv_w")
WEIGHT_ORDER = ("norm_mix_w", "norm_mlp_w", "w_mod", "b_mod", "w_in", "lru_conv_w", "lru_conv_b", "lru_gate_a_w",
                "lru_gate_a_b", "lru_gate_x_w", "lru_gate_x_b", "lru_lambda", "lru_norm_w", "gdn_conv_w", "gdn_a_log",
                "gdn_dt_bias", "gdn_norm_w", "w_out", "w_up", "w_down", "final_norm_w")


def kernel(x, c, norm_mix_w, norm_mlp_w, w_mod, b_mod, w_in, lru_conv_w, lru_conv_b, lru_gate_a_w, lru_gate_a_b, lru_gate_x_w, lru_gate_x_b, lru_lambda, lru_norm_w, gdn_conv_w, gdn_a_log, gdn_dt_bias, gdn_norm_w, w_out, w_up, w_down, final_norm_w, loss_target, m_norm_mix_w, m_norm_mlp_w, m_w_mod, m_b_mod, m_w_in, m_lru_conv_w, m_lru_conv_b, m_lru_gate_a_w, m_lru_gate_a_b, m_lru_gate_x_w, m_lru_gate_x_b, m_lru_lambda, m_lru_norm_w, m_gdn_conv_w, m_gdn_a_log, m_gdn_dt_bias, m_gdn_norm_w, m_w_out, m_w_up, m_w_down, m_final_norm_w, v_norm_mix_w, v_norm_mlp_w, v_w_mod, v_b_mod, v_w_in, v_lru_conv_w, v_lru_conv_b, v_lru_gate_a_w, v_lru_gate_a_b, v_lru_gate_x_w, v_lru_gate_x_b, v_lru_lambda, v_lru_norm_w, v_gdn_conv_w, v_gdn_a_log, v_gdn_dt_bias, v_gdn_norm_w, v_w_out, v_w_up, v_w_down, v_final_norm_w):
    args = dict(locals())
    w = {k: args[k] for k in WEIGHT_ORDER}
    mom = {k: args["m_" + k] for k in WEIGHT_ORDER}
    var = {k: args["v_" + k] for k in WEIGHT_ORDER}
    nl = w_in.shape[0]
    px, py, pc = _place()
    me = 4 * px + 2 * py + pc
    other_chips = [2 * (1 - px) + py, 2 * px + (1 - py), 2 * (1 - px) + (1 - py)]
    others = jnp.stack([2 * q + pc for q in other_chips] + other_chips).astype(jnp.int32)

    shapes0 = [c.shape, lru_conv_w.shape, gdn_conv_w.shape]
    (g0,) = _all_gather([_pack([c, lru_conv_w, gdn_conv_w])], "gather_cond", pltpu.VMEM)
    per_dev = [_unpack(g0[d], shapes0) for d in range(N_DEV)]
    c_all = jnp.concatenate([pd[0] for pd in per_dev], axis=0)
    lru_conv_full = jnp.concatenate([pd[1] for pd in per_dev], axis=-1)
    gdn_conv_full = jnp.concatenate([pd[2] for pd in per_dev], axis=-1)

    cols = w_mod.shape[2]
    bmod_cols = lax.dynamic_slice_in_dim(b_mod, me * cols, cols, axis=1).reshape(nl, 1, cols)
    mod_cols = _mod_local(c_all, w_mod, bmod_cols)
    (g1,) = _all_gather([mod_cols.reshape(nl * N_DEV, cols)], "gather_mod", pltpu.VMEM)
    g1 = g1.reshape(N_DEV, nl, N_DEV, cols)
    mod = jnp.transpose(lax.dynamic_index_in_dim(g1, me, axis=2, keepdims=False), (1, 0, 2)).reshape(nl, N_DEV * cols)

    shards = [a.astype(BF16) for a in (w_in, w_out, w_up, w_down)]
    (first_in,) = _all_gather([shards[0][:1]], "gather_w_in_first", pl.ANY)
    items = [(a, 0) for a in (1, 2, 3)] + [(a, l) for l in range(1, nl) for a in range(4)]
    plan = _gather_plan(items)
    lands = [lax.dynamic_update_slice_in_dim(lax.empty((N_DEV,) + a.shape, BF16), a[None], me, axis=0) for a in shards]
    flight = [_split_start("gather_weights_start", plan, len(items) * GATHER_PEERS, shards, lands, [first_in, mod])]
    mod = mod + flight[0]["token"][0, 0]

    def fetch(l, what, after):
        wanted = [k for k, (a, ll) in enumerate(items) if ll == l and (a == 0) == (what == "in")]
        if wanted:
            flight[0] = _split_wait(f"gather_weights_wait_{what}{l}", plan, flight[0],
                                    [k * GATHER_PEERS + r for k in wanted for r in range(GATHER_PEERS)], after)
        gin, gout, gup, gdn = flight[0]["lands"]
        if what == "rest":
            return gout, gup, gdn
        gin = first_in[:, 0] if l == 0 else gin[:, l]
        win = jnp.transpose(gin, (1, 0, 2)).reshape(1, D_MODEL, IN_COLS)
        wba = jnp.pad(win[:, :, IN_MAIN:], ((0, 0), (0, 0), (0, BA_PAD - (IN_COLS - IN_MAIN))))
        return win, wba, 0

    p = dict(w)
    p["lru_conv_w"], p["gdn_conv_w"] = lru_conv_full, gdn_conv_full

    order = ("w_in", "w_out", "w_up", "w_down")
    place = jnp.stack([me, 2 * px + py]).astype(jnp.int32)
    reducers = (_GradReducer("mlp_grad", ("w_up", "w_down"), w, mom, var, place, others),
                _GradReducer("mix_grad", ("w_in", "w_out"), w, mom, var, place, others))
    loss_part, grad_x, _, small, dmod = _local_step(x[0], loss_target[0], mod, p, fetch, reducers)
    loss = lax.psum(loss_part, MESH_AXES)

    small_names = sorted(small)
    slab = _pack([dmod] + [small[k] for k in small_names])
    (gs,) = _all_gather([slab], "gather_small_grads", pltpu.VMEM)
    dmod_all = gs[:, :_slab_rows(dmod.shape)].reshape(N_DEV, nl, N_MOD * D_MODEL)
    summed = _unpack(_sum_devices(gs), [dmod.shape] + [small[k].shape for k in small_names])
    grads = dict(zip(small_names, summed[1:]))
    grads["b_mod"] = summed[0]
    for k, width in (("lru_conv_w", LRU_W // N_DEV), ("gdn_conv_w", 3 * GDN_W // N_DEV)):
        grads[k] = lax.dynamic_slice_in_dim(grads[k], me * width, width, axis=2)
    names = SMALL_REPLICATED + SMALL_SHARDED
    shapes = [w[k].shape for k in names]
    dl, nm, nv = _adam_flat(_pack([w[k] for k in names]), _pack([grads[k] for k in names]),
                            _pack([mom[k] for k in names]), _pack([var[k] for k in names]))
    delta = dict(zip(names, _unpack(dl, shapes)))
    new_m = dict(zip(names, _unpack(nm, shapes)))
    new_v = dict(zip(names, _unpack(nv, shapes)))

    dmod_cols = jnp.transpose(lax.dynamic_slice_in_dim(dmod_all, me * cols, cols, axis=2), (1, 0, 2))
    grads["w_mod"], delta["w_mod"], new_m["w_mod"], new_v["w_mod"] = _wmod_update(
        c_all, dmod_cols, w_mod, m_w_mod, v_w_mod)

    for red in reducers:
        for k in red.names:
            grads[k], delta[k], new_m[k], new_v[k] = red.outs[k]

    return (loss, grad_x[None], *[grads[k] for k in WEIGHT_ORDER], *[delta[k] for k in WEIGHT_ORDER],
            *[new_m[k] for k in WEIGHT_ORDER], *[new_v[k] for k in WEIGHT_ORDER])
```

```python
import functools

import jax
import jax.numpy as jnp
from jax import lax
from jax.experimental import pallas as pl
from jax.experimental.pallas import tpu as pltpu

F32 = jnp.float32
BF16 = jnp.bfloat16

D_MODEL = 1024
LRU_W = 512
LRU_BLOCKS = 8
LRU_BLOCK = 64
LRU_C = 8.0
GDN_W = 512
HEADS = 4
HEAD_DIM = 128
CHUNK = 64
STACK = HEADS * CHUNK
CONV_K = 4
D_FF = 4096
N_MOD = 6
IN_COLS = 3080
IN_MAIN = 3072
BA_PAD = 128
EPS = 1e-6
N_DEV = 8
HALO = 8
MLP_BLOCKS = 4
PREP_CHUNKS = 2
SCAN_CHUNKS = 2
LANES = 128
ADAM_LR, ADAM_B1, ADAM_B2, ADAM_EPS, ADAM_WD, ADAM_STEP = 0.001, 0.9, 0.999, 1e-08, 0.01, 10
MESH_AXES = ("x", "y", "c")
MESH = pl.DeviceIdType.MESH

NN = (((1,), (0,)), ((), ()))
NT = (((1,), (1,)), ((), ()))
TN = (((0,), (0,)), ((), ()))


def _bdot(a, b, dims=NN):
    return lax.dot_general(a.astype(BF16), b.astype(BF16), dims, preferred_element_type=F32)


def _sdot(a, b, dims=NN):
    ah, bh = a.astype(BF16), b.astype(BF16)
    al, bl = (a - ah.astype(F32)).astype(BF16), (b - bh.astype(F32)).astype(BF16)
    return _bdot(ah, bh, dims) + (_bdot(al, bh, dims) + _bdot(ah, bl, dims))


def _hdot(a, b, dims=NN):
    return lax.dot_general(a, b, dims, precision=lax.Precision.HIGHEST, preferred_element_type=F32)


def _sds(shape, dtype=F32):
    return jax.ShapeDtypeStruct(tuple(shape), dtype)


def _tile(n, t):
    return min(n, t)


def _call(body, name, grid, in_specs, out_specs, out_shape, scratch=(), vmem_mb=48, prefetch=0, aliases=None):
    params = pltpu.CompilerParams(dimension_semantics=("arbitrary",) * len(grid), vmem_limit_bytes=vmem_mb * 2**20)
    if prefetch:
        spec = pltpu.PrefetchScalarGridSpec(num_scalar_prefetch=prefetch, grid=grid, in_specs=in_specs,
                                            out_specs=out_specs, scratch_shapes=list(scratch))
        return pl.pallas_call(body, name=name, grid_spec=spec, out_shape=out_shape, compiler_params=params,
                              input_output_aliases=aliases or {})
    return pl.pallas_call(body, name=name, grid=grid, in_specs=in_specs, out_specs=out_specs, out_shape=out_shape,
                          scratch_shapes=list(scratch), compiler_params=params, input_output_aliases=aliases or {})


def _tok(t, n, col=0):
    return pl.BlockSpec((t, n), lambda i, *_: (i, col))


def _vec(n):
    return pl.BlockSpec((1, n), lambda *_: (0, 0))


def _whole(a):
    nd = a.ndim
    return pl.BlockSpec(a.shape, lambda *_: (0,) * nd)


def _layer(l, *dims):
    return pl.BlockSpec((1,) + dims, lambda *_: (l,) + (0,) * len(dims))


def _gelu(y):
    c0, c1 = 0.7978845608028654, 0.044715
    return 0.5 * y * (1.0 + jnp.tanh(c0 * (y + c1 * y * y * y)))


def _gelu_grad(y):
    c0, c1 = 0.7978845608028654, 0.044715
    t = jnp.tanh(c0 * (y + c1 * y * y * y))
    return 0.5 * (1.0 + t) + 0.5 * y * (1.0 - t * t) * c0 * (1.0 + 3.0 * c1 * y * y)


def _softplus(v):
    return jnp.maximum(v, 0.0) + jnp.log(1.0 + jnp.exp(-jnp.where(v > 0, v, -v)))


@functools.partial(jax.custom_vjp, nondiff_argnums=(1,))
def _roll_rows(v, s):
    s = s % v.shape[0]
    return pltpu.roll(v, s, axis=0) if s else v


def _roll_rows_fwd(v, s):
    return _roll_rows(v, s), None


def _roll_rows_bwd(s, _, g):
    return (_roll_rows(g, -s),)


_roll_rows.defvjp(_roll_rows_fwd, _roll_rows_bwd)


@jax.custom_vjp
def _drop_halo(v):
    return v[HALO:]


def _drop_halo_fwd(v):
    return v[HALO:], None


def _drop_halo_bwd(_, g):
    return (jnp.concatenate([jnp.zeros((HALO, g.shape[1]), g.dtype), g], axis=0),)


_drop_halo.defvjp(_drop_halo_fwd, _drop_halo_bwd)


@functools.partial(jax.custom_vjp, nondiff_argnums=(1, 2))
def _split(v, n, axis):
    w = v.shape[axis] // n
    return tuple(lax.slice_in_dim(v, k * w, (k + 1) * w, axis=axis) for k in range(n))


def _split_fwd(v, n, axis):
    return _split(v, n, axis), None


def _split_bwd(n, axis, _, gs):
    return (jnp.concatenate(list(gs), axis=axis),)


_split.defvjp(_split_fwd, _split_bwd)


def _conv_taps(xw):
    return [_drop_halo(_roll_rows(xw, CONV_K - 1 - k)) for k in range(CONV_K)]


def _modulated_norm(xv, nw, sc, sh):
    r = lax.rsqrt(jnp.mean(xv * xv, axis=-1, keepdims=True) + EPS)
    n = xv * r * nw
    return n * (1.0 + sc) + sh, n, r


def _modulated_norm_bwd(dh, xv, n, r, nw, sc):
    dn = dh * (1.0 + sc)
    dxn = dn * nw
    dx = r * dxn - xv * (r * r * r) * jnp.mean(dxn * xv, axis=-1, keepdims=True)
    return (dx, jnp.sum(dh, axis=0, keepdims=True), jnp.sum(dh * n, axis=0, keepdims=True),
            jnp.sum(dn * xv * r, axis=0, keepdims=True))


def _inproj_fwd(x, nw, sc, sh, win, wba, l):
    s = x.shape[0]
    t = _tile(s, 512)

    def body(x_ref, nw_ref, sc_ref, sh_ref, win_ref, wba_ref, proj_ref, ba_ref):
        h, _, _ = _modulated_norm(x_ref[...], nw_ref[...], sc_ref[...], sh_ref[...])
        hb = h.astype(BF16)
        proj_ref[...] = _bdot(hb, win_ref[0])
        ba_ref[...] = _bdot(hb, wba_ref[0])

    return _call(body, "inproj_fwd", (s // t,),
                 [_tok(t, D_MODEL), _vec(D_MODEL), _vec(D_MODEL), _vec(D_MODEL), _layer(l, D_MODEL, IN_MAIN),
                  _layer(l, D_MODEL, BA_PAD)],
                 [_tok(t, IN_MAIN), _tok(t, BA_PAD)],
                 [_sds((s, IN_MAIN)), _sds((s, BA_PAD))])(x, nw, sc, sh, win, wba)


def _inproj_bwd(dpl, dpq, dpz, dba, x, dx1, nw, sc, sh, win, wba, l):
    s = x.shape[0]
    t = _tile(s, 512)

    def body(dpl_ref, dpq_ref, dpz_ref, dba_ref, x_ref, dx1_ref, nw_ref, sc_ref, sh_ref, win_ref, wba_ref,
             dx_ref, hb_ref, acc_ref):
        @pl.when(pl.program_id(0) == 0)
        def _():
            acc_ref[...] = jnp.zeros_like(acc_ref)

        dh = (_bdot(dpl_ref[...], win_ref[0, :, 0:2 * LRU_W], NT)
              + _bdot(dpq_ref[...], win_ref[0, :, 2 * LRU_W:2 * LRU_W + 3 * GDN_W], NT)
              + _bdot(dpz_ref[...], win_ref[0, :, 2 * LRU_W + 3 * GDN_W:IN_MAIN], NT)
              + _bdot(dba_ref[...], wba_ref[0], NT))
        xv = x_ref[...]
        h, n, r = _modulated_norm(xv, nw_ref[...], sc_ref[...], sh_ref[...])
        hb_ref[...] = h.astype(BF16)
        dx, dsh, dsc, dnw = _modulated_norm_bwd(dh, xv, n, r, nw_ref[...], sc_ref[...])
        dx_ref[...] = dx1_ref[...] + dx
        acc_ref[0:1, :] += dsh
        acc_ref[1:2, :] += dsc
        acc_ref[2:3, :] += dnw

    return _call(body, "inproj_bwd", (s // t,),
                 [_tok(t, 2 * LRU_W), _tok(t, 3 * GDN_W), _tok(t, GDN_W), _tok(t, BA_PAD), _tok(t, D_MODEL),
                  _tok(t, D_MODEL), _vec(D_MODEL), _vec(D_MODEL), _vec(D_MODEL), _layer(l, D_MODEL, IN_MAIN),
                  _layer(l, D_MODEL, BA_PAD)],
                 [_tok(t, D_MODEL), _tok(t, D_MODEL), pl.BlockSpec((8, D_MODEL), lambda i: (0, 0))],
                 [_sds((s, D_MODEL)), _sds((s, D_MODEL), BF16), _sds((8, D_MODEL))])(
                     dpl, dpq, dpz, dba, x, dx1, nw, sc, sh, win, wba)


def _lru_gates(xw, cw_rows, cb, wa, wx, gab, gxb, lam):
    taps = _conv_taps(xw)
    xr = cb + cw_rows[0] * taps[0] + cw_rows[1] * taps[1] + cw_rows[2] * taps[2] + cw_rows[3] * taps[3]
    xb = xr.astype(BF16)
    r = jax.nn.sigmoid(_bdot(xb, wa) + gab)
    i = jax.nn.sigmoid(_bdot(xb, wx) + gxb)
    z = jnp.exp(-jnp.where(lam > 0, lam, -lam))
    w1 = 1.0 + z
    log1p_z = jnp.where(w1 == 1.0, z, jnp.log(w1) * z / (w1 - 1.0))
    ls = jnp.minimum(lam, 0.0) - log1p_z
    la = LRU_C * r * ls
    a = jnp.exp(la)
    mm_raw = -jnp.tanh(la) * (a * a + 1.0)
    mult = jnp.sqrt(jnp.maximum(mm_raw, 1e-12))
    return dict(taps=taps, xr=xr, r=r, i=i, ls=ls, a=a, mm_raw=mm_raw, mult=mult)


def _lru_specs(s, t, tile_of):
    nh = t // HALO
    xl = pl.BlockSpec((t, LRU_W), lambda i: (tile_of(i), 0))
    yl = pl.BlockSpec((t, LRU_W), lambda i: (tile_of(i), 1))
    hx = pl.BlockSpec((HALO, LRU_W), lambda i: (jnp.maximum(tile_of(i) * nh - 1, 0), 0))
    return xl, yl, hx


def _lru_fwd(proj, cw, cb, wa, wx, gab, gxb, lam, lnw):
    s = proj.shape[0]
    t = _tile(s, 256)
    xl, yl, hx = _lru_specs(s, t, lambda i: i)

    def body(xl_ref, yl_ref, hx_ref, cw_ref, cb_ref, wa_ref, wx_ref, gab_ref, gxb_ref, lam_ref, lnw_ref,
             out_ref, h_ref, a_s, b_s, hc):
        i = pl.program_id(0)

        @pl.when(i == 0)
        def _():
            hc[...] = jnp.zeros_like(hc)

        halo = jnp.where(i > 0, hx_ref[...], 0.0)
        xw = jnp.concatenate([halo, xl_ref[...]], axis=0)
        g = _lru_gates(xw, [cw_ref[k:k + 1, :] for k in range(CONV_K)], cb_ref[...], wa_ref[...], wx_ref[...],
                       gab_ref[...], gxb_ref[...], lam_ref[...])
        a_s[...] = g["a"]
        b_s[...] = g["mult"] * (g["i"] * g["xr"])

        def step(k, h):
            h = a_s[pl.ds(k, 1), :] * h + b_s[pl.ds(k, 1), :]
            h_ref[pl.ds(k, 1), :] = h
            return h

        hc[...] = lax.fori_loop(0, t, step, hc[...], unroll=8)
        m = h_ref[...] * _gelu(yl_ref[...])
        out_ref[...] = m * lax.rsqrt(jnp.mean(m * m, axis=-1, keepdims=True) + EPS) * lnw_ref[...]

    return _call(body, "lru_fwd", (s // t,),
                 [xl, yl, hx, _whole(cw), _vec(LRU_W), _whole(wa), _whole(wx)] + [_vec(LRU_W)] * 4,
                 [_tok(t, LRU_W), _tok(t, LRU_W)],
                 [_sds((s, LRU_W)), _sds((s, LRU_W))],
                 scratch=[pltpu.VMEM((t, LRU_W), F32), pltpu.VMEM((t, LRU_W), F32), pltpu.VMEM((1, LRU_W), F32)])(
                     proj, proj, proj, cw, cb, wa, wx, gab, gxb, lam, lnw)


def _lru_bwd(dout, proj, hs, cw, cb, wa, wx, gab, gxb, lam, lnw):
    s = proj.shape[0]
    t = _tile(s, 256)
    nt = s // t
    rev = lambda i: nt - 1 - i
    xl, yl, hx = _lru_specs(s, t, rev)
    nh = t // HALO
    tk = pl.BlockSpec((t, LRU_W), lambda i: (rev(i), 0))
    hh = pl.BlockSpec((HALO, LRU_W), lambda i: (jnp.maximum(rev(i) * nh - 1, 0), 0))

    def body(do_ref, xl_ref, yl_ref, hx_ref, h_ref, hh_ref, cw_ref, cb_ref, wa_ref, wx_ref, gab_ref, gxb_ref,
             lam_ref, lnw_ref, dp_ref, dwa_ref, dwx_ref, rows_ref, dh_s, dhd_s, carry, dxr_next):
        i = pl.program_id(0)
        first_tile = rev(i) == 0

        @pl.when(i == 0)
        def _():
            carry[...] = jnp.zeros_like(carry)
            dxr_next[...] = jnp.zeros_like(dxr_next)
            dwa_ref[...] = jnp.zeros_like(dwa_ref)
            dwx_ref[...] = jnp.zeros_like(dwx_ref)
            rows_ref[...] = jnp.zeros_like(rows_ref)

        halo = jnp.where(first_tile, 0.0, hx_ref[...])
        xw = jnp.concatenate([halo, xl_ref[...]], axis=0)
        cw_rows = [cw_ref[k:k + 1, :] for k in range(CONV_K)]
        lam_v = lam_ref[...]
        g = _lru_gates(xw, cw_rows, cb_ref[...], wa_ref[...], wx_ref[...], gab_ref[...], gxb_ref[...], lam_v)
        a, r, gi, xr, mult = g["a"], g["r"], g["i"], g["xr"], g["mult"]
        hv = h_ref[...]
        yv = yl_ref[...]
        gl = _gelu(yv)
        m = hv * gl
        rn = lax.rsqrt(jnp.mean(m * m, axis=-1, keepdims=True) + EPS)
        dov = do_ref[...]
        dmn = dov * lnw_ref[...]
        rows_ref[4:5, :] += jnp.sum(dov * m * rn, axis=0, keepdims=True)
        dm = rn * dmn - m * (rn * rn * rn) * jnp.mean(dmn * m, axis=-1, keepdims=True)
        dhd_s[...] = dm * gl
        dy = dm * hv * _gelu_grad(yv)
        dh_s[...] = a

        def step(k, c):
            row = t - 1 - k
            d = dhd_s[pl.ds(row, 1), :] + c
            c = dh_s[pl.ds(row, 1), :] * d
            dh_s[pl.ds(row, 1), :] = d
            return c

        carry[...] = lax.fori_loop(0, t, step, carry[...], unroll=8)
        dH = dh_s[...]
        hprev_halo = jnp.where(first_tile, 0.0, hh_ref[...])
        hprev = _drop_halo(_roll_rows(jnp.concatenate([hprev_halo, hv], axis=0), 1))
        da = dH * hprev
        dmult = dH * gi * xr
        di = dH * mult * xr
        dxr = dH * mult * gi
        dla = jnp.where(g["mm_raw"] > 1e-12, dmult * (0.5 / mult) * (-2.0 * a * a), 0.0) + da * a
        dr = dla * (LRU_C * g["ls"])
        sig_neg = jax.nn.sigmoid(-lam_v)
        rows_ref[3:4, :] += jnp.sum(dla * (LRU_C * r), axis=0, keepdims=True) * sig_neg
        drp = dr * r * (1.0 - r)
        dip = di * gi * (1.0 - gi)
        rows_ref[1:2, :] += jnp.sum(drp, axis=0, keepdims=True)
        rows_ref[2:3, :] += jnp.sum(dip, axis=0, keepdims=True)
        xb = xr.astype(BF16)
        drb = drp.astype(BF16)
        dib = dip.astype(BF16)
        dwa_ref[...] += _bdot(xb, drb, TN)
        dwx_ref[...] += _bdot(xb, dib, TN)
        dxr = dxr + _bdot(drb, wa_ref[...], NT) + _bdot(dib, wx_ref[...], NT)
        rows_ref[0:1, :] += jnp.sum(dxr, axis=0, keepdims=True)
        ext = jnp.concatenate([dxr, dxr_next[...]], axis=0)
        dx = cw_rows[CONV_K - 1] * dxr
        for k in range(CONV_K - 1):
            dx = dx + cw_rows[k] * _roll_rows(ext, -(CONV_K - 1 - k))[0:t]
        for k in range(CONV_K):
            rows_ref[8 + k:9 + k, :] += jnp.sum(dxr * g["taps"][k], axis=0, keepdims=True)
        dxr_next[...] = dxr[0:HALO]
        dp_ref[...] = jnp.concatenate([dx, dy], axis=1).astype(BF16)

    acc = lambda shape: pl.BlockSpec(shape, lambda i: (0, 0))
    return _call(body, "lru_bwd", (nt,),
                 [tk, xl, yl, hx, tk, hh, _whole(cw), _vec(LRU_W), _whole(wa), _whole(wx)] + [_vec(LRU_W)] * 4,
                 [pl.BlockSpec((t, 2 * LRU_W), lambda i: (rev(i), 0)), acc((LRU_W, LRU_W)), acc((LRU_W, LRU_W)),
                  acc((16, LRU_W))],
                 [_sds((s, 2 * LRU_W), BF16), _sds((LRU_W, LRU_W)), _sds((LRU_W, LRU_W)), _sds((16, LRU_W))],
                 scratch=[pltpu.VMEM((t, LRU_W), F32), pltpu.VMEM((t, LRU_W), F32), pltpu.VMEM((1, LRU_W), F32),
                          pltpu.VMEM((HALO, LRU_W), F32)])(
                     dout, proj, proj, proj, hs, hs, cw, cb, wa, wx, gab, gxb, lam, lnw)


def _gdn_masks():
    row = lax.broadcasted_iota(jnp.int32, (STACK, STACK), 0)
    col = lax.broadcasted_iota(jnp.int32, (STACK, STACK), 1)
    same = (row // CHUNK) == (col // CHUNK)
    return jnp.stack([(same & (col <= row)).astype(F32), (same & (col < row)).astype(F32), (row == col).astype(F32)])


def _conv_silu(xw, rows):
    taps = _conv_taps(xw)
    y = rows[0] * taps[0] + rows[1] * taps[1] + rows[2] * taps[2] + rows[3] * taps[3]
    return y * jax.nn.sigmoid(y)


def _split3(v):
    hi = v.astype(BF16)
    r1 = v - hi.astype(F32)
    mid = r1.astype(BF16)
    return hi, mid, (r1 - mid.astype(F32)).astype(BF16)


def _mask_dot_raw(mask, v, dims):
    parts = _split3(v)
    d = lambda p: lax.dot_general(mask, p, dims, preferred_element_type=F32)
    return d(parts[0]) + (d(parts[1]) + d(parts[2]))


@jax.custom_vjp
def _mask_dot(mask, v):
    return _mask_dot_raw(mask, v, NN)


def _mask_dot_fwd(mask, v):
    return _mask_dot_raw(mask, v, NN), mask


def _mask_dot_bwd(mask, ct):
    return jnp.zeros_like(mask), _mask_dot_raw(mask, ct, TN)


_mask_dot.defvjp(_mask_dot_fwd, _mask_dot_bwd)


def _unit_lower_inverse(ns, eye):
    tinvs = [eye + n for n in ns]
    ps = list(ns)
    for _ in range(5):
        ps = [_bdot(p, p) for p in ps]
        tinvs = [t + _bdot(t, p) for t, p in zip(tinvs, ps)]
    return tuple(t.astype(BF16) for t in tinvs)


def _refined(ns, rhss, tinvs, dims):
    x0s = [_bdot(t, r, dims) for t, r in zip(tinvs, rhss)]
    ress = [r - x0 + _sdot(n, x0, dims) for n, r, x0 in zip(ns, rhss, x0s)]
    return tuple(x0 + _bdot(t, res, dims) for t, x0, res in zip(tinvs, x0s, ress))


@jax.custom_vjp
def _unit_lower_solve(ns, rhss, tinvs):
    return _refined(ns, rhss, tinvs, NN)


def _unit_lower_solve_fwd(ns, rhss, tinvs):
    xs = _unit_lower_solve(ns, rhss, tinvs)
    return xs, (ns, tinvs, xs)


def _unit_lower_solve_bwd(res, cts):
    ns, tinvs, xs = res
    ys = _refined(ns, cts, tinvs, TN)
    return (tuple(_bdot(y, x, NT) for y, x in zip(ys, xs)), ys, tuple(jnp.zeros_like(t) for t in tinvs))


_unit_lower_solve.defvjp(_unit_lower_solve_fwd, _unit_lower_solve_bwd)


def _gdn_prep(xqs, xks, xvs, bas, cwq, cwk, cwv, pa, pd, masks, tinvs=None, with_inverse=False):
    lower, strict, eye = masks[0], masks[1], masks[2]
    lower_b = lower.astype(BF16)
    lane = lax.broadcasted_iota(jnp.int32, (CHUNK, LANES), 1)
    each = lambda f, *lists: [f(*vals) for vals in zip(*lists)]
    stack = lambda xw, rows: jnp.concatenate(_split(_conv_silu(xw, rows), HEADS, 1), axis=0)
    qs, ks, vs = (each(lambda xw: stack(xw, cw), xs) for xs, cw in ((xqs, cwq), (xks, cwk), (xvs, cwv)))
    qns = each(lambda q: q * lax.rsqrt(jnp.sum(q * q, axis=-1, keepdims=True) + 1e-6) * (HEAD_DIM ** -0.5), qs)
    kns = each(lambda k: k * lax.rsqrt(jnp.sum(k * k, axis=-1, keepdims=True) + 1e-6), ks)

    def col(a, j):
        return jnp.broadcast_to(jnp.sum(jnp.where(lane == j, a, 0.0), axis=1, keepdims=True), (CHUNK, HEAD_DIM))

    betas = each(lambda ba: jnp.concatenate([col(jax.nn.sigmoid(ba), h) for h in range(HEADS)], axis=0), bas)
    g_heads = each(lambda ba: [col(-jnp.exp(pa) * _softplus(ba + pd), HEADS + h) for h in range(HEADS)], bas)
    gs = each(lambda gh: jnp.concatenate(gh, axis=0), g_heads)
    gls = each(lambda gh: jnp.concatenate([jnp.broadcast_to(jnp.sum(g, axis=0, keepdims=True), (CHUNK, HEAD_DIM))
                                           for g in gh], axis=0), g_heads)
    gcs = each(lambda g: _mask_dot(lower_b, g), gs)

    def decay_of(gc):
        gc_rows = jnp.transpose(gc)
        return jnp.exp((jnp.concatenate([gc, gc], axis=1) - jnp.concatenate([gc_rows, gc_rows], axis=0)) * lower)

    decays = each(decay_of, gcs)
    egcs = each(jnp.exp, gcs)
    kbs = each(lambda kn, beta: kn * beta, kns, betas)
    ns = tuple(each(lambda kb, kn, decay: -(_bdot(kb, kn, NT) * decay * strict), kbs, kns, decays))
    if tinvs is None:
        tinvs = _unit_lower_inverse([lax.stop_gradient(n) for n in ns], eye)
    rhss = tuple(each(lambda v, beta, kb, egc: jnp.concatenate([v * beta, kb * egc], axis=1), vs, betas, kbs, egcs))
    sols = _unit_lower_solve(ns, rhss, tuple(tinvs))
    attns = each(lambda qn, kn, decay: _bdot(qn, kn, NT) * decay * lower, qns, kns, decays)
    outs = []
    for sol, qn, kn, egc, gl, gc, attn, tinv in zip(sols, qns, kns, egcs, gls, gcs, attns, tinvs):
        u, w = _split(sol, 2, 1)
        out = (u, w, qn * egc, kn * jnp.exp(gl - gc), attn, jnp.exp(gl))
        outs.append(out + (tinv,) if with_inverse else out)
    return outs


def _gdn_scan(states, u, w, qd, kt, attn, egl, z, nw):
    us, ws, qds, kts, egls = (_split(a, HEADS, 0) for a in (u, w, qd, kt, egl))
    vn = [us[h] - _bdot(ws[h], states[h]) for h in range(HEADS)]
    o = jnp.concatenate([_bdot(qds[h], states[h]) for h in range(HEADS)], axis=0)
    o = o + _bdot(attn, jnp.concatenate(vn, axis=0))
    new = [states[h] * jnp.concatenate([egls[h], egls[h]], axis=0) + _bdot(kts[h], vn[h], TN) for h in range(HEADS)]
    on = o * lax.rsqrt(jnp.mean(o * o, axis=-1, keepdims=True) + EPS) * nw
    return new, on * (z * jax.nn.sigmoid(z))


def _gdn_in_specs(step_of, chunks):
    nh = chunks * CHUNK // HALO
    main = [pl.BlockSpec((chunks * CHUNK, GDN_W), functools.partial(lambda col, i: (step_of(i), col), col))
            for col in (2, 3, 4)]
    halo = [pl.BlockSpec((HALO, GDN_W), functools.partial(lambda col, i: (jnp.maximum(step_of(i) * nh - 1, 0), col),
                                                         col)) for col in (2, 3, 4)]
    return main, halo


def _stk(width, step_of, chunks=1):
    return pl.BlockSpec((chunks * STACK, width), lambda i: (step_of(i), 0))


def _chunk_inputs(main_refs, halo_refs, k, first_step):
    rows = slice(k * CHUNK, (k + 1) * CHUNK)
    if k == 0:
        halos = [jnp.where(first_step, 0.0, h[...]) for h in halo_refs]
    else:
        halos = [m[k * CHUNK - HALO:k * CHUNK, :] for m in main_refs]
    return [jnp.concatenate([h, m[rows, :]], axis=0) for h, m in zip(halos, main_refs)]


def _gdn_prep_fwd(proj, ba, cw, pa, pd, masks):
    s = proj.shape[0]
    nc = s // CHUNK
    per = min(PREP_CHUNKS, nc)
    main, halo = _gdn_in_specs(lambda i: i, per)

    def body(xq_ref, xk_ref, xv_ref, hq_ref, hk_ref, hv_ref, ba_ref, cw_ref, pa_ref, pd_ref, mk_ref, *out_refs):
        first_step = pl.program_id(0) == 0
        rows = [[cw_ref[k:k + 1, j * GDN_W:(j + 1) * GDN_W] for k in range(CONV_K)] for j in range(3)]
        cst = [mk_ref[0], mk_ref[1], mk_ref[2]]
        xs = [_chunk_inputs((xq_ref, xk_ref, xv_ref), (hq_ref, hk_ref, hv_ref), k, first_step) for k in range(per)]
        outs = _gdn_prep([x[0] for x in xs], [x[1] for x in xs], [x[2] for x in xs],
                         [ba_ref[k * CHUNK:(k + 1) * CHUNK, :] for k in range(per)], rows[0], rows[1], rows[2],
                         pa_ref[...], pd_ref[...], cst, with_inverse=True)
        for k, out in enumerate(outs):
            for ref, val in zip(out_refs, out):
                ref[k * STACK:(k + 1) * STACK, :] = val.astype(ref.dtype)

    ident = lambda i: i
    stacked = lambda dt: _sds((nc * STACK, HEAD_DIM), dt)
    wide, thin = _stk(STACK, ident, per), _stk(HEAD_DIM, ident, per)
    return _call(body, "gdn_prep_fwd", (nc // per,),
                 main + halo + [_tok(per * CHUNK, BA_PAD), _whole(cw), _vec(BA_PAD), _vec(BA_PAD), _whole(masks)],
                 [thin] * 4 + [wide, thin, wide],
                 [stacked(F32), stacked(BF16), stacked(BF16), stacked(BF16), _sds((nc * STACK, STACK), BF16),
                  stacked(F32), _sds((nc * STACK, STACK), BF16)])(
                     proj, proj, proj, proj, proj, proj, ba, cw, pa, pd, masks)


def _gdn_prep_bwd(cts, tinv, proj, ba, cw, pa, pd, masks):
    s = proj.shape[0]
    nc = s // CHUNK
    per = min(PREP_CHUNKS, nc)
    steps = nc // per
    rev = lambda i: steps - 1 - i
    main, halo = _gdn_in_specs(rev, per)

    def body(du_ref, dw_ref, dqd_ref, dkt_ref, dattn_ref, degl_ref, tinv_ref, xq_ref, xk_ref, xv_ref, hq_ref, hk_ref,
             hv_ref, ba_ref, cw_ref, pa_ref, pd_ref, mk_ref, dp_ref, dba_ref, dcw_ref, dpar_ref, carry):
        i = pl.program_id(0)
        first_step = rev(i) == 0

        @pl.when(i == 0)
        def _():
            carry[...] = jnp.zeros_like(carry)
            dcw_ref[...] = jnp.zeros_like(dcw_ref)
            dpar_ref[...] = jnp.zeros_like(dpar_ref)

        rows = [[cw_ref[k:k + 1, j * GDN_W:(j + 1) * GDN_W] for k in range(CONV_K)] for j in range(3)]
        cst = [mk_ref[0], mk_ref[1], mk_ref[2]]
        xs = [_chunk_inputs((xq_ref, xk_ref, xv_ref), (hq_ref, hk_ref, hv_ref), k, first_step) for k in range(per)]
        stks = [slice(k * STACK, (k + 1) * STACK) for k in range(per)]
        tinvs = [tinv_ref[stk, :] for stk in stks]
        fn = lambda xqs, xks, xvs, bs, rq, rk, rv, a, d: _gdn_prep(xqs, xks, xvs, bs, rq, rk, rv, a, d, cst, tinvs=tinvs)
        _, vjp = jax.vjp(fn, [x[0] for x in xs], [x[1] for x in xs], [x[2] for x in xs],
                         [ba_ref[k * CHUNK:(k + 1) * CHUNK, :] for k in range(per)], rows[0], rows[1], rows[2],
                         pa_ref[...], pd_ref[...])
        dxqs, dxks, dxvs, dbas, drq, drk, drv, dpa, dpd = vjp(
            [tuple(ref[stk, :] for ref in (du_ref, dw_ref, dqd_ref, dkt_ref, dattn_ref, degl_ref)) for stk in stks])
        dxws = [jnp.concatenate(parts, axis=1) for parts in zip(dxqs, dxks, dxvs)]
        for k in range(per):
            dba_ref[k * CHUNK:(k + 1) * CHUNK, :] = dbas[k].astype(BF16)
        for j, dr in enumerate((drq, drk, drv)):
            for kk in range(CONV_K):
                dcw_ref[kk:kk + 1, j * GDN_W:(j + 1) * GDN_W] += dr[kk]
        dpar_ref[0:1, :] += dpa
        dpar_ref[1:2, :] += dpd
        pad = jnp.zeros((CHUNK - HALO, 3 * GDN_W), F32)
        for k in range(per):
            late = carry[...] if k == per - 1 else dxws[k + 1][0:HALO]
            dp_ref[k * CHUNK:(k + 1) * CHUNK, :] = (dxws[k][HALO:] + jnp.concatenate([pad, late], axis=0)).astype(BF16)
        carry[...] = dxws[0][0:HALO]

    acc = lambda shape: pl.BlockSpec(shape, lambda i: (0, 0))
    wide, thin = _stk(STACK, rev, per), _stk(HEAD_DIM, rev, per)
    return _call(body, "gdn_prep_bwd", (steps,),
                 [thin] * 4 + [wide, thin, wide] + main + halo
                 + [pl.BlockSpec((per * CHUNK, BA_PAD), lambda i: (rev(i), 0)), _whole(cw), _vec(BA_PAD), _vec(BA_PAD),
                    _whole(masks)],
                 [pl.BlockSpec((per * CHUNK, 3 * GDN_W), lambda i: (rev(i), 0)),
                  pl.BlockSpec((per * CHUNK, BA_PAD), lambda i: (rev(i), 0)), acc((CONV_K, 3 * GDN_W)),
                  acc((8, BA_PAD))],
                 [_sds((s, 3 * GDN_W), BF16), _sds((s, BA_PAD), BF16), _sds((CONV_K, 3 * GDN_W)), _sds((8, BA_PAD))],
                 scratch=[pltpu.VMEM((HALO, 3 * GDN_W), F32)])(
                     *cts, tinv, proj, proj, proj, proj, proj, proj, ba, cw, pa, pd, masks)


def _stack_heads(v):
    return jnp.concatenate(_split(v, HEADS, 1), axis=0)


def _unstack_heads(v):
    return jnp.concatenate(_split(v, HEADS, 0), axis=1)


def _gdn_scan_fwd(prep, proj, nw):
    s = proj.shape[0]
    nc = s // CHUNK
    per = min(SCAN_CHUNKS, nc)
    ident = lambda i: i
    srows = HEADS * HEAD_DIM

    def body(u_ref, w_ref, qd_ref, kt_ref, attn_ref, egl_ref, z_ref, nw_ref, out_ref, st_ref, state):
        @pl.when(pl.program_id(0) == 0)
        def _():
            state[...] = jnp.zeros_like(state)

        states = [state[h * HEAD_DIM:(h + 1) * HEAD_DIM, :] for h in range(HEADS)]
        for k in range(per):
            stk, tok = slice(k * STACK, (k + 1) * STACK), slice(k * CHUNK, (k + 1) * CHUNK)
            for h in range(HEADS):
                st_ref[k * srows + h * HEAD_DIM:k * srows + (h + 1) * HEAD_DIM, :] = states[h]
            states, out = _gdn_scan(states, u_ref[stk, :], w_ref[stk, :], qd_ref[stk, :], kt_ref[stk, :],
                                    attn_ref[stk, :], egl_ref[stk, :], _stack_heads(z_ref[tok, :]), nw_ref[...])
            out_ref[tok, :] = _unstack_heads(out)
        for h in range(HEADS):
            state[h * HEAD_DIM:(h + 1) * HEAD_DIM, :] = states[h]

    thin, wide = _stk(HEAD_DIM, ident, per), _stk(STACK, ident, per)
    return _call(body, "gdn_scan_fwd", (nc // per,),
                 [thin] * 4 + [wide, thin, _tok(per * CHUNK, GDN_W, col=5), _vec(HEAD_DIM)],
                 [_tok(per * CHUNK, GDN_W), pl.BlockSpec((per * srows, HEAD_DIM), lambda i: (i, 0))],
                 [_sds((s, GDN_W)), _sds((nc * srows, HEAD_DIM))],
                 scratch=[pltpu.VMEM((srows, HEAD_DIM), F32)])(*prep, proj, nw)


def _gdn_scan_bwd(dout, prep, st, proj, nw):
    s = proj.shape[0]
    nc = s // CHUNK
    per = min(SCAN_CHUNKS, nc)
    steps = nc // per
    rev = lambda i: steps - 1 - i
    srows = HEADS * HEAD_DIM

    def body(do_ref, u_ref, w_ref, qd_ref, kt_ref, attn_ref, egl_ref, st_ref, z_ref, nw_ref,
             du_ref, dw_ref, dqd_ref, dkt_ref, dattn_ref, degl_ref, dz_ref, dnw_ref, dstate):
        @pl.when(pl.program_id(0) == 0)
        def _():
            dstate[...] = jnp.zeros_like(dstate)
            dnw_ref[...] = jnp.zeros_like(dnw_ref)

        dnew = [dstate[h * HEAD_DIM:(h + 1) * HEAD_DIM, :] for h in range(HEADS)]
        for k in reversed(range(per)):
            stk, tok = slice(k * STACK, (k + 1) * STACK), slice(k * CHUNK, (k + 1) * CHUNK)
            states = [st_ref[k * srows + h * HEAD_DIM:k * srows + (h + 1) * HEAD_DIM, :] for h in range(HEADS)]
            f32 = lambda ref: ref[stk, :].astype(F32)
            _, vjp = jax.vjp(_gdn_scan, states, u_ref[stk, :], f32(w_ref), f32(qd_ref), f32(kt_ref), f32(attn_ref),
                             egl_ref[stk, :], _stack_heads(z_ref[tok, :]), nw_ref[...])
            dnew, du, dw, dqd, dkt, dattn, degl, dz, dnw = vjp((dnew, _stack_heads(do_ref[tok, :])))
            for ref, val in zip((du_ref, dw_ref, dqd_ref, dkt_ref, dattn_ref, degl_ref),
                                (du, dw, dqd, dkt, dattn, degl)):
                ref[stk, :] = val
            dz_ref[tok, :] = _unstack_heads(dz).astype(BF16)
            dnw_ref[0:1, :] += dnw
        for h in range(HEADS):
            dstate[h * HEAD_DIM:(h + 1) * HEAD_DIM, :] = dnew[h]

    tokr = lambda n, col=0: pl.BlockSpec((per * CHUNK, n), lambda i: (rev(i), col))
    thin, wide = _stk(HEAD_DIM, rev, per), _stk(STACK, rev, per)
    return _call(body, "gdn_scan_bwd", (steps,),
                 [tokr(GDN_W)] + [thin] * 4 + [wide, thin, pl.BlockSpec((per * srows, HEAD_DIM), lambda i: (rev(i), 0)),
                                               tokr(GDN_W, 5), _vec(HEAD_DIM)],
                 [thin] * 4 + [wide, thin, tokr(GDN_W), pl.BlockSpec((8, HEAD_DIM), lambda i: (0, 0))],
                 [_sds((nc * STACK, HEAD_DIM))] * 4 + [_sds((nc * STACK, STACK)), _sds((nc * STACK, HEAD_DIM)),
                                                       _sds((s, GDN_W), BF16), _sds((8, HEAD_DIM))],
                 scratch=[pltpu.VMEM((srows, HEAD_DIM), F32)])(dout, *prep, st, proj, nw)


def _wo_specs(l):
    half = N_DEV // 2
    return [pl.BlockSpec((half, 1, D_MODEL // N_DEV, D_MODEL), functools.partial(lambda k, *_: (k, l, 0, 0), k))
            for k in range(2)]


def _wo_half(ref):
    return ref[:, 0].reshape(ref.shape[0] * ref.shape[2], ref.shape[3])


def _out_mlp_fwd(ol, og, x, wo, g1, nw2, sc2, sh2, g2, wup, wdn, l):
    s = x.shape[0]
    t = _tile(s, 512)
    nj = wup.shape[0] // MLP_BLOCKS
    fc = wup.shape[3]

    def body(ol_ref, og_ref, x_ref, wol_ref, wog_ref, g1_ref, nw_ref, sc_ref, sh_ref, g2_ref, wup_ref, wdn_ref,
             x1_ref, mix_ref, ff_ref, x2_ref, h2_s, acc_s):
        j = pl.program_id(1)

        @pl.when(j == 0)
        def _():
            mix = _bdot(ol_ref[...], _wo_half(wol_ref)) + _bdot(og_ref[...], _wo_half(wog_ref))
            x1 = x_ref[...] + g1_ref[...] * mix
            mix_ref[...] = mix.astype(BF16)
            x1_ref[...] = x1
            h2, _, _ = _modulated_norm(x1, nw_ref[...], sc_ref[...], sh_ref[...])
            h2_s[...] = h2.astype(BF16)
            acc_s[...] = jnp.zeros_like(acc_s)

        part = None
        for b in range(MLP_BLOCKS):
            up = _bdot(h2_s[...], wup_ref[b, 0])
            down = _bdot(jnp.square(jnp.maximum(up, 0.0)), wdn_ref[b, 0])
            part = down if part is None else part + down
        acc_s[...] += part

        @pl.when(j == nj - 1)
        def _():
            ff_ref[...] = acc_s[...].astype(BF16)
            x2_ref[...] = x1_ref[...] + g2_ref[...] * acc_s[...]

    tk = lambda n: pl.BlockSpec((t, n), lambda i, j: (i, 0))
    return _call(body, "out_mlp_fwd", (s // t, nj),
                 [tk(LRU_W), tk(GDN_W), tk(D_MODEL)] + _wo_specs(l) + [_vec(D_MODEL)] * 5
                 + [pl.BlockSpec((MLP_BLOCKS, 1, D_MODEL, fc), lambda i, j: (j, l, 0, 0)),
                    pl.BlockSpec((MLP_BLOCKS, 1, fc, D_MODEL), lambda i, j: (j, l, 0, 0))],
                 [tk(D_MODEL)] * 4,
                 [_sds((s, D_MODEL)), _sds((s, D_MODEL), BF16), _sds((s, D_MODEL), BF16), _sds((s, D_MODEL))],
                 scratch=[pltpu.VMEM((t, D_MODEL), BF16), pltpu.VMEM((t, D_MODEL), F32)])(
                     ol, og, x, wo, wo, g1, nw2, sc2, sh2, g2, wup, wdn)


def _mlp_bwd(dx2, x1, ff, nw2, sc2, sh2, g2, wup, wdn, l):
    s = x1.shape[0]
    t = _tile(s, 512)
    nj = wup.shape[0] // MLP_BLOCKS
    fc = wup.shape[3]

    def body(dx2_ref, x1_ref, ff_ref, nw_ref, sc_ref, sh_ref, g2_ref, wup_ref, wdn_ref,
             act_ref, dup_ref, h2_ref, dff_ref, dx1_ref, rows_ref, dh2_s):
        i, j = pl.program_id(0), pl.program_id(1)

        @pl.when((i == 0) & (j == 0))
        def _():
            rows_ref[...] = jnp.zeros_like(rows_ref)

        @pl.when(j == 0)
        def _():
            h2, _, _ = _modulated_norm(x1_ref[...], nw_ref[...], sc_ref[...], sh_ref[...])
            h2_ref[...] = h2.astype(BF16)
            dx2 = dx2_ref[...]
            dff_ref[...] = (dx2 * g2_ref[...]).astype(BF16)
            rows_ref[2:3, :] += jnp.sum(dx2 * ff_ref[...].astype(F32), axis=0, keepdims=True)
            dh2_s[...] = jnp.zeros_like(dh2_s)

        part = None
        for b in range(MLP_BLOCKS):
            cols = slice(b * fc, (b + 1) * fc)
            up = _bdot(h2_ref[...], wup_ref[b, 0])
            ru = jnp.maximum(up, 0.0)
            act_ref[:, cols] = (ru * ru).astype(BF16)
            dup = (_bdot(dff_ref[...], wdn_ref[b, 0], NT) * (2.0 * ru)).astype(BF16)
            dup_ref[:, cols] = dup
            back = _bdot(dup, wup_ref[b, 0], NT)
            part = back if part is None else part + back
        dh2_s[...] += part

        @pl.when(j == nj - 1)
        def _():
            xv = x1_ref[...]
            _, n, r = _modulated_norm(xv, nw_ref[...], sc_ref[...], sh_ref[...])
            dx, dsh, dsc, dnw = _modulated_norm_bwd(dh2_s[...], xv, n, r, nw_ref[...], sc_ref[...])
            dx1_ref[...] = dx2_ref[...] + dx
            rows_ref[0:1, :] += dsh
            rows_ref[1:2, :] += dsc
            rows_ref[3:4, :] += dnw

    tk = lambda n: pl.BlockSpec((t, n), lambda i, j: (i, 0))
    tj = pl.BlockSpec((t, MLP_BLOCKS * fc), lambda i, j: (i, j))
    return _call(body, "mlp_bwd", (s // t, nj),
                 [tk(D_MODEL)] * 3 + [_vec(D_MODEL)] * 4
                 + [pl.BlockSpec((MLP_BLOCKS, 1, D_MODEL, fc), lambda i, j: (j, l, 0, 0)),
                    pl.BlockSpec((MLP_BLOCKS, 1, fc, D_MODEL), lambda i, j: (j, l, 0, 0))],
                 [tj, tj, tk(D_MODEL), tk(D_MODEL), tk(D_MODEL), pl.BlockSpec((8, D_MODEL), lambda i, j: (0, 0))],
                 [_sds((s, D_FF), BF16), _sds((s, D_FF), BF16), _sds((s, D_MODEL), BF16),
                  _sds((s, D_MODEL), BF16), _sds((s, D_MODEL)), _sds((8, D_MODEL))],
                 scratch=[pltpu.VMEM((t, D_MODEL), F32)], vmem_mb=56)(dx2, x1, ff, nw2, sc2, sh2, g2, wup, wdn)


def _outproj_bwd(dx1, mix, g1, wo, l):
    s = dx1.shape[0]
    t = _tile(s, 512)

    def body(dx1_ref, mix_ref, g1_ref, wol_ref, wog_ref, dmix_ref, dol_ref, dog_ref, rows_ref):
        @pl.when(pl.program_id(0) == 0)
        def _():
            rows_ref[...] = jnp.zeros_like(rows_ref)

        dx1v = dx1_ref[...]
        rows_ref[0:1, :] += jnp.sum(dx1v * mix_ref[...].astype(F32), axis=0, keepdims=True)
        dmix = (dx1v * g1_ref[...]).astype(BF16)
        dmix_ref[...] = dmix
        dol_ref[...] = _bdot(dmix, _wo_half(wol_ref), NT)
        dog_ref[...] = _bdot(dmix, _wo_half(wog_ref), NT)

    return _call(body, "outproj_bwd", (s // t,),
                 [_tok(t, D_MODEL), _tok(t, D_MODEL), _vec(D_MODEL)] + _wo_specs(l),
                 [_tok(t, D_MODEL), _tok(t, LRU_W), _tok(t, GDN_W), pl.BlockSpec((8, D_MODEL), lambda i: (0, 0))],
                 [_sds((s, D_MODEL), BF16), _sds((s, LRU_W)), _sds((s, GDN_W)), _sds((8, D_MODEL))])(dx1, mix, g1, wo, wo)


def _tn_matmul(a, b, name, out=None, l=0, blocked=False, row_block=0):
    s, m = a.shape
    n = b.shape[1]
    ts, bm = _tile(s, 2048), _tile(m, 1024)
    bn = next(w for w in ((512,) if blocked else (1024, 512, 640, 384, 256, 128)) if n % w == 0)

    def body(a_ref, b_ref, *rest):
        o_ref = rest[-1]

        @pl.when(pl.program_id(2) == 0)
        def _():
            o_ref[...] = jnp.zeros_like(o_ref)

        acc = _bdot(a_ref[...], b_ref[...], TN)
        o_ref[...] += acc.reshape(o_ref.shape)

    in_specs = [pl.BlockSpec((ts, bm), lambda i, j, k: (k, i)), pl.BlockSpec((ts, bn), lambda i, j, k: (k, j))]
    grid = (m // bm, n // bn, s // ts)
    if out is None:
        return _call(body, name, grid, in_specs, pl.BlockSpec((bm, bn), lambda i, j, k: (i, j)), _sds((m, n)))(a, b)
    if blocked:
        out_spec = pl.BlockSpec((1, 1, bm, bn), lambda i, j, k: (l, j, i, 0))
    else:
        out_spec = pl.BlockSpec((1, bm, bn), lambda i, j, k: (l, i + row_block * (m // bm), j))
    return _call(body, name, grid, in_specs + [pl.BlockSpec(memory_space=pl.ANY)], out_spec,
                 _sds(out.shape), aliases={2: 0})(a, b, out)


def _final_fwd_bwd(x, target, fw):
    s = x.shape[0]
    t = _tile(s, 512)

    def body(x_ref, tg_ref, fw_ref, dx_ref, rows_ref):
        @pl.when(pl.program_id(0) == 0)
        def _():
            rows_ref[...] = jnp.zeros_like(rows_ref)

        xv = x_ref[...]
        fwv = fw_ref[...]
        r = lax.rsqrt(jnp.mean(xv * xv, axis=-1, keepdims=True) + EPS)
        err = xv * r * fwv - tg_ref[...]
        part = 0.5 * jnp.sum(jnp.mean(err * err, axis=-1, keepdims=True), axis=0, keepdims=True)
        rows_ref[1:2, :] += jnp.broadcast_to(part, (1, D_MODEL))
        dy = err * (1.0 / D_MODEL)
        rows_ref[0:1, :] += jnp.sum(dy * xv * r, axis=0, keepdims=True)
        dxn = dy * fwv
        dx_ref[...] = r * dxn - xv * (r * r * r) * jnp.mean(dxn * xv, axis=-1, keepdims=True)

    return _call(body, "final_fwd_bwd", (s // t,),
                 [_tok(t, D_MODEL), _tok(t, D_MODEL), _vec(D_MODEL)],
                 [_tok(t, D_MODEL), pl.BlockSpec((8, D_MODEL), lambda i: (0, 0))],
                 [_sds((s, D_MODEL)), _sds((8, D_MODEL))])(x, target, fw)


def _adamw(w, g, m, v):
    m = ADAM_B1 * m + (1.0 - ADAM_B1) * g
    v = ADAM_B2 * v + (1.0 - ADAM_B2) * (g * g)
    m_hat = m / (1.0 - ADAM_B1 ** ADAM_STEP)
    v_hat = v / (1.0 - ADAM_B2 ** ADAM_STEP)
    return -ADAM_LR * (m_hat / (jnp.sqrt(v_hat) + ADAM_EPS) + ADAM_WD * w), m, v


def _mod_local(c_all, wmod, bmod_cols):
    nl, _, cols = wmod.shape

    def body(c_ref, w_ref, b_ref, o_ref):
        cv = c_ref[...]
        o_ref[0] = _bdot(cv * jax.nn.sigmoid(cv), w_ref[0]) + b_ref[0]

    return _call(body, "mod_local", (nl,),
                 [_whole(c_all), pl.BlockSpec((1, D_MODEL, cols), lambda l: (l, 0, 0)),
                  pl.BlockSpec((1, 1, cols), lambda l: (l, 0, 0))],
                 pl.BlockSpec((1, N_DEV, cols), lambda l: (l, 0, 0)), _sds((nl, N_DEV, cols)))(c_all, wmod, bmod_cols)


def _wmod_update(c_all, dmod_cols, w, m, v):
    nl, _, cols = w.shape

    def body(c_ref, d_ref, w_ref, m_ref, v_ref, g_ref, dl_ref, nm_ref, nv_ref):
        cv = c_ref[...]
        g = _bdot(cv * jax.nn.sigmoid(cv), d_ref[0], TN)
        g_ref[0] = g
        dl_ref[0], nm_ref[0], nv_ref[0] = _adamw(w_ref[0], g, m_ref[0], v_ref[0])

    wspec = pl.BlockSpec((1, D_MODEL, cols), lambda l: (l, 0, 0))
    return _call(body, "wmod_update", (nl,),
                 [_whole(c_all), pl.BlockSpec((1, N_DEV, cols), lambda l: (l, 0, 0)), wspec, wspec, wspec],
                 [wspec] * 4, [_sds(w.shape)] * 4)(c_all, dmod_cols, w, m, v)


def _sum_devices(gathered):
    _, r, _ = gathered.shape

    def body(g_ref, o_ref):
        acc = g_ref[0]
        for d in range(1, N_DEV):
            acc = acc + g_ref[d]
        o_ref[...] = acc

    return _call(body, "sum_devices", (1,), [_whole(gathered)], pl.BlockSpec((r, LANES), lambda i: (0, 0)),
                 _sds((r, LANES)))(gathered)


def _adam_flat(w, g, m, v):
    r = w.shape[0]

    def body(w_ref, g_ref, m_ref, v_ref, dl_ref, nm_ref, nv_ref):
        dl_ref[...], nm_ref[...], nv_ref[...] = _adamw(w_ref[...], g_ref[...], m_ref[...], v_ref[...])

    spec = pl.BlockSpec((r, LANES), lambda i: (0, 0))
    return _call(body, "adam_small", (1,), [spec] * 4, [spec] * 3, [_sds((r, LANES))] * 3)(w, g, m, v)


def _pair_add(x, p, others):
    _, r, c = x.shape
    tr = _tile(r, 128 if c > 512 else 256)

    def body(others_ref, x_ref, p_ref, o_ref):
        o_ref[...] = (x_ref[...] + p_ref[...]).astype(BF16)

    return _call(body, "pair_add", (3, r // tr),
                 [pl.BlockSpec((1, tr, c), lambda q, i, others_ref: (others_ref[q], i, 0)),
                  pl.BlockSpec((1, tr, c), lambda q, i, others_ref: (others_ref[3 + q], i, 0))],
                 pl.BlockSpec((1, tr, c), lambda q, i, others_ref: (q, i, 0)), _sds((3, r, c), BF16),
                 prefetch=1)(others, x, p)


def _reduce_adam(x, p, q, place, w, m, v, l, outs):
    _, r, c = x.shape
    tr = _tile(r, 128 if c > 512 else 256)

    def body(place_ref, x_ref, p_ref, q_ref, w_ref, m_ref, v_ref, *rest):
        g_ref, dl_ref, nm_ref, nv_ref = rest[-4:]
        g = (((x_ref[0] + p_ref[0]) + q_ref[0].astype(F32)) + q_ref[1].astype(F32)) + q_ref[2].astype(F32)
        g_ref[0] = g
        dl_ref[0], nm_ref[0], nv_ref[0] = _adamw(w_ref[0], g, m_ref[0], v_ref[0])

    flat = pl.BlockSpec((1, tr, c), lambda i, place_ref: (l, i, 0))
    through = pl.BlockSpec(memory_space=pl.ANY)
    return _call(body, "reduce_adam", (r // tr,),
                 [pl.BlockSpec((1, tr, c), lambda i, place_ref: (place_ref[0], i, 0)),
                  pl.BlockSpec((1, tr, c), lambda i, place_ref: (place_ref[1], i, 0)),
                  pl.BlockSpec((3, tr, c), lambda i, place_ref: (0, i, 0)), flat, flat, flat] + [through] * 4,
                 [flat] * 4, [_sds(w.shape)] * 4, prefetch=1, aliases={7 + k: k for k in range(4)})(
                     place, x, p, q, w, m, v, *outs)


def _place():
    return lax.axis_index("x"), lax.axis_index("y"), lax.axis_index("c")


def _all_gather(xs, name, space):
    n = len(xs)

    def body(*refs):
        x_refs, o_refs = refs[:n], refs[n:2 * n]
        send_sems, recv_sems, local_sems = refs[2 * n:]
        x, y, c = _place()
        me, sibling = (x, y, c), (x, y, 1 - c)
        chips = [(1 - x, y), (x, 1 - y), (1 - x, 1 - y)]

        def blk(a, p):
            return o_refs[a].at[4 * p[0] + 2 * p[1] + p[2]]

        def copy(a, k, block, to, src=None):
            return pltpu.make_async_remote_copy(
                src_ref=blk(a, block) if src is None else src, dst_ref=blk(a, block),
                send_sem=send_sems.at[a, k], recv_sem=recv_sems.at[a, k], device_id=to, device_id_type=MESH)

        mine = [pltpu.make_async_copy(x_refs[a], blk(a, me), local_sems.at[a]) for a in range(n)]
        for cp in mine:
            cp.start()
        first = []
        for a in range(n):
            first.append(copy(a, 0, me, sibling, src=x_refs[a]))
            first += [copy(a, 1 + j, me, (*chip, c), src=x_refs[a]) for j, chip in enumerate(chips)]
        for cp in first:
            cp.start()
        passed = []
        for j, chip in enumerate(chips):
            for a in range(n):
                copy(a, 1 + j, (*chip, c), me).wait_recv()
                cp = copy(a, 4 + j, (*chip, c), sibling)
                cp.start()
                passed.append(cp)
        for a in range(n):
            copy(a, 0, sibling, me).wait_recv()
        for j, chip in enumerate(chips):
            for a in range(n):
                copy(a, 4 + j, (*chip, 1 - c), me).wait_recv()
        for cp in first + passed:
            cp.wait_send()
        for cp in mine:
            cp.wait()

    spec = pl.BlockSpec(memory_space=space)
    return pl.pallas_call(
        body, name=name, out_shape=[_sds((N_DEV,) + a.shape, a.dtype) for a in xs],
        in_specs=[spec] * n, out_specs=[spec] * n,
        scratch_shapes=[pltpu.SemaphoreType.DMA((n, 7)), pltpu.SemaphoreType.DMA((n, 7)),
                        pltpu.SemaphoreType.DMA((n,))])(*xs)


_HBM_SPEC = pl.BlockSpec(memory_space=pltpu.HBM)
_SEM_SPEC = pl.BlockSpec(memory_space=pltpu.SEMAPHORE)
_EFFECT = pltpu.SideEffectType.DATAFLOW_SIDE_EFFECTING


def _descriptors(plan, src_refs, land_refs, send_sems, recv_sems, which=None):
    return [pltpu.make_async_remote_copy(src_ref=s, dst_ref=d, send_sem=send_sems.at[k], recv_sem=recv_sems.at[k],
                                         device_id=dev, device_id_type=MESH)
            for k, (s, d, dev) in enumerate(plan(src_refs, land_refs)) if which is None or k in which]


def _split_start(name, plan, n, srcs, lands, after):
    ns, nb = len(srcs), len(srcs) + len(lands)
    after = list(after) if isinstance(after, (list, tuple)) else [after]
    sems = nb + len(after)

    def body(*refs):
        for cp in _descriptors(plan, refs[:ns], refs[ns:nb], refs[sems], refs[sems + 1]):
            cp.start()
        refs[-1][...] = jnp.zeros_like(refs[-1])

    bufs = [pltpu.with_memory_space_constraint(a, pltpu.HBM) for a in list(srcs) + list(lands)]
    outs = pl.pallas_call(
        body, name=name,
        out_shape=(pltpu.SemaphoreType.DMA((n,)), pltpu.SemaphoreType.DMA((n,)))
        + tuple(pltpu.HBM(a.shape, a.dtype) for a in bufs) + (_sds((8, LANES)),),
        in_specs=[_HBM_SPEC] * nb + [pl.BlockSpec(memory_space=pl.ANY)] * len(after),
        out_specs=(_SEM_SPEC, _SEM_SPEC) + (_HBM_SPEC,) * nb + (pl.BlockSpec(memory_space=pltpu.VMEM),),
        input_output_aliases={i: 2 + i for i in range(nb)},
        compiler_params=pltpu.CompilerParams(has_side_effects=_EFFECT))(*bufs, *after)
    return dict(send=outs[0], recv=outs[1], srcs=list(outs[2:2 + ns]), lands=list(outs[2 + ns:2 + nb]), token=outs[-1])


def _split_wait(name, plan, flight, which, after):
    srcs, lands = flight["srcs"], flight["lands"]
    ns, nb = len(srcs), len(srcs) + len(lands)

    def body(*refs):
        for cp in _descriptors(plan, refs[:ns], refs[ns:nb], refs[nb], refs[nb + 1], set(which)):
            cp.wait_send()
            cp.wait_recv()

    outs = pl.pallas_call(
        body, name=name, out_shape=tuple(pltpu.HBM(a.shape, a.dtype) for a in srcs + lands),
        in_specs=[_HBM_SPEC] * nb + [_SEM_SPEC, _SEM_SPEC, pl.BlockSpec(memory_space=pl.ANY)],
        out_specs=(_HBM_SPEC,) * nb, input_output_aliases={i: i for i in range(nb)},
        compiler_params=pltpu.CompilerParams(has_side_effects=_EFFECT))(*srcs, *lands, flight["send"], flight["recv"],
                                                                       after)
    return dict(flight, srcs=list(outs[:ns]), lands=list(outs[ns:nb]))


GATHER_PEERS = N_DEV - 1


def _gather_plan(items):
    def plan(src_refs, land_refs):
        x, y, c = _place()
        me = 4 * x + 2 * y + c
        out = []
        for a, l in items:
            for r in range(1, N_DEV):
                peer = (1 - x if r & 4 else x, 1 - y if r & 2 else y, 1 - c if r & 1 else c)
                out.append((src_refs[a].at[l], land_refs[a].at[me, l], peer))
        return out

    return plan


def _pair_plan(narr):
    def plan(src_refs, land_refs):
        x, y, c = _place()
        return [(src_refs[a].at[2 * q + (1 - c)], land_refs[a].at[q], (x, y, 1 - c))
                for a in range(narr) for q in range(4)]

    return plan


def _chip_plan(narr):
    def plan(src_refs, land_refs):
        x, y, c = _place()
        chips = [(1 - x, y), (x, 1 - y), (1 - x, 1 - y)]
        return [(src_refs[a].at[r], land_refs[a].at[r], (*chip, c)) for a in range(narr) for r, chip in enumerate(chips)]

    return plan


class _GradReducer:
    def __init__(self, tag, names, w, mom, var, place, others):
        self.tag, self.names, self.w, self.mom, self.var, self.place = tag, names, w, mom, var, place
        self.others = others
        self.outs = {k: [lax.empty(w[k].shape, F32) for _ in range(4)] for k in names}
        self.n = len(names)

    def start(self, l, grads):
        self.l, self.xs = l, [grads[k] for k in self.names]
        lands = [lax.empty((4,) + a.shape[1:], F32) for a in self.xs]
        self.pair = _split_start(f"{self.tag}_pair_start{l}", _pair_plan(self.n), 4 * self.n, self.xs, lands, ())
        return self.pair["token"][0, 0]

    def middle(self, after):
        self.pair = _split_wait(f"{self.tag}_pair_wait{self.l}", _pair_plan(self.n), self.pair, range(4 * self.n),
                                after)
        self.xs, self.ps = self.pair["srcs"], self.pair["lands"]
        ys = [_pair_add(x, p, self.others) for x, p in zip(self.xs, self.ps)]
        lands = [lax.empty((3,) + a.shape[1:], BF16) for a in ys]
        self.chip = _split_start(f"{self.tag}_chip_start{self.l}", _chip_plan(self.n), 3 * self.n, ys, lands, ())
        return self.chip["token"][0, 0]

    def finish(self, after):
        chip = _split_wait(f"{self.tag}_chip_wait{self.l}", _chip_plan(self.n), self.chip, range(3 * self.n), after)
        for k, x, p, q in zip(self.names, self.xs, self.ps, chip["lands"]):
            self.outs[k] = _reduce_adam(x, p, q, self.place, self.w[k], self.mom[k], self.var[k], self.l, self.outs[k])


def _size(shape):
    size = 1
    for d in shape:
        size *= d
    return size


def _slab_rows(shape):
    return -(-_size(shape) // (8 * LANES)) * 8


def _pack(arrs):
    parts = []
    for a in arrs:
        flat = a.reshape(-1).astype(F32)
        parts.append(jnp.pad(flat, (0, _slab_rows(a.shape) * LANES - flat.shape[0])).reshape(-1, LANES))
    return jnp.concatenate(parts, axis=0)


def _unpack(slab, shapes):
    out, off = [], 0
    for shp in shapes:
        rows = _slab_rows(shp)
        out.append(slab[off:off + rows].reshape(-1)[:_size(shp)].reshape(shp))
        off += rows
    return out


def _dense_blocks(w):
    eye = jnp.eye(LRU_BLOCKS, dtype=w.dtype)
    return (eye[:, None, :, None] * w[:, :, None, :]).reshape(LRU_W, LRU_W)


def _diag_blocks(dense):
    return jnp.stack([dense[g * LRU_BLOCK:(g + 1) * LRU_BLOCK, g * LRU_BLOCK:(g + 1) * LRU_BLOCK]
                      for g in range(LRU_BLOCKS)])


def _alpha_lanes(v):
    return jnp.zeros((1, BA_PAD), F32).at[0, HEADS:2 * HEADS].set(v)


def _local_step(x, target, mod, p, fetch, reducers=None):
    nl = mod.shape[0]
    row = lambda v: v.reshape(1, -1)
    masks = _gdn_masks()
    saved = []
    xc = x
    for l in range(nl):
        win, wba, lin = fetch(l, "in", xc)
        mv = [row(mod[l, k * D_MODEL:(k + 1) * D_MODEL]) for k in range(N_MOD)]
        sh1, sc1, g1, sh2, sc2, g2 = mv
        nw1, nw2 = row(p["norm_mix_w"][l]), row(p["norm_mlp_w"][l])
        wa, wx = _dense_blocks(p["lru_gate_a_w"][l]).astype(BF16), _dense_blocks(p["lru_gate_x_w"][l]).astype(BF16)
        lru_args = (p["lru_conv_w"][l], row(p["lru_conv_b"][l]), wa, wx, row(p["lru_gate_a_b"][l]),
                    row(p["lru_gate_x_b"][l]), row(p["lru_lambda"][l]), row(p["lru_norm_w"][l]))
        gdn_args = (p["gdn_conv_w"][l], _alpha_lanes(p["gdn_a_log"][l]), _alpha_lanes(p["gdn_dt_bias"][l]), masks)
        gnw = row(p["gdn_norm_w"][l])
        proj, ba = _inproj_fwd(xc, nw1, sc1, sh1, win, wba, lin)
        ol, hs = _lru_fwd(proj, *lru_args)
        *prep, tinv = _gdn_prep_fwd(proj, ba, *gdn_args)
        og, st = _gdn_scan_fwd(prep, proj, gnw)
        wo, wup, wdn = fetch(l, "rest", og)
        x1, mix, ff, x2 = _out_mlp_fwd(ol, og, xc, wo, g1, nw2, sc2, sh2, g2, wup, wdn, l)
        saved.append(dict(x=xc, mv=mv, nw1=nw1, nw2=nw2, lru_args=lru_args, gdn_args=gdn_args, gnw=gnw, proj=proj,
                          ba=ba, ol=ol, hs=hs, prep=prep, tinv=tinv, og=og, st=st, x1=x1, mix=mix, ff=ff,
                          win=win, wba=wba, lin=lin))
        xc = x2

    dx, frows = _final_fwd_bwd(xc, target, row(p["final_norm_w"]))
    loss_part = frows[1, 0]
    small = {k: [None] * nl for k in ("norm_mix_w", "norm_mlp_w", "lru_conv_w", "lru_conv_b", "lru_gate_a_w",
                                      "lru_gate_a_b", "lru_gate_x_w", "lru_gate_x_b", "lru_lambda", "lru_norm_w",
                                      "gdn_conv_w", "gdn_a_log", "gdn_dt_bias", "gdn_norm_w")}
    fc = D_FF // N_DEV
    big = [None] * nl
    dmod = [None] * nl
    mlp_red, mix_red = reducers or (None, None)
    busy = False
    for l in reversed(range(nl)):
        sv = saved[l]
        sh1, sc1, g1, sh2, sc2, g2 = sv["mv"]
        gnw = sv["gnw"]
        if busy:
            g2 = g2 + started
        act, dup, h2b, dffb, dx1, rows2 = _mlp_bwd(dx, sv["x1"], sv["ff"], sv["nw2"], sc2, sh2, g2, wup, wdn, l)
        if busy:
            g1 = g1 + mix_red.middle(dx1)
        g_up = _tn_matmul(h2b, dup, "grad_w_up", out=lax.empty((1, N_DEV, D_MODEL, fc), F32), blocked=True)[0]
        g_down = _tn_matmul(act, dffb, "grad_w_down", out=lax.empty((1, D_FF, D_MODEL), F32))
        g_down = g_down.reshape(N_DEV, fc, D_MODEL)
        if mlp_red is not None:
            g1 = g1 + mlp_red.start(l, dict(w_up=g_up, w_down=g_down))
        dmix, dol, dog, rows1 = _outproj_bwd(dx1, sv["mix"], g1, wo, l)
        g_out = _tn_matmul(sv["ol"], dmix, "grad_w_out_lru", out=lax.empty((1, D_MODEL, D_MODEL), F32))
        g_out = _tn_matmul(sv["og"], dmix, "grad_w_out_gdn", out=g_out, row_block=1)
        dpl, dwa, dwx, lrows = _lru_bwd(dol, sv["proj"], sv["hs"], *sv["lru_args"])
        if mlp_red is not None:
            gnw = gnw + mlp_red.middle(dpl)
        *cts, dpz, gnrow = _gdn_scan_bwd(dog, sv["prep"], sv["st"], sv["proj"], gnw)
        dpq, dba, dcw, dpar = _gdn_prep_bwd(cts, sv["tinv"], sv["proj"], sv["ba"], *sv["gdn_args"])
        dx, hb, rows0 = _inproj_bwd(dpl, dpq, dpz, dba, sv["x"], dx1, sv["nw1"], sc1, sh1, sv["win"], sv["wba"],
                                    sv["lin"])
        if busy:
            mix_red.finish(dx)
        if mlp_red is not None:
            mlp_red.finish(dx)
        dproj = jnp.concatenate([dpl, dpq, dpz, dba], axis=1)
        g_in = jnp.transpose(_tn_matmul(hb, dproj, "grad_w_in")[:, :IN_COLS].reshape(
            D_MODEL, N_DEV, IN_COLS // N_DEV), (1, 0, 2))
        big[l] = dict(w_in=g_in, w_out=g_out.reshape(N_DEV, D_MODEL // N_DEV, D_MODEL), w_up=g_up,
                      w_down=g_down)
        if mix_red is not None:
            started, busy = mix_red.start(l, big[l]), True
        dmod[l] = jnp.concatenate([rows0[0], rows0[1], rows1[0], rows2[0], rows2[1], rows2[2]])
        small["norm_mix_w"][l], small["norm_mlp_w"][l] = rows0[2], rows2[3]
        small["lru_conv_w"][l], small["lru_conv_b"][l] = lrows[8:8 + CONV_K], lrows[0]
        small["lru_gate_a_w"][l], small["lru_gate_x_w"][l] = _diag_blocks(dwa), _diag_blocks(dwx)
        small["lru_gate_a_b"][l], small["lru_gate_x_b"][l] = lrows[1], lrows[2]
        small["lru_lambda"][l], small["lru_norm_w"][l] = lrows[3], lrows[4]
        small["gdn_conv_w"][l] = dcw
        small["gdn_a_log"][l], small["gdn_dt_bias"][l] = dpar[0, HEADS:2 * HEADS], dpar[1, HEADS:2 * HEADS]
        small["gdn_norm_w"][l] = gnrow[0]
    if busy:
        mix_red.middle(dx)
        mix_red.finish(dx)
    small = {k: jnp.stack(v) for k, v in small.items()}
    small["final_norm_w"] = frows[0]
    return loss_part, dx, big, small, jnp.stack(dmod)


SMALL_REPLICATED = ("norm_mix_w", "norm_mlp_w", "b_mod", "lru_conv_b", "lru_gate_a_w", "lru_gate_a_b", "lru_gate_x_w",
                    "lru_gate_x_b", "lru_lambda", "lru_norm_w", "gdn_a_log", "gdn_dt_bias", "gdn_norm_w",
                    "final_norm_w")
SMALL_SHARDED = ("lru_conv_w", "gdn_conv_w")
WEIGHT_ORDER = ("norm_mix_w", "norm_mlp_w", "w_mod", "b_mod", "w_in", "lru_conv_w", "lru_conv_b", "lru_gate_a_w",
                "lru_gate_a_b", "lru_gate_x_w", "lru_gate_x_b", "lru_lambda", "lru_norm_w", "gdn_conv_w", "gdn_a_log",
                "gdn_dt_bias", "gdn_norm_w", "w_out", "w_up", "w_down", "final_norm_w")


def kernel(x, c, norm_mix_w, norm_mlp_w, w_mod, b_mod, w_in, lru_conv_w, lru_conv_b, lru_gate_a_w, lru_gate_a_b, lru_gate_x_w, lru_gate_x_b, lru_lambda, lru_norm_w, gdn_conv_w, gdn_a_log, gdn_dt_bias, gdn_norm_w, w_out, w_up, w_down, final_norm_w, loss_target, m_norm_mix_w, m_norm_mlp_w, m_w_mod, m_b_mod, m_w_in, m_lru_conv_w, m_lru_conv_b, m_lru_gate_a_w, m_lru_gate_a_b, m_lru_gate_x_w, m_lru_gate_x_b, m_lru_lambda, m_lru_norm_w, m_gdn_conv_w, m_gdn_a_log, m_gdn_dt_bias, m_gdn_norm_w, m_w_out, m_w_up, m_w_down, m_final_norm_w, v_norm_mix_w, v_norm_mlp_w, v_w_mod, v_b_mod, v_w_in, v_lru_conv_w, v_lru_conv_b, v_lru_gate_a_w, v_lru_gate_a_b, v_lru_gate_x_w, v_lru_gate_x_b, v_lru_lambda, v_lru_norm_w, v_gdn_conv_w, v_gdn_a_log, v_gdn_dt_bias, v_gdn_norm_w, v_w_out, v_w_up, v_w_down, v_final_norm_w):
    args = dict(locals())
    w = {k: args[k] for k in WEIGHT_ORDER}
    mom = {k: args["m_" + k] for k in WEIGHT_ORDER}
    var = {k: args["v_" + k] for k in WEIGHT_ORDER}
    nl = w_in.shape[0]
    px, py, pc = _place()
    me = 4 * px + 2 * py + pc
    other_chips = [2 * (1 - px) + py, 2 * px + (1 - py), 2 * (1 - px) + (1 - py)]
    others = jnp.stack([2 * q + pc for q in other_chips] + other_chips).astype(jnp.int32)

    shapes0 = [c.shape, lru_conv_w.shape, gdn_conv_w.shape]
    (g0,) = _all_gather([_pack([c, lru_conv_w, gdn_conv_w])], "gather_cond", pltpu.VMEM)
    per_dev = [_unpack(g0[d], shapes0) for d in range(N_DEV)]
    c_all = jnp.concatenate([pd[0] for pd in per_dev], axis=0)
    lru_conv_full = jnp.concatenate([pd[1] for pd in per_dev], axis=-1)
    gdn_conv_full = jnp.concatenate([pd[2] for pd in per_dev], axis=-1)

    cols = w_mod.shape[2]
    bmod_cols = lax.dynamic_slice_in_dim(b_mod, me * cols, cols, axis=1).reshape(nl, 1, cols)
    mod_cols = _mod_local(c_all, w_mod, bmod_cols)
    (g1,) = _all_gather([mod_cols.reshape(nl * N_DEV, cols)], "gather_mod", pltpu.VMEM)
    g1 = g1.reshape(N_DEV, nl, N_DEV, cols)
    mod = jnp.transpose(lax.dynamic_index_in_dim(g1, me, axis=2, keepdims=False), (1, 0, 2)).reshape(nl, N_DEV * cols)

    shards = [a.astype(BF16) for a in (w_in, w_out, w_up, w_down)]
    (first_in,) = _all_gather([shards[0][:1]], "gather_w_in_first", pl.ANY)
    items = [(a, 0) for a in (1, 2, 3)] + [(a, l) for l in range(1, nl) for a in range(4)]
    plan = _gather_plan(items)
    lands = [lax.dynamic_update_slice_in_dim(lax.empty((N_DEV,) + a.shape, BF16), a[None], me, axis=0) for a in shards]
    flight = [_split_start("gather_weights_start", plan, len(items) * GATHER_PEERS, shards, lands, [first_in, mod])]
    mod = mod + flight[0]["token"][0, 0]

    def fetch(l, what, after):
        wanted = [k for k, (a, ll) in enumerate(items) if ll == l and (a == 0) == (what == "in")]
        if wanted:
            flight[0] = _split_wait(f"gather_weights_wait_{what}{l}", plan, flight[0],
                                    [k * GATHER_PEERS + r for k in wanted for r in range(GATHER_PEERS)], after)
        gin, gout, gup, gdn = flight[0]["lands"]
        if what == "rest":
            return gout, gup, gdn
        gin = first_in[:, 0] if l == 0 else gin[:, l]
        win = jnp.transpose(gin, (1, 0, 2)).reshape(1, D_MODEL, IN_COLS)
        wba = jnp.pad(win[:, :, IN_MAIN:], ((0, 0), (0, 0), (0, BA_PAD - (IN_COLS - IN_MAIN))))
        return win, wba, 0

    p = dict(w)
    p["lru_conv_w"], p["gdn_conv_w"] = lru_conv_full, gdn_conv_full

    order = ("w_in", "w_out", "w_up", "w_down")
    place = jnp.stack([me, 2 * px + py]).astype(jnp.int32)
    reducers = (_GradReducer("mlp_grad", ("w_up", "w_down"), w, mom, var, place, others),
                _GradReducer("mix_grad", ("w_in", "w_out"), w, mom, var, place, others))
    loss_part, grad_x, _, small, dmod = _local_step(x[0], loss_target[0], mod, p, fetch, reducers)
    loss = lax.psum(loss_part, MESH_AXES)

    small_names = sorted(small)
    slab = _pack([dmod] + [small[k] for k in small_names])
    (gs,) = _all_gather([slab], "gather_small_grads", pltpu.VMEM)
    dmod_all = gs[:, :_slab_rows(dmod.shape)].reshape(N_DEV, nl, N_MOD * D_MODEL)
    summed = _unpack(_sum_devices(gs), [dmod.shape] + [small[k].shape for k in small_names])
    grads = dict(zip(small_names, summed[1:]))
    grads["b_mod"] = summed[0]
    for k, width in (("lru_conv_w", LRU_W // N_DEV), ("gdn_conv_w", 3 * GDN_W // N_DEV)):
        grads[k] = lax.dynamic_slice_in_dim(grads[k], me * width, width, axis=2)
    names = SMALL_REPLICATED + SMALL_SHARDED
    shapes = [w[k].shape for k in names]
    dl, nm, nv = _adam_flat(_pack([w[k] for k in names]), _pack([grads[k] for k in names]),
                            _pack([mom[k] for k in names]), _pack([var[k] for k in names]))
    delta = dict(zip(names, _unpack(dl, shapes)))
    new_m = dict(zip(names, _unpack(nm, shapes)))
    new_v = dict(zip(names, _unpack(nv, shapes)))

    dmod_cols = jnp.transpose(lax.dynamic_slice_in_dim(dmod_all, me * cols, cols, axis=2), (1, 0, 2))
    grads["w_mod"], delta["w_mod"], new_m["w_mod"], new_v["w_mod"] = _wmod_update(
        c_all, dmod_cols, w_mod, m_w_mod, v_w_mod)

    for red in reducers:
        for k in red.names:
            grads[k], delta[k], new_m[k], new_v[k] = red.outs[k]

    return (loss, grad_x[None], *[grads[k] for k in WEIGHT_ORDER], *[delta[k] for k in WEIGHT_ORDER],
            *[new_m[k] for k in WEIGHT_ORDER], *[new_v[k] for k in WEIGHT_ORDER])
```

```python
import functools

import jax
import jax.numpy as jnp
from jax import lax
from jax.experimental import pallas as pl
from jax.experimental.pallas import tpu as pltpu

F32 = jnp.float32
BF16 = jnp.bfloat16

D_MODEL = 1024
LRU_W = 512
LRU_BLOCKS = 8
LRU_BLOCK = 64
LRU_C = 8.0
GDN_W = 512
HEADS = 4
HEAD_DIM = 128
CHUNK = 64
STACK = HEADS * CHUNK
CONV_K = 4
D_FF = 4096
N_MOD = 6
IN_COLS = 3080
IN_MAIN = 3072
BA_PAD = 128
EPS = 1e-6
N_DEV = 8
HALO = 8
MLP_BLOCKS = 4
PREP_CHUNKS = 2
SCAN_CHUNKS = 2
LANES = 128
ADAM_LR, ADAM_B1, ADAM_B2, ADAM_EPS, ADAM_WD, ADAM_STEP = 0.001, 0.9, 0.999, 1e-08, 0.01, 10
MESH_AXES = ("x", "y", "c")
MESH = pl.DeviceIdType.MESH

NN = (((1,), (0,)), ((), ()))
NT = (((1,), (1,)), ((), ()))
TN = (((0,), (0,)), ((), ()))


def _bdot(a, b, dims=NN):
    return lax.dot_general(a.astype(BF16), b.astype(BF16), dims, preferred_element_type=F32)


def _sdot(a, b, dims=NN):
    ah, bh = a.astype(BF16), b.astype(BF16)
    al, bl = (a - ah.astype(F32)).astype(BF16), (b - bh.astype(F32)).astype(BF16)
    return _bdot(ah, bh, dims) + (_bdot(al, bh, dims) + _bdot(ah, bl, dims))


def _hdot(a, b, dims=NN):
    return lax.dot_general(a, b, dims, precision=lax.Precision.HIGHEST, preferred_element_type=F32)


def _sds(shape, dtype=F32):
    return jax.ShapeDtypeStruct(tuple(shape), dtype)


def _tile(n, t):
    return min(n, t)


def _call(body, name, grid, in_specs, out_specs, out_shape, scratch=(), vmem_mb=48, prefetch=0, aliases=None):
    params = pltpu.CompilerParams(dimension_semantics=("arbitrary",) * len(grid), vmem_limit_bytes=vmem_mb * 2**20)
    if prefetch:
        spec = pltpu.PrefetchScalarGridSpec(num_scalar_prefetch=prefetch, grid=grid, in_specs=in_specs,
                                            out_specs=out_specs, scratch_shapes=list(scratch))
        return pl.pallas_call(body, name=name, grid_spec=spec, out_shape=out_shape, compiler_params=params,
                              input_output_aliases=aliases or {})
    return pl.pallas_call(body, name=name, grid=grid, in_specs=in_specs, out_specs=out_specs, out_shape=out_shape,
                          scratch_shapes=list(scratch), compiler_params=params, input_output_aliases=aliases or {})


def _tok(t, n, col=0):
    return pl.BlockSpec((t, n), lambda i, *_: (i, col))


def _vec(n):
    return pl.BlockSpec((1, n), lambda *_: (0, 0))


def _whole(a):
    nd = a.ndim
    return pl.BlockSpec(a.shape, lambda *_: (0,) * nd)


def _layer(l, *dims):
    return pl.BlockSpec((1,) + dims, lambda *_: (l,) + (0,) * len(dims))


def _gelu(y):
    c0, c1 = 0.7978845608028654, 0.044715
    return 0.5 * y * (1.0 + jnp.tanh(c0 * (y + c1 * y * y * y)))


def _gelu_grad(y):
    c0, c1 = 0.7978845608028654, 0.044715
    t = jnp.tanh(c0 * (y + c1 * y * y * y))
    return 0.5 * (1.0 + t) + 0.5 * y * (1.0 - t * t) * c0 * (1.0 + 3.0 * c1 * y * y)


def _softplus(v):
    return jnp.maximum(v, 0.0) + jnp.log(1.0 + jnp.exp(-jnp.where(v > 0, v, -v)))


@functools.partial(jax.custom_vjp, nondiff_argnums=(1,))
def _roll_rows(v, s):
    s = s % v.shape[0]
    return pltpu.roll(v, s, axis=0) if s else v


def _roll_rows_fwd(v, s):
    return _roll_rows(v, s), None


def _roll_rows_bwd(s, _, g):
    return (_roll_rows(g, -s),)


_roll_rows.defvjp(_roll_rows_fwd, _roll_rows_bwd)


@jax.custom_vjp
def _drop_halo(v):
    return v[HALO:]


def _drop_halo_fwd(v):
    return v[HALO:], None


def _drop_halo_bwd(_, g):
    return (jnp.concatenate([jnp.zeros((HALO, g.shape[1]), g.dtype), g], axis=0),)


_drop_halo.defvjp(_drop_halo_fwd, _drop_halo_bwd)


@functools.partial(jax.custom_vjp, nondiff_argnums=(1, 2))
def _split(v, n, axis):
    w = v.shape[axis] // n
    return tuple(lax.slice_in_dim(v, k * w, (k + 1) * w, axis=axis) for k in range(n))


def _split_fwd(v, n, axis):
    return _split(v, n, axis), None


def _split_bwd(n, axis, _, gs):
    return (jnp.concatenate(list(gs), axis=axis),)


_split.defvjp(_split_fwd, _split_bwd)


def _conv_taps(xw):
    return [_drop_halo(_roll_rows(xw, CONV_K - 1 - k)) for k in range(CONV_K)]


def _modulated_norm(xv, nw, sc, sh):
    r = lax.rsqrt(jnp.mean(xv * xv, axis=-1, keepdims=True) + EPS)
    n = xv * r * nw
    return n * (1.0 + sc) + sh, n, r


def _modulated_norm_bwd(dh, xv, n, r, nw, sc):
    dn = dh * (1.0 + sc)
    dxn = dn * nw
    dx = r * dxn - xv * (r * r * r) * jnp.mean(dxn * xv, axis=-1, keepdims=True)
    return (dx, jnp.sum(dh, axis=0, keepdims=True), jnp.sum(dh * n, axis=0, keepdims=True),
            jnp.sum(dn * xv * r, axis=0, keepdims=True))


def _inproj_fwd(x, nw, sc, sh, win, wba, l):
    s = x.shape[0]
    t = _tile(s, 512)

    def body(x_ref, nw_ref, sc_ref, sh_ref, win_ref, wba_ref, proj_ref, ba_ref):
        h, _, _ = _modulated_norm(x_ref[...], nw_ref[...], sc_ref[...], sh_ref[...])
        hb = h.astype(BF16)
        proj_ref[...] = _bdot(hb, win_ref[0])
        ba_ref[...] = _bdot(hb, wba_ref[0])

    return _call(body, "inproj_fwd", (s // t,),
                 [_tok(t, D_MODEL), _vec(D_MODEL), _vec(D_MODEL), _vec(D_MODEL), _layer(l, D_MODEL, IN_MAIN),
                  _layer(l, D_MODEL, BA_PAD)],
                 [_tok(t, IN_MAIN), _tok(t, BA_PAD)],
                 [_sds((s, IN_MAIN)), _sds((s, BA_PAD))])(x, nw, sc, sh, win, wba)


def _inproj_bwd(dpl, dpq, dpz, dba, x, dx1, nw, sc, sh, win, wba, l):
    s = x.shape[0]
    t = _tile(s, 512)

    def body(dpl_ref, dpq_ref, dpz_ref, dba_ref, x_ref, dx1_ref, nw_ref, sc_ref, sh_ref, win_ref, wba_ref,
             dx_ref, hb_ref, acc_ref):
        @pl.when(pl.program_id(0) == 0)
        def _():
            acc_ref[...] = jnp.zeros_like(acc_ref)

        dh = (_bdot(dpl_ref[...], win_ref[0, :, 0:2 * LRU_W], NT)
              + _bdot(dpq_ref[...], win_ref[0, :, 2 * LRU_W:2 * LRU_W + 3 * GDN_W], NT)
              + _bdot(dpz_ref[...], win_ref[0, :, 2 * LRU_W + 3 * GDN_W:IN_MAIN], NT)
              + _bdot(dba_ref[...], wba_ref[0], NT))
        xv = x_ref[...]
        h, n, r = _modulated_norm(xv, nw_ref[...], sc_ref[...], sh_ref[...])
        hb_ref[...] = h.astype(BF16)
        dx, dsh, dsc, dnw = _modulated_norm_bwd(dh, xv, n, r, nw_ref[...], sc_ref[...])
        dx_ref[...] = dx1_ref[...] + dx
        acc_ref[0:1, :] += dsh
        acc_ref[1:2, :] += dsc
        acc_ref[2:3, :] += dnw

    return _call(body, "inproj_bwd", (s // t,),
                 [_tok(t, 2 * LRU_W), _tok(t, 3 * GDN_W), _tok(t, GDN_W), _tok(t, BA_PAD), _tok(t, D_MODEL),
                  _tok(t, D_MODEL), _vec(D_MODEL), _vec(D_MODEL), _vec(D_MODEL), _layer(l, D_MODEL, IN_MAIN),
                  _layer(l, D_MODEL, BA_PAD)],
                 [_tok(t, D_MODEL), _tok(t, D_MODEL), pl.BlockSpec((8, D_MODEL), lambda i: (0, 0))],
                 [_sds((s, D_MODEL)), _sds((s, D_MODEL), BF16), _sds((8, D_MODEL))])(
                     dpl, dpq, dpz, dba, x, dx1, nw, sc, sh, win, wba)


def _lru_gates(xw, cw_rows, cb, wa, wx, gab, gxb, lam):
    taps = _conv_taps(xw)
    xr = cb + cw_rows[0] * taps[0] + cw_rows[1] * taps[1] + cw_rows[2] * taps[2] + cw_rows[3] * taps[3]
    xb = xr.astype(BF16)
    r = jax.nn.sigmoid(_bdot(xb, wa) + gab)
    i = jax.nn.sigmoid(_bdot(xb, wx) + gxb)
    z = jnp.exp(-jnp.where(lam > 0, lam, -lam))
    w1 = 1.0 + z
    log1p_z = jnp.where(w1 == 1.0, z, jnp.log(w1) * z / (w1 - 1.0))
    ls = jnp.minimum(lam, 0.0) - log1p_z
    la = LRU_C * r * ls
    a = jnp.exp(la)
    mm_raw = -jnp.tanh(la) * (a * a + 1.0)
    mult = jnp.sqrt(jnp.maximum(mm_raw, 1e-12))
    return dict(taps=taps, xr=xr, r=r, i=i, ls=ls, a=a, mm_raw=mm_raw, mult=mult)


def _lru_specs(s, t, tile_of):
    nh = t // HALO
    xl = pl.BlockSpec((t, LRU_W), lambda i: (tile_of(i), 0))
    yl = pl.BlockSpec((t, LRU_W), lambda i: (tile_of(i), 1))
    hx = pl.BlockSpec((HALO, LRU_W), lambda i: (jnp.maximum(tile_of(i) * nh - 1, 0), 0))
    return xl, yl, hx


def _lru_fwd(proj, cw, cb, wa, wx, gab, gxb, lam, lnw):
    s = proj.shape[0]
    t = _tile(s, 256)
    xl, yl, hx = _lru_specs(s, t, lambda i: i)

    def body(xl_ref, yl_ref, hx_ref, cw_ref, cb_ref, wa_ref, wx_ref, gab_ref, gxb_ref, lam_ref, lnw_ref,
             out_ref, h_ref, a_s, b_s, hc):
        i = pl.program_id(0)

        @pl.when(i == 0)
        def _():
            hc[...] = jnp.zeros_like(hc)

        halo = jnp.where(i > 0, hx_ref[...], 0.0)
        xw = jnp.concatenate([halo, xl_ref[...]], axis=0)
        g = _lru_gates(xw, [cw_ref[k:k + 1, :] for k in range(CONV_K)], cb_ref[...], wa_ref[...], wx_ref[...],
                       gab_ref[...], gxb_ref[...], lam_ref[...])
        a_s[...] = g["a"]
        b_s[...] = g["mult"] * (g["i"] * g["xr"])

        def step(k, h):
            h = a_s[pl.ds(k, 1), :] * h + b_s[pl.ds(k, 1), :]
            h_ref[pl.ds(k, 1), :] = h
            return h

        hc[...] = lax.fori_loop(0, t, step, hc[...], unroll=8)
        m = h_ref[...] * _gelu(yl_ref[...])
        out_ref[...] = m * lax.rsqrt(jnp.mean(m * m, axis=-1, keepdims=True) + EPS) * lnw_ref[...]

    return _call(body, "lru_fwd", (s // t,),
                 [xl, yl, hx, _whole(cw), _vec(LRU_W), _whole(wa), _whole(wx)] + [_vec(LRU_W)] * 4,
                 [_tok(t, LRU_W), _tok(t, LRU_W)],
                 [_sds((s, LRU_W)), _sds((s, LRU_W))],
                 scratch=[pltpu.VMEM((t, LRU_W), F32), pltpu.VMEM((t, LRU_W), F32), pltpu.VMEM((1, LRU_W), F32)])(
                     proj, proj, proj, cw, cb, wa, wx, gab, gxb, lam, lnw)


def _lru_bwd(dout, proj, hs, cw, cb, wa, wx, gab, gxb, lam, lnw):
    s = proj.shape[0]
    t = _tile(s, 256)
    nt = s // t
    rev = lambda i: nt - 1 - i
    xl, yl, hx = _lru_specs(s, t, rev)
    nh = t // HALO
    tk = pl.BlockSpec((t, LRU_W), lambda i: (rev(i), 0))
    hh = pl.BlockSpec((HALO, LRU_W), lambda i: (jnp.maximum(rev(i) * nh - 1, 0), 0))

    def body(do_ref, xl_ref, yl_ref, hx_ref, h_ref, hh_ref, cw_ref, cb_ref, wa_ref, wx_ref, gab_ref, gxb_ref,
             lam_ref, lnw_ref, dp_ref, dwa_ref, dwx_ref, rows_ref, dh_s, dhd_s, carry, dxr_next):
        i = pl.program_id(0)
        first_tile = rev(i) == 0

        @pl.when(i == 0)
        def _():
            carry[...] = jnp.zeros_like(carry)
            dxr_next[...] = jnp.zeros_like(dxr_next)
            dwa_ref[...] = jnp.zeros_like(dwa_ref)
            dwx_ref[...] = jnp.zeros_like(dwx_ref)
            rows_ref[...] = jnp.zeros_like(rows_ref)

        halo = jnp.where(first_tile, 0.0, hx_ref[...])
        xw = jnp.concatenate([halo, xl_ref[...]], axis=0)
        cw_rows = [cw_ref[k:k + 1, :] for k in range(CONV_K)]
        lam_v = lam_ref[...]
        g = _lru_gates(xw, cw_rows, cb_ref[...], wa_ref[...], wx_ref[...], gab_ref[...], gxb_ref[...], lam_v)
        a, r, gi, xr, mult = g["a"], g["r"], g["i"], g["xr"], g["mult"]
        hv = h_ref[...]
        yv = yl_ref[...]
        gl = _gelu(yv)
        m = hv * gl
        rn = lax.rsqrt(jnp.mean(m * m, axis=-1, keepdims=True) + EPS)
        dov = do_ref[...]
        dmn = dov * lnw_ref[...]
        rows_ref[4:5, :] += jnp.sum(dov * m * rn, axis=0, keepdims=True)
        dm = rn * dmn - m * (rn * rn * rn) * jnp.mean(dmn * m, axis=-1, keepdims=True)
        dhd_s[...] = dm * gl
        dy = dm * hv * _gelu_grad(yv)
        dh_s[...] = a

        def step(k, c):
            row = t - 1 - k
            d = dhd_s[pl.ds(row, 1), :] + c
            c = dh_s[pl.ds(row, 1), :] * d
            dh_s[pl.ds(row, 1), :] = d
            return c

        carry[...] = lax.fori_loop(0, t, step, carry[...], unroll=8)
        dH = dh_s[...]
        hprev_halo = jnp.where(first_tile, 0.0, hh_ref[...])
        hprev = _drop_halo(_roll_rows(jnp.concatenate([hprev_halo, hv], axis=0), 1))
        da = dH * hprev
        dmult = dH * gi * xr
        di = dH * mult * xr
        dxr = dH * mult * gi
        dla = jnp.where(g["mm_raw"] > 1e-12, dmult * (0.5 / mult) * (-2.0 * a * a), 0.0) + da * a
        dr = dla * (LRU_C * g["ls"])
        sig_neg = jax.nn.sigmoid(-lam_v)
        rows_ref[3:4, :] += jnp.sum(dla * (LRU_C * r), axis=0, keepdims=True) * sig_neg
        drp = dr * r * (1.0 - r)
        dip = di * gi * (1.0 - gi)
        rows_ref[1:2, :] += jnp.sum(drp, axis=0, keepdims=True)
        rows_ref[2:3, :] += jnp.sum(dip, axis=0, keepdims=True)
        xb = xr.astype(BF16)
        drb = drp.astype(BF16)
        dib = dip.astype(BF16)
        dwa_ref[...] += _bdot(xb, drb, TN)
        dwx_ref[...] += _bdot(xb, dib, TN)
        dxr = dxr + _bdot(drb, wa_ref[...], NT) + _bdot(dib, wx_ref[...], NT)
        rows_ref[0:1, :] += jnp.sum(dxr, axis=0, keepdims=True)
        ext = jnp.concatenate([dxr, dxr_next[...]], axis=0)
        dx = cw_rows[CONV_K - 1] * dxr
        for k in range(CONV_K - 1):
            dx = dx + cw_rows[k] * _roll_rows(ext, -(CONV_K - 1 - k))[0:t]
        for k in range(CONV_K):
            rows_ref[8 + k:9 + k, :] += jnp.sum(dxr * g["taps"][k], axis=0, keepdims=True)
        dxr_next[...] = dxr[0:HALO]
        dp_ref[...] = jnp.concatenate([dx, dy], axis=1).astype(BF16)

    acc = lambda shape: pl.BlockSpec(shape, lambda i: (0, 0))
    return _call(body, "lru_bwd", (nt,),
                 [tk, xl, yl, hx, tk, hh, _whole(cw), _vec(LRU_W), _whole(wa), _whole(wx)] + [_vec(LRU_W)] * 4,
                 [pl.BlockSpec((t, 2 * LRU_W), lambda i: (rev(i), 0)), acc((LRU_W, LRU_W)), acc((LRU_W, LRU_W)),
                  acc((16, LRU_W))],
                 [_sds((s, 2 * LRU_W), BF16), _sds((LRU_W, LRU_W)), _sds((LRU_W, LRU_W)), _sds((16, LRU_W))],
                 scratch=[pltpu.VMEM((t, LRU_W), F32), pltpu.VMEM((t, LRU_W), F32), pltpu.VMEM((1, LRU_W), F32),
                          pltpu.VMEM((HALO, LRU_W), F32)])(
                     dout, proj, proj, proj, hs, hs, cw, cb, wa, wx, gab, gxb, lam, lnw)


def _gdn_masks():
    row = lax.broadcasted_iota(jnp.int32, (STACK, STACK), 0)
    col = lax.broadcasted_iota(jnp.int32, (STACK, STACK), 1)
    same = (row // CHUNK) == (col // CHUNK)
    return jnp.stack([(same & (col <= row)).astype(F32), (same & (col < row)).astype(F32), (row == col).astype(F32)])


def _conv_silu(xw, rows):
    taps = _conv_taps(xw)
    y = rows[0] * taps[0] + rows[1] * taps[1] + rows[2] * taps[2] + rows[3] * taps[3]
    return y * jax.nn.sigmoid(y)


def _split3(v):
    hi = v.astype(BF16)
    r1 = v - hi.astype(F32)
    mid = r1.astype(BF16)
    return hi, mid, (r1 - mid.astype(F32)).astype(BF16)


def _mask_dot_raw(mask, v, dims):
    parts = _split3(v)
    d = lambda p: lax.dot_general(mask, p, dims, preferred_element_type=F32)
    return d(parts[0]) + (d(parts[1]) + d(parts[2]))


@jax.custom_vjp
def _mask_dot(mask, v):
    return _mask_dot_raw(mask, v, NN)


def _mask_dot_fwd(mask, v):
    return _mask_dot_raw(mask, v, NN), mask


def _mask_dot_bwd(mask, ct):
    return jnp.zeros_like(mask), _mask_dot_raw(mask, ct, TN)


_mask_dot.defvjp(_mask_dot_fwd, _mask_dot_bwd)


def _unit_lower_inverse(ns, eye):
    tinvs = [eye + n for n in ns]
    ps = list(ns)
    for _ in range(5):
        ps = [_bdot(p, p) for p in ps]
        tinvs = [t + _bdot(t, p) for t, p in zip(tinvs, ps)]
    return tuple(t.astype(BF16) for t in tinvs)


def _refined(ns, rhss, tinvs, dims):
    x0s = [_bdot(t, r, dims) for t, r in zip(tinvs, rhss)]
    ress = [r - x0 + _sdot(n, x0, dims) for n, r, x0 in zip(ns, rhss, x0s)]
    return tuple(x0 + _bdot(t, res, dims) for t, x0, res in zip(tinvs, x0s, ress))


@jax.custom_vjp
def _unit_lower_solve(ns, rhss, tinvs):
    return _refined(ns, rhss, tinvs, NN)


def _unit_lower_solve_fwd(ns, rhss, tinvs):
    xs = _unit_lower_solve(ns, rhss, tinvs)
    return xs, (ns, tinvs, xs)


def _unit_lower_solve_bwd(res, cts):
    ns, tinvs, xs = res
    ys = _refined(ns, cts, tinvs, TN)
    return (tuple(_bdot(y, x, NT) for y, x in zip(ys, xs)), ys, tuple(jnp.zeros_like(t) for t in tinvs))


_unit_lower_solve.defvjp(_unit_lower_solve_fwd, _unit_lower_solve_bwd)


def _gdn_prep(xqs, xks, xvs, bas, cwq, cwk, cwv, pa, pd, masks, tinvs=None, with_inverse=False):
    lower, strict, eye = masks[0], masks[1], masks[2]
    lower_b = lower.astype(BF16)
    lane = lax.broadcasted_iota(jnp.int32, (CHUNK, LANES), 1)
    each = lambda f, *lists: [f(*vals) for vals in zip(*lists)]
    stack = lambda xw, rows: jnp.concatenate(_split(_conv_silu(xw, rows), HEADS, 1), axis=0)
    qs, ks, vs = (each(lambda xw: stack(xw, cw), xs) for xs, cw in ((xqs, cwq), (xks, cwk), (xvs, cwv)))
    qns = each(lambda q: q * lax.rsqrt(jnp.sum(q * q, axis=-1, keepdims=True) + 1e-6) * (HEAD_DIM ** -0.5), qs)
    kns = each(lambda k: k * lax.rsqrt(jnp.sum(k * k, axis=-1, keepdims=True) + 1e-6), ks)

    def col(a, j):
        return jnp.broadcast_to(jnp.sum(jnp.where(lane == j, a, 0.0), axis=1, keepdims=True), (CHUNK, HEAD_DIM))

    betas = each(lambda ba: jnp.concatenate([col(jax.nn.sigmoid(ba), h) for h in range(HEADS)], axis=0), bas)
    g_heads = each(lambda ba: [col(-jnp.exp(pa) * _softplus(ba + pd), HEADS + h) for h in range(HEADS)], bas)
    gs = each(lambda gh: jnp.concatenate(gh, axis=0), g_heads)
    gls = each(lambda gh: jnp.concatenate([jnp.broadcast_to(jnp.sum(g, axis=0, keepdims=True), (CHUNK, HEAD_DIM))
                                           for g in gh], axis=0), g_heads)
    gcs = each(lambda g: _mask_dot(lower_b, g), gs)

    def decay_of(gc):
        gc_rows = jnp.transpose(gc)
        return jnp.exp((jnp.concatenate([gc, gc], axis=1) - jnp.concatenate([gc_rows, gc_rows], axis=0)) * lower)

    decays = each(decay_of, gcs)
    egcs = each(jnp.exp, gcs)
    kbs = each(lambda kn, beta: kn * beta, kns, betas)
    ns = tuple(each(lambda kb, kn, decay: -(_bdot(kb, kn, NT) * decay * strict), kbs, kns, decays))
    if tinvs is None:
        tinvs = _unit_lower_inverse([lax.stop_gradient(n) for n in ns], eye)
    rhss = tuple(each(lambda v, beta, kb, egc: jnp.concatenate([v * beta, kb * egc], axis=1), vs, betas, kbs, egcs))
    sols = _unit_lower_solve(ns, rhss, tuple(tinvs))
    attns = each(lambda qn, kn, decay: _bdot(qn, kn, NT) * decay * lower, qns, kns, decays)
    outs = []
    for sol, qn, kn, egc, gl, gc, attn, tinv in zip(sols, qns, kns, egcs, gls, gcs, attns, tinvs):
        u, w = _split(sol, 2, 1)
        out = (u, w, qn * egc, kn * jnp.exp(gl - gc), attn, jnp.exp(gl))
        outs.append(out + (tinv,) if with_inverse else out)
    return outs


def _gdn_scan(states, u, w, qd, kt, attn, egl, z, nw):
    us, ws, qds, kts, egls = (_split(a, HEADS, 0) for a in (u, w, qd, kt, egl))
    vn = [us[h] - _bdot(ws[h], states[h]) for h in range(HEADS)]
    o = jnp.concatenate([_bdot(qds[h], states[h]) for h in range(HEADS)], axis=0)
    o = o + _bdot(attn, jnp.concatenate(vn, axis=0))
    new = [states[h] * jnp.concatenate([egls[h], egls[h]], axis=0) + _bdot(kts[h], vn[h], TN) for h in range(HEADS)]
    on = o * lax.rsqrt(jnp.mean(o * o, axis=-1, keepdims=True) + EPS) * nw
    return new, on * (z * jax.nn.sigmoid(z))


def _gdn_in_specs(step_of, chunks):
    nh = chunks * CHUNK // HALO
    main = [pl.BlockSpec((chunks * CHUNK, GDN_W), functools.partial(lambda col, i: (step_of(i), col), col))
            for col in (2, 3, 4)]
    halo = [pl.BlockSpec((HALO, GDN_W), functools.partial(lambda col, i: (jnp.maximum(step_of(i) * nh - 1, 0), col),
                                                         col)) for col in (2, 3, 4)]
    return main, halo


def _stk(width, step_of, chunks=1):
    return pl.BlockSpec((chunks * STACK, width), lambda i: (step_of(i), 0))


def _chunk_inputs(main_refs, halo_refs, k, first_step):
    rows = slice(k * CHUNK, (k + 1) * CHUNK)
    if k == 0:
        halos = [jnp.where(first_step, 0.0, h[...]) for h in halo_refs]
    else:
        halos = [m[k * CHUNK - HALO:k * CHUNK, :] for m in main_refs]
    return [jnp.concatenate([h, m[rows, :]], axis=0) for h, m in zip(halos, main_refs)]


def _gdn_prep_fwd(proj, ba, cw, pa, pd, masks):
    s = proj.shape[0]
    nc = s // CHUNK
    per = min(PREP_CHUNKS, nc)
    main, halo = _gdn_in_specs(lambda i: i, per)

    def body(xq_ref, xk_ref, xv_ref, hq_ref, hk_ref, hv_ref, ba_ref, cw_ref, pa_ref, pd_ref, mk_ref, *out_refs):
        first_step = pl.program_id(0) == 0
        rows = [[cw_ref[k:k + 1, j * GDN_W:(j + 1) * GDN_W] for k in range(CONV_K)] for j in range(3)]
        cst = [mk_ref[0], mk_ref[1], mk_ref[2]]
        xs = [_chunk_inputs((xq_ref, xk_ref, xv_ref), (hq_ref, hk_ref, hv_ref), k, first_step) for k in range(per)]
        outs = _gdn_prep([x[0] for x in xs], [x[1] for x in xs], [x[2] for x in xs],
                         [ba_ref[k * CHUNK:(k + 1) * CHUNK, :] for k in range(per)], rows[0], rows[1], rows[2],
                         pa_ref[...], pd_ref[...], cst, with_inverse=True)
        for k, out in enumerate(outs):
            for ref, val in zip(out_refs, out):
                ref[k * STACK:(k + 1) * STACK, :] = val.astype(ref.dtype)

    ident = lambda i: i
    stacked = lambda dt: _sds((nc * STACK, HEAD_DIM), dt)
    wide, thin = _stk(STACK, ident, per), _stk(HEAD_DIM, ident, per)
    return _call(body, "gdn_prep_fwd", (nc // per,),
                 main + halo + [_tok(per * CHUNK, BA_PAD), _whole(cw), _vec(BA_PAD), _vec(BA_PAD), _whole(masks)],
                 [thin] * 4 + [wide, thin, wide],
                 [stacked(F32), stacked(BF16), stacked(BF16), stacked(BF16), _sds((nc * STACK, STACK), BF16),
                  stacked(F32), _sds((nc * STACK, STACK), BF16)])(
                     proj, proj, proj, proj, proj, proj, ba, cw, pa, pd, masks)


def _gdn_prep_bwd(cts, tinv, proj, ba, cw, pa, pd, masks):
    s = proj.shape[0]
    nc = s // CHUNK
    per = min(PREP_CHUNKS, nc)
    steps = nc // per
    rev = lambda i: steps - 1 - i
    main, halo = _gdn_in_specs(rev, per)

    def body(du_ref, dw_ref, dqd_ref, dkt_ref, dattn_ref, degl_ref, tinv_ref, xq_ref, xk_ref, xv_ref, hq_ref, hk_ref,
             hv_ref, ba_ref, cw_ref, pa_ref, pd_ref, mk_ref, dp_ref, dba_ref, dcw_ref, dpar_ref, carry):
        i = pl.program_id(0)
        first_step = rev(i) == 0

        @pl.when(i == 0)
        def _():
            carry[...] = jnp.zeros_like(carry)
            dcw_ref[...] = jnp.zeros_like(dcw_ref)
            dpar_ref[...] = jnp.zeros_like(dpar_ref)

        rows = [[cw_ref[k:k + 1, j * GDN_W:(j + 1) * GDN_W] for k in range(CONV_K)] for j in range(3)]
        cst = [mk_ref[0], mk_ref[1], mk_ref[2]]
        xs = [_chunk_inputs((xq_ref, xk_ref, xv_ref), (hq_ref, hk_ref, hv_ref), k, first_step) for k in range(per)]
        stks = [slice(k * STACK, (k + 1) * STACK) for k in range(per)]
        tinvs = [tinv_ref[stk, :] for stk in stks]
        fn = lambda xqs, xks, xvs, bs, rq, rk, rv, a, d: _gdn_prep(xqs, xks, xvs, bs, rq, rk, rv, a, d, cst, tinvs=tinvs)
        _, vjp = jax.vjp(fn, [x[0] for x in xs], [x[1] for x in xs], [x[2] for x in xs],
                         [ba_ref[k * CHUNK:(k + 1) * CHUNK, :] for k in range(per)], rows[0], rows[1], rows[2],
                         pa_ref[...], pd_ref[...])
        dxqs, dxks, dxvs, dbas, drq, drk, drv, dpa, dpd = vjp(
            [tuple(ref[stk, :] for ref in (du_ref, dw_ref, dqd_ref, dkt_ref, dattn_ref, degl_ref)) for stk in stks])
        dxws = [jnp.concatenate(parts, axis=1) for parts in zip(dxqs, dxks, dxvs)]
        for k in range(per):
            dba_ref[k * CHUNK:(k + 1) * CHUNK, :] = dbas[k].astype(BF16)
        for j, dr in enumerate((drq, drk, drv)):
            for kk in range(CONV_K):
                dcw_ref[kk:kk + 1, j * GDN_W:(j + 1) * GDN_W] += dr[kk]
        dpar_ref[0:1, :] += dpa
        dpar_ref[1:2, :] += dpd
        pad = jnp.zeros((CHUNK - HALO, 3 * GDN_W), F32)
        for k in range(per):
            late = carry[...] if k == per - 1 else dxws[k + 1][0:HALO]
            dp_ref[k * CHUNK:(k + 1) * CHUNK, :] = (dxws[k][HALO:] + jnp.concatenate([pad, late], axis=0)).astype(BF16)
        carry[...] = dxws[0][0:HALO]

    acc = lambda shape: pl.BlockSpec(shape, lambda i: (0, 0))
    wide, thin = _stk(STACK, rev, per), _stk(HEAD_DIM, rev, per)
    return _call(body, "gdn_prep_bwd", (steps,),
                 [thin] * 4 + [wide, thin, wide] + main + halo
                 + [pl.BlockSpec((per * CHUNK, BA_PAD), lambda i: (rev(i), 0)), _whole(cw), _vec(BA_PAD), _vec(BA_PAD),
                    _whole(masks)],
                 [pl.BlockSpec((per * CHUNK, 3 * GDN_W), lambda i: (rev(i), 0)),
                  pl.BlockSpec((per * CHUNK, BA_PAD), lambda i: (rev(i), 0)), acc((CONV_K, 3 * GDN_W)),
                  acc((8, BA_PAD))],
                 [_sds((s, 3 * GDN_W), BF16), _sds((s, BA_PAD), BF16), _sds((CONV_K, 3 * GDN_W)), _sds((8, BA_PAD))],
                 scratch=[pltpu.VMEM((HALO, 3 * GDN_W), F32)])(
                     *cts, tinv, proj, proj, proj, proj, proj, proj, ba, cw, pa, pd, masks)


def _stack_heads(v):
    return jnp.concatenate(_split(v, HEADS, 1), axis=0)


def _unstack_heads(v):
    return jnp.concatenate(_split(v, HEADS, 0), axis=1)


def _gdn_scan_fwd(prep, proj, nw):
    s = proj.shape[0]
    nc = s // CHUNK
    per = min(SCAN_CHUNKS, nc)
    ident = lambda i: i
    srows = HEADS * HEAD_DIM

    def body(u_ref, w_ref, qd_ref, kt_ref, attn_ref, egl_ref, z_ref, nw_ref, out_ref, st_ref, state):
        @pl.when(pl.program_id(0) == 0)
        def _():
            state[...] = jnp.zeros_like(state)

        states = [state[h * HEAD_DIM:(h + 1) * HEAD_DIM, :] for h in range(HEADS)]
        for k in range(per):
            stk, tok = slice(k * STACK, (k + 1) * STACK), slice(k * CHUNK, (k + 1) * CHUNK)
            for h in range(HEADS):
                st_ref[k * srows + h * HEAD_DIM:k * srows + (h + 1) * HEAD_DIM, :] = states[h]
            states, out = _gdn_scan(states, u_ref[stk, :], w_ref[stk, :], qd_ref[stk, :], kt_ref[stk, :],
                                    attn_ref[stk, :], egl_ref[stk, :], _stack_heads(z_ref[tok, :]), nw_ref[...])
            out_ref[tok, :] = _unstack_heads(out)
        for h in range(HEADS):
            state[h * HEAD_DIM:(h + 1) * HEAD_DIM, :] = states[h]

    thin, wide = _stk(HEAD_DIM, ident, per), _stk(STACK, ident, per)
    return _call(body, "gdn_scan_fwd", (nc // per,),
                 [thin] * 4 + [wide, thin, _tok(per * CHUNK, GDN_W, col=5), _vec(HEAD_DIM)],
                 [_tok(per * CHUNK, GDN_W), pl.BlockSpec((per * srows, HEAD_DIM), lambda i: (i, 0))],
                 [_sds((s, GDN_W)), _sds((nc * srows, HEAD_DIM))],
                 scratch=[pltpu.VMEM((srows, HEAD_DIM), F32)])(*prep, proj, nw)


def _gdn_scan_bwd(dout, prep, st, proj, nw):
    s = proj.shape[0]
    nc = s // CHUNK
    per = min(SCAN_CHUNKS, nc)
    steps = nc // per
    rev = lambda i: steps - 1 - i
    srows = HEADS * HEAD_DIM

    def body(do_ref, u_ref, w_ref, qd_ref, kt_ref, attn_ref, egl_ref, st_ref, z_ref, nw_ref,
             du_ref, dw_ref, dqd_ref, dkt_ref, dattn_ref, degl_ref, dz_ref, dnw_ref, dstate):
        @pl.when(pl.program_id(0) == 0)
        def _():
            dstate[...] = jnp.zeros_like(dstate)
            dnw_ref[...] = jnp.zeros_like(dnw_ref)

        dnew = [dstate[h * HEAD_DIM:(h + 1) * HEAD_DIM, :] for h in range(HEADS)]
        for k in reversed(range(per)):
            stk, tok = slice(k * STACK, (k + 1) * STACK), slice(k * CHUNK, (k + 1) * CHUNK)
            states = [st_ref[k * srows + h * HEAD_DIM:k * srows + (h + 1) * HEAD_DIM, :] for h in range(HEADS)]
            f32 = lambda ref: ref[stk, :].astype(F32)
            _, vjp = jax.vjp(_gdn_scan, states, u_ref[stk, :], f32(w_ref), f32(qd_ref), f32(kt_ref), f32(attn_ref),
                             egl_ref[stk, :], _stack_heads(z_ref[tok, :]), nw_ref[...])
            dnew, du, dw, dqd, dkt, dattn, degl, dz, dnw = vjp((dnew, _stack_heads(do_ref[tok, :])))
            for ref, val in zip((du_ref, dw_ref, dqd_ref, dkt_ref, dattn_ref, degl_ref),
                                (du, dw, dqd, dkt, dattn, degl)):
                ref[stk, :] = val
            dz_ref[tok, :] = _unstack_heads(dz).astype(BF16)
            dnw_ref[0:1, :] += dnw
        for h in range(HEADS):
            dstate[h * HEAD_DIM:(h + 1) * HEAD_DIM, :] = dnew[h]

    tokr = lambda n, col=0: pl.BlockSpec((per * CHUNK, n), lambda i: (rev(i), col))
    thin, wide = _stk(HEAD_DIM, rev, per), _stk(STACK, rev, per)
    return _call(body, "gdn_scan_bwd", (steps,),
                 [tokr(GDN_W)] + [thin] * 4 + [wide, thin, pl.BlockSpec((per * srows, HEAD_DIM), lambda i: (rev(i), 0)),
                                               tokr(GDN_W, 5), _vec(HEAD_DIM)],
                 [thin] * 4 + [wide, thin, tokr(GDN_W), pl.BlockSpec((8, HEAD_DIM), lambda i: (0, 0))],
                 [_sds((nc * STACK, HEAD_DIM))] * 4 + [_sds((nc * STACK, STACK)), _sds((nc * STACK, HEAD_DIM)),
                                                       _sds((s, GDN_W), BF16), _sds((8, HEAD_DIM))],
                 scratch=[pltpu.VMEM((srows, HEAD_DIM), F32)])(dout, *prep, st, proj, nw)


def _wo_specs(l):
    half = N_DEV // 2
    return [pl.BlockSpec((half, 1, D_MODEL // N_DEV, D_MODEL), functools.partial(lambda k, *_: (k, l, 0, 0), k))
            for k in range(2)]


def _wo_half(ref):
    return ref[:, 0].reshape(ref.shape[0] * ref.shape[2], ref.shape[3])


def _out_mlp_fwd(ol, og, x, wo, g1, nw2, sc2, sh2, g2, wup, wdn, l):
    s = x.shape[0]
    t = _tile(s, 512)
    nj = wup.shape[0] // MLP_BLOCKS
    fc = wup.shape[3]

    def body(ol_ref, og_ref, x_ref, wol_ref, wog_ref, g1_ref, nw_ref, sc_ref, sh_ref, g2_ref, wup_ref, wdn_ref,
             x1_ref, mix_ref, ff_ref, x2_ref, h2_s, acc_s):
        j = pl.program_id(1)

        @pl.when(j == 0)
        def _():
            mix = _bdot(ol_ref[...], _wo_half(wol_ref)) + _bdot(og_ref[...], _wo_half(wog_ref))
            x1 = x_ref[...] + g1_ref[...] * mix
            mix_ref[...] = mix.astype(BF16)
            x1_ref[...] = x1
            h2, _, _ = _modulated_norm(x1, nw_ref[...], sc_ref[...], sh_ref[...])
            h2_s[...] = h2.astype(BF16)
            acc_s[...] = jnp.zeros_like(acc_s)

        part = None
        for b in range(MLP_BLOCKS):
            up = _bdot(h2_s[...], wup_ref[b, 0])
            down = _bdot(jnp.square(jnp.maximum(up, 0.0)), wdn_ref[b, 0])
            part = down if part is None else part + down
        acc_s[...] += part

        @pl.when(j == nj - 1)
        def _():
            ff_ref[...] = acc_s[...].astype(BF16)
            x2_ref[...] = x1_ref[...] + g2_ref[...] * acc_s[...]

    tk = lambda n: pl.BlockSpec((t, n), lambda i, j: (i, 0))
    return _call(body, "out_mlp_fwd", (s // t, nj),
                 [tk(LRU_W), tk(GDN_W), tk(D_MODEL)] + _wo_specs(l) + [_vec(D_MODEL)] * 5
                 + [pl.BlockSpec((MLP_BLOCKS, 1, D_MODEL, fc), lambda i, j: (j, l, 0, 0)),
                    pl.BlockSpec((MLP_BLOCKS, 1, fc, D_MODEL), lambda i, j: (j, l, 0, 0))],
                 [tk(D_MODEL)] * 4,
                 [_sds((s, D_MODEL)), _sds((s, D_MODEL), BF16), _sds((s, D_MODEL), BF16), _sds((s, D_MODEL))],
                 scratch=[pltpu.VMEM((t, D_MODEL), BF16), pltpu.VMEM((t, D_MODEL), F32)])(
                     ol, og, x, wo, wo, g1, nw2, sc2, sh2, g2, wup, wdn)


def _mlp_bwd(dx2, x1, ff, nw2, sc2, sh2, g2, wup, wdn, l):
    s = x1.shape[0]
    t = _tile(s, 512)
    nj = wup.shape[0] // MLP_BLOCKS
    fc = wup.shape[3]

    def body(dx2_ref, x1_ref, ff_ref, nw_ref, sc_ref, sh_ref, g2_ref, wup_ref, wdn_ref,
             act_ref, dup_ref, h2_ref, dff_ref, dx1_ref, rows_ref, dh2_s):
        i, j = pl.program_id(0), pl.program_id(1)

        @pl.when((i == 0) & (j == 0))
        def _():
            rows_ref[...] = jnp.zeros_like(rows_ref)

        @pl.when(j == 0)
        def _():
            h2, _, _ = _modulated_norm(x1_ref[...], nw_ref[...], sc_ref[...], sh_ref[...])
            h2_ref[...] = h2.astype(BF16)
            dx2 = dx2_ref[...]
            dff_ref[...] = (dx2 * g2_ref[...]).astype(BF16)
            rows_ref[2:3, :] += jnp.sum(dx2 * ff_ref[...].astype(F32), axis=0, keepdims=True)
            dh2_s[...] = jnp.zeros_like(dh2_s)

        part = None
        for b in range(MLP_BLOCKS):
            cols = slice(b * fc, (b + 1) * fc)
            up = _bdot(h2_ref[...], wup_ref[b, 0])
            ru = jnp.maximum(up, 0.0)
            act_ref[:, cols] = (ru * ru).astype(BF16)
            dup = (_bdot(dff_ref[...], wdn_ref[b, 0], NT) * (2.0 * ru)).astype(BF16)
            dup_ref[:, cols] = dup
            back = _bdot(dup, wup_ref[b, 0], NT)
            part = back if part is None else part + back
        dh2_s[...] += part

        @pl.when(j == nj - 1)
        def _():
            xv = x1_ref[...]
            _, n, r = _modulated_norm(xv, nw_ref[...], sc_ref[...], sh_ref[...])
            dx, dsh, dsc, dnw = _modulated_norm_bwd(dh2_s[...], xv, n, r, nw_ref[...], sc_ref[...])
            dx1_ref[...] = dx2_ref[...] + dx
            rows_ref[0:1, :] += dsh
            rows_ref[1:2, :] += dsc
            rows_ref[3:4, :] += dnw

    tk = lambda n: pl.BlockSpec((t, n), lambda i, j: (i, 0))
    tj = pl.BlockSpec((t, MLP_BLOCKS * fc), lambda i, j: (i, j))
    return _call(body, "mlp_bwd", (s // t, nj),
                 [tk(D_MODEL)] * 3 + [_vec(D_MODEL)] * 4
                 + [pl.BlockSpec((MLP_BLOCKS, 1, D_MODEL, fc), lambda i, j: (j, l, 0, 0)),
                    pl.BlockSpec((MLP_BLOCKS, 1, fc, D_MODEL), lambda i, j: (j, l, 0, 0))],
                 [tj, tj, tk(D_MODEL), tk(D_MODEL), tk(D_MODEL), pl.BlockSpec((8, D_MODEL), lambda i, j: (0, 0))],
                 [_sds((s, D_FF), BF16), _sds((s, D_FF), BF16), _sds((s, D_MODEL), BF16),
                  _sds((s, D_MODEL), BF16), _sds((s, D_MODEL)), _sds((8, D_MODEL))],
                 scratch=[pltpu.VMEM((t, D_MODEL), F32)], vmem_mb=56)(dx2, x1, ff, nw2, sc2, sh2, g2, wup, wdn)


def _outproj_bwd(dx1, mix, g1, wo, l):
    s = dx1.shape[0]
    t = _tile(s, 512)

    def body(dx1_ref, mix_ref, g1_ref, wol_ref, wog_ref, dmix_ref, dol_ref, dog_ref, rows_ref):
        @pl.when(pl.program_id(0) == 0)
        def _():
            rows_ref[...] = jnp.zeros_like(rows_ref)

        dx1v = dx1_ref[...]
        rows_ref[0:1, :] += jnp.sum(dx1v * mix_ref[...].astype(F32), axis=0, keepdims=True)
        dmix = (dx1v * g1_ref[...]).astype(BF16)
        dmix_ref[...] = dmix
        dol_ref[...] = _bdot(dmix, _wo_half(wol_ref), NT)
        dog_ref[...] = _bdot(dmix, _wo_half(wog_ref), NT)

    return _call(body, "outproj_bwd", (s // t,),
                 [_tok(t, D_MODEL), _tok(t, D_MODEL), _vec(D_MODEL)] + _wo_specs(l),
                 [_tok(t, D_MODEL), _tok(t, LRU_W), _tok(t, GDN_W), pl.BlockSpec((8, D_MODEL), lambda i: (0, 0))],
                 [_sds((s, D_MODEL), BF16), _sds((s, LRU_W)), _sds((s, GDN_W)), _sds((8, D_MODEL))])(dx1, mix, g1, wo, wo)


def _tn_matmul(a, b, name, out=None, l=0, blocked=False, row_block=0):
    s, m = a.shape
    n = b.shape[1]
    ts, bm = _tile(s, 2048), _tile(m, 1024)
    bn = next(w for w in ((512,) if blocked else (1024, 512, 640, 384, 256, 128)) if n % w == 0)

    def body(a_ref, b_ref, *rest):
        o_ref = rest[-1]

        @pl.when(pl.program_id(2) == 0)
        def _():
            o_ref[...] = jnp.zeros_like(o_ref)

        acc = _bdot(a_ref[...], b_ref[...], TN)
        o_ref[...] += acc.reshape(o_ref.shape)

    in_specs = [pl.BlockSpec((ts, bm), lambda i, j, k: (k, i)), pl.BlockSpec((ts, bn), lambda i, j, k: (k, j))]
    grid = (m // bm, n // bn, s // ts)
    if out is None:
        return _call(body, name, grid, in_specs, pl.BlockSpec((bm, bn), lambda i, j, k: (i, j)), _sds((m, n)))(a, b)
    if blocked:
        out_spec = pl.BlockSpec((1, 1, bm, bn), lambda i, j, k: (l, j, i, 0))
    else:
        out_spec = pl.BlockSpec((1, bm, bn), lambda i, j, k: (l, i + row_block * (m // bm), j))
    return _call(body, name, grid, in_specs + [pl.BlockSpec(memory_space=pl.ANY)], out_spec,
                 _sds(out.shape), aliases={2: 0})(a, b, out)


def _final_fwd_bwd(x, target, fw):
    s = x.shape[0]
    t = _tile(s, 512)

    def body(x_ref, tg_ref, fw_ref, dx_ref, rows_ref):
        @pl.when(pl.program_id(0) == 0)
        def _():
            rows_ref[...] = jnp.zeros_like(rows_ref)

        xv = x_ref[...]
        fwv = fw_ref[...]
        r = lax.rsqrt(jnp.mean(xv * xv, axis=-1, keepdims=True) + EPS)
        err = xv * r * fwv - tg_ref[...]
        part = 0.5 * jnp.sum(jnp.mean(err * err, axis=-1, keepdims=True), axis=0, keepdims=True)
        rows_ref[1:2, :] += jnp.broadcast_to(part, (1, D_MODEL))
        dy = err * (1.0 / D_MODEL)
        rows_ref[0:1, :] += jnp.sum(dy * xv * r, axis=0, keepdims=True)
        dxn = dy * fwv
        dx_ref[...] = r * dxn - xv * (r * r * r) * jnp.mean(dxn * xv, axis=-1, keepdims=True)

    return _call(body, "final_fwd_bwd", (s // t,),
                 [_tok(t, D_MODEL), _tok(t, D_MODEL), _vec(D_MODEL)],
                 [_tok(t, D_MODEL), pl.BlockSpec((8, D_MODEL), lambda i: (0, 0))],
                 [_sds((s, D_MODEL)), _sds((8, D_MODEL))])(x, target, fw)


def _adamw(w, g, m, v):
    m = ADAM_B1 * m + (1.0 - ADAM_B1) * g
    v = ADAM_B2 * v + (1.0 - ADAM_B2) * (g * g)
    m_hat = m / (1.0 - ADAM_B1 ** ADAM_STEP)
    v_hat = v / (1.0 - ADAM_B2 ** ADAM_STEP)
    return -ADAM_LR * (m_hat / (jnp.sqrt(v_hat) + ADAM_EPS) + ADAM_WD * w), m, v


def _mod_local(c_all, wmod, bmod_cols):
    nl, _, cols = wmod.shape

    def body(c_ref, w_ref, b_ref, o_ref):
        cv = c_ref[...]
        o_ref[0] = _bdot(cv * jax.nn.sigmoid(cv), w_ref[0]) + b_ref[0]

    return _call(body, "mod_local", (nl,),
                 [_whole(c_all), pl.BlockSpec((1, D_MODEL, cols), lambda l: (l, 0, 0)),
                  pl.BlockSpec((1, 1, cols), lambda l: (l, 0, 0))],
                 pl.BlockSpec((1, N_DEV, cols), lambda l: (l, 0, 0)), _sds((nl, N_DEV, cols)))(c_all, wmod, bmod_cols)


def _wmod_update(c_all, dmod_cols, w, m, v):
    nl, _, cols = w.shape

    def body(c_ref, d_ref, w_ref, m_ref, v_ref, g_ref, dl_ref, nm_ref, nv_ref):
        cv = c_ref[...]
        g = _bdot(cv * jax.nn.sigmoid(cv), d_ref[0], TN)
        g_ref[0] = g
        dl_ref[0], nm_ref[0], nv_ref[0] = _adamw(w_ref[0], g, m_ref[0], v_ref[0])

    wspec = pl.BlockSpec((1, D_MODEL, cols), lambda l: (l, 0, 0))
    return _call(body, "wmod_update", (nl,),
                 [_whole(c_all), pl.BlockSpec((1, N_DEV, cols), lambda l: (l, 0, 0)), wspec, wspec, wspec],
                 [wspec] * 4, [_sds(w.shape)] * 4)(c_all, dmod_cols, w, m, v)


def _sum_devices(gathered):
    _, r, _ = gathered.shape

    def body(g_ref, o_ref):
        acc = g_ref[0]
        for d in range(1, N_DEV):
            acc = acc + g_ref[d]
        o_ref[...] = acc

    return _call(body, "sum_devices", (1,), [_whole(gathered)], pl.BlockSpec((r, LANES), lambda i: (0, 0)),
                 _sds((r, LANES)))(gathered)


def _adam_flat(w, g, m, v):
    r = w.shape[0]

    def body(w_ref, g_ref, m_ref, v_ref, dl_ref, nm_ref, nv_ref):
        dl_ref[...], nm_ref[...], nv_ref[...] = _adamw(w_ref[...], g_ref[...], m_ref[...], v_ref[...])

    spec = pl.BlockSpec((r, LANES), lambda i: (0, 0))
    return _call(body, "adam_small", (1,), [spec] * 4, [spec] * 3, [_sds((r, LANES))] * 3)(w, g, m, v)


def _pair_add(x, p, others):
    _, r, c = x.shape
    tr = _tile(r, 128 if c > 512 else 256)

    def body(others_ref, x_ref, p_ref, o_ref):
        o_ref[...] = (x_ref[...] + p_ref[...]).astype(BF16)

    return _call(body, "pair_add", (3, r // tr),
                 [pl.BlockSpec((1, tr, c), lambda q, i, others_ref: (others_ref[q], i, 0)),
                  pl.BlockSpec((1, tr, c), lambda q, i, others_ref: (others_ref[3 + q], i, 0))],
                 pl.BlockSpec((1, tr, c), lambda q, i, others_ref: (q, i, 0)), _sds((3, r, c), BF16),
                 prefetch=1)(others, x, p)


def _reduce_adam(x, p, q, place, w, m, v, l, outs):
    _, r, c = x.shape
    tr = _tile(r, 128 if c > 512 else 256)

    def body(place_ref, x_ref, p_ref, q_ref, w_ref, m_ref, v_ref, *rest):
        g_ref, dl_ref, nm_ref, nv_ref = rest[-4:]
        g = (((x_ref[0] + p_ref[0]) + q_ref[0].astype(F32)) + q_ref[1].astype(F32)) + q_ref[2].astype(F32)
        g_ref[0] = g
        dl_ref[0], nm_ref[0], nv_ref[0] = _adamw(w_ref[0], g, m_ref[0], v_ref[0])

    flat = pl.BlockSpec((1, tr, c), lambda i, place_ref: (l, i, 0))
    through = pl.BlockSpec(memory_space=pl.ANY)
    return _call(body, "reduce_adam", (r // tr,),
                 [pl.BlockSpec((1, tr, c), lambda i, place_ref: (place_ref[0], i, 0)),
                  pl.BlockSpec((1, tr, c), lambda i, place_ref: (place_ref[1], i, 0)),
                  pl.BlockSpec((3, tr, c), lambda i, place_ref: (0, i, 0)), flat, flat, flat] + [through] * 4,
                 [flat] * 4, [_sds(w.shape)] * 4, prefetch=1, aliases={7 + k: k for k in range(4)})(
                     place, x, p, q, w, m, v, *outs)


def _place():
    return lax.axis_index("x"), lax.axis_index("y"), lax.axis_index("c")


def _all_gather(xs, name, space):
    n = len(xs)

    def body(*refs):
        x_refs, o_refs = refs[:n], refs[n:2 * n]
        send_sems, recv_sems, local_sems = refs[2 * n:]
        x, y, c = _place()
        me, sibling = (x, y, c), (x, y, 1 - c)
        chips = [(1 - x, y), (x, 1 - y), (1 - x, 1 - y)]

        def blk(a, p):
            return o_refs[a].at[4 * p[0] + 2 * p[1] + p[2]]

        def copy(a, k, block, to, src=None):
            return pltpu.make_async_remote_copy(
                src_ref=blk(a, block) if src is None else src, dst_ref=blk(a, block),
                send_sem=send_sems.at[a, k], recv_sem=recv_sems.at[a, k], device_id=to, device_id_type=MESH)

        mine = [pltpu.make_async_copy(x_refs[a], blk(a, me), local_sems.at[a]) for a in range(n)]
        for cp in mine:
            cp.start()
        first = []
        for a in range(n):
            first.append(copy(a, 0, me, sibling, src=x_refs[a]))
            first += [copy(a, 1 + j, me, (*chip, c), src=x_refs[a]) for j, chip in enumerate(chips)]
        for cp in first:
            cp.start()
        passed = []
        for j, chip in enumerate(chips):
            for a in range(n):
                copy(a, 1 + j, (*chip, c), me).wait_recv()
                cp = copy(a, 4 + j, (*chip, c), sibling)
                cp.start()
                passed.append(cp)
        for a in range(n):
            copy(a, 0, sibling, me).wait_recv()
        for j, chip in enumerate(chips):
            for a in range(n):
                copy(a, 4 + j, (*chip, 1 - c), me).wait_recv()
        for cp in first + passed:
            cp.wait_send()
        for cp in mine:
            cp.wait()

    spec = pl.BlockSpec(memory_space=space)
    return pl.pallas_call(
        body, name=name, out_shape=[_sds((N_DEV,) + a.shape, a.dtype) for a in xs],
        in_specs=[spec] * n, out_specs=[spec] * n,
        scratch_shapes=[pltpu.SemaphoreType.DMA((n, 7)), pltpu.SemaphoreType.DMA((n, 7)),
                        pltpu.SemaphoreType.DMA((n,))])(*xs)


_HBM_SPEC = pl.BlockSpec(memory_space=pltpu.HBM)
_SEM_SPEC = pl.BlockSpec(memory_space=pltpu.SEMAPHORE)
_EFFECT = pltpu.SideEffectType.DATAFLOW_SIDE_EFFECTING


def _descriptors(plan, src_refs, land_refs, send_sems, recv_sems, which=None):
    return [pltpu.make_async_remote_copy(src_ref=s, dst_ref=d, send_sem=send_sems.at[k], recv_sem=recv_sems.at[k],
                                         device_id=dev, device_id_type=MESH)
            for k, (s, d, dev) in enumerate(plan(src_refs, land_refs)) if which is None or k in which]


def _split_start(name, plan, n, srcs, lands, after):
    ns, nb = len(srcs), len(srcs) + len(lands)
    after = list(after) if isinstance(after, (list, tuple)) else [after]
    sems = nb + len(after)

    def body(*refs):
        for cp in _descriptors(plan, refs[:ns], refs[ns:nb], refs[sems], refs[sems + 1]):
            cp.start()
        refs[-1][...] = jnp.zeros_like(refs[-1])

    bufs = [pltpu.with_memory_space_constraint(a, pltpu.HBM) for a in list(srcs) + list(lands)]
    outs = pl.pallas_call(
        body, name=name,
        out_shape=(pltpu.SemaphoreType.DMA((n,)), pltpu.SemaphoreType.DMA((n,)))
        + tuple(pltpu.HBM(a.shape, a.dtype) for a in bufs) + (_sds((8, LANES)),),
        in_specs=[_HBM_SPEC] * nb + [pl.BlockSpec(memory_space=pl.ANY)] * len(after),
        out_specs=(_SEM_SPEC, _SEM_SPEC) + (_HBM_SPEC,) * nb + (pl.BlockSpec(memory_space=pltpu.VMEM),),
        input_output_aliases={i: 2 + i for i in range(nb)},
        compiler_params=pltpu.CompilerParams(has_side_effects=_EFFECT))(*bufs, *after)
    return dict(send=outs[0], recv=outs[1], srcs=list(outs[2:2 + ns]), lands=list(outs[2 + ns:2 + nb]), token=outs[-1])


def _split_wait(name, plan, flight, which, after):
    srcs, lands = flight["srcs"], flight["lands"]
    ns, nb = len(srcs), len(srcs) + len(lands)

    def body(*refs):
        for cp in _descriptors(plan, refs[:ns], refs[ns:nb], refs[nb], refs[nb + 1], set(which)):
            cp.wait_send()
            cp.wait_recv()

    outs = pl.pallas_call(
        body, name=name, out_shape=tuple(pltpu.HBM(a.shape, a.dtype) for a in srcs + lands),
        in_specs=[_HBM_SPEC] * nb + [_SEM_SPEC, _SEM_SPEC, pl.BlockSpec(memory_space=pl.ANY)],
        out_specs=(_HBM_SPEC,) * nb, input_output_aliases={i: i for i in range(nb)},
        compiler_params=pltpu.CompilerParams(has_side_effects=_EFFECT))(*srcs, *lands, flight["send"], flight["recv"],
                                                                       after)
    return dict(flight, srcs=list(outs[:ns]), lands=list(outs[ns:nb]))


GATHER_PEERS = N_DEV - 1


def _gather_plan(items):
    def plan(src_refs, land_refs):
        x, y, c = _place()
        me = 4 * x + 2 * y + c
        out = []
        for a, l in items:
            for r in range(1, N_DEV):
                peer = (1 - x if r & 4 else x, 1 - y if r & 2 else y, 1 - c if r & 1 else c)
                out.append((src_refs[a].at[l], land_refs[a].at[me, l], peer))
        return out

    return plan


def _pair_plan(narr):
    def plan(src_refs, land_refs):
        x, y, c = _place()
        return [(src_refs[a].at[2 * q + (1 - c)], land_refs[a].at[q], (x, y, 1 - c))
                for a in range(narr) for q in range(4)]

    return plan


def _chip_plan(narr):
    def plan(src_refs, land_refs):
        x, y, c = _place()
        chips = [(1 - x, y), (x, 1 - y), (1 - x, 1 - y)]
        return [(src_refs[a].at[r], land_refs[a].at[r], (*chip, c)) for a in range(narr) for r, chip in enumerate(chips)]

    return plan


class _GradReducer:
    def __init__(self, tag, names, w, mom, var, place, others):
        self.tag, self.names, self.w, self.mom, self.var, self.place = tag, names, w, mom, var, place
        self.others = others
        self.outs = {k: [lax.empty(w[k].shape, F32) for _ in range(4)] for k in names}
        self.n = len(names)

    def start(self, l, grads):
        self.l, self.xs = l, [grads[k] for k in self.names]
        lands = [lax.empty((4,) + a.shape[1:], F32) for a in self.xs]
        self.pair = _split_start(f"{self.tag}_pair_start{l}", _pair_plan(self.n), 4 * self.n, self.xs, lands, ())
        return self.pair["token"][0, 0]

    def middle(self, after):
        self.pair = _split_wait(f"{self.tag}_pair_wait{self.l}", _pair_plan(self.n), self.pair, range(4 * self.n),
                                after)
        self.xs, self.ps = self.pair["srcs"], self.pair["lands"]
        ys = [_pair_add(x, p, self.others) for x, p in zip(self.xs, self.ps)]
        lands = [lax.empty((3,) + a.shape[1:], BF16) for a in ys]
        self.chip = _split_start(f"{self.tag}_chip_start{self.l}", _chip_plan(self.n), 3 * self.n, ys, lands, ())
        return self.chip["token"][0, 0]

    def finish(self, after):
        chip = _split_wait(f"{self.tag}_chip_wait{self.l}", _chip_plan(self.n), self.chip, range(3 * self.n), after)
        for k, x, p, q in zip(self.names, self.xs, self.ps, chip["lands"]):
            self.outs[k] = _reduce_adam(x, p, q, self.place, self.w[k], self.mom[k], self.var[k], self.l, self.outs[k])


def _size(shape):
    size = 1
    for d in shape:
        size *= d
    return size


def _slab_rows(shape):
    return -(-_size(shape) // (8 * LANES)) * 8


def _pack(arrs):
    parts = []
    for a in arrs:
        flat = a.reshape(-1).astype(F32)
        parts.append(jnp.pad(flat, (0, _slab_rows(a.shape) * LANES - flat.shape[0])).reshape(-1, LANES))
    return jnp.concatenate(parts, axis=0)


def _unpack(slab, shapes):
    out, off = [], 0
    for shp in shapes:
        rows = _slab_rows(shp)
        out.append(slab[off:off + rows].reshape(-1)[:_size(shp)].reshape(shp))
        off += rows
    return out


def _dense_blocks(w):
    eye = jnp.eye(LRU_BLOCKS, dtype=w.dtype)
    return (eye[:, None, :, None] * w[:, :, None, :]).reshape(LRU_W, LRU_W)


def _diag_blocks(dense):
    on_diagonal = jnp.eye(LRU_BLOCKS, dtype=bool)[:, None, :, None]
    return jnp.sum(jnp.where(on_diagonal, dense.reshape(LRU_BLOCKS, LRU_BLOCK, LRU_BLOCKS, LRU_BLOCK), 0.0), axis=2)


def _alpha_lanes(v):
    return jnp.zeros((1, BA_PAD), F32).at[0, HEADS:2 * HEADS].set(v)


def _local_step(x, target, mod, p, fetch, reducers=None):
    nl = mod.shape[0]
    row = lambda v: v.reshape(1, -1)
    masks = _gdn_masks()
    saved = []
    xc = x
    for l in range(nl):
        win, wba, lin = fetch(l, "in", xc)
        mv = [row(mod[l, k * D_MODEL:(k + 1) * D_MODEL]) for k in range(N_MOD)]
        sh1, sc1, g1, sh2, sc2, g2 = mv
        nw1, nw2 = row(p["norm_mix_w"][l]), row(p["norm_mlp_w"][l])
        wa, wx = _dense_blocks(p["lru_gate_a_w"][l]).astype(BF16), _dense_blocks(p["lru_gate_x_w"][l]).astype(BF16)
        lru_args = (p["lru_conv_w"][l], row(p["lru_conv_b"][l]), wa, wx, row(p["lru_gate_a_b"][l]),
                    row(p["lru_gate_x_b"][l]), row(p["lru_lambda"][l]), row(p["lru_norm_w"][l]))
        gdn_args = (p["gdn_conv_w"][l], _alpha_lanes(p["gdn_a_log"][l]), _alpha_lanes(p["gdn_dt_bias"][l]), masks)
        gnw = row(p["gdn_norm_w"][l])
        proj, ba = _inproj_fwd(xc, nw1, sc1, sh1, win, wba, lin)
        ol, hs = _lru_fwd(proj, *lru_args)
        *prep, tinv = _gdn_prep_fwd(proj, ba, *gdn_args)
        og, st = _gdn_scan_fwd(prep, proj, gnw)
        wo, wup, wdn = fetch(l, "rest", og)
        x1, mix, ff, x2 = _out_mlp_fwd(ol, og, xc, wo, g1, nw2, sc2, sh2, g2, wup, wdn, l)
        saved.append(dict(x=xc, mv=mv, nw1=nw1, nw2=nw2, lru_args=lru_args, gdn_args=gdn_args, gnw=gnw, proj=proj,
                          ba=ba, ol=ol, hs=hs, prep=prep, tinv=tinv, og=og, st=st, x1=x1, mix=mix, ff=ff,
                          win=win, wba=wba, lin=lin))
        xc = x2

    dx, frows = _final_fwd_bwd(xc, target, row(p["final_norm_w"]))
    loss_part = frows[1, 0]
    small = {k: [None] * nl for k in ("norm_mix_w", "norm_mlp_w", "lru_conv_w", "lru_conv_b", "lru_gate_a_w",
                                      "lru_gate_a_b", "lru_gate_x_w", "lru_gate_x_b", "lru_lambda", "lru_norm_w",
                                      "gdn_conv_w", "gdn_a_log", "gdn_dt_bias", "gdn_norm_w")}
    fc = D_FF // N_DEV
    big = [None] * nl
    dmod = [None] * nl
    mlp_red, mix_red = reducers or (None, None)
    busy = False
    for l in reversed(range(nl)):
        sv = saved[l]
        sh1, sc1, g1, sh2, sc2, g2 = sv["mv"]
        gnw = sv["gnw"]
        if busy:
            g2 = g2 + started
        act, dup, h2b, dffb, dx1, rows2 = _mlp_bwd(dx, sv["x1"], sv["ff"], sv["nw2"], sc2, sh2, g2, wup, wdn, l)
        if busy:
            g1 = g1 + mix_red.middle(dx1)
        g_up = _tn_matmul(h2b, dup, "grad_w_up", out=lax.empty((1, N_DEV, D_MODEL, fc), F32), blocked=True)[0]
        g_down = _tn_matmul(act, dffb, "grad_w_down", out=lax.empty((1, D_FF, D_MODEL), F32))
        g_down = g_down.reshape(N_DEV, fc, D_MODEL)
        if mlp_red is not None:
            g1 = g1 + mlp_red.start(l, dict(w_up=g_up, w_down=g_down))
        dmix, dol, dog, rows1 = _outproj_bwd(dx1, sv["mix"], g1, wo, l)
        g_out = _tn_matmul(sv["ol"], dmix, "grad_w_out_lru", out=lax.empty((1, D_MODEL, D_MODEL), F32))
        g_out = _tn_matmul(sv["og"], dmix, "grad_w_out_gdn", out=g_out, row_block=1)
        dpl, dwa, dwx, lrows = _lru_bwd(dol, sv["proj"], sv["hs"], *sv["lru_args"])
        if mlp_red is not None:
            gnw = gnw + mlp_red.middle(dpl)
        *cts, dpz, gnrow = _gdn_scan_bwd(dog, sv["prep"], sv["st"], sv["proj"], gnw)
        dpq, dba, dcw, dpar = _gdn_prep_bwd(cts, sv["tinv"], sv["proj"], sv["ba"], *sv["gdn_args"])
        dx, hb, rows0 = _inproj_bwd(dpl, dpq, dpz, dba, sv["x"], dx1, sv["nw1"], sc1, sh1, sv["win"], sv["wba"],
                                    sv["lin"])
        if busy:
            mix_red.finish(dx)
        if mlp_red is not None:
            mlp_red.finish(dx)
        dproj = jnp.concatenate([dpl, dpq, dpz, dba], axis=1)
        g_in = jnp.transpose(_tn_matmul(hb, dproj, "grad_w_in")[:, :IN_COLS].reshape(
            D_MODEL, N_DEV, IN_COLS // N_DEV), (1, 0, 2))
        big[l] = dict(w_in=g_in, w_out=g_out.reshape(N_DEV, D_MODEL // N_DEV, D_MODEL), w_up=g_up,
                      w_down=g_down)
        if mix_red is not None:
            started, busy = mix_red.start(l, big[l]), True
        dmod[l] = jnp.concatenate([rows0[0], rows0[1], rows1[0], rows2[0], rows2[1], rows2[2]])
        small["norm_mix_w"][l], small["norm_mlp_w"][l] = rows0[2], rows2[3]
        small["lru_conv_w"][l], small["lru_conv_b"][l] = lrows[8:8 + CONV_K], lrows[0]
        small["lru_gate_a_w"][l], small["lru_gate_x_w"][l] = _diag_blocks(dwa), _diag_blocks(dwx)
        small["lru_gate_a_b"][l], small["lru_gate_x_b"][l] = lrows[1], lrows[2]
        small["lru_lambda"][l], small["lru_norm_w"][l] = lrows[3], lrows[4]
        small["gdn_conv_w"][l] = dcw
        small["gdn_a_log"][l], small["gdn_dt_bias"][l] = dpar[0, HEADS:2 * HEADS], dpar[1, HEADS:2 * HEADS]
        small["gdn_norm_w"][l] = gnrow[0]
    small = {k: jnp.stack(v) for k, v in small.items()}
    small["final_norm_w"] = frows[0]
    return loss_part, dx, big, small, jnp.stack(dmod)


SMALL_REPLICATED = ("norm_mix_w", "norm_mlp_w", "b_mod", "lru_conv_b", "lru_gate_a_w", "lru_gate_a_b", "lru_gate_x_w",
                    "lru_gate_x_b", "lru_lambda", "lru_norm_w", "gdn_a_log", "gdn_dt_bias", "gdn_norm_w",
                    "final_norm_w")
SMALL_SHARDED = ("lru_conv_w", "gdn_conv_w")
WEIGHT_ORDER = ("norm_mix_w", "norm_mlp_w", "w_mod", "b_mod", "w_in", "lru_conv_w", "lru_conv_b", "lru_gate_a_w",
                "lru_gate_a_b", "lru_gate_x_w", "lru_gate_x_b", "lru_lambda", "lru_norm_w", "gdn_conv_w", "gdn_a_log",
                "gdn_dt_bias", "gdn_norm_w", "w_out", "w_up", "w_down", "final_norm_w")


def kernel(x, c, norm_mix_w, norm_mlp_w, w_mod, b_mod, w_in, lru_conv_w, lru_conv_b, lru_gate_a_w, lru_gate_a_b, lru_gate_x_w, lru_gate_x_b, lru_lambda, lru_norm_w, gdn_conv_w, gdn_a_log, gdn_dt_bias, gdn_norm_w, w_out, w_up, w_down, final_norm_w, loss_target, m_norm_mix_w, m_norm_mlp_w, m_w_mod, m_b_mod, m_w_in, m_lru_conv_w, m_lru_conv_b, m_lru_gate_a_w, m_lru_gate_a_b, m_lru_gate_x_w, m_lru_gate_x_b, m_lru_lambda, m_lru_norm_w, m_gdn_conv_w, m_gdn_a_log, m_gdn_dt_bias, m_gdn_norm_w, m_w_out, m_w_up, m_w_down, m_final_norm_w, v_norm_mix_w, v_norm_mlp_w, v_w_mod, v_b_mod, v_w_in, v_lru_conv_w, v_lru_conv_b, v_lru_gate_a_w, v_lru_gate_a_b, v_lru_gate_x_w, v_lru_gate_x_b, v_lru_lambda, v_lru_norm_w, v_gdn_conv_w, v_gdn_a_log, v_gdn_dt_bias, v_gdn_norm_w, v_w_out, v_w_up, v_w_down, v_final_norm_w):
    args = dict(locals())
    w = {k: args[k] for k in WEIGHT_ORDER}
    mom = {k: args["m_" + k] for k in WEIGHT_ORDER}
    var = {k: args["v_" + k] for k in WEIGHT_ORDER}
    nl = w_in.shape[0]
    px, py, pc = _place()
    me = 4 * px + 2 * py + pc
    other_chips = [2 * (1 - px) + py, 2 * px + (1 - py), 2 * (1 - px) + (1 - py)]
    others = jnp.stack([2 * q + pc for q in other_chips] + other_chips).astype(jnp.int32)

    shapes0 = [c.shape, lru_conv_w.shape, gdn_conv_w.shape]
    (g0,) = _all_gather([_pack([c, lru_conv_w, gdn_conv_w])], "gather_cond", pltpu.VMEM)
    per_dev = [_unpack(g0[d], shapes0) for d in range(N_DEV)]
    c_all = jnp.concatenate([pd[0] for pd in per_dev], axis=0)
    lru_conv_full = jnp.concatenate([pd[1] for pd in per_dev], axis=-1)
    gdn_conv_full = jnp.concatenate([pd[2] for pd in per_dev], axis=-1)

    cols = w_mod.shape[2]
    bmod_cols = lax.dynamic_slice_in_dim(b_mod, me * cols, cols, axis=1).reshape(nl, 1, cols)
    mod_cols = _mod_local(c_all, w_mod, bmod_cols)
    (g1,) = _all_gather([mod_cols.reshape(nl * N_DEV, cols)], "gather_mod", pltpu.VMEM)
    g1 = g1.reshape(N_DEV, nl, N_DEV, cols)
    mod = jnp.transpose(lax.dynamic_index_in_dim(g1, me, axis=2, keepdims=False), (1, 0, 2)).reshape(nl, N_DEV * cols)

    shards = [a.astype(BF16) for a in (w_in, w_out, w_up, w_down)]
    (first_in,) = _all_gather([shards[0][:1]], "gather_w_in_first", pl.ANY)
    items = [(a, 0) for a in (1, 2, 3)] + [(a, l) for l in range(1, nl) for a in range(4)]
    plan = _gather_plan(items)
    lands = [lax.dynamic_update_slice_in_dim(lax.empty((N_DEV,) + a.shape, BF16), a[None], me, axis=0) for a in shards]
    flight = [_split_start("gather_weights_start", plan, len(items) * GATHER_PEERS, shards, lands, [first_in, mod])]
    mod = mod + flight[0]["token"][0, 0]

    def fetch(l, what, after):
        wanted = [k for k, (a, ll) in enumerate(items) if ll == l and (a == 0) == (what == "in")]
        if wanted:
            flight[0] = _split_wait(f"gather_weights_wait_{what}{l}", plan, flight[0],
                                    [k * GATHER_PEERS + r for k in wanted for r in range(GATHER_PEERS)], after)
        gin, gout, gup, gdn = flight[0]["lands"]
        if what == "rest":
            return gout, gup, gdn
        gin = first_in[:, 0] if l == 0 else gin[:, l]
        win = jnp.transpose(gin, (1, 0, 2)).reshape(1, D_MODEL, IN_COLS)
        wba = jnp.pad(win[:, :, IN_MAIN:], ((0, 0), (0, 0), (0, BA_PAD - (IN_COLS - IN_MAIN))))
        return win, wba, 0

    p = dict(w)
    p["lru_conv_w"], p["gdn_conv_w"] = lru_conv_full, gdn_conv_full

    order = ("w_in", "w_out", "w_up", "w_down")
    place = jnp.stack([me, 2 * px + py]).astype(jnp.int32)
    reducers = (_GradReducer("mlp_grad", ("w_up", "w_down"), w, mom, var, place, others),
                _GradReducer("mix_grad", ("w_in", "w_out"), w, mom, var, place, others))
    loss_part, grad_x, _, small, dmod = _local_step(x[0], loss_target[0], mod, p, fetch, reducers)
    loss = lax.psum(loss_part, MESH_AXES)

    small_names = sorted(small)
    slab = _pack([dmod] + [small[k] for k in small_names])
    (gs,) = _all_gather([slab], "gather_small_grads", pltpu.VMEM)
    chips_started = reducers[1].middle(gs)
    dmod_all = gs[:, :_slab_rows(dmod.shape)].reshape(N_DEV, nl, N_MOD * D_MODEL)
    summed = _unpack(_sum_devices(gs) + chips_started, [dmod.shape] + [small[k].shape for k in small_names])
    grads = dict(zip(small_names, summed[1:]))
    grads["b_mod"] = summed[0]
    for k, width in (("lru_conv_w", LRU_W // N_DEV), ("gdn_conv_w", 3 * GDN_W // N_DEV)):
        grads[k] = lax.dynamic_slice_in_dim(grads[k], me * width, width, axis=2)
    names = SMALL_REPLICATED + SMALL_SHARDED
    shapes = [w[k].shape for k in names]
    dl, nm, nv = _adam_flat(_pack([w[k] for k in names]), _pack([grads[k] for k in names]),
                            _pack([mom[k] for k in names]), _pack([var[k] for k in names]))
    delta = dict(zip(names, _unpack(dl, shapes)))
    new_m = dict(zip(names, _unpack(nm, shapes)))
    new_v = dict(zip(names, _unpack(nv, shapes)))

    dmod_cols = jnp.transpose(lax.dynamic_slice_in_dim(dmod_all, me * cols, cols, axis=2), (1, 0, 2))
    grads["w_mod"], delta["w_mod"], new_m["w_mod"], new_v["w_mod"] = _wmod_update(
        c_all, dmod_cols, w_mod, m_w_mod, v_w_mod)

    reducers[1].finish(new_v["w_mod"])
    for red in reducers:
        for k in red.names:
            grads[k], delta[k], new_m[k], new_v[k] = red.outs[k]

    return (loss, grad_x[None], *[grads[k] for k in WEIGHT_ORDER], *[delta[k] for k in WEIGHT_ORDER],
            *[new_m[k] for k in WEIGHT_ORDER], *[new_v[k] for k in WEIGHT_ORDER])
```

```python
import functools

import jax
import jax.numpy as jnp
from jax import lax
from jax.experimental import pallas as pl
from jax.experimental.pallas import tpu as pltpu

F32 = jnp.float32
BF16 = jnp.bfloat16

D_MODEL = 1024
LRU_W = 512
LRU_BLOCKS = 8
LRU_BLOCK = 64
LRU_C = 8.0
GDN_W = 512
HEADS = 4
HEAD_DIM = 128
CHUNK = 64
STACK = HEADS * CHUNK
CONV_K = 4
D_FF = 4096
N_MOD = 6
IN_COLS = 3080
IN_MAIN = 3072
BA_PAD = 128
EPS = 1e-6
N_DEV = 8
HALO = 8
MLP_BLOCKS = 4
PREP_CHUNKS = 2
SCAN_CHUNKS = 4
LANES = 128
ADAM_LR, ADAM_B1, ADAM_B2, ADAM_EPS, ADAM_WD, ADAM_STEP = 0.001, 0.9, 0.999, 1e-08, 0.01, 10
MESH_AXES = ("x", "y", "c")
MESH = pl.DeviceIdType.MESH

NN = (((1,), (0,)), ((), ()))
NT = (((1,), (1,)), ((), ()))
TN = (((0,), (0,)), ((), ()))


def _bdot(a, b, dims=NN):
    return lax.dot_general(a.astype(BF16), b.astype(BF16), dims, preferred_element_type=F32)


def _sdot(a, b, dims=NN):
    ah, bh = a.astype(BF16), b.astype(BF16)
    al, bl = (a - ah.astype(F32)).astype(BF16), (b - bh.astype(F32)).astype(BF16)
    return _bdot(ah, bh, dims) + (_bdot(al, bh, dims) + _bdot(ah, bl, dims))


def _hdot(a, b, dims=NN):
    return lax.dot_general(a, b, dims, precision=lax.Precision.HIGHEST, preferred_element_type=F32)


def _sds(shape, dtype=F32):
    return jax.ShapeDtypeStruct(tuple(shape), dtype)


def _tile(n, t):
    return min(n, t)


def _call(body, name, grid, in_specs, out_specs, out_shape, scratch=(), vmem_mb=48, prefetch=0, aliases=None):
    params = pltpu.CompilerParams(dimension_semantics=("arbitrary",) * len(grid), vmem_limit_bytes=vmem_mb * 2**20)
    if prefetch:
        spec = pltpu.PrefetchScalarGridSpec(num_scalar_prefetch=prefetch, grid=grid, in_specs=in_specs,
                                            out_specs=out_specs, scratch_shapes=list(scratch))
        return pl.pallas_call(body, name=name, grid_spec=spec, out_shape=out_shape, compiler_params=params,
                              input_output_aliases=aliases or {})
    return pl.pallas_call(body, name=name, grid=grid, in_specs=in_specs, out_specs=out_specs, out_shape=out_shape,
                          scratch_shapes=list(scratch), compiler_params=params, input_output_aliases=aliases or {})


def _tok(t, n, col=0):
    return pl.BlockSpec((t, n), lambda i, *_: (i, col))


def _vec(n):
    return pl.BlockSpec((1, n), lambda *_: (0, 0))


def _whole(a):
    nd = a.ndim
    return pl.BlockSpec(a.shape, lambda *_: (0,) * nd)


def _layer(l, *dims):
    return pl.BlockSpec((1,) + dims, lambda *_: (l,) + (0,) * len(dims))


def _gelu(y):
    c0, c1 = 0.7978845608028654, 0.044715
    return 0.5 * y * (1.0 + jnp.tanh(c0 * (y + c1 * y * y * y)))


def _gelu_grad(y):
    c0, c1 = 0.7978845608028654, 0.044715
    t = jnp.tanh(c0 * (y + c1 * y * y * y))
    return 0.5 * (1.0 + t) + 0.5 * y * (1.0 - t * t) * c0 * (1.0 + 3.0 * c1 * y * y)


def _softplus(v):
    return jnp.maximum(v, 0.0) + jnp.log(1.0 + jnp.exp(-jnp.where(v > 0, v, -v)))


@functools.partial(jax.custom_vjp, nondiff_argnums=(1,))
def _roll_rows(v, s):
    s = s % v.shape[0]
    return pltpu.roll(v, s, axis=0) if s else v


def _roll_rows_fwd(v, s):
    return _roll_rows(v, s), None


def _roll_rows_bwd(s, _, g):
    return (_roll_rows(g, -s),)


_roll_rows.defvjp(_roll_rows_fwd, _roll_rows_bwd)


@jax.custom_vjp
def _drop_halo(v):
    return v[HALO:]


def _drop_halo_fwd(v):
    return v[HALO:], None


def _drop_halo_bwd(_, g):
    return (jnp.concatenate([jnp.zeros((HALO, g.shape[1]), g.dtype), g], axis=0),)


_drop_halo.defvjp(_drop_halo_fwd, _drop_halo_bwd)


@functools.partial(jax.custom_vjp, nondiff_argnums=(1, 2))
def _split(v, n, axis):
    w = v.shape[axis] // n
    return tuple(lax.slice_in_dim(v, k * w, (k + 1) * w, axis=axis) for k in range(n))


def _split_fwd(v, n, axis):
    return _split(v, n, axis), None


def _split_bwd(n, axis, _, gs):
    return (jnp.concatenate(list(gs), axis=axis),)


_split.defvjp(_split_fwd, _split_bwd)


def _conv_taps(xw):
    return [_drop_halo(_roll_rows(xw, CONV_K - 1 - k)) for k in range(CONV_K)]


def _modulated_norm(xv, nw, sc, sh):
    r = lax.rsqrt(jnp.mean(xv * xv, axis=-1, keepdims=True) + EPS)
    n = xv * r * nw
    return n * (1.0 + sc) + sh, n, r


def _modulated_norm_bwd(dh, xv, n, r, nw, sc):
    dn = dh * (1.0 + sc)
    dxn = dn * nw
    dx = r * dxn - xv * (r * r * r) * jnp.mean(dxn * xv, axis=-1, keepdims=True)
    return (dx, jnp.sum(dh, axis=0, keepdims=True), jnp.sum(dh * n, axis=0, keepdims=True),
            jnp.sum(dn * xv * r, axis=0, keepdims=True))


def _inproj_fwd(x, nw, sc, sh, win, wba, l):
    s = x.shape[0]
    t = _tile(s, 512)

    def body(x_ref, nw_ref, sc_ref, sh_ref, win_ref, wba_ref, proj_ref, ba_ref):
        h, _, _ = _modulated_norm(x_ref[...], nw_ref[...], sc_ref[...], sh_ref[...])
        hb = h.astype(BF16)
        proj_ref[...] = _bdot(hb, win_ref[0])
        ba_ref[...] = _bdot(hb, wba_ref[0])

    return _call(body, "inproj_fwd", (s // t,),
                 [_tok(t, D_MODEL), _vec(D_MODEL), _vec(D_MODEL), _vec(D_MODEL), _layer(l, D_MODEL, IN_MAIN),
                  _layer(l, D_MODEL, BA_PAD)],
                 [_tok(t, IN_MAIN), _tok(t, BA_PAD)],
                 [_sds((s, IN_MAIN)), _sds((s, BA_PAD))])(x, nw, sc, sh, win, wba)


def _inproj_bwd(dpl, dpq, dpz, dba, x, dx1, nw, sc, sh, win, wba, l):
    s = x.shape[0]
    t = _tile(s, 512)

    def body(dpl_ref, dpq_ref, dpz_ref, dba_ref, x_ref, dx1_ref, nw_ref, sc_ref, sh_ref, win_ref, wba_ref,
             dx_ref, hb_ref, acc_ref):
        @pl.when(pl.program_id(0) == 0)
        def _():
            acc_ref[...] = jnp.zeros_like(acc_ref)

        dh = (_bdot(dpl_ref[...], win_ref[0, :, 0:2 * LRU_W], NT)
              + _bdot(dpq_ref[...], win_ref[0, :, 2 * LRU_W:2 * LRU_W + 3 * GDN_W], NT)
              + _bdot(dpz_ref[...], win_ref[0, :, 2 * LRU_W + 3 * GDN_W:IN_MAIN], NT)
              + _bdot(dba_ref[...], wba_ref[0], NT))
        xv = x_ref[...]
        h, n, r = _modulated_norm(xv, nw_ref[...], sc_ref[...], sh_ref[...])
        hb_ref[...] = h.astype(BF16)
        dx, dsh, dsc, dnw = _modulated_norm_bwd(dh, xv, n, r, nw_ref[...], sc_ref[...])
        dx_ref[...] = dx1_ref[...] + dx
        acc_ref[0:1, :] += dsh
        acc_ref[1:2, :] += dsc
        acc_ref[2:3, :] += dnw

    return _call(body, "inproj_bwd", (s // t,),
                 [_tok(t, 2 * LRU_W), _tok(t, 3 * GDN_W), _tok(t, GDN_W), _tok(t, BA_PAD), _tok(t, D_MODEL),
                  _tok(t, D_MODEL), _vec(D_MODEL), _vec(D_MODEL), _vec(D_MODEL), _layer(l, D_MODEL, IN_MAIN),
                  _layer(l, D_MODEL, BA_PAD)],
                 [_tok(t, D_MODEL), _tok(t, D_MODEL), pl.BlockSpec((8, D_MODEL), lambda i: (0, 0))],
                 [_sds((s, D_MODEL)), _sds((s, D_MODEL), BF16), _sds((8, D_MODEL))])(
                     dpl, dpq, dpz, dba, x, dx1, nw, sc, sh, win, wba)


def _lru_gates(xw, cw_rows, cb, wa, wx, gab, gxb, lam):
    taps = _conv_taps(xw)
    xr = cb + cw_rows[0] * taps[0] + cw_rows[1] * taps[1] + cw_rows[2] * taps[2] + cw_rows[3] * taps[3]
    xb = xr.astype(BF16)
    r = jax.nn.sigmoid(_bdot(xb, wa) + gab)
    i = jax.nn.sigmoid(_bdot(xb, wx) + gxb)
    z = jnp.exp(-jnp.where(lam > 0, lam, -lam))
    w1 = 1.0 + z
    log1p_z = jnp.where(w1 == 1.0, z, jnp.log(w1) * z / (w1 - 1.0))
    ls = jnp.minimum(lam, 0.0) - log1p_z
    la = LRU_C * r * ls
    a = jnp.exp(la)
    mm_raw = -jnp.tanh(la) * (a * a + 1.0)
    mult = jnp.sqrt(jnp.maximum(mm_raw, 1e-12))
    return dict(taps=taps, xr=xr, r=r, i=i, ls=ls, a=a, mm_raw=mm_raw, mult=mult)


def _lru_specs(s, t, tile_of):
    nh = t // HALO
    xl = pl.BlockSpec((t, LRU_W), lambda i: (tile_of(i), 0))
    yl = pl.BlockSpec((t, LRU_W), lambda i: (tile_of(i), 1))
    hx = pl.BlockSpec((HALO, LRU_W), lambda i: (jnp.maximum(tile_of(i) * nh - 1, 0), 0))
    return xl, yl, hx


def _lru_fwd(proj, cw, cb, wa, wx, gab, gxb, lam, lnw):
    s = proj.shape[0]
    t = _tile(s, 256)
    xl, yl, hx = _lru_specs(s, t, lambda i: i)

    def body(xl_ref, yl_ref, hx_ref, cw_ref, cb_ref, wa_ref, wx_ref, gab_ref, gxb_ref, lam_ref, lnw_ref,
             out_ref, h_ref, a_s, b_s, hc):
        i = pl.program_id(0)

        @pl.when(i == 0)
        def _():
            hc[...] = jnp.zeros_like(hc)

        halo = jnp.where(i > 0, hx_ref[...], 0.0)
        xw = jnp.concatenate([halo, xl_ref[...]], axis=0)
        g = _lru_gates(xw, [cw_ref[k:k + 1, :] for k in range(CONV_K)], cb_ref[...], wa_ref[...], wx_ref[...],
                       gab_ref[...], gxb_ref[...], lam_ref[...])
        a_s[...] = g["a"]
        b_s[...] = g["mult"] * (g["i"] * g["xr"])

        def step(k, h):
            h = a_s[pl.ds(k, 1), :] * h + b_s[pl.ds(k, 1), :]
            h_ref[pl.ds(k, 1), :] = h
            return h

        hc[...] = lax.fori_loop(0, t, step, hc[...], unroll=8)
        m = h_ref[...] * _gelu(yl_ref[...])
        out_ref[...] = m * lax.rsqrt(jnp.mean(m * m, axis=-1, keepdims=True) + EPS) * lnw_ref[...]

    return _call(body, "lru_fwd", (s // t,),
                 [xl, yl, hx, _whole(cw), _vec(LRU_W), _whole(wa), _whole(wx)] + [_vec(LRU_W)] * 4,
                 [_tok(t, LRU_W), _tok(t, LRU_W)],
                 [_sds((s, LRU_W)), _sds((s, LRU_W))],
                 scratch=[pltpu.VMEM((t, LRU_W), F32), pltpu.VMEM((t, LRU_W), F32), pltpu.VMEM((1, LRU_W), F32)])(
                     proj, proj, proj, cw, cb, wa, wx, gab, gxb, lam, lnw)


def _lru_bwd(dout, proj, hs, cw, cb, wa, wx, gab, gxb, lam, lnw):
    s = proj.shape[0]
    t = _tile(s, 256)
    nt = s // t
    rev = lambda i: nt - 1 - i
    xl, yl, hx = _lru_specs(s, t, rev)
    nh = t // HALO
    tk = pl.BlockSpec((t, LRU_W), lambda i: (rev(i), 0))
    hh = pl.BlockSpec((HALO, LRU_W), lambda i: (jnp.maximum(rev(i) * nh - 1, 0), 0))

    def body(do_ref, xl_ref, yl_ref, hx_ref, h_ref, hh_ref, cw_ref, cb_ref, wa_ref, wx_ref, gab_ref, gxb_ref,
             lam_ref, lnw_ref, dp_ref, dwa_ref, dwx_ref, rows_ref, dh_s, dhd_s, carry, dxr_next):
        i = pl.program_id(0)
        first_tile = rev(i) == 0

        @pl.when(i == 0)
        def _():
            carry[...] = jnp.zeros_like(carry)
            dxr_next[...] = jnp.zeros_like(dxr_next)
            dwa_ref[...] = jnp.zeros_like(dwa_ref)
            dwx_ref[...] = jnp.zeros_like(dwx_ref)
            rows_ref[...] = jnp.zeros_like(rows_ref)

        halo = jnp.where(first_tile, 0.0, hx_ref[...])
        xw = jnp.concatenate([halo, xl_ref[...]], axis=0)
        cw_rows = [cw_ref[k:k + 1, :] for k in range(CONV_K)]
        lam_v = lam_ref[...]
        g = _lru_gates(xw, cw_rows, cb_ref[...], wa_ref[...], wx_ref[...], gab_ref[...], gxb_ref[...], lam_v)
        a, r, gi, xr, mult = g["a"], g["r"], g["i"], g["xr"], g["mult"]
        hv = h_ref[...]
        yv = yl_ref[...]
        gl = _gelu(yv)
        m = hv * gl
        rn = lax.rsqrt(jnp.mean(m * m, axis=-1, keepdims=True) + EPS)
        dov = do_ref[...]
        dmn = dov * lnw_ref[...]
        rows_ref[4:5, :] += jnp.sum(dov * m * rn, axis=0, keepdims=True)
        dm = rn * dmn - m * (rn * rn * rn) * jnp.mean(dmn * m, axis=-1, keepdims=True)
        dhd_s[...] = dm * gl
        dy = dm * hv * _gelu_grad(yv)
        dh_s[...] = a

        def step(k, c):
            row = t - 1 - k
            d = dhd_s[pl.ds(row, 1), :] + c
            c = dh_s[pl.ds(row, 1), :] * d
            dh_s[pl.ds(row, 1), :] = d
            return c

        carry[...] = lax.fori_loop(0, t, step, carry[...], unroll=8)
        dH = dh_s[...]
        hprev_halo = jnp.where(first_tile, 0.0, hh_ref[...])
        hprev = _drop_halo(_roll_rows(jnp.concatenate([hprev_halo, hv], axis=0), 1))
        da = dH * hprev
        dmult = dH * gi * xr
        di = dH * mult * xr
        dxr = dH * mult * gi
        dla = jnp.where(g["mm_raw"] > 1e-12, dmult * (0.5 / mult) * (-2.0 * a * a), 0.0) + da * a
        dr = dla * (LRU_C * g["ls"])
        sig_neg = jax.nn.sigmoid(-lam_v)
        rows_ref[3:4, :] += jnp.sum(dla * (LRU_C * r), axis=0, keepdims=True) * sig_neg
        drp = dr * r * (1.0 - r)
        dip = di * gi * (1.0 - gi)
        rows_ref[1:2, :] += jnp.sum(drp, axis=0, keepdims=True)
        rows_ref[2:3, :] += jnp.sum(dip, axis=0, keepdims=True)
        xb = xr.astype(BF16)
        drb = drp.astype(BF16)
        dib = dip.astype(BF16)
        dwa_ref[...] += _bdot(xb, drb, TN)
        dwx_ref[...] += _bdot(xb, dib, TN)
        dxr = dxr + _bdot(drb, wa_ref[...], NT) + _bdot(dib, wx_ref[...], NT)
        rows_ref[0:1, :] += jnp.sum(dxr, axis=0, keepdims=True)
        ext = jnp.concatenate([dxr, dxr_next[...]], axis=0)
        dx = cw_rows[CONV_K - 1] * dxr
        for k in range(CONV_K - 1):
            dx = dx + cw_rows[k] * _roll_rows(ext, -(CONV_K - 1 - k))[0:t]
        for k in range(CONV_K):
            rows_ref[8 + k:9 + k, :] += jnp.sum(dxr * g["taps"][k], axis=0, keepdims=True)
        dxr_next[...] = dxr[0:HALO]
        dp_ref[...] = jnp.concatenate([dx, dy], axis=1).astype(BF16)

    acc = lambda shape: pl.BlockSpec(shape, lambda i: (0, 0))
    return _call(body, "lru_bwd", (nt,),
                 [tk, xl, yl, hx, tk, hh, _whole(cw), _vec(LRU_W), _whole(wa), _whole(wx)] + [_vec(LRU_W)] * 4,
                 [pl.BlockSpec((t, 2 * LRU_W), lambda i: (rev(i), 0)), acc((LRU_W, LRU_W)), acc((LRU_W, LRU_W)),
                  acc((16, LRU_W))],
                 [_sds((s, 2 * LRU_W), BF16), _sds((LRU_W, LRU_W)), _sds((LRU_W, LRU_W)), _sds((16, LRU_W))],
                 scratch=[pltpu.VMEM((t, LRU_W), F32), pltpu.VMEM((t, LRU_W), F32), pltpu.VMEM((1, LRU_W), F32),
                          pltpu.VMEM((HALO, LRU_W), F32)])(
                     dout, proj, proj, proj, hs, hs, cw, cb, wa, wx, gab, gxb, lam, lnw)


def _gdn_masks():
    row = lax.broadcasted_iota(jnp.int32, (STACK, STACK), 0)
    col = lax.broadcasted_iota(jnp.int32, (STACK, STACK), 1)
    same = (row // CHUNK) == (col // CHUNK)
    return jnp.stack([(same & (col <= row)).astype(F32), (same & (col < row)).astype(F32), (row == col).astype(F32)])


def _conv_silu(xw, rows):
    taps = _conv_taps(xw)
    y = rows[0] * taps[0] + rows[1] * taps[1] + rows[2] * taps[2] + rows[3] * taps[3]
    return y * jax.nn.sigmoid(y)


def _split3(v):
    hi = v.astype(BF16)
    r1 = v - hi.astype(F32)
    mid = r1.astype(BF16)
    return hi, mid, (r1 - mid.astype(F32)).astype(BF16)


def _mask_dot_raw(mask, v, dims):
    parts = _split3(v)
    d = lambda p: lax.dot_general(mask, p, dims, preferred_element_type=F32)
    return d(parts[0]) + (d(parts[1]) + d(parts[2]))


@jax.custom_vjp
def _mask_dot(mask, v):
    return _mask_dot_raw(mask, v, NN)


def _mask_dot_fwd(mask, v):
    return _mask_dot_raw(mask, v, NN), mask


def _mask_dot_bwd(mask, ct):
    return jnp.zeros_like(mask), _mask_dot_raw(mask, ct, TN)


_mask_dot.defvjp(_mask_dot_fwd, _mask_dot_bwd)


def _unit_lower_inverse(ns, eye):
    tinvs = [eye + n for n in ns]
    ps = list(ns)
    for _ in range(5):
        ps = [_bdot(p, p) for p in ps]
        tinvs = [t + _bdot(t, p) for t, p in zip(tinvs, ps)]
    return tuple(t.astype(BF16) for t in tinvs)


def _refined(ns, rhss, tinvs, dims):
    x0s = [_bdot(t, r, dims) for t, r in zip(tinvs, rhss)]
    ress = [r - x0 + _sdot(n, x0, dims) for n, r, x0 in zip(ns, rhss, x0s)]
    return tuple(x0 + _bdot(t, res, dims) for t, x0, res in zip(tinvs, x0s, ress))


@jax.custom_vjp
def _unit_lower_solve(ns, rhss, tinvs):
    return _refined(ns, rhss, tinvs, NN)


def _unit_lower_solve_fwd(ns, rhss, tinvs):
    xs = _unit_lower_solve(ns, rhss, tinvs)
    return xs, (ns, tinvs, xs)


def _unit_lower_solve_bwd(res, cts):
    ns, tinvs, xs = res
    ys = _refined(ns, cts, tinvs, TN)
    return (tuple(_bdot(y, x, NT) for y, x in zip(ys, xs)), ys, tuple(jnp.zeros_like(t) for t in tinvs))


_unit_lower_solve.defvjp(_unit_lower_solve_fwd, _unit_lower_solve_bwd)


def _gdn_prep(xqs, xks, xvs, bas, cwq, cwk, cwv, pa, pd, masks, tinvs=None, with_inverse=False):
    lower, strict, eye = masks[0], masks[1], masks[2]
    lower_b = lower.astype(BF16)
    lane = lax.broadcasted_iota(jnp.int32, (CHUNK, LANES), 1)
    each = lambda f, *lists: [f(*vals) for vals in zip(*lists)]
    stack = lambda xw, rows: jnp.concatenate(_split(_conv_silu(xw, rows), HEADS, 1), axis=0)
    qs, ks, vs = (each(lambda xw: stack(xw, cw), xs) for xs, cw in ((xqs, cwq), (xks, cwk), (xvs, cwv)))
    qns = each(lambda q: q * lax.rsqrt(jnp.sum(q * q, axis=-1, keepdims=True) + 1e-6) * (HEAD_DIM ** -0.5), qs)
    kns = each(lambda k: k * lax.rsqrt(jnp.sum(k * k, axis=-1, keepdims=True) + 1e-6), ks)

    def col(a, j):
        return jnp.broadcast_to(jnp.sum(jnp.where(lane == j, a, 0.0), axis=1, keepdims=True), (CHUNK, HEAD_DIM))

    betas = each(lambda ba: jnp.concatenate([col(jax.nn.sigmoid(ba), h) for h in range(HEADS)], axis=0), bas)
    g_heads = each(lambda ba: [col(-jnp.exp(pa) * _softplus(ba + pd), HEADS + h) for h in range(HEADS)], bas)
    gs = each(lambda gh: jnp.concatenate(gh, axis=0), g_heads)
    gls = each(lambda gh: jnp.concatenate([jnp.broadcast_to(jnp.sum(g, axis=0, keepdims=True), (CHUNK, HEAD_DIM))
                                           for g in gh], axis=0), g_heads)
    gcs = each(lambda g: _mask_dot(lower_b, g), gs)

    def decay_of(gc):
        gc_rows = jnp.transpose(gc)
        return jnp.exp((jnp.concatenate([gc, gc], axis=1) - jnp.concatenate([gc_rows, gc_rows], axis=0)) * lower)

    decays = each(decay_of, gcs)
    egcs = each(jnp.exp, gcs)
    kbs = each(lambda kn, beta: kn * beta, kns, betas)
    ns = tuple(each(lambda kb, kn, decay: -(_bdot(kb, kn, NT) * decay * strict), kbs, kns, decays))
    if tinvs is None:
        tinvs = _unit_lower_inverse([lax.stop_gradient(n) for n in ns], eye)
    rhss = tuple(each(lambda v, beta, kb, egc: jnp.concatenate([v * beta, kb * egc], axis=1), vs, betas, kbs, egcs))
    sols = _unit_lower_solve(ns, rhss, tuple(tinvs))
    attns = each(lambda qn, kn, decay: _bdot(qn, kn, NT) * decay * lower, qns, kns, decays)
    outs = []
    for sol, qn, kn, egc, gl, gc, attn, tinv in zip(sols, qns, kns, egcs, gls, gcs, attns, tinvs):
        u, w = _split(sol, 2, 1)
        out = (u, w, qn * egc, kn * jnp.exp(gl - gc), attn, jnp.exp(gl))
        outs.append(out + (tinv,) if with_inverse else out)
    return outs


def _gdn_scan(states, u, w, qd, kt, attn, egl, z, nw):
    us, ws, qds, kts, egls = (_split(a, HEADS, 0) for a in (u, w, qd, kt, egl))
    vn = [us[h] - _bdot(ws[h], states[h]) for h in range(HEADS)]
    o = jnp.concatenate([_bdot(qds[h], states[h]) for h in range(HEADS)], axis=0)
    o = o + _bdot(attn, jnp.concatenate(vn, axis=0))
    new = [states[h] * jnp.concatenate([egls[h], egls[h]], axis=0) + _bdot(kts[h], vn[h], TN) for h in range(HEADS)]
    on = o * lax.rsqrt(jnp.mean(o * o, axis=-1, keepdims=True) + EPS) * nw
    return new, on * (z * jax.nn.sigmoid(z))


def _gdn_in_specs(step_of, chunks):
    nh = chunks * CHUNK // HALO
    main = [pl.BlockSpec((chunks * CHUNK, GDN_W), functools.partial(lambda col, i: (step_of(i), col), col))
            for col in (2, 3, 4)]
    halo = [pl.BlockSpec((HALO, GDN_W), functools.partial(lambda col, i: (jnp.maximum(step_of(i) * nh - 1, 0), col),
                                                         col)) for col in (2, 3, 4)]
    return main, halo


def _stk(width, step_of, chunks=1):
    return pl.BlockSpec((chunks * STACK, width), lambda i: (step_of(i), 0))


def _chunk_inputs(main_refs, halo_refs, k, first_step):
    rows = slice(k * CHUNK, (k + 1) * CHUNK)
    if k == 0:
        halos = [jnp.where(first_step, 0.0, h[...]) for h in halo_refs]
    else:
        halos = [m[k * CHUNK - HALO:k * CHUNK, :] for m in main_refs]
    return [jnp.concatenate([h, m[rows, :]], axis=0) for h, m in zip(halos, main_refs)]


def _gdn_prep_fwd(proj, ba, cw, pa, pd, masks):
    s = proj.shape[0]
    nc = s // CHUNK
    per = min(PREP_CHUNKS, nc)
    main, halo = _gdn_in_specs(lambda i: i, per)

    def body(xq_ref, xk_ref, xv_ref, hq_ref, hk_ref, hv_ref, ba_ref, cw_ref, pa_ref, pd_ref, mk_ref, *out_refs):
        first_step = pl.program_id(0) == 0
        rows = [[cw_ref[k:k + 1, j * GDN_W:(j + 1) * GDN_W] for k in range(CONV_K)] for j in range(3)]
        cst = [mk_ref[0], mk_ref[1], mk_ref[2]]
        xs = [_chunk_inputs((xq_ref, xk_ref, xv_ref), (hq_ref, hk_ref, hv_ref), k, first_step) for k in range(per)]
        outs = _gdn_prep([x[0] for x in xs], [x[1] for x in xs], [x[2] for x in xs],
                         [ba_ref[k * CHUNK:(k + 1) * CHUNK, :] for k in range(per)], rows[0], rows[1], rows[2],
                         pa_ref[...], pd_ref[...], cst, with_inverse=True)
        for k, out in enumerate(outs):
            for ref, val in zip(out_refs, out):
                ref[k * STACK:(k + 1) * STACK, :] = val.astype(ref.dtype)

    ident = lambda i: i
    stacked = lambda dt: _sds((nc * STACK, HEAD_DIM), dt)
    wide, thin = _stk(STACK, ident, per), _stk(HEAD_DIM, ident, per)
    return _call(body, "gdn_prep_fwd", (nc // per,),
                 main + halo + [_tok(per * CHUNK, BA_PAD), _whole(cw), _vec(BA_PAD), _vec(BA_PAD), _whole(masks)],
                 [thin] * 4 + [wide, thin, wide],
                 [stacked(F32), stacked(BF16), stacked(BF16), stacked(BF16), _sds((nc * STACK, STACK), BF16),
                  stacked(F32), _sds((nc * STACK, STACK), BF16)])(
                     proj, proj, proj, proj, proj, proj, ba, cw, pa, pd, masks)


def _gdn_prep_bwd(cts, tinv, proj, ba, cw, pa, pd, masks):
    s = proj.shape[0]
    nc = s // CHUNK
    per = min(PREP_CHUNKS, nc)
    steps = nc // per
    rev = lambda i: steps - 1 - i
    main, halo = _gdn_in_specs(rev, per)

    def body(du_ref, dw_ref, dqd_ref, dkt_ref, dattn_ref, degl_ref, tinv_ref, xq_ref, xk_ref, xv_ref, hq_ref, hk_ref,
             hv_ref, ba_ref, cw_ref, pa_ref, pd_ref, mk_ref, dp_ref, dba_ref, dcw_ref, dpar_ref, carry):
        i = pl.program_id(0)
        first_step = rev(i) == 0

        @pl.when(i == 0)
        def _():
            carry[...] = jnp.zeros_like(carry)
            dcw_ref[...] = jnp.zeros_like(dcw_ref)
            dpar_ref[...] = jnp.zeros_like(dpar_ref)

        rows = [[cw_ref[k:k + 1, j * GDN_W:(j + 1) * GDN_W] for k in range(CONV_K)] for j in range(3)]
        cst = [mk_ref[0], mk_ref[1], mk_ref[2]]
        xs = [_chunk_inputs((xq_ref, xk_ref, xv_ref), (hq_ref, hk_ref, hv_ref), k, first_step) for k in range(per)]
        stks = [slice(k * STACK, (k + 1) * STACK) for k in range(per)]
        tinvs = [tinv_ref[stk, :] for stk in stks]
        fn = lambda xqs, xks, xvs, bs, rq, rk, rv, a, d: _gdn_prep(xqs, xks, xvs, bs, rq, rk, rv, a, d, cst, tinvs=tinvs)
        _, vjp = jax.vjp(fn, [x[0] for x in xs], [x[1] for x in xs], [x[2] for x in xs],
                         [ba_ref[k * CHUNK:(k + 1) * CHUNK, :] for k in range(per)], rows[0], rows[1], rows[2],
                         pa_ref[...], pd_ref[...])
        dxqs, dxks, dxvs, dbas, drq, drk, drv, dpa, dpd = vjp(
            [tuple(ref[stk, :] for ref in (du_ref, dw_ref, dqd_ref, dkt_ref, dattn_ref, degl_ref)) for stk in stks])
        dxws = [jnp.concatenate(parts, axis=1) for parts in zip(dxqs, dxks, dxvs)]
        for k in range(per):
            dba_ref[k * CHUNK:(k + 1) * CHUNK, :] = dbas[k].astype(BF16)
        for j, dr in enumerate((drq, drk, drv)):
            for kk in range(CONV_K):
                dcw_ref[kk:kk + 1, j * GDN_W:(j + 1) * GDN_W] += dr[kk]
        dpar_ref[0:1, :] += dpa
        dpar_ref[1:2, :] += dpd
        pad = jnp.zeros((CHUNK - HALO, 3 * GDN_W), F32)
        for k in range(per):
            late = carry[...] if k == per - 1 else dxws[k + 1][0:HALO]
            dp_ref[k * CHUNK:(k + 1) * CHUNK, :] = (dxws[k][HALO:] + jnp.concatenate([pad, late], axis=0)).astype(BF16)
        carry[...] = dxws[0][0:HALO]

    acc = lambda shape: pl.BlockSpec(shape, lambda i: (0, 0))
    wide, thin = _stk(STACK, rev, per), _stk(HEAD_DIM, rev, per)
    return _call(body, "gdn_prep_bwd", (steps,),
                 [thin] * 4 + [wide, thin, wide] + main + halo
                 + [pl.BlockSpec((per * CHUNK, BA_PAD), lambda i: (rev(i), 0)), _whole(cw), _vec(BA_PAD), _vec(BA_PAD),
                    _whole(masks)],
                 [pl.BlockSpec((per * CHUNK, 3 * GDN_W), lambda i: (rev(i), 0)),
                  pl.BlockSpec((per * CHUNK, BA_PAD), lambda i: (rev(i), 0)), acc((CONV_K, 3 * GDN_W)),
                  acc((8, BA_PAD))],
                 [_sds((s, 3 * GDN_W), BF16), _sds((s, BA_PAD), BF16), _sds((CONV_K, 3 * GDN_W)), _sds((8, BA_PAD))],
                 scratch=[pltpu.VMEM((HALO, 3 * GDN_W), F32)])(
                     *cts, tinv, proj, proj, proj, proj, proj, proj, ba, cw, pa, pd, masks)


def _stack_heads(v):
    return jnp.concatenate(_split(v, HEADS, 1), axis=0)


def _unstack_heads(v):
    return jnp.concatenate(_split(v, HEADS, 0), axis=1)


def _gdn_scan_fwd(prep, proj, nw):
    s = proj.shape[0]
    nc = s // CHUNK
    per = min(SCAN_CHUNKS, nc)
    ident = lambda i: i
    srows = HEADS * HEAD_DIM

    def body(u_ref, w_ref, qd_ref, kt_ref, attn_ref, egl_ref, z_ref, nw_ref, out_ref, st_ref, state):
        @pl.when(pl.program_id(0) == 0)
        def _():
            state[...] = jnp.zeros_like(state)

        states = [state[h * HEAD_DIM:(h + 1) * HEAD_DIM, :] for h in range(HEADS)]
        for k in range(per):
            stk, tok = slice(k * STACK, (k + 1) * STACK), slice(k * CHUNK, (k + 1) * CHUNK)
            for h in range(HEADS):
                st_ref[k * srows + h * HEAD_DIM:k * srows + (h + 1) * HEAD_DIM, :] = states[h]
            states, out = _gdn_scan(states, u_ref[stk, :], w_ref[stk, :], qd_ref[stk, :], kt_ref[stk, :],
                                    attn_ref[stk, :], egl_ref[stk, :], _stack_heads(z_ref[tok, :]), nw_ref[...])
            out_ref[tok, :] = _unstack_heads(out)
        for h in range(HEADS):
            state[h * HEAD_DIM:(h + 1) * HEAD_DIM, :] = states[h]

    thin, wide = _stk(HEAD_DIM, ident, per), _stk(STACK, ident, per)
    return _call(body, "gdn_scan_fwd", (nc // per,),
                 [thin] * 4 + [wide, thin, _tok(per * CHUNK, GDN_W, col=5), _vec(HEAD_DIM)],
                 [_tok(per * CHUNK, GDN_W), pl.BlockSpec((per * srows, HEAD_DIM), lambda i: (i, 0))],
                 [_sds((s, GDN_W)), _sds((nc * srows, HEAD_DIM))],
                 scratch=[pltpu.VMEM((srows, HEAD_DIM), F32)])(*prep, proj, nw)


def _gdn_scan_bwd(dout, prep, st, proj, nw):
    s = proj.shape[0]
    nc = s // CHUNK
    per = min(SCAN_CHUNKS, nc)
    steps = nc // per
    rev = lambda i: steps - 1 - i
    srows = HEADS * HEAD_DIM

    def body(do_ref, u_ref, w_ref, qd_ref, kt_ref, attn_ref, egl_ref, st_ref, z_ref, nw_ref,
             du_ref, dw_ref, dqd_ref, dkt_ref, dattn_ref, degl_ref, dz_ref, dnw_ref, dstate):
        @pl.when(pl.program_id(0) == 0)
        def _():
            dstate[...] = jnp.zeros_like(dstate)
            dnw_ref[...] = jnp.zeros_like(dnw_ref)

        dnew = [dstate[h * HEAD_DIM:(h + 1) * HEAD_DIM, :] for h in range(HEADS)]
        for k in reversed(range(per)):
            stk, tok = slice(k * STACK, (k + 1) * STACK), slice(k * CHUNK, (k + 1) * CHUNK)
            states = [st_ref[k * srows + h * HEAD_DIM:k * srows + (h + 1) * HEAD_DIM, :] for h in range(HEADS)]
            f32 = lambda ref: ref[stk, :].astype(F32)
            _, vjp = jax.vjp(_gdn_scan, states, u_ref[stk, :], f32(w_ref), f32(qd_ref), f32(kt_ref), f32(attn_ref),
                             egl_ref[stk, :], _stack_heads(z_ref[tok, :]), nw_ref[...])
            dnew, du, dw, dqd, dkt, dattn, degl, dz, dnw = vjp((dnew, _stack_heads(do_ref[tok, :])))
            for ref, val in zip((du_ref, dw_ref, dqd_ref, dkt_ref, dattn_ref, degl_ref),
                                (du, dw, dqd, dkt, dattn, degl)):
                ref[stk, :] = val
            dz_ref[tok, :] = _unstack_heads(dz).astype(BF16)
            dnw_ref[0:1, :] += dnw
        for h in range(HEADS):
            dstate[h * HEAD_DIM:(h + 1) * HEAD_DIM, :] = dnew[h]

    tokr = lambda n, col=0: pl.BlockSpec((per * CHUNK, n), lambda i: (rev(i), col))
    thin, wide = _stk(HEAD_DIM, rev, per), _stk(STACK, rev, per)
    return _call(body, "gdn_scan_bwd", (steps,),
                 [tokr(GDN_W)] + [thin] * 4 + [wide, thin, pl.BlockSpec((per * srows, HEAD_DIM), lambda i: (rev(i), 0)),
                                               tokr(GDN_W, 5), _vec(HEAD_DIM)],
                 [thin] * 4 + [wide, thin, tokr(GDN_W), pl.BlockSpec((8, HEAD_DIM), lambda i: (0, 0))],
                 [_sds((nc * STACK, HEAD_DIM))] * 4 + [_sds((nc * STACK, STACK)), _sds((nc * STACK, HEAD_DIM)),
                                                       _sds((s, GDN_W), BF16), _sds((8, HEAD_DIM))],
                 scratch=[pltpu.VMEM((srows, HEAD_DIM), F32)])(dout, *prep, st, proj, nw)


def _wo_specs(l):
    half = N_DEV // 2
    return [pl.BlockSpec((half, 1, D_MODEL // N_DEV, D_MODEL), functools.partial(lambda k, *_: (k, l, 0, 0), k))
            for k in range(2)]


def _wo_half(ref):
    return ref[:, 0].reshape(ref.shape[0] * ref.shape[2], ref.shape[3])


def _out_mlp_fwd(ol, og, x, wo, g1, nw2, sc2, sh2, g2, wup, wdn, l):
    s = x.shape[0]
    t = _tile(s, 512)
    nj = wup.shape[0] // MLP_BLOCKS
    fc = wup.shape[3]

    def body(ol_ref, og_ref, x_ref, wol_ref, wog_ref, g1_ref, nw_ref, sc_ref, sh_ref, g2_ref, wup_ref, wdn_ref,
             x1_ref, mix_ref, ff_ref, x2_ref, h2_s, acc_s):
        j = pl.program_id(1)

        @pl.when(j == 0)
        def _():
            mix = _bdot(ol_ref[...], _wo_half(wol_ref)) + _bdot(og_ref[...], _wo_half(wog_ref))
            x1 = x_ref[...] + g1_ref[...] * mix
            mix_ref[...] = mix.astype(BF16)
            x1_ref[...] = x1
            h2, _, _ = _modulated_norm(x1, nw_ref[...], sc_ref[...], sh_ref[...])
            h2_s[...] = h2.astype(BF16)
            acc_s[...] = jnp.zeros_like(acc_s)

        part = None
        for b in range(MLP_BLOCKS):
            up = _bdot(h2_s[...], wup_ref[b, 0])
            down = _bdot(jnp.square(jnp.maximum(up, 0.0)), wdn_ref[b, 0])
            part = down if part is None else part + down
        acc_s[...] += part

        @pl.when(j == nj - 1)
        def _():
            ff_ref[...] = acc_s[...].astype(BF16)
            x2_ref[...] = x1_ref[...] + g2_ref[...] * acc_s[...]

    tk = lambda n: pl.BlockSpec((t, n), lambda i, j: (i, 0))
    return _call(body, "out_mlp_fwd", (s // t, nj),
                 [tk(LRU_W), tk(GDN_W), tk(D_MODEL)] + _wo_specs(l) + [_vec(D_MODEL)] * 5
                 + [pl.BlockSpec((MLP_BLOCKS, 1, D_MODEL, fc), lambda i, j: (j, l, 0, 0)),
                    pl.BlockSpec((MLP_BLOCKS, 1, fc, D_MODEL), lambda i, j: (j, l, 0, 0))],
                 [tk(D_MODEL)] * 4,
                 [_sds((s, D_MODEL)), _sds((s, D_MODEL), BF16), _sds((s, D_MODEL), BF16), _sds((s, D_MODEL))],
                 scratch=[pltpu.VMEM((t, D_MODEL), BF16), pltpu.VMEM((t, D_MODEL), F32)])(
                     ol, og, x, wo, wo, g1, nw2, sc2, sh2, g2, wup, wdn)


def _mlp_bwd(dx2, x1, ff, nw2, sc2, sh2, g2, wup, wdn, l):
    s = x1.shape[0]
    t = _tile(s, 512)
    nj = wup.shape[0] // MLP_BLOCKS
    fc = wup.shape[3]

    def body(dx2_ref, x1_ref, ff_ref, nw_ref, sc_ref, sh_ref, g2_ref, wup_ref, wdn_ref,
             act_ref, dup_ref, h2_ref, dff_ref, dx1_ref, rows_ref, dh2_s):
        i, j = pl.program_id(0), pl.program_id(1)

        @pl.when((i == 0) & (j == 0))
        def _():
            rows_ref[...] = jnp.zeros_like(rows_ref)

        @pl.when(j == 0)
        def _():
            h2, _, _ = _modulated_norm(x1_ref[...], nw_ref[...], sc_ref[...], sh_ref[...])
            h2_ref[...] = h2.astype(BF16)
            dx2 = dx2_ref[...]
            dff_ref[...] = (dx2 * g2_ref[...]).astype(BF16)
            rows_ref[2:3, :] += jnp.sum(dx2 * ff_ref[...].astype(F32), axis=0, keepdims=True)
            dh2_s[...] = jnp.zeros_like(dh2_s)

        part = None
        for b in range(MLP_BLOCKS):
            cols = slice(b * fc, (b + 1) * fc)
            up = _bdot(h2_ref[...], wup_ref[b, 0])
            ru = jnp.maximum(up, 0.0)
            act_ref[:, cols] = (ru * ru).astype(BF16)
            dup = (_bdot(dff_ref[...], wdn_ref[b, 0], NT) * (2.0 * ru)).astype(BF16)
            dup_ref[:, cols] = dup
            back = _bdot(dup, wup_ref[b, 0], NT)
            part = back if part is None else part + back
        dh2_s[...] += part

        @pl.when(j == nj - 1)
        def _():
            xv = x1_ref[...]
            _, n, r = _modulated_norm(xv, nw_ref[...], sc_ref[...], sh_ref[...])
            dx, dsh, dsc, dnw = _modulated_norm_bwd(dh2_s[...], xv, n, r, nw_ref[...], sc_ref[...])
            dx1_ref[...] = dx2_ref[...] + dx
            rows_ref[0:1, :] += dsh
            rows_ref[1:2, :] += dsc
            rows_ref[3:4, :] += dnw

    tk = lambda n: pl.BlockSpec((t, n), lambda i, j: (i, 0))
    tj = pl.BlockSpec((t, MLP_BLOCKS * fc), lambda i, j: (i, j))
    return _call(body, "mlp_bwd", (s // t, nj),
                 [tk(D_MODEL)] * 3 + [_vec(D_MODEL)] * 4
                 + [pl.BlockSpec((MLP_BLOCKS, 1, D_MODEL, fc), lambda i, j: (j, l, 0, 0)),
                    pl.BlockSpec((MLP_BLOCKS, 1, fc, D_MODEL), lambda i, j: (j, l, 0, 0))],
                 [tj, tj, tk(D_MODEL), tk(D_MODEL), tk(D_MODEL), pl.BlockSpec((8, D_MODEL), lambda i, j: (0, 0))],
                 [_sds((s, D_FF), BF16), _sds((s, D_FF), BF16), _sds((s, D_MODEL), BF16),
                  _sds((s, D_MODEL), BF16), _sds((s, D_MODEL)), _sds((8, D_MODEL))],
                 scratch=[pltpu.VMEM((t, D_MODEL), F32)], vmem_mb=56)(dx2, x1, ff, nw2, sc2, sh2, g2, wup, wdn)


def _outproj_bwd(dx1, mix, g1, wo, l):
    s = dx1.shape[0]
    t = _tile(s, 512)

    def body(dx1_ref, mix_ref, g1_ref, wol_ref, wog_ref, dmix_ref, dol_ref, dog_ref, rows_ref):
        @pl.when(pl.program_id(0) == 0)
        def _():
            rows_ref[...] = jnp.zeros_like(rows_ref)

        dx1v = dx1_ref[...]
        rows_ref[0:1, :] += jnp.sum(dx1v * mix_ref[...].astype(F32), axis=0, keepdims=True)
        dmix = (dx1v * g1_ref[...]).astype(BF16)
        dmix_ref[...] = dmix
        dol_ref[...] = _bdot(dmix, _wo_half(wol_ref), NT)
        dog_ref[...] = _bdot(dmix, _wo_half(wog_ref), NT)

    return _call(body, "outproj_bwd", (s // t,),
                 [_tok(t, D_MODEL), _tok(t, D_MODEL), _vec(D_MODEL)] + _wo_specs(l),
                 [_tok(t, D_MODEL), _tok(t, LRU_W), _tok(t, GDN_W), pl.BlockSpec((8, D_MODEL), lambda i: (0, 0))],
                 [_sds((s, D_MODEL), BF16), _sds((s, LRU_W)), _sds((s, GDN_W)), _sds((8, D_MODEL))])(dx1, mix, g1, wo, wo)


def _tn_matmul(a, b, name, out=None, l=0, blocked=False, row_block=0):
    s, m = a.shape
    n = b.shape[1]
    ts, bm = _tile(s, 2048), _tile(m, 1024)
    bn = next(w for w in ((512,) if blocked else (1024, 512, 640, 384, 256, 128)) if n % w == 0)

    def body(a_ref, b_ref, *rest):
        o_ref = rest[-1]

        @pl.when(pl.program_id(2) == 0)
        def _():
            o_ref[...] = jnp.zeros_like(o_ref)

        acc = _bdot(a_ref[...], b_ref[...], TN)
        o_ref[...] += acc.reshape(o_ref.shape)

    in_specs = [pl.BlockSpec((ts, bm), lambda i, j, k: (k, i)), pl.BlockSpec((ts, bn), lambda i, j, k: (k, j))]
    grid = (m // bm, n // bn, s // ts)
    if out is None:
        return _call(body, name, grid, in_specs, pl.BlockSpec((bm, bn), lambda i, j, k: (i, j)), _sds((m, n)))(a, b)
    if blocked:
        out_spec = pl.BlockSpec((1, 1, bm, bn), lambda i, j, k: (l, j, i, 0))
    else:
        out_spec = pl.BlockSpec((1, bm, bn), lambda i, j, k: (l, i + row_block * (m // bm), j))
    return _call(body, name, grid, in_specs + [pl.BlockSpec(memory_space=pl.ANY)], out_spec,
                 _sds(out.shape), aliases={2: 0})(a, b, out)


def _final_fwd_bwd(x, target, fw):
    s = x.shape[0]
    t = _tile(s, 512)

    def body(x_ref, tg_ref, fw_ref, dx_ref, rows_ref):
        @pl.when(pl.program_id(0) == 0)
        def _():
            rows_ref[...] = jnp.zeros_like(rows_ref)

        xv = x_ref[...]
        fwv = fw_ref[...]
        r = lax.rsqrt(jnp.mean(xv * xv, axis=-1, keepdims=True) + EPS)
        err = xv * r * fwv - tg_ref[...]
        part = 0.5 * jnp.sum(jnp.mean(err * err, axis=-1, keepdims=True), axis=0, keepdims=True)
        rows_ref[1:2, :] += jnp.broadcast_to(part, (1, D_MODEL))
        dy = err * (1.0 / D_MODEL)
        rows_ref[0:1, :] += jnp.sum(dy * xv * r, axis=0, keepdims=True)
        dxn = dy * fwv
        dx_ref[...] = r * dxn - xv * (r * r * r) * jnp.mean(dxn * xv, axis=-1, keepdims=True)

    return _call(body, "final_fwd_bwd", (s // t,),
                 [_tok(t, D_MODEL), _tok(t, D_MODEL), _vec(D_MODEL)],
                 [_tok(t, D_MODEL), pl.BlockSpec((8, D_MODEL), lambda i: (0, 0))],
                 [_sds((s, D_MODEL)), _sds((8, D_MODEL))])(x, target, fw)


def _adamw(w, g, m, v):
    m = ADAM_B1 * m + (1.0 - ADAM_B1) * g
    v = ADAM_B2 * v + (1.0 - ADAM_B2) * (g * g)
    m_hat = m / (1.0 - ADAM_B1 ** ADAM_STEP)
    v_hat = v / (1.0 - ADAM_B2 ** ADAM_STEP)
    return -ADAM_LR * (m_hat / (jnp.sqrt(v_hat) + ADAM_EPS) + ADAM_WD * w), m, v


def _mod_local(c_all, wmod, bmod_cols):
    nl, _, cols = wmod.shape

    def body(c_ref, w_ref, b_ref, o_ref):
        cv = c_ref[...]
        o_ref[0] = _bdot(cv * jax.nn.sigmoid(cv), w_ref[0]) + b_ref[0]

    return _call(body, "mod_local", (nl,),
                 [_whole(c_all), pl.BlockSpec((1, D_MODEL, cols), lambda l: (l, 0, 0)),
                  pl.BlockSpec((1, 1, cols), lambda l: (l, 0, 0))],
                 pl.BlockSpec((1, N_DEV, cols), lambda l: (l, 0, 0)), _sds((nl, N_DEV, cols)))(c_all, wmod, bmod_cols)


def _wmod_update(c_all, dmod_cols, w, m, v):
    nl, _, cols = w.shape

    def body(c_ref, d_ref, w_ref, m_ref, v_ref, g_ref, dl_ref, nm_ref, nv_ref):
        cv = c_ref[...]
        g = _bdot(cv * jax.nn.sigmoid(cv), d_ref[0], TN)
        g_ref[0] = g
        dl_ref[0], nm_ref[0], nv_ref[0] = _adamw(w_ref[0], g, m_ref[0], v_ref[0])

    wspec = pl.BlockSpec((1, D_MODEL, cols), lambda l: (l, 0, 0))
    return _call(body, "wmod_update", (nl,),
                 [_whole(c_all), pl.BlockSpec((1, N_DEV, cols), lambda l: (l, 0, 0)), wspec, wspec, wspec],
                 [wspec] * 4, [_sds(w.shape)] * 4)(c_all, dmod_cols, w, m, v)


def _sum_devices(gathered):
    _, r, _ = gathered.shape

    def body(g_ref, o_ref):
        acc = g_ref[0]
        for d in range(1, N_DEV):
            acc = acc + g_ref[d]
        o_ref[...] = acc

    return _call(body, "sum_devices", (1,), [_whole(gathered)], pl.BlockSpec((r, LANES), lambda i: (0, 0)),
                 _sds((r, LANES)))(gathered)


def _adam_flat(w, g, m, v):
    r = w.shape[0]

    def body(w_ref, g_ref, m_ref, v_ref, dl_ref, nm_ref, nv_ref):
        dl_ref[...], nm_ref[...], nv_ref[...] = _adamw(w_ref[...], g_ref[...], m_ref[...], v_ref[...])

    spec = pl.BlockSpec((r, LANES), lambda i: (0, 0))
    return _call(body, "adam_small", (1,), [spec] * 4, [spec] * 3, [_sds((r, LANES))] * 3)(w, g, m, v)


def _pair_add(x, p, others):
    _, r, c = x.shape
    tr = _tile(r, 128 if c > 512 else 256)

    def body(others_ref, x_ref, p_ref, o_ref):
        o_ref[...] = (x_ref[...] + p_ref[...]).astype(BF16)

    return _call(body, "pair_add", (3, r // tr),
                 [pl.BlockSpec((1, tr, c), lambda q, i, others_ref: (others_ref[q], i, 0)),
                  pl.BlockSpec((1, tr, c), lambda q, i, others_ref: (others_ref[3 + q], i, 0))],
                 pl.BlockSpec((1, tr, c), lambda q, i, others_ref: (q, i, 0)), _sds((3, r, c), BF16),
                 prefetch=1)(others, x, p)


def _reduce_adam(x, p, q, place, w, m, v, l, outs):
    _, r, c = x.shape
    tr = _tile(r, 128 if c > 512 else 256)

    def body(place_ref, x_ref, p_ref, q_ref, w_ref, m_ref, v_ref, *rest):
        g_ref, dl_ref, nm_ref, nv_ref = rest[-4:]
        g = (((x_ref[0] + p_ref[0]) + q_ref[0].astype(F32)) + q_ref[1].astype(F32)) + q_ref[2].astype(F32)
        g_ref[0] = g
        dl_ref[0], nm_ref[0], nv_ref[0] = _adamw(w_ref[0], g, m_ref[0], v_ref[0])

    flat = pl.BlockSpec((1, tr, c), lambda i, place_ref: (l, i, 0))
    through = pl.BlockSpec(memory_space=pl.ANY)
    return _call(body, "reduce_adam", (r // tr,),
                 [pl.BlockSpec((1, tr, c), lambda i, place_ref: (place_ref[0], i, 0)),
                  pl.BlockSpec((1, tr, c), lambda i, place_ref: (place_ref[1], i, 0)),
                  pl.BlockSpec((3, tr, c), lambda i, place_ref: (0, i, 0)), flat, flat, flat] + [through] * 4,
                 [flat] * 4, [_sds(w.shape)] * 4, prefetch=1, aliases={7 + k: k for k in range(4)})(
                     place, x, p, q, w, m, v, *outs)


def _place():
    return lax.axis_index("x"), lax.axis_index("y"), lax.axis_index("c")


def _all_gather(xs, name, space):
    n = len(xs)

    def body(*refs):
        x_refs, o_refs = refs[:n], refs[n:2 * n]
        send_sems, recv_sems, local_sems = refs[2 * n:]
        x, y, c = _place()
        me, sibling = (x, y, c), (x, y, 1 - c)
        chips = [(1 - x, y), (x, 1 - y), (1 - x, 1 - y)]

        def blk(a, p):
            return o_refs[a].at[4 * p[0] + 2 * p[1] + p[2]]

        def copy(a, k, block, to, src=None):
            return pltpu.make_async_remote_copy(
                src_ref=blk(a, block) if src is None else src, dst_ref=blk(a, block),
                send_sem=send_sems.at[a, k], recv_sem=recv_sems.at[a, k], device_id=to, device_id_type=MESH)

        mine = [pltpu.make_async_copy(x_refs[a], blk(a, me), local_sems.at[a]) for a in range(n)]
        for cp in mine:
            cp.start()
        first = []
        for a in range(n):
            first.append(copy(a, 0, me, sibling, src=x_refs[a]))
            first += [copy(a, 1 + j, me, (*chip, c), src=x_refs[a]) for j, chip in enumerate(chips)]
        for cp in first:
            cp.start()
        passed = []
        for j, chip in enumerate(chips):
            for a in range(n):
                copy(a, 1 + j, (*chip, c), me).wait_recv()
                cp = copy(a, 4 + j, (*chip, c), sibling)
                cp.start()
                passed.append(cp)
        for a in range(n):
            copy(a, 0, sibling, me).wait_recv()
        for j, chip in enumerate(chips):
            for a in range(n):
                copy(a, 4 + j, (*chip, 1 - c), me).wait_recv()
        for cp in first + passed:
            cp.wait_send()
        for cp in mine:
            cp.wait()

    spec = pl.BlockSpec(memory_space=space)
    return pl.pallas_call(
        body, name=name, out_shape=[_sds((N_DEV,) + a.shape, a.dtype) for a in xs],
        in_specs=[spec] * n, out_specs=[spec] * n,
        scratch_shapes=[pltpu.SemaphoreType.DMA((n, 7)), pltpu.SemaphoreType.DMA((n, 7)),
                        pltpu.SemaphoreType.DMA((n,))])(*xs)


_HBM_SPEC = pl.BlockSpec(memory_space=pltpu.HBM)
_SEM_SPEC = pl.BlockSpec(memory_space=pltpu.SEMAPHORE)
_EFFECT = pltpu.SideEffectType.DATAFLOW_SIDE_EFFECTING


def _descriptors(plan, src_refs, land_refs, send_sems, recv_sems, which=None):
    return [pltpu.make_async_remote_copy(src_ref=s, dst_ref=d, send_sem=send_sems.at[k], recv_sem=recv_sems.at[k],
                                         device_id=dev, device_id_type=MESH)
            for k, (s, d, dev) in enumerate(plan(src_refs, land_refs)) if which is None or k in which]


def _split_start(name, plan, n, srcs, lands, after):
    ns, nb = len(srcs), len(srcs) + len(lands)
    after = list(after) if isinstance(after, (list, tuple)) else [after]
    sems = nb + len(after)

    def body(*refs):
        for cp in _descriptors(plan, refs[:ns], refs[ns:nb], refs[sems], refs[sems + 1]):
            cp.start()
        refs[-1][...] = jnp.zeros_like(refs[-1])

    bufs = [pltpu.with_memory_space_constraint(a, pltpu.HBM) for a in list(srcs) + list(lands)]
    outs = pl.pallas_call(
        body, name=name,
        out_shape=(pltpu.SemaphoreType.DMA((n,)), pltpu.SemaphoreType.DMA((n,)))
        + tuple(pltpu.HBM(a.shape, a.dtype) for a in bufs) + (_sds((8, LANES)),),
        in_specs=[_HBM_SPEC] * nb + [pl.BlockSpec(memory_space=pl.ANY)] * len(after),
        out_specs=(_SEM_SPEC, _SEM_SPEC) + (_HBM_SPEC,) * nb + (pl.BlockSpec(memory_space=pltpu.VMEM),),
        input_output_aliases={i: 2 + i for i in range(nb)},
        compiler_params=pltpu.CompilerParams(has_side_effects=_EFFECT))(*bufs, *after)
    return dict(send=outs[0], recv=outs[1], srcs=list(outs[2:2 + ns]), lands=list(outs[2 + ns:2 + nb]), token=outs[-1])


def _split_wait(name, plan, flight, which, after):
    srcs, lands = flight["srcs"], flight["lands"]
    ns, nb = len(srcs), len(srcs) + len(lands)

    def body(*refs):
        for cp in _descriptors(plan, refs[:ns], refs[ns:nb], refs[nb], refs[nb + 1], set(which)):
            cp.wait_send()
            cp.wait_recv()

    outs = pl.pallas_call(
        body, name=name, out_shape=tuple(pltpu.HBM(a.shape, a.dtype) for a in srcs + lands),
        in_specs=[_HBM_SPEC] * nb + [_SEM_SPEC, _SEM_SPEC, pl.BlockSpec(memory_space=pl.ANY)],
        out_specs=(_HBM_SPEC,) * nb, input_output_aliases={i: i for i in range(nb)},
        compiler_params=pltpu.CompilerParams(has_side_effects=_EFFECT))(*srcs, *lands, flight["send"], flight["recv"],
                                                                       after)
    return dict(flight, srcs=list(outs[:ns]), lands=list(outs[ns:nb]))


GATHER_PEERS = N_DEV - 1


def _gather_plan(items):
    def plan(src_refs, land_refs):
        x, y, c = _place()
        me = 4 * x + 2 * y + c
        out = []
        for a, l in items:
            for r in range(1, N_DEV):
                peer = (1 - x if r & 4 else x, 1 - y if r & 2 else y, 1 - c if r & 1 else c)
                out.append((src_refs[a].at[l], land_refs[a].at[me, l], peer))
        return out

    return plan


def _pair_plan(narr):
    def plan(src_refs, land_refs):
        x, y, c = _place()
        return [(src_refs[a].at[2 * q + (1 - c)], land_refs[a].at[q], (x, y, 1 - c))
                for a in range(narr) for q in range(4)]

    return plan


def _chip_plan(narr):
    def plan(src_refs, land_refs):
        x, y, c = _place()
        chips = [(1 - x, y), (x, 1 - y), (1 - x, 1 - y)]
        return [(src_refs[a].at[r], land_refs[a].at[r], (*chip, c)) for a in range(narr) for r, chip in enumerate(chips)]

    return plan


class _GradReducer:
    def __init__(self, tag, names, w, mom, var, place, others):
        self.tag, self.names, self.w, self.mom, self.var, self.place = tag, names, w, mom, var, place
        self.others = others
        self.outs = {k: [lax.empty(w[k].shape, F32) for _ in range(4)] for k in names}
        self.n = len(names)

    def start(self, l, grads):
        self.l, self.xs = l, [grads[k] for k in self.names]
        lands = [lax.empty((4,) + a.shape[1:], F32) for a in self.xs]
        self.pair = _split_start(f"{self.tag}_pair_start{l}", _pair_plan(self.n), 4 * self.n, self.xs, lands, ())
        return self.pair["token"][0, 0]

    def middle(self, after):
        self.pair = _split_wait(f"{self.tag}_pair_wait{self.l}", _pair_plan(self.n), self.pair, range(4 * self.n),
                                after)
        self.xs, self.ps = self.pair["srcs"], self.pair["lands"]
        ys = [_pair_add(x, p, self.others) for x, p in zip(self.xs, self.ps)]
        lands = [lax.empty((3,) + a.shape[1:], BF16) for a in ys]
        self.chip = _split_start(f"{self.tag}_chip_start{self.l}", _chip_plan(self.n), 3 * self.n, ys, lands, ())
        return self.chip["token"][0, 0]

    def finish(self, after):
        chip = _split_wait(f"{self.tag}_chip_wait{self.l}", _chip_plan(self.n), self.chip, range(3 * self.n), after)
        for k, x, p, q in zip(self.names, self.xs, self.ps, chip["lands"]):
            self.outs[k] = _reduce_adam(x, p, q, self.place, self.w[k], self.mom[k], self.var[k], self.l, self.outs[k])


def _size(shape):
    size = 1
    for d in shape:
        size *= d
    return size


def _slab_rows(shape):
    return -(-_size(shape) // (8 * LANES)) * 8


def _pack(arrs):
    parts = []
    for a in arrs:
        flat = a.reshape(-1).astype(F32)
        parts.append(jnp.pad(flat, (0, _slab_rows(a.shape) * LANES - flat.shape[0])).reshape(-1, LANES))
    return jnp.concatenate(parts, axis=0)


def _unpack(slab, shapes):
    out, off = [], 0
    for shp in shapes:
        rows = _slab_rows(shp)
        out.append(slab[off:off + rows].reshape(-1)[:_size(shp)].reshape(shp))
        off += rows
    return out


def _dense_blocks(w):
    eye = jnp.eye(LRU_BLOCKS, dtype=w.dtype)
    return (eye[:, None, :, None] * w[:, :, None, :]).reshape(LRU_W, LRU_W)


def _diag_blocks(dense):
    on_diagonal = jnp.eye(LRU_BLOCKS, dtype=bool)[:, None, :, None]
    return jnp.sum(jnp.where(on_diagonal, dense.reshape(LRU_BLOCKS, LRU_BLOCK, LRU_BLOCKS, LRU_BLOCK), 0.0), axis=2)


def _alpha_lanes(v):
    return jnp.zeros((1, BA_PAD), F32).at[0, HEADS:2 * HEADS].set(v)


def _local_step(x, target, mod, p, fetch, reducers=None):
    nl = mod.shape[0]
    row = lambda v: v.reshape(1, -1)
    masks = _gdn_masks()
    saved = []
    xc = x
    for l in range(nl):
        win, wba, lin = fetch(l, "in", xc)
        mv = [row(mod[l, k * D_MODEL:(k + 1) * D_MODEL]) for k in range(N_MOD)]
        sh1, sc1, g1, sh2, sc2, g2 = mv
        nw1, nw2 = row(p["norm_mix_w"][l]), row(p["norm_mlp_w"][l])
        wa, wx = _dense_blocks(p["lru_gate_a_w"][l]).astype(BF16), _dense_blocks(p["lru_gate_x_w"][l]).astype(BF16)
        lru_args = (p["lru_conv_w"][l], row(p["lru_conv_b"][l]), wa, wx, row(p["lru_gate_a_b"][l]),
                    row(p["lru_gate_x_b"][l]), row(p["lru_lambda"][l]), row(p["lru_norm_w"][l]))
        gdn_args = (p["gdn_conv_w"][l], _alpha_lanes(p["gdn_a_log"][l]), _alpha_lanes(p["gdn_dt_bias"][l]), masks)
        gnw = row(p["gdn_norm_w"][l])
        proj, ba = _inproj_fwd(xc, nw1, sc1, sh1, win, wba, lin)
        ol, hs = _lru_fwd(proj, *lru_args)
        *prep, tinv = _gdn_prep_fwd(proj, ba, *gdn_args)
        og, st = _gdn_scan_fwd(prep, proj, gnw)
        wo, wup, wdn = fetch(l, "rest", og)
        x1, mix, ff, x2 = _out_mlp_fwd(ol, og, xc, wo, g1, nw2, sc2, sh2, g2, wup, wdn, l)
        saved.append(dict(x=xc, mv=mv, nw1=nw1, nw2=nw2, lru_args=lru_args, gdn_args=gdn_args, gnw=gnw, proj=proj,
                          ba=ba, ol=ol, hs=hs, prep=prep, tinv=tinv, og=og, st=st, x1=x1, mix=mix, ff=ff,
                          win=win, wba=wba, lin=lin))
        xc = x2

    dx, frows = _final_fwd_bwd(xc, target, row(p["final_norm_w"]))
    loss_part = frows[1, 0]
    small = {k: [None] * nl for k in ("norm_mix_w", "norm_mlp_w", "lru_conv_w", "lru_conv_b", "lru_gate_a_w",
                                      "lru_gate_a_b", "lru_gate_x_w", "lru_gate_x_b", "lru_lambda", "lru_norm_w",
                                      "gdn_conv_w", "gdn_a_log", "gdn_dt_bias", "gdn_norm_w")}
    fc = D_FF // N_DEV
    big = [None] * nl
    dmod = [None] * nl
    mlp_red, mix_red = reducers or (None, None)
    busy = False
    for l in reversed(range(nl)):
        sv = saved[l]
        sh1, sc1, g1, sh2, sc2, g2 = sv["mv"]
        gnw = sv["gnw"]
        if busy:
            g2 = g2 + started
        act, dup, h2b, dffb, dx1, rows2 = _mlp_bwd(dx, sv["x1"], sv["ff"], sv["nw2"], sc2, sh2, g2, wup, wdn, l)
        if busy:
            g1 = g1 + mix_red.middle(dx1)
        g_up = _tn_matmul(h2b, dup, "grad_w_up", out=lax.empty((1, N_DEV, D_MODEL, fc), F32), blocked=True)[0]
        g_down = _tn_matmul(act, dffb, "grad_w_down", out=lax.empty((1, D_FF, D_MODEL), F32))
        g_down = g_down.reshape(N_DEV, fc, D_MODEL)
        if mlp_red is not None:
            g1 = g1 + mlp_red.start(l, dict(w_up=g_up, w_down=g_down))
        dmix, dol, dog, rows1 = _outproj_bwd(dx1, sv["mix"], g1, wo, l)
        g_out = _tn_matmul(sv["ol"], dmix, "grad_w_out_lru", out=lax.empty((1, D_MODEL, D_MODEL), F32))
        g_out = _tn_matmul(sv["og"], dmix, "grad_w_out_gdn", out=g_out, row_block=1)
        dpl, dwa, dwx, lrows = _lru_bwd(dol, sv["proj"], sv["hs"], *sv["lru_args"])
        if mlp_red is not None:
            gnw = gnw + mlp_red.middle(dpl)
        *cts, dpz, gnrow = _gdn_scan_bwd(dog, sv["prep"], sv["st"], sv["proj"], gnw)
        dpq, dba, dcw, dpar = _gdn_prep_bwd(cts, sv["tinv"], sv["proj"], sv["ba"], *sv["gdn_args"])
        dx, hb, rows0 = _inproj_bwd(dpl, dpq, dpz, dba, sv["x"], dx1, sv["nw1"], sc1, sh1, sv["win"], sv["wba"],
                                    sv["lin"])
        if busy:
            mix_red.finish(dx)
        if mlp_red is not None:
            mlp_red.finish(dx)
        dproj = jnp.concatenate([dpl, dpq, dpz, dba], axis=1)
        g_in = jnp.transpose(_tn_matmul(hb, dproj, "grad_w_in")[:, :IN_COLS].reshape(
            D_MODEL, N_DEV, IN_COLS // N_DEV), (1, 0, 2))
        big[l] = dict(w_in=g_in, w_out=g_out.reshape(N_DEV, D_MODEL // N_DEV, D_MODEL), w_up=g_up,
                      w_down=g_down)
        if mix_red is not None:
            started, busy = mix_red.start(l, big[l]), True
        dmod[l] = jnp.concatenate([rows0[0], rows0[1], rows1[0], rows2[0], rows2[1], rows2[2]])
        small["norm_mix_w"][l], small["norm_mlp_w"][l] = rows0[2], rows2[3]
        small["lru_conv_w"][l], small["lru_conv_b"][l] = lrows[8:8 + CONV_K], lrows[0]
        small["lru_gate_a_w"][l], small["lru_gate_x_w"][l] = _diag_blocks(dwa), _diag_blocks(dwx)
        small["lru_gate_a_b"][l], small["lru_gate_x_b"][l] = lrows[1], lrows[2]
        small["lru_lambda"][l], small["lru_norm_w"][l] = lrows[3], lrows[4]
        small["gdn_conv_w"][l] = dcw
        small["gdn_a_log"][l], small["gdn_dt_bias"][l] = dpar[0, HEADS:2 * HEADS], dpar[1, HEADS:2 * HEADS]
        small["gdn_norm_w"][l] = gnrow[0]
    small = {k: jnp.stack(v) for k, v in small.items()}
    small["final_norm_w"] = frows[0]
    return loss_part, dx, big, small, jnp.stack(dmod)


SMALL_REPLICATED = ("norm_mix_w", "norm_mlp_w", "b_mod", "lru_conv_b", "lru_gate_a_w", "lru_gate_a_b", "lru_gate_x_w",
                    "lru_gate_x_b", "lru_lambda", "lru_norm_w", "gdn_a_log", "gdn_dt_bias", "gdn_norm_w",
                    "final_norm_w")
SMALL_SHARDED = ("lru_conv_w", "gdn_conv_w")
WEIGHT_ORDER = ("norm_mix_w", "norm_mlp_w", "w_mod", "b_mod", "w_in", "lru_conv_w", "lru_conv_b", "lru_gate_a_w",
                "lru_gate_a_b", "lru_gate_x_w", "lru_gate_x_b", "lru_lambda", "lru_norm_w", "gdn_conv_w", "gdn_a_log",
                "gdn_dt_bias", "gdn_norm_w", "w_out", "w_up", "w_down", "final_norm_w")


def kernel(x, c, norm_mix_w, norm_mlp_w, w_mod, b_mod, w_in, lru_conv_w, lru_conv_b, lru_gate_a_w, lru_gate_a_b, lru_gate_x_w, lru_gate_x_b, lru_lambda, lru_norm_w, gdn_conv_w, gdn_a_log, gdn_dt_bias, gdn_norm_w, w_out, w_up, w_down, final_norm_w, loss_target, m_norm_mix_w, m_norm_mlp_w, m_w_mod, m_b_mod, m_w_in, m_lru_conv_w, m_lru_conv_b, m_lru_gate_a_w, m_lru_gate_a_b, m_lru_gate_x_w, m_lru_gate_x_b, m_lru_lambda, m_lru_norm_w, m_gdn_conv_w, m_gdn_a_log, m_gdn_dt_bias, m_gdn_norm_w, m_w_out, m_w_up, m_w_down, m_final_norm_w, v_norm_mix_w, v_norm_mlp_w, v_w_mod, v_b_mod, v_w_in, v_lru_conv_w, v_lru_conv_b, v_lru_gate_a_w, v_lru_gate_a_b, v_lru_gate_x_w, v_lru_gate_x_b, v_lru_lambda, v_lru_norm_w, v_gdn_conv_w, v_gdn_a_log, v_gdn_dt_bias, v_gdn_norm_w, v_w_out, v_w_up, v_w_down, v_final_norm_w):
    args = dict(locals())
    w = {k: args[k] for k in WEIGHT_ORDER}
    mom = {k: args["m_" + k] for k in WEIGHT_ORDER}
    var = {k: args["v_" + k] for k in WEIGHT_ORDER}
    nl = w_in.shape[0]
    px, py, pc = _place()
    me = 4 * px + 2 * py + pc
    other_chips = [2 * (1 - px) + py, 2 * px + (1 - py), 2 * (1 - px) + (1 - py)]
    others = jnp.stack([2 * q + pc for q in other_chips] + other_chips).astype(jnp.int32)

    shapes0 = [c.shape, lru_conv_w.shape, gdn_conv_w.shape]
    (g0,) = _all_gather([_pack([c, lru_conv_w, gdn_conv_w])], "gather_cond", pltpu.VMEM)
    per_dev = [_unpack(g0[d], shapes0) for d in range(N_DEV)]
    c_all = jnp.concatenate([pd[0] for pd in per_dev], axis=0)
    lru_conv_full = jnp.concatenate([pd[1] for pd in per_dev], axis=-1)
    gdn_conv_full = jnp.concatenate([pd[2] for pd in per_dev], axis=-1)

    cols = w_mod.shape[2]
    bmod_cols = lax.dynamic_slice_in_dim(b_mod, me * cols, cols, axis=1).reshape(nl, 1, cols)
    mod_cols = _mod_local(c_all, w_mod, bmod_cols)
    (g1,) = _all_gather([mod_cols.reshape(nl * N_DEV, cols)], "gather_mod", pltpu.VMEM)
    g1 = g1.reshape(N_DEV, nl, N_DEV, cols)
    mod = jnp.transpose(lax.dynamic_index_in_dim(g1, me, axis=2, keepdims=False), (1, 0, 2)).reshape(nl, N_DEV * cols)

    shards = [a.astype(BF16) for a in (w_in, w_out, w_up, w_down)]
    (first_in,) = _all_gather([shards[0][:1]], "gather_w_in_first", pl.ANY)
    items = [(a, 0) for a in (1, 2, 3)] + [(a, l) for l in range(1, nl) for a in range(4)]
    plan = _gather_plan(items)
    lands = [lax.dynamic_update_slice_in_dim(lax.empty((N_DEV,) + a.shape, BF16), a[None], me, axis=0) for a in shards]
    flight = [_split_start("gather_weights_start", plan, len(items) * GATHER_PEERS, shards, lands, [first_in, mod])]
    mod = mod + flight[0]["token"][0, 0]

    def fetch(l, what, after):
        wanted = [k for k, (a, ll) in enumerate(items) if ll == l and (a == 0) == (what == "in")]
        if wanted:
            flight[0] = _split_wait(f"gather_weights_wait_{what}{l}", plan, flight[0],
                                    [k * GATHER_PEERS + r for k in wanted for r in range(GATHER_PEERS)], after)
        gin, gout, gup, gdn = flight[0]["lands"]
        if what == "rest":
            return gout, gup, gdn
        gin = first_in[:, 0] if l == 0 else gin[:, l]
        win = jnp.transpose(gin, (1, 0, 2)).reshape(1, D_MODEL, IN_COLS)
        wba = jnp.pad(win[:, :, IN_MAIN:], ((0, 0), (0, 0), (0, BA_PAD - (IN_COLS - IN_MAIN))))
        return win, wba, 0

    p = dict(w)
    p["lru_conv_w"], p["gdn_conv_w"] = lru_conv_full, gdn_conv_full

    place = jnp.stack([me, 2 * px + py]).astype(jnp.int32)
    reducers = (_GradReducer("mlp_grad", ("w_up", "w_down"), w, mom, var, place, others),
                _GradReducer("mix_grad", ("w_in", "w_out"), w, mom, var, place, others))
    loss_part, grad_x, _, small, dmod = _local_step(x[0], loss_target[0], mod, p, fetch, reducers)
    loss = lax.psum(loss_part, MESH_AXES)

    small_names = sorted(small)
    slab = _pack([dmod] + [small[k] for k in small_names])
    (gs,) = _all_gather([slab], "gather_small_grads", pltpu.VMEM)
    chips_started = reducers[1].middle(gs)
    dmod_all = gs[:, :_slab_rows(dmod.shape)].reshape(N_DEV, nl, N_MOD * D_MODEL)
    summed = _unpack(_sum_devices(gs) + chips_started, [dmod.shape] + [small[k].shape for k in small_names])
    grads = dict(zip(small_names, summed[1:]))
    grads["b_mod"] = summed[0]
    for k, width in (("lru_conv_w", LRU_W // N_DEV), ("gdn_conv_w", 3 * GDN_W // N_DEV)):
        grads[k] = lax.dynamic_slice_in_dim(grads[k], me * width, width, axis=2)
    names = SMALL_REPLICATED + SMALL_SHARDED
    shapes = [w[k].shape for k in names]
    dl, nm, nv = _adam_flat(_pack([w[k] for k in names]), _pack([grads[k] for k in names]),
                            _pack([mom[k] for k in names]), _pack([var[k] for k in names]))
    delta = dict(zip(names, _unpack(dl, shapes)))
    new_m = dict(zip(names, _unpack(nm, shapes)))
    new_v = dict(zip(names, _unpack(nv, shapes)))

    dmod_cols = jnp.transpose(lax.dynamic_slice_in_dim(dmod_all, me * cols, cols, axis=2), (1, 0, 2))
    grads["w_mod"], delta["w_mod"], new_m["w_mod"], new_v["w_mod"] = _wmod_update(
        c_all, dmod_cols, w_mod, m_w_mod, v_w_mod)

    reducers[1].finish(new_v["w_mod"])
    for red in reducers:
        for k in red.names:
            grads[k], delta[k], new_m[k], new_v[k] = red.outs[k]

    return (loss, grad_x[None], *[grads[k] for k in WEIGHT_ORDER], *[delta[k] for k in WEIGHT_ORDER],
            *[new_m[k] for k in WEIGHT_ORDER], *[new_v[k] for k in WEIGHT_ORDER])
```

```python
import functools

import jax
import jax.numpy as jnp
from jax import lax
from jax.experimental import pallas as pl
from jax.experimental.pallas import tpu as pltpu

F32 = jnp.float32
BF16 = jnp.bfloat16

D_MODEL = 1024
LRU_W = 512
LRU_BLOCKS = 8
LRU_BLOCK = 64
LRU_C = 8.0
GDN_W = 512
HEADS = 4
HEAD_DIM = 128
CHUNK = 64
STACK = HEADS * CHUNK
CONV_K = 4
D_FF = 4096
N_MOD = 6
IN_COLS = 3080
IN_MAIN = 3072
BA_PAD = 128
EPS = 1e-6
N_DEV = 8
HALO = 8
MLP_BLOCKS = 4
PREP_CHUNKS = 2
SCAN_CHUNKS = 8
LANES = 128
ADAM_LR, ADAM_B1, ADAM_B2, ADAM_EPS, ADAM_WD, ADAM_STEP = 0.001, 0.9, 0.999, 1e-08, 0.01, 10
MESH_AXES = ("x", "y", "c")
MESH = pl.DeviceIdType.MESH

NN = (((1,), (0,)), ((), ()))
NT = (((1,), (1,)), ((), ()))
TN = (((0,), (0,)), ((), ()))


def _bdot(a, b, dims=NN):
    return lax.dot_general(a.astype(BF16), b.astype(BF16), dims, preferred_element_type=F32)


def _sdot(a, b, dims=NN):
    ah, bh = a.astype(BF16), b.astype(BF16)
    al, bl = (a - ah.astype(F32)).astype(BF16), (b - bh.astype(F32)).astype(BF16)
    return _bdot(ah, bh, dims) + (_bdot(al, bh, dims) + _bdot(ah, bl, dims))


def _hdot(a, b, dims=NN):
    return lax.dot_general(a, b, dims, precision=lax.Precision.HIGHEST, preferred_element_type=F32)


def _sds(shape, dtype=F32):
    return jax.ShapeDtypeStruct(tuple(shape), dtype)


def _tile(n, t):
    return min(n, t)


def _call(body, name, grid, in_specs, out_specs, out_shape, scratch=(), vmem_mb=48, prefetch=0, aliases=None):
    params = pltpu.CompilerParams(dimension_semantics=("arbitrary",) * len(grid), vmem_limit_bytes=vmem_mb * 2**20)
    if prefetch:
        spec = pltpu.PrefetchScalarGridSpec(num_scalar_prefetch=prefetch, grid=grid, in_specs=in_specs,
                                            out_specs=out_specs, scratch_shapes=list(scratch))
        return pl.pallas_call(body, name=name, grid_spec=spec, out_shape=out_shape, compiler_params=params,
                              input_output_aliases=aliases or {})
    return pl.pallas_call(body, name=name, grid=grid, in_specs=in_specs, out_specs=out_specs, out_shape=out_shape,
                          scratch_shapes=list(scratch), compiler_params=params, input_output_aliases=aliases or {})


def _tok(t, n, col=0):
    return pl.BlockSpec((t, n), lambda i, *_: (i, col))


def _vec(n):
    return pl.BlockSpec((1, n), lambda *_: (0, 0))


def _whole(a):
    nd = a.ndim
    return pl.BlockSpec(a.shape, lambda *_: (0,) * nd)


def _layer(l, *dims):
    return pl.BlockSpec((1,) + dims, lambda *_: (l,) + (0,) * len(dims))


def _gelu(y):
    c0, c1 = 0.7978845608028654, 0.044715
    return 0.5 * y * (1.0 + jnp.tanh(c0 * (y + c1 * y * y * y)))


def _gelu_grad(y):
    c0, c1 = 0.7978845608028654, 0.044715
    t = jnp.tanh(c0 * (y + c1 * y * y * y))
    return 0.5 * (1.0 + t) + 0.5 * y * (1.0 - t * t) * c0 * (1.0 + 3.0 * c1 * y * y)


def _softplus(v):
    return jnp.maximum(v, 0.0) + jnp.log(1.0 + jnp.exp(-jnp.where(v > 0, v, -v)))


@functools.partial(jax.custom_vjp, nondiff_argnums=(1,))
def _roll_rows(v, s):
    s = s % v.shape[0]
    return pltpu.roll(v, s, axis=0) if s else v


def _roll_rows_fwd(v, s):
    return _roll_rows(v, s), None


def _roll_rows_bwd(s, _, g):
    return (_roll_rows(g, -s),)


_roll_rows.defvjp(_roll_rows_fwd, _roll_rows_bwd)


@jax.custom_vjp
def _drop_halo(v):
    return v[HALO:]


def _drop_halo_fwd(v):
    return v[HALO:], None


def _drop_halo_bwd(_, g):
    return (jnp.concatenate([jnp.zeros((HALO, g.shape[1]), g.dtype), g], axis=0),)


_drop_halo.defvjp(_drop_halo_fwd, _drop_halo_bwd)


@functools.partial(jax.custom_vjp, nondiff_argnums=(1, 2))
def _split(v, n, axis):
    w = v.shape[axis] // n
    return tuple(lax.slice_in_dim(v, k * w, (k + 1) * w, axis=axis) for k in range(n))


def _split_fwd(v, n, axis):
    return _split(v, n, axis), None


def _split_bwd(n, axis, _, gs):
    return (jnp.concatenate(list(gs), axis=axis),)


_split.defvjp(_split_fwd, _split_bwd)


def _conv_taps(xw):
    return [_drop_halo(_roll_rows(xw, CONV_K - 1 - k)) for k in range(CONV_K)]


def _modulated_norm(xv, nw, sc, sh):
    r = lax.rsqrt(jnp.mean(xv * xv, axis=-1, keepdims=True) + EPS)
    n = xv * r * nw
    return n * (1.0 + sc) + sh, n, r


def _modulated_norm_bwd(dh, xv, n, r, nw, sc):
    dn = dh * (1.0 + sc)
    dxn = dn * nw
    dx = r * dxn - xv * (r * r * r) * jnp.mean(dxn * xv, axis=-1, keepdims=True)
    return (dx, jnp.sum(dh, axis=0, keepdims=True), jnp.sum(dh * n, axis=0, keepdims=True),
            jnp.sum(dn * xv * r, axis=0, keepdims=True))


def _inproj_fwd(x, nw, sc, sh, win, wba, l):
    s = x.shape[0]
    t = _tile(s, 512)

    def body(x_ref, nw_ref, sc_ref, sh_ref, win_ref, wba_ref, proj_ref, ba_ref):
        h, _, _ = _modulated_norm(x_ref[...], nw_ref[...], sc_ref[...], sh_ref[...])
        hb = h.astype(BF16)
        proj_ref[...] = _bdot(hb, win_ref[0])
        ba_ref[...] = _bdot(hb, wba_ref[0])

    return _call(body, "inproj_fwd", (s // t,),
                 [_tok(t, D_MODEL), _vec(D_MODEL), _vec(D_MODEL), _vec(D_MODEL), _layer(l, D_MODEL, IN_MAIN),
                  _layer(l, D_MODEL, BA_PAD)],
                 [_tok(t, IN_MAIN), _tok(t, BA_PAD)],
                 [_sds((s, IN_MAIN)), _sds((s, BA_PAD))])(x, nw, sc, sh, win, wba)


def _inproj_bwd(dpl, dpq, dpz, dba, x, dx1, nw, sc, sh, win, wba, l):
    s = x.shape[0]
    t = _tile(s, 512)

    def body(dpl_ref, dpq_ref, dpz_ref, dba_ref, x_ref, dx1_ref, nw_ref, sc_ref, sh_ref, win_ref, wba_ref,
             dx_ref, hb_ref, acc_ref):
        @pl.when(pl.program_id(0) == 0)
        def _():
            acc_ref[...] = jnp.zeros_like(acc_ref)

        dh = (_bdot(dpl_ref[...], win_ref[0, :, 0:2 * LRU_W], NT)
              + _bdot(dpq_ref[...], win_ref[0, :, 2 * LRU_W:2 * LRU_W + 3 * GDN_W], NT)
              + _bdot(dpz_ref[...], win_ref[0, :, 2 * LRU_W + 3 * GDN_W:IN_MAIN], NT)
              + _bdot(dba_ref[...], wba_ref[0], NT))
        xv = x_ref[...]
        h, n, r = _modulated_norm(xv, nw_ref[...], sc_ref[...], sh_ref[...])
        hb_ref[...] = h.astype(BF16)
        dx, dsh, dsc, dnw = _modulated_norm_bwd(dh, xv, n, r, nw_ref[...], sc_ref[...])
        dx_ref[...] = dx1_ref[...] + dx
        acc_ref[0:1, :] += dsh
        acc_ref[1:2, :] += dsc
        acc_ref[2:3, :] += dnw

    return _call(body, "inproj_bwd", (s // t,),
                 [_tok(t, 2 * LRU_W), _tok(t, 3 * GDN_W), _tok(t, GDN_W), _tok(t, BA_PAD), _tok(t, D_MODEL),
                  _tok(t, D_MODEL), _vec(D_MODEL), _vec(D_MODEL), _vec(D_MODEL), _layer(l, D_MODEL, IN_MAIN),
                  _layer(l, D_MODEL, BA_PAD)],
                 [_tok(t, D_MODEL), _tok(t, D_MODEL), pl.BlockSpec((8, D_MODEL), lambda i: (0, 0))],
                 [_sds((s, D_MODEL)), _sds((s, D_MODEL), BF16), _sds((8, D_MODEL))])(
                     dpl, dpq, dpz, dba, x, dx1, nw, sc, sh, win, wba)


def _lru_gates(xw, cw_rows, cb, wa, wx, gab, gxb, lam):
    taps = _conv_taps(xw)
    xr = cb + cw_rows[0] * taps[0] + cw_rows[1] * taps[1] + cw_rows[2] * taps[2] + cw_rows[3] * taps[3]
    xb = xr.astype(BF16)
    r = jax.nn.sigmoid(_bdot(xb, wa) + gab)
    i = jax.nn.sigmoid(_bdot(xb, wx) + gxb)
    z = jnp.exp(-jnp.where(lam > 0, lam, -lam))
    w1 = 1.0 + z
    log1p_z = jnp.where(w1 == 1.0, z, jnp.log(w1) * z / (w1 - 1.0))
    ls = jnp.minimum(lam, 0.0) - log1p_z
    la = LRU_C * r * ls
    a = jnp.exp(la)
    mm_raw = -jnp.tanh(la) * (a * a + 1.0)
    mult = jnp.sqrt(jnp.maximum(mm_raw, 1e-12))
    return dict(taps=taps, xr=xr, r=r, i=i, ls=ls, a=a, mm_raw=mm_raw, mult=mult)


def _lru_specs(s, t, tile_of):
    nh = t // HALO
    xl = pl.BlockSpec((t, LRU_W), lambda i: (tile_of(i), 0))
    yl = pl.BlockSpec((t, LRU_W), lambda i: (tile_of(i), 1))
    hx = pl.BlockSpec((HALO, LRU_W), lambda i: (jnp.maximum(tile_of(i) * nh - 1, 0), 0))
    return xl, yl, hx


def _lru_fwd(proj, cw, cb, wa, wx, gab, gxb, lam, lnw):
    s = proj.shape[0]
    t = _tile(s, 256)
    xl, yl, hx = _lru_specs(s, t, lambda i: i)

    def body(xl_ref, yl_ref, hx_ref, cw_ref, cb_ref, wa_ref, wx_ref, gab_ref, gxb_ref, lam_ref, lnw_ref,
             out_ref, h_ref, a_s, b_s, hc):
        i = pl.program_id(0)

        @pl.when(i == 0)
        def _():
            hc[...] = jnp.zeros_like(hc)

        halo = jnp.where(i > 0, hx_ref[...], 0.0)
        xw = jnp.concatenate([halo, xl_ref[...]], axis=0)
        g = _lru_gates(xw, [cw_ref[k:k + 1, :] for k in range(CONV_K)], cb_ref[...], wa_ref[...], wx_ref[...],
                       gab_ref[...], gxb_ref[...], lam_ref[...])
        a_s[...] = g["a"]
        b_s[...] = g["mult"] * (g["i"] * g["xr"])

        def step(k, h):
            h = a_s[pl.ds(k, 1), :] * h + b_s[pl.ds(k, 1), :]
            h_ref[pl.ds(k, 1), :] = h
            return h

        hc[...] = lax.fori_loop(0, t, step, hc[...], unroll=8)
        m = h_ref[...] * _gelu(yl_ref[...])
        out_ref[...] = m * lax.rsqrt(jnp.mean(m * m, axis=-1, keepdims=True) + EPS) * lnw_ref[...]

    return _call(body, "lru_fwd", (s // t,),
                 [xl, yl, hx, _whole(cw), _vec(LRU_W), _whole(wa), _whole(wx)] + [_vec(LRU_W)] * 4,
                 [_tok(t, LRU_W), _tok(t, LRU_W)],
                 [_sds((s, LRU_W)), _sds((s, LRU_W))],
                 scratch=[pltpu.VMEM((t, LRU_W), F32), pltpu.VMEM((t, LRU_W), F32), pltpu.VMEM((1, LRU_W), F32)])(
                     proj, proj, proj, cw, cb, wa, wx, gab, gxb, lam, lnw)


def _lru_bwd(dout, proj, hs, cw, cb, wa, wx, gab, gxb, lam, lnw):
    s = proj.shape[0]
    t = _tile(s, 256)
    nt = s // t
    rev = lambda i: nt - 1 - i
    xl, yl, hx = _lru_specs(s, t, rev)
    nh = t // HALO
    tk = pl.BlockSpec((t, LRU_W), lambda i: (rev(i), 0))
    hh = pl.BlockSpec((HALO, LRU_W), lambda i: (jnp.maximum(rev(i) * nh - 1, 0), 0))

    def body(do_ref, xl_ref, yl_ref, hx_ref, h_ref, hh_ref, cw_ref, cb_ref, wa_ref, wx_ref, gab_ref, gxb_ref,
             lam_ref, lnw_ref, dp_ref, dwa_ref, dwx_ref, rows_ref, dh_s, dhd_s, carry, dxr_next):
        i = pl.program_id(0)
        first_tile = rev(i) == 0

        @pl.when(i == 0)
        def _():
            carry[...] = jnp.zeros_like(carry)
            dxr_next[...] = jnp.zeros_like(dxr_next)
            dwa_ref[...] = jnp.zeros_like(dwa_ref)
            dwx_ref[...] = jnp.zeros_like(dwx_ref)
            rows_ref[...] = jnp.zeros_like(rows_ref)

        halo = jnp.where(first_tile, 0.0, hx_ref[...])
        xw = jnp.concatenate([halo, xl_ref[...]], axis=0)
        cw_rows = [cw_ref[k:k + 1, :] for k in range(CONV_K)]
        lam_v = lam_ref[...]
        g = _lru_gates(xw, cw_rows, cb_ref[...], wa_ref[...], wx_ref[...], gab_ref[...], gxb_ref[...], lam_v)
        a, r, gi, xr, mult = g["a"], g["r"], g["i"], g["xr"], g["mult"]
        hv = h_ref[...]
        yv = yl_ref[...]
        gl = _gelu(yv)
        m = hv * gl
        rn = lax.rsqrt(jnp.mean(m * m, axis=-1, keepdims=True) + EPS)
        dov = do_ref[...]
        dmn = dov * lnw_ref[...]
        rows_ref[4:5, :] += jnp.sum(dov * m * rn, axis=0, keepdims=True)
        dm = rn * dmn - m * (rn * rn * rn) * jnp.mean(dmn * m, axis=-1, keepdims=True)
        dhd_s[...] = dm * gl
        dy = dm * hv * _gelu_grad(yv)
        dh_s[...] = a

        def step(k, c):
            row = t - 1 - k
            d = dhd_s[pl.ds(row, 1), :] + c
            c = dh_s[pl.ds(row, 1), :] * d
            dh_s[pl.ds(row, 1), :] = d
            return c

        carry[...] = lax.fori_loop(0, t, step, carry[...], unroll=8)
        dH = dh_s[...]
        hprev_halo = jnp.where(first_tile, 0.0, hh_ref[...])
        hprev = _drop_halo(_roll_rows(jnp.concatenate([hprev_halo, hv], axis=0), 1))
        da = dH * hprev
        dmult = dH * gi * xr
        di = dH * mult * xr
        dxr = dH * mult * gi
        dla = jnp.where(g["mm_raw"] > 1e-12, dmult * (0.5 / mult) * (-2.0 * a * a), 0.0) + da * a
        dr = dla * (LRU_C * g["ls"])
        sig_neg = jax.nn.sigmoid(-lam_v)
        rows_ref[3:4, :] += jnp.sum(dla * (LRU_C * r), axis=0, keepdims=True) * sig_neg
        drp = dr * r * (1.0 - r)
        dip = di * gi * (1.0 - gi)
        rows_ref[1:2, :] += jnp.sum(drp, axis=0, keepdims=True)
        rows_ref[2:3, :] += jnp.sum(dip, axis=0, keepdims=True)
        xb = xr.astype(BF16)
        drb = drp.astype(BF16)
        dib = dip.astype(BF16)
        dwa_ref[...] += _bdot(xb, drb, TN)
        dwx_ref[...] += _bdot(xb, dib, TN)
        dxr = dxr + _bdot(drb, wa_ref[...], NT) + _bdot(dib, wx_ref[...], NT)
        rows_ref[0:1, :] += jnp.sum(dxr, axis=0, keepdims=True)
        ext = jnp.concatenate([dxr, dxr_next[...]], axis=0)
        dx = cw_rows[CONV_K - 1] * dxr
        for k in range(CONV_K - 1):
            dx = dx + cw_rows[k] * _roll_rows(ext, -(CONV_K - 1 - k))[0:t]
        for k in range(CONV_K):
            rows_ref[8 + k:9 + k, :] += jnp.sum(dxr * g["taps"][k], axis=0, keepdims=True)
        dxr_next[...] = dxr[0:HALO]
        dp_ref[...] = jnp.concatenate([dx, dy], axis=1).astype(BF16)

    acc = lambda shape: pl.BlockSpec(shape, lambda i: (0, 0))
    return _call(body, "lru_bwd", (nt,),
                 [tk, xl, yl, hx, tk, hh, _whole(cw), _vec(LRU_W), _whole(wa), _whole(wx)] + [_vec(LRU_W)] * 4,
                 [pl.BlockSpec((t, 2 * LRU_W), lambda i: (rev(i), 0)), acc((LRU_W, LRU_W)), acc((LRU_W, LRU_W)),
                  acc((16, LRU_W))],
                 [_sds((s, 2 * LRU_W), BF16), _sds((LRU_W, LRU_W)), _sds((LRU_W, LRU_W)), _sds((16, LRU_W))],
                 scratch=[pltpu.VMEM((t, LRU_W), F32), pltpu.VMEM((t, LRU_W), F32), pltpu.VMEM((1, LRU_W), F32),
                          pltpu.VMEM((HALO, LRU_W), F32)])(
                     dout, proj, proj, proj, hs, hs, cw, cb, wa, wx, gab, gxb, lam, lnw)


def _gdn_masks():
    row = lax.broadcasted_iota(jnp.int32, (STACK, STACK), 0)
    col = lax.broadcasted_iota(jnp.int32, (STACK, STACK), 1)
    same = (row // CHUNK) == (col // CHUNK)
    return jnp.stack([(same & (col <= row)).astype(F32), (same & (col < row)).astype(F32), (row == col).astype(F32)])


def _conv_silu(xw, rows):
    taps = _conv_taps(xw)
    y = rows[0] * taps[0] + rows[1] * taps[1] + rows[2] * taps[2] + rows[3] * taps[3]
    return y * jax.nn.sigmoid(y)


def _split3(v):
    hi = v.astype(BF16)
    r1 = v - hi.astype(F32)
    mid = r1.astype(BF16)
    return hi, mid, (r1 - mid.astype(F32)).astype(BF16)


def _mask_dot_raw(mask, v, dims):
    parts = _split3(v)
    d = lambda p: lax.dot_general(mask, p, dims, preferred_element_type=F32)
    return d(parts[0]) + (d(parts[1]) + d(parts[2]))


@jax.custom_vjp
def _mask_dot(mask, v):
    return _mask_dot_raw(mask, v, NN)


def _mask_dot_fwd(mask, v):
    return _mask_dot_raw(mask, v, NN), mask


def _mask_dot_bwd(mask, ct):
    return jnp.zeros_like(mask), _mask_dot_raw(mask, ct, TN)


_mask_dot.defvjp(_mask_dot_fwd, _mask_dot_bwd)


def _unit_lower_inverse(ns, eye):
    tinvs = [eye + n for n in ns]
    ps = list(ns)
    for _ in range(5):
        ps = [_bdot(p, p) for p in ps]
        tinvs = [t + _bdot(t, p) for t, p in zip(tinvs, ps)]
    return tuple(t.astype(BF16) for t in tinvs)


def _refined(ns, rhss, tinvs, dims):
    x0s = [_bdot(t, r, dims) for t, r in zip(tinvs, rhss)]
    ress = [r - x0 + _sdot(n, x0, dims) for n, r, x0 in zip(ns, rhss, x0s)]
    return tuple(x0 + _bdot(t, res, dims) for t, x0, res in zip(tinvs, x0s, ress))


@jax.custom_vjp
def _unit_lower_solve(ns, rhss, tinvs):
    return _refined(ns, rhss, tinvs, NN)


def _unit_lower_solve_fwd(ns, rhss, tinvs):
    xs = _unit_lower_solve(ns, rhss, tinvs)
    return xs, (ns, tinvs, xs)


def _unit_lower_solve_bwd(res, cts):
    ns, tinvs, xs = res
    ys = _refined(ns, cts, tinvs, TN)
    return (tuple(_bdot(y, x, NT) for y, x in zip(ys, xs)), ys, tuple(jnp.zeros_like(t) for t in tinvs))


_unit_lower_solve.defvjp(_unit_lower_solve_fwd, _unit_lower_solve_bwd)


def _gdn_prep(xqs, xks, xvs, bas, cwq, cwk, cwv, pa, pd, masks, tinvs=None, with_inverse=False):
    lower, strict, eye = masks[0], masks[1], masks[2]
    lower_b = lower.astype(BF16)
    lane = lax.broadcasted_iota(jnp.int32, (CHUNK, LANES), 1)
    each = lambda f, *lists: [f(*vals) for vals in zip(*lists)]
    stack = lambda xw, rows: jnp.concatenate(_split(_conv_silu(xw, rows), HEADS, 1), axis=0)
    qs, ks, vs = (each(lambda xw: stack(xw, cw), xs) for xs, cw in ((xqs, cwq), (xks, cwk), (xvs, cwv)))
    qns = each(lambda q: q * lax.rsqrt(jnp.sum(q * q, axis=-1, keepdims=True) + 1e-6) * (HEAD_DIM ** -0.5), qs)
    kns = each(lambda k: k * lax.rsqrt(jnp.sum(k * k, axis=-1, keepdims=True) + 1e-6), ks)

    def col(a, j):
        return jnp.broadcast_to(jnp.sum(jnp.where(lane == j, a, 0.0), axis=1, keepdims=True), (CHUNK, HEAD_DIM))

    betas = each(lambda ba: jnp.concatenate([col(jax.nn.sigmoid(ba), h) for h in range(HEADS)], axis=0), bas)
    g_heads = each(lambda ba: [col(-jnp.exp(pa) * _softplus(ba + pd), HEADS + h) for h in range(HEADS)], bas)
    gs = each(lambda gh: jnp.concatenate(gh, axis=0), g_heads)
    gls = each(lambda gh: jnp.concatenate([jnp.broadcast_to(jnp.sum(g, axis=0, keepdims=True), (CHUNK, HEAD_DIM))
                                           for g in gh], axis=0), g_heads)
    gcs = each(lambda g: _mask_dot(lower_b, g), gs)

    def decay_of(gc):
        gc_rows = jnp.transpose(gc)
        return jnp.exp((jnp.concatenate([gc, gc], axis=1) - jnp.concatenate([gc_rows, gc_rows], axis=0)) * lower)

    decays = each(decay_of, gcs)
    egcs = each(jnp.exp, gcs)
    kbs = each(lambda kn, beta: kn * beta, kns, betas)
    ns = tuple(each(lambda kb, kn, decay: -(_bdot(kb, kn, NT) * decay * strict), kbs, kns, decays))
    if tinvs is None:
        tinvs = _unit_lower_inverse([lax.stop_gradient(n) for n in ns], eye)
    rhss = tuple(each(lambda v, beta, kb, egc: jnp.concatenate([v * beta, kb * egc], axis=1), vs, betas, kbs, egcs))
    sols = _unit_lower_solve(ns, rhss, tuple(tinvs))
    attns = each(lambda qn, kn, decay: _bdot(qn, kn, NT) * decay * lower, qns, kns, decays)
    outs = []
    for sol, qn, kn, egc, gl, gc, attn, tinv in zip(sols, qns, kns, egcs, gls, gcs, attns, tinvs):
        u, w = _split(sol, 2, 1)
        out = (u, w, qn * egc, kn * jnp.exp(gl - gc), attn, jnp.exp(gl))
        outs.append(out + (tinv,) if with_inverse else out)
    return outs


def _gdn_scan(states, u, w, qd, kt, attn, egl, z, nw):
    us, ws, qds, kts, egls = (_split(a, HEADS, 0) for a in (u, w, qd, kt, egl))
    vn = [us[h] - _bdot(ws[h], states[h]) for h in range(HEADS)]
    o = jnp.concatenate([_bdot(qds[h], states[h]) for h in range(HEADS)], axis=0)
    o = o + _bdot(attn, jnp.concatenate(vn, axis=0))
    new = [states[h] * jnp.concatenate([egls[h], egls[h]], axis=0) + _bdot(kts[h], vn[h], TN) for h in range(HEADS)]
    on = o * lax.rsqrt(jnp.mean(o * o, axis=-1, keepdims=True) + EPS) * nw
    return new, on * (z * jax.nn.sigmoid(z))


def _gdn_in_specs(step_of, chunks):
    nh = chunks * CHUNK // HALO
    main = [pl.BlockSpec((chunks * CHUNK, GDN_W), functools.partial(lambda col, i: (step_of(i), col), col))
            for col in (2, 3, 4)]
    halo = [pl.BlockSpec((HALO, GDN_W), functools.partial(lambda col, i: (jnp.maximum(step_of(i) * nh - 1, 0), col),
                                                         col)) for col in (2, 3, 4)]
    return main, halo


def _stk(width, step_of, chunks=1):
    return pl.BlockSpec((chunks * STACK, width), lambda i: (step_of(i), 0))


def _chunk_inputs(main_refs, halo_refs, k, first_step):
    rows = slice(k * CHUNK, (k + 1) * CHUNK)
    if k == 0:
        halos = [jnp.where(first_step, 0.0, h[...]) for h in halo_refs]
    else:
        halos = [m[k * CHUNK - HALO:k * CHUNK, :] for m in main_refs]
    return [jnp.concatenate([h, m[rows, :]], axis=0) for h, m in zip(halos, main_refs)]


def _gdn_prep_fwd(proj, ba, cw, pa, pd, masks):
    s = proj.shape[0]
    nc = s // CHUNK
    per = min(PREP_CHUNKS, nc)
    main, halo = _gdn_in_specs(lambda i: i, per)

    def body(xq_ref, xk_ref, xv_ref, hq_ref, hk_ref, hv_ref, ba_ref, cw_ref, pa_ref, pd_ref, mk_ref, *out_refs):
        first_step = pl.program_id(0) == 0
        rows = [[cw_ref[k:k + 1, j * GDN_W:(j + 1) * GDN_W] for k in range(CONV_K)] for j in range(3)]
        cst = [mk_ref[0], mk_ref[1], mk_ref[2]]
        xs = [_chunk_inputs((xq_ref, xk_ref, xv_ref), (hq_ref, hk_ref, hv_ref), k, first_step) for k in range(per)]
        outs = _gdn_prep([x[0] for x in xs], [x[1] for x in xs], [x[2] for x in xs],
                         [ba_ref[k * CHUNK:(k + 1) * CHUNK, :] for k in range(per)], rows[0], rows[1], rows[2],
                         pa_ref[...], pd_ref[...], cst, with_inverse=True)
        for k, out in enumerate(outs):
            for ref, val in zip(out_refs, out):
                ref[k * STACK:(k + 1) * STACK, :] = val.astype(ref.dtype)

    ident = lambda i: i
    stacked = lambda dt: _sds((nc * STACK, HEAD_DIM), dt)
    wide, thin = _stk(STACK, ident, per), _stk(HEAD_DIM, ident, per)
    return _call(body, "gdn_prep_fwd", (nc // per,),
                 main + halo + [_tok(per * CHUNK, BA_PAD), _whole(cw), _vec(BA_PAD), _vec(BA_PAD), _whole(masks)],
                 [thin] * 4 + [wide, thin, wide],
                 [stacked(F32), stacked(BF16), stacked(BF16), stacked(BF16), _sds((nc * STACK, STACK), BF16),
                  stacked(F32), _sds((nc * STACK, STACK), BF16)])(
                     proj, proj, proj, proj, proj, proj, ba, cw, pa, pd, masks)


def _gdn_prep_bwd(cts, tinv, proj, ba, cw, pa, pd, masks):
    s = proj.shape[0]
    nc = s // CHUNK
    per = min(PREP_CHUNKS, nc)
    steps = nc // per
    rev = lambda i: steps - 1 - i
    main, halo = _gdn_in_specs(rev, per)

    def body(du_ref, dw_ref, dqd_ref, dkt_ref, dattn_ref, degl_ref, tinv_ref, xq_ref, xk_ref, xv_ref, hq_ref, hk_ref,
             hv_ref, ba_ref, cw_ref, pa_ref, pd_ref, mk_ref, dp_ref, dba_ref, dcw_ref, dpar_ref, carry):
        i = pl.program_id(0)
        first_step = rev(i) == 0

        @pl.when(i == 0)
        def _():
            carry[...] = jnp.zeros_like(carry)
            dcw_ref[...] = jnp.zeros_like(dcw_ref)
            dpar_ref[...] = jnp.zeros_like(dpar_ref)

        rows = [[cw_ref[k:k + 1, j * GDN_W:(j + 1) * GDN_W] for k in range(CONV_K)] for j in range(3)]
        cst = [mk_ref[0], mk_ref[1], mk_ref[2]]
        xs = [_chunk_inputs((xq_ref, xk_ref, xv_ref), (hq_ref, hk_ref, hv_ref), k, first_step) for k in range(per)]
        stks = [slice(k * STACK, (k + 1) * STACK) for k in range(per)]
        tinvs = [tinv_ref[stk, :] for stk in stks]
        fn = lambda xqs, xks, xvs, bs, rq, rk, rv, a, d: _gdn_prep(xqs, xks, xvs, bs, rq, rk, rv, a, d, cst, tinvs=tinvs)
        _, vjp = jax.vjp(fn, [x[0] for x in xs], [x[1] for x in xs], [x[2] for x in xs],
                         [ba_ref[k * CHUNK:(k + 1) * CHUNK, :] for k in range(per)], rows[0], rows[1], rows[2],
                         pa_ref[...], pd_ref[...])
        dxqs, dxks, dxvs, dbas, drq, drk, drv, dpa, dpd = vjp(
            [tuple(ref[stk, :] for ref in (du_ref, dw_ref, dqd_ref, dkt_ref, dattn_ref, degl_ref)) for stk in stks])
        dxws = [jnp.concatenate(parts, axis=1) for parts in zip(dxqs, dxks, dxvs)]
        for k in range(per):
            dba_ref[k * CHUNK:(k + 1) * CHUNK, :] = dbas[k].astype(BF16)
        for j, dr in enumerate((drq, drk, drv)):
            for kk in range(CONV_K):
                dcw_ref[kk:kk + 1, j * GDN_W:(j + 1) * GDN_W] += dr[kk]
        dpar_ref[0:1, :] += dpa
        dpar_ref[1:2, :] += dpd
        pad = jnp.zeros((CHUNK - HALO, 3 * GDN_W), F32)
        for k in range(per):
            late = carry[...] if k == per - 1 else dxws[k + 1][0:HALO]
            dp_ref[k * CHUNK:(k + 1) * CHUNK, :] = (dxws[k][HALO:] + jnp.concatenate([pad, late], axis=0)).astype(BF16)
        carry[...] = dxws[0][0:HALO]

    acc = lambda shape: pl.BlockSpec(shape, lambda i: (0, 0))
    wide, thin = _stk(STACK, rev, per), _stk(HEAD_DIM, rev, per)
    return _call(body, "gdn_prep_bwd", (steps,),
                 [thin] * 4 + [wide, thin, wide] + main + halo
                 + [pl.BlockSpec((per * CHUNK, BA_PAD), lambda i: (rev(i), 0)), _whole(cw), _vec(BA_PAD), _vec(BA_PAD),
                    _whole(masks)],
                 [pl.BlockSpec((per * CHUNK, 3 * GDN_W), lambda i: (rev(i), 0)),
                  pl.BlockSpec((per * CHUNK, BA_PAD), lambda i: (rev(i), 0)), acc((CONV_K, 3 * GDN_W)),
                  acc((8, BA_PAD))],
                 [_sds((s, 3 * GDN_W), BF16), _sds((s, BA_PAD), BF16), _sds((CONV_K, 3 * GDN_W)), _sds((8, BA_PAD))],
                 scratch=[pltpu.VMEM((HALO, 3 * GDN_W), F32)])(
                     *cts, tinv, proj, proj, proj, proj, proj, proj, ba, cw, pa, pd, masks)


def _stack_heads(v):
    return jnp.concatenate(_split(v, HEADS, 1), axis=0)


def _unstack_heads(v):
    return jnp.concatenate(_split(v, HEADS, 0), axis=1)


def _gdn_scan_fwd(prep, proj, nw):
    s = proj.shape[0]
    nc = s // CHUNK
    per = min(SCAN_CHUNKS, nc)
    ident = lambda i: i
    srows = HEADS * HEAD_DIM

    def body(u_ref, w_ref, qd_ref, kt_ref, attn_ref, egl_ref, z_ref, nw_ref, out_ref, st_ref, state):
        @pl.when(pl.program_id(0) == 0)
        def _():
            state[...] = jnp.zeros_like(state)

        states = [state[h * HEAD_DIM:(h + 1) * HEAD_DIM, :] for h in range(HEADS)]
        for k in range(per):
            stk, tok = slice(k * STACK, (k + 1) * STACK), slice(k * CHUNK, (k + 1) * CHUNK)
            for h in range(HEADS):
                st_ref[k * srows + h * HEAD_DIM:k * srows + (h + 1) * HEAD_DIM, :] = states[h]
            states, out = _gdn_scan(states, u_ref[stk, :], w_ref[stk, :], qd_ref[stk, :], kt_ref[stk, :],
                                    attn_ref[stk, :], egl_ref[stk, :], _stack_heads(z_ref[tok, :]), nw_ref[...])
            out_ref[tok, :] = _unstack_heads(out)
        for h in range(HEADS):
            state[h * HEAD_DIM:(h + 1) * HEAD_DIM, :] = states[h]

    thin, wide = _stk(HEAD_DIM, ident, per), _stk(STACK, ident, per)
    return _call(body, "gdn_scan_fwd", (nc // per,),
                 [thin] * 4 + [wide, thin, _tok(per * CHUNK, GDN_W, col=5), _vec(HEAD_DIM)],
                 [_tok(per * CHUNK, GDN_W), pl.BlockSpec((per * srows, HEAD_DIM), lambda i: (i, 0))],
                 [_sds((s, GDN_W)), _sds((nc * srows, HEAD_DIM))],
                 scratch=[pltpu.VMEM((srows, HEAD_DIM), F32)])(*prep, proj, nw)


def _gdn_scan_bwd(dout, prep, st, proj, nw):
    s = proj.shape[0]
    nc = s // CHUNK
    per = min(SCAN_CHUNKS, nc)
    steps = nc // per
    rev = lambda i: steps - 1 - i
    srows = HEADS * HEAD_DIM

    def body(do_ref, u_ref, w_ref, qd_ref, kt_ref, attn_ref, egl_ref, st_ref, z_ref, nw_ref,
             du_ref, dw_ref, dqd_ref, dkt_ref, dattn_ref, degl_ref, dz_ref, dnw_ref, dstate):
        @pl.when(pl.program_id(0) == 0)
        def _():
            dstate[...] = jnp.zeros_like(dstate)
            dnw_ref[...] = jnp.zeros_like(dnw_ref)

        dnew = [dstate[h * HEAD_DIM:(h + 1) * HEAD_DIM, :] for h in range(HEADS)]
        for k in reversed(range(per)):
            stk, tok = slice(k * STACK, (k + 1) * STACK), slice(k * CHUNK, (k + 1) * CHUNK)
            states = [st_ref[k * srows + h * HEAD_DIM:k * srows + (h + 1) * HEAD_DIM, :] for h in range(HEADS)]
            f32 = lambda ref: ref[stk, :].astype(F32)
            _, vjp = jax.vjp(_gdn_scan, states, u_ref[stk, :], f32(w_ref), f32(qd_ref), f32(kt_ref), f32(attn_ref),
                             egl_ref[stk, :], _stack_heads(z_ref[tok, :]), nw_ref[...])
            dnew, du, dw, dqd, dkt, dattn, degl, dz, dnw = vjp((dnew, _stack_heads(do_ref[tok, :])))
            for ref, val in zip((du_ref, dw_ref, dqd_ref, dkt_ref, dattn_ref, degl_ref),
                                (du, dw, dqd, dkt, dattn, degl)):
                ref[stk, :] = val
            dz_ref[tok, :] = _unstack_heads(dz).astype(BF16)
            dnw_ref[0:1, :] += dnw
        for h in range(HEADS):
            dstate[h * HEAD_DIM:(h + 1) * HEAD_DIM, :] = dnew[h]

    tokr = lambda n, col=0: pl.BlockSpec((per * CHUNK, n), lambda i: (rev(i), col))
    thin, wide = _stk(HEAD_DIM, rev, per), _stk(STACK, rev, per)
    return _call(body, "gdn_scan_bwd", (steps,),
                 [tokr(GDN_W)] + [thin] * 4 + [wide, thin, pl.BlockSpec((per * srows, HEAD_DIM), lambda i: (rev(i), 0)),
                                               tokr(GDN_W, 5), _vec(HEAD_DIM)],
                 [thin] * 4 + [wide, thin, tokr(GDN_W), pl.BlockSpec((8, HEAD_DIM), lambda i: (0, 0))],
                 [_sds((nc * STACK, HEAD_DIM))] * 4 + [_sds((nc * STACK, STACK)), _sds((nc * STACK, HEAD_DIM)),
                                                       _sds((s, GDN_W), BF16), _sds((8, HEAD_DIM))],
                 scratch=[pltpu.VMEM((srows, HEAD_DIM), F32)])(dout, *prep, st, proj, nw)


def _wo_specs(l):
    half = N_DEV // 2
    return [pl.BlockSpec((half, 1, D_MODEL // N_DEV, D_MODEL), functools.partial(lambda k, *_: (k, l, 0, 0), k))
            for k in range(2)]


def _wo_half(ref):
    return ref[:, 0].reshape(ref.shape[0] * ref.shape[2], ref.shape[3])


def _out_mlp_fwd(ol, og, x, wo, g1, nw2, sc2, sh2, g2, wup, wdn, l):
    s = x.shape[0]
    t = _tile(s, 512)
    nj = wup.shape[0] // MLP_BLOCKS
    fc = wup.shape[3]

    def body(ol_ref, og_ref, x_ref, wol_ref, wog_ref, g1_ref, nw_ref, sc_ref, sh_ref, g2_ref, wup_ref, wdn_ref,
             x1_ref, mix_ref, ff_ref, x2_ref, h2_s, acc_s):
        j = pl.program_id(1)

        @pl.when(j == 0)
        def _():
            mix = _bdot(ol_ref[...], _wo_half(wol_ref)) + _bdot(og_ref[...], _wo_half(wog_ref))
            x1 = x_ref[...] + g1_ref[...] * mix
            mix_ref[...] = mix.astype(BF16)
            x1_ref[...] = x1
            h2, _, _ = _modulated_norm(x1, nw_ref[...], sc_ref[...], sh_ref[...])
            h2_s[...] = h2.astype(BF16)
            acc_s[...] = jnp.zeros_like(acc_s)

        part = None
        for b in range(MLP_BLOCKS):
            up = _bdot(h2_s[...], wup_ref[b, 0])
            down = _bdot(jnp.square(jnp.maximum(up, 0.0)), wdn_ref[b, 0])
            part = down if part is None else part + down
        acc_s[...] += part

        @pl.when(j == nj - 1)
        def _():
            ff_ref[...] = acc_s[...].astype(BF16)
            x2_ref[...] = x1_ref[...] + g2_ref[...] * acc_s[...]

    tk = lambda n: pl.BlockSpec((t, n), lambda i, j: (i, 0))
    return _call(body, "out_mlp_fwd", (s // t, nj),
                 [tk(LRU_W), tk(GDN_W), tk(D_MODEL)] + _wo_specs(l) + [_vec(D_MODEL)] * 5
                 + [pl.BlockSpec((MLP_BLOCKS, 1, D_MODEL, fc), lambda i, j: (j, l, 0, 0)),
                    pl.BlockSpec((MLP_BLOCKS, 1, fc, D_MODEL), lambda i, j: (j, l, 0, 0))],
                 [tk(D_MODEL)] * 4,
                 [_sds((s, D_MODEL)), _sds((s, D_MODEL), BF16), _sds((s, D_MODEL), BF16), _sds((s, D_MODEL))],
                 scratch=[pltpu.VMEM((t, D_MODEL), BF16), pltpu.VMEM((t, D_MODEL), F32)])(
                     ol, og, x, wo, wo, g1, nw2, sc2, sh2, g2, wup, wdn)


def _mlp_bwd(dx2, x1, ff, nw2, sc2, sh2, g2, wup, wdn, l):
    s = x1.shape[0]
    t = _tile(s, 512)
    nj = wup.shape[0] // MLP_BLOCKS
    fc = wup.shape[3]

    def body(dx2_ref, x1_ref, ff_ref, nw_ref, sc_ref, sh_ref, g2_ref, wup_ref, wdn_ref,
             act_ref, dup_ref, h2_ref, dff_ref, dx1_ref, rows_ref, dh2_s):
        i, j = pl.program_id(0), pl.program_id(1)

        @pl.when((i == 0) & (j == 0))
        def _():
            rows_ref[...] = jnp.zeros_like(rows_ref)

        @pl.when(j == 0)
        def _():
            h2, _, _ = _modulated_norm(x1_ref[...], nw_ref[...], sc_ref[...], sh_ref[...])
            h2_ref[...] = h2.astype(BF16)
            dx2 = dx2_ref[...]
            dff_ref[...] = (dx2 * g2_ref[...]).astype(BF16)
            rows_ref[2:3, :] += jnp.sum(dx2 * ff_ref[...].astype(F32), axis=0, keepdims=True)
            dh2_s[...] = jnp.zeros_like(dh2_s)

        part = None
        for b in range(MLP_BLOCKS):
            cols = slice(b * fc, (b + 1) * fc)
            up = _bdot(h2_ref[...], wup_ref[b, 0])
            ru = jnp.maximum(up, 0.0)
            act_ref[:, cols] = (ru * ru).astype(BF16)
            dup = (_bdot(dff_ref[...], wdn_ref[b, 0], NT) * (2.0 * ru)).astype(BF16)
            dup_ref[:, cols] = dup
            back = _bdot(dup, wup_ref[b, 0], NT)
            part = back if part is None else part + back
        dh2_s[...] += part

        @pl.when(j == nj - 1)
        def _():
            xv = x1_ref[...]
            _, n, r = _modulated_norm(xv, nw_ref[...], sc_ref[...], sh_ref[...])
            dx, dsh, dsc, dnw = _modulated_norm_bwd(dh2_s[...], xv, n, r, nw_ref[...], sc_ref[...])
            dx1_ref[...] = dx2_ref[...] + dx
            rows_ref[0:1, :] += dsh
            rows_ref[1:2, :] += dsc
            rows_ref[3:4, :] += dnw

    tk = lambda n: pl.BlockSpec((t, n), lambda i, j: (i, 0))
    tj = pl.BlockSpec((t, MLP_BLOCKS * fc), lambda i, j: (i, j))
    return _call(body, "mlp_bwd", (s // t, nj),
                 [tk(D_MODEL)] * 3 + [_vec(D_MODEL)] * 4
                 + [pl.BlockSpec((MLP_BLOCKS, 1, D_MODEL, fc), lambda i, j: (j, l, 0, 0)),
                    pl.BlockSpec((MLP_BLOCKS, 1, fc, D_MODEL), lambda i, j: (j, l, 0, 0))],
                 [tj, tj, tk(D_MODEL), tk(D_MODEL), tk(D_MODEL), pl.BlockSpec((8, D_MODEL), lambda i, j: (0, 0))],
                 [_sds((s, D_FF), BF16), _sds((s, D_FF), BF16), _sds((s, D_MODEL), BF16),
                  _sds((s, D_MODEL), BF16), _sds((s, D_MODEL)), _sds((8, D_MODEL))],
                 scratch=[pltpu.VMEM((t, D_MODEL), F32)], vmem_mb=56)(dx2, x1, ff, nw2, sc2, sh2, g2, wup, wdn)


def _outproj_bwd(dx1, mix, g1, wo, l):
    s = dx1.shape[0]
    t = _tile(s, 512)

    def body(dx1_ref, mix_ref, g1_ref, wol_ref, wog_ref, dmix_ref, dol_ref, dog_ref, rows_ref):
        @pl.when(pl.program_id(0) == 0)
        def _():
            rows_ref[...] = jnp.zeros_like(rows_ref)

        dx1v = dx1_ref[...]
        rows_ref[0:1, :] += jnp.sum(dx1v * mix_ref[...].astype(F32), axis=0, keepdims=True)
        dmix = (dx1v * g1_ref[...]).astype(BF16)
        dmix_ref[...] = dmix
        dol_ref[...] = _bdot(dmix, _wo_half(wol_ref), NT)
        dog_ref[...] = _bdot(dmix, _wo_half(wog_ref), NT)

    return _call(body, "outproj_bwd", (s // t,),
                 [_tok(t, D_MODEL), _tok(t, D_MODEL), _vec(D_MODEL)] + _wo_specs(l),
                 [_tok(t, D_MODEL), _tok(t, LRU_W), _tok(t, GDN_W), pl.BlockSpec((8, D_MODEL), lambda i: (0, 0))],
                 [_sds((s, D_MODEL), BF16), _sds((s, LRU_W)), _sds((s, GDN_W)), _sds((8, D_MODEL))])(dx1, mix, g1, wo, wo)


def _tn_matmul(a, b, name, out=None, l=0, blocked=False, row_block=0):
    s, m = a.shape
    n = b.shape[1]
    ts, bm = _tile(s, 2048), _tile(m, 1024)
    bn = next(w for w in ((512,) if blocked else (1024, 512, 640, 384, 256, 128)) if n % w == 0)

    def body(a_ref, b_ref, *rest):
        o_ref = rest[-1]

        @pl.when(pl.program_id(2) == 0)
        def _():
            o_ref[...] = jnp.zeros_like(o_ref)

        acc = _bdot(a_ref[...], b_ref[...], TN)
        o_ref[...] += acc.reshape(o_ref.shape)

    in_specs = [pl.BlockSpec((ts, bm), lambda i, j, k: (k, i)), pl.BlockSpec((ts, bn), lambda i, j, k: (k, j))]
    grid = (m // bm, n // bn, s // ts)
    if out is None:
        return _call(body, name, grid, in_specs, pl.BlockSpec((bm, bn), lambda i, j, k: (i, j)), _sds((m, n)))(a, b)
    if blocked:
        out_spec = pl.BlockSpec((1, 1, bm, bn), lambda i, j, k: (l, j, i, 0))
    else:
        out_spec = pl.BlockSpec((1, bm, bn), lambda i, j, k: (l, i + row_block * (m // bm), j))
    return _call(body, name, grid, in_specs + [pl.BlockSpec(memory_space=pl.ANY)], out_spec,
                 _sds(out.shape), aliases={2: 0})(a, b, out)


def _final_fwd_bwd(x, target, fw):
    s = x.shape[0]
    t = _tile(s, 512)

    def body(x_ref, tg_ref, fw_ref, dx_ref, rows_ref):
        @pl.when(pl.program_id(0) == 0)
        def _():
            rows_ref[...] = jnp.zeros_like(rows_ref)

        xv = x_ref[...]
        fwv = fw_ref[...]
        r = lax.rsqrt(jnp.mean(xv * xv, axis=-1, keepdims=True) + EPS)
        err = xv * r * fwv - tg_ref[...]
        part = 0.5 * jnp.sum(jnp.mean(err * err, axis=-1, keepdims=True), axis=0, keepdims=True)
        rows_ref[1:2, :] += jnp.broadcast_to(part, (1, D_MODEL))
        dy = err * (1.0 / D_MODEL)
        rows_ref[0:1, :] += jnp.sum(dy * xv * r, axis=0, keepdims=True)
        dxn = dy * fwv
        dx_ref[...] = r * dxn - xv * (r * r * r) * jnp.mean(dxn * xv, axis=-1, keepdims=True)

    return _call(body, "final_fwd_bwd", (s // t,),
                 [_tok(t, D_MODEL), _tok(t, D_MODEL), _vec(D_MODEL)],
                 [_tok(t, D_MODEL), pl.BlockSpec((8, D_MODEL), lambda i: (0, 0))],
                 [_sds((s, D_MODEL)), _sds((8, D_MODEL))])(x, target, fw)


def _adamw(w, g, m, v):
    m = ADAM_B1 * m + (1.0 - ADAM_B1) * g
    v = ADAM_B2 * v + (1.0 - ADAM_B2) * (g * g)
    m_hat = m / (1.0 - ADAM_B1 ** ADAM_STEP)
    v_hat = v / (1.0 - ADAM_B2 ** ADAM_STEP)
    return -ADAM_LR * (m_hat / (jnp.sqrt(v_hat) + ADAM_EPS) + ADAM_WD * w), m, v


def _mod_local(c_all, wmod, bmod_cols):
    nl, _, cols = wmod.shape

    def body(c_ref, w_ref, b_ref, o_ref):
        cv = c_ref[...]
        o_ref[0] = _bdot(cv * jax.nn.sigmoid(cv), w_ref[0]) + b_ref[0]

    return _call(body, "mod_local", (nl,),
                 [_whole(c_all), pl.BlockSpec((1, D_MODEL, cols), lambda l: (l, 0, 0)),
                  pl.BlockSpec((1, 1, cols), lambda l: (l, 0, 0))],
                 pl.BlockSpec((1, N_DEV, cols), lambda l: (l, 0, 0)), _sds((nl, N_DEV, cols)))(c_all, wmod, bmod_cols)


def _wmod_update(c_all, dmod_cols, w, m, v):
    nl, _, cols = w.shape

    def body(c_ref, d_ref, w_ref, m_ref, v_ref, g_ref, dl_ref, nm_ref, nv_ref):
        cv = c_ref[...]
        g = _bdot(cv * jax.nn.sigmoid(cv), d_ref[0], TN)
        g_ref[0] = g
        dl_ref[0], nm_ref[0], nv_ref[0] = _adamw(w_ref[0], g, m_ref[0], v_ref[0])

    wspec = pl.BlockSpec((1, D_MODEL, cols), lambda l: (l, 0, 0))
    return _call(body, "wmod_update", (nl,),
                 [_whole(c_all), pl.BlockSpec((1, N_DEV, cols), lambda l: (l, 0, 0)), wspec, wspec, wspec],
                 [wspec] * 4, [_sds(w.shape)] * 4)(c_all, dmod_cols, w, m, v)


def _sum_devices(gathered):
    _, r, _ = gathered.shape

    def body(g_ref, o_ref):
        acc = g_ref[0]
        for d in range(1, N_DEV):
            acc = acc + g_ref[d]
        o_ref[...] = acc

    return _call(body, "sum_devices", (1,), [_whole(gathered)], pl.BlockSpec((r, LANES), lambda i: (0, 0)),
                 _sds((r, LANES)))(gathered)


def _adam_flat(w, g, m, v):
    r = w.shape[0]

    def body(w_ref, g_ref, m_ref, v_ref, dl_ref, nm_ref, nv_ref):
        dl_ref[...], nm_ref[...], nv_ref[...] = _adamw(w_ref[...], g_ref[...], m_ref[...], v_ref[...])

    spec = pl.BlockSpec((r, LANES), lambda i: (0, 0))
    return _call(body, "adam_small", (1,), [spec] * 4, [spec] * 3, [_sds((r, LANES))] * 3)(w, g, m, v)


def _pair_add(x, p, others):
    _, r, c = x.shape
    tr = _tile(r, 128 if c > 512 else 256)

    def body(others_ref, x_ref, p_ref, o_ref):
        o_ref[...] = (x_ref[...] + p_ref[...]).astype(BF16)

    return _call(body, "pair_add", (3, r // tr),
                 [pl.BlockSpec((1, tr, c), lambda q, i, others_ref: (others_ref[q], i, 0)),
                  pl.BlockSpec((1, tr, c), lambda q, i, others_ref: (others_ref[3 + q], i, 0))],
                 pl.BlockSpec((1, tr, c), lambda q, i, others_ref: (q, i, 0)), _sds((3, r, c), BF16),
                 prefetch=1)(others, x, p)


def _reduce_adam(x, p, q, place, w, m, v, l, outs):
    _, r, c = x.shape
    tr = _tile(r, 128 if c > 512 else 256)

    def body(place_ref, x_ref, p_ref, q_ref, w_ref, m_ref, v_ref, *rest):
        g_ref, dl_ref, nm_ref, nv_ref = rest[-4:]
        g = (((x_ref[0] + p_ref[0]) + q_ref[0].astype(F32)) + q_ref[1].astype(F32)) + q_ref[2].astype(F32)
        g_ref[0] = g
        dl_ref[0], nm_ref[0], nv_ref[0] = _adamw(w_ref[0], g, m_ref[0], v_ref[0])

    flat = pl.BlockSpec((1, tr, c), lambda i, place_ref: (l, i, 0))
    through = pl.BlockSpec(memory_space=pl.ANY)
    return _call(body, "reduce_adam", (r // tr,),
                 [pl.BlockSpec((1, tr, c), lambda i, place_ref: (place_ref[0], i, 0)),
                  pl.BlockSpec((1, tr, c), lambda i, place_ref: (place_ref[1], i, 0)),
                  pl.BlockSpec((3, tr, c), lambda i, place_ref: (0, i, 0)), flat, flat, flat] + [through] * 4,
                 [flat] * 4, [_sds(w.shape)] * 4, prefetch=1, aliases={7 + k: k for k in range(4)})(
                     place, x, p, q, w, m, v, *outs)


def _place():
    return lax.axis_index("x"), lax.axis_index("y"), lax.axis_index("c")


def _all_gather(xs, name, space):
    n = len(xs)

    def body(*refs):
        x_refs, o_refs = refs[:n], refs[n:2 * n]
        send_sems, recv_sems, local_sems = refs[2 * n:]
        x, y, c = _place()
        me, sibling = (x, y, c), (x, y, 1 - c)
        chips = [(1 - x, y), (x, 1 - y), (1 - x, 1 - y)]

        def blk(a, p):
            return o_refs[a].at[4 * p[0] + 2 * p[1] + p[2]]

        def copy(a, k, block, to, src=None):
            return pltpu.make_async_remote_copy(
                src_ref=blk(a, block) if src is None else src, dst_ref=blk(a, block),
                send_sem=send_sems.at[a, k], recv_sem=recv_sems.at[a, k], device_id=to, device_id_type=MESH)

        mine = [pltpu.make_async_copy(x_refs[a], blk(a, me), local_sems.at[a]) for a in range(n)]
        for cp in mine:
            cp.start()
        first = []
        for a in range(n):
            first.append(copy(a, 0, me, sibling, src=x_refs[a]))
            first += [copy(a, 1 + j, me, (*chip, c), src=x_refs[a]) for j, chip in enumerate(chips)]
        for cp in first:
            cp.start()
        passed = []
        for j, chip in enumerate(chips):
            for a in range(n):
                copy(a, 1 + j, (*chip, c), me).wait_recv()
                cp = copy(a, 4 + j, (*chip, c), sibling)
                cp.start()
                passed.append(cp)
        for a in range(n):
            copy(a, 0, sibling, me).wait_recv()
        for j, chip in enumerate(chips):
            for a in range(n):
                copy(a, 4 + j, (*chip, 1 - c), me).wait_recv()
        for cp in first + passed:
            cp.wait_send()
        for cp in mine:
            cp.wait()

    spec = pl.BlockSpec(memory_space=space)
    return pl.pallas_call(
        body, name=name, out_shape=[_sds((N_DEV,) + a.shape, a.dtype) for a in xs],
        in_specs=[spec] * n, out_specs=[spec] * n,
        scratch_shapes=[pltpu.SemaphoreType.DMA((n, 7)), pltpu.SemaphoreType.DMA((n, 7)),
                        pltpu.SemaphoreType.DMA((n,))])(*xs)


_HBM_SPEC = pl.BlockSpec(memory_space=pltpu.HBM)
_SEM_SPEC = pl.BlockSpec(memory_space=pltpu.SEMAPHORE)
_EFFECT = pltpu.SideEffectType.DATAFLOW_SIDE_EFFECTING


def _descriptors(plan, src_refs, land_refs, send_sems, recv_sems, which=None):
    return [pltpu.make_async_remote_copy(src_ref=s, dst_ref=d, send_sem=send_sems.at[k], recv_sem=recv_sems.at[k],
                                         device_id=dev, device_id_type=MESH)
            for k, (s, d, dev) in enumerate(plan(src_refs, land_refs)) if which is None or k in which]


def _split_start(name, plan, n, srcs, lands, after):
    ns, nb = len(srcs), len(srcs) + len(lands)
    after = list(after) if isinstance(after, (list, tuple)) else [after]
    sems = nb + len(after)

    def body(*refs):
        for cp in _descriptors(plan, refs[:ns], refs[ns:nb], refs[sems], refs[sems + 1]):
            cp.start()
        refs[-1][...] = jnp.zeros_like(refs[-1])

    bufs = [pltpu.with_memory_space_constraint(a, pltpu.HBM) for a in list(srcs) + list(lands)]
    outs = pl.pallas_call(
        body, name=name,
        out_shape=(pltpu.SemaphoreType.DMA((n,)), pltpu.SemaphoreType.DMA((n,)))
        + tuple(pltpu.HBM(a.shape, a.dtype) for a in bufs) + (_sds((8, LANES)),),
        in_specs=[_HBM_SPEC] * nb + [pl.BlockSpec(memory_space=pl.ANY)] * len(after),
        out_specs=(_SEM_SPEC, _SEM_SPEC) + (_HBM_SPEC,) * nb + (pl.BlockSpec(memory_space=pltpu.VMEM),),
        input_output_aliases={i: 2 + i for i in range(nb)},
        compiler_params=pltpu.CompilerParams(has_side_effects=_EFFECT))(*bufs, *after)
    return dict(send=outs[0], recv=outs[1], srcs=list(outs[2:2 + ns]), lands=list(outs[2 + ns:2 + nb]), token=outs[-1])


def _split_wait(name, plan, flight, which, after):
    srcs, lands = flight["srcs"], flight["lands"]
    ns, nb = len(srcs), len(srcs) + len(lands)

    def body(*refs):
        for cp in _descriptors(plan, refs[:ns], refs[ns:nb], refs[nb], refs[nb + 1], set(which)):
            cp.wait_send()
            cp.wait_recv()

    outs = pl.pallas_call(
        body, name=name, out_shape=tuple(pltpu.HBM(a.shape, a.dtype) for a in srcs + lands),
        in_specs=[_HBM_SPEC] * nb + [_SEM_SPEC, _SEM_SPEC, pl.BlockSpec(memory_space=pl.ANY)],
        out_specs=(_HBM_SPEC,) * nb, input_output_aliases={i: i for i in range(nb)},
        compiler_params=pltpu.CompilerParams(has_side_effects=_EFFECT))(*srcs, *lands, flight["send"], flight["recv"],
                                                                       after)
    return dict(flight, srcs=list(outs[:ns]), lands=list(outs[ns:nb]))


GATHER_PEERS = N_DEV - 1


def _gather_plan(items):
    def plan(src_refs, land_refs):
        x, y, c = _place()
        me = 4 * x + 2 * y + c
        out = []
        for a, l in items:
            for r in range(1, N_DEV):
                peer = (1 - x if r & 4 else x, 1 - y if r & 2 else y, 1 - c if r & 1 else c)
                out.append((src_refs[a].at[l], land_refs[a].at[me, l], peer))
        return out

    return plan


def _pair_plan(narr):
    def plan(src_refs, land_refs):
        x, y, c = _place()
        return [(src_refs[a].at[2 * q + (1 - c)], land_refs[a].at[q], (x, y, 1 - c))
                for a in range(narr) for q in range(4)]

    return plan


def _chip_plan(narr):
    def plan(src_refs, land_refs):
        x, y, c = _place()
        chips = [(1 - x, y), (x, 1 - y), (1 - x, 1 - y)]
        return [(src_refs[a].at[r], land_refs[a].at[r], (*chip, c)) for a in range(narr) for r, chip in enumerate(chips)]

    return plan


class _GradReducer:
    def __init__(self, tag, names, w, mom, var, place, others):
        self.tag, self.names, self.w, self.mom, self.var, self.place = tag, names, w, mom, var, place
        self.others = others
        self.outs = {k: [lax.empty(w[k].shape, F32) for _ in range(4)] for k in names}
        self.n = len(names)

    def start(self, l, grads):
        self.l, self.xs = l, [grads[k] for k in self.names]
        lands = [lax.empty((4,) + a.shape[1:], F32) for a in self.xs]
        self.pair = _split_start(f"{self.tag}_pair_start{l}", _pair_plan(self.n), 4 * self.n, self.xs, lands, ())
        return self.pair["token"][0, 0]

    def middle(self, after):
        self.pair = _split_wait(f"{self.tag}_pair_wait{self.l}", _pair_plan(self.n), self.pair, range(4 * self.n),
                                after)
        self.xs, self.ps = self.pair["srcs"], self.pair["lands"]
        ys = [_pair_add(x, p, self.others) for x, p in zip(self.xs, self.ps)]
        lands = [lax.empty((3,) + a.shape[1:], BF16) for a in ys]
        self.chip = _split_start(f"{self.tag}_chip_start{self.l}", _chip_plan(self.n), 3 * self.n, ys, lands, ())
        return self.chip["token"][0, 0]

    def finish(self, after):
        chip = _split_wait(f"{self.tag}_chip_wait{self.l}", _chip_plan(self.n), self.chip, range(3 * self.n), after)
        for k, x, p, q in zip(self.names, self.xs, self.ps, chip["lands"]):
            self.outs[k] = _reduce_adam(x, p, q, self.place, self.w[k], self.mom[k], self.var[k], self.l, self.outs[k])


def _size(shape):
    size = 1
    for d in shape:
        size *= d
    return size


def _slab_rows(shape):
    return -(-_size(shape) // (8 * LANES)) * 8


def _pack(arrs):
    parts = []
    for a in arrs:
        flat = a.reshape(-1).astype(F32)
        parts.append(jnp.pad(flat, (0, _slab_rows(a.shape) * LANES - flat.shape[0])).reshape(-1, LANES))
    return jnp.concatenate(parts, axis=0)


def _unpack(slab, shapes):
    out, off = [], 0
    for shp in shapes:
        rows = _slab_rows(shp)
        out.append(slab[off:off + rows].reshape(-1)[:_size(shp)].reshape(shp))
        off += rows
    return out


def _dense_blocks(w):
    eye = jnp.eye(LRU_BLOCKS, dtype=w.dtype)
    return (eye[:, None, :, None] * w[:, :, None, :]).reshape(LRU_W, LRU_W)


def _diag_blocks(dense):
    on_diagonal = jnp.eye(LRU_BLOCKS, dtype=bool)[:, None, :, None]
    return jnp.sum(jnp.where(on_diagonal, dense.reshape(LRU_BLOCKS, LRU_BLOCK, LRU_BLOCKS, LRU_BLOCK), 0.0), axis=2)


def _alpha_lanes(v):
    return jnp.zeros((1, BA_PAD), F32).at[0, HEADS:2 * HEADS].set(v)


def _local_step(x, target, mod, p, fetch, reducers=None):
    nl = mod.shape[0]
    row = lambda v: v.reshape(1, -1)
    masks = _gdn_masks()
    saved = []
    xc = x
    for l in range(nl):
        win, wba, lin = fetch(l, "in", xc)
        mv = [row(mod[l, k * D_MODEL:(k + 1) * D_MODEL]) for k in range(N_MOD)]
        sh1, sc1, g1, sh2, sc2, g2 = mv
        nw1, nw2 = row(p["norm_mix_w"][l]), row(p["norm_mlp_w"][l])
        wa, wx = _dense_blocks(p["lru_gate_a_w"][l]).astype(BF16), _dense_blocks(p["lru_gate_x_w"][l]).astype(BF16)
        lru_args = (p["lru_conv_w"][l], row(p["lru_conv_b"][l]), wa, wx, row(p["lru_gate_a_b"][l]),
                    row(p["lru_gate_x_b"][l]), row(p["lru_lambda"][l]), row(p["lru_norm_w"][l]))
        gdn_args = (p["gdn_conv_w"][l], _alpha_lanes(p["gdn_a_log"][l]), _alpha_lanes(p["gdn_dt_bias"][l]), masks)
        gnw = row(p["gdn_norm_w"][l])
        proj, ba = _inproj_fwd(xc, nw1, sc1, sh1, win, wba, lin)
        ol, hs = _lru_fwd(proj, *lru_args)
        *prep, tinv = _gdn_prep_fwd(proj, ba, *gdn_args)
        og, st = _gdn_scan_fwd(prep, proj, gnw)
        wo, wup, wdn = fetch(l, "rest", og)
        x1, mix, ff, x2 = _out_mlp_fwd(ol, og, xc, wo, g1, nw2, sc2, sh2, g2, wup, wdn, l)
        saved.append(dict(x=xc, mv=mv, nw1=nw1, nw2=nw2, lru_args=lru_args, gdn_args=gdn_args, gnw=gnw, proj=proj,
                          ba=ba, ol=ol, hs=hs, prep=prep, tinv=tinv, og=og, st=st, x1=x1, mix=mix, ff=ff,
                          win=win, wba=wba, lin=lin))
        xc = x2

    dx, frows = _final_fwd_bwd(xc, target, row(p["final_norm_w"]))
    loss_part = frows[1, 0]
    small = {k: [None] * nl for k in ("norm_mix_w", "norm_mlp_w", "lru_conv_w", "lru_conv_b", "lru_gate_a_w",
                                      "lru_gate_a_b", "lru_gate_x_w", "lru_gate_x_b", "lru_lambda", "lru_norm_w",
                                      "gdn_conv_w", "gdn_a_log", "gdn_dt_bias", "gdn_norm_w")}
    fc = D_FF // N_DEV
    big = [None] * nl
    dmod = [None] * nl
    mlp_red, mix_red = reducers or (None, None)
    busy = False
    for l in reversed(range(nl)):
        sv = saved[l]
        sh1, sc1, g1, sh2, sc2, g2 = sv["mv"]
        gnw = sv["gnw"]
        if busy:
            g2 = g2 + started
        act, dup, h2b, dffb, dx1, rows2 = _mlp_bwd(dx, sv["x1"], sv["ff"], sv["nw2"], sc2, sh2, g2, wup, wdn, l)
        if busy:
            g1 = g1 + mix_red.middle(dx1)
        g_up = _tn_matmul(h2b, dup, "grad_w_up", out=lax.empty((1, N_DEV, D_MODEL, fc), F32), blocked=True)[0]
        g_down = _tn_matmul(act, dffb, "grad_w_down", out=lax.empty((1, D_FF, D_MODEL), F32))
        g_down = g_down.reshape(N_DEV, fc, D_MODEL)
        if mlp_red is not None:
            g1 = g1 + mlp_red.start(l, dict(w_up=g_up, w_down=g_down))
        dmix, dol, dog, rows1 = _outproj_bwd(dx1, sv["mix"], g1, wo, l)
        g_out = _tn_matmul(sv["ol"], dmix, "grad_w_out_lru", out=lax.empty((1, D_MODEL, D_MODEL), F32))
        g_out = _tn_matmul(sv["og"], dmix, "grad_w_out_gdn", out=g_out, row_block=1)
        dpl, dwa, dwx, lrows = _lru_bwd(dol, sv["proj"], sv["hs"], *sv["lru_args"])
        if mlp_red is not None:
            gnw = gnw + mlp_red.middle(dpl)
        *cts, dpz, gnrow = _gdn_scan_bwd(dog, sv["prep"], sv["st"], sv["proj"], gnw)
        dpq, dba, dcw, dpar = _gdn_prep_bwd(cts, sv["tinv"], sv["proj"], sv["ba"], *sv["gdn_args"])
        dx, hb, rows0 = _inproj_bwd(dpl, dpq, dpz, dba, sv["x"], dx1, sv["nw1"], sc1, sh1, sv["win"], sv["wba"],
                                    sv["lin"])
        if busy:
            mix_red.finish(dx)
        if mlp_red is not None:
            mlp_red.finish(dx)
        dproj = jnp.concatenate([dpl, dpq, dpz, dba], axis=1)
        g_in = jnp.transpose(_tn_matmul(hb, dproj, "grad_w_in")[:, :IN_COLS].reshape(
            D_MODEL, N_DEV, IN_COLS // N_DEV), (1, 0, 2))
        big[l] = dict(w_in=g_in, w_out=g_out.reshape(N_DEV, D_MODEL // N_DEV, D_MODEL), w_up=g_up,
                      w_down=g_down)
        if mix_red is not None:
            started, busy = mix_red.start(l, big[l]), True
        dmod[l] = jnp.concatenate([rows0[0], rows0[1], rows1[0], rows2[0], rows2[1], rows2[2]])
        small["norm_mix_w"][l], small["norm_mlp_w"][l] = rows0[2], rows2[3]
        small["lru_conv_w"][l], small["lru_conv_b"][l] = lrows[8:8 + CONV_K], lrows[0]
        small["lru_gate_a_w"][l], small["lru_gate_x_w"][l] = _diag_blocks(dwa), _diag_blocks(dwx)
        small["lru_gate_a_b"][l], small["lru_gate_x_b"][l] = lrows[1], lrows[2]
        small["lru_lambda"][l], small["lru_norm_w"][l] = lrows[3], lrows[4]
        small["gdn_conv_w"][l] = dcw
        small["gdn_a_log"][l], small["gdn_dt_bias"][l] = dpar[0, HEADS:2 * HEADS], dpar[1, HEADS:2 * HEADS]
        small["gdn_norm_w"][l] = gnrow[0]
    small = {k: jnp.stack(v) for k, v in small.items()}
    small["final_norm_w"] = frows[0]
    return loss_part, dx, big, small, jnp.stack(dmod)


SMALL_REPLICATED = ("norm_mix_w", "norm_mlp_w", "b_mod", "lru_conv_b", "lru_gate_a_w", "lru_gate_a_b", "lru_gate_x_w",
                    "lru_gate_x_b", "lru_lambda", "lru_norm_w", "gdn_a_log", "gdn_dt_bias", "gdn_norm_w",
                    "final_norm_w")
SMALL_SHARDED = ("lru_conv_w", "gdn_conv_w")
WEIGHT_ORDER = ("norm_mix_w", "norm_mlp_w", "w_mod", "b_mod", "w_in", "lru_conv_w", "lru_conv_b", "lru_gate_a_w",
                "lru_gate_a_b", "lru_gate_x_w", "lru_gate_x_b", "lru_lambda", "lru_norm_w", "gdn_conv_w", "gdn_a_log",
                "gdn_dt_bias", "gdn_norm_w", "w_out", "w_up", "w_down", "final_norm_w")


def kernel(x, c, norm_mix_w, norm_mlp_w, w_mod, b_mod, w_in, lru_conv_w, lru_conv_b, lru_gate_a_w, lru_gate_a_b, lru_gate_x_w, lru_gate_x_b, lru_lambda, lru_norm_w, gdn_conv_w, gdn_a_log, gdn_dt_bias, gdn_norm_w, w_out, w_up, w_down, final_norm_w, loss_target, m_norm_mix_w, m_norm_mlp_w, m_w_mod, m_b_mod, m_w_in, m_lru_conv_w, m_lru_conv_b, m_lru_gate_a_w, m_lru_gate_a_b, m_lru_gate_x_w, m_lru_gate_x_b, m_lru_lambda, m_lru_norm_w, m_gdn_conv_w, m_gdn_a_log, m_gdn_dt_bias, m_gdn_norm_w, m_w_out, m_w_up, m_w_down, m_final_norm_w, v_norm_mix_w, v_norm_mlp_w, v_w_mod, v_b_mod, v_w_in, v_lru_conv_w, v_lru_conv_b, v_lru_gate_a_w, v_lru_gate_a_b, v_lru_gate_x_w, v_lru_gate_x_b, v_lru_lambda, v_lru_norm_w, v_gdn_conv_w, v_gdn_a_log, v_gdn_dt_bias, v_gdn_norm_w, v_w_out, v_w_up, v_w_down, v_final_norm_w):
    args = dict(locals())
    w = {k: args[k] for k in WEIGHT_ORDER}
    mom = {k: args["m_" + k] for k in WEIGHT_ORDER}
    var = {k: args["v_" + k] for k in WEIGHT_ORDER}
    nl = w_in.shape[0]
    px, py, pc = _place()
    me = 4 * px + 2 * py + pc
    other_chips = [2 * (1 - px) + py, 2 * px + (1 - py), 2 * (1 - px) + (1 - py)]
    others = jnp.stack([2 * q + pc for q in other_chips] + other_chips).astype(jnp.int32)

    shapes0 = [c.shape, lru_conv_w.shape, gdn_conv_w.shape]
    (g0,) = _all_gather([_pack([c, lru_conv_w, gdn_conv_w])], "gather_cond", pltpu.VMEM)
    per_dev = [_unpack(g0[d], shapes0) for d in range(N_DEV)]
    c_all = jnp.concatenate([pd[0] for pd in per_dev], axis=0)
    lru_conv_full = jnp.concatenate([pd[1] for pd in per_dev], axis=-1)
    gdn_conv_full = jnp.concatenate([pd[2] for pd in per_dev], axis=-1)

    cols = w_mod.shape[2]
    bmod_cols = lax.dynamic_slice_in_dim(b_mod, me * cols, cols, axis=1).reshape(nl, 1, cols)
    mod_cols = _mod_local(c_all, w_mod, bmod_cols)
    (g1,) = _all_gather([mod_cols.reshape(nl * N_DEV, cols)], "gather_mod", pltpu.VMEM)
    g1 = g1.reshape(N_DEV, nl, N_DEV, cols)
    mod = jnp.transpose(lax.dynamic_index_in_dim(g1, me, axis=2, keepdims=False), (1, 0, 2)).reshape(nl, N_DEV * cols)

    shards = [a.astype(BF16) for a in (w_in, w_out, w_up, w_down)]
    (first_in,) = _all_gather([shards[0][:1]], "gather_w_in_first", pl.ANY)
    items = [(a, 0) for a in (1, 2, 3)] + [(a, l) for l in range(1, nl) for a in range(4)]
    plan = _gather_plan(items)
    lands = [lax.dynamic_update_slice_in_dim(lax.empty((N_DEV,) + a.shape, BF16), a[None], me, axis=0) for a in shards]
    flight = [_split_start("gather_weights_start", plan, len(items) * GATHER_PEERS, shards, lands, [first_in, mod])]
    mod = mod + flight[0]["token"][0, 0]

    def fetch(l, what, after):
        wanted = [k for k, (a, ll) in enumerate(items) if ll == l and (a == 0) == (what == "in")]
        if wanted:
            flight[0] = _split_wait(f"gather_weights_wait_{what}{l}", plan, flight[0],
                                    [k * GATHER_PEERS + r for k in wanted for r in range(GATHER_PEERS)], after)
        gin, gout, gup, gdn = flight[0]["lands"]
        if what == "rest":
            return gout, gup, gdn
        gin = first_in[:, 0] if l == 0 else gin[:, l]
        win = jnp.transpose(gin, (1, 0, 2)).reshape(1, D_MODEL, IN_COLS)
        wba = jnp.pad(win[:, :, IN_MAIN:], ((0, 0), (0, 0), (0, BA_PAD - (IN_COLS - IN_MAIN))))
        return win, wba, 0

    p = dict(w)
    p["lru_conv_w"], p["gdn_conv_w"] = lru_conv_full, gdn_conv_full

    place = jnp.stack([me, 2 * px + py]).astype(jnp.int32)
    reducers = (_GradReducer("mlp_grad", ("w_up", "w_down"), w, mom, var, place, others),
                _GradReducer("mix_grad", ("w_in", "w_out"), w, mom, var, place, others))
    loss_part, grad_x, _, small, dmod = _local_step(x[0], loss_target[0], mod, p, fetch, reducers)
    loss = lax.psum(loss_part, MESH_AXES)

    small_names = sorted(small)
    slab = _pack([dmod] + [small[k] for k in small_names])
    (gs,) = _all_gather([slab], "gather_small_grads", pltpu.VMEM)
    chips_started = reducers[1].middle(gs)
    dmod_all = gs[:, :_slab_rows(dmod.shape)].reshape(N_DEV, nl, N_MOD * D_MODEL)
    summed = _unpack(_sum_devices(gs) + chips_started, [dmod.shape] + [small[k].shape for k in small_names])
    grads = dict(zip(small_names, summed[1:]))
    grads["b_mod"] = summed[0]
    for k, width in (("lru_conv_w", LRU_W // N_DEV), ("gdn_conv_w", 3 * GDN_W // N_DEV)):
        grads[k] = lax.dynamic_slice_in_dim(grads[k], me * width, width, axis=2)
    names = SMALL_REPLICATED + SMALL_SHARDED
    shapes = [w[k].shape for k in names]
    dl, nm, nv = _adam_flat(_pack([w[k] for k in names]), _pack([grads[k] for k in names]),
                            _pack([mom[k] for k in names]), _pack([var[k] for k in names]))
    delta = dict(zip(names, _unpack(dl, shapes)))
    new_m = dict(zip(names, _unpack(nm, shapes)))
    new_v = dict(zip(names, _unpack(nv, shapes)))

    dmod_cols = jnp.transpose(lax.dynamic_slice_in_dim(dmod_all, me * cols, cols, axis=2), (1, 0, 2))
    grads["w_mod"], delta["w_mod"], new_m["w_mod"], new_v["w_mod"] = _wmod_update(
        c_all, dmod_cols, w_mod, m_w_mod, v_w_mod)

    reducers[1].finish(new_v["w_mod"])
    for red in reducers:
        for k in red.names:
            grads[k], delta[k], new_m[k], new_v[k] = red.outs[k]

    return (loss, grad_x[None], *[grads[k] for k in WEIGHT_ORDER], *[delta[k] for k in WEIGHT_ORDER],
            *[new_m[k] for k in WEIGHT_ORDER], *[new_v[k] for k in WEIGHT_ORDER])
```

```python
import functools

import jax
import jax.numpy as jnp
from jax import lax
from jax.experimental import pallas as pl
from jax.experimental.pallas import tpu as pltpu

F32 = jnp.float32
BF16 = jnp.bfloat16

D_MODEL = 1024
LRU_W = 512
LRU_BLOCKS = 8
LRU_BLOCK = 64
LRU_C = 8.0
GDN_W = 512
HEADS = 4
HEAD_DIM = 128
CHUNK = 64
STACK = HEADS * CHUNK
CONV_K = 4
D_FF = 4096
N_MOD = 6
IN_COLS = 3080
IN_MAIN = 3072
BA_PAD = 128
EPS = 1e-6
N_DEV = 8
HALO = 8
MLP_BLOCKS = 4
PREP_CHUNKS = 4
SCAN_CHUNKS = 8
LANES = 128
ADAM_LR, ADAM_B1, ADAM_B2, ADAM_EPS, ADAM_WD, ADAM_STEP = 0.001, 0.9, 0.999, 1e-08, 0.01, 10
MESH_AXES = ("x", "y", "c")
MESH = pl.DeviceIdType.MESH

NN = (((1,), (0,)), ((), ()))
NT = (((1,), (1,)), ((), ()))
TN = (((0,), (0,)), ((), ()))


def _bdot(a, b, dims=NN):
    return lax.dot_general(a.astype(BF16), b.astype(BF16), dims, preferred_element_type=F32)


def _sdot(a, b, dims=NN):
    ah, bh = a.astype(BF16), b.astype(BF16)
    al, bl = (a - ah.astype(F32)).astype(BF16), (b - bh.astype(F32)).astype(BF16)
    return _bdot(ah, bh, dims) + (_bdot(al, bh, dims) + _bdot(ah, bl, dims))


def _hdot(a, b, dims=NN):
    return lax.dot_general(a, b, dims, precision=lax.Precision.HIGHEST, preferred_element_type=F32)


def _sds(shape, dtype=F32):
    return jax.ShapeDtypeStruct(tuple(shape), dtype)


def _tile(n, t):
    return min(n, t)


def _call(body, name, grid, in_specs, out_specs, out_shape, scratch=(), vmem_mb=48, prefetch=0, aliases=None):
    params = pltpu.CompilerParams(dimension_semantics=("arbitrary",) * len(grid), vmem_limit_bytes=vmem_mb * 2**20)
    if prefetch:
        spec = pltpu.PrefetchScalarGridSpec(num_scalar_prefetch=prefetch, grid=grid, in_specs=in_specs,
                                            out_specs=out_specs, scratch_shapes=list(scratch))
        return pl.pallas_call(body, name=name, grid_spec=spec, out_shape=out_shape, compiler_params=params,
                              input_output_aliases=aliases or {})
    return pl.pallas_call(body, name=name, grid=grid, in_specs=in_specs, out_specs=out_specs, out_shape=out_shape,
                          scratch_shapes=list(scratch), compiler_params=params, input_output_aliases=aliases or {})


def _tok(t, n, col=0):
    return pl.BlockSpec((t, n), lambda i, *_: (i, col))


def _vec(n):
    return pl.BlockSpec((1, n), lambda *_: (0, 0))


def _whole(a):
    nd = a.ndim
    return pl.BlockSpec(a.shape, lambda *_: (0,) * nd)


def _layer(l, *dims):
    return pl.BlockSpec((1,) + dims, lambda *_: (l,) + (0,) * len(dims))


def _gelu(y):
    c0, c1 = 0.7978845608028654, 0.044715
    return 0.5 * y * (1.0 + jnp.tanh(c0 * (y + c1 * y * y * y)))


def _gelu_grad(y):
    c0, c1 = 0.7978845608028654, 0.044715
    t = jnp.tanh(c0 * (y + c1 * y * y * y))
    return 0.5 * (1.0 + t) + 0.5 * y * (1.0 - t * t) * c0 * (1.0 + 3.0 * c1 * y * y)


def _softplus(v):
    return jnp.maximum(v, 0.0) + jnp.log(1.0 + jnp.exp(-jnp.where(v > 0, v, -v)))


@functools.partial(jax.custom_vjp, nondiff_argnums=(1,))
def _roll_rows(v, s):
    s = s % v.shape[0]
    return pltpu.roll(v, s, axis=0) if s else v


def _roll_rows_fwd(v, s):
    return _roll_rows(v, s), None


def _roll_rows_bwd(s, _, g):
    return (_roll_rows(g, -s),)


_roll_rows.defvjp(_roll_rows_fwd, _roll_rows_bwd)


@jax.custom_vjp
def _drop_halo(v):
    return v[HALO:]


def _drop_halo_fwd(v):
    return v[HALO:], None


def _drop_halo_bwd(_, g):
    return (jnp.concatenate([jnp.zeros((HALO, g.shape[1]), g.dtype), g], axis=0),)


_drop_halo.defvjp(_drop_halo_fwd, _drop_halo_bwd)


@functools.partial(jax.custom_vjp, nondiff_argnums=(1, 2))
def _split(v, n, axis):
    w = v.shape[axis] // n
    return tuple(lax.slice_in_dim(v, k * w, (k + 1) * w, axis=axis) for k in range(n))


def _split_fwd(v, n, axis):
    return _split(v, n, axis), None


def _split_bwd(n, axis, _, gs):
    return (jnp.concatenate(list(gs), axis=axis),)


_split.defvjp(_split_fwd, _split_bwd)


def _conv_taps(xw):
    return [_drop_halo(_roll_rows(xw, CONV_K - 1 - k)) for k in range(CONV_K)]


def _modulated_norm(xv, nw, sc, sh):
    r = lax.rsqrt(jnp.mean(xv * xv, axis=-1, keepdims=True) + EPS)
    n = xv * r * nw
    return n * (1.0 + sc) + sh, n, r


def _modulated_norm_bwd(dh, xv, n, r, nw, sc):
    dn = dh * (1.0 + sc)
    dxn = dn * nw
    dx = r * dxn - xv * (r * r * r) * jnp.mean(dxn * xv, axis=-1, keepdims=True)
    return (dx, jnp.sum(dh, axis=0, keepdims=True), jnp.sum(dh * n, axis=0, keepdims=True),
            jnp.sum(dn * xv * r, axis=0, keepdims=True))


def _inproj_fwd(x, nw, sc, sh, win, wba, l):
    s = x.shape[0]
    t = _tile(s, 512)

    def body(x_ref, nw_ref, sc_ref, sh_ref, win_ref, wba_ref, proj_ref, ba_ref):
        h, _, _ = _modulated_norm(x_ref[...], nw_ref[...], sc_ref[...], sh_ref[...])
        hb = h.astype(BF16)
        proj_ref[...] = _bdot(hb, win_ref[0])
        ba_ref[...] = _bdot(hb, wba_ref[0])

    return _call(body, "inproj_fwd", (s // t,),
                 [_tok(t, D_MODEL), _vec(D_MODEL), _vec(D_MODEL), _vec(D_MODEL), _layer(l, D_MODEL, IN_MAIN),
                  _layer(l, D_MODEL, BA_PAD)],
                 [_tok(t, IN_MAIN), _tok(t, BA_PAD)],
                 [_sds((s, IN_MAIN)), _sds((s, BA_PAD))])(x, nw, sc, sh, win, wba)


def _inproj_bwd(dpl, dpq, dpz, dba, x, dx1, nw, sc, sh, win, wba, l):
    s = x.shape[0]
    t = _tile(s, 512)

    def body(dpl_ref, dpq_ref, dpz_ref, dba_ref, x_ref, dx1_ref, nw_ref, sc_ref, sh_ref, win_ref, wba_ref,
             dx_ref, hb_ref, acc_ref):
        @pl.when(pl.program_id(0) == 0)
        def _():
            acc_ref[...] = jnp.zeros_like(acc_ref)

        dh = (_bdot(dpl_ref[...], win_ref[0, :, 0:2 * LRU_W], NT)
              + _bdot(dpq_ref[...], win_ref[0, :, 2 * LRU_W:2 * LRU_W + 3 * GDN_W], NT)
              + _bdot(dpz_ref[...], win_ref[0, :, 2 * LRU_W + 3 * GDN_W:IN_MAIN], NT)
              + _bdot(dba_ref[...], wba_ref[0], NT))
        xv = x_ref[...]
        h, n, r = _modulated_norm(xv, nw_ref[...], sc_ref[...], sh_ref[...])
        hb_ref[...] = h.astype(BF16)
        dx, dsh, dsc, dnw = _modulated_norm_bwd(dh, xv, n, r, nw_ref[...], sc_ref[...])
        dx_ref[...] = dx1_ref[...] + dx
        acc_ref[0:1, :] += dsh
        acc_ref[1:2, :] += dsc
        acc_ref[2:3, :] += dnw

    return _call(body, "inproj_bwd", (s // t,),
                 [_tok(t, 2 * LRU_W), _tok(t, 3 * GDN_W), _tok(t, GDN_W), _tok(t, BA_PAD), _tok(t, D_MODEL),
                  _tok(t, D_MODEL), _vec(D_MODEL), _vec(D_MODEL), _vec(D_MODEL), _layer(l, D_MODEL, IN_MAIN),
                  _layer(l, D_MODEL, BA_PAD)],
                 [_tok(t, D_MODEL), _tok(t, D_MODEL), pl.BlockSpec((8, D_MODEL), lambda i: (0, 0))],
                 [_sds((s, D_MODEL)), _sds((s, D_MODEL), BF16), _sds((8, D_MODEL))])(
                     dpl, dpq, dpz, dba, x, dx1, nw, sc, sh, win, wba)


def _lru_gates(xw, cw_rows, cb, wa, wx, gab, gxb, lam):
    taps = _conv_taps(xw)
    xr = cb + cw_rows[0] * taps[0] + cw_rows[1] * taps[1] + cw_rows[2] * taps[2] + cw_rows[3] * taps[3]
    xb = xr.astype(BF16)
    r = jax.nn.sigmoid(_bdot(xb, wa) + gab)
    i = jax.nn.sigmoid(_bdot(xb, wx) + gxb)
    z = jnp.exp(-jnp.where(lam > 0, lam, -lam))
    w1 = 1.0 + z
    log1p_z = jnp.where(w1 == 1.0, z, jnp.log(w1) * z / (w1 - 1.0))
    ls = jnp.minimum(lam, 0.0) - log1p_z
    la = LRU_C * r * ls
    a = jnp.exp(la)
    mm_raw = -jnp.tanh(la) * (a * a + 1.0)
    mult = jnp.sqrt(jnp.maximum(mm_raw, 1e-12))
    return dict(taps=taps, xr=xr, r=r, i=i, ls=ls, a=a, mm_raw=mm_raw, mult=mult)


def _lru_specs(s, t, tile_of):
    nh = t // HALO
    xl = pl.BlockSpec((t, LRU_W), lambda i: (tile_of(i), 0))
    yl = pl.BlockSpec((t, LRU_W), lambda i: (tile_of(i), 1))
    hx = pl.BlockSpec((HALO, LRU_W), lambda i: (jnp.maximum(tile_of(i) * nh - 1, 0), 0))
    return xl, yl, hx


def _lru_fwd(proj, cw, cb, wa, wx, gab, gxb, lam, lnw):
    s = proj.shape[0]
    t = _tile(s, 256)
    xl, yl, hx = _lru_specs(s, t, lambda i: i)

    def body(xl_ref, yl_ref, hx_ref, cw_ref, cb_ref, wa_ref, wx_ref, gab_ref, gxb_ref, lam_ref, lnw_ref,
             out_ref, h_ref, a_s, b_s, hc):
        i = pl.program_id(0)

        @pl.when(i == 0)
        def _():
            hc[...] = jnp.zeros_like(hc)

        halo = jnp.where(i > 0, hx_ref[...], 0.0)
        xw = jnp.concatenate([halo, xl_ref[...]], axis=0)
        g = _lru_gates(xw, [cw_ref[k:k + 1, :] for k in range(CONV_K)], cb_ref[...], wa_ref[...], wx_ref[...],
                       gab_ref[...], gxb_ref[...], lam_ref[...])
        a_s[...] = g["a"]
        b_s[...] = g["mult"] * (g["i"] * g["xr"])

        def step(k, h):
            h = a_s[pl.ds(k, 1), :] * h + b_s[pl.ds(k, 1), :]
            h_ref[pl.ds(k, 1), :] = h
            return h

        hc[...] = lax.fori_loop(0, t, step, hc[...], unroll=8)
        m = h_ref[...] * _gelu(yl_ref[...])
        out_ref[...] = m * lax.rsqrt(jnp.mean(m * m, axis=-1, keepdims=True) + EPS) * lnw_ref[...]

    return _call(body, "lru_fwd", (s // t,),
                 [xl, yl, hx, _whole(cw), _vec(LRU_W), _whole(wa), _whole(wx)] + [_vec(LRU_W)] * 4,
                 [_tok(t, LRU_W), _tok(t, LRU_W)],
                 [_sds((s, LRU_W)), _sds((s, LRU_W))],
                 scratch=[pltpu.VMEM((t, LRU_W), F32), pltpu.VMEM((t, LRU_W), F32), pltpu.VMEM((1, LRU_W), F32)])(
                     proj, proj, proj, cw, cb, wa, wx, gab, gxb, lam, lnw)


def _lru_bwd(dout, proj, hs, cw, cb, wa, wx, gab, gxb, lam, lnw):
    s = proj.shape[0]
    t = _tile(s, 256)
    nt = s // t
    rev = lambda i: nt - 1 - i
    xl, yl, hx = _lru_specs(s, t, rev)
    nh = t // HALO
    tk = pl.BlockSpec((t, LRU_W), lambda i: (rev(i), 0))
    hh = pl.BlockSpec((HALO, LRU_W), lambda i: (jnp.maximum(rev(i) * nh - 1, 0), 0))

    def body(do_ref, xl_ref, yl_ref, hx_ref, h_ref, hh_ref, cw_ref, cb_ref, wa_ref, wx_ref, gab_ref, gxb_ref,
             lam_ref, lnw_ref, dp_ref, dwa_ref, dwx_ref, rows_ref, dh_s, dhd_s, carry, dxr_next):
        i = pl.program_id(0)
        first_tile = rev(i) == 0

        @pl.when(i == 0)
        def _():
            carry[...] = jnp.zeros_like(carry)
            dxr_next[...] = jnp.zeros_like(dxr_next)
            dwa_ref[...] = jnp.zeros_like(dwa_ref)
            dwx_ref[...] = jnp.zeros_like(dwx_ref)
            rows_ref[...] = jnp.zeros_like(rows_ref)

        halo = jnp.where(first_tile, 0.0, hx_ref[...])
        xw = jnp.concatenate([halo, xl_ref[...]], axis=0)
        cw_rows = [cw_ref[k:k + 1, :] for k in range(CONV_K)]
        lam_v = lam_ref[...]
        g = _lru_gates(xw, cw_rows, cb_ref[...], wa_ref[...], wx_ref[...], gab_ref[...], gxb_ref[...], lam_v)
        a, r, gi, xr, mult = g["a"], g["r"], g["i"], g["xr"], g["mult"]
        hv = h_ref[...]
        yv = yl_ref[...]
        gl = _gelu(yv)
        m = hv * gl
        rn = lax.rsqrt(jnp.mean(m * m, axis=-1, keepdims=True) + EPS)
        dov = do_ref[...]
        dmn = dov * lnw_ref[...]
        rows_ref[4:5, :] += jnp.sum(dov * m * rn, axis=0, keepdims=True)
        dm = rn * dmn - m * (rn * rn * rn) * jnp.mean(dmn * m, axis=-1, keepdims=True)
        dhd_s[...] = dm * gl
        dy = dm * hv * _gelu_grad(yv)
        dh_s[...] = a

        def step(k, c):
            row = t - 1 - k
            d = dhd_s[pl.ds(row, 1), :] + c
            c = dh_s[pl.ds(row, 1), :] * d
            dh_s[pl.ds(row, 1), :] = d
            return c

        carry[...] = lax.fori_loop(0, t, step, carry[...], unroll=8)
        dH = dh_s[...]
        hprev_halo = jnp.where(first_tile, 0.0, hh_ref[...])
        hprev = _drop_halo(_roll_rows(jnp.concatenate([hprev_halo, hv], axis=0), 1))
        da = dH * hprev
        dmult = dH * gi * xr
        di = dH * mult * xr
        dxr = dH * mult * gi
        dla = jnp.where(g["mm_raw"] > 1e-12, dmult * (0.5 / mult) * (-2.0 * a * a), 0.0) + da * a
        dr = dla * (LRU_C * g["ls"])
        sig_neg = jax.nn.sigmoid(-lam_v)
        rows_ref[3:4, :] += jnp.sum(dla * (LRU_C * r), axis=0, keepdims=True) * sig_neg
        drp = dr * r * (1.0 - r)
        dip = di * gi * (1.0 - gi)
        rows_ref[1:2, :] += jnp.sum(drp, axis=0, keepdims=True)
        rows_ref[2:3, :] += jnp.sum(dip, axis=0, keepdims=True)
        xb = xr.astype(BF16)
        drb = drp.astype(BF16)
        dib = dip.astype(BF16)
        dwa_ref[...] += _bdot(xb, drb, TN)
        dwx_ref[...] += _bdot(xb, dib, TN)
        dxr = dxr + _bdot(drb, wa_ref[...], NT) + _bdot(dib, wx_ref[...], NT)
        rows_ref[0:1, :] += jnp.sum(dxr, axis=0, keepdims=True)
        ext = jnp.concatenate([dxr, dxr_next[...]], axis=0)
        dx = cw_rows[CONV_K - 1] * dxr
        for k in range(CONV_K - 1):
            dx = dx + cw_rows[k] * _roll_rows(ext, -(CONV_K - 1 - k))[0:t]
        for k in range(CONV_K):
            rows_ref[8 + k:9 + k, :] += jnp.sum(dxr * g["taps"][k], axis=0, keepdims=True)
        dxr_next[...] = dxr[0:HALO]
        dp_ref[...] = jnp.concatenate([dx, dy], axis=1).astype(BF16)

    acc = lambda shape: pl.BlockSpec(shape, lambda i: (0, 0))
    return _call(body, "lru_bwd", (nt,),
                 [tk, xl, yl, hx, tk, hh, _whole(cw), _vec(LRU_W), _whole(wa), _whole(wx)] + [_vec(LRU_W)] * 4,
                 [pl.BlockSpec((t, 2 * LRU_W), lambda i: (rev(i), 0)), acc((LRU_W, LRU_W)), acc((LRU_W, LRU_W)),
                  acc((16, LRU_W))],
                 [_sds((s, 2 * LRU_W), BF16), _sds((LRU_W, LRU_W)), _sds((LRU_W, LRU_W)), _sds((16, LRU_W))],
                 scratch=[pltpu.VMEM((t, LRU_W), F32), pltpu.VMEM((t, LRU_W), F32), pltpu.VMEM((1, LRU_W), F32),
                          pltpu.VMEM((HALO, LRU_W), F32)])(
                     dout, proj, proj, proj, hs, hs, cw, cb, wa, wx, gab, gxb, lam, lnw)


def _gdn_masks():
    row = lax.broadcasted_iota(jnp.int32, (STACK, STACK), 0)
    col = lax.broadcasted_iota(jnp.int32, (STACK, STACK), 1)
    same = (row // CHUNK) == (col // CHUNK)
    return jnp.stack([(same & (col <= row)).astype(F32), (same & (col < row)).astype(F32), (row == col).astype(F32)])


def _conv_silu(xw, rows):
    taps = _conv_taps(xw)
    y = rows[0] * taps[0] + rows[1] * taps[1] + rows[2] * taps[2] + rows[3] * taps[3]
    return y * jax.nn.sigmoid(y)


def _split3(v):
    hi = v.astype(BF16)
    r1 = v - hi.astype(F32)
    mid = r1.astype(BF16)
    return hi, mid, (r1 - mid.astype(F32)).astype(BF16)


def _mask_dot_raw(mask, v, dims):
    parts = _split3(v)
    d = lambda p: lax.dot_general(mask, p, dims, preferred_element_type=F32)
    return d(parts[0]) + (d(parts[1]) + d(parts[2]))


@jax.custom_vjp
def _mask_dot(mask, v):
    return _mask_dot_raw(mask, v, NN)


def _mask_dot_fwd(mask, v):
    return _mask_dot_raw(mask, v, NN), mask


def _mask_dot_bwd(mask, ct):
    return jnp.zeros_like(mask), _mask_dot_raw(mask, ct, TN)


_mask_dot.defvjp(_mask_dot_fwd, _mask_dot_bwd)


def _unit_lower_inverse(ns, eye):
    tinvs = [eye + n for n in ns]
    ps = list(ns)
    for _ in range(5):
        ps = [_bdot(p, p) for p in ps]
        tinvs = [t + _bdot(t, p) for t, p in zip(tinvs, ps)]
    return tuple(t.astype(BF16) for t in tinvs)


def _refined(ns, rhss, tinvs, dims):
    x0s = [_bdot(t, r, dims) for t, r in zip(tinvs, rhss)]
    ress = [r - x0 + _sdot(n, x0, dims) for n, r, x0 in zip(ns, rhss, x0s)]
    return tuple(x0 + _bdot(t, res, dims) for t, x0, res in zip(tinvs, x0s, ress))


@jax.custom_vjp
def _unit_lower_solve(ns, rhss, tinvs):
    return _refined(ns, rhss, tinvs, NN)


def _unit_lower_solve_fwd(ns, rhss, tinvs):
    xs = _unit_lower_solve(ns, rhss, tinvs)
    return xs, (ns, tinvs, xs)


def _unit_lower_solve_bwd(res, cts):
    ns, tinvs, xs = res
    ys = _refined(ns, cts, tinvs, TN)
    return (tuple(_bdot(y, x, NT) for y, x in zip(ys, xs)), ys, tuple(jnp.zeros_like(t) for t in tinvs))


_unit_lower_solve.defvjp(_unit_lower_solve_fwd, _unit_lower_solve_bwd)


def _gdn_prep(xqs, xks, xvs, bas, cwq, cwk, cwv, pa, pd, masks, tinvs=None, with_inverse=False):
    lower, strict, eye = masks[0], masks[1], masks[2]
    lower_b = lower.astype(BF16)
    lane = lax.broadcasted_iota(jnp.int32, (CHUNK, LANES), 1)
    each = lambda f, *lists: [f(*vals) for vals in zip(*lists)]
    stack = lambda xw, rows: jnp.concatenate(_split(_conv_silu(xw, rows), HEADS, 1), axis=0)
    qs, ks, vs = (each(lambda xw: stack(xw, cw), xs) for xs, cw in ((xqs, cwq), (xks, cwk), (xvs, cwv)))
    qns = each(lambda q: q * lax.rsqrt(jnp.sum(q * q, axis=-1, keepdims=True) + 1e-6) * (HEAD_DIM ** -0.5), qs)
    kns = each(lambda k: k * lax.rsqrt(jnp.sum(k * k, axis=-1, keepdims=True) + 1e-6), ks)

    def col(a, j):
        return jnp.broadcast_to(jnp.sum(jnp.where(lane == j, a, 0.0), axis=1, keepdims=True), (CHUNK, HEAD_DIM))

    betas = each(lambda ba: jnp.concatenate([col(jax.nn.sigmoid(ba), h) for h in range(HEADS)], axis=0), bas)
    g_heads = each(lambda ba: [col(-jnp.exp(pa) * _softplus(ba + pd), HEADS + h) for h in range(HEADS)], bas)
    gs = each(lambda gh: jnp.concatenate(gh, axis=0), g_heads)
    gls = each(lambda gh: jnp.concatenate([jnp.broadcast_to(jnp.sum(g, axis=0, keepdims=True), (CHUNK, HEAD_DIM))
                                           for g in gh], axis=0), g_heads)
    gcs = each(lambda g: _mask_dot(lower_b, g), gs)

    def decay_of(gc):
        gc_rows = jnp.transpose(gc)
        return jnp.exp((jnp.concatenate([gc, gc], axis=1) - jnp.concatenate([gc_rows, gc_rows], axis=0)) * lower)

    decays = each(decay_of, gcs)
    egcs = each(jnp.exp, gcs)
    kbs = each(lambda kn, beta: kn * beta, kns, betas)
    ns = tuple(each(lambda kb, kn, decay: -(_bdot(kb, kn, NT) * decay * strict), kbs, kns, decays))
    if tinvs is None:
        tinvs = _unit_lower_inverse([lax.stop_gradient(n) for n in ns], eye)
    rhss = tuple(each(lambda v, beta, kb, egc: jnp.concatenate([v * beta, kb * egc], axis=1), vs, betas, kbs, egcs))
    sols = _unit_lower_solve(ns, rhss, tuple(tinvs))
    attns = each(lambda qn, kn, decay: _bdot(qn, kn, NT) * decay * lower, qns, kns, decays)
    outs = []
    for sol, qn, kn, egc, gl, gc, attn, tinv in zip(sols, qns, kns, egcs, gls, gcs, attns, tinvs):
        u, w = _split(sol, 2, 1)
        out = (u, w, qn * egc, kn * jnp.exp(gl - gc), attn, jnp.exp(gl))
        outs.append(out + (tinv,) if with_inverse else out)
    return outs


def _gdn_scan(states, u, w, qd, kt, attn, egl, z, nw):
    us, ws, qds, kts, egls = (_split(a, HEADS, 0) for a in (u, w, qd, kt, egl))
    vn = [us[h] - _bdot(ws[h], states[h]) for h in range(HEADS)]
    o = jnp.concatenate([_bdot(qds[h], states[h]) for h in range(HEADS)], axis=0)
    o = o + _bdot(attn, jnp.concatenate(vn, axis=0))
    new = [states[h] * jnp.concatenate([egls[h], egls[h]], axis=0) + _bdot(kts[h], vn[h], TN) for h in range(HEADS)]
    on = o * lax.rsqrt(jnp.mean(o * o, axis=-1, keepdims=True) + EPS) * nw
    return new, on * (z * jax.nn.sigmoid(z))


def _gdn_in_specs(step_of, chunks):
    nh = chunks * CHUNK // HALO
    main = [pl.BlockSpec((chunks * CHUNK, GDN_W), functools.partial(lambda col, i: (step_of(i), col), col))
            for col in (2, 3, 4)]
    halo = [pl.BlockSpec((HALO, GDN_W), functools.partial(lambda col, i: (jnp.maximum(step_of(i) * nh - 1, 0), col),
                                                         col)) for col in (2, 3, 4)]
    return main, halo


def _stk(width, step_of, chunks=1):
    return pl.BlockSpec((chunks * STACK, width), lambda i: (step_of(i), 0))


def _chunk_inputs(main_refs, halo_refs, k, first_step):
    rows = slice(k * CHUNK, (k + 1) * CHUNK)
    if k == 0:
        halos = [jnp.where(first_step, 0.0, h[...]) for h in halo_refs]
    else:
        halos = [m[k * CHUNK - HALO:k * CHUNK, :] for m in main_refs]
    return [jnp.concatenate([h, m[rows, :]], axis=0) for h, m in zip(halos, main_refs)]


def _gdn_prep_fwd(proj, ba, cw, pa, pd, masks):
    s = proj.shape[0]
    nc = s // CHUNK
    per = min(PREP_CHUNKS, nc)
    main, halo = _gdn_in_specs(lambda i: i, per)

    def body(xq_ref, xk_ref, xv_ref, hq_ref, hk_ref, hv_ref, ba_ref, cw_ref, pa_ref, pd_ref, mk_ref, *out_refs):
        first_step = pl.program_id(0) == 0
        rows = [[cw_ref[k:k + 1, j * GDN_W:(j + 1) * GDN_W] for k in range(CONV_K)] for j in range(3)]
        cst = [mk_ref[0], mk_ref[1], mk_ref[2]]
        xs = [_chunk_inputs((xq_ref, xk_ref, xv_ref), (hq_ref, hk_ref, hv_ref), k, first_step) for k in range(per)]
        outs = _gdn_prep([x[0] for x in xs], [x[1] for x in xs], [x[2] for x in xs],
                         [ba_ref[k * CHUNK:(k + 1) * CHUNK, :] for k in range(per)], rows[0], rows[1], rows[2],
                         pa_ref[...], pd_ref[...], cst, with_inverse=True)
        for k, out in enumerate(outs):
            for ref, val in zip(out_refs, out):
                ref[k * STACK:(k + 1) * STACK, :] = val.astype(ref.dtype)

    ident = lambda i: i
    stacked = lambda dt: _sds((nc * STACK, HEAD_DIM), dt)
    wide, thin = _stk(STACK, ident, per), _stk(HEAD_DIM, ident, per)
    return _call(body, "gdn_prep_fwd", (nc // per,),
                 main + halo + [_tok(per * CHUNK, BA_PAD), _whole(cw), _vec(BA_PAD), _vec(BA_PAD), _whole(masks)],
                 [thin] * 4 + [wide, thin, wide],
                 [stacked(F32), stacked(BF16), stacked(BF16), stacked(BF16), _sds((nc * STACK, STACK), BF16),
                  stacked(F32), _sds((nc * STACK, STACK), BF16)])(
                     proj, proj, proj, proj, proj, proj, ba, cw, pa, pd, masks)


def _gdn_prep_bwd(cts, tinv, proj, ba, cw, pa, pd, masks):
    s = proj.shape[0]
    nc = s // CHUNK
    per = min(PREP_CHUNKS, nc)
    steps = nc // per
    rev = lambda i: steps - 1 - i
    main, halo = _gdn_in_specs(rev, per)

    def body(du_ref, dw_ref, dqd_ref, dkt_ref, dattn_ref, degl_ref, tinv_ref, xq_ref, xk_ref, xv_ref, hq_ref, hk_ref,
             hv_ref, ba_ref, cw_ref, pa_ref, pd_ref, mk_ref, dp_ref, dba_ref, dcw_ref, dpar_ref, carry):
        i = pl.program_id(0)
        first_step = rev(i) == 0

        @pl.when(i == 0)
        def _():
            carry[...] = jnp.zeros_like(carry)
            dcw_ref[...] = jnp.zeros_like(dcw_ref)
            dpar_ref[...] = jnp.zeros_like(dpar_ref)

        rows = [[cw_ref[k:k + 1, j * GDN_W:(j + 1) * GDN_W] for k in range(CONV_K)] for j in range(3)]
        cst = [mk_ref[0], mk_ref[1], mk_ref[2]]
        xs = [_chunk_inputs((xq_ref, xk_ref, xv_ref), (hq_ref, hk_ref, hv_ref), k, first_step) for k in range(per)]
        stks = [slice(k * STACK, (k + 1) * STACK) for k in range(per)]
        tinvs = [tinv_ref[stk, :] for stk in stks]
        fn = lambda xqs, xks, xvs, bs, rq, rk, rv, a, d: _gdn_prep(xqs, xks, xvs, bs, rq, rk, rv, a, d, cst, tinvs=tinvs)
        _, vjp = jax.vjp(fn, [x[0] for x in xs], [x[1] for x in xs], [x[2] for x in xs],
                         [ba_ref[k * CHUNK:(k + 1) * CHUNK, :] for k in range(per)], rows[0], rows[1], rows[2],
                         pa_ref[...], pd_ref[...])
        dxqs, dxks, dxvs, dbas, drq, drk, drv, dpa, dpd = vjp(
            [tuple(ref[stk, :] for ref in (du_ref, dw_ref, dqd_ref, dkt_ref, dattn_ref, degl_ref)) for stk in stks])
        dxws = [jnp.concatenate(parts, axis=1) for parts in zip(dxqs, dxks, dxvs)]
        for k in range(per):
            dba_ref[k * CHUNK:(k + 1) * CHUNK, :] = dbas[k].astype(BF16)
        for j, dr in enumerate((drq, drk, drv)):
            for kk in range(CONV_K):
                dcw_ref[kk:kk + 1, j * GDN_W:(j + 1) * GDN_W] += dr[kk]
        dpar_ref[0:1, :] += dpa
        dpar_ref[1:2, :] += dpd
        pad = jnp.zeros((CHUNK - HALO, 3 * GDN_W), F32)
        for k in range(per):
            late = carry[...] if k == per - 1 else dxws[k + 1][0:HALO]
            dp_ref[k * CHUNK:(k + 1) * CHUNK, :] = (dxws[k][HALO:] + jnp.concatenate([pad, late], axis=0)).astype(BF16)
        carry[...] = dxws[0][0:HALO]

    acc = lambda shape: pl.BlockSpec(shape, lambda i: (0, 0))
    wide, thin = _stk(STACK, rev, per), _stk(HEAD_DIM, rev, per)
    return _call(body, "gdn_prep_bwd", (steps,),
                 [thin] * 4 + [wide, thin, wide] + main + halo
                 + [pl.BlockSpec((per * CHUNK, BA_PAD), lambda i: (rev(i), 0)), _whole(cw), _vec(BA_PAD), _vec(BA_PAD),
                    _whole(masks)],
                 [pl.BlockSpec((per * CHUNK, 3 * GDN_W), lambda i: (rev(i), 0)),
                  pl.BlockSpec((per * CHUNK, BA_PAD), lambda i: (rev(i), 0)), acc((CONV_K, 3 * GDN_W)),
                  acc((8, BA_PAD))],
                 [_sds((s, 3 * GDN_W), BF16), _sds((s, BA_PAD), BF16), _sds((CONV_K, 3 * GDN_W)), _sds((8, BA_PAD))],
                 scratch=[pltpu.VMEM((HALO, 3 * GDN_W), F32)])(
                     *cts, tinv, proj, proj, proj, proj, proj, proj, ba, cw, pa, pd, masks)


def _stack_heads(v):
    return jnp.concatenate(_split(v, HEADS, 1), axis=0)


def _unstack_heads(v):
    return jnp.concatenate(_split(v, HEADS, 0), axis=1)


def _gdn_scan_fwd(prep, proj, nw):
    s = proj.shape[0]
    nc = s // CHUNK
    per = min(SCAN_CHUNKS, nc)
    ident = lambda i: i
    srows = HEADS * HEAD_DIM

    def body(u_ref, w_ref, qd_ref, kt_ref, attn_ref, egl_ref, z_ref, nw_ref, out_ref, st_ref, state):
        @pl.when(pl.program_id(0) == 0)
        def _():
            state[...] = jnp.zeros_like(state)

        states = [state[h * HEAD_DIM:(h + 1) * HEAD_DIM, :] for h in range(HEADS)]
        for k in range(per):
            stk, tok = slice(k * STACK, (k + 1) * STACK), slice(k * CHUNK, (k + 1) * CHUNK)
            for h in range(HEADS):
                st_ref[k * srows + h * HEAD_DIM:k * srows + (h + 1) * HEAD_DIM, :] = states[h]
            states, out = _gdn_scan(states, u_ref[stk, :], w_ref[stk, :], qd_ref[stk, :], kt_ref[stk, :],
                                    attn_ref[stk, :], egl_ref[stk, :], _stack_heads(z_ref[tok, :]), nw_ref[...])
            out_ref[tok, :] = _unstack_heads(out)
        for h in range(HEADS):
            state[h * HEAD_DIM:(h + 1) * HEAD_DIM, :] = states[h]

    thin, wide = _stk(HEAD_DIM, ident, per), _stk(STACK, ident, per)
    return _call(body, "gdn_scan_fwd", (nc // per,),
                 [thin] * 4 + [wide, thin, _tok(per * CHUNK, GDN_W, col=5), _vec(HEAD_DIM)],
                 [_tok(per * CHUNK, GDN_W), pl.BlockSpec((per * srows, HEAD_DIM), lambda i: (i, 0))],
                 [_sds((s, GDN_W)), _sds((nc * srows, HEAD_DIM))],
                 scratch=[pltpu.VMEM((srows, HEAD_DIM), F32)])(*prep, proj, nw)


def _gdn_scan_bwd(dout, prep, st, proj, nw):
    s = proj.shape[0]
    nc = s // CHUNK
    per = min(SCAN_CHUNKS, nc)
    steps = nc // per
    rev = lambda i: steps - 1 - i
    srows = HEADS * HEAD_DIM

    def body(do_ref, u_ref, w_ref, qd_ref, kt_ref, attn_ref, egl_ref, st_ref, z_ref, nw_ref,
             du_ref, dw_ref, dqd_ref, dkt_ref, dattn_ref, degl_ref, dz_ref, dnw_ref, dstate):
        @pl.when(pl.program_id(0) == 0)
        def _():
            dstate[...] = jnp.zeros_like(dstate)
            dnw_ref[...] = jnp.zeros_like(dnw_ref)

        dnew = [dstate[h * HEAD_DIM:(h + 1) * HEAD_DIM, :] for h in range(HEADS)]
        for k in reversed(range(per)):
            stk, tok = slice(k * STACK, (k + 1) * STACK), slice(k * CHUNK, (k + 1) * CHUNK)
            states = [st_ref[k * srows + h * HEAD_DIM:k * srows + (h + 1) * HEAD_DIM, :] for h in range(HEADS)]
            f32 = lambda ref: ref[stk, :].astype(F32)
            _, vjp = jax.vjp(_gdn_scan, states, u_ref[stk, :], f32(w_ref), f32(qd_ref), f32(kt_ref), f32(attn_ref),
                             egl_ref[stk, :], _stack_heads(z_ref[tok, :]), nw_ref[...])
            dnew, du, dw, dqd, dkt, dattn, degl, dz, dnw = vjp((dnew, _stack_heads(do_ref[tok, :])))
            for ref, val in zip((du_ref, dw_ref, dqd_ref, dkt_ref, dattn_ref, degl_ref),
                                (du, dw, dqd, dkt, dattn, degl)):
                ref[stk, :] = val
            dz_ref[tok, :] = _unstack_heads(dz).astype(BF16)
            dnw_ref[0:1, :] += dnw
        for h in range(HEADS):
            dstate[h * HEAD_DIM:(h + 1) * HEAD_DIM, :] = dnew[h]

    tokr = lambda n, col=0: pl.BlockSpec((per * CHUNK, n), lambda i: (rev(i), col))
    thin, wide = _stk(HEAD_DIM, rev, per), _stk(STACK, rev, per)
    return _call(body, "gdn_scan_bwd", (steps,),
                 [tokr(GDN_W)] + [thin] * 4 + [wide, thin, pl.BlockSpec((per * srows, HEAD_DIM), lambda i: (rev(i), 0)),
                                               tokr(GDN_W, 5), _vec(HEAD_DIM)],
                 [thin] * 4 + [wide, thin, tokr(GDN_W), pl.BlockSpec((8, HEAD_DIM), lambda i: (0, 0))],
                 [_sds((nc * STACK, HEAD_DIM))] * 4 + [_sds((nc * STACK, STACK)), _sds((nc * STACK, HEAD_DIM)),
                                                       _sds((s, GDN_W), BF16), _sds((8, HEAD_DIM))],
                 scratch=[pltpu.VMEM((srows, HEAD_DIM), F32)])(dout, *prep, st, proj, nw)


def _wo_specs(l):
    half = N_DEV // 2
    return [pl.BlockSpec((half, 1, D_MODEL // N_DEV, D_MODEL), functools.partial(lambda k, *_: (k, l, 0, 0), k))
            for k in range(2)]


def _wo_half(ref):
    return ref[:, 0].reshape(ref.shape[0] * ref.shape[2], ref.shape[3])


def _out_mlp_fwd(ol, og, x, wo, g1, nw2, sc2, sh2, g2, wup, wdn, l):
    s = x.shape[0]
    t = _tile(s, 512)
    nj = wup.shape[0] // MLP_BLOCKS
    fc = wup.shape[3]

    def body(ol_ref, og_ref, x_ref, wol_ref, wog_ref, g1_ref, nw_ref, sc_ref, sh_ref, g2_ref, wup_ref, wdn_ref,
             x1_ref, mix_ref, ff_ref, x2_ref, h2_s, acc_s):
        j = pl.program_id(1)

        @pl.when(j == 0)
        def _():
            mix = _bdot(ol_ref[...], _wo_half(wol_ref)) + _bdot(og_ref[...], _wo_half(wog_ref))
            x1 = x_ref[...] + g1_ref[...] * mix
            mix_ref[...] = mix.astype(BF16)
            x1_ref[...] = x1
            h2, _, _ = _modulated_norm(x1, nw_ref[...], sc_ref[...], sh_ref[...])
            h2_s[...] = h2.astype(BF16)
            acc_s[...] = jnp.zeros_like(acc_s)

        part = None
        for b in range(MLP_BLOCKS):
            up = _bdot(h2_s[...], wup_ref[b, 0])
            down = _bdot(jnp.square(jnp.maximum(up, 0.0)), wdn_ref[b, 0])
            part = down if part is None else part + down
        acc_s[...] += part

        @pl.when(j == nj - 1)
        def _():
            ff_ref[...] = acc_s[...].astype(BF16)
            x2_ref[...] = x1_ref[...] + g2_ref[...] * acc_s[...]

    tk = lambda n: pl.BlockSpec((t, n), lambda i, j: (i, 0))
    return _call(body, "out_mlp_fwd", (s // t, nj),
                 [tk(LRU_W), tk(GDN_W), tk(D_MODEL)] + _wo_specs(l) + [_vec(D_MODEL)] * 5
                 + [pl.BlockSpec((MLP_BLOCKS, 1, D_MODEL, fc), lambda i, j: (j, l, 0, 0)),
                    pl.BlockSpec((MLP_BLOCKS, 1, fc, D_MODEL), lambda i, j: (j, l, 0, 0))],
                 [tk(D_MODEL)] * 4,
                 [_sds((s, D_MODEL)), _sds((s, D_MODEL), BF16), _sds((s, D_MODEL), BF16), _sds((s, D_MODEL))],
                 scratch=[pltpu.VMEM((t, D_MODEL), BF16), pltpu.VMEM((t, D_MODEL), F32)])(
                     ol, og, x, wo, wo, g1, nw2, sc2, sh2, g2, wup, wdn)


def _mlp_bwd(dx2, x1, ff, nw2, sc2, sh2, g2, wup, wdn, l):
    s = x1.shape[0]
    t = _tile(s, 512)
    nj = wup.shape[0] // MLP_BLOCKS
    fc = wup.shape[3]

    def body(dx2_ref, x1_ref, ff_ref, nw_ref, sc_ref, sh_ref, g2_ref, wup_ref, wdn_ref,
             act_ref, dup_ref, h2_ref, dff_ref, dx1_ref, rows_ref, dh2_s):
        i, j = pl.program_id(0), pl.program_id(1)

        @pl.when((i == 0) & (j == 0))
        def _():
            rows_ref[...] = jnp.zeros_like(rows_ref)

        @pl.when(j == 0)
        def _():
            h2, _, _ = _modulated_norm(x1_ref[...], nw_ref[...], sc_ref[...], sh_ref[...])
            h2_ref[...] = h2.astype(BF16)
            dx2 = dx2_ref[...]
            dff_ref[...] = (dx2 * g2_ref[...]).astype(BF16)
            rows_ref[2:3, :] += jnp.sum(dx2 * ff_ref[...].astype(F32), axis=0, keepdims=True)
            dh2_s[...] = jnp.zeros_like(dh2_s)

        part = None
        for b in range(MLP_BLOCKS):
            cols = slice(b * fc, (b + 1) * fc)
            up = _bdot(h2_ref[...], wup_ref[b, 0])
            ru = jnp.maximum(up, 0.0)
            act_ref[:, cols] = (ru * ru).astype(BF16)
            dup = (_bdot(dff_ref[...], wdn_ref[b, 0], NT) * (2.0 * ru)).astype(BF16)
            dup_ref[:, cols] = dup
            back = _bdot(dup, wup_ref[b, 0], NT)
            part = back if part is None else part + back
        dh2_s[...] += part

        @pl.when(j == nj - 1)
        def _():
            xv = x1_ref[...]
            _, n, r = _modulated_norm(xv, nw_ref[...], sc_ref[...], sh_ref[...])
            dx, dsh, dsc, dnw = _modulated_norm_bwd(dh2_s[...], xv, n, r, nw_ref[...], sc_ref[...])
            dx1_ref[...] = dx2_ref[...] + dx
            rows_ref[0:1, :] += dsh
            rows_ref[1:2, :] += dsc
            rows_ref[3:4, :] += dnw

    tk = lambda n: pl.BlockSpec((t, n), lambda i, j: (i, 0))
    tj = pl.BlockSpec((t, MLP_BLOCKS * fc), lambda i, j: (i, j))
    return _call(body, "mlp_bwd", (s // t, nj),
                 [tk(D_MODEL)] * 3 + [_vec(D_MODEL)] * 4
                 + [pl.BlockSpec((MLP_BLOCKS, 1, D_MODEL, fc), lambda i, j: (j, l, 0, 0)),
                    pl.BlockSpec((MLP_BLOCKS, 1, fc, D_MODEL), lambda i, j: (j, l, 0, 0))],
                 [tj, tj, tk(D_MODEL), tk(D_MODEL), tk(D_MODEL), pl.BlockSpec((8, D_MODEL), lambda i, j: (0, 0))],
                 [_sds((s, D_FF), BF16), _sds((s, D_FF), BF16), _sds((s, D_MODEL), BF16),
                  _sds((s, D_MODEL), BF16), _sds((s, D_MODEL)), _sds((8, D_MODEL))],
                 scratch=[pltpu.VMEM((t, D_MODEL), F32)], vmem_mb=56)(dx2, x1, ff, nw2, sc2, sh2, g2, wup, wdn)


def _outproj_bwd(dx1, mix, g1, wo, l):
    s = dx1.shape[0]
    t = _tile(s, 512)

    def body(dx1_ref, mix_ref, g1_ref, wol_ref, wog_ref, dmix_ref, dol_ref, dog_ref, rows_ref):
        @pl.when(pl.program_id(0) == 0)
        def _():
            rows_ref[...] = jnp.zeros_like(rows_ref)

        dx1v = dx1_ref[...]
        rows_ref[0:1, :] += jnp.sum(dx1v * mix_ref[...].astype(F32), axis=0, keepdims=True)
        dmix = (dx1v * g1_ref[...]).astype(BF16)
        dmix_ref[...] = dmix
        dol_ref[...] = _bdot(dmix, _wo_half(wol_ref), NT)
        dog_ref[...] = _bdot(dmix, _wo_half(wog_ref), NT)

    return _call(body, "outproj_bwd", (s // t,),
                 [_tok(t, D_MODEL), _tok(t, D_MODEL), _vec(D_MODEL)] + _wo_specs(l),
                 [_tok(t, D_MODEL), _tok(t, LRU_W), _tok(t, GDN_W), pl.BlockSpec((8, D_MODEL), lambda i: (0, 0))],
                 [_sds((s, D_MODEL), BF16), _sds((s, LRU_W)), _sds((s, GDN_W)), _sds((8, D_MODEL))])(dx1, mix, g1, wo, wo)


def _tn_matmul(a, b, name, out=None, l=0, blocked=False, row_block=0):
    s, m = a.shape
    n = b.shape[1]
    ts, bm = _tile(s, 2048), _tile(m, 1024)
    bn = next(w for w in ((512,) if blocked else (1024, 512, 640, 384, 256, 128)) if n % w == 0)

    def body(a_ref, b_ref, *rest):
        o_ref = rest[-1]

        @pl.when(pl.program_id(2) == 0)
        def _():
            o_ref[...] = jnp.zeros_like(o_ref)

        acc = _bdot(a_ref[...], b_ref[...], TN)
        o_ref[...] += acc.reshape(o_ref.shape)

    in_specs = [pl.BlockSpec((ts, bm), lambda i, j, k: (k, i)), pl.BlockSpec((ts, bn), lambda i, j, k: (k, j))]
    grid = (m // bm, n // bn, s // ts)
    if out is None:
        return _call(body, name, grid, in_specs, pl.BlockSpec((bm, bn), lambda i, j, k: (i, j)), _sds((m, n)))(a, b)
    if blocked:
        out_spec = pl.BlockSpec((1, 1, bm, bn), lambda i, j, k: (l, j, i, 0))
    else:
        out_spec = pl.BlockSpec((1, bm, bn), lambda i, j, k: (l, i + row_block * (m // bm), j))
    return _call(body, name, grid, in_specs + [pl.BlockSpec(memory_space=pl.ANY)], out_spec,
                 _sds(out.shape), aliases={2: 0})(a, b, out)


def _final_fwd_bwd(x, target, fw):
    s = x.shape[0]
    t = _tile(s, 512)

    def body(x_ref, tg_ref, fw_ref, dx_ref, rows_ref):
        @pl.when(pl.program_id(0) == 0)
        def _():
            rows_ref[...] = jnp.zeros_like(rows_ref)

        xv = x_ref[...]
        fwv = fw_ref[...]
        r = lax.rsqrt(jnp.mean(xv * xv, axis=-1, keepdims=True) + EPS)
        err = xv * r * fwv - tg_ref[...]
        part = 0.5 * jnp.sum(jnp.mean(err * err, axis=-1, keepdims=True), axis=0, keepdims=True)
        rows_ref[1:2, :] += jnp.broadcast_to(part, (1, D_MODEL))
        dy = err * (1.0 / D_MODEL)
        rows_ref[0:1, :] += jnp.sum(dy * xv * r, axis=0, keepdims=True)
        dxn = dy * fwv
        dx_ref[...] = r * dxn - xv * (r * r * r) * jnp.mean(dxn * xv, axis=-1, keepdims=True)

    return _call(body, "final_fwd_bwd", (s // t,),
                 [_tok(t, D_MODEL), _tok(t, D_MODEL), _vec(D_MODEL)],
                 [_tok(t, D_MODEL), pl.BlockSpec((8, D_MODEL), lambda i: (0, 0))],
                 [_sds((s, D_MODEL)), _sds((8, D_MODEL))])(x, target, fw)


def _adamw(w, g, m, v):
    m = ADAM_B1 * m + (1.0 - ADAM_B1) * g
    v = ADAM_B2 * v + (1.0 - ADAM_B2) * (g * g)
    m_hat = m / (1.0 - ADAM_B1 ** ADAM_STEP)
    v_hat = v / (1.0 - ADAM_B2 ** ADAM_STEP)
    return -ADAM_LR * (m_hat / (jnp.sqrt(v_hat) + ADAM_EPS) + ADAM_WD * w), m, v


def _mod_local(c_all, wmod, bmod_cols):
    nl, _, cols = wmod.shape

    def body(c_ref, w_ref, b_ref, o_ref):
        cv = c_ref[...]
        o_ref[0] = _bdot(cv * jax.nn.sigmoid(cv), w_ref[0]) + b_ref[0]

    return _call(body, "mod_local", (nl,),
                 [_whole(c_all), pl.BlockSpec((1, D_MODEL, cols), lambda l: (l, 0, 0)),
                  pl.BlockSpec((1, 1, cols), lambda l: (l, 0, 0))],
                 pl.BlockSpec((1, N_DEV, cols), lambda l: (l, 0, 0)), _sds((nl, N_DEV, cols)))(c_all, wmod, bmod_cols)


def _wmod_update(c_all, dmod_cols, w, m, v):
    nl, _, cols = w.shape

    def body(c_ref, d_ref, w_ref, m_ref, v_ref, g_ref, dl_ref, nm_ref, nv_ref):
        cv = c_ref[...]
        g = _bdot(cv * jax.nn.sigmoid(cv), d_ref[0], TN)
        g_ref[0] = g
        dl_ref[0], nm_ref[0], nv_ref[0] = _adamw(w_ref[0], g, m_ref[0], v_ref[0])

    wspec = pl.BlockSpec((1, D_MODEL, cols), lambda l: (l, 0, 0))
    return _call(body, "wmod_update", (nl,),
                 [_whole(c_all), pl.BlockSpec((1, N_DEV, cols), lambda l: (l, 0, 0)), wspec, wspec, wspec],
                 [wspec] * 4, [_sds(w.shape)] * 4)(c_all, dmod_cols, w, m, v)


def _sum_devices(gathered):
    _, r, _ = gathered.shape

    def body(g_ref, o_ref):
        acc = g_ref[0]
        for d in range(1, N_DEV):
            acc = acc + g_ref[d]
        o_ref[...] = acc

    return _call(body, "sum_devices", (1,), [_whole(gathered)], pl.BlockSpec((r, LANES), lambda i: (0, 0)),
                 _sds((r, LANES)))(gathered)


def _adam_flat(w, g, m, v):
    r = w.shape[0]

    def body(w_ref, g_ref, m_ref, v_ref, dl_ref, nm_ref, nv_ref):
        dl_ref[...], nm_ref[...], nv_ref[...] = _adamw(w_ref[...], g_ref[...], m_ref[...], v_ref[...])

    spec = pl.BlockSpec((r, LANES), lambda i: (0, 0))
    return _call(body, "adam_small", (1,), [spec] * 4, [spec] * 3, [_sds((r, LANES))] * 3)(w, g, m, v)


def _pair_add(x, p, others):
    _, r, c = x.shape
    tr = _tile(r, 128 if c > 512 else 256)

    def body(others_ref, x_ref, p_ref, o_ref):
        o_ref[...] = (x_ref[...] + p_ref[...]).astype(BF16)

    return _call(body, "pair_add", (3, r // tr),
                 [pl.BlockSpec((1, tr, c), lambda q, i, others_ref: (others_ref[q], i, 0)),
                  pl.BlockSpec((1, tr, c), lambda q, i, others_ref: (others_ref[3 + q], i, 0))],
                 pl.BlockSpec((1, tr, c), lambda q, i, others_ref: (q, i, 0)), _sds((3, r, c), BF16),
                 prefetch=1)(others, x, p)


def _reduce_adam(x, p, q, place, w, m, v, l, outs):
    _, r, c = x.shape
    tr = _tile(r, 128 if c > 512 else 256)

    def body(place_ref, x_ref, p_ref, q_ref, w_ref, m_ref, v_ref, *rest):
        g_ref, dl_ref, nm_ref, nv_ref = rest[-4:]
        g = (((x_ref[0] + p_ref[0]) + q_ref[0].astype(F32)) + q_ref[1].astype(F32)) + q_ref[2].astype(F32)
        g_ref[0] = g
        dl_ref[0], nm_ref[0], nv_ref[0] = _adamw(w_ref[0], g, m_ref[0], v_ref[0])

    flat = pl.BlockSpec((1, tr, c), lambda i, place_ref: (l, i, 0))
    through = pl.BlockSpec(memory_space=pl.ANY)
    return _call(body, "reduce_adam", (r // tr,),
                 [pl.BlockSpec((1, tr, c), lambda i, place_ref: (place_ref[0], i, 0)),
                  pl.BlockSpec((1, tr, c), lambda i, place_ref: (place_ref[1], i, 0)),
                  pl.BlockSpec((3, tr, c), lambda i, place_ref: (0, i, 0)), flat, flat, flat] + [through] * 4,
                 [flat] * 4, [_sds(w.shape)] * 4, prefetch=1, aliases={7 + k: k for k in range(4)})(
                     place, x, p, q, w, m, v, *outs)


def _place():
    return lax.axis_index("x"), lax.axis_index("y"), lax.axis_index("c")


def _all_gather(xs, name, space):
    n = len(xs)

    def body(*refs):
        x_refs, o_refs = refs[:n], refs[n:2 * n]
        send_sems, recv_sems, local_sems = refs[2 * n:]
        x, y, c = _place()
        me, sibling = (x, y, c), (x, y, 1 - c)
        chips = [(1 - x, y), (x, 1 - y), (1 - x, 1 - y)]

        def blk(a, p):
            return o_refs[a].at[4 * p[0] + 2 * p[1] + p[2]]

        def copy(a, k, block, to, src=None):
            return pltpu.make_async_remote_copy(
                src_ref=blk(a, block) if src is None else src, dst_ref=blk(a, block),
                send_sem=send_sems.at[a, k], recv_sem=recv_sems.at[a, k], device_id=to, device_id_type=MESH)

        mine = [pltpu.make_async_copy(x_refs[a], blk(a, me), local_sems.at[a]) for a in range(n)]
        for cp in mine:
            cp.start()
        first = []
        for a in range(n):
            first.append(copy(a, 0, me, sibling, src=x_refs[a]))
            first += [copy(a, 1 + j, me, (*chip, c), src=x_refs[a]) for j, chip in enumerate(chips)]
        for cp in first:
            cp.start()
        passed = []
        for j, chip in enumerate(chips):
            for a in range(n):
                copy(a, 1 + j, (*chip, c), me).wait_recv()
                cp = copy(a, 4 + j, (*chip, c), sibling)
                cp.start()
                passed.append(cp)
        for a in range(n):
            copy(a, 0, sibling, me).wait_recv()
        for j, chip in enumerate(chips):
            for a in range(n):
                copy(a, 4 + j, (*chip, 1 - c), me).wait_recv()
        for cp in first + passed:
            cp.wait_send()
        for cp in mine:
            cp.wait()

    spec = pl.BlockSpec(memory_space=space)
    return pl.pallas_call(
        body, name=name, out_shape=[_sds((N_DEV,) + a.shape, a.dtype) for a in xs],
        in_specs=[spec] * n, out_specs=[spec] * n,
        scratch_shapes=[pltpu.SemaphoreType.DMA((n, 7)), pltpu.SemaphoreType.DMA((n, 7)),
                        pltpu.SemaphoreType.DMA((n,))])(*xs)


_HBM_SPEC = pl.BlockSpec(memory_space=pltpu.HBM)
_SEM_SPEC = pl.BlockSpec(memory_space=pltpu.SEMAPHORE)
_EFFECT = pltpu.SideEffectType.DATAFLOW_SIDE_EFFECTING


def _descriptors(plan, src_refs, land_refs, send_sems, recv_sems, which=None):
    return [pltpu.make_async_remote_copy(src_ref=s, dst_ref=d, send_sem=send_sems.at[k], recv_sem=recv_sems.at[k],
                                         device_id=dev, device_id_type=MESH)
            for k, (s, d, dev) in enumerate(plan(src_refs, land_refs)) if which is None or k in which]


def _split_start(name, plan, n, srcs, lands, after):
    ns, nb = len(srcs), len(srcs) + len(lands)
    after = list(after) if isinstance(after, (list, tuple)) else [after]
    sems = nb + len(after)

    def body(*refs):
        for cp in _descriptors(plan, refs[:ns], refs[ns:nb], refs[sems], refs[sems + 1]):
            cp.start()
        refs[-1][...] = jnp.zeros_like(refs[-1])

    bufs = [pltpu.with_memory_space_constraint(a, pltpu.HBM) for a in list(srcs) + list(lands)]
    outs = pl.pallas_call(
        body, name=name,
        out_shape=(pltpu.SemaphoreType.DMA((n,)), pltpu.SemaphoreType.DMA((n,)))
        + tuple(pltpu.HBM(a.shape, a.dtype) for a in bufs) + (_sds((8, LANES)),),
        in_specs=[_HBM_SPEC] * nb + [pl.BlockSpec(memory_space=pl.ANY)] * len(after),
        out_specs=(_SEM_SPEC, _SEM_SPEC) + (_HBM_SPEC,) * nb + (pl.BlockSpec(memory_space=pltpu.VMEM),),
        input_output_aliases={i: 2 + i for i in range(nb)},
        compiler_params=pltpu.CompilerParams(has_side_effects=_EFFECT))(*bufs, *after)
    return dict(send=outs[0], recv=outs[1], srcs=list(outs[2:2 + ns]), lands=list(outs[2 + ns:2 + nb]), token=outs[-1])


def _split_wait(name, plan, flight, which, after):
    srcs, lands = flight["srcs"], flight["lands"]
    ns, nb = len(srcs), len(srcs) + len(lands)

    def body(*refs):
        for cp in _descriptors(plan, refs[:ns], refs[ns:nb], refs[nb], refs[nb + 1], set(which)):
            cp.wait_send()
            cp.wait_recv()

    outs = pl.pallas_call(
        body, name=name, out_shape=tuple(pltpu.HBM(a.shape, a.dtype) for a in srcs + lands),
        in_specs=[_HBM_SPEC] * nb + [_SEM_SPEC, _SEM_SPEC, pl.BlockSpec(memory_space=pl.ANY)],
        out_specs=(_HBM_SPEC,) * nb, input_output_aliases={i: i for i in range(nb)},
        compiler_params=pltpu.CompilerParams(has_side_effects=_EFFECT))(*srcs, *lands, flight["send"], flight["recv"],
                                                                       after)
    return dict(flight, srcs=list(outs[:ns]), lands=list(outs[ns:nb]))


GATHER_PEERS = N_DEV - 1


def _gather_plan(items):
    def plan(src_refs, land_refs):
        x, y, c = _place()
        me = 4 * x + 2 * y + c
        out = []
        for a, l in items:
            for r in range(1, N_DEV):
                peer = (1 - x if r & 4 else x, 1 - y if r & 2 else y, 1 - c if r & 1 else c)
                out.append((src_refs[a].at[l], land_refs[a].at[me, l], peer))
        return out

    return plan


def _pair_plan(narr):
    def plan(src_refs, land_refs):
        x, y, c = _place()
        return [(src_refs[a].at[2 * q + (1 - c)], land_refs[a].at[q], (x, y, 1 - c))
                for a in range(narr) for q in range(4)]

    return plan


def _chip_plan(narr):
    def plan(src_refs, land_refs):
        x, y, c = _place()
        chips = [(1 - x, y), (x, 1 - y), (1 - x, 1 - y)]
        return [(src_refs[a].at[r], land_refs[a].at[r], (*chip, c)) for a in range(narr) for r, chip in enumerate(chips)]

    return plan


class _GradReducer:
    def __init__(self, tag, names, w, mom, var, place, others):
        self.tag, self.names, self.w, self.mom, self.var, self.place = tag, names, w, mom, var, place
        self.others = others
        self.outs = {k: [lax.empty(w[k].shape, F32) for _ in range(4)] for k in names}
        self.n = len(names)

    def start(self, l, grads):
        self.l, self.xs = l, [grads[k] for k in self.names]
        lands = [lax.empty((4,) + a.shape[1:], F32) for a in self.xs]
        self.pair = _split_start(f"{self.tag}_pair_start{l}", _pair_plan(self.n), 4 * self.n, self.xs, lands, ())
        return self.pair["token"][0, 0]

    def middle(self, after):
        self.pair = _split_wait(f"{self.tag}_pair_wait{self.l}", _pair_plan(self.n), self.pair, range(4 * self.n),
                                after)
        self.xs, self.ps = self.pair["srcs"], self.pair["lands"]
        ys = [_pair_add(x, p, self.others) for x, p in zip(self.xs, self.ps)]
        lands = [lax.empty((3,) + a.shape[1:], BF16) for a in ys]
        self.chip = _split_start(f"{self.tag}_chip_start{self.l}", _chip_plan(self.n), 3 * self.n, ys, lands, ())
        return self.chip["token"][0, 0]

    def finish(self, after):
        chip = _split_wait(f"{self.tag}_chip_wait{self.l}", _chip_plan(self.n), self.chip, range(3 * self.n), after)
        for k, x, p, q in zip(self.names, self.xs, self.ps, chip["lands"]):
            self.outs[k] = _reduce_adam(x, p, q, self.place, self.w[k], self.mom[k], self.var[k], self.l, self.outs[k])


def _size(shape):
    size = 1
    for d in shape:
        size *= d
    return size


def _slab_rows(shape):
    return -(-_size(shape) // (8 * LANES)) * 8


def _pack(arrs):
    parts = []
    for a in arrs:
        flat = a.reshape(-1).astype(F32)
        parts.append(jnp.pad(flat, (0, _slab_rows(a.shape) * LANES - flat.shape[0])).reshape(-1, LANES))
    return jnp.concatenate(parts, axis=0)


def _unpack(slab, shapes):
    out, off = [], 0
    for shp in shapes:
        rows = _slab_rows(shp)
        out.append(slab[off:off + rows].reshape(-1)[:_size(shp)].reshape(shp))
        off += rows
    return out


def _dense_blocks(w):
    eye = jnp.eye(LRU_BLOCKS, dtype=w.dtype)
    return (eye[:, None, :, None] * w[:, :, None, :]).reshape(LRU_W, LRU_W)


def _diag_blocks(dense):
    on_diagonal = jnp.eye(LRU_BLOCKS, dtype=bool)[:, None, :, None]
    return jnp.sum(jnp.where(on_diagonal, dense.reshape(LRU_BLOCKS, LRU_BLOCK, LRU_BLOCKS, LRU_BLOCK), 0.0), axis=2)


def _alpha_lanes(v):
    return jnp.zeros((1, BA_PAD), F32).at[0, HEADS:2 * HEADS].set(v)


def _local_step(x, target, mod, p, fetch, reducers=None):
    nl = mod.shape[0]
    row = lambda v: v.reshape(1, -1)
    masks = _gdn_masks()
    saved = []
    xc = x
    for l in range(nl):
        win, wba, lin = fetch(l, "in", xc)
        mv = [row(mod[l, k * D_MODEL:(k + 1) * D_MODEL]) for k in range(N_MOD)]
        sh1, sc1, g1, sh2, sc2, g2 = mv
        nw1, nw2 = row(p["norm_mix_w"][l]), row(p["norm_mlp_w"][l])
        wa, wx = _dense_blocks(p["lru_gate_a_w"][l]).astype(BF16), _dense_blocks(p["lru_gate_x_w"][l]).astype(BF16)
        lru_args = (p["lru_conv_w"][l], row(p["lru_conv_b"][l]), wa, wx, row(p["lru_gate_a_b"][l]),
                    row(p["lru_gate_x_b"][l]), row(p["lru_lambda"][l]), row(p["lru_norm_w"][l]))
        gdn_args = (p["gdn_conv_w"][l], _alpha_lanes(p["gdn_a_log"][l]), _alpha_lanes(p["gdn_dt_bias"][l]), masks)
        gnw = row(p["gdn_norm_w"][l])
        proj, ba = _inproj_fwd(xc, nw1, sc1, sh1, win, wba, lin)
        ol, hs = _lru_fwd(proj, *lru_args)
        *prep, tinv = _gdn_prep_fwd(proj, ba, *gdn_args)
        og, st = _gdn_scan_fwd(prep, proj, gnw)
        wo, wup, wdn = fetch(l, "rest", og)
        x1, mix, ff, x2 = _out_mlp_fwd(ol, og, xc, wo, g1, nw2, sc2, sh2, g2, wup, wdn, l)
        saved.append(dict(x=xc, mv=mv, nw1=nw1, nw2=nw2, lru_args=lru_args, gdn_args=gdn_args, gnw=gnw, proj=proj,
                          ba=ba, ol=ol, hs=hs, prep=prep, tinv=tinv, og=og, st=st, x1=x1, mix=mix, ff=ff,
                          win=win, wba=wba, lin=lin))
        xc = x2

    dx, frows = _final_fwd_bwd(xc, target, row(p["final_norm_w"]))
    loss_part = frows[1, 0]
    small = {k: [None] * nl for k in ("norm_mix_w", "norm_mlp_w", "lru_conv_w", "lru_conv_b", "lru_gate_a_w",
                                      "lru_gate_a_b", "lru_gate_x_w", "lru_gate_x_b", "lru_lambda", "lru_norm_w",
                                      "gdn_conv_w", "gdn_a_log", "gdn_dt_bias", "gdn_norm_w")}
    fc = D_FF // N_DEV
    big = [None] * nl
    dmod = [None] * nl
    mlp_red, mix_red = reducers or (None, None)
    busy = False
    for l in reversed(range(nl)):
        sv = saved[l]
        sh1, sc1, g1, sh2, sc2, g2 = sv["mv"]
        gnw = sv["gnw"]
        if busy:
            g2 = g2 + started
        act, dup, h2b, dffb, dx1, rows2 = _mlp_bwd(dx, sv["x1"], sv["ff"], sv["nw2"], sc2, sh2, g2, wup, wdn, l)
        if busy:
            g1 = g1 + mix_red.middle(dx1)
        g_up = _tn_matmul(h2b, dup, "grad_w_up", out=lax.empty((1, N_DEV, D_MODEL, fc), F32), blocked=True)[0]
        g_down = _tn_matmul(act, dffb, "grad_w_down", out=lax.empty((1, D_FF, D_MODEL), F32))
        g_down = g_down.reshape(N_DEV, fc, D_MODEL)
        if mlp_red is not None:
            g1 = g1 + mlp_red.start(l, dict(w_up=g_up, w_down=g_down))
        dmix, dol, dog, rows1 = _outproj_bwd(dx1, sv["mix"], g1, wo, l)
        g_out = _tn_matmul(sv["ol"], dmix, "grad_w_out_lru", out=lax.empty((1, D_MODEL, D_MODEL), F32))
        g_out = _tn_matmul(sv["og"], dmix, "grad_w_out_gdn", out=g_out, row_block=1)
        dpl, dwa, dwx, lrows = _lru_bwd(dol, sv["proj"], sv["hs"], *sv["lru_args"])
        if mlp_red is not None:
            gnw = gnw + mlp_red.middle(dpl)
        *cts, dpz, gnrow = _gdn_scan_bwd(dog, sv["prep"], sv["st"], sv["proj"], gnw)
        dpq, dba, dcw, dpar = _gdn_prep_bwd(cts, sv["tinv"], sv["proj"], sv["ba"], *sv["gdn_args"])
        dx, hb, rows0 = _inproj_bwd(dpl, dpq, dpz, dba, sv["x"], dx1, sv["nw1"], sc1, sh1, sv["win"], sv["wba"],
                                    sv["lin"])
        if busy:
            mix_red.finish(dx)
        if mlp_red is not None:
            mlp_red.finish(dx)
        dproj = jnp.concatenate([dpl, dpq, dpz, dba], axis=1)
        g_in = jnp.transpose(_tn_matmul(hb, dproj, "grad_w_in")[:, :IN_COLS].reshape(
            D_MODEL, N_DEV, IN_COLS // N_DEV), (1, 0, 2))
        big[l] = dict(w_in=g_in, w_out=g_out.reshape(N_DEV, D_MODEL // N_DEV, D_MODEL), w_up=g_up,
                      w_down=g_down)
        if mix_red is not None:
            started, busy = mix_red.start(l, big[l]), True
        dmod[l] = jnp.concatenate([rows0[0], rows0[1], rows1[0], rows2[0], rows2[1], rows2[2]])
        small["norm_mix_w"][l], small["norm_mlp_w"][l] = rows0[2], rows2[3]
        small["lru_conv_w"][l], small["lru_conv_b"][l] = lrows[8:8 + CONV_K], lrows[0]
        small["lru_gate_a_w"][l], small["lru_gate_x_w"][l] = _diag_blocks(dwa), _diag_blocks(dwx)
        small["lru_gate_a_b"][l], small["lru_gate_x_b"][l] = lrows[1], lrows[2]
        small["lru_lambda"][l], small["lru_norm_w"][l] = lrows[3], lrows[4]
        small["gdn_conv_w"][l] = dcw
        small["gdn_a_log"][l], small["gdn_dt_bias"][l] = dpar[0, HEADS:2 * HEADS], dpar[1, HEADS:2 * HEADS]
        small["gdn_norm_w"][l] = gnrow[0]
    small = {k: jnp.stack(v) for k, v in small.items()}
    small["final_norm_w"] = frows[0]
    return loss_part, dx, big, small, jnp.stack(dmod)


SMALL_REPLICATED = ("norm_mix_w", "norm_mlp_w", "b_mod", "lru_conv_b", "lru_gate_a_w", "lru_gate_a_b", "lru_gate_x_w",
                    "lru_gate_x_b", "lru_lambda", "lru_norm_w", "gdn_a_log", "gdn_dt_bias", "gdn_norm_w",
                    "final_norm_w")
SMALL_SHARDED = ("lru_conv_w", "gdn_conv_w")
WEIGHT_ORDER = ("norm_mix_w", "norm_mlp_w", "w_mod", "b_mod", "w_in", "lru_conv_w", "lru_conv_b", "lru_gate_a_w",
                "lru_gate_a_b", "lru_gate_x_w", "lru_gate_x_b", "lru_lambda", "lru_norm_w", "gdn_conv_w", "gdn_a_log",
                "gdn_dt_bias", "gdn_norm_w", "w_out", "w_up", "w_down", "final_norm_w")


def kernel(x, c, norm_mix_w, norm_mlp_w, w_mod, b_mod, w_in, lru_conv_w, lru_conv_b, lru_gate_a_w, lru_gate_a_b, lru_gate_x_w, lru_gate_x_b, lru_lambda, lru_norm_w, gdn_conv_w, gdn_a_log, gdn_dt_bias, gdn_norm_w, w_out, w_up, w_down, final_norm_w, loss_target, m_norm_mix_w, m_norm_mlp_w, m_w_mod, m_b_mod, m_w_in, m_lru_conv_w, m_lru_conv_b, m_lru_gate_a_w, m_lru_gate_a_b, m_lru_gate_x_w, m_lru_gate_x_b, m_lru_lambda, m_lru_norm_w, m_gdn_conv_w, m_gdn_a_log, m_gdn_dt_bias, m_gdn_norm_w, m_w_out, m_w_up, m_w_down, m_final_norm_w, v_norm_mix_w, v_norm_mlp_w, v_w_mod, v_b_mod, v_w_in, v_lru_conv_w, v_lru_conv_b, v_lru_gate_a_w, v_lru_gate_a_b, v_lru_gate_x_w, v_lru_gate_x_b, v_lru_lambda, v_lru_norm_w, v_gdn_conv_w, v_gdn_a_log, v_gdn_dt_bias, v_gdn_norm_w, v_w_out, v_w_up, v_w_down, v_final_norm_w):
    args = dict(locals())
    w = {k: args[k] for k in WEIGHT_ORDER}
    mom = {k: args["m_" + k] for k in WEIGHT_ORDER}
    var = {k: args["v_" + k] for k in WEIGHT_ORDER}
    nl = w_in.shape[0]
    px, py, pc = _place()
    me = 4 * px + 2 * py + pc
    other_chips = [2 * (1 - px) + py, 2 * px + (1 - py), 2 * (1 - px) + (1 - py)]
    others = jnp.stack([2 * q + pc for q in other_chips] + other_chips).astype(jnp.int32)

    shapes0 = [c.shape, lru_conv_w.shape, gdn_conv_w.shape]
    (g0,) = _all_gather([_pack([c, lru_conv_w, gdn_conv_w])], "gather_cond", pltpu.VMEM)
    per_dev = [_unpack(g0[d], shapes0) for d in range(N_DEV)]
    c_all = jnp.concatenate([pd[0] for pd in per_dev], axis=0)
    lru_conv_full = jnp.concatenate([pd[1] for pd in per_dev], axis=-1)
    gdn_conv_full = jnp.concatenate([pd[2] for pd in per_dev], axis=-1)

    cols = w_mod.shape[2]
    bmod_cols = lax.dynamic_slice_in_dim(b_mod, me * cols, cols, axis=1).reshape(nl, 1, cols)
    mod_cols = _mod_local(c_all, w_mod, bmod_cols)
    (g1,) = _all_gather([mod_cols.reshape(nl * N_DEV, cols)], "gather_mod", pltpu.VMEM)
    g1 = g1.reshape(N_DEV, nl, N_DEV, cols)
    mod = jnp.transpose(lax.dynamic_index_in_dim(g1, me, axis=2, keepdims=False), (1, 0, 2)).reshape(nl, N_DEV * cols)

    shards = [a.astype(BF16) for a in (w_in, w_out, w_up, w_down)]
    (first_in,) = _all_gather([shards[0][:1]], "gather_w_in_first", pl.ANY)
    items = [(a, 0) for a in (1, 2, 3)] + [(a, l) for l in range(1, nl) for a in range(4)]
    plan = _gather_plan(items)
    lands = [lax.dynamic_update_slice_in_dim(lax.empty((N_DEV,) + a.shape, BF16), a[None], me, axis=0) for a in shards]
    flight = [_split_start("gather_weights_start", plan, len(items) * GATHER_PEERS, shards, lands, [first_in, mod])]
    mod = mod + flight[0]["token"][0, 0]

    def fetch(l, what, after):
        wanted = [k for k, (a, ll) in enumerate(items) if ll == l and (a == 0) == (what == "in")]
        if wanted:
            flight[0] = _split_wait(f"gather_weights_wait_{what}{l}", plan, flight[0],
                                    [k * GATHER_PEERS + r for k in wanted for r in range(GATHER_PEERS)], after)
        gin, gout, gup, gdn = flight[0]["lands"]
        if what == "rest":
            return gout, gup, gdn
        gin = first_in[:, 0] if l == 0 else gin[:, l]
        win = jnp.transpose(gin, (1, 0, 2)).reshape(1, D_MODEL, IN_COLS)
        wba = jnp.pad(win[:, :, IN_MAIN:], ((0, 0), (0, 0), (0, BA_PAD - (IN_COLS - IN_MAIN))))
        return win, wba, 0

    p = dict(w)
    p["lru_conv_w"], p["gdn_conv_w"] = lru_conv_full, gdn_conv_full

    place = jnp.stack([me, 2 * px + py]).astype(jnp.int32)
    reducers = (_GradReducer("mlp_grad", ("w_up", "w_down"), w, mom, var, place, others),
                _GradReducer("mix_grad", ("w_in", "w_out"), w, mom, var, place, others))
    loss_part, grad_x, _, small, dmod = _local_step(x[0], loss_target[0], mod, p, fetch, reducers)
    loss = lax.psum(loss_part, MESH_AXES)

    small_names = sorted(small)
    slab = _pack([dmod] + [small[k] for k in small_names])
    (gs,) = _all_gather([slab], "gather_small_grads", pltpu.VMEM)
    chips_started = reducers[1].middle(gs)
    dmod_all = gs[:, :_slab_rows(dmod.shape)].reshape(N_DEV, nl, N_MOD * D_MODEL)
    summed = _unpack(_sum_devices(gs) + chips_started, [dmod.shape] + [small[k].shape for k in small_names])
    grads = dict(zip(small_names, summed[1:]))
    grads["b_mod"] = summed[0]
    for k, width in (("lru_conv_w", LRU_W // N_DEV), ("gdn_conv_w", 3 * GDN_W // N_DEV)):
        grads[k] = lax.dynamic_slice_in_dim(grads[k], me * width, width, axis=2)
    names = SMALL_REPLICATED + SMALL_SHARDED
    shapes = [w[k].shape for k in names]
    dl, nm, nv = _adam_flat(_pack([w[k] for k in names]), _pack([grads[k] for k in names]),
                            _pack([mom[k] for k in names]), _pack([var[k] for k in names]))
    delta = dict(zip(names, _unpack(dl, shapes)))
    new_m = dict(zip(names, _unpack(nm, shapes)))
    new_v = dict(zip(names, _unpack(nv, shapes)))

    dmod_cols = jnp.transpose(lax.dynamic_slice_in_dim(dmod_all, me * cols, cols, axis=2), (1, 0, 2))
    grads["w_mod"], delta["w_mod"], new_m["w_mod"], new_v["w_mod"] = _wmod_update(
        c_all, dmod_cols, w_mod, m_w_mod, v_w_mod)

    reducers[1].finish(new_v["w_mod"])
    for red in reducers:
        for k in red.names:
            grads[k], delta[k], new_m[k], new_v[k] = red.outs[k]

    return (loss, grad_x[None], *[grads[k] for k in WEIGHT_ORDER], *[delta[k] for k in WEIGHT_ORDER],
            *[new_m[k] for k in WEIGHT_ORDER], *[new_v[k] for k in WEIGHT_ORDER])
```

```python
import functools

import jax
import jax.numpy as jnp
from jax import lax
from jax.experimental import pallas as pl
from jax.experimental.pallas import tpu as pltpu

F32 = jnp.float32
BF16 = jnp.bfloat16

D_MODEL = 1024
LRU_W = 512
LRU_BLOCKS = 8
LRU_BLOCK = 64
LRU_C = 8.0
GDN_W = 512
HEADS = 4
HEAD_DIM = 128
CHUNK = 64
STACK = HEADS * CHUNK
CONV_K = 4
D_FF = 4096
N_MOD = 6
IN_COLS = 3080
IN_MAIN = 3072
BA_PAD = 128
EPS = 1e-6
N_DEV = 8
HALO = 8
MLP_BLOCKS = 4
PREP_CHUNKS = 4
SCAN_CHUNKS = 8
LANES = 128
ADAM_LR, ADAM_B1, ADAM_B2, ADAM_EPS, ADAM_WD, ADAM_STEP = 0.001, 0.9, 0.999, 1e-08, 0.01, 10
MESH_AXES = ("x", "y", "c")
MESH = pl.DeviceIdType.MESH

NN = (((1,), (0,)), ((), ()))
NT = (((1,), (1,)), ((), ()))
TN = (((0,), (0,)), ((), ()))


def _bdot(a, b, dims=NN):
    return lax.dot_general(a.astype(BF16), b.astype(BF16), dims, preferred_element_type=F32)


def _sdot(a, b, dims=NN):
    ah, bh = a.astype(BF16), b.astype(BF16)
    al, bl = (a - ah.astype(F32)).astype(BF16), (b - bh.astype(F32)).astype(BF16)
    return _bdot(ah, bh, dims) + (_bdot(al, bh, dims) + _bdot(ah, bl, dims))


def _hdot(a, b, dims=NN):
    return lax.dot_general(a, b, dims, precision=lax.Precision.HIGHEST, preferred_element_type=F32)


def _sds(shape, dtype=F32):
    return jax.ShapeDtypeStruct(tuple(shape), dtype)


def _tile(n, t):
    return min(n, t)


def _call(body, name, grid, in_specs, out_specs, out_shape, scratch=(), vmem_mb=48, prefetch=0, aliases=None):
    params = pltpu.CompilerParams(dimension_semantics=("arbitrary",) * len(grid), vmem_limit_bytes=vmem_mb * 2**20)
    if prefetch:
        spec = pltpu.PrefetchScalarGridSpec(num_scalar_prefetch=prefetch, grid=grid, in_specs=in_specs,
                                            out_specs=out_specs, scratch_shapes=list(scratch))
        return pl.pallas_call(body, name=name, grid_spec=spec, out_shape=out_shape, compiler_params=params,
                              input_output_aliases=aliases or {})
    return pl.pallas_call(body, name=name, grid=grid, in_specs=in_specs, out_specs=out_specs, out_shape=out_shape,
                          scratch_shapes=list(scratch), compiler_params=params, input_output_aliases=aliases or {})


def _tok(t, n, col=0):
    return pl.BlockSpec((t, n), lambda i, *_: (i, col))


def _vec(n):
    return pl.BlockSpec((1, n), lambda *_: (0, 0))


def _whole(a):
    nd = a.ndim
    return pl.BlockSpec(a.shape, lambda *_: (0,) * nd)


def _layer(l, *dims):
    return pl.BlockSpec((1,) + dims, lambda *_: (l,) + (0,) * len(dims))


def _gelu(y):
    c0, c1 = 0.7978845608028654, 0.044715
    return 0.5 * y * (1.0 + jnp.tanh(c0 * (y + c1 * y * y * y)))


def _gelu_grad(y):
    c0, c1 = 0.7978845608028654, 0.044715
    t = jnp.tanh(c0 * (y + c1 * y * y * y))
    return 0.5 * (1.0 + t) + 0.5 * y * (1.0 - t * t) * c0 * (1.0 + 3.0 * c1 * y * y)


def _softplus(v):
    return jnp.maximum(v, 0.0) + jnp.log(1.0 + jnp.exp(-jnp.where(v > 0, v, -v)))


@functools.partial(jax.custom_vjp, nondiff_argnums=(1,))
def _roll_rows(v, s):
    s = s % v.shape[0]
    return pltpu.roll(v, s, axis=0) if s else v


def _roll_rows_fwd(v, s):
    return _roll_rows(v, s), None


def _roll_rows_bwd(s, _, g):
    return (_roll_rows(g, -s),)


_roll_rows.defvjp(_roll_rows_fwd, _roll_rows_bwd)


@jax.custom_vjp
def _drop_halo(v):
    return v[HALO:]


def _drop_halo_fwd(v):
    return v[HALO:], None


def _drop_halo_bwd(_, g):
    return (jnp.concatenate([jnp.zeros((HALO, g.shape[1]), g.dtype), g], axis=0),)


_drop_halo.defvjp(_drop_halo_fwd, _drop_halo_bwd)


@functools.partial(jax.custom_vjp, nondiff_argnums=(1, 2))
def _split(v, n, axis):
    w = v.shape[axis] // n
    return tuple(lax.slice_in_dim(v, k * w, (k + 1) * w, axis=axis) for k in range(n))


def _split_fwd(v, n, axis):
    return _split(v, n, axis), None


def _split_bwd(n, axis, _, gs):
    return (jnp.concatenate(list(gs), axis=axis),)


_split.defvjp(_split_fwd, _split_bwd)


def _conv_taps(xw):
    return [_drop_halo(_roll_rows(xw, CONV_K - 1 - k)) for k in range(CONV_K)]


def _modulated_norm(xv, nw, sc, sh):
    r = lax.rsqrt(jnp.mean(xv * xv, axis=-1, keepdims=True) + EPS)
    n = xv * r * nw
    return n * (1.0 + sc) + sh, n, r


def _modulated_norm_bwd(dh, xv, n, r, nw, sc):
    dn = dh * (1.0 + sc)
    dxn = dn * nw
    dx = r * dxn - xv * (r * r * r) * jnp.mean(dxn * xv, axis=-1, keepdims=True)
    return (dx, jnp.sum(dh, axis=0, keepdims=True), jnp.sum(dh * n, axis=0, keepdims=True),
            jnp.sum(dn * xv * r, axis=0, keepdims=True))


def _inproj_fwd(x, nw, sc, sh, win, wba, l):
    s = x.shape[0]
    t = _tile(s, 512)

    def body(x_ref, nw_ref, sc_ref, sh_ref, win_ref, wba_ref, proj_ref, ba_ref):
        h, _, _ = _modulated_norm(x_ref[...], nw_ref[...], sc_ref[...], sh_ref[...])
        hb = h.astype(BF16)
        proj_ref[...] = _bdot(hb, win_ref[0])
        ba_ref[...] = _bdot(hb, wba_ref[0])

    return _call(body, "inproj_fwd", (s // t,),
                 [_tok(t, D_MODEL), _vec(D_MODEL), _vec(D_MODEL), _vec(D_MODEL), _layer(l, D_MODEL, IN_MAIN),
                  _layer(l, D_MODEL, BA_PAD)],
                 [_tok(t, IN_MAIN), _tok(t, BA_PAD)],
                 [_sds((s, IN_MAIN)), _sds((s, BA_PAD))])(x, nw, sc, sh, win, wba)


def _inproj_bwd(dpl, dpq, dpz, dba, x, dx1, nw, sc, sh, win, wba, l):
    s = x.shape[0]
    t = _tile(s, 512)

    def body(dpl_ref, dpq_ref, dpz_ref, dba_ref, x_ref, dx1_ref, nw_ref, sc_ref, sh_ref, win_ref, wba_ref,
             dx_ref, hb_ref, acc_ref):
        @pl.when(pl.program_id(0) == 0)
        def _():
            acc_ref[...] = jnp.zeros_like(acc_ref)

        dh = (_bdot(dpl_ref[...], win_ref[0, :, 0:2 * LRU_W], NT)
              + _bdot(dpq_ref[...], win_ref[0, :, 2 * LRU_W:2 * LRU_W + 3 * GDN_W], NT)
              + _bdot(dpz_ref[...], win_ref[0, :, 2 * LRU_W + 3 * GDN_W:IN_MAIN], NT)
              + _bdot(dba_ref[...], wba_ref[0], NT))
        xv = x_ref[...]
        h, n, r = _modulated_norm(xv, nw_ref[...], sc_ref[...], sh_ref[...])
        hb_ref[...] = h.astype(BF16)
        dx, dsh, dsc, dnw = _modulated_norm_bwd(dh, xv, n, r, nw_ref[...], sc_ref[...])
        dx_ref[...] = dx1_ref[...] + dx
        acc_ref[0:1, :] += dsh
        acc_ref[1:2, :] += dsc
        acc_ref[2:3, :] += dnw

    return _call(body, "inproj_bwd", (s // t,),
                 [_tok(t, 2 * LRU_W), _tok(t, 3 * GDN_W), _tok(t, GDN_W), _tok(t, BA_PAD), _tok(t, D_MODEL),
                  _tok(t, D_MODEL), _vec(D_MODEL), _vec(D_MODEL), _vec(D_MODEL), _layer(l, D_MODEL, IN_MAIN),
                  _layer(l, D_MODEL, BA_PAD)],
                 [_tok(t, D_MODEL), _tok(t, D_MODEL), pl.BlockSpec((8, D_MODEL), lambda i: (0, 0))],
                 [_sds((s, D_MODEL)), _sds((s, D_MODEL), BF16), _sds((8, D_MODEL))])(
                     dpl, dpq, dpz, dba, x, dx1, nw, sc, sh, win, wba)


def _lru_gates(xw, cw_rows, cb, wa, wx, gab, gxb, lam):
    taps = _conv_taps(xw)
    xr = cb + cw_rows[0] * taps[0] + cw_rows[1] * taps[1] + cw_rows[2] * taps[2] + cw_rows[3] * taps[3]
    xb = xr.astype(BF16)
    r = jax.nn.sigmoid(_bdot(xb, wa) + gab)
    i = jax.nn.sigmoid(_bdot(xb, wx) + gxb)
    z = jnp.exp(-jnp.where(lam > 0, lam, -lam))
    w1 = 1.0 + z
    log1p_z = jnp.where(w1 == 1.0, z, jnp.log(w1) * z / (w1 - 1.0))
    ls = jnp.minimum(lam, 0.0) - log1p_z
    la = LRU_C * r * ls
    a = jnp.exp(la)
    mm_raw = -jnp.tanh(la) * (a * a + 1.0)
    mult = jnp.sqrt(jnp.maximum(mm_raw, 1e-12))
    return dict(taps=taps, xr=xr, r=r, i=i, ls=ls, a=a, mm_raw=mm_raw, mult=mult)


def _lru_specs(s, t, tile_of):
    nh = t // HALO
    xl = pl.BlockSpec((t, LRU_W), lambda i: (tile_of(i), 0))
    yl = pl.BlockSpec((t, LRU_W), lambda i: (tile_of(i), 1))
    hx = pl.BlockSpec((HALO, LRU_W), lambda i: (jnp.maximum(tile_of(i) * nh - 1, 0), 0))
    return xl, yl, hx


def _lru_fwd(proj, cw, cb, wa, wx, gab, gxb, lam, lnw):
    s = proj.shape[0]
    t = _tile(s, 256)
    xl, yl, hx = _lru_specs(s, t, lambda i: i)

    def body(xl_ref, yl_ref, hx_ref, cw_ref, cb_ref, wa_ref, wx_ref, gab_ref, gxb_ref, lam_ref, lnw_ref,
             out_ref, h_ref, a_s, b_s, hc):
        i = pl.program_id(0)

        @pl.when(i == 0)
        def _():
            hc[...] = jnp.zeros_like(hc)

        halo = jnp.where(i > 0, hx_ref[...], 0.0)
        xw = jnp.concatenate([halo, xl_ref[...]], axis=0)
        g = _lru_gates(xw, [cw_ref[k:k + 1, :] for k in range(CONV_K)], cb_ref[...], wa_ref[...], wx_ref[...],
                       gab_ref[...], gxb_ref[...], lam_ref[...])
        a_s[...] = g["a"]
        b_s[...] = g["mult"] * (g["i"] * g["xr"])

        row = lax.broadcasted_iota(jnp.int32, (HALO, LRU_W), 0)

        def group(k, h):
            rows = pl.ds(pl.multiple_of(k * HALO, HALO), HALO)
            a, b = a_s[rows, :], b_s[rows, :]
            for shift in (1, 2, 4):
                earlier = row >= shift
                b = a * jnp.where(earlier, pltpu.roll(b, shift, axis=0), 0.0) + b
                a = a * jnp.where(earlier, pltpu.roll(a, shift, axis=0), 1.0)
            h_ref[rows, :] = a * h + b
            return h_ref[pl.ds(k * HALO + HALO - 1, 1), :]

        hc[...] = lax.fori_loop(0, t // HALO, group, hc[...], unroll=4)
        m = h_ref[...] * _gelu(yl_ref[...])
        out_ref[...] = m * lax.rsqrt(jnp.mean(m * m, axis=-1, keepdims=True) + EPS) * lnw_ref[...]

    return _call(body, "lru_fwd", (s // t,),
                 [xl, yl, hx, _whole(cw), _vec(LRU_W), _whole(wa), _whole(wx)] + [_vec(LRU_W)] * 4,
                 [_tok(t, LRU_W), _tok(t, LRU_W)],
                 [_sds((s, LRU_W)), _sds((s, LRU_W))],
                 scratch=[pltpu.VMEM((t, LRU_W), F32), pltpu.VMEM((t, LRU_W), F32), pltpu.VMEM((1, LRU_W), F32)])(
                     proj, proj, proj, cw, cb, wa, wx, gab, gxb, lam, lnw)


def _lru_bwd(dout, proj, hs, cw, cb, wa, wx, gab, gxb, lam, lnw):
    s = proj.shape[0]
    t = _tile(s, 256)
    nt = s // t
    rev = lambda i: nt - 1 - i
    xl, yl, hx = _lru_specs(s, t, rev)
    nh = t // HALO
    tk = pl.BlockSpec((t, LRU_W), lambda i: (rev(i), 0))
    hh = pl.BlockSpec((HALO, LRU_W), lambda i: (jnp.maximum(rev(i) * nh - 1, 0), 0))

    def body(do_ref, xl_ref, yl_ref, hx_ref, h_ref, hh_ref, cw_ref, cb_ref, wa_ref, wx_ref, gab_ref, gxb_ref,
             lam_ref, lnw_ref, dp_ref, dwa_ref, dwx_ref, rows_ref, dh_s, dhd_s, carry, dxr_next):
        i = pl.program_id(0)
        first_tile = rev(i) == 0

        @pl.when(i == 0)
        def _():
            carry[...] = jnp.zeros_like(carry)
            dxr_next[...] = jnp.zeros_like(dxr_next)
            dwa_ref[...] = jnp.zeros_like(dwa_ref)
            dwx_ref[...] = jnp.zeros_like(dwx_ref)
            rows_ref[...] = jnp.zeros_like(rows_ref)

        halo = jnp.where(first_tile, 0.0, hx_ref[...])
        xw = jnp.concatenate([halo, xl_ref[...]], axis=0)
        cw_rows = [cw_ref[k:k + 1, :] for k in range(CONV_K)]
        lam_v = lam_ref[...]
        g = _lru_gates(xw, cw_rows, cb_ref[...], wa_ref[...], wx_ref[...], gab_ref[...], gxb_ref[...], lam_v)
        a, r, gi, xr, mult = g["a"], g["r"], g["i"], g["xr"], g["mult"]
        hv = h_ref[...]
        yv = yl_ref[...]
        gl = _gelu(yv)
        m = hv * gl
        rn = lax.rsqrt(jnp.mean(m * m, axis=-1, keepdims=True) + EPS)
        dov = do_ref[...]
        dmn = dov * lnw_ref[...]
        rows_ref[4:5, :] += jnp.sum(dov * m * rn, axis=0, keepdims=True)
        dm = rn * dmn - m * (rn * rn * rn) * jnp.mean(dmn * m, axis=-1, keepdims=True)
        dhd_s[...] = dm * gl
        dy = dm * hv * _gelu_grad(yv)
        dh_s[...] = a

        def step(k, c):
            row = t - 1 - k
            d = dhd_s[pl.ds(row, 1), :] + c
            c = dh_s[pl.ds(row, 1), :] * d
            dh_s[pl.ds(row, 1), :] = d
            return c

        carry[...] = lax.fori_loop(0, t, step, carry[...], unroll=8)
        dH = dh_s[...]
        hprev_halo = jnp.where(first_tile, 0.0, hh_ref[...])
        hprev = _drop_halo(_roll_rows(jnp.concatenate([hprev_halo, hv], axis=0), 1))
        da = dH * hprev
        dmult = dH * gi * xr
        di = dH * mult * xr
        dxr = dH * mult * gi
        dla = jnp.where(g["mm_raw"] > 1e-12, dmult * (0.5 / mult) * (-2.0 * a * a), 0.0) + da * a
        dr = dla * (LRU_C * g["ls"])
        sig_neg = jax.nn.sigmoid(-lam_v)
        rows_ref[3:4, :] += jnp.sum(dla * (LRU_C * r), axis=0, keepdims=True) * sig_neg
        drp = dr * r * (1.0 - r)
        dip = di * gi * (1.0 - gi)
        rows_ref[1:2, :] += jnp.sum(drp, axis=0, keepdims=True)
        rows_ref[2:3, :] += jnp.sum(dip, axis=0, keepdims=True)
        xb = xr.astype(BF16)
        drb = drp.astype(BF16)
        dib = dip.astype(BF16)
        dwa_ref[...] += _bdot(xb, drb, TN)
        dwx_ref[...] += _bdot(xb, dib, TN)
        dxr = dxr + _bdot(drb, wa_ref[...], NT) + _bdot(dib, wx_ref[...], NT)
        rows_ref[0:1, :] += jnp.sum(dxr, axis=0, keepdims=True)
        ext = jnp.concatenate([dxr, dxr_next[...]], axis=0)
        dx = cw_rows[CONV_K - 1] * dxr
        for k in range(CONV_K - 1):
            dx = dx + cw_rows[k] * _roll_rows(ext, -(CONV_K - 1 - k))[0:t]
        for k in range(CONV_K):
            rows_ref[8 + k:9 + k, :] += jnp.sum(dxr * g["taps"][k], axis=0, keepdims=True)
        dxr_next[...] = dxr[0:HALO]
        dp_ref[...] = jnp.concatenate([dx, dy], axis=1).astype(BF16)

    acc = lambda shape: pl.BlockSpec(shape, lambda i: (0, 0))
    return _call(body, "lru_bwd", (nt,),
                 [tk, xl, yl, hx, tk, hh, _whole(cw), _vec(LRU_W), _whole(wa), _whole(wx)] + [_vec(LRU_W)] * 4,
                 [pl.BlockSpec((t, 2 * LRU_W), lambda i: (rev(i), 0)), acc((LRU_W, LRU_W)), acc((LRU_W, LRU_W)),
                  acc((16, LRU_W))],
                 [_sds((s, 2 * LRU_W), BF16), _sds((LRU_W, LRU_W)), _sds((LRU_W, LRU_W)), _sds((16, LRU_W))],
                 scratch=[pltpu.VMEM((t, LRU_W), F32), pltpu.VMEM((t, LRU_W), F32), pltpu.VMEM((1, LRU_W), F32),
                          pltpu.VMEM((HALO, LRU_W), F32)])(
                     dout, proj, proj, proj, hs, hs, cw, cb, wa, wx, gab, gxb, lam, lnw)


def _gdn_masks():
    row = lax.broadcasted_iota(jnp.int32, (STACK, STACK), 0)
    col = lax.broadcasted_iota(jnp.int32, (STACK, STACK), 1)
    same = (row // CHUNK) == (col // CHUNK)
    return jnp.stack([(same & (col <= row)).astype(F32), (same & (col < row)).astype(F32), (row == col).astype(F32)])


def _conv_silu(xw, rows):
    taps = _conv_taps(xw)
    y = rows[0] * taps[0] + rows[1] * taps[1] + rows[2] * taps[2] + rows[3] * taps[3]
    return y * jax.nn.sigmoid(y)


def _split3(v):
    hi = v.astype(BF16)
    r1 = v - hi.astype(F32)
    mid = r1.astype(BF16)
    return hi, mid, (r1 - mid.astype(F32)).astype(BF16)


def _mask_dot_raw(mask, v, dims):
    parts = _split3(v)
    d = lambda p: lax.dot_general(mask, p, dims, preferred_element_type=F32)
    return d(parts[0]) + (d(parts[1]) + d(parts[2]))


@jax.custom_vjp
def _mask_dot(mask, v):
    return _mask_dot_raw(mask, v, NN)


def _mask_dot_fwd(mask, v):
    return _mask_dot_raw(mask, v, NN), mask


def _mask_dot_bwd(mask, ct):
    return jnp.zeros_like(mask), _mask_dot_raw(mask, ct, TN)


_mask_dot.defvjp(_mask_dot_fwd, _mask_dot_bwd)


def _unit_lower_inverse(ns, eye):
    tinvs = [eye + n for n in ns]
    ps = list(ns)
    for _ in range(5):
        ps = [_bdot(p, p) for p in ps]
        tinvs = [t + _bdot(t, p) for t, p in zip(tinvs, ps)]
    return tuple(t.astype(BF16) for t in tinvs)


def _refined(ns, rhss, tinvs, dims):
    x0s = [_bdot(t, r, dims) for t, r in zip(tinvs, rhss)]
    ress = [r - x0 + _sdot(n, x0, dims) for n, r, x0 in zip(ns, rhss, x0s)]
    return tuple(x0 + _bdot(t, res, dims) for t, x0, res in zip(tinvs, x0s, ress))


@jax.custom_vjp
def _unit_lower_solve(ns, rhss, tinvs):
    return _refined(ns, rhss, tinvs, NN)


def _unit_lower_solve_fwd(ns, rhss, tinvs):
    xs = _unit_lower_solve(ns, rhss, tinvs)
    return xs, (ns, tinvs, xs)


def _unit_lower_solve_bwd(res, cts):
    ns, tinvs, xs = res
    ys = _refined(ns, cts, tinvs, TN)
    return (tuple(_bdot(y, x, NT) for y, x in zip(ys, xs)), ys, tuple(jnp.zeros_like(t) for t in tinvs))


_unit_lower_solve.defvjp(_unit_lower_solve_fwd, _unit_lower_solve_bwd)


def _gdn_prep(xqs, xks, xvs, bas, cwq, cwk, cwv, pa, pd, masks, tinvs=None, with_inverse=False):
    lower, strict, eye = masks[0], masks[1], masks[2]
    lower_b = lower.astype(BF16)
    lane = lax.broadcasted_iota(jnp.int32, (CHUNK, LANES), 1)
    each = lambda f, *lists: [f(*vals) for vals in zip(*lists)]
    stack = lambda xw, rows: jnp.concatenate(_split(_conv_silu(xw, rows), HEADS, 1), axis=0)
    qs, ks, vs = (each(lambda xw: stack(xw, cw), xs) for xs, cw in ((xqs, cwq), (xks, cwk), (xvs, cwv)))
    qns = each(lambda q: q * lax.rsqrt(jnp.sum(q * q, axis=-1, keepdims=True) + 1e-6) * (HEAD_DIM ** -0.5), qs)
    kns = each(lambda k: k * lax.rsqrt(jnp.sum(k * k, axis=-1, keepdims=True) + 1e-6), ks)

    def col(a, j):
        return jnp.broadcast_to(jnp.sum(jnp.where(lane == j, a, 0.0), axis=1, keepdims=True), (CHUNK, HEAD_DIM))

    betas = each(lambda ba: jnp.concatenate([col(jax.nn.sigmoid(ba), h) for h in range(HEADS)], axis=0), bas)
    g_heads = each(lambda ba: [col(-jnp.exp(pa) * _softplus(ba + pd), HEADS + h) for h in range(HEADS)], bas)
    gs = each(lambda gh: jnp.concatenate(gh, axis=0), g_heads)
    gls = each(lambda gh: jnp.concatenate([jnp.broadcast_to(jnp.sum(g, axis=0, keepdims=True), (CHUNK, HEAD_DIM))
                                           for g in gh], axis=0), g_heads)
    gcs = each(lambda g: _mask_dot(lower_b, g), gs)

    def decay_of(gc):
        gc_rows = jnp.transpose(gc)
        return jnp.exp((jnp.concatenate([gc, gc], axis=1) - jnp.concatenate([gc_rows, gc_rows], axis=0)) * lower)

    decays = each(decay_of, gcs)
    egcs = each(jnp.exp, gcs)
    kbs = each(lambda kn, beta: kn * beta, kns, betas)
    ns = tuple(each(lambda kb, kn, decay: -(_bdot(kb, kn, NT) * decay * strict), kbs, kns, decays))
    if tinvs is None:
        tinvs = _unit_lower_inverse([lax.stop_gradient(n) for n in ns], eye)
    rhss = tuple(each(lambda v, beta, kb, egc: jnp.concatenate([v * beta, kb * egc], axis=1), vs, betas, kbs, egcs))
    sols = _unit_lower_solve(ns, rhss, tuple(tinvs))
    attns = each(lambda qn, kn, decay: _bdot(qn, kn, NT) * decay * lower, qns, kns, decays)
    outs = []
    for sol, qn, kn, egc, gl, gc, attn, tinv in zip(sols, qns, kns, egcs, gls, gcs, attns, tinvs):
        u, w = _split(sol, 2, 1)
        out = (u, w, qn * egc, kn * jnp.exp(gl - gc), attn, jnp.exp(gl))
        outs.append(out + (tinv,) if with_inverse else out)
    return outs


def _gdn_scan(states, u, w, qd, kt, attn, egl, z, nw):
    us, ws, qds, kts, egls = (_split(a, HEADS, 0) for a in (u, w, qd, kt, egl))
    vn = [us[h] - _bdot(ws[h], states[h]) for h in range(HEADS)]
    o = jnp.concatenate([_bdot(qds[h], states[h]) for h in range(HEADS)], axis=0)
    o = o + _bdot(attn, jnp.concatenate(vn, axis=0))
    new = [states[h] * jnp.concatenate([egls[h], egls[h]], axis=0) + _bdot(kts[h], vn[h], TN) for h in range(HEADS)]
    on = o * lax.rsqrt(jnp.mean(o * o, axis=-1, keepdims=True) + EPS) * nw
    return new, on * (z * jax.nn.sigmoid(z))


def _gdn_in_specs(step_of, chunks):
    nh = chunks * CHUNK // HALO
    main = [pl.BlockSpec((chunks * CHUNK, GDN_W), functools.partial(lambda col, i: (step_of(i), col), col))
            for col in (2, 3, 4)]
    halo = [pl.BlockSpec((HALO, GDN_W), functools.partial(lambda col, i: (jnp.maximum(step_of(i) * nh - 1, 0), col),
                                                         col)) for col in (2, 3, 4)]
    return main, halo


def _stk(width, step_of, chunks=1):
    return pl.BlockSpec((chunks * STACK, width), lambda i: (step_of(i), 0))


def _chunk_inputs(main_refs, halo_refs, k, first_step):
    rows = slice(k * CHUNK, (k + 1) * CHUNK)
    if k == 0:
        halos = [jnp.where(first_step, 0.0, h[...]) for h in halo_refs]
    else:
        halos = [m[k * CHUNK - HALO:k * CHUNK, :] for m in main_refs]
    return [jnp.concatenate([h, m[rows, :]], axis=0) for h, m in zip(halos, main_refs)]


def _gdn_prep_fwd(proj, ba, cw, pa, pd, masks):
    s = proj.shape[0]
    nc = s // CHUNK
    per = min(PREP_CHUNKS, nc)
    main, halo = _gdn_in_specs(lambda i: i, per)

    def body(xq_ref, xk_ref, xv_ref, hq_ref, hk_ref, hv_ref, ba_ref, cw_ref, pa_ref, pd_ref, mk_ref, *out_refs):
        first_step = pl.program_id(0) == 0
        rows = [[cw_ref[k:k + 1, j * GDN_W:(j + 1) * GDN_W] for k in range(CONV_K)] for j in range(3)]
        cst = [mk_ref[0], mk_ref[1], mk_ref[2]]
        xs = [_chunk_inputs((xq_ref, xk_ref, xv_ref), (hq_ref, hk_ref, hv_ref), k, first_step) for k in range(per)]
        outs = _gdn_prep([x[0] for x in xs], [x[1] for x in xs], [x[2] for x in xs],
                         [ba_ref[k * CHUNK:(k + 1) * CHUNK, :] for k in range(per)], rows[0], rows[1], rows[2],
                         pa_ref[...], pd_ref[...], cst, with_inverse=True)
        for k, out in enumerate(outs):
            for ref, val in zip(out_refs, out):
                ref[k * STACK:(k + 1) * STACK, :] = val.astype(ref.dtype)

    ident = lambda i: i
    stacked = lambda dt: _sds((nc * STACK, HEAD_DIM), dt)
    wide, thin = _stk(STACK, ident, per), _stk(HEAD_DIM, ident, per)
    return _call(body, "gdn_prep_fwd", (nc // per,),
                 main + halo + [_tok(per * CHUNK, BA_PAD), _whole(cw), _vec(BA_PAD), _vec(BA_PAD), _whole(masks)],
                 [thin] * 4 + [wide, thin, wide],
                 [stacked(F32), stacked(BF16), stacked(BF16), stacked(BF16), _sds((nc * STACK, STACK), BF16),
                  stacked(F32), _sds((nc * STACK, STACK), BF16)])(
                     proj, proj, proj, proj, proj, proj, ba, cw, pa, pd, masks)


def _gdn_prep_bwd(cts, tinv, proj, ba, cw, pa, pd, masks):
    s = proj.shape[0]
    nc = s // CHUNK
    per = min(PREP_CHUNKS, nc)
    steps = nc // per
    rev = lambda i: steps - 1 - i
    main, halo = _gdn_in_specs(rev, per)

    def body(du_ref, dw_ref, dqd_ref, dkt_ref, dattn_ref, degl_ref, tinv_ref, xq_ref, xk_ref, xv_ref, hq_ref, hk_ref,
             hv_ref, ba_ref, cw_ref, pa_ref, pd_ref, mk_ref, dp_ref, dba_ref, dcw_ref, dpar_ref, carry):
        i = pl.program_id(0)
        first_step = rev(i) == 0

        @pl.when(i == 0)
        def _():
            carry[...] = jnp.zeros_like(carry)
            dcw_ref[...] = jnp.zeros_like(dcw_ref)
            dpar_ref[...] = jnp.zeros_like(dpar_ref)

        rows = [[cw_ref[k:k + 1, j * GDN_W:(j + 1) * GDN_W] for k in range(CONV_K)] for j in range(3)]
        cst = [mk_ref[0], mk_ref[1], mk_ref[2]]
        xs = [_chunk_inputs((xq_ref, xk_ref, xv_ref), (hq_ref, hk_ref, hv_ref), k, first_step) for k in range(per)]
        stks = [slice(k * STACK, (k + 1) * STACK) for k in range(per)]
        tinvs = [tinv_ref[stk, :] for stk in stks]
        fn = lambda xqs, xks, xvs, bs, rq, rk, rv, a, d: _gdn_prep(xqs, xks, xvs, bs, rq, rk, rv, a, d, cst, tinvs=tinvs)
        _, vjp = jax.vjp(fn, [x[0] for x in xs], [x[1] for x in xs], [x[2] for x in xs],
                         [ba_ref[k * CHUNK:(k + 1) * CHUNK, :] for k in range(per)], rows[0], rows[1], rows[2],
                         pa_ref[...], pd_ref[...])
        dxqs, dxks, dxvs, dbas, drq, drk, drv, dpa, dpd = vjp(
            [tuple(ref[stk, :] for ref in (du_ref, dw_ref, dqd_ref, dkt_ref, dattn_ref, degl_ref)) for stk in stks])
        dxws = [jnp.concatenate(parts, axis=1) for parts in zip(dxqs, dxks, dxvs)]
        for k in range(per):
            dba_ref[k * CHUNK:(k + 1) * CHUNK, :] = dbas[k].astype(BF16)
        for j, dr in enumerate((drq, drk, drv)):
            for kk in range(CONV_K):
                dcw_ref[kk:kk + 1, j * GDN_W:(j + 1) * GDN_W] += dr[kk]
        dpar_ref[0:1, :] += dpa
        dpar_ref[1:2, :] += dpd
        pad = jnp.zeros((CHUNK - HALO, 3 * GDN_W), F32)
        for k in range(per):
            late = carry[...] if k == per - 1 else dxws[k + 1][0:HALO]
            dp_ref[k * CHUNK:(k + 1) * CHUNK, :] = (dxws[k][HALO:] + jnp.concatenate([pad, late], axis=0)).astype(BF16)
        carry[...] = dxws[0][0:HALO]

    acc = lambda shape: pl.BlockSpec(shape, lambda i: (0, 0))
    wide, thin = _stk(STACK, rev, per), _stk(HEAD_DIM, rev, per)
    return _call(body, "gdn_prep_bwd", (steps,),
                 [thin] * 4 + [wide, thin, wide] + main + halo
                 + [pl.BlockSpec((per * CHUNK, BA_PAD), lambda i: (rev(i), 0)), _whole(cw), _vec(BA_PAD), _vec(BA_PAD),
                    _whole(masks)],
                 [pl.BlockSpec((per * CHUNK, 3 * GDN_W), lambda i: (rev(i), 0)),
                  pl.BlockSpec((per * CHUNK, BA_PAD), lambda i: (rev(i), 0)), acc((CONV_K, 3 * GDN_W)),
                  acc((8, BA_PAD))],
                 [_sds((s, 3 * GDN_W), BF16), _sds((s, BA_PAD), BF16), _sds((CONV_K, 3 * GDN_W)), _sds((8, BA_PAD))],
                 scratch=[pltpu.VMEM((HALO, 3 * GDN_W), F32)])(
                     *cts, tinv, proj, proj, proj, proj, proj, proj, ba, cw, pa, pd, masks)


def _stack_heads(v):
    return jnp.concatenate(_split(v, HEADS, 1), axis=0)


def _unstack_heads(v):
    return jnp.concatenate(_split(v, HEADS, 0), axis=1)


def _gdn_scan_fwd(prep, proj, nw):
    s = proj.shape[0]
    nc = s // CHUNK
    per = min(SCAN_CHUNKS, nc)
    ident = lambda i: i
    srows = HEADS * HEAD_DIM

    def body(u_ref, w_ref, qd_ref, kt_ref, attn_ref, egl_ref, z_ref, nw_ref, out_ref, st_ref, state):
        @pl.when(pl.program_id(0) == 0)
        def _():
            state[...] = jnp.zeros_like(state)

        states = [state[h * HEAD_DIM:(h + 1) * HEAD_DIM, :] for h in range(HEADS)]
        for k in range(per):
            stk, tok = slice(k * STACK, (k + 1) * STACK), slice(k * CHUNK, (k + 1) * CHUNK)
            for h in range(HEADS):
                st_ref[k * srows + h * HEAD_DIM:k * srows + (h + 1) * HEAD_DIM, :] = states[h]
            states, out = _gdn_scan(states, u_ref[stk, :], w_ref[stk, :], qd_ref[stk, :], kt_ref[stk, :],
                                    attn_ref[stk, :], egl_ref[stk, :], _stack_heads(z_ref[tok, :]), nw_ref[...])
            out_ref[tok, :] = _unstack_heads(out)
        for h in range(HEADS):
            state[h * HEAD_DIM:(h + 1) * HEAD_DIM, :] = states[h]

    thin, wide = _stk(HEAD_DIM, ident, per), _stk(STACK, ident, per)
    return _call(body, "gdn_scan_fwd", (nc // per,),
                 [thin] * 4 + [wide, thin, _tok(per * CHUNK, GDN_W, col=5), _vec(HEAD_DIM)],
                 [_tok(per * CHUNK, GDN_W), pl.BlockSpec((per * srows, HEAD_DIM), lambda i: (i, 0))],
                 [_sds((s, GDN_W)), _sds((nc * srows, HEAD_DIM))],
                 scratch=[pltpu.VMEM((srows, HEAD_DIM), F32)])(*prep, proj, nw)


def _gdn_scan_bwd(dout, prep, st, proj, nw):
    s = proj.shape[0]
    nc = s // CHUNK
    per = min(SCAN_CHUNKS, nc)
    steps = nc // per
    rev = lambda i: steps - 1 - i
    srows = HEADS * HEAD_DIM

    def body(do_ref, u_ref, w_ref, qd_ref, kt_ref, attn_ref, egl_ref, st_ref, z_ref, nw_ref,
             du_ref, dw_ref, dqd_ref, dkt_ref, dattn_ref, degl_ref, dz_ref, dnw_ref, dstate):
        @pl.when(pl.program_id(0) == 0)
        def _():
            dstate[...] = jnp.zeros_like(dstate)
            dnw_ref[...] = jnp.zeros_like(dnw_ref)

        dnew = [dstate[h * HEAD_DIM:(h + 1) * HEAD_DIM, :] for h in range(HEADS)]
        for k in reversed(range(per)):
            stk, tok = slice(k * STACK, (k + 1) * STACK), slice(k * CHUNK, (k + 1) * CHUNK)
            states = [st_ref[k * srows + h * HEAD_DIM:k * srows + (h + 1) * HEAD_DIM, :] for h in range(HEADS)]
            f32 = lambda ref: ref[stk, :].astype(F32)
            _, vjp = jax.vjp(_gdn_scan, states, u_ref[stk, :], f32(w_ref), f32(qd_ref), f32(kt_ref), f32(attn_ref),
                             egl_ref[stk, :], _stack_heads(z_ref[tok, :]), nw_ref[...])
            dnew, du, dw, dqd, dkt, dattn, degl, dz, dnw = vjp((dnew, _stack_heads(do_ref[tok, :])))
            for ref, val in zip((du_ref, dw_ref, dqd_ref, dkt_ref, dattn_ref, degl_ref),
                                (du, dw, dqd, dkt, dattn, degl)):
                ref[stk, :] = val
            dz_ref[tok, :] = _unstack_heads(dz).astype(BF16)
            dnw_ref[0:1, :] += dnw
        for h in range(HEADS):
            dstate[h * HEAD_DIM:(h + 1) * HEAD_DIM, :] = dnew[h]

    tokr = lambda n, col=0: pl.BlockSpec((per * CHUNK, n), lambda i: (rev(i), col))
    thin, wide = _stk(HEAD_DIM, rev, per), _stk(STACK, rev, per)
    return _call(body, "gdn_scan_bwd", (steps,),
                 [tokr(GDN_W)] + [thin] * 4 + [wide, thin, pl.BlockSpec((per * srows, HEAD_DIM), lambda i: (rev(i), 0)),
                                               tokr(GDN_W, 5), _vec(HEAD_DIM)],
                 [thin] * 4 + [wide, thin, tokr(GDN_W), pl.BlockSpec((8, HEAD_DIM), lambda i: (0, 0))],
                 [_sds((nc * STACK, HEAD_DIM))] * 4 + [_sds((nc * STACK, STACK)), _sds((nc * STACK, HEAD_DIM)),
                                                       _sds((s, GDN_W), BF16), _sds((8, HEAD_DIM))],
                 scratch=[pltpu.VMEM((srows, HEAD_DIM), F32)])(dout, *prep, st, proj, nw)


def _wo_specs(l):
    half = N_DEV // 2
    return [pl.BlockSpec((half, 1, D_MODEL // N_DEV, D_MODEL), functools.partial(lambda k, *_: (k, l, 0, 0), k))
            for k in range(2)]


def _wo_half(ref):
    return ref[:, 0].reshape(ref.shape[0] * ref.shape[2], ref.shape[3])


def _out_mlp_fwd(ol, og, x, wo, g1, nw2, sc2, sh2, g2, wup, wdn, l):
    s = x.shape[0]
    t = _tile(s, 512)
    nj = wup.shape[0] // MLP_BLOCKS
    fc = wup.shape[3]

    def body(ol_ref, og_ref, x_ref, wol_ref, wog_ref, g1_ref, nw_ref, sc_ref, sh_ref, g2_ref, wup_ref, wdn_ref,
             x1_ref, mix_ref, ff_ref, x2_ref, h2_s, acc_s):
        j = pl.program_id(1)

        @pl.when(j == 0)
        def _():
            mix = _bdot(ol_ref[...], _wo_half(wol_ref)) + _bdot(og_ref[...], _wo_half(wog_ref))
            x1 = x_ref[...] + g1_ref[...] * mix
            mix_ref[...] = mix.astype(BF16)
            x1_ref[...] = x1
            h2, _, _ = _modulated_norm(x1, nw_ref[...], sc_ref[...], sh_ref[...])
            h2_s[...] = h2.astype(BF16)
            acc_s[...] = jnp.zeros_like(acc_s)

        part = None
        for b in range(MLP_BLOCKS):
            up = _bdot(h2_s[...], wup_ref[b, 0])
            down = _bdot(jnp.square(jnp.maximum(up, 0.0)), wdn_ref[b, 0])
            part = down if part is None else part + down
        acc_s[...] += part

        @pl.when(j == nj - 1)
        def _():
            ff_ref[...] = acc_s[...].astype(BF16)
            x2_ref[...] = x1_ref[...] + g2_ref[...] * acc_s[...]

    tk = lambda n: pl.BlockSpec((t, n), lambda i, j: (i, 0))
    return _call(body, "out_mlp_fwd", (s // t, nj),
                 [tk(LRU_W), tk(GDN_W), tk(D_MODEL)] + _wo_specs(l) + [_vec(D_MODEL)] * 5
                 + [pl.BlockSpec((MLP_BLOCKS, 1, D_MODEL, fc), lambda i, j: (j, l, 0, 0)),
                    pl.BlockSpec((MLP_BLOCKS, 1, fc, D_MODEL), lambda i, j: (j, l, 0, 0))],
                 [tk(D_MODEL)] * 4,
                 [_sds((s, D_MODEL)), _sds((s, D_MODEL), BF16), _sds((s, D_MODEL), BF16), _sds((s, D_MODEL))],
                 scratch=[pltpu.VMEM((t, D_MODEL), BF16), pltpu.VMEM((t, D_MODEL), F32)])(
                     ol, og, x, wo, wo, g1, nw2, sc2, sh2, g2, wup, wdn)


def _mlp_bwd(dx2, x1, ff, nw2, sc2, sh2, g2, wup, wdn, l):
    s = x1.shape[0]
    t = _tile(s, 512)
    nj = wup.shape[0] // MLP_BLOCKS
    fc = wup.shape[3]

    def body(dx2_ref, x1_ref, ff_ref, nw_ref, sc_ref, sh_ref, g2_ref, wup_ref, wdn_ref,
             act_ref, dup_ref, h2_ref, dff_ref, dx1_ref, rows_ref, dh2_s):
        i, j = pl.program_id(0), pl.program_id(1)

        @pl.when((i == 0) & (j == 0))
        def _():
            rows_ref[...] = jnp.zeros_like(rows_ref)

        @pl.when(j == 0)
        def _():
            h2, _, _ = _modulated_norm(x1_ref[...], nw_ref[...], sc_ref[...], sh_ref[...])
            h2_ref[...] = h2.astype(BF16)
            dx2 = dx2_ref[...]
            dff_ref[...] = (dx2 * g2_ref[...]).astype(BF16)
            rows_ref[2:3, :] += jnp.sum(dx2 * ff_ref[...].astype(F32), axis=0, keepdims=True)
            dh2_s[...] = jnp.zeros_like(dh2_s)

        part = None
        for b in range(MLP_BLOCKS):
            cols = slice(b * fc, (b + 1) * fc)
            up = _bdot(h2_ref[...], wup_ref[b, 0])
            ru = jnp.maximum(up, 0.0)
            act_ref[:, cols] = (ru * ru).astype(BF16)
            dup = (_bdot(dff_ref[...], wdn_ref[b, 0], NT) * (2.0 * ru)).astype(BF16)
            dup_ref[:, cols] = dup
            back = _bdot(dup, wup_ref[b, 0], NT)
            part = back if part is None else part + back
        dh2_s[...] += part

        @pl.when(j == nj - 1)
        def _():
            xv = x1_ref[...]
            _, n, r = _modulated_norm(xv, nw_ref[...], sc_ref[...], sh_ref[...])
            dx, dsh, dsc, dnw = _modulated_norm_bwd(dh2_s[...], xv, n, r, nw_ref[...], sc_ref[...])
            dx1_ref[...] = dx2_ref[...] + dx
            rows_ref[0:1, :] += dsh
            rows_ref[1:2, :] += dsc
            rows_ref[3:4, :] += dnw

    tk = lambda n: pl.BlockSpec((t, n), lambda i, j: (i, 0))
    tj = pl.BlockSpec((t, MLP_BLOCKS * fc), lambda i, j: (i, j))
    return _call(body, "mlp_bwd", (s // t, nj),
                 [tk(D_MODEL)] * 3 + [_vec(D_MODEL)] * 4
                 + [pl.BlockSpec((MLP_BLOCKS, 1, D_MODEL, fc), lambda i, j: (j, l, 0, 0)),
                    pl.BlockSpec((MLP_BLOCKS, 1, fc, D_MODEL), lambda i, j: (j, l, 0, 0))],
                 [tj, tj, tk(D_MODEL), tk(D_MODEL), tk(D_MODEL), pl.BlockSpec((8, D_MODEL), lambda i, j: (0, 0))],
                 [_sds((s, D_FF), BF16), _sds((s, D_FF), BF16), _sds((s, D_MODEL), BF16),
                  _sds((s, D_MODEL), BF16), _sds((s, D_MODEL)), _sds((8, D_MODEL))],
                 scratch=[pltpu.VMEM((t, D_MODEL), F32)], vmem_mb=56)(dx2, x1, ff, nw2, sc2, sh2, g2, wup, wdn)


def _outproj_bwd(dx1, mix, g1, wo, l):
    s = dx1.shape[0]
    t = _tile(s, 512)

    def body(dx1_ref, mix_ref, g1_ref, wol_ref, wog_ref, dmix_ref, dol_ref, dog_ref, rows_ref):
        @pl.when(pl.program_id(0) == 0)
        def _():
            rows_ref[...] = jnp.zeros_like(rows_ref)

        dx1v = dx1_ref[...]
        rows_ref[0:1, :] += jnp.sum(dx1v * mix_ref[...].astype(F32), axis=0, keepdims=True)
        dmix = (dx1v * g1_ref[...]).astype(BF16)
        dmix_ref[...] = dmix
        dol_ref[...] = _bdot(dmix, _wo_half(wol_ref), NT)
        dog_ref[...] = _bdot(dmix, _wo_half(wog_ref), NT)

    return _call(body, "outproj_bwd", (s // t,),
                 [_tok(t, D_MODEL), _tok(t, D_MODEL), _vec(D_MODEL)] + _wo_specs(l),
                 [_tok(t, D_MODEL), _tok(t, LRU_W), _tok(t, GDN_W), pl.BlockSpec((8, D_MODEL), lambda i: (0, 0))],
                 [_sds((s, D_MODEL), BF16), _sds((s, LRU_W)), _sds((s, GDN_W)), _sds((8, D_MODEL))])(dx1, mix, g1, wo, wo)


def _tn_matmul(a, b, name, out=None, l=0, blocked=False, row_block=0):
    s, m = a.shape
    n = b.shape[1]
    ts, bm = _tile(s, 2048), _tile(m, 1024)
    bn = next(w for w in ((512,) if blocked else (1024, 512, 640, 384, 256, 128)) if n % w == 0)

    def body(a_ref, b_ref, *rest):
        o_ref = rest[-1]

        @pl.when(pl.program_id(2) == 0)
        def _():
            o_ref[...] = jnp.zeros_like(o_ref)

        acc = _bdot(a_ref[...], b_ref[...], TN)
        o_ref[...] += acc.reshape(o_ref.shape)

    in_specs = [pl.BlockSpec((ts, bm), lambda i, j, k: (k, i)), pl.BlockSpec((ts, bn), lambda i, j, k: (k, j))]
    grid = (m // bm, n // bn, s // ts)
    if out is None:
        return _call(body, name, grid, in_specs, pl.BlockSpec((bm, bn), lambda i, j, k: (i, j)), _sds((m, n)))(a, b)
    if blocked:
        out_spec = pl.BlockSpec((1, 1, bm, bn), lambda i, j, k: (l, j, i, 0))
    else:
        out_spec = pl.BlockSpec((1, bm, bn), lambda i, j, k: (l, i + row_block * (m // bm), j))
    return _call(body, name, grid, in_specs + [pl.BlockSpec(memory_space=pl.ANY)], out_spec,
                 _sds(out.shape), aliases={2: 0})(a, b, out)


def _final_fwd_bwd(x, target, fw):
    s = x.shape[0]
    t = _tile(s, 512)

    def body(x_ref, tg_ref, fw_ref, dx_ref, rows_ref):
        @pl.when(pl.program_id(0) == 0)
        def _():
            rows_ref[...] = jnp.zeros_like(rows_ref)

        xv = x_ref[...]
        fwv = fw_ref[...]
        r = lax.rsqrt(jnp.mean(xv * xv, axis=-1, keepdims=True) + EPS)
        err = xv * r * fwv - tg_ref[...]
        part = 0.5 * jnp.sum(jnp.mean(err * err, axis=-1, keepdims=True), axis=0, keepdims=True)
        rows_ref[1:2, :] += jnp.broadcast_to(part, (1, D_MODEL))
        dy = err * (1.0 / D_MODEL)
        rows_ref[0:1, :] += jnp.sum(dy * xv * r, axis=0, keepdims=True)
        dxn = dy * fwv
        dx_ref[...] = r * dxn - xv * (r * r * r) * jnp.mean(dxn * xv, axis=-1, keepdims=True)

    return _call(body, "final_fwd_bwd", (s // t,),
                 [_tok(t, D_MODEL), _tok(t, D_MODEL), _vec(D_MODEL)],
                 [_tok(t, D_MODEL), pl.BlockSpec((8, D_MODEL), lambda i: (0, 0))],
                 [_sds((s, D_MODEL)), _sds((8, D_MODEL))])(x, target, fw)


def _adamw(w, g, m, v):
    m = ADAM_B1 * m + (1.0 - ADAM_B1) * g
    v = ADAM_B2 * v + (1.0 - ADAM_B2) * (g * g)
    m_hat = m / (1.0 - ADAM_B1 ** ADAM_STEP)
    v_hat = v / (1.0 - ADAM_B2 ** ADAM_STEP)
    return -ADAM_LR * (m_hat / (jnp.sqrt(v_hat) + ADAM_EPS) + ADAM_WD * w), m, v


def _mod_local(c_all, wmod, bmod_cols):
    nl, _, cols = wmod.shape

    def body(c_ref, w_ref, b_ref, o_ref):
        cv = c_ref[...]
        o_ref[0] = _bdot(cv * jax.nn.sigmoid(cv), w_ref[0]) + b_ref[0]

    return _call(body, "mod_local", (nl,),
                 [_whole(c_all), pl.BlockSpec((1, D_MODEL, cols), lambda l: (l, 0, 0)),
                  pl.BlockSpec((1, 1, cols), lambda l: (l, 0, 0))],
                 pl.BlockSpec((1, N_DEV, cols), lambda l: (l, 0, 0)), _sds((nl, N_DEV, cols)))(c_all, wmod, bmod_cols)


def _wmod_update(c_all, dmod_cols, w, m, v):
    nl, _, cols = w.shape

    def body(c_ref, d_ref, w_ref, m_ref, v_ref, g_ref, dl_ref, nm_ref, nv_ref):
        cv = c_ref[...]
        g = _bdot(cv * jax.nn.sigmoid(cv), d_ref[0], TN)
        g_ref[0] = g
        dl_ref[0], nm_ref[0], nv_ref[0] = _adamw(w_ref[0], g, m_ref[0], v_ref[0])

    wspec = pl.BlockSpec((1, D_MODEL, cols), lambda l: (l, 0, 0))
    return _call(body, "wmod_update", (nl,),
                 [_whole(c_all), pl.BlockSpec((1, N_DEV, cols), lambda l: (l, 0, 0)), wspec, wspec, wspec],
                 [wspec] * 4, [_sds(w.shape)] * 4)(c_all, dmod_cols, w, m, v)


def _sum_devices(gathered):
    _, r, _ = gathered.shape

    def body(g_ref, o_ref):
        acc = g_ref[0]
        for d in range(1, N_DEV):
            acc = acc + g_ref[d]
        o_ref[...] = acc

    return _call(body, "sum_devices", (1,), [_whole(gathered)], pl.BlockSpec((r, LANES), lambda i: (0, 0)),
                 _sds((r, LANES)))(gathered)


def _adam_flat(w, g, m, v):
    r = w.shape[0]

    def body(w_ref, g_ref, m_ref, v_ref, dl_ref, nm_ref, nv_ref):
        dl_ref[...], nm_ref[...], nv_ref[...] = _adamw(w_ref[...], g_ref[...], m_ref[...], v_ref[...])

    spec = pl.BlockSpec((r, LANES), lambda i: (0, 0))
    return _call(body, "adam_small", (1,), [spec] * 4, [spec] * 3, [_sds((r, LANES))] * 3)(w, g, m, v)


def _pair_add(x, p, others):
    _, r, c = x.shape
    tr = _tile(r, 128 if c > 512 else 256)

    def body(others_ref, x_ref, p_ref, o_ref):
        o_ref[...] = (x_ref[...] + p_ref[...]).astype(BF16)

    return _call(body, "pair_add", (3, r // tr),
                 [pl.BlockSpec((1, tr, c), lambda q, i, others_ref: (others_ref[q], i, 0)),
                  pl.BlockSpec((1, tr, c), lambda q, i, others_ref: (others_ref[3 + q], i, 0))],
                 pl.BlockSpec((1, tr, c), lambda q, i, others_ref: (q, i, 0)), _sds((3, r, c), BF16),
                 prefetch=1)(others, x, p)


def _reduce_adam(x, p, q, place, w, m, v, l, outs):
    _, r, c = x.shape
    tr = _tile(r, 128 if c > 512 else 256)

    def body(place_ref, x_ref, p_ref, q_ref, w_ref, m_ref, v_ref, *rest):
        g_ref, dl_ref, nm_ref, nv_ref = rest[-4:]
        g = (((x_ref[0] + p_ref[0]) + q_ref[0].astype(F32)) + q_ref[1].astype(F32)) + q_ref[2].astype(F32)
        g_ref[0] = g
        dl_ref[0], nm_ref[0], nv_ref[0] = _adamw(w_ref[0], g, m_ref[0], v_ref[0])

    flat = pl.BlockSpec((1, tr, c), lambda i, place_ref: (l, i, 0))
    through = pl.BlockSpec(memory_space=pl.ANY)
    return _call(body, "reduce_adam", (r // tr,),
                 [pl.BlockSpec((1, tr, c), lambda i, place_ref: (place_ref[0], i, 0)),
                  pl.BlockSpec((1, tr, c), lambda i, place_ref: (place_ref[1], i, 0)),
                  pl.BlockSpec((3, tr, c), lambda i, place_ref: (0, i, 0)), flat, flat, flat] + [through] * 4,
                 [flat] * 4, [_sds(w.shape)] * 4, prefetch=1, aliases={7 + k: k for k in range(4)})(
                     place, x, p, q, w, m, v, *outs)


def _place():
    return lax.axis_index("x"), lax.axis_index("y"), lax.axis_index("c")


def _all_gather(xs, name, space):
    n = len(xs)

    def body(*refs):
        x_refs, o_refs = refs[:n], refs[n:2 * n]
        send_sems, recv_sems, local_sems = refs[2 * n:]
        x, y, c = _place()
        me, sibling = (x, y, c), (x, y, 1 - c)
        chips = [(1 - x, y), (x, 1 - y), (1 - x, 1 - y)]

        def blk(a, p):
            return o_refs[a].at[4 * p[0] + 2 * p[1] + p[2]]

        def copy(a, k, block, to, src=None):
            return pltpu.make_async_remote_copy(
                src_ref=blk(a, block) if src is None else src, dst_ref=blk(a, block),
                send_sem=send_sems.at[a, k], recv_sem=recv_sems.at[a, k], device_id=to, device_id_type=MESH)

        mine = [pltpu.make_async_copy(x_refs[a], blk(a, me), local_sems.at[a]) for a in range(n)]
        for cp in mine:
            cp.start()
        first = []
        for a in range(n):
            first.append(copy(a, 0, me, sibling, src=x_refs[a]))
            first += [copy(a, 1 + j, me, (*chip, c), src=x_refs[a]) for j, chip in enumerate(chips)]
        for cp in first:
            cp.start()
        passed = []
        for j, chip in enumerate(chips):
            for a in range(n):
                copy(a, 1 + j, (*chip, c), me).wait_recv()
                cp = copy(a, 4 + j, (*chip, c), sibling)
                cp.start()
                passed.append(cp)
        for a in range(n):
            copy(a, 0, sibling, me).wait_recv()
        for j, chip in enumerate(chips):
            for a in range(n):
                copy(a, 4 + j, (*chip, 1 - c), me).wait_recv()
        for cp in first + passed:
            cp.wait_send()
        for cp in mine:
            cp.wait()

    spec = pl.BlockSpec(memory_space=space)
    return pl.pallas_call(
        body, name=name, out_shape=[_sds((N_DEV,) + a.shape, a.dtype) for a in xs],
        in_specs=[spec] * n, out_specs=[spec] * n,
        scratch_shapes=[pltpu.SemaphoreType.DMA((n, 7)), pltpu.SemaphoreType.DMA((n, 7)),
                        pltpu.SemaphoreType.DMA((n,))])(*xs)


_HBM_SPEC = pl.BlockSpec(memory_space=pltpu.HBM)
_SEM_SPEC = pl.BlockSpec(memory_space=pltpu.SEMAPHORE)
_EFFECT = pltpu.SideEffectType.DATAFLOW_SIDE_EFFECTING


def _descriptors(plan, src_refs, land_refs, send_sems, recv_sems, which=None):
    return [pltpu.make_async_remote_copy(src_ref=s, dst_ref=d, send_sem=send_sems.at[k], recv_sem=recv_sems.at[k],
                                         device_id=dev, device_id_type=MESH)
            for k, (s, d, dev) in enumerate(plan(src_refs, land_refs)) if which is None or k in which]


def _split_start(name, plan, n, srcs, lands, after):
    ns, nb = len(srcs), len(srcs) + len(lands)
    after = list(after) if isinstance(after, (list, tuple)) else [after]
    sems = nb + len(after)

    def body(*refs):
        for cp in _descriptors(plan, refs[:ns], refs[ns:nb], refs[sems], refs[sems + 1]):
            cp.start()
        refs[-1][...] = jnp.zeros_like(refs[-1])

    bufs = [pltpu.with_memory_space_constraint(a, pltpu.HBM) for a in list(srcs) + list(lands)]
    outs = pl.pallas_call(
        body, name=name,
        out_shape=(pltpu.SemaphoreType.DMA((n,)), pltpu.SemaphoreType.DMA((n,)))
        + tuple(pltpu.HBM(a.shape, a.dtype) for a in bufs) + (_sds((8, LANES)),),
        in_specs=[_HBM_SPEC] * nb + [pl.BlockSpec(memory_space=pl.ANY)] * len(after),
        out_specs=(_SEM_SPEC, _SEM_SPEC) + (_HBM_SPEC,) * nb + (pl.BlockSpec(memory_space=pltpu.VMEM),),
        input_output_aliases={i: 2 + i for i in range(nb)},
        compiler_params=pltpu.CompilerParams(has_side_effects=_EFFECT))(*bufs, *after)
    return dict(send=outs[0], recv=outs[1], srcs=list(outs[2:2 + ns]), lands=list(outs[2 + ns:2 + nb]), token=outs[-1])


def _split_wait(name, plan, flight, which, after):
    srcs, lands = flight["srcs"], flight["lands"]
    ns, nb = len(srcs), len(srcs) + len(lands)

    def body(*refs):
        for cp in _descriptors(plan, refs[:ns], refs[ns:nb], refs[nb], refs[nb + 1], set(which)):
            cp.wait_send()
            cp.wait_recv()

    outs = pl.pallas_call(
        body, name=name, out_shape=tuple(pltpu.HBM(a.shape, a.dtype) for a in srcs + lands),
        in_specs=[_HBM_SPEC] * nb + [_SEM_SPEC, _SEM_SPEC, pl.BlockSpec(memory_space=pl.ANY)],
        out_specs=(_HBM_SPEC,) * nb, input_output_aliases={i: i for i in range(nb)},
        compiler_params=pltpu.CompilerParams(has_side_effects=_EFFECT))(*srcs, *lands, flight["send"], flight["recv"],
                                                                       after)
    return dict(flight, srcs=list(outs[:ns]), lands=list(outs[ns:nb]))


GATHER_PEERS = N_DEV - 1


def _gather_plan(items):
    def plan(src_refs, land_refs):
        x, y, c = _place()
        me = 4 * x + 2 * y + c
        out = []
        for a, l in items:
            for r in range(1, N_DEV):
                peer = (1 - x if r & 4 else x, 1 - y if r & 2 else y, 1 - c if r & 1 else c)
                out.append((src_refs[a].at[l], land_refs[a].at[me, l], peer))
        return out

    return plan


def _pair_plan(narr):
    def plan(src_refs, land_refs):
        x, y, c = _place()
        return [(src_refs[a].at[2 * q + (1 - c)], land_refs[a].at[q], (x, y, 1 - c))
                for a in range(narr) for q in range(4)]

    return plan


def _chip_plan(narr):
    def plan(src_refs, land_refs):
        x, y, c = _place()
        chips = [(1 - x, y), (x, 1 - y), (1 - x, 1 - y)]
        return [(src_refs[a].at[r], land_refs[a].at[r], (*chip, c)) for a in range(narr) for r, chip in enumerate(chips)]

    return plan


class _GradReducer:
    def __init__(self, tag, names, w, mom, var, place, others):
        self.tag, self.names, self.w, self.mom, self.var, self.place = tag, names, w, mom, var, place
        self.others = others
        self.outs = {k: [lax.empty(w[k].shape, F32) for _ in range(4)] for k in names}
        self.n = len(names)

    def start(self, l, grads):
        self.l, self.xs = l, [grads[k] for k in self.names]
        lands = [lax.empty((4,) + a.shape[1:], F32) for a in self.xs]
        self.pair = _split_start(f"{self.tag}_pair_start{l}", _pair_plan(self.n), 4 * self.n, self.xs, lands, ())
        return self.pair["token"][0, 0]

    def middle(self, after):
        self.pair = _split_wait(f"{self.tag}_pair_wait{self.l}", _pair_plan(self.n), self.pair, range(4 * self.n),
                                after)
        self.xs, self.ps = self.pair["srcs"], self.pair["lands"]
        ys = [_pair_add(x, p, self.others) for x, p in zip(self.xs, self.ps)]
        lands = [lax.empty((3,) + a.shape[1:], BF16) for a in ys]
        self.chip = _split_start(f"{self.tag}_chip_start{self.l}", _chip_plan(self.n), 3 * self.n, ys, lands, ())
        return self.chip["token"][0, 0]

    def finish(self, after):
        chip = _split_wait(f"{self.tag}_chip_wait{self.l}", _chip_plan(self.n), self.chip, range(3 * self.n), after)
        for k, x, p, q in zip(self.names, self.xs, self.ps, chip["lands"]):
            self.outs[k] = _reduce_adam(x, p, q, self.place, self.w[k], self.mom[k], self.var[k], self.l, self.outs[k])


def _size(shape):
    size = 1
    for d in shape:
        size *= d
    return size


def _slab_rows(shape):
    return -(-_size(shape) // (8 * LANES)) * 8


def _pack(arrs):
    parts = []
    for a in arrs:
        flat = a.reshape(-1).astype(F32)
        parts.append(jnp.pad(flat, (0, _slab_rows(a.shape) * LANES - flat.shape[0])).reshape(-1, LANES))
    return jnp.concatenate(parts, axis=0)


def _unpack(slab, shapes):
    out, off = [], 0
    for shp in shapes:
        rows = _slab_rows(shp)
        out.append(slab[off:off + rows].reshape(-1)[:_size(shp)].reshape(shp))
        off += rows
    return out


def _dense_blocks(w):
    eye = jnp.eye(LRU_BLOCKS, dtype=w.dtype)
    return (eye[:, None, :, None] * w[:, :, None, :]).reshape(LRU_W, LRU_W)


def _diag_blocks(dense):
    on_diagonal = jnp.eye(LRU_BLOCKS, dtype=bool)[:, None, :, None]
    return jnp.sum(jnp.where(on_diagonal, dense.reshape(LRU_BLOCKS, LRU_BLOCK, LRU_BLOCKS, LRU_BLOCK), 0.0), axis=2)


def _alpha_lanes(v):
    return jnp.zeros((1, BA_PAD), F32).at[0, HEADS:2 * HEADS].set(v)


def _local_step(x, target, mod, p, fetch, reducers=None):
    nl = mod.shape[0]
    row = lambda v: v.reshape(1, -1)
    masks = _gdn_masks()
    saved = []
    xc = x
    for l in range(nl):
        win, wba, lin = fetch(l, "in", xc)
        mv = [row(mod[l, k * D_MODEL:(k + 1) * D_MODEL]) for k in range(N_MOD)]
        sh1, sc1, g1, sh2, sc2, g2 = mv
        nw1, nw2 = row(p["norm_mix_w"][l]), row(p["norm_mlp_w"][l])
        wa, wx = _dense_blocks(p["lru_gate_a_w"][l]).astype(BF16), _dense_blocks(p["lru_gate_x_w"][l]).astype(BF16)
        lru_args = (p["lru_conv_w"][l], row(p["lru_conv_b"][l]), wa, wx, row(p["lru_gate_a_b"][l]),
                    row(p["lru_gate_x_b"][l]), row(p["lru_lambda"][l]), row(p["lru_norm_w"][l]))
        gdn_args = (p["gdn_conv_w"][l], _alpha_lanes(p["gdn_a_log"][l]), _alpha_lanes(p["gdn_dt_bias"][l]), masks)
        gnw = row(p["gdn_norm_w"][l])
        proj, ba = _inproj_fwd(xc, nw1, sc1, sh1, win, wba, lin)
        ol, hs = _lru_fwd(proj, *lru_args)
        *prep, tinv = _gdn_prep_fwd(proj, ba, *gdn_args)
        og, st = _gdn_scan_fwd(prep, proj, gnw)
        wo, wup, wdn = fetch(l, "rest", og)
        x1, mix, ff, x2 = _out_mlp_fwd(ol, og, xc, wo, g1, nw2, sc2, sh2, g2, wup, wdn, l)
        saved.append(dict(x=xc, mv=mv, nw1=nw1, nw2=nw2, lru_args=lru_args, gdn_args=gdn_args, gnw=gnw, proj=proj,
                          ba=ba, ol=ol, hs=hs, prep=prep, tinv=tinv, og=og, st=st, x1=x1, mix=mix, ff=ff,
                          win=win, wba=wba, lin=lin))
        xc = x2

    dx, frows = _final_fwd_bwd(xc, target, row(p["final_norm_w"]))
    loss_part = frows[1, 0]
    small = {k: [None] * nl for k in ("norm_mix_w", "norm_mlp_w", "lru_conv_w", "lru_conv_b", "lru_gate_a_w",
                                      "lru_gate_a_b", "lru_gate_x_w", "lru_gate_x_b", "lru_lambda", "lru_norm_w",
                                      "gdn_conv_w", "gdn_a_log", "gdn_dt_bias", "gdn_norm_w")}
    fc = D_FF // N_DEV
    big = [None] * nl
    dmod = [None] * nl
    mlp_red, mix_red = reducers or (None, None)
    busy = False
    for l in reversed(range(nl)):
        sv = saved[l]
        sh1, sc1, g1, sh2, sc2, g2 = sv["mv"]
        gnw = sv["gnw"]
        if busy:
            g2 = g2 + started
        act, dup, h2b, dffb, dx1, rows2 = _mlp_bwd(dx, sv["x1"], sv["ff"], sv["nw2"], sc2, sh2, g2, wup, wdn, l)
        if busy:
            g1 = g1 + mix_red.middle(dx1)
        g_up = _tn_matmul(h2b, dup, "grad_w_up", out=lax.empty((1, N_DEV, D_MODEL, fc), F32), blocked=True)[0]
        g_down = _tn_matmul(act, dffb, "grad_w_down", out=lax.empty((1, D_FF, D_MODEL), F32))
        g_down = g_down.reshape(N_DEV, fc, D_MODEL)
        if mlp_red is not None:
            g1 = g1 + mlp_red.start(l, dict(w_up=g_up, w_down=g_down))
        dmix, dol, dog, rows1 = _outproj_bwd(dx1, sv["mix"], g1, wo, l)
        g_out = _tn_matmul(sv["ol"], dmix, "grad_w_out_lru", out=lax.empty((1, D_MODEL, D_MODEL), F32))
        g_out = _tn_matmul(sv["og"], dmix, "grad_w_out_gdn", out=g_out, row_block=1)
        dpl, dwa, dwx, lrows = _lru_bwd(dol, sv["proj"], sv["hs"], *sv["lru_args"])
        if mlp_red is not None:
            gnw = gnw + mlp_red.middle(dpl)
        *cts, dpz, gnrow = _gdn_scan_bwd(dog, sv["prep"], sv["st"], sv["proj"], gnw)
        dpq, dba, dcw, dpar = _gdn_prep_bwd(cts, sv["tinv"], sv["proj"], sv["ba"], *sv["gdn_args"])
        dx, hb, rows0 = _inproj_bwd(dpl, dpq, dpz, dba, sv["x"], dx1, sv["nw1"], sc1, sh1, sv["win"], sv["wba"],
                                    sv["lin"])
        if busy:
            mix_red.finish(dx)
        if mlp_red is not None:
            mlp_red.finish(dx)
        dproj = jnp.concatenate([dpl, dpq, dpz, dba], axis=1)
        g_in = jnp.transpose(_tn_matmul(hb, dproj, "grad_w_in")[:, :IN_COLS].reshape(
            D_MODEL, N_DEV, IN_COLS // N_DEV), (1, 0, 2))
        big[l] = dict(w_in=g_in, w_out=g_out.reshape(N_DEV, D_MODEL // N_DEV, D_MODEL), w_up=g_up,
                      w_down=g_down)
        if mix_red is not None:
            started, busy = mix_red.start(l, big[l]), True
        dmod[l] = jnp.concatenate([rows0[0], rows0[1], rows1[0], rows2[0], rows2[1], rows2[2]])
        small["norm_mix_w"][l], small["norm_mlp_w"][l] = rows0[2], rows2[3]
        small["lru_conv_w"][l], small["lru_conv_b"][l] = lrows[8:8 + CONV_K], lrows[0]
        small["lru_gate_a_w"][l], small["lru_gate_x_w"][l] = _diag_blocks(dwa), _diag_blocks(dwx)
        small["lru_gate_a_b"][l], small["lru_gate_x_b"][l] = lrows[1], lrows[2]
        small["lru_lambda"][l], small["lru_norm_w"][l] = lrows[3], lrows[4]
        small["gdn_conv_w"][l] = dcw
        small["gdn_a_log"][l], small["gdn_dt_bias"][l] = dpar[0, HEADS:2 * HEADS], dpar[1, HEADS:2 * HEADS]
        small["gdn_norm_w"][l] = gnrow[0]
    small = {k: jnp.stack(v) for k, v in small.items()}
    small["final_norm_w"] = frows[0]
    return loss_part, dx, big, small, jnp.stack(dmod)


SMALL_REPLICATED = ("norm_mix_w", "norm_mlp_w", "b_mod", "lru_conv_b", "lru_gate_a_w", "lru_gate_a_b", "lru_gate_x_w",
                    "lru_gate_x_b", "lru_lambda", "lru_norm_w", "gdn_a_log", "gdn_dt_bias", "gdn_norm_w",
                    "final_norm_w")
SMALL_SHARDED = ("lru_conv_w", "gdn_conv_w")
WEIGHT_ORDER = ("norm_mix_w", "norm_mlp_w", "w_mod", "b_mod", "w_in", "lru_conv_w", "lru_conv_b", "lru_gate_a_w",
                "lru_gate_a_b", "lru_gate_x_w", "lru_gate_x_b", "lru_lambda", "lru_norm_w", "gdn_conv_w", "gdn_a_log",
                "gdn_dt_bias", "gdn_norm_w", "w_out", "w_up", "w_down", "final_norm_w")


def kernel(x, c, norm_mix_w, norm_mlp_w, w_mod, b_mod, w_in, lru_conv_w, lru_conv_b, lru_gate_a_w, lru_gate_a_b, lru_gate_x_w, lru_gate_x_b, lru_lambda, lru_norm_w, gdn_conv_w, gdn_a_log, gdn_dt_bias, gdn_norm_w, w_out, w_up, w_down, final_norm_w, loss_target, m_norm_mix_w, m_norm_mlp_w, m_w_mod, m_b_mod, m_w_in, m_lru_conv_w, m_lru_conv_b, m_lru_gate_a_w, m_lru_gate_a_b, m_lru_gate_x_w, m_lru_gate_x_b, m_lru_lambda, m_lru_norm_w, m_gdn_conv_w, m_gdn_a_log, m_gdn_dt_bias, m_gdn_norm_w, m_w_out, m_w_up, m_w_down, m_final_norm_w, v_norm_mix_w, v_norm_mlp_w, v_w_mod, v_b_mod, v_w_in, v_lru_conv_w, v_lru_conv_b, v_lru_gate_a_w, v_lru_gate_a_b, v_lru_gate_x_w, v_lru_gate_x_b, v_lru_lambda, v_lru_norm_w, v_gdn_conv_w, v_gdn_a_log, v_gdn_dt_bias, v_gdn_norm_w, v_w_out, v_w_up, v_w_down, v_final_norm_w):
    args = dict(locals())
    w = {k: args[k] for k in WEIGHT_ORDER}
    mom = {k: args["m_" + k] for k in WEIGHT_ORDER}
    var = {k: args["v_" + k] for k in WEIGHT_ORDER}
    nl = w_in.shape[0]
    px, py, pc = _place()
    me = 4 * px + 2 * py + pc
    other_chips = [2 * (1 - px) + py, 2 * px + (1 - py), 2 * (1 - px) + (1 - py)]
    others = jnp.stack([2 * q + pc for q in other_chips] + other_chips).astype(jnp.int32)

    shapes0 = [c.shape, lru_conv_w.shape, gdn_conv_w.shape]
    (g0,) = _all_gather([_pack([c, lru_conv_w, gdn_conv_w])], "gather_cond", pltpu.VMEM)
    per_dev = [_unpack(g0[d], shapes0) for d in range(N_DEV)]
    c_all = jnp.concatenate([pd[0] for pd in per_dev], axis=0)
    lru_conv_full = jnp.concatenate([pd[1] for pd in per_dev], axis=-1)
    gdn_conv_full = jnp.concatenate([pd[2] for pd in per_dev], axis=-1)

    cols = w_mod.shape[2]
    bmod_cols = lax.dynamic_slice_in_dim(b_mod, me * cols, cols, axis=1).reshape(nl, 1, cols)
    mod_cols = _mod_local(c_all, w_mod, bmod_cols)
    (g1,) = _all_gather([mod_cols.reshape(nl * N_DEV, cols)], "gather_mod", pltpu.VMEM)
    g1 = g1.reshape(N_DEV, nl, N_DEV, cols)
    mod = jnp.transpose(lax.dynamic_index_in_dim(g1, me, axis=2, keepdims=False), (1, 0, 2)).reshape(nl, N_DEV * cols)

    shards = [a.astype(BF16) for a in (w_in, w_out, w_up, w_down)]
    (first_in,) = _all_gather([shards[0][:1]], "gather_w_in_first", pl.ANY)
    items = [(a, 0) for a in (1, 2, 3)] + [(a, l) for l in range(1, nl) for a in range(4)]
    plan = _gather_plan(items)
    lands = [lax.dynamic_update_slice_in_dim(lax.empty((N_DEV,) + a.shape, BF16), a[None], me, axis=0) for a in shards]
    flight = [_split_start("gather_weights_start", plan, len(items) * GATHER_PEERS, shards, lands, [first_in, mod])]
    mod = mod + flight[0]["token"][0, 0]

    def fetch(l, what, after):
        wanted = [k for k, (a, ll) in enumerate(items) if ll == l and (a == 0) == (what == "in")]
        if wanted:
            flight[0] = _split_wait(f"gather_weights_wait_{what}{l}", plan, flight[0],
                                    [k * GATHER_PEERS + r for k in wanted for r in range(GATHER_PEERS)], after)
        gin, gout, gup, gdn = flight[0]["lands"]
        if what == "rest":
            return gout, gup, gdn
        gin = first_in[:, 0] if l == 0 else gin[:, l]
        win = jnp.transpose(gin, (1, 0, 2)).reshape(1, D_MODEL, IN_COLS)
        wba = jnp.pad(win[:, :, IN_MAIN:], ((0, 0), (0, 0), (0, BA_PAD - (IN_COLS - IN_MAIN))))
        return win, wba, 0

    p = dict(w)
    p["lru_conv_w"], p["gdn_conv_w"] = lru_conv_full, gdn_conv_full

    place = jnp.stack([me, 2 * px + py]).astype(jnp.int32)
    reducers = (_GradReducer("mlp_grad", ("w_up", "w_down"), w, mom, var, place, others),
                _GradReducer("mix_grad", ("w_in", "w_out"), w, mom, var, place, others))
    loss_part, grad_x, _, small, dmod = _local_step(x[0], loss_target[0], mod, p, fetch, reducers)
    loss = lax.psum(loss_part, MESH_AXES)

    small_names = sorted(small)
    slab = _pack([dmod] + [small[k] for k in small_names])
    (gs,) = _all_gather([slab], "gather_small_grads", pltpu.VMEM)
    chips_started = reducers[1].middle(gs)
    dmod_all = gs[:, :_slab_rows(dmod.shape)].reshape(N_DEV, nl, N_MOD * D_MODEL)
    summed = _unpack(_sum_devices(gs) + chips_started, [dmod.shape] + [small[k].shape for k in small_names])
    grads = dict(zip(small_names, summed[1:]))
    grads["b_mod"] = summed[0]
    for k, width in (("lru_conv_w", LRU_W // N_DEV), ("gdn_conv_w", 3 * GDN_W // N_DEV)):
        grads[k] = lax.dynamic_slice_in_dim(grads[k], me * width, width, axis=2)
    names = SMALL_REPLICATED + SMALL_SHARDED
    shapes = [w[k].shape for k in names]
    dl, nm, nv = _adam_flat(_pack([w[k] for k in names]), _pack([grads[k] for k in names]),
                            _pack([mom[k] for k in names]), _pack([var[k] for k in names]))
    delta = dict(zip(names, _unpack(dl, shapes)))
    new_m = dict(zip(names, _unpack(nm, shapes)))
    new_v = dict(zip(names, _unpack(nv, shapes)))

    dmod_cols = jnp.transpose(lax.dynamic_slice_in_dim(dmod_all, me * cols, cols, axis=2), (1, 0, 2))
    grads["w_mod"], delta["w_mod"], new_m["w_mod"], new_v["w_mod"] = _wmod_update(
        c_all, dmod_cols, w_mod, m_w_mod, v_w_mod)

    reducers[1].finish(new_v["w_mod"])
    for red in reducers:
        for k in red.names:
            grads[k], delta[k], new_m[k], new_v[k] = red.outs[k]

    return (loss, grad_x[None], *[grads[k] for k in WEIGHT_ORDER], *[delta[k] for k in WEIGHT_ORDER],
            *[new_m[k] for k in WEIGHT_ORDER], *[new_v[k] for k in WEIGHT_ORDER])
```
